```python
import math
import jax, jax.numpy as jnp
from jax import lax
import numpy as np

D_MODEL = 1024
BATCH = 8
SEQ = 4096
DEPTH = 1

SB_HEADS = 8
SB_HEAD_DIM = 64
SB_WIDTH = SB_HEADS * SB_HEAD_DIM
MLA_HEADS = 8
MLA_NOPE_DIM = 64
MLA_ROPE_DIM = 32
MLA_V_DIM = 64
MLA_Q_RANK = 384
MLA_KV_RANK = 256
MLA_WIDTH = MLA_HEADS * MLA_V_DIM
MLA_QK_DIM = MLA_NOPE_DIM + MLA_ROPE_DIM

Q_BLOCK = 128
ROPE_BASE = 10000.0
EPS = 1e-6

SPLITS = (SB_WIDTH, SB_WIDTH, SB_WIDTH, SB_WIDTH,
          MLA_Q_RANK, MLA_KV_RANK, MLA_ROPE_DIM, MLA_WIDTH,
          D_MODEL, D_MODEL)
IN_WIDTH = int(sum(SPLITS))
SPLIT_POINTS = tuple(int(v) for v in np.cumsum(SPLITS)[:-1])

kernel_name = "hybrid_stickbreaking_mla_adaln_block"


def rmsnorm(x, g):
    xf = x.astype(jnp.float32)
    y = xf * lax.rsqrt(jnp.mean(xf * xf, axis=-1, keepdims=True) + EPS)
    return (y * g.astype(jnp.float32)).astype(x.dtype)


def rope(x, positions):
    r = x.shape[-1]
    inv_freq = ROPE_BASE ** (-jnp.arange(0, r, 2, dtype=jnp.float32) / r)
    ang = positions.astype(jnp.float32)[:, :, None, None] * inv_freq
    cos, sin = jnp.cos(ang), jnp.sin(ang)
    xf = x.astype(jnp.float32)
    x1, x2 = xf[..., : r // 2], xf[..., r // 2:]
    return jnp.concatenate([x1 * cos - x2 * sin, x1 * sin + x2 * cos], axis=-1)


def to_blocks(t):
    b, h, s, d = t.shape
    return t.reshape(b, h, s // Q_BLOCK, Q_BLOCK, d).transpose(2, 0, 1, 3, 4)


def from_blocks(t):
    n, b, h, q, d = t.shape
    return t.transpose(1, 2, 0, 3, 4).reshape(b, h, n * q, d)


def stick_breaking_attention(q, k, v):
    s_len = k.shape[2]
    scale = 1.0 / math.sqrt(q.shape[-1])
    key_idx = jnp.arange(s_len)

    def block(args):
        qb, i = args
        z = jnp.einsum('bhqd,bhkd->bhqk', qb, k) * scale
        q_idx = i * Q_BLOCK + jnp.arange(Q_BLOCK)
        strict = key_idx[None, :] < q_idx[:, None]
        log_one_minus = jnp.where(strict, jax.nn.log_sigmoid(-z), 0.0)
        suffix = lax.cumsum(log_one_minus, axis=3, reverse=True) - log_one_minus
        w = jnp.where(strict, jnp.exp(jax.nn.log_sigmoid(z) + suffix), 0.0)
        return jnp.einsum('bhqk,bhkd->bhqd', w, v)

    n_blk = s_len // Q_BLOCK
    out = lax.map(block, (to_blocks(q), jnp.arange(n_blk)))
    return from_blocks(out)


def latent_attention(q_nope, q_pe, k_nope, k_pe, v):
    s_len = k_nope.shape[2]
    scale = 1.0 / math.sqrt(MLA_QK_DIM)
    key_idx = jnp.arange(s_len)

    def block(args):
        qn, qp, i = args
        sc = (jnp.einsum('bhqd,bhkd->bhqk', qn, k_nope)
              + jnp.einsum('bhqr,bkr->bhqk', qp, k_pe)) * scale
        q_idx = i * Q_BLOCK + jnp.arange(Q_BLOCK)
        causal = key_idx[None, :] <= q_idx[:, None]
        p = jax.nn.softmax(jnp.where(causal, sc, -jnp.inf), axis=-1)
        return jnp.einsum('bhqk,bhkd->bhqd', p, v)

    n_blk = s_len // Q_BLOCK
    out = lax.map(block, (to_blocks(q_nope), to_blocks(q_pe), jnp.arange(n_blk)))
    return from_blocks(out)


def split_heads(t, h):
    b, s, w = t.shape
    return t.reshape(b, s, h, w // h).transpose(0, 2, 1, 3)


def merge_heads(t):
    b, h, s, d = t.shape
    return t.transpose(0, 2, 1, 3).reshape(b, s, h * d)


def _fwd_setup_inputs(seed: int = 0) -> dict:
    key = jax.random.key(seed)
    ks = jax.random.split(key, 16)
    f32 = jnp.float32

    def nrm(k, shape, fan_in, mult=1.0):
        return jax.random.normal(k, shape, f32) * (mult * fan_in ** -0.5)

    x = jax.random.normal(ks[0], (BATCH, SEQ, D_MODEL), f32)
    c = jax.random.normal(ks[1], (BATCH, D_MODEL), f32)
    positions = jnp.broadcast_to(jnp.arange(SEQ, dtype=jnp.int32), (BATCH, SEQ))
    w_ada = nrm(ks[2], (DEPTH, D_MODEL, 3 * D_MODEL), D_MODEL, 0.2)
    b_ada = jax.random.normal(ks[3], (DEPTH, 3 * D_MODEL), f32) * 0.02
    norm_gain = 1.0 + 0.02 * jax.random.normal(ks[4], (DEPTH, D_MODEL), f32)
    w_in = nrm(ks[5], (DEPTH, D_MODEL, IN_WIDTH), D_MODEL)
    q_norm_gain = 1.0 + 0.02 * jax.random.normal(ks[6], (DEPTH, MLA_Q_RANK), f32)
    w_uq = nrm(ks[7], (DEPTH, MLA_Q_RANK, MLA_HEADS * MLA_QK_DIM), MLA_Q_RANK)
    kv_norm_gain = 1.0 + 0.02 * jax.random.normal(ks[8], (DEPTH, MLA_KV_RANK), f32)
    w_ukv = nrm(ks[9], (DEPTH, MLA_KV_RANK, MLA_HEADS * (MLA_NOPE_DIM + MLA_V_DIM)), MLA_KV_RANK)
    w_branch_a = nrm(ks[10], (DEPTH, SB_WIDTH, D_MODEL), SB_WIDTH)
    w_branch_b = nrm(ks[11], (DEPTH, MLA_WIDTH, D_MODEL), MLA_WIDTH)
    w_out = nrm(ks[12], (DEPTH, D_MODEL, D_MODEL), D_MODEL)
    final_norm_gain = 1.0 + 0.02 * jax.random.normal(ks[13], (D_MODEL,), f32)
    return {"x": x, "c": c, "positions": positions, "w_ada": w_ada, "b_ada": b_ada,
            "norm_gain": norm_gain, "w_in": w_in, "q_norm_gain": q_norm_gain, "w_uq": w_uq,
            "kv_norm_gain": kv_norm_gain, "w_ukv": w_ukv, "w_branch_a": w_branch_a,
            "w_branch_b": w_branch_b, "w_out": w_out, "final_norm_gain": final_norm_gain}


def _fwd_reference(x, c, positions, w_ada, b_ada, norm_gain, w_in, q_norm_gain, w_uq,
              kv_norm_gain, w_ukv, w_branch_a, w_branch_b, w_out, final_norm_gain):
    b, s, _ = x.shape
    f32 = jnp.float32
    for l in range(DEPTH):
        mod = c @ w_ada[l] + b_ada[l]
        shift, scale, gate = jnp.split(mod, 3, axis=-1)
        h = rmsnorm(x, norm_gain[l]) * (1.0 + scale[:, None, :]) + shift[:, None, :]

        proj = h @ w_in[l]
        (sb_q, sb_k, sb_v, sb_z, c_q, c_kv, k_rot, mla_z, g_a, g_b) = jnp.split(proj, SPLIT_POINTS, axis=-1)

        o_a = stick_breaking_attention(split_heads(sb_q, SB_HEADS).astype(f32),
                                       split_heads(sb_k, SB_HEADS).astype(f32),
                                       split_heads(sb_v, SB_HEADS).astype(f32))
        o_a = merge_heads(o_a).astype(x.dtype)
        y_a = (o_a * jax.nn.silu(sb_z)) @ w_branch_a[l]

        q = (rmsnorm(c_q, q_norm_gain[l]) @ w_uq[l]).reshape(b, s, MLA_HEADS, MLA_QK_DIM)
        q_nope = q[..., :MLA_NOPE_DIM].astype(f32)
        q_pe = rope(q[..., MLA_NOPE_DIM:], positions)
        kv = (rmsnorm(c_kv, kv_norm_gain[l]) @ w_ukv[l]).reshape(b, s, MLA_HEADS, MLA_NOPE_DIM + MLA_V_DIM)
        k_nope = kv[..., :MLA_NOPE_DIM].astype(f32)
        v_b = kv[..., MLA_NOPE_DIM:].astype(f32)
        k_pe = rope(k_rot[:, :, None, :], positions)[:, :, 0, :]
        o_b = latent_attention(q_nope.transpose(0, 2, 1, 3), q_pe.transpose(0, 2, 1, 3),
                               k_nope.transpose(0, 2, 1, 3), k_pe, v_b.transpose(0, 2, 1, 3))
        o_b = merge_heads(o_b).astype(x.dtype)
        y_b = (o_b * jax.nn.silu(mla_z)) @ w_branch_b[l]

        merged = jax.nn.sigmoid(g_a) * y_a + jax.nn.sigmoid(g_b) * y_b
        x = x + gate[:, None, :] * (merged @ w_out[l])
    return rmsnorm(x, final_norm_gain)


import jax as _jax
import jax.numpy as _jnp

TWIN_FORMAT = 'train_step'
FWD_PARAMS = ['x', 'c', 'positions', 'w_ada', 'b_ada', 'norm_gain', 'w_in', 'q_norm_gain', 'w_uq', 'kv_norm_gain', 'w_ukv', 'w_branch_a', 'w_branch_b', 'w_out', 'final_norm_gain']
TWIN_WEIGHTS = ['w_ada', 'b_ada', 'norm_gain', 'w_in', 'q_norm_gain', 'w_uq', 'kv_norm_gain', 'w_ukv', 'w_branch_a', 'w_branch_b', 'w_out', 'final_norm_gain']
TWIN_DIFF_INPUT = 'x'
TWIN_INPUTS = ['x', 'c', 'positions', 'w_ada', 'b_ada', 'norm_gain', 'w_in', 'q_norm_gain', 'w_uq', 'kv_norm_gain', 'w_ukv', 'w_branch_a', 'w_branch_b', 'w_out', 'final_norm_gain', 'loss_target', 'm_w_ada', 'm_b_ada', 'm_norm_gain', 'm_w_in', 'm_q_norm_gain', 'm_w_uq', 'm_kv_norm_gain', 'm_w_ukv', 'm_w_branch_a', 'm_w_branch_b', 'm_w_out', 'm_final_norm_gain', 'v_w_ada', 'v_b_ada', 'v_norm_gain', 'v_w_in', 'v_q_norm_gain', 'v_w_uq', 'v_kv_norm_gain', 'v_w_ukv', 'v_w_branch_a', 'v_w_branch_b', 'v_w_out', 'v_final_norm_gain']
TWIN_OUTPUTS = ['loss', 'grad_x', 'grad_w_ada', 'grad_b_ada', 'grad_norm_gain', 'grad_w_in', 'grad_q_norm_gain', 'grad_w_uq', 'grad_kv_norm_gain', 'grad_w_ukv', 'grad_w_branch_a', 'grad_w_branch_b', 'grad_w_out', 'grad_final_norm_gain', 'delta_w_ada', 'delta_b_ada', 'delta_norm_gain', 'delta_w_in', 'delta_q_norm_gain', 'delta_w_uq', 'delta_kv_norm_gain', 'delta_w_ukv', 'delta_w_branch_a', 'delta_w_branch_b', 'delta_w_out', 'delta_final_norm_gain', 'new_m_w_ada', 'new_m_b_ada', 'new_m_norm_gain', 'new_m_w_in', 'new_m_q_norm_gain', 'new_m_w_uq', 'new_m_kv_norm_gain', 'new_m_w_ukv', 'new_m_w_branch_a', 'new_m_w_branch_b', 'new_m_w_out', 'new_m_final_norm_gain', 'new_v_w_ada', 'new_v_b_ada', 'new_v_norm_gain', 'new_v_w_in', 'new_v_q_norm_gain', 'new_v_w_uq', 'new_v_kv_norm_gain', 'new_v_w_ukv', 'new_v_w_branch_a', 'new_v_w_branch_b', 'new_v_w_out', 'new_v_final_norm_gain']
TWIN_LEAF_KINDS = {'loss': 'loss', 'grad_x': 'grad_x', 'grad_w_ada': 'grad_w', 'grad_b_ada': 'grad_w', 'grad_norm_gain': 'grad_w', 'grad_w_in': 'grad_w', 'grad_q_norm_gain': 'grad_w', 'grad_w_uq': 'grad_w', 'grad_kv_norm_gain': 'grad_w', 'grad_w_ukv': 'grad_w', 'grad_w_branch_a': 'grad_w', 'grad_w_branch_b': 'grad_w', 'grad_w_out': 'grad_w', 'grad_final_norm_gain': 'grad_w', 'delta_w_ada': 'delta_w', 'delta_b_ada': 'delta_w', 'delta_norm_gain': 'delta_w', 'delta_w_in': 'delta_w', 'delta_q_norm_gain': 'delta_w', 'delta_w_uq': 'delta_w', 'delta_kv_norm_gain': 'delta_w', 'delta_w_ukv': 'delta_w', 'delta_w_branch_a': 'delta_w', 'delta_w_branch_b': 'delta_w', 'delta_w_out': 'delta_w', 'delta_final_norm_gain': 'delta_w', 'new_m_w_ada': 'new_m', 'new_m_b_ada': 'new_m', 'new_m_norm_gain': 'new_m', 'new_m_w_in': 'new_m', 'new_m_q_norm_gain': 'new_m', 'new_m_w_uq': 'new_m', 'new_m_kv_norm_gain': 'new_m', 'new_m_w_ukv': 'new_m', 'new_m_w_branch_a': 'new_m', 'new_m_w_branch_b': 'new_m', 'new_m_w_out': 'new_m', 'new_m_final_norm_gain': 'new_m', 'new_v_w_ada': 'new_v', 'new_v_b_ada': 'new_v', 'new_v_norm_gain': 'new_v', 'new_v_w_in': 'new_v', 'new_v_q_norm_gain': 'new_v', 'new_v_w_uq': 'new_v', 'new_v_kv_norm_gain': 'new_v', 'new_v_w_ukv': 'new_v', 'new_v_w_branch_a': 'new_v', 'new_v_w_branch_b': 'new_v', 'new_v_w_out': 'new_v', 'new_v_final_norm_gain': 'new_v'}


def _forward(args):
    return _fwd_reference(*[args[k] for k in FWD_PARAMS])


def _output_shape():
    def fwd():
        inp = _fwd_setup_inputs(0)
        return _fwd_reference(*[inp[k] for k in FWD_PARAMS])
    out = _jax.eval_shape(fwd)
    return out.shape, out.dtype

N_MICROBATCH = 1
ADAM_LR = 0.001
ADAM_B1 = 0.9
ADAM_B2 = 0.999
ADAM_EPS = 1e-08
ADAM_WD = 0.01
ADAM_STEP = 10
PER_EXAMPLE_BATCH_AXIS = {'x': 0, 'c': 0, 'positions': 0, 'loss_target': 0}
SHARED_INPUTS = []
_WEIGHT_DTYPES = {'w_ada': _jnp.float32, 'b_ada': _jnp.float32, 'norm_gain': _jnp.float32, 'w_in': _jnp.float32, 'q_norm_gain': _jnp.float32, 'w_uq': _jnp.float32, 'kv_norm_gain': _jnp.float32, 'w_ukv': _jnp.float32, 'w_branch_a': _jnp.float32, 'w_branch_b': _jnp.float32, 'w_out': _jnp.float32, 'final_norm_gain': _jnp.float32}
MOMENT_SCALE = {'w_ada': 3.024684e-02, 'b_ada': 3.081229e-02, 'norm_gain': 1.476536e-02, 'w_in': 6.646815e-03, 'q_norm_gain': 3.681540e-03, 'w_uq': 2.481346e-03, 'kv_norm_gain': 7.782797e-03, 'w_ukv': 3.931350e-03, 'w_branch_a': 8.484417e-03, 'w_branch_b': 3.521253e-03, 'w_out': 9.178269e-03, 'final_norm_gain': 3.200744e+01}


def _to_microbatches(a, axis):
    t = _jnp.moveaxis(a, axis, 0)
    t = t.reshape((N_MICROBATCH, t.shape[0] // N_MICROBATCH) + t.shape[1:])
    return _jnp.moveaxis(t, 1, axis + 1)


def setup_inputs(seed: int = 0) -> dict:
    inp = _fwd_setup_inputs(seed)
    key = _jax.random.fold_in(_jax.random.key(seed), 7919)
    shape, _ = _output_shape()
    out = dict(inp)
    out["loss_target"] = _jax.random.normal(_jax.random.fold_in(key, 0), shape, _jnp.float32)
    for i, name in enumerate(TWIN_WEIGHTS):
        w = inp[name].astype(_jnp.float32)
        if MOMENT_SCALE is None:
            s = _jnp.sqrt(_jnp.mean(_jnp.square(w)) + 1e-30)
        else:
            s = MOMENT_SCALE[name]
        km, kv = _jax.random.split(_jax.random.fold_in(key, i + 1))
        out[name] = w
        out["m_" + name] = s * _jax.random.normal(km, w.shape, _jnp.float32)
        out["v_" + name] = (s * s) * _jax.random.uniform(kv, w.shape, _jnp.float32, 0.5, 1.5)
    if N_MICROBATCH > 1:
        for name, axis in PER_EXAMPLE_BATCH_AXIS.items():
            out[name] = _to_microbatches(out[name], axis)
    return {'x': out['x'], 'c': out['c'], 'positions': out['positions'], 'w_ada': out['w_ada'], 'b_ada': out['b_ada'], 'norm_gain': out['norm_gain'], 'w_in': out['w_in'], 'q_norm_gain': out['q_norm_gain'], 'w_uq': out['w_uq'], 'kv_norm_gain': out['kv_norm_gain'], 'w_ukv': out['w_ukv'], 'w_branch_a': out['w_branch_a'], 'w_branch_b': out['w_branch_b'], 'w_out': out['w_out'], 'final_norm_gain': out['final_norm_gain'], 'loss_target': out['loss_target'], 'm_w_ada': out['m_w_ada'], 'm_b_ada': out['m_b_ada'], 'm_norm_gain': out['m_norm_gain'], 'm_w_in': out['m_w_in'], 'm_q_norm_gain': out['m_q_norm_gain'], 'm_w_uq': out['m_w_uq'], 'm_kv_norm_gain': out['m_kv_norm_gain'], 'm_w_ukv': out['m_w_ukv'], 'm_w_branch_a': out['m_w_branch_a'], 'm_w_branch_b': out['m_w_branch_b'], 'm_w_out': out['m_w_out'], 'm_final_norm_gain': out['m_final_norm_gain'], 'v_w_ada': out['v_w_ada'], 'v_b_ada': out['v_b_ada'], 'v_norm_gain': out['v_norm_gain'], 'v_w_in': out['v_w_in'], 'v_q_norm_gain': out['v_q_norm_gain'], 'v_w_uq': out['v_w_uq'], 'v_kv_norm_gain': out['v_kv_norm_gain'], 'v_w_ukv': out['v_w_ukv'], 'v_w_branch_a': out['v_w_branch_a'], 'v_w_branch_b': out['v_w_branch_b'], 'v_w_out': out['v_w_out'], 'v_final_norm_gain': out['v_final_norm_gain']}


def _loss(weights, diff, rest, loss_target):
    with _jax.named_scope("forward"):
        args = {**rest, TWIN_DIFF_INPUT: diff, **{k: w.astype(_WEIGHT_DTYPES[k]) for k, w in weights.items()}}
        y = _forward(args)
    with _jax.named_scope("loss_head"):
        err = _jnp.square(y.astype(_jnp.float32) - loss_target)
        return 0.5 * _jnp.sum(_jnp.mean(err, axis=-1)) if err.ndim else 0.5 * err


def _adamw(w, g, m, v):
    m = ADAM_B1 * m + (1.0 - ADAM_B1) * g
    v = ADAM_B2 * v + (1.0 - ADAM_B2) * _jnp.square(g)
    m_hat = m / (1.0 - ADAM_B1 ** ADAM_STEP)
    v_hat = v / (1.0 - ADAM_B2 ** ADAM_STEP)
    delta = -ADAM_LR * (m_hat / (_jnp.sqrt(v_hat) + ADAM_EPS) + ADAM_WD * w)
    return delta, m, v


def reference(x, c, positions, w_ada, b_ada, norm_gain, w_in, q_norm_gain, w_uq, kv_norm_gain, w_ukv, w_branch_a, w_branch_b, w_out, final_norm_gain, loss_target, m_w_ada, m_b_ada, m_norm_gain, m_w_in, m_q_norm_gain, m_w_uq, m_kv_norm_gain, m_w_ukv, m_w_branch_a, m_w_branch_b, m_w_out, m_final_norm_gain, v_w_ada, v_b_ada, v_norm_gain, v_w_in, v_q_norm_gain, v_w_uq, v_kv_norm_gain, v_w_ukv, v_w_branch_a, v_w_branch_b, v_w_out, v_final_norm_gain):
    given = dict(x=x, c=c, positions=positions, w_ada=w_ada, b_ada=b_ada, norm_gain=norm_gain, w_in=w_in, q_norm_gain=q_norm_gain, w_uq=w_uq, kv_norm_gain=kv_norm_gain, w_ukv=w_ukv, w_branch_a=w_branch_a, w_branch_b=w_branch_b, w_out=w_out, final_norm_gain=final_norm_gain, loss_target=loss_target, m_w_ada=m_w_ada, m_b_ada=m_b_ada, m_norm_gain=m_norm_gain, m_w_in=m_w_in, m_q_norm_gain=m_q_norm_gain, m_w_uq=m_w_uq, m_kv_norm_gain=m_kv_norm_gain, m_w_ukv=m_w_ukv, m_w_branch_a=m_w_branch_a, m_w_branch_b=m_w_branch_b, m_w_out=m_w_out, m_final_norm_gain=m_final_norm_gain, v_w_ada=v_w_ada, v_b_ada=v_b_ada, v_norm_gain=v_norm_gain, v_w_in=v_w_in, v_q_norm_gain=v_q_norm_gain, v_w_uq=v_w_uq, v_kv_norm_gain=v_kv_norm_gain, v_w_ukv=v_w_ukv, v_w_branch_a=v_w_branch_a, v_w_branch_b=v_w_branch_b, v_w_out=v_w_out, v_final_norm_gain=v_final_norm_gain)
    weights = {n: given[n] for n in TWIN_WEIGHTS}
    shared = {n: given[n] for n in SHARED_INPUTS}
    per_example = {n: given[n] for n in ['x', 'c', 'positions']}
    grad_fn = _jax.value_and_grad(_loss, argnums=(0, 1))

    def one_microbatch(ex, loss_target):
        ex = dict(ex)
        diff = ex.pop(TWIN_DIFF_INPUT)
        return grad_fn(weights, diff, {**shared, **ex}, loss_target)

    if N_MICROBATCH == 1:
        loss, (grad_w, grad_x) = one_microbatch(per_example, given["loss_target"])
    else:
        def body(carry, xs):
            loss_sum, grad_sum = carry
            l_k, (gw_k, gx_k) = one_microbatch(xs[0], xs[1])
            with _jax.named_scope("update"):
                return (loss_sum + l_k, _jax.tree.map(_jnp.add, grad_sum, gw_k)), gx_k

        init = (_jnp.zeros((), _jnp.float32), _jax.tree.map(_jnp.zeros_like, weights))
        (loss, grad_w), grad_x = _jax.lax.scan(body, init, (per_example, given["loss_target"]))
    with _jax.named_scope("update"):
        delta_w, new_m, new_v = {}, {}, {}
        for n in TWIN_WEIGHTS:
            delta_w[n], new_m[n], new_v[n] = _adamw(weights[n], grad_w[n], given["m_" + n], given["v_" + n])
    return (loss, grad_x, *[grad_w[n] for n in TWIN_WEIGHTS], *[delta_w[n] for n in TWIN_WEIGHTS],
            *[new_m[n] for n in TWIN_WEIGHTS], *[new_v[n] for n in TWIN_WEIGHTS])
```

```python
import functools
import math

import jax
import jax.numpy as jnp
from jax import lax
from jax.experimental import pallas as pl
from jax.experimental.pallas import tpu as pltpu

F32 = jnp.float32
BF16 = jnp.bfloat16
MXU_DTYPE = jnp.bfloat16

N_DEV = 8
D = 1024
HEADS = 8
HD = 64
SB_W = 512
MLA_W = 512
Q_RANK = 384
KV_RANK = 256
ROPE = 32
NOPE = 64
QK_DIM = NOPE + ROPE
EPS = 1e-6
ROPE_BASE = 10000.0

ADAM_LR = 0.001
ADAM_B1 = 0.9
ADAM_B2 = 0.999
ADAM_EPS = 1e-08
ADAM_WD = 0.01
ADAM_STEP = 10

LANES = 128
VMEM_LIMIT = 48 * 1024 * 1024

O_GA, O_GB = 0, 1024
O_SBQ, O_SBK, O_SBV, O_SBZ = 2048, 2560, 3072, 3584
O_MLAZ = 4096
O_CQ = 4608
O_CKV = 5120
O_KROT = 5376
IN_PAD = 5632

BQ = 128
BK = 128


def _cparams(*sem):
    return pltpu.CompilerParams(dimension_semantics=sem, vmem_limit_bytes=VMEM_LIMIT)


def _tile_of(n, cap=512):
    if n <= cap:
        return n
    for t in (512, 384, 256, 128):
        if t <= cap and n % t == 0:
            return t
    raise ValueError(n)


def _rowwise(fn, rows, vecs, outs, reds=(), *, name, tile=256):
    norm = []
    for r in rows:
        if isinstance(r, tuple):
            arr, cb, w = r[:3]
            ro = r[3] if len(r) > 3 else 0
        else:
            arr, cb, w, ro = r, 0, r.shape[1], 0
        norm.append((arr, cb, w, ro))
    s_len = norm[0][0].shape[0]
    tile = min(tile, s_len)
    assert s_len % tile == 0
    n_row, n_vec, n_out, n_red = len(norm), len(vecs), len(outs), len(reds)

    def body(*refs):
        step = pl.program_id(0)
        row_refs = refs[:n_row]
        vec_refs = refs[n_row:n_row + n_vec]
        out_refs = refs[n_row + n_vec:n_row + n_vec + n_out]
        red_refs = refs[n_row + n_vec + n_out:]
        row_res, red_res = fn(*[r[...] for r in row_refs], *[v[...] for v in vec_refs])
        for o, val in zip(out_refs, row_res):
            o[...] = val.astype(o.dtype)
        if n_red:
            @pl.when(step == 0)
            def _():
                for r in red_refs:
                    r[...] = jnp.zeros(r.shape, r.dtype)
            for r, val in zip(red_refs, red_res):
                r[...] += val

    in_specs = []
    for arr, cb, w, ro in norm:
        in_specs.append(pl.BlockSpec((tile, w), functools.partial(lambda i, cb, rb: (i + rb, cb), cb=cb, rb=ro // tile)))
        assert ro % tile == 0
    for v in vecs:
        in_specs.append(pl.BlockSpec(v.shape, lambda i: (0, 0)))
    out_shape = [jax.ShapeDtypeStruct((s_len, w), dt) for w, dt in outs]
    out_specs = [pl.BlockSpec((tile, w), lambda i: (i, 0)) for w, _ in outs]
    out_shape += [jax.ShapeDtypeStruct((1, w), F32) for w in reds]
    out_specs += [pl.BlockSpec((1, w), lambda i: (0, 0)) for w in reds]
    res = pl.pallas_call(
        body, name=name, grid=(s_len // tile,), in_specs=in_specs, out_specs=out_specs, out_shape=out_shape,
        compiler_params=_cparams("arbitrary" if n_red else "parallel"),
    )(*[a for a, _, _, _ in norm], *vecs)
    return res


def _mm(a, b, *, ta=False, tb=False, out_dtype=F32, name):
    m, k = (a.shape[1], a.shape[0]) if ta else a.shape
    n = b.shape[0] if tb else b.shape[1]
    assert (b.shape[1] if tb else b.shape[0]) == k
    tm, tn, tk = _tile_of(m), _tile_of(n), _tile_of(k)
    nk = k // tk
    dims = (((0 if ta else 1,), (1 if tb else 0,)), ((), ()))

    def body(a_ref, b_ref, o_ref, acc_ref):
        kk = pl.program_id(2)

        @pl.when(kk == 0)
        def _():
            acc_ref[...] = jnp.zeros(acc_ref.shape, F32)

        acc_ref[...] += lax.dot_general(a_ref[...].astype(MXU_DTYPE), b_ref[...].astype(MXU_DTYPE), dims,
                                        preferred_element_type=F32)

        @pl.when(kk == nk - 1)
        def _():
            o_ref[...] = acc_ref[...].astype(o_ref.dtype)

    a_spec = pl.BlockSpec((tk, tm), lambda i, j, kk: (kk, i)) if ta else pl.BlockSpec((tm, tk), lambda i, j, kk: (i, kk))
    b_spec = pl.BlockSpec((tn, tk), lambda i, j, kk: (j, kk)) if tb else pl.BlockSpec((tk, tn), lambda i, j, kk: (kk, j))
    return pl.pallas_call(
        body, name=name, grid=(m // tm, n // tn, nk), in_specs=[a_spec, b_spec],
        out_specs=pl.BlockSpec((tm, tn), lambda i, j, kk: (i, j)),
        out_shape=jax.ShapeDtypeStruct((m, n), out_dtype),
        scratch_shapes=[pltpu.VMEM((tm, tn), F32)],
        compiler_params=_cparams("parallel", "parallel", "arbitrary"),
    )(a, b)


_NT = (((1,), (1,)), ((), ()))
_TN = (((0,), (0,)), ((), ()))


def _dot(a, b):
    return jnp.dot(a, b, preferred_element_type=F32)


def _dot_nt(a, b):
    return lax.dot_general(a, b, _NT, preferred_element_type=F32)


def _dot_tn(a, b):
    return lax.dot_general(a, b, _TN, preferred_element_type=F32)


def _split_dot(x, tri):
    hi = x.astype(MXU_DTYPE)
    lo = (x - hi.astype(F32)).astype(MXU_DTYPE)
    return _dot(hi, tri) + _dot(lo, tri)


def _neg_softplus(z):
    return -(jnp.maximum(z, 0.0) + jnp.log1p(jnp.exp(-jnp.abs(z))))


def _attn_consts():
    row = lax.broadcasted_iota(jnp.int32, (BQ, BK), 0)
    col = lax.broadcasted_iota(jnp.int32, (BQ, BK), 1)
    lane = lax.broadcasted_iota(jnp.int32, (BQ, LANES), 1)
    return row, col, lane


def _sb_forward(proj, s_len):
    nq = s_len // BQ
    assert nq <= HD
    qc, kc, vc = O_SBQ // LANES, O_SBK // LANES, O_SBV // LANES

    def body(q_ref, k_ref, v_ref, o_ref, tails_ref):
        i = pl.program_id(1)
        row, col, lane = _attn_consts()
        strict = col < row
        tri = (row >= col).astype(MXU_DTYPE)
        q = q_ref[...] * 0.125
        res, tails = [], []
        for h in range(2):
            hm = (lane < HD) if h == 0 else (lane >= HD)
            qh = jnp.where(hm, q, 0.0).astype(MXU_DTYPE)

            def step(j, st, diag):
                carry, acc, tail = st
                sl = pl.ds(pl.multiple_of(j * BK, BK), BK)
                kj = k_ref[sl, :].astype(MXU_DTYPE)
                vj = v_ref[sl, :].astype(MXU_DTYPE)
                z = _dot_nt(qh, kj)
                lom = _neg_softplus(z)
                if diag:
                    lom = jnp.where(strict, lom, 0.0)
                a = jnp.exp(z + _split_dot(lom, tri) + carry)
                if diag:
                    a = jnp.where(strict, a, 0.0)
                acc = acc + _dot(a.astype(MXU_DTYPE), vj)
                tail = jnp.where(lane == h * HD + j, carry, tail)
                return carry + jnp.sum(lom, axis=1, keepdims=True), acc, tail

            zero = jnp.zeros((BQ, LANES), F32)
            st = step(i, (jnp.zeros((BQ, 1), F32), zero, zero), True)
            st = lax.fori_loop(0, i, lambda jj, s: step(i - 1 - jj, s, False), st)
            res.append(st[1])
            tails.append(st[2])
        o_ref[...] = jnp.where(lane < HD, res[0], res[1])
        tails_ref[...] = jnp.where(lane < HD, tails[0], tails[1])

    blk = pl.BlockSpec((BQ, LANES), lambda p, i: (i, p))
    out = jax.ShapeDtypeStruct((s_len, SB_W), F32)
    return pl.pallas_call(
        body, name="sb_fwd", grid=(HEADS // 2, nq),
        in_specs=[pl.BlockSpec((BQ, LANES), lambda p, i: (i, qc + p)),
                  pl.BlockSpec((s_len, LANES), lambda p, i: (0, kc + p)),
                  pl.BlockSpec((s_len, LANES), lambda p, i: (0, vc + p))],
        out_specs=[blk, blk], out_shape=[out, out],
        compiler_params=_cparams("parallel", "parallel"),
    )(proj, proj, proj)


def _sb_backward(proj, tails, do, s_len):
    nq = s_len // BQ
    qc, kc, vc = O_SBQ // LANES, O_SBK // LANES, O_SBV // LANES

    def body(q_ref, k_ref, v_ref, tails_ref, do_ref, dq_ref, dk_ref, dv_ref):
        i = pl.program_id(1)

        @pl.when(i == 0)
        def _():
            dk_ref[...] = jnp.zeros(dk_ref.shape, F32)
            dv_ref[...] = jnp.zeros(dv_ref.shape, F32)

        row, col, lane = _attn_consts()
        strict = col < row
        tri = (row >= col).astype(MXU_DTYPE)
        tri_p = (row <= col).astype(MXU_DTYPE)
        q = q_ref[...] * 0.125
        tails_blk = tails_ref[...]
        do_blk = do_ref[...]
        res = []
        for h in range(2):
            hm = (lane < HD) if h == 0 else (lane >= HD)
            qh = jnp.where(hm, q, 0.0).astype(MXU_DTYPE)
            doh_b = jnp.where(hm, do_blk, 0.0).astype(MXU_DTYPE)

            def step(j, st, diag):
                before, dq = st
                sl = pl.ds(pl.multiple_of(j * BK, BK), BK)
                kj = k_ref[sl, :].astype(MXU_DTYPE)
                vj = v_ref[sl, :].astype(MXU_DTYPE)
                tail = jnp.sum(jnp.where(lane == h * HD + j, tails_blk, 0.0), axis=1, keepdims=True)
                z = _dot_nt(qh, kj)
                lom = _neg_softplus(z)
                if diag:
                    lom = jnp.where(strict, lom, 0.0)
                a = jnp.exp(z + _split_dot(lom, tri) + tail)
                if diag:
                    a = jnp.where(strict, a, 0.0)
                dl = _dot_nt(doh_b, vj) * a
                upto = before + _split_dot(dl, tri_p)
                dz = dl - jnp.exp(z + lom) * upto
                if diag:
                    dz = jnp.where(strict, dz, 0.0)
                dzb = dz.astype(MXU_DTYPE)
                dk_ref[sl, :] += _dot_tn(dzb, qh)
                dv_ref[sl, :] += _dot_tn(a.astype(MXU_DTYPE), doh_b)
                return before + jnp.sum(dl, axis=1, keepdims=True), dq + _dot(dzb, kj)

            st = (jnp.zeros((BQ, 1), F32), jnp.zeros((BQ, LANES), F32))
            st = lax.fori_loop(0, i, lambda j, s: step(j, s, False), st)
            st = step(i, st, True)
            res.append(st[1])
        dq_ref[...] = jnp.where(lane < HD, res[0], res[1]) * 0.125

    blk = pl.BlockSpec((BQ, LANES), lambda p, i: (i, p))
    full = pl.BlockSpec((s_len, LANES), lambda p, i: (0, p))
    out = jax.ShapeDtypeStruct((s_len, SB_W), F32)
    return pl.pallas_call(
        body, name="sb_bwd", grid=(HEADS // 2, nq),
        in_specs=[pl.BlockSpec((BQ, LANES), lambda p, i: (i, qc + p)),
                  pl.BlockSpec((s_len, LANES), lambda p, i: (0, kc + p)),
                  pl.BlockSpec((s_len, LANES), lambda p, i: (0, vc + p)),
                  blk, blk],
        out_specs=[blk, full, full], out_shape=[out, out, out],
        compiler_params=_cparams("arbitrary", "arbitrary"),
    )(proj, proj, proj, tails, do)


def _pe_mask(p, h):
    lane = lax.broadcasted_iota(jnp.int32, (BQ, 2 * LANES), 1)
    return ((lane % LANES) // (ROPE // 2)) == (2 * p + h)


def _mla_forward(q_mla, q_pe, kv, k_pe, s_len):
    nq = s_len // BQ
    scale = 1.0 / math.sqrt(QK_DIM)

    def body(qn_ref, qpe_ref, kn_ref, kpe_ref, v_ref, o_ref, lse_ref):
        p = pl.program_id(0)
        i = pl.program_id(1)
        row, col, lane = _attn_consts()
        causal = col <= row
        qn = qn_ref[...]
        qpe = qpe_ref[...]
        res, stats = [], []
        for h in range(2):
            hm = (lane < HD) if h == 0 else (lane >= HD)
            qnh = jnp.where(hm, qn, 0.0).astype(MXU_DTYPE)
            qph = jnp.where(_pe_mask(p, h), qpe, 0.0).astype(MXU_DTYPE)

            def step(j, st, diag):
                m, l, acc = st
                sl = pl.ds(pl.multiple_of(j * BK, BK), BK)
                s = (_dot_nt(qnh, kn_ref[sl, :].astype(MXU_DTYPE))
                     + _dot_nt(qph, kpe_ref[sl, :].astype(MXU_DTYPE))) * scale
                if diag:
                    s = jnp.where(causal, s, -jnp.inf)
                m_new = jnp.maximum(m, jnp.max(s, axis=1, keepdims=True))
                alpha = jnp.exp(m - m_new)
                pr = jnp.exp(s - m_new)
                l = alpha * l + jnp.sum(pr, axis=1, keepdims=True)
                acc = alpha * acc + _dot(pr.astype(MXU_DTYPE), v_ref[sl, :].astype(MXU_DTYPE))
                return m_new, l, acc

            st = step(i, (jnp.full((BQ, 1), -1e30, F32), jnp.zeros((BQ, 1), F32), jnp.zeros((BQ, LANES), F32)), True)
            m, l, acc = lax.fori_loop(0, i, lambda jj, s: step(i - 1 - jj, s, False), st)
            res.append(acc / l)
            stats.append(m + jnp.log(l))
        o_ref[...] = jnp.where(lane < HD, res[0], res[1])
        lse_ref[...] = jnp.where(lane < HD, stats[0], stats[1])

    blk = pl.BlockSpec((BQ, LANES), lambda p, i: (i, p))
    out = jax.ShapeDtypeStruct((s_len, MLA_W), F32)
    return pl.pallas_call(
        body, name="mla_fwd", grid=(HEADS // 2, nq),
        in_specs=[blk,
                  pl.BlockSpec((BQ, 2 * LANES), lambda p, i: (i, 0)),
                  pl.BlockSpec((s_len, LANES), lambda p, i: (0, p)),
                  pl.BlockSpec((s_len, 2 * LANES), lambda p, i: (0, 0)),
                  pl.BlockSpec((s_len, LANES), lambda p, i: (0, MLA_W // LANES + p))],
        out_specs=[blk, blk], out_shape=[out, out],
        compiler_params=_cparams("parallel", "parallel"),
    )(q_mla, q_pe, kv, k_pe, kv)


def _mla_backward(q_mla, q_pe, kv, k_pe, o, lse, do, s_len):
    nq = s_len // BQ
    scale = 1.0 / math.sqrt(QK_DIM)

    def body(qn_ref, qpe_ref, kn_ref, kpe_ref, v_ref, o_ref, lse_ref, do_ref,
             dqn_ref, dqpe_ref, dkn_ref, dkpe_ref, dv_ref):
        p = pl.program_id(0)
        i = pl.program_id(1)

        @pl.when(i == 0)
        def _():
            dkn_ref[...] = jnp.zeros(dkn_ref.shape, F32)
            dkpe_ref[...] = jnp.zeros(dkpe_ref.shape, F32)
            dv_ref[...] = jnp.zeros(dv_ref.shape, F32)

        row, col, lane = _attn_consts()
        causal = col <= row
        qn = qn_ref[...]
        qpe = qpe_ref[...]
        o_blk = o_ref[...]
        do_blk = do_ref[...]
        lse_blk = lse_ref[...]
        res_n, res_pe = [], []
        for h in range(2):
            hm = (lane < HD) if h == 0 else (lane >= HD)
            pem = _pe_mask(p, h)
            qnh = jnp.where(hm, qn, 0.0).astype(MXU_DTYPE)
            qph = jnp.where(pem, qpe, 0.0).astype(MXU_DTYPE)
            doh = jnp.where(hm, do_blk, 0.0)
            doh_b = doh.astype(MXU_DTYPE)
            delta = jnp.sum(doh * o_blk, axis=1, keepdims=True)
            lse_h = jnp.sum(jnp.where(lane == h * HD, lse_blk, 0.0), axis=1, keepdims=True)

            def step(j, st, diag):
                dqn, dqp = st
                sl = pl.ds(pl.multiple_of(j * BK, BK), BK)
                knj = kn_ref[sl, :].astype(MXU_DTYPE)
                kpj = kpe_ref[sl, :].astype(MXU_DTYPE)
                vj = v_ref[sl, :].astype(MXU_DTYPE)
                s = (_dot_nt(qnh, knj) + _dot_nt(qph, kpj)) * scale
                pr = jnp.exp(s - lse_h)
                if diag:
                    pr = jnp.where(causal, pr, 0.0)
                ds = pr * (_dot_nt(doh_b, vj) - delta) * scale
                dsb = ds.astype(MXU_DTYPE)
                dkn_ref[sl, :] += _dot_tn(dsb, qnh)
                dkpe_ref[sl, :] += _dot_tn(dsb, qph)
                dv_ref[sl, :] += _dot_tn(pr.astype(MXU_DTYPE), doh_b)
                return dqn + _dot(dsb, knj), dqp + _dot(dsb, kpj)

            st = step(i, (jnp.zeros((BQ, LANES), F32), jnp.zeros((BQ, 2 * LANES), F32)), True)
            st = lax.fori_loop(0, i, lambda jj, s: step(i - 1 - jj, s, False), st)
            res_n.append(st[0])
            res_pe.append(jnp.where(pem, st[1], 0.0))
        dqn_ref[...] = jnp.where(lane < HD, res_n[0], res_n[1])
        dqpe_ref[...] = res_pe[0] + res_pe[1]

    blk = pl.BlockSpec((BQ, LANES), lambda p, i: (i, p))
    full = pl.BlockSpec((s_len, LANES), lambda p, i: (0, p))
    out = jax.ShapeDtypeStruct((s_len, MLA_W), F32)
    out_pe = jax.ShapeDtypeStruct((4 * s_len, 2 * LANES), F32)
    return pl.pallas_call(
        body, name="mla_bwd", grid=(HEADS // 2, nq),
        in_specs=[blk,
                  pl.BlockSpec((BQ, 2 * LANES), lambda p, i: (i, 0)),
                  pl.BlockSpec((s_len, LANES), lambda p, i: (0, p)),
                  pl.BlockSpec((s_len, 2 * LANES), lambda p, i: (0, 0)),
                  pl.BlockSpec((s_len, LANES), lambda p, i: (0, MLA_W // LANES + p)),
                  blk, blk, blk],
        out_specs=[blk, pl.BlockSpec((BQ, 2 * LANES), lambda p, i: (p * nq + i, 0)), full,
                   pl.BlockSpec((s_len, 2 * LANES), lambda p, i: (p, 0)), full],
        out_shape=[out, out_pe, out, out_pe, out],
        compiler_params=_cparams("arbitrary", "arbitrary"),
    )(q_mla, q_pe, kv, k_pe, kv, o, lse, do)


def _mesh_pos():
    return lax.axis_index("x"), lax.axis_index("y"), lax.axis_index("c")


def _dev_index(px, py, pc):
    return 4 * px + 2 * py + pc


def _all_gather(block, name):
    def body(x_ref, out_ref, send_sems, recv_sems, local_sem):
        x, y, c = _mesh_pos()
        me, sibling = (x, y, c), (x, y, 1 - c)
        chips = [(1 - x, y), (x, 1 - y), (1 - x, 1 - y)]

        def slot(pos):
            return out_ref.at[_dev_index(*pos)]

        def copy(k, blockpos, to, src=None):
            return pltpu.make_async_remote_copy(
                src_ref=slot(blockpos) if src is None else src, dst_ref=slot(blockpos),
                send_sem=send_sems.at[k], recv_sem=recv_sems.at[k],
                device_id=to, device_id_type=pl.DeviceIdType.MESH)

        mine = pltpu.make_async_copy(x_ref, slot(me), local_sem)
        mine.start()
        first = [copy(0, me, sibling, src=x_ref)]
        first += [copy(1 + j, me, (*chip, c), src=x_ref) for j, chip in enumerate(chips)]
        for cp in first:
            cp.start()
        passed = [copy(4 + j, (*chip, c), sibling) for j, chip in enumerate(chips)]
        for j, chip in enumerate(chips):
            copy(1 + j, (*chip, c), me).wait_recv()
            passed[j].start()
        copy(0, sibling, me).wait_recv()
        for j, chip in enumerate(chips):
            copy(4 + j, (*chip, 1 - c), me).wait_recv()
        for cp in first + passed:
            cp.wait_send()
        mine.wait()

    return pl.pallas_call(
        body, name=name,
        out_shape=jax.ShapeDtypeStruct((N_DEV,) + block.shape, block.dtype),
        in_specs=[pl.BlockSpec(memory_space=pl.ANY)], out_specs=pl.BlockSpec(memory_space=pl.ANY),
        scratch_shapes=[pltpu.SemaphoreType.DMA((7,)), pltpu.SemaphoreType.DMA((7,)), pltpu.SemaphoreType.DMA],
    )(block)


def _exchange(blocks, name):
    def body(x_ref, out_ref, send_sems, recv_sems, local_sem):
        x, y, c = _mesh_pos()
        me = _dev_index(x, y, c)
        flips = [(fx, fy, fc) for fx in (0, 1) for fy in (0, 1) for fc in (0, 1)][1:]
        peers = [(1 - x if fx else x, 1 - y if fy else y, 1 - c if fc else c) for fx, fy, fc in flips]
        mine = pltpu.make_async_copy(x_ref.at[me], out_ref.at[me], local_sem)
        mine.start()
        sends = []
        for k, peer in enumerate(peers):
            cp = pltpu.make_async_remote_copy(
                src_ref=x_ref.at[_dev_index(*peer)], dst_ref=out_ref.at[me],
                send_sem=send_sems.at[k], recv_sem=recv_sems.at[k],
                device_id=peer, device_id_type=pl.DeviceIdType.MESH)
            cp.start()
            sends.append(cp)
        for k, peer in enumerate(peers):
            pltpu.make_async_remote_copy(
                src_ref=x_ref.at[me], dst_ref=out_ref.at[_dev_index(*peer)],
                send_sem=send_sems.at[k], recv_sem=recv_sems.at[k],
                device_id=peer, device_id_type=pl.DeviceIdType.MESH).wait_recv()
        for cp in sends:
            cp.wait_send()
        mine.wait()

    return pl.pallas_call(
        body, name=name,
        out_shape=jax.ShapeDtypeStruct(blocks.shape, blocks.dtype),
        in_specs=[pl.BlockSpec(memory_space=pl.ANY)], out_specs=pl.BlockSpec(memory_space=pl.ANY),
        scratch_shapes=[pltpu.SemaphoreType.DMA((7,)), pltpu.SemaphoreType.DMA((7,)), pltpu.SemaphoreType.DMA],
    )(blocks)


def _sum_blocks(parts, name):
    n, r, c = parts.shape
    tile = max(t for t in range(16, min(r, 2048) + 1, 16) if r % t == 0)

    def body(p_ref, o_ref):
        acc = p_ref[0].astype(F32)
        for s in range(1, n):
            acc = acc + p_ref[s].astype(F32)
        o_ref[...] = acc

    return pl.pallas_call(
        body, name=name, grid=(r // tile,),
        in_specs=[pl.BlockSpec((n, tile, c), lambda i: (0, i, 0))],
        out_specs=pl.BlockSpec((tile, c), lambda i: (i, 0)),
        out_shape=jax.ShapeDtypeStruct((r, c), F32),
        compiler_params=_cparams("parallel"),
    )(parts)


def _sigmoid(x):
    return 1.0 / (1.0 + jnp.exp(-x))


def _silu(x):
    return x * _sigmoid(x)


def _silu_grad(x):
    s = _sigmoid(x)
    return s * (1.0 + x * (1.0 - s))


def _colsum(x):
    return jnp.sum(x, axis=0, keepdims=True)


def _rms(x):
    return lax.rsqrt(jnp.mean(x * x, axis=-1, keepdims=True) + EPS)


def _rms_bwd(xn, r, dxn):
    return r * (dxn - xn * jnp.mean(dxn * xn, axis=-1, keepdims=True))


def _adamw(w, g, m, v):
    m = ADAM_B1 * m + (1.0 - ADAM_B1) * g
    v = ADAM_B2 * v + (1.0 - ADAM_B2) * jnp.square(g)
    m_hat = m / (1.0 - ADAM_B1 ** ADAM_STEP)
    v_hat = v / (1.0 - ADAM_B2 ** ADAM_STEP)
    delta = -ADAM_LR * (m_hat / (jnp.sqrt(v_hat) + ADAM_EPS) + ADAM_WD * w)
    return delta, m, v


def _adamw_call(w, g, m, v, name):
    r, c = w.shape
    tile = r if (r <= 256 or r % 256) else 256
    return _rowwise(lambda w_, g_, m_, v_: (_adamw(w_, g_, m_, v_), ()), [w, g, m, v], [],
                    [(c, F32)] * 3, name=name, tile=tile)


def _uq_to_kernel_layout(w):
    lead = w.shape[:-1]
    t = w.reshape(lead + (HEADS, QK_DIM))
    return jnp.concatenate([t[..., :NOPE].reshape(lead + (HEADS * NOPE,)),
                            t[..., NOPE:NOPE + ROPE // 2].reshape(lead + (LANES,)),
                            t[..., NOPE + ROPE // 2:].reshape(lead + (LANES,))], axis=-1)


def _uq_from_kernel_layout(w):
    lead = w.shape[:-1]
    nope = w[..., :HEADS * NOPE].reshape(lead + (HEADS, NOPE))
    r1 = w[..., HEADS * NOPE:HEADS * NOPE + LANES].reshape(lead + (HEADS, ROPE // 2))
    r2 = w[..., HEADS * NOPE + LANES:].reshape(lead + (HEADS, ROPE // 2))
    return jnp.concatenate([nope, r1, r2], axis=-1).reshape(lead + (HEADS * QK_DIM,))


def _ukv_to_kernel_layout(w):
    lead = w.shape[:-1]
    t = w.reshape(lead + (HEADS, NOPE + HD))
    return jnp.concatenate([t[..., :NOPE].reshape(lead + (HEADS * NOPE,)),
                            t[..., NOPE:].reshape(lead + (HEADS * HD,))], axis=-1)


def _ukv_from_kernel_layout(w):
    lead = w.shape[:-1]
    kn = w[..., :HEADS * NOPE].reshape(lead + (HEADS, NOPE))
    vv = w[..., HEADS * NOPE:].reshape(lead + (HEADS, HD))
    return jnp.concatenate([kn, vv], axis=-1).reshape(lead + (HEADS * (NOPE + HD),))


def _w_in_to_kernel_layout(w):
    sb_q, sb_k, sb_v, sb_z = (w[:, 512 * t:512 * (t + 1)] for t in range(4))
    c_q = w[:, 2048:2432]
    c_kv = w[:, 2432:2688]
    k_rot = w[:, 2688:2720]
    mla_z = w[:, 2720:3232]
    g_a = w[:, 3232:4256]
    g_b = w[:, 4256:5280]
    zeros = jnp.zeros((w.shape[0], LANES), w.dtype)
    k1 = jnp.tile(k_rot[:, :ROPE // 2], (1, HEADS))
    k2 = jnp.tile(k_rot[:, ROPE // 2:], (1, HEADS))
    return jnp.concatenate([g_a, g_b, sb_q, sb_k, sb_v, sb_z, mla_z, c_q, zeros, c_kv, k1, k2], axis=1)


def kernel(x, c, positions, w_ada, b_ada, norm_gain, w_in, q_norm_gain, w_uq, kv_norm_gain, w_ukv, w_branch_a, w_branch_b, w_out, final_norm_gain, loss_target, m_w_ada, m_b_ada, m_norm_gain, m_w_in, m_q_norm_gain, m_w_uq, m_kv_norm_gain, m_w_ukv, m_w_branch_a, m_w_branch_b, m_w_out, m_final_norm_gain, v_w_ada, v_b_ada, v_norm_gain, v_w_in, v_q_norm_gain, v_w_uq, v_kv_norm_gain, v_w_ukv, v_w_branch_a, v_w_branch_b, v_w_out, v_final_norm_gain):
    s_len = x.shape[1]
    me = _dev_index(*_mesh_pos())
    x2d = x[0]
    tgt = loss_target[0]

    big = [w_in[0], w_uq[0], w_ukv[0], w_branch_a[0], w_branch_b[0], w_out[0]]
    big_sizes = [int(w.size) for w in big]
    packed = jnp.concatenate([w.astype(BF16).reshape(-1, LANES) for w in big], axis=0)
    gathered = _all_gather(packed, "gather_weights")
    offs = [0]
    for n in big_sizes:
        offs.append(offs[-1] + n // LANES)

    def unpack(t, shape):
        return gathered[:, offs[t]:offs[t + 1], :].reshape((N_DEV,) + shape)

    def cols(t, shape):
        return unpack(t, shape).transpose(1, 0, 2).reshape(shape[0], N_DEV * shape[1])

    w_in_k = _w_in_to_kernel_layout(cols(0, big[0].shape))
    w_uq_k = _uq_to_kernel_layout(cols(1, big[1].shape))
    w_ukv_k = _ukv_to_kernel_layout(cols(2, big[2].shape))
    w_a_f = cols(3, big[3].shape)
    w_b_f = cols(4, big[4].shape)
    w_out_f = unpack(5, big[5].shape).reshape(D, D)

    c_all = _all_gather(c.reshape(8, LANES), "gather_c").reshape(N_DEV, D)
    mod_cols = _mm(c_all, w_ada[0], name="ada_mod")
    mod_all = _all_gather(mod_cols, "gather_mod")
    mod = lax.dynamic_index_in_dim(mod_all, me, axis=1, keepdims=False).reshape(1, 3 * D)
    mod_shift, mod_scale, mod_gate = mod[:, :D], mod[:, D:2 * D], mod[:, 2 * D:]
    b_shift, b_scale, b_gate = b_ada[:, :D], b_ada[:, D:2 * D], b_ada[:, 2 * D:]
    g1 = norm_gain
    gq, gkv = q_norm_gain, kv_norm_gain
    gf = final_norm_gain.reshape(1, D)

    def f_h(x_, g1_, ms, bs, msc, bsc):
        xn = x_ * _rms(x_)
        return (xn * g1_ * (1.0 + (msc + bsc)) + (ms + bs),), ()

    (h,) = _rowwise(f_h, [x2d], [g1, mod_shift, b_shift, mod_scale, b_scale], [(D, BF16)], name="ada_norm")
    proj = _mm(h, w_in_k, name="proj_in")

    o_a, sb_tails = _sb_forward(proj, s_len)

    def f_lat(cq, ckv, gq_, gkv_):
        return (cq * _rms(cq) * gq_, ckv * _rms(ckv) * gkv_), ()

    cq_n, ckv_n = _rowwise(f_lat, [(proj, O_CQ // Q_RANK, Q_RANK), (proj, O_CKV // KV_RANK, KV_RANK)], [gq, gkv],
                           [(Q_RANK, BF16), (KV_RANK, BF16)], name="latent_norm")
    q_mla = _mm(cq_n, w_uq_k, name="q_up")
    kv = _mm(ckv_n, w_ukv_k, name="kv_up")

    inv_freq = ROPE_BASE ** (-jnp.arange(0, ROPE, 2, dtype=F32) / ROPE)
    inv_freq_t = jnp.tile(inv_freq, HEADS).reshape(1, LANES)
    pos_col = positions.reshape(s_len, 1).astype(F32)

    def f_rope(pos, q1, q2, k1, k2, freq):
        ang = pos * freq
        cs, sn = jnp.cos(ang), jnp.sin(ang)
        return (jnp.concatenate([q1 * cs - q2 * sn, q1 * sn + q2 * cs], axis=1),
                jnp.concatenate([k1 * cs - k2 * sn, k1 * sn + k2 * cs], axis=1), cs, sn), ()

    q_pe, k_pe, cos_t, sin_t = _rowwise(
        f_rope, [pos_col, (q_mla, 4, LANES), (q_mla, 5, LANES),
                 (proj, O_KROT // LANES, LANES), (proj, O_KROT // LANES + 1, LANES)], [inv_freq_t],
        [(2 * LANES, F32), (2 * LANES, F32), (LANES, F32), (LANES, F32)], name="rope")

    o_b, lse = _mla_forward(q_mla, q_pe, kv, k_pe, s_len)

    def f_gate(oa, za, ob, zb):
        return (oa * _silu(za), ob * _silu(zb)), ()

    ya_in, yb_in = _rowwise(f_gate, [o_a, (proj, O_SBZ // SB_W, SB_W), o_b, (proj, O_MLAZ // MLA_W, MLA_W)], [],
                            [(SB_W, BF16), (MLA_W, BF16)], name="branch_gate")
    y_a = _mm(ya_in, w_a_f, name="branch_a")
    y_b = _mm(yb_in, w_b_f, name="branch_b")

    def f_merge(ga, gb, ya, yb):
        return (_sigmoid(ga) * ya + _sigmoid(gb) * yb,), ()

    (merged,) = _rowwise(f_merge, [(proj, O_GA // D, D), (proj, O_GB // D, D), y_a, y_b], [], [(D, BF16)], name="merge")
    out = _mm(merged, w_out_f, name="out_proj")

    def f_loss(x_, out_, t_, mg, bg, gf_):
        gate = mg + bg
        x2 = x_ + gate * out_
        r2 = _rms(x2)
        xn2 = x2 * r2
        err = xn2 * gf_ - t_
        loss = jnp.full((1, LANES), 0.5 / D, F32) * jnp.sum(err * err)
        dy = err * (1.0 / D)
        dx2 = _rms_bwd(xn2, r2, dy * gf_)
        return (dx2, dx2 * gate), (loss, _colsum(dy * xn2), _colsum(dx2 * out_))

    dx2, d_out, loss_part, d_gf, d_gate = _rowwise(
        f_loss, [x2d, out, tgt], [mod_gate, b_gate, gf], [(D, F32), (D, BF16)], [LANES, D, D], name="loss_head")

    d_merged = _mm(d_out, w_out_f, tb=True, name="d_merged")
    dw_out = _mm(merged, d_out, ta=True, name="dw_out")

    def f_dmerge(dm, ga, gb, ya, yb):
        sa, sb = _sigmoid(ga), _sigmoid(gb)
        return (dm * sa, dm * sb, dm * ya * sa * (1.0 - sa), dm * yb * sb * (1.0 - sb)), ()

    d_ya, d_yb, d_ga, d_gb = _rowwise(f_dmerge, [d_merged, (proj, O_GA // D, D), (proj, O_GB // D, D), y_a, y_b], [],
                                      [(D, BF16)] * 4, name="d_merge")
    dw_a = _mm(ya_in, d_ya, ta=True, name="dw_branch_a")
    dw_b = _mm(yb_in, d_yb, ta=True, name="dw_branch_b")
    d_ya_in = _mm(d_ya, w_a_f, tb=True, name="d_branch_a")
    d_yb_in = _mm(d_yb, w_b_f, tb=True, name="d_branch_b")

    def f_dgate(da, oa, za, db, ob, zb):
        return (da * _silu(za), da * oa * _silu_grad(za), db * _silu(zb), db * ob * _silu_grad(zb)), ()

    d_oa, d_sbz, d_ob, d_mlaz = _rowwise(
        f_dgate, [d_ya_in, o_a, (proj, O_SBZ // SB_W, SB_W), d_yb_in, o_b, (proj, O_MLAZ // MLA_W, MLA_W)], [],
        [(SB_W, F32), (SB_W, BF16), (MLA_W, F32), (MLA_W, BF16)], name="d_branch_gate")

    d_sbq, d_sbk, d_sbv = _sb_backward(proj, sb_tails, d_oa, s_len)
    dqn, dqpe4, dkn, dkpe4, dv_b = _mla_backward(q_mla, q_pe, kv, k_pe, o_b, lse, d_ob, s_len)

    def f_drope(dqn_, dkn_, dv_, q0, q1, q2, q3, k0, k1, k2, k3, cs, sn):
        dq = (q0 + q1) + (q2 + q3)
        dk = (k0 + k1) + (k2 + k3)
        dq1, dq2 = dq[:, :LANES], dq[:, LANES:]
        dk1, dk2 = dk[:, :LANES], dk[:, LANES:]
        return (jnp.concatenate([dqn_, dq1 * cs + dq2 * sn, dq2 * cs - dq1 * sn], axis=1),
                jnp.concatenate([dkn_, dv_], axis=1),
                jnp.concatenate([dk1 * cs + dk2 * sn, dk2 * cs - dk1 * sn], axis=1)), ()

    pe_rows = [(dqpe4, 0, 2 * LANES, t * s_len) for t in range(4)] + [(dkpe4, 0, 2 * LANES, t * s_len) for t in range(4)]
    dq_k, dkv_k, d_krot = _rowwise(f_drope, [dqn, dkn, dv_b] + pe_rows + [cos_t, sin_t], [],
                                   [(HEADS * QK_DIM, BF16), (2 * MLA_W, BF16), (2 * LANES, BF16)], name="d_rope")
    dw_uq_k = _mm(cq_n, dq_k, ta=True, name="dw_uq")
    dw_ukv_k = _mm(ckv_n, dkv_k, ta=True, name="dw_ukv")
    d_cqn = _mm(dq_k, w_uq_k, tb=True, name="d_cq_norm")
    d_ckvn = _mm(dkv_k, w_ukv_k, tb=True, name="d_ckv_norm")

    def f_dlat(cq, dcqn, ckv, dckvn, gq_, gkv_):
        rq, rkv = _rms(cq), _rms(ckv)
        cqn, ckvn = cq * rq, ckv * rkv
        return ((_rms_bwd(cqn, rq, dcqn * gq_), _rms_bwd(ckvn, rkv, dckvn * gkv_)),
                (_colsum(dcqn * cqn), _colsum(dckvn * ckvn)))

    d_cq, d_ckv, d_gq, d_gkv = _rowwise(
        f_dlat, [(proj, O_CQ // Q_RANK, Q_RANK), d_cqn, (proj, O_CKV // KV_RANK, KV_RANK), d_ckvn], [gq, gkv],
        [(Q_RANK, BF16), (KV_RANK, BF16)], [Q_RANK, KV_RANK], name="d_latent_norm")

    d_proj = jnp.concatenate([d_ga, d_gb, d_sbq.astype(BF16), d_sbk.astype(BF16), d_sbv.astype(BF16), d_sbz, d_mlaz,
                              d_cq, jnp.zeros((s_len, LANES), BF16), d_ckv, d_krot], axis=1)
    dw_in_k = _mm(h, d_proj, ta=True, name="dw_in")
    dw_krot_t = _mm(d_krot, h, ta=True, name="dw_krot")
    dh = _mm(d_proj, w_in_k, tb=True, name="d_h")

    def f_dx(x_, dh_, dx2_, g1_, msc, bsc):
        r = _rms(x_)
        xn = x_ * r
        dn1 = dh_ * (1.0 + (msc + bsc))
        return ((dx2_ + _rms_bwd(xn, r, dn1 * g1_),),
                (_colsum(dh_), _colsum(dh_ * (xn * g1_)), _colsum(dn1 * xn)))

    grad_x2d, d_shift, d_scale, d_g1 = _rowwise(f_dx, [x2d, dh, dx2], [g1, mod_scale, b_scale], [(D, F32)],
                                                [D, D, D], name="d_ada_norm")

    def krot_body(t_ref, o_ref):
        half = ROPE // 2
        for part in range(2):
            acc = t_ref[part * LANES:part * LANES + half, :]
            for hh in range(1, HEADS):
                acc = acc + t_ref[part * LANES + hh * half:part * LANES + (hh + 1) * half, :]
            o_ref[part * half:(part + 1) * half, :] = acc

    dw_krot = pl.pallas_call(krot_body, name="dw_krot_sum", out_shape=jax.ShapeDtypeStruct((ROPE, D), F32))(dw_krot_t).T

    dw_in_full = jnp.concatenate([
        dw_in_k[:, O_SBQ:O_SBQ + 2048], dw_in_k[:, O_CQ:O_CQ + Q_RANK], dw_in_k[:, O_CKV:O_CKV + KV_RANK], dw_krot,
        dw_in_k[:, O_MLAZ:O_MLAZ + MLA_W], dw_in_k[:, O_GA:O_GA + 2 * D]], axis=1)
    dw_uq_full = _uq_from_kernel_layout(dw_uq_k)
    dw_ukv_full = _ukv_from_kernel_layout(dw_ukv_k)

    def col_blocks(g):
        kdim, n8 = g.shape
        return g.astype(BF16).reshape(kdim, N_DEV, n8 // N_DEV).transpose(1, 0, 2).reshape(N_DEV, -1, LANES)

    g_blocks = jnp.concatenate([col_blocks(dw_in_full), col_blocks(dw_uq_full), col_blocks(dw_ukv_full),
                                col_blocks(dw_a), col_blocks(dw_b),
                                dw_out.astype(BF16).reshape(N_DEV, -1, LANES)], axis=1)
    g_recv = _exchange(g_blocks, "exchange_grads")
    g_sum = _sum_blocks(g_recv, "sum_grads")
    g_big = [g_sum[offs[t]:offs[t + 1]].reshape(big[t].shape) for t in range(6)]

    small = jnp.concatenate([d_shift, d_scale, d_gate, d_g1, d_gq, d_gkv, d_gf], axis=1)
    n_small = small.shape[1]
    pad = (-n_small) % (8 * LANES)
    small = jnp.pad(small, ((0, 0), (0, pad))).reshape(-1, LANES)
    small_all = _all_gather(small, "gather_small")
    small_sum = _sum_blocks(small_all, "sum_small").reshape(1, -1)
    g_b_ada = small_sum[:, :3 * D]
    g_g1 = small_sum[:, 3 * D:4 * D]
    g_gq = small_sum[:, 4 * D:4 * D + Q_RANK]
    g_gkv = small_sum[:, 4 * D + Q_RANK:4 * D + Q_RANK + KV_RANK]
    g_gf = small_sum[:, 4 * D + Q_RANK + KV_RANK:4 * D + Q_RANK + KV_RANK + D]

    dmod_all = small_all.reshape(N_DEV, -1)[:, :3 * D]
    dmod_cols = lax.dynamic_slice_in_dim(dmod_all, me * (3 * D // N_DEV), 3 * D // N_DEV, axis=1)
    g_w_ada = _mm(c_all, dmod_cols, ta=True, name="dw_ada")

    loss = lax.psum(loss_part[0, 0], ("x", "y", "c"))

    names = ["w_ada", "b_ada", "norm_gain", "w_in", "q_norm_gain", "w_uq", "kv_norm_gain", "w_ukv",
             "w_branch_a", "w_branch_b", "w_out", "final_norm_gain"]
    weights = dict(w_ada=w_ada, b_ada=b_ada, norm_gain=norm_gain, w_in=w_in, q_norm_gain=q_norm_gain, w_uq=w_uq,
                   kv_norm_gain=kv_norm_gain, w_ukv=w_ukv, w_branch_a=w_branch_a, w_branch_b=w_branch_b, w_out=w_out,
                   final_norm_gain=final_norm_gain)
    moms = dict(w_ada=m_w_ada, b_ada=m_b_ada, norm_gain=m_norm_gain, w_in=m_w_in, q_norm_gain=m_q_norm_gain,
                w_uq=m_w_uq, kv_norm_gain=m_kv_norm_gain, w_ukv=m_w_ukv, w_branch_a=m_w_branch_a,
                w_branch_b=m_w_branch_b, w_out=m_w_out, final_norm_gain=m_final_norm_gain)
    vels = dict(w_ada=v_w_ada, b_ada=v_b_ada, norm_gain=v_norm_gain, w_in=v_w_in, q_norm_gain=v_q_norm_gain,
                w_uq=v_w_uq, kv_norm_gain=v_kv_norm_gain, w_ukv=v_w_ukv, w_branch_a=v_w_branch_a,
                w_branch_b=v_w_branch_b, w_out=v_w_out, final_norm_gain=v_final_norm_gain)
    grads2d = dict(w_ada=g_w_ada, b_ada=g_b_ada, norm_gain=g_g1, w_in=g_big[0], q_norm_gain=g_gq, w_uq=g_big[1],
                   kv_norm_gain=g_gkv, w_ukv=g_big[2], w_branch_a=g_big[3], w_branch_b=g_big[4], w_out=g_big[5],
                   final_norm_gain=g_gf)

    grads, deltas, new_m, new_v = [], [], [], []
    for n in names:
        w = weights[n]
        shape2d = grads2d[n].shape
        d_, m_, v_ = _adamw_call(w.reshape(shape2d), grads2d[n], moms[n].reshape(shape2d), vels[n].reshape(shape2d),
                                 "adamw_" + n)
        grads.append(grads2d[n].reshape(w.shape))
        deltas.append(d_.reshape(w.shape))
        new_m.append(m_.reshape(w.shape))
        new_v.append(v_.reshape(w.shape))

    return (loss, grad_x2d.reshape(x.shape), *grads, *deltas, *new_m, *new_v)
```

```python
import functools
import math

import jax
import jax.numpy as jnp
from jax import lax
from jax.experimental import pallas as pl
from jax.experimental.pallas import tpu as pltpu

F32 = jnp.float32
BF16 = jnp.bfloat16
MXU_DTYPE = jnp.bfloat16

N_DEV = 8
D = 1024
HEADS = 8
HD = 64
SB_W = 512
MLA_W = 512
Q_RANK = 384
KV_RANK = 256
ROPE = 32
NOPE = 64
QK_DIM = NOPE + ROPE
EPS = 1e-6
ROPE_BASE = 10000.0

ADAM_LR = 0.001
ADAM_B1 = 0.9
ADAM_B2 = 0.999
ADAM_EPS = 1e-08
ADAM_WD = 0.01
ADAM_STEP = 10

LANES = 128
VMEM_LIMIT = 48 * 1024 * 1024

O_GA, O_GB = 0, 1024
O_SBQ, O_SBK, O_SBV, O_SBZ = 2048, 2560, 3072, 3584
O_MLAZ = 4096
O_CQ = 4608
O_CKV = 5120
O_KROT = 5376
IN_PAD = 5632

BQ = 256
BK = 256


def _cparams(*sem):
    return pltpu.CompilerParams(dimension_semantics=sem, vmem_limit_bytes=VMEM_LIMIT)


def _tile_of(n, cap=512):
    if n <= cap:
        return n
    for t in (512, 384, 256, 128):
        if t <= cap and n % t == 0:
            return t
    raise ValueError(n)


def _rowwise(fn, rows, vecs, outs, reds=(), *, name, tile=256):
    norm = []
    for r in rows:
        if isinstance(r, tuple):
            arr, cb, w = r[:3]
            ro = r[3] if len(r) > 3 else 0
        else:
            arr, cb, w, ro = r, 0, r.shape[1], 0
        norm.append((arr, cb, w, ro))
    s_len = norm[0][0].shape[0]
    tile = min(tile, s_len)
    assert s_len % tile == 0
    n_row, n_vec, n_out, n_red = len(norm), len(vecs), len(outs), len(reds)

    def body(*refs):
        step = pl.program_id(0)
        row_refs = refs[:n_row]
        vec_refs = refs[n_row:n_row + n_vec]
        out_refs = refs[n_row + n_vec:n_row + n_vec + n_out]
        red_refs = refs[n_row + n_vec + n_out:]
        row_res, red_res = fn(*[r[...] for r in row_refs], *[v[...] for v in vec_refs])
        for o, val in zip(out_refs, row_res):
            o[...] = val.astype(o.dtype)
        if n_red:
            @pl.when(step == 0)
            def _():
                for r in red_refs:
                    r[...] = jnp.zeros(r.shape, r.dtype)
            for r, val in zip(red_refs, red_res):
                r[...] += val

    in_specs = []
    for arr, cb, w, ro in norm:
        in_specs.append(pl.BlockSpec((tile, w), functools.partial(lambda i, cb, rb: (i + rb, cb), cb=cb, rb=ro // tile)))
        assert ro % tile == 0
    for v in vecs:
        in_specs.append(pl.BlockSpec(v.shape, lambda i: (0, 0)))
    out_shape = [jax.ShapeDtypeStruct((s_len, w), dt) for w, dt in outs]
    out_specs = [pl.BlockSpec((tile, w), lambda i: (i, 0)) for w, _ in outs]
    out_shape += [jax.ShapeDtypeStruct((1, w), F32) for w in reds]
    out_specs += [pl.BlockSpec((1, w), lambda i: (0, 0)) for w in reds]
    res = pl.pallas_call(
        body, name=name, grid=(s_len // tile,), in_specs=in_specs, out_specs=out_specs, out_shape=out_shape,
        compiler_params=_cparams("arbitrary" if n_red else "parallel"),
    )(*[a for a, _, _, _ in norm], *vecs)
    return res


def _mm(a, b, *, ta=False, tb=False, out_dtype=F32, name):
    m, k = (a.shape[1], a.shape[0]) if ta else a.shape
    n = b.shape[0] if tb else b.shape[1]
    assert (b.shape[1] if tb else b.shape[0]) == k
    tm, tn, tk = _tile_of(m), _tile_of(n), _tile_of(k)
    nk = k // tk
    dims = (((0 if ta else 1,), (1 if tb else 0,)), ((), ()))

    def body(a_ref, b_ref, o_ref, acc_ref):
        kk = pl.program_id(2)

        @pl.when(kk == 0)
        def _():
            acc_ref[...] = jnp.zeros(acc_ref.shape, F32)

        acc_ref[...] += lax.dot_general(a_ref[...].astype(MXU_DTYPE), b_ref[...].astype(MXU_DTYPE), dims,
                                        preferred_element_type=F32)

        @pl.when(kk == nk - 1)
        def _():
            o_ref[...] = acc_ref[...].astype(o_ref.dtype)

    a_spec = pl.BlockSpec((tk, tm), lambda i, j, kk: (kk, i)) if ta else pl.BlockSpec((tm, tk), lambda i, j, kk: (i, kk))
    b_spec = pl.BlockSpec((tn, tk), lambda i, j, kk: (j, kk)) if tb else pl.BlockSpec((tk, tn), lambda i, j, kk: (kk, j))
    return pl.pallas_call(
        body, name=name, grid=(m // tm, n // tn, nk), in_specs=[a_spec, b_spec],
        out_specs=pl.BlockSpec((tm, tn), lambda i, j, kk: (i, j)),
        out_shape=jax.ShapeDtypeStruct((m, n), out_dtype),
        scratch_shapes=[pltpu.VMEM((tm, tn), F32)],
        compiler_params=_cparams("parallel", "parallel", "arbitrary"),
    )(a, b)


_NT = (((1,), (1,)), ((), ()))
_TN = (((0,), (0,)), ((), ()))


def _dot(a, b):
    return jnp.dot(a, b, preferred_element_type=F32)


def _dot_nt(a, b):
    return lax.dot_general(a, b, _NT, preferred_element_type=F32)


def _dot_tn(a, b):
    return lax.dot_general(a, b, _TN, preferred_element_type=F32)


def _split_dot(x, tri):
    hi = x.astype(MXU_DTYPE)
    lo = (x - hi.astype(F32)).astype(MXU_DTYPE)
    return _dot(hi, tri) + _dot(lo, tri)


def _neg_softplus(z):
    return -(jnp.maximum(z, 0.0) + jnp.log1p(jnp.exp(-jnp.abs(z))))


def _attn_consts():
    row = lax.broadcasted_iota(jnp.int32, (BQ, BK), 0)
    col = lax.broadcasted_iota(jnp.int32, (BQ, BK), 1)
    trow = lax.broadcasted_iota(jnp.int32, (BK, BK), 0)
    tcol = lax.broadcasted_iota(jnp.int32, (BK, BK), 1)
    lane = lax.broadcasted_iota(jnp.int32, (BQ, LANES), 1)
    klane = lax.broadcasted_iota(jnp.int32, (BK, LANES), 1)
    return row, col, trow, tcol, lane, klane


def _diag_block(i):
    jd = (i * BQ) // BK
    return jd, i * BQ - jd * BK


def _key_slice(j):
    return pl.ds(pl.multiple_of(j * BK, BK), BK)


def _sb_forward(proj, s_len):
    nq = s_len // BQ
    assert s_len // BK <= HD
    qc, kc, vc = O_SBQ // LANES, O_SBK // LANES, O_SBV // LANES

    def body(q_ref, k_ref, v_ref, o_ref, tails_ref):
        i = pl.program_id(1)
        jd, off = _diag_block(i)
        row, col, trow, tcol, lane, klane = _attn_consts()
        strict = col < row + off
        tri = (trow >= tcol).astype(MXU_DTYPE)
        q = q_ref[...] * 0.125
        qh = [jnp.where(lane < HD, q, 0.0).astype(MXU_DTYPE), jnp.where(lane >= HD, q, 0.0).astype(MXU_DTYPE)]

        def step(j, st, diag):
            carry, acc, tail = st
            sl = _key_slice(j)
            kj = k_ref[sl, :].astype(MXU_DTYPE)
            vj = v_ref[sl, :].astype(MXU_DTYPE)
            vh = [jnp.where(klane < HD, vj, 0), jnp.where(klane >= HD, vj, 0)]
            new_carry = []
            for h in range(2):
                z = _dot_nt(qh[h], kj)
                lom = _neg_softplus(z)
                if diag:
                    lom = jnp.where(strict, lom, 0.0)
                a = jnp.exp(z + _split_dot(lom, tri) + carry[h])
                if diag:
                    a = jnp.where(strict, a, 0.0)
                acc = acc + _dot(a.astype(MXU_DTYPE), vh[h])
                tail = jnp.where(lane == h * HD + j, carry[h], tail)
                new_carry.append(carry[h] + jnp.sum(lom, axis=1, keepdims=True))
            return tuple(new_carry), acc, tail

        zero = jnp.zeros((BQ, LANES), F32)
        st = step(jd, ((jnp.zeros((BQ, 1), F32),) * 2, zero, zero), True)
        st = lax.fori_loop(0, jd, lambda jj, s: step(jd - 1 - jj, s, False), st)
        o_ref[...] = st[1]
        tails_ref[...] = st[2]

    blk = pl.BlockSpec((BQ, LANES), lambda p, i: (i, p))
    out = jax.ShapeDtypeStruct((s_len, SB_W), F32)
    return pl.pallas_call(
        body, name="sb_fwd", grid=(HEADS // 2, nq),
        in_specs=[pl.BlockSpec((BQ, LANES), lambda p, i: (i, qc + p)),
                  pl.BlockSpec((s_len, LANES), lambda p, i: (0, kc + p)),
                  pl.BlockSpec((s_len, LANES), lambda p, i: (0, vc + p))],
        out_specs=[blk, blk], out_shape=[out, out],
        compiler_params=_cparams("parallel", "parallel"),
    )(proj, proj, proj)


def _sb_backward(proj, tails, do, s_len):
    nq = s_len // BQ
    qc, kc, vc = O_SBQ // LANES, O_SBK // LANES, O_SBV // LANES

    def body(q_ref, k_ref, v_ref, tails_ref, do_ref, dq_ref, dk_ref, dv_ref):
        i = pl.program_id(1)
        jd, off = _diag_block(i)

        @pl.when(i == 0)
        def _():
            dk_ref[...] = jnp.zeros(dk_ref.shape, F32)
            dv_ref[...] = jnp.zeros(dv_ref.shape, F32)

        row, col, trow, tcol, lane, klane = _attn_consts()
        strict = col < row + off
        tri = (trow >= tcol).astype(MXU_DTYPE)
        tri_p = (trow <= tcol).astype(MXU_DTYPE)
        q = q_ref[...] * 0.125
        tails_blk = tails_ref[...]
        do_blk = do_ref[...]
        hm = [lane < HD, lane >= HD]
        km = [klane < HD, klane >= HD]
        qh = [jnp.where(m, q, 0.0).astype(MXU_DTYPE) for m in hm]
        doh = [jnp.where(m, do_blk, 0.0).astype(MXU_DTYPE) for m in hm]

        def step(j, st, diag):
            before, dq = st
            sl = _key_slice(j)
            kj = k_ref[sl, :].astype(MXU_DTYPE)
            vj = v_ref[sl, :].astype(MXU_DTYPE)
            dk_add = dv_add = None
            new_before = []
            for h in range(2):
                tail = jnp.sum(jnp.where(lane == h * HD + j, tails_blk, 0.0), axis=1, keepdims=True)
                z = _dot_nt(qh[h], kj)
                lom = _neg_softplus(z)
                if diag:
                    lom = jnp.where(strict, lom, 0.0)
                a = jnp.exp(z + _split_dot(lom, tri) + tail)
                if diag:
                    a = jnp.where(strict, a, 0.0)
                dl = _dot_nt(doh[h], vj) * a
                upto = before[h] + _split_dot(dl, tri_p)
                dz = dl - jnp.exp(z + lom) * upto
                if diag:
                    dz = jnp.where(strict, dz, 0.0)
                dzb = dz.astype(MXU_DTYPE)
                dq = dq + _dot(dzb, jnp.where(km[h], kj, 0))
                dk_h = _dot_tn(dzb, qh[h])
                dv_h = _dot_tn(a.astype(MXU_DTYPE), doh[h])
                dk_add = dk_h if dk_add is None else dk_add + dk_h
                dv_add = dv_h if dv_add is None else dv_add + dv_h
                new_before.append(before[h] + jnp.sum(dl, axis=1, keepdims=True))
            dk_ref[sl, :] += dk_add
            dv_ref[sl, :] += dv_add
            return tuple(new_before), dq

        st = ((jnp.zeros((BQ, 1), F32),) * 2, jnp.zeros((BQ, LANES), F32))
        st = lax.fori_loop(0, jd, lambda j, s: step(j, s, False), st)
        st = step(jd, st, True)
        dq_ref[...] = st[1] * 0.125

    blk = pl.BlockSpec((BQ, LANES), lambda p, i: (i, p))
    full = pl.BlockSpec((s_len, LANES), lambda p, i: (0, p))
    out = jax.ShapeDtypeStruct((s_len, SB_W), F32)
    return pl.pallas_call(
        body, name="sb_bwd", grid=(HEADS // 2, nq),
        in_specs=[pl.BlockSpec((BQ, LANES), lambda p, i: (i, qc + p)),
                  pl.BlockSpec((s_len, LANES), lambda p, i: (0, kc + p)),
                  pl.BlockSpec((s_len, LANES), lambda p, i: (0, vc + p)),
                  blk, blk],
        out_specs=[blk, full, full], out_shape=[out, out, out],
        compiler_params=_cparams("arbitrary", "arbitrary"),
    )(proj, proj, proj, tails, do)


def _pe_mask(rows, head):
    lane = lax.broadcasted_iota(jnp.int32, (rows, 2 * LANES), 1)
    return ((lane % LANES) // (ROPE // 2)) == head


def _mla_forward(q_mla, q_pe, kv, k_pe, s_len):
    nq = s_len // BQ
    scale = 1.0 / math.sqrt(QK_DIM)

    def body(qn_ref, qpe_ref, kn_ref, kpe_ref, v_ref, o_ref, lse_ref):
        p = pl.program_id(0)
        i = pl.program_id(1)
        jd, off = _diag_block(i)
        row, col, trow, tcol, lane, klane = _attn_consts()
        causal = col <= row + off
        qn = qn_ref[...]
        qpe = qpe_ref[...]
        hm = [lane < HD, lane >= HD]
        km = [klane < HD, klane >= HD]
        qnh = [jnp.where(m, qn, 0.0).astype(MXU_DTYPE) for m in hm]
        qph = [jnp.where(_pe_mask(BQ, 2 * p + h), qpe, 0.0).astype(MXU_DTYPE) for h in range(2)]

        def step(j, st, diag):
            m_run, l_run, acc = st
            sl = _key_slice(j)
            knj = kn_ref[sl, :].astype(MXU_DTYPE)
            kpj = kpe_ref[sl, :].astype(MXU_DTYPE)
            vj = v_ref[sl, :].astype(MXU_DTYPE)
            new_m, new_l, alphas, adds = [], [], [], []
            for h in range(2):
                s = (_dot_nt(qnh[h], knj) + _dot_nt(qph[h], kpj)) * scale
                if diag:
                    s = jnp.where(causal, s, -jnp.inf)
                m_new = jnp.maximum(m_run[h], jnp.max(s, axis=1, keepdims=True))
                alpha = jnp.exp(m_run[h] - m_new)
                pr = jnp.exp(s - m_new)
                new_m.append(m_new)
                new_l.append(alpha * l_run[h] + jnp.sum(pr, axis=1, keepdims=True))
                alphas.append(alpha)
                adds.append(_dot(pr.astype(MXU_DTYPE), jnp.where(km[h], vj, 0)))
            acc = jnp.where(hm[0], alphas[0], alphas[1]) * acc + (adds[0] + adds[1])
            return tuple(new_m), tuple(new_l), acc

        st = ((jnp.full((BQ, 1), -1e30, F32),) * 2, (jnp.zeros((BQ, 1), F32),) * 2, jnp.zeros((BQ, LANES), F32))
        st = step(jd, st, True)
        m_run, l_run, acc = lax.fori_loop(0, jd, lambda jj, s: step(jd - 1 - jj, s, False), st)
        o_ref[...] = acc / jnp.where(hm[0], l_run[0], l_run[1])
        lse_ref[...] = jnp.where(hm[0], m_run[0] + jnp.log(l_run[0]), m_run[1] + jnp.log(l_run[1]))

    blk = pl.BlockSpec((BQ, LANES), lambda p, i: (i, p))
    out = jax.ShapeDtypeStruct((s_len, MLA_W), F32)
    return pl.pallas_call(
        body, name="mla_fwd", grid=(HEADS // 2, nq),
        in_specs=[blk,
                  pl.BlockSpec((BQ, 2 * LANES), lambda p, i: (i, 0)),
                  pl.BlockSpec((s_len, LANES), lambda p, i: (0, p)),
                  pl.BlockSpec((s_len, 2 * LANES), lambda p, i: (0, 0)),
                  pl.BlockSpec((s_len, LANES), lambda p, i: (0, MLA_W // LANES + p))],
        out_specs=[blk, blk], out_shape=[out, out],
        compiler_params=_cparams("parallel", "parallel"),
    )(q_mla, q_pe, kv, k_pe, kv)


def _mla_backward(q_mla, q_pe, kv, k_pe, o, lse, do, s_len):
    nq = s_len // BQ
    scale = 1.0 / math.sqrt(QK_DIM)

    def body(qn_ref, qpe_ref, kn_ref, kpe_ref, v_ref, o_ref, lse_ref, do_ref,
             dqn_ref, dqpe_ref, dkn_ref, dkpe_ref, dv_ref):
        p = pl.program_id(0)
        i = pl.program_id(1)

        @pl.when(i == 0)
        def _():
            dkn_ref[...] = jnp.zeros(dkn_ref.shape, F32)
            dkpe_ref[...] = jnp.zeros(dkpe_ref.shape, F32)
            dv_ref[...] = jnp.zeros(dv_ref.shape, F32)

        jd, off = _diag_block(i)
        row, col, trow, tcol, lane, klane = _attn_consts()
        causal = col <= row + off
        qn = qn_ref[...]
        qpe = qpe_ref[...]
        o_blk = o_ref[...]
        do_blk = do_ref[...]
        lse_blk = lse_ref[...]
        hm = [lane < HD, lane >= HD]
        km = [klane < HD, klane >= HD]
        kpm = [_pe_mask(BK, 2 * p + h) for h in range(2)]
        qnh = [jnp.where(m, qn, 0.0).astype(MXU_DTYPE) for m in hm]
        qph = [jnp.where(_pe_mask(BQ, 2 * p + h), qpe, 0.0).astype(MXU_DTYPE) for h in range(2)]
        doh_f = [jnp.where(m, do_blk, 0.0) for m in hm]
        doh = [d.astype(MXU_DTYPE) for d in doh_f]
        delta = [jnp.sum(d * o_blk, axis=1, keepdims=True) for d in doh_f]
        lse_h = [jnp.sum(jnp.where(lane == h * HD, lse_blk, 0.0), axis=1, keepdims=True) for h in range(2)]

        def step(j, st, diag):
            dqn, dqp = st
            sl = _key_slice(j)
            knj = kn_ref[sl, :].astype(MXU_DTYPE)
            kpj = kpe_ref[sl, :].astype(MXU_DTYPE)
            vj = v_ref[sl, :].astype(MXU_DTYPE)
            dkn_add = dkp_add = dv_add = None
            for h in range(2):
                s = (_dot_nt(qnh[h], knj) + _dot_nt(qph[h], kpj)) * scale
                pr = jnp.exp(s - lse_h[h])
                if diag:
                    pr = jnp.where(causal, pr, 0.0)
                ds = pr * (_dot_nt(doh[h], vj) - delta[h]) * scale
                dsb = ds.astype(MXU_DTYPE)
                dqn = dqn + _dot(dsb, jnp.where(km[h], knj, 0))
                dqp = dqp + _dot(dsb, jnp.where(kpm[h], kpj, 0))
                t_n, t_p, t_v = _dot_tn(dsb, qnh[h]), _dot_tn(dsb, qph[h]), _dot_tn(pr.astype(MXU_DTYPE), doh[h])
                dkn_add = t_n if dkn_add is None else dkn_add + t_n
                dkp_add = t_p if dkp_add is None else dkp_add + t_p
                dv_add = t_v if dv_add is None else dv_add + t_v
            dkn_ref[sl, :] += dkn_add
            dkpe_ref[sl, :] += dkp_add
            dv_ref[sl, :] += dv_add
            return dqn, dqp

        st = step(jd, (jnp.zeros((BQ, LANES), F32), jnp.zeros((BQ, 2 * LANES), F32)), True)
        st = lax.fori_loop(0, jd, lambda jj, s: step(jd - 1 - jj, s, False), st)
        dqn_ref[...] = st[0]
        dqpe_ref[...] = st[1]

    blk = pl.BlockSpec((BQ, LANES), lambda p, i: (i, p))
    full = pl.BlockSpec((s_len, LANES), lambda p, i: (0, p))
    out = jax.ShapeDtypeStruct((s_len, MLA_W), F32)
    out_pe = jax.ShapeDtypeStruct((4 * s_len, 2 * LANES), F32)
    return pl.pallas_call(
        body, name="mla_bwd", grid=(HEADS // 2, nq),
        in_specs=[blk,
                  pl.BlockSpec((BQ, 2 * LANES), lambda p, i: (i, 0)),
                  pl.BlockSpec((s_len, LANES), lambda p, i: (0, p)),
                  pl.BlockSpec((s_len, 2 * LANES), lambda p, i: (0, 0)),
                  pl.BlockSpec((s_len, LANES), lambda p, i: (0, MLA_W // LANES + p)),
                  blk, blk, blk],
        out_specs=[blk, pl.BlockSpec((BQ, 2 * LANES), lambda p, i: (p * nq + i, 0)), full,
                   pl.BlockSpec((s_len, 2 * LANES), lambda p, i: (p, 0)), full],
        out_shape=[out, out_pe, out, out_pe, out],
        compiler_params=_cparams("arbitrary", "arbitrary"),
    )(q_mla, q_pe, kv, k_pe, kv, o, lse, do)


def _mesh_pos():
    return lax.axis_index("x"), lax.axis_index("y"), lax.axis_index("c")


def _dev_index(px, py, pc):
    return 4 * px + 2 * py + pc


def _all_gather(block, name):
    def body(x_ref, out_ref, send_sems, recv_sems, local_sem):
        x, y, c = _mesh_pos()
        me, sibling = (x, y, c), (x, y, 1 - c)
        chips = [(1 - x, y), (x, 1 - y), (1 - x, 1 - y)]

        def slot(pos):
            return out_ref.at[_dev_index(*pos)]

        def copy(k, blockpos, to, src=None):
            return pltpu.make_async_remote_copy(
                src_ref=slot(blockpos) if src is None else src, dst_ref=slot(blockpos),
                send_sem=send_sems.at[k], recv_sem=recv_sems.at[k],
                device_id=to, device_id_type=pl.DeviceIdType.MESH)

        mine = pltpu.make_async_copy(x_ref, slot(me), local_sem)
        mine.start()
        first = [copy(0, me, sibling, src=x_ref)]
        first += [copy(1 + j, me, (*chip, c), src=x_ref) for j, chip in enumerate(chips)]
        for cp in first:
            cp.start()
        passed = [copy(4 + j, (*chip, c), sibling) for j, chip in enumerate(chips)]
        for j, chip in enumerate(chips):
            copy(1 + j, (*chip, c), me).wait_recv()
            passed[j].start()
        copy(0, sibling, me).wait_recv()
        for j, chip in enumerate(chips):
            copy(4 + j, (*chip, 1 - c), me).wait_recv()
        for cp in first + passed:
            cp.wait_send()
        mine.wait()

    return pl.pallas_call(
        body, name=name,
        out_shape=jax.ShapeDtypeStruct((N_DEV,) + block.shape, block.dtype),
        in_specs=[pl.BlockSpec(memory_space=pl.ANY)], out_specs=pl.BlockSpec(memory_space=pl.ANY),
        scratch_shapes=[pltpu.SemaphoreType.DMA((7,)), pltpu.SemaphoreType.DMA((7,)), pltpu.SemaphoreType.DMA],
    )(block)


def _exchange(blocks, name):
    def body(x_ref, out_ref, send_sems, recv_sems, local_sem):
        x, y, c = _mesh_pos()
        me = _dev_index(x, y, c)
        flips = [(fx, fy, fc) for fx in (0, 1) for fy in (0, 1) for fc in (0, 1)][1:]
        peers = [(1 - x if fx else x, 1 - y if fy else y, 1 - c if fc else c) for fx, fy, fc in flips]
        mine = pltpu.make_async_copy(x_ref.at[me], out_ref.at[me], local_sem)
        mine.start()
        sends = []
        for k, peer in enumerate(peers):
            cp = pltpu.make_async_remote_copy(
                src_ref=x_ref.at[_dev_index(*peer)], dst_ref=out_ref.at[me],
                send_sem=send_sems.at[k], recv_sem=recv_sems.at[k],
                device_id=peer, device_id_type=pl.DeviceIdType.MESH)
            cp.start()
            sends.append(cp)
        for k, peer in enumerate(peers):
            pltpu.make_async_remote_copy(
                src_ref=x_ref.at[me], dst_ref=out_ref.at[_dev_index(*peer)],
                send_sem=send_sems.at[k], recv_sem=recv_sems.at[k],
                device_id=peer, device_id_type=pl.DeviceIdType.MESH).wait_recv()
        for cp in sends:
            cp.wait_send()
        mine.wait()

    return pl.pallas_call(
        body, name=name,
        out_shape=jax.ShapeDtypeStruct(blocks.shape, blocks.dtype),
        in_specs=[pl.BlockSpec(memory_space=pl.ANY)], out_specs=pl.BlockSpec(memory_space=pl.ANY),
        scratch_shapes=[pltpu.SemaphoreType.DMA((7,)), pltpu.SemaphoreType.DMA((7,)), pltpu.SemaphoreType.DMA],
    )(blocks)


def _sum_blocks(parts, name):
    n, r, c = parts.shape
    tile = max(t for t in range(16, min(r, 2048) + 1, 16) if r % t == 0)

    def body(p_ref, o_ref):
        acc = p_ref[0].astype(F32)
        for s in range(1, n):
            acc = acc + p_ref[s].astype(F32)
        o_ref[...] = acc

    return pl.pallas_call(
        body, name=name, grid=(r // tile,),
        in_specs=[pl.BlockSpec((n, tile, c), lambda i: (0, i, 0))],
        out_specs=pl.BlockSpec((tile, c), lambda i: (i, 0)),
        out_shape=jax.ShapeDtypeStruct((r, c), F32),
        compiler_params=_cparams("parallel"),
    )(parts)


def _sigmoid(x):
    return 1.0 / (1.0 + jnp.exp(-x))


def _silu(x):
    return x * _sigmoid(x)


def _silu_grad(x):
    s = _sigmoid(x)
    return s * (1.0 + x * (1.0 - s))


def _colsum(x):
    return jnp.sum(x, axis=0, keepdims=True)


def _rms(x):
    return lax.rsqrt(jnp.mean(x * x, axis=-1, keepdims=True) + EPS)


def _rms_bwd(xn, r, dxn):
    return r * (dxn - xn * jnp.mean(dxn * xn, axis=-1, keepdims=True))


def _adamw(w, g, m, v):
    m = ADAM_B1 * m + (1.0 - ADAM_B1) * g
    v = ADAM_B2 * v + (1.0 - ADAM_B2) * jnp.square(g)
    m_hat = m / (1.0 - ADAM_B1 ** ADAM_STEP)
    v_hat = v / (1.0 - ADAM_B2 ** ADAM_STEP)
    delta = -ADAM_LR * (m_hat / (jnp.sqrt(v_hat) + ADAM_EPS) + ADAM_WD * w)
    return delta, m, v


def _adamw_call(w, g, m, v, name):
    r, c = w.shape
    tile = r if (r <= 256 or r % 256) else 256
    return _rowwise(lambda w_, g_, m_, v_: (_adamw(w_, g_, m_, v_), ()), [w, g, m, v], [],
                    [(c, F32)] * 3, name=name, tile=tile)


def _uq_to_kernel_layout(w):
    lead = w.shape[:-1]
    t = w.reshape(lead + (HEADS, QK_DIM))
    return jnp.concatenate([t[..., :NOPE].reshape(lead + (HEADS * NOPE,)),
                            t[..., NOPE:NOPE + ROPE // 2].reshape(lead + (LANES,)),
                            t[..., NOPE + ROPE // 2:].reshape(lead + (LANES,))], axis=-1)


def _uq_from_kernel_layout(w):
    lead = w.shape[:-1]
    nope = w[..., :HEADS * NOPE].reshape(lead + (HEADS, NOPE))
    r1 = w[..., HEADS * NOPE:HEADS * NOPE + LANES].reshape(lead + (HEADS, ROPE // 2))
    r2 = w[..., HEADS * NOPE + LANES:].reshape(lead + (HEADS, ROPE // 2))
    return jnp.concatenate([nope, r1, r2], axis=-1).reshape(lead + (HEADS * QK_DIM,))


def _ukv_to_kernel_layout(w):
    lead = w.shape[:-1]
    t = w.reshape(lead + (HEADS, NOPE + HD))
    return jnp.concatenate([t[..., :NOPE].reshape(lead + (HEADS * NOPE,)),
                            t[..., NOPE:].reshape(lead + (HEADS * HD,))], axis=-1)


def _ukv_from_kernel_layout(w):
    lead = w.shape[:-1]
    kn = w[..., :HEADS * NOPE].reshape(lead + (HEADS, NOPE))
    vv = w[..., HEADS * NOPE:].reshape(lead + (HEADS, HD))
    return jnp.concatenate([kn, vv], axis=-1).reshape(lead + (HEADS * (NOPE + HD),))


def _w_in_to_kernel_layout(w):
    sb_q, sb_k, sb_v, sb_z = (w[:, 512 * t:512 * (t + 1)] for t in range(4))
    c_q = w[:, 2048:2432]
    c_kv = w[:, 2432:2688]
    k_rot = w[:, 2688:2720]
    mla_z = w[:, 2720:3232]
    g_a = w[:, 3232:4256]
    g_b = w[:, 4256:5280]
    zeros = jnp.zeros((w.shape[0], LANES), w.dtype)
    k1 = jnp.tile(k_rot[:, :ROPE // 2], (1, HEADS))
    k2 = jnp.tile(k_rot[:, ROPE // 2:], (1, HEADS))
    return jnp.concatenate([g_a, g_b, sb_q, sb_k, sb_v, sb_z, mla_z, c_q, zeros, c_kv, k1, k2], axis=1)


def kernel(x, c, positions, w_ada, b_ada, norm_gain, w_in, q_norm_gain, w_uq, kv_norm_gain, w_ukv, w_branch_a, w_branch_b, w_out, final_norm_gain, loss_target, m_w_ada, m_b_ada, m_norm_gain, m_w_in, m_q_norm_gain, m_w_uq, m_kv_norm_gain, m_w_ukv, m_w_branch_a, m_w_branch_b, m_w_out, m_final_norm_gain, v_w_ada, v_b_ada, v_norm_gain, v_w_in, v_q_norm_gain, v_w_uq, v_kv_norm_gain, v_w_ukv, v_w_branch_a, v_w_branch_b, v_w_out, v_final_norm_gain):
    s_len = x.shape[1]
    me = _dev_index(*_mesh_pos())
    x2d = x[0]
    tgt = loss_target[0]

    big = [w_in[0], w_uq[0], w_ukv[0], w_branch_a[0], w_branch_b[0], w_out[0]]
    big_sizes = [int(w.size) for w in big]
    packed = jnp.concatenate([w.astype(BF16).reshape(-1, LANES) for w in big], axis=0)
    gathered = _all_gather(packed, "gather_weights")
    offs = [0]
    for n in big_sizes:
        offs.append(offs[-1] + n // LANES)

    def unpack(t, shape):
        return gathered[:, offs[t]:offs[t + 1], :].reshape((N_DEV,) + shape)

    def cols(t, shape):
        return unpack(t, shape).transpose(1, 0, 2).reshape(shape[0], N_DEV * shape[1])

    w_in_k = _w_in_to_kernel_layout(cols(0, big[0].shape))
    w_uq_k = _uq_to_kernel_layout(cols(1, big[1].shape))
    w_ukv_k = _ukv_to_kernel_layout(cols(2, big[2].shape))
    w_a_f = cols(3, big[3].shape)
    w_b_f = cols(4, big[4].shape)
    w_out_f = unpack(5, big[5].shape).reshape(D, D)

    c_all = _all_gather(c.reshape(8, LANES), "gather_c").reshape(N_DEV, D)
    mod_cols = _mm(c_all, w_ada[0], name="ada_mod")
    mod_all = _all_gather(mod_cols, "gather_mod")
    mod = lax.dynamic_index_in_dim(mod_all, me, axis=1, keepdims=False).reshape(1, 3 * D)
    mod_shift, mod_scale, mod_gate = mod[:, :D], mod[:, D:2 * D], mod[:, 2 * D:]
    b_shift, b_scale, b_gate = b_ada[:, :D], b_ada[:, D:2 * D], b_ada[:, 2 * D:]
    g1 = norm_gain
    gq, gkv = q_norm_gain, kv_norm_gain
    gf = final_norm_gain.reshape(1, D)

    def f_h(x_, g1_, ms, bs, msc, bsc):
        xn = x_ * _rms(x_)
        return (xn * g1_ * (1.0 + (msc + bsc)) + (ms + bs),), ()

    (h,) = _rowwise(f_h, [x2d], [g1, mod_shift, b_shift, mod_scale, b_scale], [(D, BF16)], name="ada_norm")
    proj = _mm(h, w_in_k, name="proj_in")

    o_a, sb_tails = _sb_forward(proj, s_len)

    def f_lat(cq, ckv, gq_, gkv_):
        return (cq * _rms(cq) * gq_, ckv * _rms(ckv) * gkv_), ()

    cq_n, ckv_n = _rowwise(f_lat, [(proj, O_CQ // Q_RANK, Q_RANK), (proj, O_CKV // KV_RANK, KV_RANK)], [gq, gkv],
                           [(Q_RANK, BF16), (KV_RANK, BF16)], name="latent_norm")
    q_mla = _mm(cq_n, w_uq_k, name="q_up")
    kv = _mm(ckv_n, w_ukv_k, name="kv_up")

    inv_freq = ROPE_BASE ** (-jnp.arange(0, ROPE, 2, dtype=F32) / ROPE)
    inv_freq_t = jnp.tile(inv_freq, HEADS).reshape(1, LANES)
    pos_col = positions.reshape(s_len, 1).astype(F32)

    def f_rope(pos, q1, q2, k1, k2, freq):
        ang = pos * freq
        cs, sn = jnp.cos(ang), jnp.sin(ang)
        return (jnp.concatenate([q1 * cs - q2 * sn, q1 * sn + q2 * cs], axis=1),
                jnp.concatenate([k1 * cs - k2 * sn, k1 * sn + k2 * cs], axis=1), cs, sn), ()

    q_pe, k_pe, cos_t, sin_t = _rowwise(
        f_rope, [pos_col, (q_mla, 4, LANES), (q_mla, 5, LANES),
                 (proj, O_KROT // LANES, LANES), (proj, O_KROT // LANES + 1, LANES)], [inv_freq_t],
        [(2 * LANES, F32), (2 * LANES, F32), (LANES, F32), (LANES, F32)], name="rope")

    o_b, lse = _mla_forward(q_mla, q_pe, kv, k_pe, s_len)

    def f_gate(oa, za, ob, zb):
        return (oa * _silu(za), ob * _silu(zb)), ()

    ya_in, yb_in = _rowwise(f_gate, [o_a, (proj, O_SBZ // SB_W, SB_W), o_b, (proj, O_MLAZ // MLA_W, MLA_W)], [],
                            [(SB_W, BF16), (MLA_W, BF16)], name="branch_gate")
    y_a = _mm(ya_in, w_a_f, name="branch_a")
    y_b = _mm(yb_in, w_b_f, name="branch_b")

    def f_merge(ga, gb, ya, yb):
        return (_sigmoid(ga) * ya + _sigmoid(gb) * yb,), ()

    (merged,) = _rowwise(f_merge, [(proj, O_GA // D, D), (proj, O_GB // D, D), y_a, y_b], [], [(D, BF16)], name="merge")
    out = _mm(merged, w_out_f, name="out_proj")

    def f_loss(x_, out_, t_, mg, bg, gf_):
        gate = mg + bg
        x2 = x_ + gate * out_
        r2 = _rms(x2)
        xn2 = x2 * r2
        err = xn2 * gf_ - t_
        loss = jnp.full((1, LANES), 0.5 / D, F32) * jnp.sum(err * err)
        dy = err * (1.0 / D)
        dx2 = _rms_bwd(xn2, r2, dy * gf_)
        return (dx2, dx2 * gate), (loss, _colsum(dy * xn2), _colsum(dx2 * out_))

    dx2, d_out, loss_part, d_gf, d_gate = _rowwise(
        f_loss, [x2d, out, tgt], [mod_gate, b_gate, gf], [(D, F32), (D, BF16)], [LANES, D, D], name="loss_head")

    d_merged = _mm(d_out, w_out_f, tb=True, name="d_merged")
    dw_out = _mm(merged, d_out, ta=True, name="dw_out")

    def f_dmerge(dm, ga, gb, ya, yb):
        sa, sb = _sigmoid(ga), _sigmoid(gb)
        return (dm * sa, dm * sb, dm * ya * sa * (1.0 - sa), dm * yb * sb * (1.0 - sb)), ()

    d_ya, d_yb, d_ga, d_gb = _rowwise(f_dmerge, [d_merged, (proj, O_GA // D, D), (proj, O_GB // D, D), y_a, y_b], [],
                                      [(D, BF16)] * 4, name="d_merge")
    dw_a = _mm(ya_in, d_ya, ta=True, name="dw_branch_a")
    dw_b = _mm(yb_in, d_yb, ta=True, name="dw_branch_b")
    d_ya_in = _mm(d_ya, w_a_f, tb=True, name="d_branch_a")
    d_yb_in = _mm(d_yb, w_b_f, tb=True, name="d_branch_b")

    def f_dgate(da, oa, za, db, ob, zb):
        return (da * _silu(za), da * oa * _silu_grad(za), db * _silu(zb), db * ob * _silu_grad(zb)), ()

    d_oa, d_sbz, d_ob, d_mlaz = _rowwise(
        f_dgate, [d_ya_in, o_a, (proj, O_SBZ // SB_W, SB_W), d_yb_in, o_b, (proj, O_MLAZ // MLA_W, MLA_W)], [],
        [(SB_W, F32), (SB_W, BF16), (MLA_W, F32), (MLA_W, BF16)], name="d_branch_gate")

    d_sbq, d_sbk, d_sbv = _sb_backward(proj, sb_tails, d_oa, s_len)
    dqn, dqpe4, dkn, dkpe4, dv_b = _mla_backward(q_mla, q_pe, kv, k_pe, o_b, lse, d_ob, s_len)

    def f_drope(dqn_, dkn_, dv_, q0, q1, q2, q3, k0, k1, k2, k3, cs, sn):
        dq = (q0 + q1) + (q2 + q3)
        dk = (k0 + k1) + (k2 + k3)
        dq1, dq2 = dq[:, :LANES], dq[:, LANES:]
        dk1, dk2 = dk[:, :LANES], dk[:, LANES:]
        return (jnp.concatenate([dqn_, dq1 * cs + dq2 * sn, dq2 * cs - dq1 * sn], axis=1),
                jnp.concatenate([dkn_, dv_], axis=1),
                jnp.concatenate([dk1 * cs + dk2 * sn, dk2 * cs - dk1 * sn], axis=1)), ()

    pe_rows = [(dqpe4, 0, 2 * LANES, t * s_len) for t in range(4)] + [(dkpe4, 0, 2 * LANES, t * s_len) for t in range(4)]
    dq_k, dkv_k, d_krot = _rowwise(f_drope, [dqn, dkn, dv_b] + pe_rows + [cos_t, sin_t], [],
                                   [(HEADS * QK_DIM, BF16), (2 * MLA_W, BF16), (2 * LANES, BF16)], name="d_rope")
    dw_uq_k = _mm(cq_n, dq_k, ta=True, name="dw_uq")
    dw_ukv_k = _mm(ckv_n, dkv_k, ta=True, name="dw_ukv")
    d_cqn = _mm(dq_k, w_uq_k, tb=True, name="d_cq_norm")
    d_ckvn = _mm(dkv_k, w_ukv_k, tb=True, name="d_ckv_norm")

    def f_dlat(cq, dcqn, ckv, dckvn, gq_, gkv_):
        rq, rkv = _rms(cq), _rms(ckv)
        cqn, ckvn = cq * rq, ckv * rkv
        return ((_rms_bwd(cqn, rq, dcqn * gq_), _rms_bwd(ckvn, rkv, dckvn * gkv_)),
                (_colsum(dcqn * cqn), _colsum(dckvn * ckvn)))

    d_cq, d_ckv, d_gq, d_gkv = _rowwise(
        f_dlat, [(proj, O_CQ // Q_RANK, Q_RANK), d_cqn, (proj, O_CKV // KV_RANK, KV_RANK), d_ckvn], [gq, gkv],
        [(Q_RANK, BF16), (KV_RANK, BF16)], [Q_RANK, KV_RANK], name="d_latent_norm")

    d_proj = jnp.concatenate([d_ga, d_gb, d_sbq.astype(BF16), d_sbk.astype(BF16), d_sbv.astype(BF16), d_sbz, d_mlaz,
                              d_cq, jnp.zeros((s_len, LANES), BF16), d_ckv, d_krot], axis=1)
    dw_in_k = _mm(h, d_proj, ta=True, name="dw_in")
    dw_krot_t = _mm(d_krot, h, ta=True, name="dw_krot")
    dh = _mm(d_proj, w_in_k, tb=True, name="d_h")

    def f_dx(x_, dh_, dx2_, g1_, msc, bsc):
        r = _rms(x_)
        xn = x_ * r
        dn1 = dh_ * (1.0 + (msc + bsc))
        return ((dx2_ + _rms_bwd(xn, r, dn1 * g1_),),
                (_colsum(dh_), _colsum(dh_ * (xn * g1_)), _colsum(dn1 * xn)))

    grad_x2d, d_shift, d_scale, d_g1 = _rowwise(f_dx, [x2d, dh, dx2], [g1, mod_scale, b_scale], [(D, F32)],
                                                [D, D, D], name="d_ada_norm")

    def krot_body(t_ref, o_ref):
        half = ROPE // 2
        for part in range(2):
            acc = t_ref[part * LANES:part * LANES + half, :]
            for hh in range(1, HEADS):
                acc = acc + t_ref[part * LANES + hh * half:part * LANES + (hh + 1) * half, :]
            o_ref[part * half:(part + 1) * half, :] = acc

    dw_krot = pl.pallas_call(krot_body, name="dw_krot_sum", out_shape=jax.ShapeDtypeStruct((ROPE, D), F32))(dw_krot_t).T

    dw_in_full = jnp.concatenate([
        dw_in_k[:, O_SBQ:O_SBQ + 2048], dw_in_k[:, O_CQ:O_CQ + Q_RANK], dw_in_k[:, O_CKV:O_CKV + KV_RANK], dw_krot,
        dw_in_k[:, O_MLAZ:O_MLAZ + MLA_W], dw_in_k[:, O_GA:O_GA + 2 * D]], axis=1)
    dw_uq_full = _uq_from_kernel_layout(dw_uq_k)
    dw_ukv_full = _ukv_from_kernel_layout(dw_ukv_k)

    def col_blocks(g):
        kdim, n8 = g.shape
        return g.astype(BF16).reshape(kdim, N_DEV, n8 // N_DEV).transpose(1, 0, 2).reshape(N_DEV, -1, LANES)

    g_blocks = jnp.concatenate([col_blocks(dw_in_full), col_blocks(dw_uq_full), col_blocks(dw_ukv_full),
                                col_blocks(dw_a), col_blocks(dw_b),
                                dw_out.astype(BF16).reshape(N_DEV, -1, LANES)], axis=1)
    g_recv = _exchange(g_blocks, "exchange_grads")
    g_sum = _sum_blocks(g_recv, "sum_grads")
    g_big = [g_sum[offs[t]:offs[t + 1]].reshape(big[t].shape) for t in range(6)]

    small = jnp.concatenate([d_shift, d_scale, d_gate, d_g1, d_gq, d_gkv, d_gf], axis=1)
    n_small = small.shape[1]
    pad = (-n_small) % (8 * LANES)
    small = jnp.pad(small, ((0, 0), (0, pad))).reshape(-1, LANES)
    small_all = _all_gather(small, "gather_small")
    small_sum = _sum_blocks(small_all, "sum_small").reshape(1, -1)
    g_b_ada = small_sum[:, :3 * D]
    g_g1 = small_sum[:, 3 * D:4 * D]
    g_gq = small_sum[:, 4 * D:4 * D + Q_RANK]
    g_gkv = small_sum[:, 4 * D + Q_RANK:4 * D + Q_RANK + KV_RANK]
    g_gf = small_sum[:, 4 * D + Q_RANK + KV_RANK:4 * D + Q_RANK + KV_RANK + D]

    dmod_all = small_all.reshape(N_DEV, -1)[:, :3 * D]
    dmod_cols = lax.dynamic_slice_in_dim(dmod_all, me * (3 * D // N_DEV), 3 * D // N_DEV, axis=1)
    g_w_ada = _mm(c_all, dmod_cols, ta=True, name="dw_ada")

    loss = lax.psum(loss_part[0, 0], ("x", "y", "c"))

    names = ["w_ada", "b_ada", "norm_gain", "w_in", "q_norm_gain", "w_uq", "kv_norm_gain", "w_ukv",
             "w_branch_a", "w_branch_b", "w_out", "final_norm_gain"]
    weights = dict(w_ada=w_ada, b_ada=b_ada, norm_gain=norm_gain, w_in=w_in, q_norm_gain=q_norm_gain, w_uq=w_uq,
                   kv_norm_gain=kv_norm_gain, w_ukv=w_ukv, w_branch_a=w_branch_a, w_branch_b=w_branch_b, w_out=w_out,
                   final_norm_gain=final_norm_gain)
    moms = dict(w_ada=m_w_ada, b_ada=m_b_ada, norm_gain=m_norm_gain, w_in=m_w_in, q_norm_gain=m_q_norm_gain,
                w_uq=m_w_uq, kv_norm_gain=m_kv_norm_gain, w_ukv=m_w_ukv, w_branch_a=m_w_branch_a,
                w_branch_b=m_w_branch_b, w_out=m_w_out, final_norm_gain=m_final_norm_gain)
    vels = dict(w_ada=v_w_ada, b_ada=v_b_ada, norm_gain=v_norm_gain, w_in=v_w_in, q_norm_gain=v_q_norm_gain,
                w_uq=v_w_uq, kv_norm_gain=v_kv_norm_gain, w_ukv=v_w_ukv, w_branch_a=v_w_branch_a,
                w_branch_b=v_w_branch_b, w_out=v_w_out, final_norm_gain=v_final_norm_gain)
    grads2d = dict(w_ada=g_w_ada, b_ada=g_b_ada, norm_gain=g_g1, w_in=g_big[0], q_norm_gain=g_gq, w_uq=g_big[1],
                   kv_norm_gain=g_gkv, w_ukv=g_big[2], w_branch_a=g_big[3], w_branch_b=g_big[4], w_out=g_big[5],
                   final_norm_gain=g_gf)

    grads, deltas, new_m, new_v = [], [], [], []
    for n in names:
        w = weights[n]
        shape2d = grads2d[n].shape
        d_, m_, v_ = _adamw_call(w.reshape(shape2d), grads2d[n], moms[n].reshape(shape2d), vels[n].reshape(shape2d),
                                 "adamw_" + n)
        grads.append(grads2d[n].reshape(w.shape))
        deltas.append(d_.reshape(w.shape))
        new_m.append(m_.reshape(w.shape))
        new_v.append(v_.reshape(w.shape))

    return (loss, grad_x2d.reshape(x.shape), *grads, *deltas, *new_m, *new_v)
```

```python
import functools
import math

import jax
import jax.numpy as jnp
from jax import lax
from jax.experimental import pallas as pl
from jax.experimental.pallas import tpu as pltpu

F32 = jnp.float32
BF16 = jnp.bfloat16
MXU_DTYPE = jnp.bfloat16

N_DEV = 8
D = 1024
HEADS = 8
HD = 64
SB_W = 512
MLA_W = 512
Q_RANK = 384
KV_RANK = 256
ROPE = 32
NOPE = 64
QK_DIM = NOPE + ROPE
EPS = 1e-6
ROPE_BASE = 10000.0

ADAM_LR = 0.001
ADAM_B1 = 0.9
ADAM_B2 = 0.999
ADAM_EPS = 1e-08
ADAM_WD = 0.01
ADAM_STEP = 10

LANES = 128
VMEM_LIMIT = 48 * 1024 * 1024

O_GA, O_GB = 0, 1024
O_SBQ, O_SBK, O_SBV, O_SBZ = 2048, 2560, 3072, 3584
O_MLAZ = 4096
O_CQ = 4608
O_CKV = 5120
O_KROT = 5376
IN_PAD = 5632

BQ = 256
BK = 256


def _cparams(*sem):
    return pltpu.CompilerParams(dimension_semantics=sem, vmem_limit_bytes=VMEM_LIMIT)


def _tile_of(n, cap=512):
    if n <= cap:
        return n
    for t in (1024, 768, 512, 384, 256, 128):
        if t <= cap and n % t == 0:
            return t
    raise ValueError(n)


def _rowwise(fn, rows, vecs, outs, reds=(), *, name, tile=256):
    norm = []
    for r in rows:
        if isinstance(r, tuple):
            arr, cb, w = r[:3]
            ro = r[3] if len(r) > 3 else 0
        else:
            arr, cb, w, ro = r, 0, r.shape[1], 0
        norm.append((arr, cb, w, ro))
    s_len = norm[0][0].shape[0]
    tile = min(tile, s_len)
    assert s_len % tile == 0
    n_row, n_vec, n_out, n_red = len(norm), len(vecs), len(outs), len(reds)

    def body(*refs):
        step = pl.program_id(0)
        row_refs = refs[:n_row]
        vec_refs = refs[n_row:n_row + n_vec]
        out_refs = refs[n_row + n_vec:n_row + n_vec + n_out]
        red_refs = refs[n_row + n_vec + n_out:]
        row_res, red_res = fn(*[r[...] for r in row_refs], *[v[...] for v in vec_refs])
        for o, val in zip(out_refs, row_res):
            o[...] = val.astype(o.dtype)
        if n_red:
            @pl.when(step == 0)
            def _():
                for r in red_refs:
                    r[...] = jnp.zeros(r.shape, r.dtype)
            for r, val in zip(red_refs, red_res):
                r[...] += val

    in_specs = []
    for arr, cb, w, ro in norm:
        in_specs.append(pl.BlockSpec((tile, w), functools.partial(lambda i, cb, rb: (i + rb, cb), cb=cb, rb=ro // tile)))
        assert ro % tile == 0
    for v in vecs:
        in_specs.append(pl.BlockSpec(v.shape, lambda i: (0, 0)))
    out_shape = [jax.ShapeDtypeStruct((s_len, w), dt) for w, dt in outs]
    out_specs = [pl.BlockSpec((tile, w), lambda i: (i, 0)) for w, _ in outs]
    out_shape += [jax.ShapeDtypeStruct((1, w), F32) for w in reds]
    out_specs += [pl.BlockSpec((1, w), lambda i: (0, 0)) for w in reds]
    res = pl.pallas_call(
        body, name=name, grid=(s_len // tile,), in_specs=in_specs, out_specs=out_specs, out_shape=out_shape,
        compiler_params=_cparams("arbitrary" if n_red else "parallel"),
    )(*[a for a, _, _, _ in norm], *vecs)
    return res


def _mm(a, b, *, ta=False, tb=False, out_dtype=F32, name):
    m, k = (a.shape[1], a.shape[0]) if ta else a.shape
    n = b.shape[0] if tb else b.shape[1]
    assert (b.shape[1] if tb else b.shape[0]) == k
    tm, tn, tk = _tile_of(m, 1024), _tile_of(n, 1024 if n <= 1024 else 512), _tile_of(k, 1024)
    nk = k // tk
    dims = (((0 if ta else 1,), (1 if tb else 0,)), ((), ()))

    def body(a_ref, b_ref, o_ref, acc_ref):
        kk = pl.program_id(2)

        @pl.when(kk == 0)
        def _():
            acc_ref[...] = jnp.zeros(acc_ref.shape, F32)

        acc_ref[...] += lax.dot_general(a_ref[...].astype(MXU_DTYPE), b_ref[...].astype(MXU_DTYPE), dims,
                                        preferred_element_type=F32)

        @pl.when(kk == nk - 1)
        def _():
            o_ref[...] = acc_ref[...].astype(o_ref.dtype)

    a_spec = pl.BlockSpec((tk, tm), lambda i, j, kk: (kk, i)) if ta else pl.BlockSpec((tm, tk), lambda i, j, kk: (i, kk))
    b_spec = pl.BlockSpec((tn, tk), lambda i, j, kk: (j, kk)) if tb else pl.BlockSpec((tk, tn), lambda i, j, kk: (kk, j))
    return pl.pallas_call(
        body, name=name, grid=(m // tm, n // tn, nk), in_specs=[a_spec, b_spec],
        out_specs=pl.BlockSpec((tm, tn), lambda i, j, kk: (i, j)),
        out_shape=jax.ShapeDtypeStruct((m, n), out_dtype),
        scratch_shapes=[pltpu.VMEM((tm, tn), F32)],
        compiler_params=_cparams("parallel", "parallel", "arbitrary"),
    )(a, b)


_NT = (((1,), (1,)), ((), ()))
_TN = (((0,), (0,)), ((), ()))


def _dot(a, b):
    return jnp.dot(a, b, preferred_element_type=F32)


def _dot_nt(a, b):
    return lax.dot_general(a, b, _NT, preferred_element_type=F32)


def _dot_tn(a, b):
    return lax.dot_general(a, b, _TN, preferred_element_type=F32)


def _split_dot(x, tri):
    hi = x.astype(MXU_DTYPE)
    lo = (x - hi.astype(F32)).astype(MXU_DTYPE)
    return _dot(hi, tri) + _dot(lo, tri)


def _neg_softplus(z):
    u = jnp.exp2(jnp.abs(z) * (-1.0 / math.log(2.0)))
    return -jnp.maximum(z, 0.0) - jnp.log(1.0 + u)


def _walk_blocks(step, st, n, descending):
    pairs = n // 2

    def two(t, s):
        j = (n - 1 - 2 * t) if descending else 2 * t
        return step([j, j - 1 if descending else j + 1], s)

    st = lax.fori_loop(0, pairs, two, st)
    last = 0 if descending else n - 1
    return lax.fori_loop(0, n - 2 * pairs, lambda _, s: step([last], s), st)


def _chains(js):
    return [(h, t) for t in range(len(js)) for h in range(2)]


def _rowsum(x):
    return jnp.sum(x, axis=1, keepdims=True)


def _attn_consts():
    row = lax.broadcasted_iota(jnp.int32, (BQ, BK), 0)
    col = lax.broadcasted_iota(jnp.int32, (BQ, BK), 1)
    trow = lax.broadcasted_iota(jnp.int32, (BK, BK), 0)
    tcol = lax.broadcasted_iota(jnp.int32, (BK, BK), 1)
    lane = lax.broadcasted_iota(jnp.int32, (BQ, LANES), 1)
    klane = lax.broadcasted_iota(jnp.int32, (BK, LANES), 1)
    return row, col, trow, tcol, lane, klane


def _diag_block(i):
    jd = (i * BQ) // BK
    return jd, i * BQ - jd * BK


def _key_slice(j):
    return pl.ds(pl.multiple_of(j * BK, BK), BK)


def _sb_forward(proj, s_len):
    nq = s_len // BQ
    assert s_len // BK <= HD
    qc, kc, vc = O_SBQ // LANES, O_SBK // LANES, O_SBV // LANES

    def body(q_ref, k_ref, v_ref, o_ref, tails_ref):
        i = pl.program_id(1)
        jd, off = _diag_block(i)
        row, col, trow, tcol, lane, klane = _attn_consts()
        strict = col < row + off
        tri = (trow >= tcol).astype(MXU_DTYPE)
        q = q_ref[...] * 0.125
        qh = [jnp.where(lane < HD, q, 0.0).astype(MXU_DTYPE), jnp.where(lane >= HD, q, 0.0).astype(MXU_DTYPE)]

        km = [klane < HD, klane >= HD]

        def step(js, st, diag):
            carry, acc, tail = st
            chains = _chains(js)
            kj = [k_ref[_key_slice(j), :].astype(MXU_DTYPE) for j in js]
            vj = [v_ref[_key_slice(j), :].astype(MXU_DTYPE) for j in js]
            z = {(h, t): _dot_nt(qh[h], kj[t]) for h, t in chains}
            run = list(carry)
            suf, carry_in = {}, {}
            for h, t in chains:
                lom = _neg_softplus(z[h, t])
                if diag:
                    lom = jnp.where(strict, lom, 0.0)
                suf[h, t] = _split_dot(lom, tri)
                carry_in[h, t] = run[h]
                run[h] = run[h] + _rowsum(lom)
            for h, t in chains:
                a = jnp.exp(z[h, t] + suf[h, t] + carry_in[h, t])
                if diag:
                    a = jnp.where(strict, a, 0.0)
                acc = acc + _dot(a.astype(MXU_DTYPE), jnp.where(km[h], vj[t], 0))
                tail = jnp.where(lane == h * HD + js[t], carry_in[h, t], tail)
            return tuple(run), acc, tail

        zero = jnp.zeros((BQ, LANES), F32)
        st = step([jd], ((jnp.zeros((BQ, 1), F32),) * 2, zero, zero), True)
        st = _walk_blocks(lambda js, s: step(js, s, False), st, jd, True)
        o_ref[...] = st[1]
        tails_ref[...] = st[2]

    blk = pl.BlockSpec((BQ, LANES), lambda p, i: (i, p))
    out = jax.ShapeDtypeStruct((s_len, SB_W), F32)
    return pl.pallas_call(
        body, name="sb_fwd", grid=(HEADS // 2, nq),
        in_specs=[pl.BlockSpec((BQ, LANES), lambda p, i: (i, qc + p)),
                  pl.BlockSpec((s_len, LANES), lambda p, i: (0, kc + p)),
                  pl.BlockSpec((s_len, LANES), lambda p, i: (0, vc + p))],
        out_specs=[blk, blk], out_shape=[out, out],
        compiler_params=_cparams("parallel", "parallel"),
    )(proj, proj, proj)


def _sb_backward(proj, tails, do, s_len):
    nq = s_len // BQ
    qc, kc, vc = O_SBQ // LANES, O_SBK // LANES, O_SBV // LANES

    def body(q_ref, k_ref, v_ref, tails_ref, do_ref, dq_ref, dk_ref, dv_ref):
        i = pl.program_id(1)
        jd, off = _diag_block(i)

        @pl.when(i == 0)
        def _():
            dk_ref[...] = jnp.zeros(dk_ref.shape, F32)
            dv_ref[...] = jnp.zeros(dv_ref.shape, F32)

        row, col, trow, tcol, lane, klane = _attn_consts()
        strict = col < row + off
        tri = (trow >= tcol).astype(MXU_DTYPE)
        tri_p = (trow <= tcol).astype(MXU_DTYPE)
        q = q_ref[...] * 0.125
        tails_blk = tails_ref[...]
        do_blk = do_ref[...]
        hm = [lane < HD, lane >= HD]
        km = [klane < HD, klane >= HD]
        qh = [jnp.where(m, q, 0.0).astype(MXU_DTYPE) for m in hm]
        doh = [jnp.where(m, do_blk, 0.0).astype(MXU_DTYPE) for m in hm]

        def step(js, st, diag):
            before, dq = st
            chains = _chains(js)
            kj = [k_ref[_key_slice(j), :].astype(MXU_DTYPE) for j in js]
            vj = [v_ref[_key_slice(j), :].astype(MXU_DTYPE) for j in js]
            z = {(h, t): _dot_nt(qh[h], kj[t]) for h, t in chains}
            da = {(h, t): _dot_nt(doh[h], vj[t]) for h, t in chains}
            suf, sig = {}, {}
            for h, t in chains:
                lom = _neg_softplus(z[h, t])
                if diag:
                    lom = jnp.where(strict, lom, 0.0)
                suf[h, t] = _split_dot(lom, tri)
                sig[h, t] = jnp.exp(z[h, t] + lom)
            run = list(before)
            dl, pre, before_in = {}, {}, {}
            dk_add, dv_add = [None] * len(js), [None] * len(js)
            for h, t in chains:
                tail = _rowsum(jnp.where(lane == h * HD + js[t], tails_blk, 0.0))
                a = jnp.exp(z[h, t] + suf[h, t] + tail)
                if diag:
                    a = jnp.where(strict, a, 0.0)
                dl[h, t] = da[h, t] * a
                pre[h, t] = _dot(dl[h, t].astype(MXU_DTYPE), tri_p)
                dv_h = _dot_tn(a.astype(MXU_DTYPE), doh[h])
                dv_add[t] = dv_h if dv_add[t] is None else dv_add[t] + dv_h
                before_in[h, t] = run[h]
                run[h] = run[h] + _rowsum(dl[h, t])
            for h, t in chains:
                upto = before_in[h, t] + pre[h, t]
                dz = dl[h, t] - sig[h, t] * upto
                if diag:
                    dz = jnp.where(strict, dz, 0.0)
                dzb = dz.astype(MXU_DTYPE)
                dq = dq + _dot(dzb, jnp.where(km[h], kj[t], 0))
                dk_h = _dot_tn(dzb, qh[h])
                dk_add[t] = dk_h if dk_add[t] is None else dk_add[t] + dk_h
            for t, j in enumerate(js):
                dk_ref[_key_slice(j), :] += dk_add[t]
                dv_ref[_key_slice(j), :] += dv_add[t]
            return tuple(run), dq

        st = ((jnp.zeros((BQ, 1), F32),) * 2, jnp.zeros((BQ, LANES), F32))
        st = _walk_blocks(lambda js, s: step(js, s, False), st, jd, False)
        st = step([jd], st, True)
        dq_ref[...] = st[1] * 0.125

    blk = pl.BlockSpec((BQ, LANES), lambda p, i: (i, p))
    full = pl.BlockSpec((s_len, LANES), lambda p, i: (0, p))
    out = jax.ShapeDtypeStruct((s_len, SB_W), F32)
    return pl.pallas_call(
        body, name="sb_bwd", grid=(HEADS // 2, nq),
        in_specs=[pl.BlockSpec((BQ, LANES), lambda p, i: (i, qc + p)),
                  pl.BlockSpec((s_len, LANES), lambda p, i: (0, kc + p)),
                  pl.BlockSpec((s_len, LANES), lambda p, i: (0, vc + p)),
                  blk, blk],
        out_specs=[blk, full, full], out_shape=[out, out, out],
        compiler_params=_cparams("arbitrary", "arbitrary"),
    )(proj, proj, proj, tails, do)


def _pe_mask(rows, head):
    lane = lax.broadcasted_iota(jnp.int32, (rows, 2 * LANES), 1)
    return ((lane % LANES) // (ROPE // 2)) == head


def _mla_forward(q_mla, q_pe, kv, k_pe, s_len):
    nq = s_len // BQ
    scale = 1.0 / math.sqrt(QK_DIM)

    def body(qn_ref, qpe_ref, kn_ref, kpe_ref, v_ref, o_ref, lse_ref):
        p = pl.program_id(0)
        i = pl.program_id(1)
        jd, off = _diag_block(i)
        row, col, trow, tcol, lane, klane = _attn_consts()
        causal = col <= row + off
        qn = qn_ref[...]
        qpe = qpe_ref[...]
        hm = [lane < HD, lane >= HD]
        km = [klane < HD, klane >= HD]
        qnh = [jnp.where(m, qn, 0.0).astype(MXU_DTYPE) for m in hm]
        qph = [jnp.where(_pe_mask(BQ, 2 * p + h), qpe, 0.0).astype(MXU_DTYPE) for h in range(2)]

        def step(js, st, diag):
            m_run, l_run, acc = st
            chains = _chains(js)
            knj = [kn_ref[_key_slice(j), :].astype(MXU_DTYPE) for j in js]
            kpj = [kpe_ref[_key_slice(j), :].astype(MXU_DTYPE) for j in js]
            vj = [v_ref[_key_slice(j), :].astype(MXU_DTYPE) for j in js]
            s = {}
            for h, t in chains:
                s[h, t] = (_dot_nt(qnh[h], knj[t]) + _dot_nt(qph[h], kpj[t])) * scale
                if diag:
                    s[h, t] = jnp.where(causal, s[h, t], -jnp.inf)
            m_new, alpha, l_new = [], [], []
            for h in range(2):
                top = m_run[h]
                for t in range(len(js)):
                    top = jnp.maximum(top, jnp.max(s[h, t], axis=1, keepdims=True))
                m_new.append(top)
                alpha.append(jnp.exp(m_run[h] - top))
                l_new.append(alpha[h] * l_run[h])
            add = None
            for h, t in chains:
                pr = jnp.exp(s[h, t] - m_new[h])
                l_new[h] = l_new[h] + _rowsum(pr)
                part = _dot(pr.astype(MXU_DTYPE), jnp.where(km[h], vj[t], 0))
                add = part if add is None else add + part
            acc = jnp.where(hm[0], alpha[0], alpha[1]) * acc + add
            return tuple(m_new), tuple(l_new), acc

        st = ((jnp.full((BQ, 1), -1e30, F32),) * 2, (jnp.zeros((BQ, 1), F32),) * 2, jnp.zeros((BQ, LANES), F32))
        st = step([jd], st, True)
        m_run, l_run, acc = _walk_blocks(lambda js, s: step(js, s, False), st, jd, True)
        o_ref[...] = acc / jnp.where(hm[0], l_run[0], l_run[1])
        lse_ref[...] = jnp.where(hm[0], m_run[0] + jnp.log(l_run[0]), m_run[1] + jnp.log(l_run[1]))

    blk = pl.BlockSpec((BQ, LANES), lambda p, i: (i, p))
    out = jax.ShapeDtypeStruct((s_len, MLA_W), F32)
    return pl.pallas_call(
        body, name="mla_fwd", grid=(HEADS // 2, nq),
        in_specs=[blk,
                  pl.BlockSpec((BQ, 2 * LANES), lambda p, i: (i, 0)),
                  pl.BlockSpec((s_len, LANES), lambda p, i: (0, p)),
                  pl.BlockSpec((s_len, 2 * LANES), lambda p, i: (0, 0)),
                  pl.BlockSpec((s_len, LANES), lambda p, i: (0, MLA_W // LANES + p))],
        out_specs=[blk, blk], out_shape=[out, out],
        compiler_params=_cparams("parallel", "parallel"),
    )(q_mla, q_pe, kv, k_pe, kv)


def _mla_backward(q_mla, q_pe, kv, k_pe, o, lse, do, s_len):
    nq = s_len // BQ
    scale = 1.0 / math.sqrt(QK_DIM)

    def body(qn_ref, qpe_ref, kn_ref, kpe_ref, v_ref, o_ref, lse_ref, do_ref,
             dqn_ref, dqpe_ref, dkn_ref, dkpe_ref, dv_ref):
        p = pl.program_id(0)
        i = pl.program_id(1)

        @pl.when(i == 0)
        def _():
            dkn_ref[...] = jnp.zeros(dkn_ref.shape, F32)
            dkpe_ref[...] = jnp.zeros(dkpe_ref.shape, F32)
            dv_ref[...] = jnp.zeros(dv_ref.shape, F32)

        jd, off = _diag_block(i)
        row, col, trow, tcol, lane, klane = _attn_consts()
        causal = col <= row + off
        qn = qn_ref[...]
        qpe = qpe_ref[...]
        o_blk = o_ref[...]
        do_blk = do_ref[...]
        lse_blk = lse_ref[...]
        hm = [lane < HD, lane >= HD]
        km = [klane < HD, klane >= HD]
        kpm = [_pe_mask(BK, 2 * p + h) for h in range(2)]
        qnh = [jnp.where(m, qn, 0.0).astype(MXU_DTYPE) for m in hm]
        qph = [jnp.where(_pe_mask(BQ, 2 * p + h), qpe, 0.0).astype(MXU_DTYPE) for h in range(2)]
        doh_f = [jnp.where(m, do_blk, 0.0) for m in hm]
        doh = [d.astype(MXU_DTYPE) for d in doh_f]
        delta = [jnp.sum(d * o_blk, axis=1, keepdims=True) for d in doh_f]
        lse_h = [jnp.sum(jnp.where(lane == h * HD, lse_blk, 0.0), axis=1, keepdims=True) for h in range(2)]

        def step(js, st, diag):
            dqn, dqp = st
            chains = _chains(js)
            knj = [kn_ref[_key_slice(j), :].astype(MXU_DTYPE) for j in js]
            kpj = [kpe_ref[_key_slice(j), :].astype(MXU_DTYPE) for j in js]
            vj = [v_ref[_key_slice(j), :].astype(MXU_DTYPE) for j in js]
            s = {(h, t): _dot_nt(qnh[h], knj[t]) + _dot_nt(qph[h], kpj[t]) for h, t in chains}
            dp = {(h, t): _dot_nt(doh[h], vj[t]) for h, t in chains}
            adds = [[None] * len(js) for _ in range(3)]

            def accumulate(slot, t, part):
                adds[slot][t] = part if adds[slot][t] is None else adds[slot][t] + part

            for h, t in chains:
                pr = jnp.exp(s[h, t] * scale - lse_h[h])
                if diag:
                    pr = jnp.where(causal, pr, 0.0)
                dsb = (pr * (dp[h, t] - delta[h]) * scale).astype(MXU_DTYPE)
                dqn = dqn + _dot(dsb, jnp.where(km[h], knj[t], 0))
                dqp = dqp + _dot(dsb, jnp.where(kpm[h], kpj[t], 0))
                accumulate(0, t, _dot_tn(dsb, qnh[h]))
                accumulate(1, t, _dot_tn(dsb, qph[h]))
                accumulate(2, t, _dot_tn(pr.astype(MXU_DTYPE), doh[h]))
            for t, j in enumerate(js):
                dkn_ref[_key_slice(j), :] += adds[0][t]
                dkpe_ref[_key_slice(j), :] += adds[1][t]
                dv_ref[_key_slice(j), :] += adds[2][t]
            return dqn, dqp

        st = step([jd], (jnp.zeros((BQ, LANES), F32), jnp.zeros((BQ, 2 * LANES), F32)), True)
        st = _walk_blocks(lambda js, s: step(js, s, False), st, jd, True)
        dqn_ref[...] = st[0]
        dqpe_ref[...] = st[1]

    blk = pl.BlockSpec((BQ, LANES), lambda p, i: (i, p))
    full = pl.BlockSpec((s_len, LANES), lambda p, i: (0, p))
    out = jax.ShapeDtypeStruct((s_len, MLA_W), F32)
    out_pe = jax.ShapeDtypeStruct((4 * s_len, 2 * LANES), F32)
    return pl.pallas_call(
        body, name="mla_bwd", grid=(HEADS // 2, nq),
        in_specs=[blk,
                  pl.BlockSpec((BQ, 2 * LANES), lambda p, i: (i, 0)),
                  pl.BlockSpec((s_len, LANES), lambda p, i: (0, p)),
                  pl.BlockSpec((s_len, 2 * LANES), lambda p, i: (0, 0)),
                  pl.BlockSpec((s_len, LANES), lambda p, i: (0, MLA_W // LANES + p)),
                  blk, blk, blk],
        out_specs=[blk, pl.BlockSpec((BQ, 2 * LANES), lambda p, i: (p * nq + i, 0)), full,
                   pl.BlockSpec((s_len, 2 * LANES), lambda p, i: (p, 0)), full],
        out_shape=[out, out_pe, out, out_pe, out],
        compiler_params=_cparams("arbitrary", "arbitrary"),
    )(q_mla, q_pe, kv, k_pe, kv, o, lse, do)


def _mesh_pos():
    return lax.axis_index("x"), lax.axis_index("y"), lax.axis_index("c")


def _dev_index(px, py, pc):
    return 4 * px + 2 * py + pc


def _all_gather(block, name):
    def body(x_ref, out_ref, send_sems, recv_sems, local_sem):
        x, y, c = _mesh_pos()
        me, sibling = (x, y, c), (x, y, 1 - c)
        chips = [(1 - x, y), (x, 1 - y), (1 - x, 1 - y)]

        def slot(pos):
            return out_ref.at[_dev_index(*pos)]

        def copy(k, blockpos, to, src=None):
            return pltpu.make_async_remote_copy(
                src_ref=slot(blockpos) if src is None else src, dst_ref=slot(blockpos),
                send_sem=send_sems.at[k], recv_sem=recv_sems.at[k],
                device_id=to, device_id_type=pl.DeviceIdType.MESH)

        mine = pltpu.make_async_copy(x_ref, slot(me), local_sem)
        mine.start()
        first = [copy(0, me, sibling, src=x_ref)]
        first += [copy(1 + j, me, (*chip, c), src=x_ref) for j, chip in enumerate(chips)]
        for cp in first:
            cp.start()
        passed = [copy(4 + j, (*chip, c), sibling) for j, chip in enumerate(chips)]
        for j, chip in enumerate(chips):
            copy(1 + j, (*chip, c), me).wait_recv()
            passed[j].start()
        copy(0, sibling, me).wait_recv()
        for j, chip in enumerate(chips):
            copy(4 + j, (*chip, 1 - c), me).wait_recv()
        for cp in first + passed:
            cp.wait_send()
        mine.wait()

    return pl.pallas_call(
        body, name=name,
        out_shape=jax.ShapeDtypeStruct((N_DEV,) + block.shape, block.dtype),
        in_specs=[pl.BlockSpec(memory_space=pl.ANY)], out_specs=pl.BlockSpec(memory_space=pl.ANY),
        scratch_shapes=[pltpu.SemaphoreType.DMA((7,)), pltpu.SemaphoreType.DMA((7,)), pltpu.SemaphoreType.DMA],
    )(block)


def _exchange(blocks, name):
    def body(x_ref, out_ref, send_sems, recv_sems, local_sem):
        x, y, c = _mesh_pos()
        me = _dev_index(x, y, c)
        flips = [(fx, fy, fc) for fx in (0, 1) for fy in (0, 1) for fc in (0, 1)][1:]
        peers = [(1 - x if fx else x, 1 - y if fy else y, 1 - c if fc else c) for fx, fy, fc in flips]
        mine = pltpu.make_async_copy(x_ref.at[me], out_ref.at[me], local_sem)
        mine.start()
        sends = []
        for k, peer in enumerate(peers):
            cp = pltpu.make_async_remote_copy(
                src_ref=x_ref.at[_dev_index(*peer)], dst_ref=out_ref.at[me],
                send_sem=send_sems.at[k], recv_sem=recv_sems.at[k],
                device_id=peer, device_id_type=pl.DeviceIdType.MESH)
            cp.start()
            sends.append(cp)
        for k, peer in enumerate(peers):
            pltpu.make_async_remote_copy(
                src_ref=x_ref.at[me], dst_ref=out_ref.at[_dev_index(*peer)],
                send_sem=send_sems.at[k], recv_sem=recv_sems.at[k],
                device_id=peer, device_id_type=pl.DeviceIdType.MESH).wait_recv()
        for cp in sends:
            cp.wait_send()
        mine.wait()

    return pl.pallas_call(
        body, name=name,
        out_shape=jax.ShapeDtypeStruct(blocks.shape, blocks.dtype),
        in_specs=[pl.BlockSpec(memory_space=pl.ANY)], out_specs=pl.BlockSpec(memory_space=pl.ANY),
        scratch_shapes=[pltpu.SemaphoreType.DMA((7,)), pltpu.SemaphoreType.DMA((7,)), pltpu.SemaphoreType.DMA],
    )(blocks)


def _sum_blocks(parts, name):
    n, r, c = parts.shape
    tile = max(t for t in range(16, min(r, 2048) + 1, 16) if r % t == 0)

    def body(p_ref, o_ref):
        acc = p_ref[0].astype(F32)
        for s in range(1, n):
            acc = acc + p_ref[s].astype(F32)
        o_ref[...] = acc

    return pl.pallas_call(
        body, name=name, grid=(r // tile,),
        in_specs=[pl.BlockSpec((n, tile, c), lambda i: (0, i, 0))],
        out_specs=pl.BlockSpec((tile, c), lambda i: (i, 0)),
        out_shape=jax.ShapeDtypeStruct((r, c), F32),
        compiler_params=_cparams("parallel"),
    )(parts)


def _sigmoid(x):
    return 1.0 / (1.0 + jnp.exp(-x))


def _silu(x):
    return x * _sigmoid(x)


def _silu_grad(x):
    s = _sigmoid(x)
    return s * (1.0 + x * (1.0 - s))


def _colsum(x):
    return jnp.sum(x, axis=0, keepdims=True)


def _rms(x):
    return lax.rsqrt(jnp.mean(x * x, axis=-1, keepdims=True) + EPS)


def _rms_bwd(xn, r, dxn):
    return r * (dxn - xn * jnp.mean(dxn * xn, axis=-1, keepdims=True))


def _adamw(w, g, m, v):
    m = ADAM_B1 * m + (1.0 - ADAM_B1) * g
    v = ADAM_B2 * v + (1.0 - ADAM_B2) * jnp.square(g)
    m_hat = m / (1.0 - ADAM_B1 ** ADAM_STEP)
    v_hat = v / (1.0 - ADAM_B2 ** ADAM_STEP)
    delta = -ADAM_LR * (m_hat / (jnp.sqrt(v_hat) + ADAM_EPS) + ADAM_WD * w)
    return delta, m, v


def _adamw_call(w, g, m, v, name):
    r, c = w.shape
    tile = r if (r <= 256 or r % 256) else 256
    return _rowwise(lambda w_, g_, m_, v_: (_adamw(w_, g_, m_, v_), ()), [w, g, m, v], [],
                    [(c, F32)] * 3, name=name, tile=tile)


def _uq_to_kernel_layout(w):
    lead = w.shape[:-1]
    t = w.reshape(lead + (HEADS, QK_DIM))
    return jnp.concatenate([t[..., :NOPE].reshape(lead + (HEADS * NOPE,)),
                            t[..., NOPE:NOPE + ROPE // 2].reshape(lead + (LANES,)),
                            t[..., NOPE + ROPE // 2:].reshape(lead + (LANES,))], axis=-1)


def _uq_from_kernel_layout(w):
    lead = w.shape[:-1]
    nope = w[..., :HEADS * NOPE].reshape(lead + (HEADS, NOPE))
    r1 = w[..., HEADS * NOPE:HEADS * NOPE + LANES].reshape(lead + (HEADS, ROPE // 2))
    r2 = w[..., HEADS * NOPE + LANES:].reshape(lead + (HEADS, ROPE // 2))
    return jnp.concatenate([nope, r1, r2], axis=-1).reshape(lead + (HEADS * QK_DIM,))


def _ukv_to_kernel_layout(w):
    lead = w.shape[:-1]
    t = w.reshape(lead + (HEADS, NOPE + HD))
    return jnp.concatenate([t[..., :NOPE].reshape(lead + (HEADS * NOPE,)),
                            t[..., NOPE:].reshape(lead + (HEADS * HD,))], axis=-1)


def _ukv_from_kernel_layout(w):
    lead = w.shape[:-1]
    kn = w[..., :HEADS * NOPE].reshape(lead + (HEADS, NOPE))
    vv = w[..., HEADS * NOPE:].reshape(lead + (HEADS, HD))
    return jnp.concatenate([kn, vv], axis=-1).reshape(lead + (HEADS * (NOPE + HD),))


def _w_in_to_kernel_layout(w):
    sb_q, sb_k, sb_v, sb_z = (w[:, 512 * t:512 * (t + 1)] for t in range(4))
    c_q = w[:, 2048:2432]
    c_kv = w[:, 2432:2688]
    k_rot = w[:, 2688:2720]
    mla_z = w[:, 2720:3232]
    g_a = w[:, 3232:4256]
    g_b = w[:, 4256:5280]
    zeros = jnp.zeros((w.shape[0], LANES), w.dtype)
    k1 = jnp.tile(k_rot[:, :ROPE // 2], (1, HEADS))
    k2 = jnp.tile(k_rot[:, ROPE // 2:], (1, HEADS))
    return jnp.concatenate([g_a, g_b, sb_q, sb_k, sb_v, sb_z, mla_z, c_q, zeros, c_kv, k1, k2], axis=1)


def kernel(x, c, positions, w_ada, b_ada, norm_gain, w_in, q_norm_gain, w_uq, kv_norm_gain, w_ukv, w_branch_a, w_branch_b, w_out, final_norm_gain, loss_target, m_w_ada, m_b_ada, m_norm_gain, m_w_in, m_q_norm_gain, m_w_uq, m_kv_norm_gain, m_w_ukv, m_w_branch_a, m_w_branch_b, m_w_out, m_final_norm_gain, v_w_ada, v_b_ada, v_norm_gain, v_w_in, v_q_norm_gain, v_w_uq, v_kv_norm_gain, v_w_ukv, v_w_branch_a, v_w_branch_b, v_w_out, v_final_norm_gain):
    s_len = x.shape[1]
    me = _dev_index(*_mesh_pos())
    x2d = x[0]
    tgt = loss_target[0]

    big = [w_in[0], w_uq[0], w_ukv[0], w_branch_a[0], w_branch_b[0], w_out[0]]
    big_sizes = [int(w.size) for w in big]
    packed = jnp.concatenate([w.astype(BF16).reshape(-1, LANES) for w in big], axis=0)
    gathered = _all_gather(packed, "gather_weights")
    offs = [0]
    for n in big_sizes:
        offs.append(offs[-1] + n // LANES)

    def unpack(t, shape):
        return gathered[:, offs[t]:offs[t + 1], :].reshape((N_DEV,) + shape)

    def cols(t, shape):
        return unpack(t, shape).transpose(1, 0, 2).reshape(shape[0], N_DEV * shape[1])

    w_in_k = _w_in_to_kernel_layout(cols(0, big[0].shape))
    w_uq_k = _uq_to_kernel_layout(cols(1, big[1].shape))
    w_ukv_k = _ukv_to_kernel_layout(cols(2, big[2].shape))
    w_a_f = cols(3, big[3].shape)
    w_b_f = cols(4, big[4].shape)
    w_out_f = unpack(5, big[5].shape).reshape(D, D)

    c_all = _all_gather(c.reshape(8, LANES), "gather_c").reshape(N_DEV, D)
    mod_cols = _mm(c_all, w_ada[0], name="ada_mod")
    mod_all = _all_gather(mod_cols, "gather_mod")
    mod = lax.dynamic_index_in_dim(mod_all, me, axis=1, keepdims=False).reshape(1, 3 * D)
    mod_shift, mod_scale, mod_gate = mod[:, :D], mod[:, D:2 * D], mod[:, 2 * D:]
    b_shift, b_scale, b_gate = b_ada[:, :D], b_ada[:, D:2 * D], b_ada[:, 2 * D:]
    g1 = norm_gain
    gq, gkv = q_norm_gain, kv_norm_gain
    gf = final_norm_gain.reshape(1, D)

    def f_h(x_, g1_, ms, bs, msc, bsc):
        xn = x_ * _rms(x_)
        return (xn * g1_ * (1.0 + (msc + bsc)) + (ms + bs),), ()

    (h,) = _rowwise(f_h, [x2d], [g1, mod_shift, b_shift, mod_scale, b_scale], [(D, BF16)], name="ada_norm")
    proj = _mm(h, w_in_k, name="proj_in")

    o_a, sb_tails = _sb_forward(proj, s_len)

    def f_lat(cq, ckv, gq_, gkv_):
        return (cq * _rms(cq) * gq_, ckv * _rms(ckv) * gkv_), ()

    cq_n, ckv_n = _rowwise(f_lat, [(proj, O_CQ // Q_RANK, Q_RANK), (proj, O_CKV // KV_RANK, KV_RANK)], [gq, gkv],
                           [(Q_RANK, BF16), (KV_RANK, BF16)], name="latent_norm")
    q_mla = _mm(cq_n, w_uq_k, name="q_up")
    kv = _mm(ckv_n, w_ukv_k, name="kv_up")

    inv_freq = ROPE_BASE ** (-jnp.arange(0, ROPE, 2, dtype=F32) / ROPE)
    inv_freq_t = jnp.tile(inv_freq, HEADS).reshape(1, LANES)
    pos_col = positions.reshape(s_len, 1).astype(F32)

    def f_rope(pos, q1, q2, k1, k2, freq):
        ang = pos * freq
        cs, sn = jnp.cos(ang), jnp.sin(ang)
        return (jnp.concatenate([q1 * cs - q2 * sn, q1 * sn + q2 * cs], axis=1),
                jnp.concatenate([k1 * cs - k2 * sn, k1 * sn + k2 * cs], axis=1), cs, sn), ()

    q_pe, k_pe, cos_t, sin_t = _rowwise(
        f_rope, [pos_col, (q_mla, 4, LANES), (q_mla, 5, LANES),
                 (proj, O_KROT // LANES, LANES), (proj, O_KROT // LANES + 1, LANES)], [inv_freq_t],
        [(2 * LANES, F32), (2 * LANES, F32), (LANES, F32), (LANES, F32)], name="rope")

    o_b, lse = _mla_forward(q_mla, q_pe, kv, k_pe, s_len)

    def f_gate(oa, za, ob, zb):
        return (oa * _silu(za), ob * _silu(zb)), ()

    ya_in, yb_in = _rowwise(f_gate, [o_a, (proj, O_SBZ // SB_W, SB_W), o_b, (proj, O_MLAZ // MLA_W, MLA_W)], [],
                            [(SB_W, BF16), (MLA_W, BF16)], name="branch_gate")
    y_a = _mm(ya_in, w_a_f, name="branch_a")
    y_b = _mm(yb_in, w_b_f, name="branch_b")

    def f_merge(ga, gb, ya, yb):
        return (_sigmoid(ga) * ya + _sigmoid(gb) * yb,), ()

    (merged,) = _rowwise(f_merge, [(proj, O_GA // D, D), (proj, O_GB // D, D), y_a, y_b], [], [(D, BF16)], name="merge")
    out = _mm(merged, w_out_f, name="out_proj")

    def f_loss(x_, out_, t_, mg, bg, gf_):
        gate = mg + bg
        x2 = x_ + gate * out_
        r2 = _rms(x2)
        xn2 = x2 * r2
        err = xn2 * gf_ - t_
        loss = jnp.full((1, LANES), 0.5 / D, F32) * jnp.sum(err * err)
        dy = err * (1.0 / D)
        dx2 = _rms_bwd(xn2, r2, dy * gf_)
        return (dx2, dx2 * gate), (loss, _colsum(dy * xn2), _colsum(dx2 * out_))

    dx2, d_out, loss_part, d_gf, d_gate = _rowwise(
        f_loss, [x2d, out, tgt], [mod_gate, b_gate, gf], [(D, F32), (D, BF16)], [LANES, D, D], name="loss_head")

    d_merged = _mm(d_out, w_out_f, tb=True, name="d_merged")
    dw_out = _mm(merged, d_out, ta=True, name="dw_out")

    def f_dmerge(dm, ga, gb, ya, yb):
        sa, sb = _sigmoid(ga), _sigmoid(gb)
        return (dm * sa, dm * sb, dm * ya * sa * (1.0 - sa), dm * yb * sb * (1.0 - sb)), ()

    d_ya, d_yb, d_ga, d_gb = _rowwise(f_dmerge, [d_merged, (proj, O_GA // D, D), (proj, O_GB // D, D), y_a, y_b], [],
                                      [(D, BF16)] * 4, name="d_merge")
    dw_a = _mm(ya_in, d_ya, ta=True, name="dw_branch_a")
    dw_b = _mm(yb_in, d_yb, ta=True, name="dw_branch_b")
    d_ya_in = _mm(d_ya, w_a_f, tb=True, name="d_branch_a")
    d_yb_in = _mm(d_yb, w_b_f, tb=True, name="d_branch_b")

    def f_dgate(da, oa, za, db, ob, zb):
        return (da * _silu(za), da * oa * _silu_grad(za), db * _silu(zb), db * ob * _silu_grad(zb)), ()

    d_oa, d_sbz, d_ob, d_mlaz = _rowwise(
        f_dgate, [d_ya_in, o_a, (proj, O_SBZ // SB_W, SB_W), d_yb_in, o_b, (proj, O_MLAZ // MLA_W, MLA_W)], [],
        [(SB_W, F32), (SB_W, BF16), (MLA_W, F32), (MLA_W, BF16)], name="d_branch_gate")

    d_sbq, d_sbk, d_sbv = _sb_backward(proj, sb_tails, d_oa, s_len)
    dqn, dqpe4, dkn, dkpe4, dv_b = _mla_backward(q_mla, q_pe, kv, k_pe, o_b, lse, d_ob, s_len)

    def f_drope(dqn_, dkn_, dv_, q0, q1, q2, q3, k0, k1, k2, k3, cs, sn):
        dq = (q0 + q1) + (q2 + q3)
        dk = (k0 + k1) + (k2 + k3)
        dq1, dq2 = dq[:, :LANES], dq[:, LANES:]
        dk1, dk2 = dk[:, :LANES], dk[:, LANES:]
        return (jnp.concatenate([dqn_, dq1 * cs + dq2 * sn, dq2 * cs - dq1 * sn], axis=1),
                jnp.concatenate([dkn_, dv_], axis=1),
                jnp.concatenate([dk1 * cs + dk2 * sn, dk2 * cs - dk1 * sn], axis=1)), ()

    pe_rows = [(dqpe4, 0, 2 * LANES, t * s_len) for t in range(4)] + [(dkpe4, 0, 2 * LANES, t * s_len) for t in range(4)]
    dq_k, dkv_k, d_krot = _rowwise(f_drope, [dqn, dkn, dv_b] + pe_rows + [cos_t, sin_t], [],
                                   [(HEADS * QK_DIM, BF16), (2 * MLA_W, BF16), (2 * LANES, BF16)], name="d_rope")
    dw_uq_k = _mm(cq_n, dq_k, ta=True, name="dw_uq")
    dw_ukv_k = _mm(ckv_n, dkv_k, ta=True, name="dw_ukv")
    d_cqn = _mm(dq_k, w_uq_k, tb=True, name="d_cq_norm")
    d_ckvn = _mm(dkv_k, w_ukv_k, tb=True, name="d_ckv_norm")

    def f_dlat(cq, dcqn, ckv, dckvn, gq_, gkv_):
        rq, rkv = _rms(cq), _rms(ckv)
        cqn, ckvn = cq * rq, ckv * rkv
        return ((_rms_bwd(cqn, rq, dcqn * gq_), _rms_bwd(ckvn, rkv, dckvn * gkv_)),
                (_colsum(dcqn * cqn), _colsum(dckvn * ckvn)))

    d_cq, d_ckv, d_gq, d_gkv = _rowwise(
        f_dlat, [(proj, O_CQ // Q_RANK, Q_RANK), d_cqn, (proj, O_CKV // KV_RANK, KV_RANK), d_ckvn], [gq, gkv],
        [(Q_RANK, BF16), (KV_RANK, BF16)], [Q_RANK, KV_RANK], name="d_latent_norm")

    d_proj = jnp.concatenate([d_ga, d_gb, d_sbq.astype(BF16), d_sbk.astype(BF16), d_sbv.astype(BF16), d_sbz, d_mlaz,
                              d_cq, jnp.zeros((s_len, LANES), BF16), d_ckv, d_krot], axis=1)
    dw_in_k = _mm(h, d_proj, ta=True, name="dw_in")
    dw_krot_t = _mm(d_krot, h, ta=True, name="dw_krot")
    dh = _mm(d_proj, w_in_k, tb=True, name="d_h")

    def f_dx(x_, dh_, dx2_, g1_, msc, bsc):
        r = _rms(x_)
        xn = x_ * r
        dn1 = dh_ * (1.0 + (msc + bsc))
        return ((dx2_ + _rms_bwd(xn, r, dn1 * g1_),),
                (_colsum(dh_), _colsum(dh_ * (xn * g1_)), _colsum(dn1 * xn)))

    grad_x2d, d_shift, d_scale, d_g1 = _rowwise(f_dx, [x2d, dh, dx2], [g1, mod_scale, b_scale], [(D, F32)],
                                                [D, D, D], name="d_ada_norm")

    def krot_body(t_ref, o_ref):
        half = ROPE // 2
        for part in range(2):
            acc = t_ref[part * LANES:part * LANES + half, :]
            for hh in range(1, HEADS):
                acc = acc + t_ref[part * LANES + hh * half:part * LANES + (hh + 1) * half, :]
            o_ref[part * half:(part + 1) * half, :] = acc

    dw_krot = pl.pallas_call(krot_body, name="dw_krot_sum", out_shape=jax.ShapeDtypeStruct((ROPE, D), F32))(dw_krot_t).T

    dw_in_full = jnp.concatenate([
        dw_in_k[:, O_SBQ:O_SBQ + 2048], dw_in_k[:, O_CQ:O_CQ + Q_RANK], dw_in_k[:, O_CKV:O_CKV + KV_RANK], dw_krot,
        dw_in_k[:, O_MLAZ:O_MLAZ + MLA_W], dw_in_k[:, O_GA:O_GA + 2 * D]], axis=1)
    dw_uq_full = _uq_from_kernel_layout(dw_uq_k)
    dw_ukv_full = _ukv_from_kernel_layout(dw_ukv_k)

    def col_blocks(g):
        kdim, n8 = g.shape
        return g.astype(BF16).reshape(kdim, N_DEV, n8 // N_DEV).transpose(1, 0, 2).reshape(N_DEV, -1, LANES)

    g_blocks = jnp.concatenate([col_blocks(dw_in_full), col_blocks(dw_uq_full), col_blocks(dw_ukv_full),
                                col_blocks(dw_a), col_blocks(dw_b),
                                dw_out.astype(BF16).reshape(N_DEV, -1, LANES)], axis=1)
    g_recv = _exchange(g_blocks, "exchange_grads")
    g_sum = _sum_blocks(g_recv, "sum_grads")
    g_big = [g_sum[offs[t]:offs[t + 1]].reshape(big[t].shape) for t in range(6)]

    small = jnp.concatenate([d_shift, d_scale, d_gate, d_g1, d_gq, d_gkv, d_gf], axis=1)
    n_small = small.shape[1]
    pad = (-n_small) % (8 * LANES)
    small = jnp.pad(small, ((0, 0), (0, pad))).reshape(-1, LANES)
    small_all = _all_gather(small, "gather_small")
    small_sum = _sum_blocks(small_all, "sum_small").reshape(1, -1)
    g_b_ada = small_sum[:, :3 * D]
    g_g1 = small_sum[:, 3 * D:4 * D]
    g_gq = small_sum[:, 4 * D:4 * D + Q_RANK]
    g_gkv = small_sum[:, 4 * D + Q_RANK:4 * D + Q_RANK + KV_RANK]
    g_gf = small_sum[:, 4 * D + Q_RANK + KV_RANK:4 * D + Q_RANK + KV_RANK + D]

    dmod_all = small_all.reshape(N_DEV, -1)[:, :3 * D]
    dmod_cols = lax.dynamic_slice_in_dim(dmod_all, me * (3 * D // N_DEV), 3 * D // N_DEV, axis=1)
    g_w_ada = _mm(c_all, dmod_cols, ta=True, name="dw_ada")

    loss = lax.psum(loss_part[0, 0], ("x", "y", "c"))

    names = ["w_ada", "b_ada", "norm_gain", "w_in", "q_norm_gain", "w_uq", "kv_norm_gain", "w_ukv",
             "w_branch_a", "w_branch_b", "w_out", "final_norm_gain"]
    weights = dict(w_ada=w_ada, b_ada=b_ada, norm_gain=norm_gain, w_in=w_in, q_norm_gain=q_norm_gain, w_uq=w_uq,
                   kv_norm_gain=kv_norm_gain, w_ukv=w_ukv, w_branch_a=w_branch_a, w_branch_b=w_branch_b, w_out=w_out,
                   final_norm_gain=final_norm_gain)
    moms = dict(w_ada=m_w_ada, b_ada=m_b_ada, norm_gain=m_norm_gain, w_in=m_w_in, q_norm_gain=m_q_norm_gain,
                w_uq=m_w_uq, kv_norm_gain=m_kv_norm_gain, w_ukv=m_w_ukv, w_branch_a=m_w_branch_a,
                w_branch_b=m_w_branch_b, w_out=m_w_out, final_norm_gain=m_final_norm_gain)
    vels = dict(w_ada=v_w_ada, b_ada=v_b_ada, norm_gain=v_norm_gain, w_in=v_w_in, q_norm_gain=v_q_norm_gain,
                w_uq=v_w_uq, kv_norm_gain=v_kv_norm_gain, w_ukv=v_w_ukv, w_branch_a=v_w_branch_a,
                w_branch_b=v_w_branch_b, w_out=v_w_out, final_norm_gain=v_final_norm_gain)
    grads2d = dict(w_ada=g_w_ada, b_ada=g_b_ada, norm_gain=g_g1, w_in=g_big[0], q_norm_gain=g_gq, w_uq=g_big[1],
                   kv_norm_gain=g_gkv, w_ukv=g_big[2], w_branch_a=g_big[3], w_branch_b=g_big[4], w_out=g_big[5],
                   final_norm_gain=g_gf)

    grads, deltas, new_m, new_v = [], [], [], []
    for n in names:
        w = weights[n]
        shape2d = grads2d[n].shape
        d_, m_, v_ = _adamw_call(w.reshape(shape2d), grads2d[n], moms[n].reshape(shape2d), vels[n].reshape(shape2d),
                                 "adamw_" + n)
        grads.append(grads2d[n].reshape(w.shape))
        deltas.append(d_.reshape(w.shape))
        new_m.append(m_.reshape(w.shape))
        new_v.append(v_.reshape(w.shape))

    return (loss, grad_x2d.reshape(x.shape), *grads, *deltas, *new_m, *new_v)
```

```python
import functools
import math

import jax
import jax.numpy as jnp
from jax import lax
from jax.experimental import pallas as pl
from jax.experimental.pallas import tpu as pltpu

F32 = jnp.float32
BF16 = jnp.bfloat16
MXU_DTYPE = jnp.bfloat16

N_DEV = 8
D = 1024
HEADS = 8
HD = 64
SB_W = 512
MLA_W = 512
Q_RANK = 384
KV_RANK = 256
ROPE = 32
NOPE = 64
QK_DIM = NOPE + ROPE
EPS = 1e-6
ROPE_BASE = 10000.0

ADAM_LR = 0.001
ADAM_B1 = 0.9
ADAM_B2 = 0.999
ADAM_EPS = 1e-08
ADAM_WD = 0.01
ADAM_STEP = 10

LANES = 128
VMEM_LIMIT = 48 * 1024 * 1024

O_GA, O_GB = 0, 1024
O_SBQ, O_SBK, O_SBV, O_SBZ = 2048, 2560, 3072, 3584
O_MLAZ = 4096
O_CQ = 4608
O_CKV = 5120
O_KROT = 5376
IN_PAD = 5632

BQ = 256
BK = 256


def _cparams(*sem):
    return pltpu.CompilerParams(dimension_semantics=sem, vmem_limit_bytes=VMEM_LIMIT)


def _tile_of(n, cap=512):
    if n <= cap:
        return n
    for t in (1024, 768, 512, 384, 256, 128):
        if t <= cap and n % t == 0:
            return t
    raise ValueError(n)


def _rowwise(fn, rows, vecs, outs, reds=(), *, name, tile=256):
    norm = []
    for r in rows:
        if isinstance(r, tuple):
            arr, cb, w = r[:3]
            ro = r[3] if len(r) > 3 else 0
        else:
            arr, cb, w, ro = r, 0, r.shape[1], 0
        norm.append((arr, cb, w, ro))
    s_len = norm[0][0].shape[0]
    tile = min(tile, s_len)
    assert s_len % tile == 0
    n_row, n_vec, n_out, n_red = len(norm), len(vecs), len(outs), len(reds)

    def body(*refs):
        step = pl.program_id(0)
        row_refs = refs[:n_row]
        vec_refs = refs[n_row:n_row + n_vec]
        out_refs = refs[n_row + n_vec:n_row + n_vec + n_out]
        red_refs = refs[n_row + n_vec + n_out:]
        row_res, red_res = fn(*[r[...] for r in row_refs], *[v[...] for v in vec_refs])
        for o, val in zip(out_refs, row_res):
            o[...] = val.astype(o.dtype)
        if n_red:
            @pl.when(step == 0)
            def _():
                for r in red_refs:
                    r[...] = jnp.zeros(r.shape, r.dtype)
            for r, val in zip(red_refs, red_res):
                r[...] += val

    in_specs = []
    for arr, cb, w, ro in norm:
        in_specs.append(pl.BlockSpec((tile, w), functools.partial(lambda i, cb, rb: (i + rb, cb), cb=cb, rb=ro // tile)))
        assert ro % tile == 0
    for v in vecs:
        in_specs.append(pl.BlockSpec(v.shape, lambda i: (0, 0)))
    out_shape = [jax.ShapeDtypeStruct((s_len, w), dt) for w, dt in outs]
    out_specs = [pl.BlockSpec((tile, w), lambda i: (i, 0)) for w, _ in outs]
    out_shape += [jax.ShapeDtypeStruct((1, w), F32) for w in reds]
    out_specs += [pl.BlockSpec((1, w), lambda i: (0, 0)) for w in reds]
    res = pl.pallas_call(
        body, name=name, grid=(s_len // tile,), in_specs=in_specs, out_specs=out_specs, out_shape=out_shape,
        compiler_params=_cparams("arbitrary" if n_red else "parallel"),
    )(*[a for a, _, _, _ in norm], *vecs)
    return res


def _mm(a, b, *, ta=False, tb=False, out_dtype=F32, name):
    m, k = (a.shape[1], a.shape[0]) if ta else a.shape
    n = b.shape[0] if tb else b.shape[1]
    assert (b.shape[1] if tb else b.shape[0]) == k
    tm, tn, tk = _tile_of(m, 1024), _tile_of(n, 1024 if n <= 1024 else 512), _tile_of(k, 1024)
    nk = k // tk
    dims = (((0 if ta else 1,), (1 if tb else 0,)), ((), ()))

    def body(a_ref, b_ref, o_ref, acc_ref):
        kk = pl.program_id(2)

        @pl.when(kk == 0)
        def _():
            acc_ref[...] = jnp.zeros(acc_ref.shape, F32)

        acc_ref[...] += lax.dot_general(a_ref[...].astype(MXU_DTYPE), b_ref[...].astype(MXU_DTYPE), dims,
                                        preferred_element_type=F32)

        @pl.when(kk == nk - 1)
        def _():
            o_ref[...] = acc_ref[...].astype(o_ref.dtype)

    a_spec = pl.BlockSpec((tk, tm), lambda i, j, kk: (kk, i)) if ta else pl.BlockSpec((tm, tk), lambda i, j, kk: (i, kk))
    b_spec = pl.BlockSpec((tn, tk), lambda i, j, kk: (j, kk)) if tb else pl.BlockSpec((tk, tn), lambda i, j, kk: (kk, j))
    return pl.pallas_call(
        body, name=name, grid=(m // tm, n // tn, nk), in_specs=[a_spec, b_spec],
        out_specs=pl.BlockSpec((tm, tn), lambda i, j, kk: (i, j)),
        out_shape=jax.ShapeDtypeStruct((m, n), out_dtype),
        scratch_shapes=[pltpu.VMEM((tm, tn), F32)],
        compiler_params=_cparams("parallel", "parallel", "arbitrary"),
    )(a, b)


_NT = (((1,), (1,)), ((), ()))
_TN = (((0,), (0,)), ((), ()))


def _dot(a, b):
    return jnp.dot(a, b, preferred_element_type=F32)


def _dot_nt(a, b):
    return lax.dot_general(a, b, _NT, preferred_element_type=F32)


def _dot_tn(a, b):
    return lax.dot_general(a, b, _TN, preferred_element_type=F32)


def _split_dot(x, tri):
    hi = x.astype(MXU_DTYPE)
    lo = (x - hi.astype(F32)).astype(MXU_DTYPE)
    return _dot(hi, tri) + _dot(lo, tri)


def _neg_softplus(z):
    u = jnp.exp2(jnp.abs(z) * (-1.0 / math.log(2.0)))
    return -jnp.maximum(z, 0.0) - jnp.log(1.0 + u)


def _walk_blocks(step, st, n, descending):
    pairs = n // 2

    def two(t, s):
        j = (n - 1 - 2 * t) if descending else 2 * t
        return step([j, j - 1 if descending else j + 1], s)

    st = lax.fori_loop(0, pairs, two, st)
    last = 0 if descending else n - 1
    return lax.fori_loop(0, n - 2 * pairs, lambda _, s: step([last], s), st)


def _chains(js):
    return [(h, t) for t in range(len(js)) for h in range(2)]


def _rowsum(x):
    return jnp.sum(x, axis=1, keepdims=True)


def _attn_consts():
    row = lax.broadcasted_iota(jnp.int32, (BQ, BK), 0)
    col = lax.broadcasted_iota(jnp.int32, (BQ, BK), 1)
    trow = lax.broadcasted_iota(jnp.int32, (BK, BK), 0)
    tcol = lax.broadcasted_iota(jnp.int32, (BK, BK), 1)
    lane = lax.broadcasted_iota(jnp.int32, (BQ, LANES), 1)
    klane = lax.broadcasted_iota(jnp.int32, (BK, LANES), 1)
    return row, col, trow, tcol, lane, klane


def _diag_block(i):
    jd = (i * BQ) // BK
    return jd, i * BQ - jd * BK


def _key_slice(j):
    return pl.ds(pl.multiple_of(j * BK, BK), BK)


def _sb_forward(proj, s_len):
    nq = s_len // BQ
    assert s_len // BK <= HD
    qc, kc, vc = O_SBQ // LANES, O_SBK // LANES, O_SBV // LANES

    def body(q_ref, k_ref, v_ref, o_ref, tails_ref):
        i = pl.program_id(1)
        jd, off = _diag_block(i)
        row, col, trow, tcol, lane, klane = _attn_consts()
        strict = col < row + off
        tri = (trow >= tcol).astype(MXU_DTYPE)
        q = q_ref[...] * 0.125
        qh = [jnp.where(lane < HD, q, 0.0).astype(MXU_DTYPE), jnp.where(lane >= HD, q, 0.0).astype(MXU_DTYPE)]

        km = [klane < HD, klane >= HD]

        def step(js, st, diag):
            carry, acc, tail = st
            chains = _chains(js)
            kj = [k_ref[_key_slice(j), :].astype(MXU_DTYPE) for j in js]
            vj = [v_ref[_key_slice(j), :].astype(MXU_DTYPE) for j in js]
            z = {(h, t): _dot_nt(qh[h], kj[t]) for h, t in chains}
            run = list(carry)
            suf, carry_in = {}, {}
            for h, t in chains:
                lom = _neg_softplus(z[h, t])
                if diag:
                    lom = jnp.where(strict, lom, 0.0)
                suf[h, t] = _split_dot(lom, tri)
                carry_in[h, t] = run[h]
                run[h] = run[h] + _rowsum(lom)
            for h, t in chains:
                a = jnp.exp(z[h, t] + suf[h, t] + carry_in[h, t])
                if diag:
                    a = jnp.where(strict, a, 0.0)
                acc = acc + _dot(a.astype(MXU_DTYPE), jnp.where(km[h], vj[t], 0))
                tail = jnp.where(lane == h * HD + js[t], carry_in[h, t], tail)
            return tuple(run), acc, tail

        zero = jnp.zeros((BQ, LANES), F32)
        st = step([jd], ((jnp.zeros((BQ, 1), F32),) * 2, zero, zero), True)
        st = _walk_blocks(lambda js, s: step(js, s, False), st, jd, True)
        o_ref[...] = st[1]
        tails_ref[...] = st[2]

    blk = pl.BlockSpec((BQ, LANES), lambda p, i: (i, p))
    out = jax.ShapeDtypeStruct((s_len, SB_W), F32)
    return pl.pallas_call(
        body, name="sb_fwd", grid=(HEADS // 2, nq),
        in_specs=[pl.BlockSpec((BQ, LANES), lambda p, i: (i, qc + p)),
                  pl.BlockSpec((s_len, LANES), lambda p, i: (0, kc + p)),
                  pl.BlockSpec((s_len, LANES), lambda p, i: (0, vc + p))],
        out_specs=[blk, blk], out_shape=[out, out],
        compiler_params=_cparams("parallel", "parallel"),
    )(proj, proj, proj)


def _sb_backward(proj, tails, do, s_len):
    nq = s_len // BQ
    qc, kc, vc = O_SBQ // LANES, O_SBK // LANES, O_SBV // LANES

    def body(q_ref, k_ref, v_ref, tails_ref, do_ref, dq_ref, dk_ref, dv_ref):
        i = pl.program_id(1)
        jd, off = _diag_block(i)

        @pl.when(i == 0)
        def _():
            dk_ref[...] = jnp.zeros(dk_ref.shape, F32)
            dv_ref[...] = jnp.zeros(dv_ref.shape, F32)

        row, col, trow, tcol, lane, klane = _attn_consts()
        strict = col < row + off
        tri = (trow >= tcol).astype(MXU_DTYPE)
        tri_p = (trow <= tcol).astype(MXU_DTYPE)
        q = q_ref[...] * 0.125
        tails_blk = tails_ref[...]
        do_blk = do_ref[...]
        hm = [lane < HD, lane >= HD]
        km = [klane < HD, klane >= HD]
        qh = [jnp.where(m, q, 0.0).astype(MXU_DTYPE) for m in hm]
        doh = [jnp.where(m, do_blk, 0.0).astype(MXU_DTYPE) for m in hm]

        def step(js, st, diag):
            before, dq = st
            chains = _chains(js)
            kj = [k_ref[_key_slice(j), :].astype(MXU_DTYPE) for j in js]
            vj = [v_ref[_key_slice(j), :].astype(MXU_DTYPE) for j in js]
            z = {(h, t): _dot_nt(qh[h], kj[t]) for h, t in chains}
            da = {(h, t): _dot_nt(doh[h], vj[t]) for h, t in chains}
            suf, sig = {}, {}
            for h, t in chains:
                lom = _neg_softplus(z[h, t])
                if diag:
                    lom = jnp.where(strict, lom, 0.0)
                suf[h, t] = _split_dot(lom, tri)
                sig[h, t] = jnp.exp(z[h, t] + lom)
            run = list(before)
            dl, pre, before_in = {}, {}, {}
            dk_add, dv_add = [None] * len(js), [None] * len(js)
            for h, t in chains:
                tail = _rowsum(jnp.where(lane == h * HD + js[t], tails_blk, 0.0))
                a = jnp.exp(z[h, t] + suf[h, t] + tail)
                if diag:
                    a = jnp.where(strict, a, 0.0)
                dl[h, t] = da[h, t] * a
                pre[h, t] = _dot(dl[h, t].astype(MXU_DTYPE), tri_p)
                dv_h = _dot_tn(a.astype(MXU_DTYPE), doh[h])
                dv_add[t] = dv_h if dv_add[t] is None else dv_add[t] + dv_h
                before_in[h, t] = run[h]
                run[h] = run[h] + _rowsum(dl[h, t])
            for h, t in chains:
                upto = before_in[h, t] + pre[h, t]
                dz = dl[h, t] - sig[h, t] * upto
                if diag:
                    dz = jnp.where(strict, dz, 0.0)
                dzb = dz.astype(MXU_DTYPE)
                dq = dq + _dot(dzb, jnp.where(km[h], kj[t], 0))
                dk_h = _dot_tn(dzb, qh[h])
                dk_add[t] = dk_h if dk_add[t] is None else dk_add[t] + dk_h
            for t, j in enumerate(js):
                dk_ref[_key_slice(j), :] += dk_add[t]
                dv_ref[_key_slice(j), :] += dv_add[t]
            return tuple(run), dq

        st = ((jnp.zeros((BQ, 1), F32),) * 2, jnp.zeros((BQ, LANES), F32))
        st = _walk_blocks(lambda js, s: step(js, s, False), st, jd, False)
        st = step([jd], st, True)
        dq_ref[...] = st[1] * 0.125

    blk = pl.BlockSpec((BQ, LANES), lambda p, i: (i, p))
    full = pl.BlockSpec((s_len, LANES), lambda p, i: (0, p))
    out = jax.ShapeDtypeStruct((s_len, SB_W), F32)
    return pl.pallas_call(
        body, name="sb_bwd", grid=(HEADS // 2, nq),
        in_specs=[pl.BlockSpec((BQ, LANES), lambda p, i: (i, qc + p)),
                  pl.BlockSpec((s_len, LANES), lambda p, i: (0, kc + p)),
                  pl.BlockSpec((s_len, LANES), lambda p, i: (0, vc + p)),
                  blk, blk],
        out_specs=[blk, full, full], out_shape=[out, out, out],
        compiler_params=_cparams("arbitrary", "arbitrary"),
    )(proj, proj, proj, tails, do)


def _pe_mask(rows, head):
    lane = lax.broadcasted_iota(jnp.int32, (rows, 2 * LANES), 1)
    return ((lane % LANES) // (ROPE // 2)) == head


def _mla_forward(q_mla, q_pe, kv, k_pe, s_len):
    nq = s_len // BQ
    scale = 1.0 / math.sqrt(QK_DIM)

    def body(qn_ref, qpe_ref, kn_ref, kpe_ref, v_ref, o_ref, lse_ref):
        p = pl.program_id(0)
        i = pl.program_id(1)
        jd, off = _diag_block(i)
        row, col, trow, tcol, lane, klane = _attn_consts()
        causal = col <= row + off
        qn = qn_ref[...]
        qpe = qpe_ref[...]
        hm = [lane < HD, lane >= HD]
        km = [klane < HD, klane >= HD]
        qnh = [jnp.where(m, qn, 0.0).astype(MXU_DTYPE) for m in hm]
        qph = [jnp.where(_pe_mask(BQ, 2 * p + h), qpe, 0.0).astype(MXU_DTYPE) for h in range(2)]

        def step(js, st, diag):
            m_run, l_run, acc = st
            chains = _chains(js)
            knj = [kn_ref[_key_slice(j), :].astype(MXU_DTYPE) for j in js]
            kpj = [kpe_ref[_key_slice(j), :].astype(MXU_DTYPE) for j in js]
            vj = [v_ref[_key_slice(j), :].astype(MXU_DTYPE) for j in js]
            s = {}
            for h, t in chains:
                s[h, t] = (_dot_nt(qnh[h], knj[t]) + _dot_nt(qph[h], kpj[t])) * scale
                if diag:
                    s[h, t] = jnp.where(causal, s[h, t], -jnp.inf)
            m_new, alpha, l_new = [], [], []
            for h in range(2):
                top = m_run[h]
                for t in range(len(js)):
                    top = jnp.maximum(top, jnp.max(s[h, t], axis=1, keepdims=True))
                m_new.append(top)
                alpha.append(jnp.exp(m_run[h] - top))
                l_new.append(alpha[h] * l_run[h])
            add = None
            for h, t in chains:
                pr = jnp.exp(s[h, t] - m_new[h])
                l_new[h] = l_new[h] + _rowsum(pr)
                part = _dot(pr.astype(MXU_DTYPE), jnp.where(km[h], vj[t], 0))
                add = part if add is None else add + part
            acc = jnp.where(hm[0], alpha[0], alpha[1]) * acc + add
            return tuple(m_new), tuple(l_new), acc

        st = ((jnp.full((BQ, 1), -1e30, F32),) * 2, (jnp.zeros((BQ, 1), F32),) * 2, jnp.zeros((BQ, LANES), F32))
        st = step([jd], st, True)
        m_run, l_run, acc = _walk_blocks(lambda js, s: step(js, s, False), st, jd, True)
        o_ref[...] = acc / jnp.where(hm[0], l_run[0], l_run[1])
        lse_ref[...] = jnp.where(hm[0], m_run[0] + jnp.log(l_run[0]), m_run[1] + jnp.log(l_run[1]))

    blk = pl.BlockSpec((BQ, LANES), lambda p, i: (i, p))
    out = jax.ShapeDtypeStruct((s_len, MLA_W), F32)
    return pl.pallas_call(
        body, name="mla_fwd", grid=(HEADS // 2, nq),
        in_specs=[blk,
                  pl.BlockSpec((BQ, 2 * LANES), lambda p, i: (i, 0)),
                  pl.BlockSpec((s_len, LANES), lambda p, i: (0, p)),
                  pl.BlockSpec((s_len, 2 * LANES), lambda p, i: (0, 0)),
                  pl.BlockSpec((s_len, LANES), lambda p, i: (0, MLA_W // LANES + p))],
        out_specs=[blk, blk], out_shape=[out, out],
        compiler_params=_cparams("parallel", "parallel"),
    )(q_mla, q_pe, kv, k_pe, kv)


def _mla_backward(q_mla, q_pe, kv, k_pe, o, lse, do, s_len):
    nq = s_len // BQ
    scale = 1.0 / math.sqrt(QK_DIM)

    def body(qn_ref, qpe_ref, kn_ref, kpe_ref, v_ref, o_ref, lse_ref, do_ref,
             dqn_ref, dqpe_ref, dkn_ref, dkpe_ref, dv_ref):
        p = pl.program_id(0)
        i = pl.program_id(1)

        @pl.when(i == 0)
        def _():
            dkn_ref[...] = jnp.zeros(dkn_ref.shape, F32)
            dkpe_ref[...] = jnp.zeros(dkpe_ref.shape, F32)
            dv_ref[...] = jnp.zeros(dv_ref.shape, F32)

        jd, off = _diag_block(i)
        row, col, trow, tcol, lane, klane = _attn_consts()
        causal = col <= row + off
        qn = qn_ref[...]
        qpe = qpe_ref[...]
        o_blk = o_ref[...]
        do_blk = do_ref[...]
        lse_blk = lse_ref[...]
        hm = [lane < HD, lane >= HD]
        km = [klane < HD, klane >= HD]
        kpm = [_pe_mask(BK, 2 * p + h) for h in range(2)]
        qnh = [jnp.where(m, qn, 0.0).astype(MXU_DTYPE) for m in hm]
        qph = [jnp.where(_pe_mask(BQ, 2 * p + h), qpe, 0.0).astype(MXU_DTYPE) for h in range(2)]
        doh_f = [jnp.where(m, do_blk, 0.0) for m in hm]
        doh = [d.astype(MXU_DTYPE) for d in doh_f]
        delta = [jnp.sum(d * o_blk, axis=1, keepdims=True) for d in doh_f]
        lse_h = [jnp.sum(jnp.where(lane == h * HD, lse_blk, 0.0), axis=1, keepdims=True) for h in range(2)]

        def step(js, st, diag):
            dqn, dqp = st
            chains = _chains(js)
            knj = [kn_ref[_key_slice(j), :].astype(MXU_DTYPE) for j in js]
            kpj = [kpe_ref[_key_slice(j), :].astype(MXU_DTYPE) for j in js]
            vj = [v_ref[_key_slice(j), :].astype(MXU_DTYPE) for j in js]
            s = {(h, t): _dot_nt(qnh[h], knj[t]) + _dot_nt(qph[h], kpj[t]) for h, t in chains}
            dp = {(h, t): _dot_nt(doh[h], vj[t]) for h, t in chains}
            adds = [[None] * len(js) for _ in range(3)]

            def accumulate(slot, t, part):
                adds[slot][t] = part if adds[slot][t] is None else adds[slot][t] + part

            for h, t in chains:
                pr = jnp.exp(s[h, t] * scale - lse_h[h])
                if diag:
                    pr = jnp.where(causal, pr, 0.0)
                dsb = (pr * (dp[h, t] - delta[h]) * scale).astype(MXU_DTYPE)
                dqn = dqn + _dot(dsb, jnp.where(km[h], knj[t], 0))
                dqp = dqp + _dot(dsb, jnp.where(kpm[h], kpj[t], 0))
                accumulate(0, t, _dot_tn(dsb, qnh[h]))
                accumulate(1, t, _dot_tn(dsb, qph[h]))
                accumulate(2, t, _dot_tn(pr.astype(MXU_DTYPE), doh[h]))
            for t, j in enumerate(js):
                dkn_ref[_key_slice(j), :] += adds[0][t]
                dkpe_ref[_key_slice(j), :] += adds[1][t]
                dv_ref[_key_slice(j), :] += adds[2][t]
            return dqn, dqp

        st = step([jd], (jnp.zeros((BQ, LANES), F32), jnp.zeros((BQ, 2 * LANES), F32)), True)
        st = _walk_blocks(lambda js, s: step(js, s, False), st, jd, True)
        dqn_ref[...] = st[0]
        dqpe_ref[...] = st[1]

    blk = pl.BlockSpec((BQ, LANES), lambda p, i: (i, p))
    full = pl.BlockSpec((s_len, LANES), lambda p, i: (0, p))
    out = jax.ShapeDtypeStruct((s_len, MLA_W), F32)
    out_pe = jax.ShapeDtypeStruct((4 * s_len, 2 * LANES), F32)
    return pl.pallas_call(
        body, name="mla_bwd", grid=(HEADS // 2, nq),
        in_specs=[blk,
                  pl.BlockSpec((BQ, 2 * LANES), lambda p, i: (i, 0)),
                  pl.BlockSpec((s_len, LANES), lambda p, i: (0, p)),
                  pl.BlockSpec((s_len, 2 * LANES), lambda p, i: (0, 0)),
                  pl.BlockSpec((s_len, LANES), lambda p, i: (0, MLA_W // LANES + p)),
                  blk, blk, blk],
        out_specs=[blk, pl.BlockSpec((BQ, 2 * LANES), lambda p, i: (p * nq + i, 0)), full,
                   pl.BlockSpec((s_len, 2 * LANES), lambda p, i: (p, 0)), full],
        out_shape=[out, out_pe, out, out_pe, out],
        compiler_params=_cparams("arbitrary", "arbitrary"),
    )(q_mla, q_pe, kv, k_pe, kv, o, lse, do)


def _mesh_pos():
    return lax.axis_index("x"), lax.axis_index("y"), lax.axis_index("c")


def _dev_index(px, py, pc):
    return 4 * px + 2 * py + pc


def _all_gather(block, name):
    return _all_gather_parts([block], name)[0]


def _all_gather_parts(blocks, name):
    n = len(blocks)

    def body(*refs):
        x_refs, out_refs = refs[:n], refs[n:2 * n]
        send_sems, recv_sems, local_sems = refs[2 * n:]
        x, y, c = _mesh_pos()
        me, sibling = (x, y, c), (x, y, 1 - c)
        chips = [(1 - x, y), (x, 1 - y), (1 - x, 1 - y)]

        def copy(a, k, blockpos, to, src=None):
            slot = out_refs[a].at[_dev_index(*blockpos)]
            return pltpu.make_async_remote_copy(
                src_ref=slot if src is None else src, dst_ref=slot,
                send_sem=send_sems.at[7 * a + k], recv_sem=recv_sems.at[7 * a + k],
                device_id=to, device_id_type=pl.DeviceIdType.MESH)

        mine = [pltpu.make_async_copy(x_refs[a], out_refs[a].at[_dev_index(*me)], local_sems.at[a]) for a in range(n)]
        for cp in mine:
            cp.start()
        first = []
        for a in range(n):
            first.append(copy(a, 0, me, sibling, src=x_refs[a]))
            first += [copy(a, 1 + j, me, (*chip, c), src=x_refs[a]) for j, chip in enumerate(chips)]
        for cp in first:
            cp.start()
        passed = []
        for j, chip in enumerate(chips):
            for a in range(n):
                copy(a, 1 + j, (*chip, c), me).wait_recv()
                passed.append(copy(a, 4 + j, (*chip, c), sibling))
                passed[-1].start()
        for a in range(n):
            copy(a, 0, sibling, me).wait_recv()
            for j, chip in enumerate(chips):
                copy(a, 4 + j, (*chip, 1 - c), me).wait_recv()
        for cp in first + passed:
            cp.wait_send()
        for cp in mine:
            cp.wait()

    return pl.pallas_call(
        body, name=name,
        out_shape=[jax.ShapeDtypeStruct((N_DEV,) + b.shape, b.dtype) for b in blocks],
        in_specs=[pl.BlockSpec(memory_space=pl.ANY)] * n, out_specs=[pl.BlockSpec(memory_space=pl.ANY)] * n,
        scratch_shapes=[pltpu.SemaphoreType.DMA((7 * n,)), pltpu.SemaphoreType.DMA((7 * n,)),
                        pltpu.SemaphoreType.DMA((n,))],
    )(*blocks)


def _exchange_parts(arrays, name):
    n = len(arrays)

    def body(*refs):
        x_refs, out_refs = refs[:n], refs[n:2 * n]
        send_sems, recv_sems, local_sems = refs[2 * n:]
        x, y, c = _mesh_pos()
        me = _dev_index(x, y, c)
        flips = [(fx, fy, fc) for fx in (0, 1) for fy in (0, 1) for fc in (0, 1)][1:]
        peers = [(1 - x if fx else x, 1 - y if fy else y, 1 - c if fc else c) for fx, fy, fc in flips]
        mine = [pltpu.make_async_copy(x_refs[a].at[me], out_refs[a].at[me], local_sems.at[a]) for a in range(n)]
        for cp in mine:
            cp.start()
        sends = []
        for a in range(n):
            for k, peer in enumerate(peers):
                sends.append(pltpu.make_async_remote_copy(
                    src_ref=x_refs[a].at[_dev_index(*peer)], dst_ref=out_refs[a].at[me],
                    send_sem=send_sems.at[7 * a + k], recv_sem=recv_sems.at[7 * a + k],
                    device_id=peer, device_id_type=pl.DeviceIdType.MESH))
                sends[-1].start()
        for a in range(n):
            for k, peer in enumerate(peers):
                pltpu.make_async_remote_copy(
                    src_ref=x_refs[a].at[me], dst_ref=out_refs[a].at[_dev_index(*peer)],
                    send_sem=send_sems.at[7 * a + k], recv_sem=recv_sems.at[7 * a + k],
                    device_id=peer, device_id_type=pl.DeviceIdType.MESH).wait_recv()
        for cp in sends:
            cp.wait_send()
        for cp in mine:
            cp.wait()

    return pl.pallas_call(
        body, name=name,
        out_shape=[jax.ShapeDtypeStruct(a.shape, a.dtype) for a in arrays],
        in_specs=[pl.BlockSpec(memory_space=pl.ANY)] * n, out_specs=[pl.BlockSpec(memory_space=pl.ANY)] * n,
        scratch_shapes=[pltpu.SemaphoreType.DMA((7 * n,)), pltpu.SemaphoreType.DMA((7 * n,)),
                        pltpu.SemaphoreType.DMA((n,))],
    )(*arrays)


def _sum_blocks(parts, name):
    n, r, c = parts.shape
    row_tiles = [t for t in range(16, min(r, 2048) + 1, 16) if r % t == 0]
    if row_tiles:
        tr, tc = max(row_tiles), c
    else:
        tr, tc = r, 2 * LANES
    assert c % tc == 0

    def body(p_ref, o_ref):
        acc = p_ref[0].astype(F32)
        for s in range(1, n):
            acc = acc + p_ref[s].astype(F32)
        o_ref[...] = acc

    return pl.pallas_call(
        body, name=name, grid=(r // tr, c // tc),
        in_specs=[pl.BlockSpec((n, tr, tc), lambda i, j: (0, i, j))],
        out_specs=pl.BlockSpec((tr, tc), lambda i, j: (i, j)),
        out_shape=jax.ShapeDtypeStruct((r, c), F32),
        compiler_params=_cparams("parallel", "parallel"),
    )(parts)


def _sigmoid(x):
    return 1.0 / (1.0 + jnp.exp(-x))


def _silu(x):
    return x * _sigmoid(x)


def _silu_grad(x):
    s = _sigmoid(x)
    return s * (1.0 + x * (1.0 - s))


def _colsum(x):
    return jnp.sum(x, axis=0, keepdims=True)


def _rms(x):
    return lax.rsqrt(jnp.mean(x * x, axis=-1, keepdims=True) + EPS)


def _rms_bwd(xn, r, dxn):
    return r * (dxn - xn * jnp.mean(dxn * xn, axis=-1, keepdims=True))


def _adamw(w, g, m, v):
    m = ADAM_B1 * m + (1.0 - ADAM_B1) * g
    v = ADAM_B2 * v + (1.0 - ADAM_B2) * jnp.square(g)
    m_hat = m / (1.0 - ADAM_B1 ** ADAM_STEP)
    v_hat = v / (1.0 - ADAM_B2 ** ADAM_STEP)
    delta = -ADAM_LR * (m_hat / (jnp.sqrt(v_hat) + ADAM_EPS) + ADAM_WD * w)
    return delta, m, v


def _adamw_call(w, g, m, v, name):
    r, c = w.shape
    if r % 256 == 0:
        tr, tc = 256, c
    elif r * c <= 256 * 1024 or c % (2 * LANES):
        tr, tc = r, c
    else:
        tr, tc = r, 2 * LANES

    def body(w_ref, g_ref, m_ref, v_ref, d_out, m_out, v_out):
        d_out[...], m_out[...], v_out[...] = _adamw(w_ref[...], g_ref[...], m_ref[...], v_ref[...])

    spec = pl.BlockSpec((tr, tc), lambda i, j: (i, j))
    return pl.pallas_call(
        body, name=name, grid=(r // tr, c // tc), in_specs=[spec] * 4, out_specs=[spec] * 3,
        out_shape=[jax.ShapeDtypeStruct((r, c), F32)] * 3, compiler_params=_cparams("parallel", "parallel"),
    )(w, g, m, v)


def _uq_to_kernel_layout(w):
    lead = w.shape[:-1]
    t = w.reshape(lead + (HEADS, QK_DIM))
    return jnp.concatenate([t[..., :NOPE].reshape(lead + (HEADS * NOPE,)),
                            t[..., NOPE:NOPE + ROPE // 2].reshape(lead + (LANES,)),
                            t[..., NOPE + ROPE // 2:].reshape(lead + (LANES,))], axis=-1)


def _uq_from_kernel_layout(w):
    lead = w.shape[:-1]
    nope = w[..., :HEADS * NOPE].reshape(lead + (HEADS, NOPE))
    r1 = w[..., HEADS * NOPE:HEADS * NOPE + LANES].reshape(lead + (HEADS, ROPE // 2))
    r2 = w[..., HEADS * NOPE + LANES:].reshape(lead + (HEADS, ROPE // 2))
    return jnp.concatenate([nope, r1, r2], axis=-1).reshape(lead + (HEADS * QK_DIM,))


def _ukv_to_kernel_layout(w):
    lead = w.shape[:-1]
    t = w.reshape(lead + (HEADS, NOPE + HD))
    return jnp.concatenate([t[..., :NOPE].reshape(lead + (HEADS * NOPE,)),
                            t[..., NOPE:].reshape(lead + (HEADS * HD,))], axis=-1)


def _ukv_from_kernel_layout(w):
    lead = w.shape[:-1]
    kn = w[..., :HEADS * NOPE].reshape(lead + (HEADS, NOPE))
    vv = w[..., HEADS * NOPE:].reshape(lead + (HEADS, HD))
    return jnp.concatenate([kn, vv], axis=-1).reshape(lead + (HEADS * (NOPE + HD),))


def _w_in_t_to_kernel_layout(wt):
    sb = wt[0:2048]
    c_q = wt[2048:2432]
    c_kv = wt[2432:2688]
    k_rot = wt[2688:2720]
    mla_z = wt[2720:3232]
    gates = wt[3232:5280]
    zeros = jnp.zeros((LANES, wt.shape[1]), wt.dtype)
    k1 = jnp.tile(k_rot[:ROPE // 2], (HEADS, 1))
    k2 = jnp.tile(k_rot[ROPE // 2:], (HEADS, 1))
    return jnp.concatenate([gates, sb, mla_z, c_q, zeros, c_kv, k1, k2], axis=0)


def _w_in_t_from_kernel_layout(gt, g_rot):
    return jnp.concatenate([gt[O_SBQ:O_SBQ + 2048], gt[O_CQ:O_CQ + Q_RANK], gt[O_CKV:O_CKV + KV_RANK], g_rot,
                            gt[O_MLAZ:O_MLAZ + MLA_W], gt[O_GA:O_GA + 2 * D]], axis=0)


def kernel(x, c, positions, w_ada, b_ada, norm_gain, w_in, q_norm_gain, w_uq, kv_norm_gain, w_ukv, w_branch_a, w_branch_b, w_out, final_norm_gain, loss_target, m_w_ada, m_b_ada, m_norm_gain, m_w_in, m_q_norm_gain, m_w_uq, m_kv_norm_gain, m_w_ukv, m_w_branch_a, m_w_branch_b, m_w_out, m_final_norm_gain, v_w_ada, v_b_ada, v_norm_gain, v_w_in, v_q_norm_gain, v_w_uq, v_kv_norm_gain, v_w_ukv, v_w_branch_a, v_w_branch_b, v_w_out, v_final_norm_gain):
    s_len = x.shape[1]
    me = _dev_index(*_mesh_pos())
    x2d = x[0]
    tgt = loss_target[0]

    w_in_t = w_in[0].T.astype(BF16)
    big = [w_uq[0], w_ukv[0], w_branch_a[0], w_branch_b[0], w_out[0]]
    big_sizes = [int(w.size) for w in big]
    packed = jnp.concatenate([w.astype(BF16).reshape(-1, LANES) for w in big], axis=0)
    g_in_t, gathered = _all_gather_parts([w_in_t, packed], "gather_weights")
    offs = [0]
    for n in big_sizes:
        offs.append(offs[-1] + n // LANES)

    def unpack(t, shape):
        return gathered[:, offs[t]:offs[t + 1], :].reshape((N_DEV,) + shape)

    def cols(t, shape):
        return unpack(t, shape).transpose(1, 0, 2).reshape(shape[0], N_DEV * shape[1])

    w_in_kt = _w_in_t_to_kernel_layout(g_in_t.reshape(N_DEV * w_in_t.shape[0], D))
    w_uq_k = _uq_to_kernel_layout(cols(0, big[0].shape))
    w_ukv_k = _ukv_to_kernel_layout(cols(1, big[1].shape))
    w_a_f = cols(2, big[2].shape)
    w_b_f = cols(3, big[3].shape)
    w_out_f = unpack(4, big[4].shape).reshape(D, D)

    c_all = _all_gather(c.reshape(8, LANES), "gather_c").reshape(N_DEV, D)
    mod_cols = _mm(c_all, w_ada[0], name="ada_mod")
    mod_all = _all_gather(mod_cols, "gather_mod")
    mod = lax.dynamic_index_in_dim(mod_all, me, axis=1, keepdims=False).reshape(1, 3 * D)
    mod_shift, mod_scale, mod_gate = mod[:, :D], mod[:, D:2 * D], mod[:, 2 * D:]
    b_shift, b_scale, b_gate = b_ada[:, :D], b_ada[:, D:2 * D], b_ada[:, 2 * D:]
    g1 = norm_gain
    gq, gkv = q_norm_gain, kv_norm_gain
    gf = final_norm_gain.reshape(1, D)

    def f_h(x_, g1_, ms, bs, msc, bsc):
        xn = x_ * _rms(x_)
        return (xn * g1_ * (1.0 + (msc + bsc)) + (ms + bs),), ()

    (h,) = _rowwise(f_h, [x2d], [g1, mod_shift, b_shift, mod_scale, b_scale], [(D, BF16)], name="ada_norm")
    proj = _mm(h, w_in_kt, tb=True, name="proj_in")

    o_a, sb_tails = _sb_forward(proj, s_len)

    def f_lat(cq, ckv, gq_, gkv_):
        return (cq * _rms(cq) * gq_, ckv * _rms(ckv) * gkv_), ()

    cq_n, ckv_n = _rowwise(f_lat, [(proj, O_CQ // Q_RANK, Q_RANK), (proj, O_CKV // KV_RANK, KV_RANK)], [gq, gkv],
                           [(Q_RANK, BF16), (KV_RANK, BF16)], name="latent_norm")
    q_mla = _mm(cq_n, w_uq_k, name="q_up")
    kv = _mm(ckv_n, w_ukv_k, name="kv_up")

    inv_freq = ROPE_BASE ** (-jnp.arange(0, ROPE, 2, dtype=F32) / ROPE)
    inv_freq_t = jnp.tile(inv_freq, HEADS).reshape(1, LANES)
    pos_col = positions.reshape(s_len, 1).astype(F32)

    def f_rope(pos, q1, q2, k1, k2, freq):
        ang = pos * freq
        cs, sn = jnp.cos(ang), jnp.sin(ang)
        return (jnp.concatenate([q1 * cs - q2 * sn, q1 * sn + q2 * cs], axis=1),
                jnp.concatenate([k1 * cs - k2 * sn, k1 * sn + k2 * cs], axis=1), cs, sn), ()

    q_pe, k_pe, cos_t, sin_t = _rowwise(
        f_rope, [pos_col, (q_mla, 4, LANES), (q_mla, 5, LANES),
                 (proj, O_KROT // LANES, LANES), (proj, O_KROT // LANES + 1, LANES)], [inv_freq_t],
        [(2 * LANES, F32), (2 * LANES, F32), (LANES, F32), (LANES, F32)], name="rope")

    o_b, lse = _mla_forward(q_mla, q_pe, kv, k_pe, s_len)

    def f_gate(oa, za, ob, zb):
        return (oa * _silu(za), ob * _silu(zb)), ()

    ya_in, yb_in = _rowwise(f_gate, [o_a, (proj, O_SBZ // SB_W, SB_W), o_b, (proj, O_MLAZ // MLA_W, MLA_W)], [],
                            [(SB_W, BF16), (MLA_W, BF16)], name="branch_gate")
    y_a = _mm(ya_in, w_a_f, name="branch_a")
    y_b = _mm(yb_in, w_b_f, name="branch_b")

    def f_merge(ga, gb, ya, yb):
        return (_sigmoid(ga) * ya + _sigmoid(gb) * yb,), ()

    (merged,) = _rowwise(f_merge, [(proj, O_GA // D, D), (proj, O_GB // D, D), y_a, y_b], [], [(D, BF16)], name="merge")
    out = _mm(merged, w_out_f, name="out_proj")

    def f_loss(x_, out_, t_, mg, bg, gf_):
        gate = mg + bg
        x2 = x_ + gate * out_
        r2 = _rms(x2)
        xn2 = x2 * r2
        err = xn2 * gf_ - t_
        loss = jnp.full((1, LANES), 0.5 / D, F32) * jnp.sum(err * err)
        dy = err * (1.0 / D)
        dx2 = _rms_bwd(xn2, r2, dy * gf_)
        return (dx2, dx2 * gate), (loss, _colsum(dy * xn2), _colsum(dx2 * out_))

    dx2, d_out, loss_part, d_gf, d_gate = _rowwise(
        f_loss, [x2d, out, tgt], [mod_gate, b_gate, gf], [(D, F32), (D, BF16)], [LANES, D, D], name="loss_head")

    d_merged = _mm(d_out, w_out_f, tb=True, name="d_merged")
    dw_out = _mm(merged, d_out, ta=True, name="dw_out")

    def f_dmerge(dm, ga, gb, ya, yb):
        sa, sb = _sigmoid(ga), _sigmoid(gb)
        return (dm * sa, dm * sb, dm * ya * sa * (1.0 - sa), dm * yb * sb * (1.0 - sb)), ()

    d_ya, d_yb, d_ga, d_gb = _rowwise(f_dmerge, [d_merged, (proj, O_GA // D, D), (proj, O_GB // D, D), y_a, y_b], [],
                                      [(D, BF16)] * 4, name="d_merge")
    dw_a = _mm(ya_in, d_ya, ta=True, name="dw_branch_a")
    dw_b = _mm(yb_in, d_yb, ta=True, name="dw_branch_b")
    d_ya_in = _mm(d_ya, w_a_f, tb=True, name="d_branch_a")
    d_yb_in = _mm(d_yb, w_b_f, tb=True, name="d_branch_b")

    def f_dgate(da, oa, za, db, ob, zb):
        return (da * _silu(za), da * oa * _silu_grad(za), db * _silu(zb), db * ob * _silu_grad(zb)), ()

    d_oa, d_sbz, d_ob, d_mlaz = _rowwise(
        f_dgate, [d_ya_in, o_a, (proj, O_SBZ // SB_W, SB_W), d_yb_in, o_b, (proj, O_MLAZ // MLA_W, MLA_W)], [],
        [(SB_W, F32), (SB_W, BF16), (MLA_W, F32), (MLA_W, BF16)], name="d_branch_gate")

    d_sbq, d_sbk, d_sbv = _sb_backward(proj, sb_tails, d_oa, s_len)
    dqn, dqpe4, dkn, dkpe4, dv_b = _mla_backward(q_mla, q_pe, kv, k_pe, o_b, lse, d_ob, s_len)

    def f_drope(dqn_, dkn_, dv_, q0, q1, q2, q3, k0, k1, k2, k3, cs, sn):
        dq = (q0 + q1) + (q2 + q3)
        dk = (k0 + k1) + (k2 + k3)
        dq1, dq2 = dq[:, :LANES], dq[:, LANES:]
        dk1, dk2 = dk[:, :LANES], dk[:, LANES:]
        return (jnp.concatenate([dqn_, dq1 * cs + dq2 * sn, dq2 * cs - dq1 * sn], axis=1),
                jnp.concatenate([dkn_, dv_], axis=1),
                jnp.concatenate([dk1 * cs + dk2 * sn, dk2 * cs - dk1 * sn], axis=1)), ()

    pe_rows = [(dqpe4, 0, 2 * LANES, t * s_len) for t in range(4)] + [(dkpe4, 0, 2 * LANES, t * s_len) for t in range(4)]
    dq_k, dkv_k, d_krot = _rowwise(f_drope, [dqn, dkn, dv_b] + pe_rows + [cos_t, sin_t], [],
                                   [(HEADS * QK_DIM, BF16), (2 * MLA_W, BF16), (2 * LANES, BF16)], name="d_rope")
    dw_uq_k = _mm(cq_n, dq_k, ta=True, name="dw_uq")
    dw_ukv_k = _mm(ckv_n, dkv_k, ta=True, name="dw_ukv")
    d_cqn = _mm(dq_k, w_uq_k, tb=True, name="d_cq_norm")
    d_ckvn = _mm(dkv_k, w_ukv_k, tb=True, name="d_ckv_norm")

    def f_dlat(cq, dcqn, ckv, dckvn, gq_, gkv_):
        rq, rkv = _rms(cq), _rms(ckv)
        cqn, ckvn = cq * rq, ckv * rkv
        return ((_rms_bwd(cqn, rq, dcqn * gq_), _rms_bwd(ckvn, rkv, dckvn * gkv_)),
                (_colsum(dcqn * cqn), _colsum(dckvn * ckvn)))

    d_cq, d_ckv, d_gq, d_gkv = _rowwise(
        f_dlat, [(proj, O_CQ // Q_RANK, Q_RANK), d_cqn, (proj, O_CKV // KV_RANK, KV_RANK), d_ckvn], [gq, gkv],
        [(Q_RANK, BF16), (KV_RANK, BF16)], [Q_RANK, KV_RANK], name="d_latent_norm")

    d_proj = jnp.concatenate([d_ga, d_gb, d_sbq.astype(BF16), d_sbk.astype(BF16), d_sbv.astype(BF16), d_sbz, d_mlaz,
                              d_cq, jnp.zeros((s_len, LANES), BF16), d_ckv, d_krot], axis=1)
    dw_in_kt = _mm(d_proj, h, ta=True, name="dw_in")
    dh = _mm(d_proj, w_in_kt, name="d_h")

    def f_dx(x_, dh_, dx2_, g1_, msc, bsc):
        r = _rms(x_)
        xn = x_ * r
        dn1 = dh_ * (1.0 + (msc + bsc))
        return ((dx2_ + _rms_bwd(xn, r, dn1 * g1_),),
                (_colsum(dh_), _colsum(dh_ * (xn * g1_)), _colsum(dn1 * xn)))

    grad_x2d, d_shift, d_scale, d_g1 = _rowwise(f_dx, [x2d, dh, dx2], [g1, mod_scale, b_scale], [(D, F32)],
                                                [D, D, D], name="d_ada_norm")

    def krot_body(t_ref, o_ref):
        half = ROPE // 2
        for part in range(2):
            acc = t_ref[part * LANES:part * LANES + half, :]
            for hh in range(1, HEADS):
                acc = acc + t_ref[part * LANES + hh * half:part * LANES + (hh + 1) * half, :]
            o_ref[part * half:(part + 1) * half, :] = acc

    dw_krot = pl.pallas_call(krot_body, name="dw_krot_sum", out_shape=jax.ShapeDtypeStruct((ROPE, D), F32))(
        dw_in_kt[O_KROT:O_KROT + 2 * LANES])

    dw_in_t = _w_in_t_from_kernel_layout(dw_in_kt, dw_krot)
    dw_uq_full = _uq_from_kernel_layout(dw_uq_k)
    dw_ukv_full = _ukv_from_kernel_layout(dw_ukv_k)

    def col_blocks(g):
        kdim, n8 = g.shape
        return g.astype(BF16).reshape(kdim, N_DEV, n8 // N_DEV).transpose(1, 0, 2).reshape(N_DEV, -1, LANES)

    g_blocks = jnp.concatenate([col_blocks(dw_uq_full), col_blocks(dw_ukv_full), col_blocks(dw_a), col_blocks(dw_b),
                                dw_out.astype(BF16).reshape(N_DEV, -1, LANES)], axis=1)
    g_in_blocks = dw_in_t.astype(BF16).reshape(N_DEV, -1, D)
    g_in_recv, g_recv = _exchange_parts([g_in_blocks, g_blocks], "exchange_grads")
    g_in_sum_t = _sum_blocks(g_in_recv, "sum_grads_w_in")
    g_sum = _sum_blocks(g_recv, "sum_grads")
    g_big = [g_sum[offs[t]:offs[t + 1]].reshape(big[t].shape) for t in range(5)]

    small = jnp.concatenate([d_shift, d_scale, d_gate, d_g1, d_gq, d_gkv, d_gf, loss_part], axis=1)
    n_small = small.shape[1]
    pad = (-n_small) % (8 * LANES)
    small = jnp.pad(small, ((0, 0), (0, pad))).reshape(-1, LANES)
    small_all = _all_gather(small, "gather_small")
    small_sum = _sum_blocks(small_all, "sum_small").reshape(1, -1)
    g_b_ada = small_sum[:, :3 * D]
    g_g1 = small_sum[:, 3 * D:4 * D]
    g_gq = small_sum[:, 4 * D:4 * D + Q_RANK]
    g_gkv = small_sum[:, 4 * D + Q_RANK:4 * D + Q_RANK + KV_RANK]
    g_gf = small_sum[:, 4 * D + Q_RANK + KV_RANK:4 * D + Q_RANK + KV_RANK + D]

    dmod_all = small_all.reshape(N_DEV, -1)[:, :3 * D]
    dmod_cols = lax.dynamic_slice_in_dim(dmod_all, me * (3 * D // N_DEV), 3 * D // N_DEV, axis=1)
    g_w_ada = _mm(c_all, dmod_cols, ta=True, name="dw_ada")

    loss = small_sum[0, n_small - LANES]

    names = ["w_ada", "b_ada", "norm_gain", "w_in", "q_norm_gain", "w_uq", "kv_norm_gain", "w_ukv",
             "w_branch_a", "w_branch_b", "w_out", "final_norm_gain"]
    weights = dict(w_ada=w_ada, b_ada=b_ada, norm_gain=norm_gain, w_in=w_in, q_norm_gain=q_norm_gain, w_uq=w_uq,
                   kv_norm_gain=kv_norm_gain, w_ukv=w_ukv, w_branch_a=w_branch_a, w_branch_b=w_branch_b, w_out=w_out,
                   final_norm_gain=final_norm_gain)
    moms = dict(w_ada=m_w_ada, b_ada=m_b_ada, norm_gain=m_norm_gain, w_in=m_w_in, q_norm_gain=m_q_norm_gain,
                w_uq=m_w_uq, kv_norm_gain=m_kv_norm_gain, w_ukv=m_w_ukv, w_branch_a=m_w_branch_a,
                w_branch_b=m_w_branch_b, w_out=m_w_out, final_norm_gain=m_final_norm_gain)
    vels = dict(w_ada=v_w_ada, b_ada=v_b_ada, norm_gain=v_norm_gain, w_in=v_w_in, q_norm_gain=v_q_norm_gain,
                w_uq=v_w_uq, kv_norm_gain=v_kv_norm_gain, w_ukv=v_w_ukv, w_branch_a=v_w_branch_a,
                w_branch_b=v_w_branch_b, w_out=v_w_out, final_norm_gain=v_final_norm_gain)
    grads2d = dict(w_ada=g_w_ada, b_ada=g_b_ada, norm_gain=g_g1, w_in=g_in_sum_t, q_norm_gain=g_gq, w_uq=g_big[0],
                   kv_norm_gain=g_gkv, w_ukv=g_big[1], w_branch_a=g_big[2], w_branch_b=g_big[3], w_out=g_big[4],
                   final_norm_gain=g_gf)

    grads, deltas, new_m, new_v = [], [], [], []
    for n in names:
        w = weights[n]
        if n == "w_in":
            to2d = lambda t: t[0].T
            back = lambda t: t.T[None]
        else:
            shape2d = grads2d[n].shape
            to2d = lambda t, s=shape2d: t.reshape(s)
            back = lambda t, s=w.shape: t.reshape(s)
        d_, m_, v_ = _adamw_call(to2d(w), grads2d[n], to2d(moms[n]), to2d(vels[n]), "adamw_" + n)
        grads.append(back(grads2d[n]))
        deltas.append(back(d_))
        new_m.append(back(m_))
        new_v.append(back(v_))

    return (loss, grad_x2d.reshape(x.shape), *grads, *deltas, *new_m, *new_v)
```

```python
import functools
import math

import jax
import jax.numpy as jnp
from jax import lax
from jax.experimental import pallas as pl
from jax.experimental.pallas import tpu as pltpu

F32 = jnp.float32
BF16 = jnp.bfloat16
MXU_DTYPE = jnp.bfloat16

N_DEV = 8
D = 1024
HEADS = 8
HD = 64
SB_W = 512
MLA_W = 512
Q_RANK = 384
KV_RANK = 256
ROPE = 32
NOPE = 64
QK_DIM = NOPE + ROPE
EPS = 1e-6
ROPE_BASE = 10000.0

ADAM_LR = 0.001
ADAM_B1 = 0.9
ADAM_B2 = 0.999
ADAM_EPS = 1e-08
ADAM_WD = 0.01
ADAM_STEP = 10

LANES = 128
VMEM_LIMIT = 48 * 1024 * 1024

O_GA, O_GB = 0, 1024
O_SBQ, O_SBK, O_SBV, O_SBZ = 2048, 2560, 3072, 3584
O_MLAZ = 4096
O_CQ = 4608
O_CKV = 5120
O_KROT = 5376
IN_PAD = 5632

BQ = 256
BK = 256


def _cparams(*sem):
    return pltpu.CompilerParams(dimension_semantics=sem, vmem_limit_bytes=VMEM_LIMIT)


def _tile_of(n, cap=512):
    if n <= cap:
        return n
    for t in (1024, 768, 512, 384, 256, 128):
        if t <= cap and n % t == 0:
            return t
    raise ValueError(n)


def _rowwise(fn, rows, vecs, outs, reds=(), *, name, tile=256):
    norm = []
    for r in rows:
        if isinstance(r, tuple):
            arr, cb, w = r[:3]
            ro = r[3] if len(r) > 3 else 0
        else:
            arr, cb, w, ro = r, 0, r.shape[1], 0
        norm.append((arr, cb, w, ro))
    s_len = norm[0][0].shape[0]
    tile = min(tile, s_len)
    assert s_len % tile == 0
    n_row, n_vec, n_out, n_red = len(norm), len(vecs), len(outs), len(reds)

    def body(*refs):
        step = pl.program_id(0)
        row_refs = refs[:n_row]
        vec_refs = refs[n_row:n_row + n_vec]
        out_refs = refs[n_row + n_vec:n_row + n_vec + n_out]
        red_refs = refs[n_row + n_vec + n_out:]
        row_res, red_res = fn(*[r[...] for r in row_refs], *[v[...] for v in vec_refs])
        for o, val in zip(out_refs, row_res):
            o[...] = val.astype(o.dtype)
        if n_red:
            @pl.when(step == 0)
            def _():
                for r in red_refs:
                    r[...] = jnp.zeros(r.shape, r.dtype)
            for r, val in zip(red_refs, red_res):
                r[...] += val

    in_specs = []
    for arr, cb, w, ro in norm:
        in_specs.append(pl.BlockSpec((tile, w), functools.partial(lambda i, cb, rb: (i + rb, cb), cb=cb, rb=ro // tile)))
        assert ro % tile == 0
    for v in vecs:
        in_specs.append(pl.BlockSpec(v.shape, lambda i: (0, 0)))
    out_shape = [jax.ShapeDtypeStruct((s_len, w), dt) for w, dt in outs]
    out_specs = [pl.BlockSpec((tile, w), lambda i: (i, 0)) for w, _ in outs]
    out_shape += [jax.ShapeDtypeStruct((1, w), F32) for w in reds]
    out_specs += [pl.BlockSpec((1, w), lambda i: (0, 0)) for w in reds]
    res = pl.pallas_call(
        body, name=name, grid=(s_len // tile,), in_specs=in_specs, out_specs=out_specs, out_shape=out_shape,
        compiler_params=_cparams("arbitrary" if n_red else "parallel"),
    )(*[a for a, _, _, _ in norm], *vecs)
    return res


def _mm(a, b, *, ta=False, tb=False, out_dtype=F32, name):
    m, k = (a.shape[1], a.shape[0]) if ta else a.shape
    n = b.shape[0] if tb else b.shape[1]
    assert (b.shape[1] if tb else b.shape[0]) == k
    tm, tn, tk = _tile_of(m, 1024), _tile_of(n, 1024 if n <= 1024 else 512), _tile_of(k, 1024)
    nk = k // tk
    dims = (((0 if ta else 1,), (1 if tb else 0,)), ((), ()))

    def body(a_ref, b_ref, o_ref, acc_ref):
        kk = pl.program_id(2)

        @pl.when(kk == 0)
        def _():
            acc_ref[...] = jnp.zeros(acc_ref.shape, F32)

        acc_ref[...] += lax.dot_general(a_ref[...].astype(MXU_DTYPE), b_ref[...].astype(MXU_DTYPE), dims,
                                        preferred_element_type=F32)

        @pl.when(kk == nk - 1)
        def _():
            o_ref[...] = acc_ref[...].astype(o_ref.dtype)

    a_spec = pl.BlockSpec((tk, tm), lambda i, j, kk: (kk, i)) if ta else pl.BlockSpec((tm, tk), lambda i, j, kk: (i, kk))
    b_spec = pl.BlockSpec((tn, tk), lambda i, j, kk: (j, kk)) if tb else pl.BlockSpec((tk, tn), lambda i, j, kk: (kk, j))
    return pl.pallas_call(
        body, name=name, grid=(m // tm, n // tn, nk), in_specs=[a_spec, b_spec],
        out_specs=pl.BlockSpec((tm, tn), lambda i, j, kk: (i, j)),
        out_shape=jax.ShapeDtypeStruct((m, n), out_dtype),
        scratch_shapes=[pltpu.VMEM((tm, tn), F32)],
        compiler_params=_cparams("parallel", "parallel", "arbitrary"),
    )(a, b)


_NT = (((1,), (1,)), ((), ()))
_TN = (((0,), (0,)), ((), ()))


def _dot(a, b):
    return jnp.dot(a, b, preferred_element_type=F32)


def _dot_nt(a, b):
    return lax.dot_general(a, b, _NT, preferred_element_type=F32)


def _dot_tn(a, b):
    return lax.dot_general(a, b, _TN, preferred_element_type=F32)


def _running_sum(x, tri):
    return _dot(x.astype(MXU_DTYPE), tri)


def _neg_softplus(z):
    u = jnp.exp2(jnp.abs(z) * (-1.0 / math.log(2.0)))
    return -jnp.maximum(z, 0.0) - jnp.log(1.0 + u)


GROUP = 2


def _walk_blocks(step, st, n, descending):
    groups = n // GROUP

    def many(t, s):
        first = (n - 1 - GROUP * t) if descending else GROUP * t
        return step([first - g if descending else first + g for g in range(GROUP)], s)

    def one(r, s):
        done = GROUP * groups + r
        return step([n - 1 - done if descending else done], s)

    st = lax.fori_loop(0, groups, many, st)
    if (BQ // BK) % GROUP == 0:
        return st
    return lax.fori_loop(0, n - GROUP * groups, one, st)


def _chains(js):
    return [(h, t) for t in range(len(js)) for h in range(2)]


def _rowsum(x):
    return jnp.sum(x, axis=1, keepdims=True)


def _attn_consts():
    row = lax.broadcasted_iota(jnp.int32, (BQ, BK), 0)
    col = lax.broadcasted_iota(jnp.int32, (BQ, BK), 1)
    trow = lax.broadcasted_iota(jnp.int32, (BK, BK), 0)
    tcol = lax.broadcasted_iota(jnp.int32, (BK, BK), 1)
    lane = lax.broadcasted_iota(jnp.int32, (BQ, LANES), 1)
    klane = lax.broadcasted_iota(jnp.int32, (BK, LANES), 1)
    return row, col, trow, tcol, lane, klane


DIAG = BQ // BK
assert BQ == DIAG * BK


def _diag_block(i):
    return i * DIAG


def _diag_blocks(jd, descending):
    places = list(reversed(range(DIAG))) if descending else list(range(DIAG))
    return [jd + g for g in places], places


def _key_slice(j):
    return pl.ds(pl.multiple_of(j * BK, BK), BK)


def _sb_forward(proj, s_len):
    nq = s_len // BQ
    assert s_len // BK <= HD
    qc, kc, vc = O_SBQ // LANES, O_SBK // LANES, O_SBV // LANES

    def body(q_ref, k_ref, v_ref, o_ref, tails_ref):
        i = pl.program_id(1)
        jd = _diag_block(i)
        row, col, trow, tcol, lane, klane = _attn_consts()
        strict = [col + g * BK < row for g in range(DIAG)]
        tri = (trow >= tcol).astype(MXU_DTYPE)
        q = q_ref[...] * 0.125
        qh = [jnp.where(lane < HD, q, 0.0).astype(MXU_DTYPE), jnp.where(lane >= HD, q, 0.0).astype(MXU_DTYPE)]

        km = [klane < HD, klane >= HD]

        def step(js, st, diag):
            carry, acc, tail = st
            chains = _chains(js)
            kj = [k_ref[_key_slice(j), :].astype(MXU_DTYPE) for j in js]
            vj = [v_ref[_key_slice(j), :].astype(MXU_DTYPE) for j in js]
            z = {(h, t): _dot_nt(qh[h], kj[t]) for h, t in chains}
            run = list(carry)
            suf, carry_in = {}, {}
            for h, t in chains:
                lom = _neg_softplus(z[h, t])
                if diag:
                    lom = jnp.where(strict[diag[t]], lom, 0.0)
                suf[h, t] = _running_sum(lom, tri)
                carry_in[h, t] = run[h]
                run[h] = run[h] + _rowsum(lom)
            for h, t in chains:
                a = jnp.exp(z[h, t] + suf[h, t] + carry_in[h, t])
                if diag:
                    a = jnp.where(strict[diag[t]], a, 0.0)
                acc = acc + _dot(a.astype(MXU_DTYPE), jnp.where(km[h], vj[t], 0))
                tail = jnp.where(lane == h * HD + js[t], carry_in[h, t], tail)
            return tuple(run), acc, tail

        zero = jnp.zeros((BQ, LANES), F32)
        diag_js, places = _diag_blocks(jd, True)
        st = step(diag_js, ((jnp.zeros((BQ, 1), F32),) * 2, zero, zero), places)
        st = _walk_blocks(lambda js, s: step(js, s, False), st, jd, True)
        o_ref[...] = st[1]
        tails_ref[...] = st[2]

    blk = pl.BlockSpec((BQ, LANES), lambda p, i: (i, p))
    out = jax.ShapeDtypeStruct((s_len, SB_W), F32)
    return pl.pallas_call(
        body, name="sb_fwd", grid=(HEADS // 2, nq),
        in_specs=[pl.BlockSpec((BQ, LANES), lambda p, i: (i, qc + p)),
                  pl.BlockSpec((s_len, LANES), lambda p, i: (0, kc + p)),
                  pl.BlockSpec((s_len, LANES), lambda p, i: (0, vc + p))],
        out_specs=[blk, blk], out_shape=[out, out],
        compiler_params=_cparams("parallel", "parallel"),
    )(proj, proj, proj)


def _sb_backward(proj, tails, do, s_len):
    nq = s_len // BQ
    qc, kc, vc = O_SBQ // LANES, O_SBK // LANES, O_SBV // LANES

    def body(q_ref, k_ref, v_ref, tails_ref, do_ref, dq_ref, dk_ref, dv_ref):
        i = pl.program_id(1)
        jd = _diag_block(i)

        @pl.when(i == 0)
        def _():
            dk_ref[...] = jnp.zeros(dk_ref.shape, F32)
            dv_ref[...] = jnp.zeros(dv_ref.shape, F32)

        row, col, trow, tcol, lane, klane = _attn_consts()
        strict = [col + g * BK < row for g in range(DIAG)]
        tri = (trow >= tcol).astype(MXU_DTYPE)
        tri_p = (trow <= tcol).astype(MXU_DTYPE)
        q = q_ref[...] * 0.125
        tails_blk = tails_ref[...]
        do_blk = do_ref[...]
        hm = [lane < HD, lane >= HD]
        km = [klane < HD, klane >= HD]
        qh = [jnp.where(m, q, 0.0).astype(MXU_DTYPE) for m in hm]
        doh = [jnp.where(m, do_blk, 0.0).astype(MXU_DTYPE) for m in hm]

        def step(js, st, diag):
            before, dq = st
            chains = _chains(js)
            kj = [k_ref[_key_slice(j), :].astype(MXU_DTYPE) for j in js]
            vj = [v_ref[_key_slice(j), :].astype(MXU_DTYPE) for j in js]
            z = {(h, t): _dot_nt(qh[h], kj[t]) for h, t in chains}
            da = {(h, t): _dot_nt(doh[h], vj[t]) for h, t in chains}
            suf, sig = {}, {}
            for h, t in chains:
                lom = _neg_softplus(z[h, t])
                if diag:
                    lom = jnp.where(strict[diag[t]], lom, 0.0)
                suf[h, t] = _running_sum(lom, tri)
                sig[h, t] = jnp.exp(z[h, t] + lom)
            run = list(before)
            dl, pre, before_in = {}, {}, {}
            dk_add, dv_add = [None] * len(js), [None] * len(js)
            for h, t in chains:
                tail = _rowsum(jnp.where(lane == h * HD + js[t], tails_blk, 0.0))
                a = jnp.exp(z[h, t] + suf[h, t] + tail)
                if diag:
                    a = jnp.where(strict[diag[t]], a, 0.0)
                dl[h, t] = da[h, t] * a
                pre[h, t] = _dot(dl[h, t].astype(MXU_DTYPE), tri_p)
                dv_h = _dot_tn(a.astype(MXU_DTYPE), doh[h])
                dv_add[t] = dv_h if dv_add[t] is None else dv_add[t] + dv_h
                before_in[h, t] = run[h]
                run[h] = run[h] + _rowsum(dl[h, t])
            for h, t in chains:
                upto = before_in[h, t] + pre[h, t]
                dz = dl[h, t] - sig[h, t] * upto
                if diag:
                    dz = jnp.where(strict[diag[t]], dz, 0.0)
                dzb = dz.astype(MXU_DTYPE)
                dq = dq + _dot(dzb, jnp.where(km[h], kj[t], 0))
                dk_h = _dot_tn(dzb, qh[h])
                dk_add[t] = dk_h if dk_add[t] is None else dk_add[t] + dk_h
            for t, j in enumerate(js):
                dk_ref[_key_slice(j), :] += dk_add[t]
                dv_ref[_key_slice(j), :] += dv_add[t]
            return tuple(run), dq

        st = ((jnp.zeros((BQ, 1), F32),) * 2, jnp.zeros((BQ, LANES), F32))
        st = _walk_blocks(lambda js, s: step(js, s, False), st, jd, False)
        diag_js, places = _diag_blocks(jd, False)
        st = step(diag_js, st, places)
        dq_ref[...] = st[1] * 0.125

    blk = pl.BlockSpec((BQ, LANES), lambda p, i: (i, p))
    full = pl.BlockSpec((s_len, LANES), lambda p, i: (0, p))
    out = jax.ShapeDtypeStruct((s_len, SB_W), F32)
    return pl.pallas_call(
        body, name="sb_bwd", grid=(HEADS // 2, nq),
        in_specs=[pl.BlockSpec((BQ, LANES), lambda p, i: (i, qc + p)),
                  pl.BlockSpec((s_len, LANES), lambda p, i: (0, kc + p)),
                  pl.BlockSpec((s_len, LANES), lambda p, i: (0, vc + p)),
                  blk, blk],
        out_specs=[blk, full, full], out_shape=[out, out, out],
        compiler_params=_cparams("arbitrary", "arbitrary"),
    )(proj, proj, proj, tails, do)


def _pair_mask(rows, h):
    lane = lax.broadcasted_iota(jnp.int32, (rows, 2 * LANES), 1)
    rot = lane - LANES
    return (((lane < LANES) & (lane // HD == h))
            | ((lane >= LANES) & (rot < 2 * ROPE) & ((rot // (ROPE // 2)) % 2 == h)))


def _mla_forward(q_cat, k_cat, kv, s_len):
    nq = s_len // BQ
    scale = 1.0 / math.sqrt(QK_DIM)

    def body(q_ref, k_ref, v_ref, o_ref, lse_ref):
        i = pl.program_id(1)
        jd = _diag_block(i)
        row, col, trow, tcol, lane, klane = _attn_consts()
        causal = [col + g * BK <= row for g in range(DIAG)]
        q = q_ref[...]
        hm = [lane < HD, lane >= HD]
        km = [klane < HD, klane >= HD]
        qh = [jnp.where(_pair_mask(BQ, h), q, 0) for h in range(2)]

        def step(js, st, diag):
            m_run, l_run, acc = st
            chains = _chains(js)
            kj = [k_ref[_key_slice(j), :] for j in js]
            vj = [v_ref[_key_slice(j), :].astype(MXU_DTYPE) for j in js]
            s = {}
            for h, t in chains:
                s[h, t] = _dot_nt(qh[h], kj[t]) * scale
                if diag:
                    s[h, t] = jnp.where(causal[diag[t]], s[h, t], -jnp.inf)
            m_new, alpha, l_new = [], [], []
            for h in range(2):
                top = m_run[h]
                for t in range(len(js)):
                    top = jnp.maximum(top, jnp.max(s[h, t], axis=1, keepdims=True))
                m_new.append(top)
                alpha.append(jnp.exp(m_run[h] - top))
                l_new.append(alpha[h] * l_run[h])
            add = None
            for h, t in chains:
                pr = jnp.exp(s[h, t] - m_new[h])
                l_new[h] = l_new[h] + _rowsum(pr)
                part = _dot(pr.astype(MXU_DTYPE), jnp.where(km[h], vj[t], 0))
                add = part if add is None else add + part
            acc = jnp.where(hm[0], alpha[0], alpha[1]) * acc + add
            return tuple(m_new), tuple(l_new), acc

        st = ((jnp.full((BQ, 1), -1e30, F32),) * 2, (jnp.zeros((BQ, 1), F32),) * 2, jnp.zeros((BQ, LANES), F32))
        diag_js, places = _diag_blocks(jd, True)
        st = step(diag_js, st, places)
        m_run, l_run, acc = _walk_blocks(lambda js, s: step(js, s, False), st, jd, True)
        o_ref[...] = acc / jnp.where(hm[0], l_run[0], l_run[1])
        lse_ref[...] = jnp.where(hm[0], m_run[0] + jnp.log(l_run[0]), m_run[1] + jnp.log(l_run[1]))

    blk = pl.BlockSpec((BQ, LANES), lambda p, i: (i, p))
    out = jax.ShapeDtypeStruct((s_len, MLA_W), F32)
    return pl.pallas_call(
        body, name="mla_fwd", grid=(HEADS // 2, nq),
        in_specs=[pl.BlockSpec((BQ, 2 * LANES), lambda p, i: (i, p)),
                  pl.BlockSpec((s_len, 2 * LANES), lambda p, i: (0, p)),
                  pl.BlockSpec((s_len, LANES), lambda p, i: (0, MLA_W // LANES + p))],
        out_specs=[blk, blk], out_shape=[out, out],
        compiler_params=_cparams("parallel", "parallel"),
    )(q_cat, k_cat, kv)


def _mla_backward(q_cat, k_cat, kv, o, lse, do, s_len):
    nq = s_len // BQ
    scale = 1.0 / math.sqrt(QK_DIM)

    def body(q_ref, k_ref, v_ref, o_ref, lse_ref, do_ref, dq_ref, dk_ref, dv_ref):
        i = pl.program_id(1)

        @pl.when(i == 0)
        def _():
            dk_ref[...] = jnp.zeros(dk_ref.shape, F32)
            dv_ref[...] = jnp.zeros(dv_ref.shape, F32)

        jd = _diag_block(i)
        row, col, trow, tcol, lane, klane = _attn_consts()
        causal = [col + g * BK <= row for g in range(DIAG)]
        q = q_ref[...]
        o_blk = o_ref[...]
        do_blk = do_ref[...]
        lse_blk = lse_ref[...]
        hm = [lane < HD, lane >= HD]
        kpm = [_pair_mask(BK, h) for h in range(2)]
        qh = [jnp.where(_pair_mask(BQ, h), q, 0) for h in range(2)]
        doh_f = [jnp.where(m, do_blk, 0.0) for m in hm]
        doh = [d.astype(MXU_DTYPE) for d in doh_f]
        delta = [jnp.sum(d * o_blk, axis=1, keepdims=True) for d in doh_f]
        lse_h = [jnp.sum(jnp.where(lane == h * HD, lse_blk, 0.0), axis=1, keepdims=True) for h in range(2)]

        def step(js, st, diag):
            dq = st
            chains = _chains(js)
            kj = [k_ref[_key_slice(j), :] for j in js]
            vj = [v_ref[_key_slice(j), :].astype(MXU_DTYPE) for j in js]
            s = {(h, t): _dot_nt(qh[h], kj[t]) for h, t in chains}
            dp = {(h, t): _dot_nt(doh[h], vj[t]) for h, t in chains}
            adds = [[None] * len(js) for _ in range(2)]

            def accumulate(slot, t, part):
                adds[slot][t] = part if adds[slot][t] is None else adds[slot][t] + part

            for h, t in chains:
                pr = jnp.exp(s[h, t] * scale - lse_h[h])
                if diag:
                    pr = jnp.where(causal[diag[t]], pr, 0.0)
                dsb = (pr * (dp[h, t] - delta[h]) * scale).astype(MXU_DTYPE)
                dq = dq + _dot(dsb, jnp.where(kpm[h], kj[t], 0))
                accumulate(0, t, _dot_tn(dsb, qh[h]))
                accumulate(1, t, _dot_tn(pr.astype(MXU_DTYPE), doh[h]))
            for t, j in enumerate(js):
                dk_ref[_key_slice(j), :] += adds[0][t]
                dv_ref[_key_slice(j), :] += adds[1][t]
            return dq

        diag_js, places = _diag_blocks(jd, True)
        st = step(diag_js, jnp.zeros((BQ, 2 * LANES), F32), places)
        dq_ref[...] = _walk_blocks(lambda js, s: step(js, s, False), st, jd, True)

    blk = pl.BlockSpec((BQ, LANES), lambda p, i: (i, p))
    full = pl.BlockSpec((s_len, LANES), lambda p, i: (0, p))
    out = jax.ShapeDtypeStruct((s_len, MLA_W), F32)
    out_cat = jax.ShapeDtypeStruct((s_len, 2 * MLA_W), F32)
    return pl.pallas_call(
        body, name="mla_bwd", grid=(HEADS // 2, nq),
        in_specs=[pl.BlockSpec((BQ, 2 * LANES), lambda p, i: (i, p)),
                  pl.BlockSpec((s_len, 2 * LANES), lambda p, i: (0, p)),
                  pl.BlockSpec((s_len, LANES), lambda p, i: (0, MLA_W // LANES + p)),
                  blk, blk, blk],
        out_specs=[pl.BlockSpec((BQ, 2 * LANES), lambda p, i: (i, p)),
                   pl.BlockSpec((s_len, 2 * LANES), lambda p, i: (0, p)), full],
        out_shape=[out_cat, out_cat, out],
        compiler_params=_cparams("arbitrary", "arbitrary"),
    )(q_cat, k_cat, kv, o, lse, do)


def _mesh_pos():
    return lax.axis_index("x"), lax.axis_index("y"), lax.axis_index("c")


def _dev_index(px, py, pc):
    return 4 * px + 2 * py + pc


def _all_gather(block, name):
    return _all_gather_parts([block], name)[0]


def _all_gather_parts(blocks, name):
    n = len(blocks)

    def body(*refs):
        x_refs, out_refs = refs[:n], refs[n:2 * n]
        send_sems, recv_sems, local_sems = refs[2 * n:]
        x, y, c = _mesh_pos()
        me, sibling = (x, y, c), (x, y, 1 - c)
        chips = [(1 - x, y), (x, 1 - y), (1 - x, 1 - y)]

        def copy(a, k, blockpos, to, src=None):
            slot = out_refs[a].at[_dev_index(*blockpos)]
            return pltpu.make_async_remote_copy(
                src_ref=slot if src is None else src, dst_ref=slot,
                send_sem=send_sems.at[7 * a + k], recv_sem=recv_sems.at[7 * a + k],
                device_id=to, device_id_type=pl.DeviceIdType.MESH)

        mine = [pltpu.make_async_copy(x_refs[a], out_refs[a].at[_dev_index(*me)], local_sems.at[a]) for a in range(n)]
        for cp in mine:
            cp.start()
        first = []
        for a in range(n):
            first.append(copy(a, 0, me, sibling, src=x_refs[a]))
            first += [copy(a, 1 + j, me, (*chip, c), src=x_refs[a]) for j, chip in enumerate(chips)]
        for cp in first:
            cp.start()
        passed = []
        for j, chip in enumerate(chips):
            for a in range(n):
                copy(a, 1 + j, (*chip, c), me).wait_recv()
                passed.append(copy(a, 4 + j, (*chip, c), sibling))
                passed[-1].start()
        for a in range(n):
            copy(a, 0, sibling, me).wait_recv()
            for j, chip in enumerate(chips):
                copy(a, 4 + j, (*chip, 1 - c), me).wait_recv()
        for cp in first + passed:
            cp.wait_send()
        for cp in mine:
            cp.wait()

    return pl.pallas_call(
        body, name=name,
        out_shape=[jax.ShapeDtypeStruct((N_DEV,) + b.shape, b.dtype) for b in blocks],
        in_specs=[pl.BlockSpec(memory_space=pl.ANY)] * n, out_specs=[pl.BlockSpec(memory_space=pl.ANY)] * n,
        scratch_shapes=[pltpu.SemaphoreType.DMA((7 * n,)), pltpu.SemaphoreType.DMA((7 * n,)),
                        pltpu.SemaphoreType.DMA((n,))],
    )(*blocks)


def _exchange_parts(arrays, name):
    n = len(arrays)

    def body(*refs):
        x_refs, out_refs = refs[:n], refs[n:2 * n]
        send_sems, recv_sems, local_sems = refs[2 * n:]
        x, y, c = _mesh_pos()
        me = _dev_index(x, y, c)
        flips = [(fx, fy, fc) for fx in (0, 1) for fy in (0, 1) for fc in (0, 1)][1:]
        peers = [(1 - x if fx else x, 1 - y if fy else y, 1 - c if fc else c) for fx, fy, fc in flips]
        mine = [pltpu.make_async_copy(x_refs[a].at[me], out_refs[a].at[me], local_sems.at[a]) for a in range(n)]
        for cp in mine:
            cp.start()
        sends = []
        for a in range(n):
            for k, peer in enumerate(peers):
                sends.append(pltpu.make_async_remote_copy(
                    src_ref=x_refs[a].at[_dev_index(*peer)], dst_ref=out_refs[a].at[me],
                    send_sem=send_sems.at[7 * a + k], recv_sem=recv_sems.at[7 * a + k],
                    device_id=peer, device_id_type=pl.DeviceIdType.MESH))
                sends[-1].start()
        for a in range(n):
            for k, peer in enumerate(peers):
                pltpu.make_async_remote_copy(
                    src_ref=x_refs[a].at[me], dst_ref=out_refs[a].at[_dev_index(*peer)],
                    send_sem=send_sems.at[7 * a + k], recv_sem=recv_sems.at[7 * a + k],
                    device_id=peer, device_id_type=pl.DeviceIdType.MESH).wait_recv()
        for cp in sends:
            cp.wait_send()
        for cp in mine:
            cp.wait()

    return pl.pallas_call(
        body, name=name,
        out_shape=[jax.ShapeDtypeStruct(a.shape, a.dtype) for a in arrays],
        in_specs=[pl.BlockSpec(memory_space=pl.ANY)] * n, out_specs=[pl.BlockSpec(memory_space=pl.ANY)] * n,
        scratch_shapes=[pltpu.SemaphoreType.DMA((7 * n,)), pltpu.SemaphoreType.DMA((7 * n,)),
                        pltpu.SemaphoreType.DMA((n,))],
    )(*arrays)


def _sum_blocks(parts, name):
    n, r, c = parts.shape
    row_tiles = [t for t in range(16, min(r, 2048) + 1, 16) if r % t == 0]
    if row_tiles:
        tr, tc = max(row_tiles), c
    else:
        tr, tc = r, 2 * LANES
    assert c % tc == 0

    def body(p_ref, o_ref):
        acc = p_ref[0].astype(F32)
        for s in range(1, n):
            acc = acc + p_ref[s].astype(F32)
        o_ref[...] = acc

    return pl.pallas_call(
        body, name=name, grid=(r // tr, c // tc),
        in_specs=[pl.BlockSpec((n, tr, tc), lambda i, j: (0, i, j))],
        out_specs=pl.BlockSpec((tr, tc), lambda i, j: (i, j)),
        out_shape=jax.ShapeDtypeStruct((r, c), F32),
        compiler_params=_cparams("parallel", "parallel"),
    )(parts)


def _sigmoid(x):
    return 1.0 / (1.0 + jnp.exp(-x))


def _silu(x):
    return x * _sigmoid(x)


def _silu_grad(x):
    s = _sigmoid(x)
    return s * (1.0 + x * (1.0 - s))


def _colsum(x):
    return jnp.sum(x, axis=0, keepdims=True)


def _rms(x):
    return lax.rsqrt(jnp.mean(x * x, axis=-1, keepdims=True) + EPS)


def _rms_bwd(xn, r, dxn):
    return r * (dxn - xn * jnp.mean(dxn * xn, axis=-1, keepdims=True))


def _adamw(w, g, m, v):
    m = ADAM_B1 * m + (1.0 - ADAM_B1) * g
    v = ADAM_B2 * v + (1.0 - ADAM_B2) * jnp.square(g)
    m_hat = m / (1.0 - ADAM_B1 ** ADAM_STEP)
    v_hat = v / (1.0 - ADAM_B2 ** ADAM_STEP)
    delta = -ADAM_LR * (m_hat / (jnp.sqrt(v_hat) + ADAM_EPS) + ADAM_WD * w)
    return delta, m, v


def _adamw_call(w, g, m, v, name):
    r, c = w.shape
    if r % 256 == 0:
        tr, tc = 256, c
    elif r * c <= 256 * 1024 or c % (2 * LANES):
        tr, tc = r, c
    else:
        tr, tc = r, 2 * LANES

    def body(w_ref, g_ref, m_ref, v_ref, d_out, m_out, v_out):
        d_out[...], m_out[...], v_out[...] = _adamw(w_ref[...], g_ref[...], m_ref[...], v_ref[...])

    spec = pl.BlockSpec((tr, tc), lambda i, j: (i, j))
    return pl.pallas_call(
        body, name=name, grid=(r // tr, c // tc), in_specs=[spec] * 4, out_specs=[spec] * 3,
        out_shape=[jax.ShapeDtypeStruct((r, c), F32)] * 3, compiler_params=_cparams("parallel", "parallel"),
    )(w, g, m, v)


def _uq_to_kernel_layout(w):
    lead = w.shape[:-1]
    t = w.reshape(lead + (HEADS, QK_DIM))
    return jnp.concatenate([t[..., :NOPE].reshape(lead + (HEADS * NOPE,)),
                            t[..., NOPE:NOPE + ROPE // 2].reshape(lead + (LANES,)),
                            t[..., NOPE + ROPE // 2:].reshape(lead + (LANES,))], axis=-1)


def _uq_from_kernel_layout(w):
    lead = w.shape[:-1]
    nope = w[..., :HEADS * NOPE].reshape(lead + (HEADS, NOPE))
    r1 = w[..., HEADS * NOPE:HEADS * NOPE + LANES].reshape(lead + (HEADS, ROPE // 2))
    r2 = w[..., HEADS * NOPE + LANES:].reshape(lead + (HEADS, ROPE // 2))
    return jnp.concatenate([nope, r1, r2], axis=-1).reshape(lead + (HEADS * QK_DIM,))


def _ukv_to_kernel_layout(w):
    lead = w.shape[:-1]
    t = w.reshape(lead + (HEADS, NOPE + HD))
    return jnp.concatenate([t[..., :NOPE].reshape(lead + (HEADS * NOPE,)),
                            t[..., NOPE:].reshape(lead + (HEADS * HD,))], axis=-1)


def _ukv_from_kernel_layout(w):
    lead = w.shape[:-1]
    kn = w[..., :HEADS * NOPE].reshape(lead + (HEADS, NOPE))
    vv = w[..., HEADS * NOPE:].reshape(lead + (HEADS, HD))
    return jnp.concatenate([kn, vv], axis=-1).reshape(lead + (HEADS * (NOPE + HD),))


def _w_in_t_to_kernel_layout(wt):
    sb = wt[0:2048]
    c_q = wt[2048:2432]
    c_kv = wt[2432:2688]
    k_rot = wt[2688:2720]
    mla_z = wt[2720:3232]
    gates = wt[3232:5280]
    zeros = jnp.zeros((LANES, wt.shape[1]), wt.dtype)
    k1 = jnp.tile(k_rot[:ROPE // 2], (HEADS, 1))
    k2 = jnp.tile(k_rot[ROPE // 2:], (HEADS, 1))
    return jnp.concatenate([gates, sb, mla_z, c_q, zeros, c_kv, k1, k2], axis=0)


def _w_in_t_from_kernel_layout(gt, g_rot):
    return jnp.concatenate([gt[O_SBQ:O_SBQ + 2048], gt[O_CQ:O_CQ + Q_RANK], gt[O_CKV:O_CKV + KV_RANK], g_rot,
                            gt[O_MLAZ:O_MLAZ + MLA_W], gt[O_GA:O_GA + 2 * D]], axis=0)


def kernel(x, c, positions, w_ada, b_ada, norm_gain, w_in, q_norm_gain, w_uq, kv_norm_gain, w_ukv, w_branch_a, w_branch_b, w_out, final_norm_gain, loss_target, m_w_ada, m_b_ada, m_norm_gain, m_w_in, m_q_norm_gain, m_w_uq, m_kv_norm_gain, m_w_ukv, m_w_branch_a, m_w_branch_b, m_w_out, m_final_norm_gain, v_w_ada, v_b_ada, v_norm_gain, v_w_in, v_q_norm_gain, v_w_uq, v_kv_norm_gain, v_w_ukv, v_w_branch_a, v_w_branch_b, v_w_out, v_final_norm_gain):
    s_len = x.shape[1]
    me = _dev_index(*_mesh_pos())
    x2d = x[0]
    tgt = loss_target[0]

    w_in_t = w_in[0].T.astype(BF16)
    big = [w_uq[0], w_ukv[0], w_branch_a[0], w_branch_b[0], w_out[0]]
    big_sizes = [int(w.size) for w in big]
    packed = jnp.concatenate([w.astype(BF16).reshape(-1, LANES) for w in big], axis=0)
    g_in_t, gathered = _all_gather_parts([w_in_t, packed], "gather_weights")
    offs = [0]
    for n in big_sizes:
        offs.append(offs[-1] + n // LANES)

    def unpack(t, shape):
        return gathered[:, offs[t]:offs[t + 1], :].reshape((N_DEV,) + shape)

    def cols(t, shape):
        return unpack(t, shape).transpose(1, 0, 2).reshape(shape[0], N_DEV * shape[1])

    w_in_kt = _w_in_t_to_kernel_layout(g_in_t.reshape(N_DEV * w_in_t.shape[0], D))
    w_uq_k = _uq_to_kernel_layout(cols(0, big[0].shape))
    w_ukv_k = _ukv_to_kernel_layout(cols(1, big[1].shape))
    w_a_f = cols(2, big[2].shape)
    w_b_f = cols(3, big[3].shape)
    w_out_f = unpack(4, big[4].shape).reshape(D, D)

    c_all = _all_gather(c.reshape(8, LANES), "gather_c").reshape(N_DEV, D)
    mod_cols = _mm(c_all, w_ada[0], name="ada_mod")
    mod_all = _all_gather(mod_cols, "gather_mod")
    mod = lax.dynamic_index_in_dim(mod_all, me, axis=1, keepdims=False).reshape(1, 3 * D)
    mod_shift, mod_scale, mod_gate = mod[:, :D], mod[:, D:2 * D], mod[:, 2 * D:]
    b_shift, b_scale, b_gate = b_ada[:, :D], b_ada[:, D:2 * D], b_ada[:, 2 * D:]
    g1 = norm_gain
    gq, gkv = q_norm_gain, kv_norm_gain
    gf = final_norm_gain.reshape(1, D)

    def f_h(x_, g1_, ms, bs, msc, bsc):
        xn = x_ * _rms(x_)
        return (xn * g1_ * (1.0 + (msc + bsc)) + (ms + bs),), ()

    (h,) = _rowwise(f_h, [x2d], [g1, mod_shift, b_shift, mod_scale, b_scale], [(D, BF16)], name="ada_norm")
    proj = _mm(h, w_in_kt, tb=True, name="proj_in")

    o_a, sb_tails = _sb_forward(proj, s_len)

    def f_lat(cq, ckv, gq_, gkv_):
        return (cq * _rms(cq) * gq_, ckv * _rms(ckv) * gkv_), ()

    cq_n, ckv_n = _rowwise(f_lat, [(proj, O_CQ // Q_RANK, Q_RANK), (proj, O_CKV // KV_RANK, KV_RANK)], [gq, gkv],
                           [(Q_RANK, BF16), (KV_RANK, BF16)], name="latent_norm")
    q_mla = _mm(cq_n, w_uq_k, name="q_up")
    kv = _mm(ckv_n, w_ukv_k, name="kv_up")

    inv_freq = ROPE_BASE ** (-jnp.arange(0, ROPE, 2, dtype=F32) / ROPE)
    inv_freq_t = jnp.tile(inv_freq, HEADS).reshape(1, LANES)
    pos_col = positions.reshape(s_len, 1).astype(F32)

    pairs = HEADS // 2

    def f_rope(pos, qn, q1, q2, kn, k1, k2, freq):
        ang = pos * freq
        cs, sn = jnp.cos(ang), jnp.sin(ang)
        q1r, q2r = q1 * cs - q2 * sn, q1 * sn + q2 * cs
        k1r, k2r = k1 * cs - k2 * sn, k1 * sn + k2 * cs
        lane = lax.broadcasted_iota(jnp.int32, q1.shape, 1)
        first, second = lane < ROPE, (lane >= ROPE) & (lane < 2 * ROPE)
        k_rot = jnp.where(first, k1r, jnp.where(second, k2r, 0.0))
        q_parts, k_parts = [], []
        for p in range(pairs):
            q_rot = jnp.where(first, pltpu.roll(q1r, (LANES - ROPE * p) % LANES, 1),
                              jnp.where(second, pltpu.roll(q2r, (LANES + ROPE - ROPE * p) % LANES, 1), 0.0))
            q_parts += [qn[:, LANES * p:LANES * (p + 1)], q_rot]
            k_parts += [kn[:, LANES * p:LANES * (p + 1)], k_rot]
        return (jnp.concatenate(q_parts, axis=1), jnp.concatenate(k_parts, axis=1), cs, sn), ()

    q_cat, k_cat, cos_t, sin_t = _rowwise(
        f_rope, [pos_col, (q_mla, 0, MLA_W), (q_mla, 4, LANES), (q_mla, 5, LANES), (kv, 0, MLA_W),
                 (proj, O_KROT // LANES, LANES), (proj, O_KROT // LANES + 1, LANES)], [inv_freq_t],
        [(2 * MLA_W, BF16), (2 * MLA_W, BF16), (LANES, F32), (LANES, F32)], name="rope")

    o_b, lse = _mla_forward(q_cat, k_cat, kv, s_len)

    def f_gate(oa, za, ob, zb):
        return (oa * _silu(za), ob * _silu(zb)), ()

    ya_in, yb_in = _rowwise(f_gate, [o_a, (proj, O_SBZ // SB_W, SB_W), o_b, (proj, O_MLAZ // MLA_W, MLA_W)], [],
                            [(SB_W, BF16), (MLA_W, BF16)], name="branch_gate")
    y_a = _mm(ya_in, w_a_f, name="branch_a")
    y_b = _mm(yb_in, w_b_f, name="branch_b")

    def f_merge(ga, gb, ya, yb):
        return (_sigmoid(ga) * ya + _sigmoid(gb) * yb,), ()

    (merged,) = _rowwise(f_merge, [(proj, O_GA // D, D), (proj, O_GB // D, D), y_a, y_b], [], [(D, BF16)], name="merge")
    out = _mm(merged, w_out_f, name="out_proj")

    def f_loss(x_, out_, t_, mg, bg, gf_):
        gate = mg + bg
        x2 = x_ + gate * out_
        r2 = _rms(x2)
        xn2 = x2 * r2
        err = xn2 * gf_ - t_
        loss = jnp.full((1, LANES), 0.5 / D, F32) * jnp.sum(err * err)
        dy = err * (1.0 / D)
        dx2 = _rms_bwd(xn2, r2, dy * gf_)
        return (dx2, dx2 * gate), (loss, _colsum(dy * xn2), _colsum(dx2 * out_))

    dx2, d_out, loss_part, d_gf, d_gate = _rowwise(
        f_loss, [x2d, out, tgt], [mod_gate, b_gate, gf], [(D, F32), (D, BF16)], [LANES, D, D], name="loss_head")

    d_merged = _mm(d_out, w_out_f, tb=True, name="d_merged")
    dw_out = _mm(merged, d_out, ta=True, name="dw_out")

    def f_dmerge(dm, ga, gb, ya, yb):
        sa, sb = _sigmoid(ga), _sigmoid(gb)
        return (dm * sa, dm * sb, dm * ya * sa * (1.0 - sa), dm * yb * sb * (1.0 - sb)), ()

    d_ya, d_yb, d_ga, d_gb = _rowwise(f_dmerge, [d_merged, (proj, O_GA // D, D), (proj, O_GB // D, D), y_a, y_b], [],
                                      [(D, BF16)] * 4, name="d_merge")
    dw_a = _mm(ya_in, d_ya, ta=True, name="dw_branch_a")
    dw_b = _mm(yb_in, d_yb, ta=True, name="dw_branch_b")
    d_ya_in = _mm(d_ya, w_a_f, tb=True, name="d_branch_a")
    d_yb_in = _mm(d_yb, w_b_f, tb=True, name="d_branch_b")

    def f_dgate(da, oa, za, db, ob, zb):
        return (da * _silu(za), da * oa * _silu_grad(za), db * _silu(zb), db * ob * _silu_grad(zb)), ()

    d_oa, d_sbz, d_ob, d_mlaz = _rowwise(
        f_dgate, [d_ya_in, o_a, (proj, O_SBZ // SB_W, SB_W), d_yb_in, o_b, (proj, O_MLAZ // MLA_W, MLA_W)], [],
        [(SB_W, F32), (SB_W, BF16), (MLA_W, F32), (MLA_W, BF16)], name="d_branch_gate")

    d_sbq, d_sbk, d_sbv = _sb_backward(proj, sb_tails, d_oa, s_len)
    dq_cat, dk_cat, dv_b = _mla_backward(q_cat, k_cat, kv, o_b, lse, d_ob, s_len)

    def f_drope(dq, dk, dv_, cs, sn):
        lane = lax.broadcasted_iota(jnp.int32, cs.shape, 1)
        first, second = lane < ROPE, (lane >= ROPE) & (lane < 2 * ROPE)
        dq1 = dq2 = dk1 = dk2 = None
        for p in range(pairs):
            q_rot = dq[:, LANES * (2 * p + 1):LANES * (2 * p + 2)]
            k_rot = dk[:, LANES * (2 * p + 1):LANES * (2 * p + 2)]
            parts = (pltpu.roll(jnp.where(first, q_rot, 0.0), (ROPE * p) % LANES, 1),
                     pltpu.roll(jnp.where(second, q_rot, 0.0), (LANES - ROPE + ROPE * p) % LANES, 1),
                     jnp.where(first, k_rot, 0.0), jnp.where(second, k_rot, 0.0))
            if p == 0:
                dq1, dq2, dk1, dk2 = parts
            else:
                dq1, dq2, dk1, dk2 = dq1 + parts[0], dq2 + parts[1], dk1 + parts[2], dk2 + parts[3]
        dqn_ = [dq[:, 2 * LANES * p:2 * LANES * p + LANES] for p in range(pairs)]
        dkn_ = [dk[:, 2 * LANES * p:2 * LANES * p + LANES] for p in range(pairs)]
        return (jnp.concatenate(dqn_ + [dq1 * cs + dq2 * sn, dq2 * cs - dq1 * sn], axis=1),
                jnp.concatenate(dkn_ + [dv_], axis=1),
                jnp.concatenate([dk1 * cs + dk2 * sn, dk2 * cs - dk1 * sn], axis=1)), ()

    dq_k, dkv_k, d_krot = _rowwise(f_drope, [dq_cat, dk_cat, dv_b, cos_t, sin_t], [],
                                   [(HEADS * QK_DIM, BF16), (2 * MLA_W, BF16), (2 * LANES, BF16)], name="d_rope")
    dw_uq_k = _mm(cq_n, dq_k, ta=True, name="dw_uq")
    dw_ukv_k = _mm(ckv_n, dkv_k, ta=True, name="dw_ukv")
    d_cqn = _mm(dq_k, w_uq_k, tb=True, name="d_cq_norm")
    d_ckvn = _mm(dkv_k, w_ukv_k, tb=True, name="d_ckv_norm")

    def f_dlat(cq, dcqn, ckv, dckvn, gq_, gkv_):
        rq, rkv = _rms(cq), _rms(ckv)
        cqn, ckvn = cq * rq, ckv * rkv
        return ((_rms_bwd(cqn, rq, dcqn * gq_), _rms_bwd(ckvn, rkv, dckvn * gkv_)),
                (_colsum(dcqn * cqn), _colsum(dckvn * ckvn)))

    d_cq, d_ckv, d_gq, d_gkv = _rowwise(
        f_dlat, [(proj, O_CQ // Q_RANK, Q_RANK), d_cqn, (proj, O_CKV // KV_RANK, KV_RANK), d_ckvn], [gq, gkv],
        [(Q_RANK, BF16), (KV_RANK, BF16)], [Q_RANK, KV_RANK], name="d_latent_norm")

    d_proj = jnp.concatenate([d_ga, d_gb, d_sbq.astype(BF16), d_sbk.astype(BF16), d_sbv.astype(BF16), d_sbz, d_mlaz,
                              d_cq, jnp.zeros((s_len, LANES), BF16), d_ckv, d_krot], axis=1)
    dw_in_kt = _mm(d_proj, h, ta=True, name="dw_in")
    dh = _mm(d_proj, w_in_kt, name="d_h")

    def f_dx(x_, dh_, dx2_, g1_, msc, bsc):
        r = _rms(x_)
        xn = x_ * r
        dn1 = dh_ * (1.0 + (msc + bsc))
        return ((dx2_ + _rms_bwd(xn, r, dn1 * g1_),),
                (_colsum(dh_), _colsum(dh_ * (xn * g1_)), _colsum(dn1 * xn)))

    grad_x2d, d_shift, d_scale, d_g1 = _rowwise(f_dx, [x2d, dh, dx2], [g1, mod_scale, b_scale], [(D, F32)],
                                                [D, D, D], name="d_ada_norm")

    def krot_body(t_ref, o_ref):
        half = ROPE // 2
        for part in range(2):
            acc = t_ref[part * LANES:part * LANES + half, :]
            for hh in range(1, HEADS):
                acc = acc + t_ref[part * LANES + hh * half:part * LANES + (hh + 1) * half, :]
            o_ref[part * half:(part + 1) * half, :] = acc

    dw_krot = pl.pallas_call(krot_body, name="dw_krot_sum", out_shape=jax.ShapeDtypeStruct((ROPE, D), F32))(
        dw_in_kt[O_KROT:O_KROT + 2 * LANES])

    dw_in_t = _w_in_t_from_kernel_layout(dw_in_kt, dw_krot)
    dw_uq_full = _uq_from_kernel_layout(dw_uq_k)
    dw_ukv_full = _ukv_from_kernel_layout(dw_ukv_k)

    def col_blocks(g):
        kdim, n8 = g.shape
        return g.astype(BF16).reshape(kdim, N_DEV, n8 // N_DEV).transpose(1, 0, 2).reshape(N_DEV, -1, LANES)

    g_blocks = jnp.concatenate([col_blocks(dw_uq_full), col_blocks(dw_ukv_full), col_blocks(dw_a), col_blocks(dw_b),
                                dw_out.astype(BF16).reshape(N_DEV, -1, LANES)], axis=1)
    g_in_blocks = dw_in_t.astype(BF16).reshape(N_DEV, -1, D)
    g_in_recv, g_recv = _exchange_parts([g_in_blocks, g_blocks], "exchange_grads")
    g_in_sum_t = _sum_blocks(g_in_recv, "sum_grads_w_in")
    g_sum = _sum_blocks(g_recv, "sum_grads")
    g_big = [g_sum[offs[t]:offs[t + 1]].reshape(big[t].shape) for t in range(5)]

    small = jnp.concatenate([d_shift, d_scale, d_gate, d_g1, d_gq, d_gkv, d_gf, loss_part], axis=1)
    n_small = small.shape[1]
    pad = (-n_small) % (8 * LANES)
    small = jnp.pad(small, ((0, 0), (0, pad))).reshape(-1, LANES)
    small_all = _all_gather(small, "gather_small")
    small_sum = _sum_blocks(small_all, "sum_small").reshape(1, -1)
    g_b_ada = small_sum[:, :3 * D]
    g_g1 = small_sum[:, 3 * D:4 * D]
    g_gq = small_sum[:, 4 * D:4 * D + Q_RANK]
    g_gkv = small_sum[:, 4 * D + Q_RANK:4 * D + Q_RANK + KV_RANK]
    g_gf = small_sum[:, 4 * D + Q_RANK + KV_RANK:4 * D + Q_RANK + KV_RANK + D]

    dmod_all = small_all.reshape(N_DEV, -1)[:, :3 * D]
    dmod_cols = lax.dynamic_slice_in_dim(dmod_all, me * (3 * D // N_DEV), 3 * D // N_DEV, axis=1)
    g_w_ada = _mm(c_all, dmod_cols, ta=True, name="dw_ada")

    loss = small_sum[0, n_small - LANES]

    names = ["w_ada", "b_ada", "norm_gain", "w_in", "q_norm_gain", "w_uq", "kv_norm_gain", "w_ukv",
             "w_branch_a", "w_branch_b", "w_out", "final_norm_gain"]
    weights = dict(w_ada=w_ada, b_ada=b_ada, norm_gain=norm_gain, w_in=w_in, q_norm_gain=q_norm_gain, w_uq=w_uq,
                   kv_norm_gain=kv_norm_gain, w_ukv=w_ukv, w_branch_a=w_branch_a, w_branch_b=w_branch_b, w_out=w_out,
                   final_norm_gain=final_norm_gain)
    moms = dict(w_ada=m_w_ada, b_ada=m_b_ada, norm_gain=m_norm_gain, w_in=m_w_in, q_norm_gain=m_q_norm_gain,
                w_uq=m_w_uq, kv_norm_gain=m_kv_norm_gain, w_ukv=m_w_ukv, w_branch_a=m_w_branch_a,
                w_branch_b=m_w_branch_b, w_out=m_w_out, final_norm_gain=m_final_norm_gain)
    vels = dict(w_ada=v_w_ada, b_ada=v_b_ada, norm_gain=v_norm_gain, w_in=v_w_in, q_norm_gain=v_q_norm_gain,
                w_uq=v_w_uq, kv_norm_gain=v_kv_norm_gain, w_ukv=v_w_ukv, w_branch_a=v_w_branch_a,
                w_branch_b=v_w_branch_b, w_out=v_w_out, final_norm_gain=v_final_norm_gain)
    grads2d = dict(w_ada=g_w_ada, b_ada=g_b_ada, norm_gain=g_g1, w_in=g_in_sum_t, q_norm_gain=g_gq, w_uq=g_big[0],
                   kv_norm_gain=g_gkv, w_ukv=g_big[1], w_branch_a=g_big[2], w_branch_b=g_big[3], w_out=g_big[4],
                   final_norm_gain=g_gf)

    grads, deltas, new_m, new_v = [], [], [], []
    for n in names:
        w = weights[n]
        if n == "w_in":
            to2d = lambda t: t[0].T
            back = lambda t: t.T[None]
        else:
            shape2d = grads2d[n].shape
            to2d = lambda t, s=shape2d: t.reshape(s)
            back = lambda t, s=w.shape: t.reshape(s)
        d_, m_, v_ = _adamw_call(to2d(w), grads2d[n], to2d(moms[n]), to2d(vels[n]), "adamw_" + n)
        grads.append(back(grads2d[n]))
        deltas.append(back(d_))
        new_m.append(back(m_))
        new_v.append(back(v_))

    return (loss, grad_x2d.reshape(x.shape), *grads, *deltas, *new_m, *new_v)
```

```python
import functools
import math

import jax
import jax.numpy as jnp
from jax import lax
from jax.experimental import pallas as pl
from jax.experimental.pallas import tpu as pltpu

F32 = jnp.float32
BF16 = jnp.bfloat16
MXU_DTYPE = jnp.bfloat16

N_DEV = 8
D = 1024
HEADS = 8
HD = 64
SB_W = 512
MLA_W = 512
Q_RANK = 384
KV_RANK = 256
ROPE = 32
NOPE = 64
QK_DIM = NOPE + ROPE
EPS = 1e-6
ROPE_BASE = 10000.0

ADAM_LR = 0.001
ADAM_B1 = 0.9
ADAM_B2 = 0.999
ADAM_EPS = 1e-08
ADAM_WD = 0.01
ADAM_STEP = 10

LANES = 128
VMEM_LIMIT = 48 * 1024 * 1024

O_GA, O_GB = 0, 1024
O_SBQ, O_SBK, O_SBV, O_SBZ = 2048, 2560, 3072, 3584
O_MLAZ = 4096
O_CQ = 4608
O_CKV = 5120
O_KROT = 5376
IN_PAD = 5632

BQ = 256
BK = 256


def _cparams(*sem):
    return pltpu.CompilerParams(dimension_semantics=sem, vmem_limit_bytes=VMEM_LIMIT)


def _tile_of(n, cap=512):
    if n <= cap:
        return n
    for t in (1024, 768, 512, 384, 256, 128):
        if t <= cap and n % t == 0:
            return t
    raise ValueError(n)


def _rowwise(fn, rows, vecs, outs, reds=(), *, name, tile=256):
    norm = []
    for r in rows:
        if isinstance(r, tuple):
            arr, cb, w = r[:3]
            ro = r[3] if len(r) > 3 else 0
        else:
            arr, cb, w, ro = r, 0, r.shape[1], 0
        norm.append((arr, cb, w, ro))
    s_len = norm[0][0].shape[0]
    tile = min(tile, s_len)
    assert s_len % tile == 0
    n_row, n_vec, n_out, n_red = len(norm), len(vecs), len(outs), len(reds)

    def body(*refs):
        step = pl.program_id(0)
        row_refs = refs[:n_row]
        vec_refs = refs[n_row:n_row + n_vec]
        out_refs = refs[n_row + n_vec:n_row + n_vec + n_out]
        red_refs = refs[n_row + n_vec + n_out:]
        row_res, red_res = fn(*[r[...] for r in row_refs], *[v[...] for v in vec_refs])
        for o, val in zip(out_refs, row_res):
            o[...] = val.astype(o.dtype)
        if n_red:
            @pl.when(step == 0)
            def _():
                for r in red_refs:
                    r[...] = jnp.zeros(r.shape, r.dtype)
            for r, val in zip(red_refs, red_res):
                r[...] += val

    in_specs = []
    for arr, cb, w, ro in norm:
        in_specs.append(pl.BlockSpec((tile, w), functools.partial(lambda i, cb, rb: (i + rb, cb), cb=cb, rb=ro // tile)))
        assert ro % tile == 0
    for v in vecs:
        in_specs.append(pl.BlockSpec(v.shape, lambda i: (0, 0)))
    out_shape = [jax.ShapeDtypeStruct((s_len, w), dt) for w, dt in outs]
    out_specs = [pl.BlockSpec((tile, w), lambda i: (i, 0)) for w, _ in outs]
    out_shape += [jax.ShapeDtypeStruct((1, w), F32) for w in reds]
    out_specs += [pl.BlockSpec((1, w), lambda i: (0, 0)) for w in reds]
    res = pl.pallas_call(
        body, name=name, grid=(s_len // tile,), in_specs=in_specs, out_specs=out_specs, out_shape=out_shape,
        compiler_params=_cparams("arbitrary" if n_red else "parallel"),
    )(*[a for a, _, _, _ in norm], *vecs)
    return res


def _mm(a, b, *, ta=False, tb=False, out_dtype=F32, name, exchange=None):
    m, k = (a.shape[1], a.shape[0]) if ta else a.shape
    n = b.shape[0] if tb else b.shape[1]
    assert (b.shape[1] if tb else b.shape[0]) == k
    tm, tn, tk = _tile_of(m, 1024), _tile_of(n, 1024 if n <= 1024 else 512), _tile_of(k, 1024)
    ni, nj, nk = m // tm, n // tn, k // tk
    dims = (((0 if ta else 1,), (1 if tb else 0,)), ((), ()))
    n_ex = len(exchange.arrays) if exchange else 0

    def body(*refs):
        a_ref, b_ref = refs[:2]
        x_refs, o_ref, out_refs = refs[2:2 + n_ex], refs[2 + n_ex], refs[3 + n_ex:3 + 2 * n_ex]
        acc_ref, sems = refs[3 + 2 * n_ex], refs[4 + 2 * n_ex:]
        i, j, kk = pl.program_id(0), pl.program_id(1), pl.program_id(2)

        if exchange:
            @pl.when((i == 0) & (j == 0) & (kk == 0))
            def _():
                exchange.start(x_refs, out_refs, sems)

        @pl.when(kk == 0)
        def _():
            acc_ref[...] = jnp.zeros(acc_ref.shape, F32)

        acc_ref[...] += lax.dot_general(a_ref[...].astype(MXU_DTYPE), b_ref[...].astype(MXU_DTYPE), dims,
                                        preferred_element_type=F32)

        @pl.when(kk == nk - 1)
        def _():
            o_ref[...] = acc_ref[...].astype(o_ref.dtype)

        if exchange:
            @pl.when((i == ni - 1) & (j == nj - 1) & (kk == nk - 1))
            def _():
                exchange.wait(x_refs, out_refs, sems)

    a_spec = pl.BlockSpec((tk, tm), lambda i, j, kk: (kk, i)) if ta else pl.BlockSpec((tm, tk), lambda i, j, kk: (i, kk))
    b_spec = pl.BlockSpec((tn, tk), lambda i, j, kk: (j, kk)) if tb else pl.BlockSpec((tk, tn), lambda i, j, kk: (kk, j))
    o_spec = pl.BlockSpec((tm, tn), lambda i, j, kk: (i, j))
    o_shape = jax.ShapeDtypeStruct((m, n), out_dtype)
    if not exchange:
        return pl.pallas_call(
            body, name=name, grid=(ni, nj, nk), in_specs=[a_spec, b_spec], out_specs=o_spec, out_shape=o_shape,
            scratch_shapes=[pltpu.VMEM((tm, tn), F32)],
            compiler_params=_cparams("parallel", "parallel", "arbitrary"),
        )(a, b)
    res = pl.pallas_call(
        body, name=name, grid=(ni, nj, nk), in_specs=[a_spec, b_spec] + exchange.in_specs,
        out_specs=[o_spec] + exchange.out_specs, out_shape=[o_shape] + exchange.out_shape,
        scratch_shapes=[pltpu.VMEM((tm, tn), F32)] + exchange.scratch,
        compiler_params=_cparams("arbitrary", "arbitrary", "arbitrary"),
    )(a, b, *exchange.arrays)
    return res[0], res[1:]


_NT = (((1,), (1,)), ((), ()))
_TN = (((0,), (0,)), ((), ()))


def _dot(a, b):
    return jnp.dot(a, b, preferred_element_type=F32)


def _dot_nt(a, b):
    return lax.dot_general(a, b, _NT, preferred_element_type=F32)


def _dot_tn(a, b):
    return lax.dot_general(a, b, _TN, preferred_element_type=F32)


def _running_sum(x, tri):
    return _dot(x.astype(MXU_DTYPE), tri)


def _neg_softplus(z):
    u = jnp.exp2(jnp.abs(z) * (-1.0 / math.log(2.0)))
    return -jnp.maximum(z, 0.0) - jnp.log(1.0 + u)


GROUP = 2


def _walk_blocks(step, st, n, descending):
    groups = n // GROUP

    def many(t, s):
        first = (n - 1 - GROUP * t) if descending else GROUP * t
        return step([first - g if descending else first + g for g in range(GROUP)], s)

    def one(r, s):
        done = GROUP * groups + r
        return step([n - 1 - done if descending else done], s)

    st = lax.fori_loop(0, groups, many, st)
    if (BQ // BK) % GROUP == 0:
        return st
    return lax.fori_loop(0, n - GROUP * groups, one, st)


def _chains(js):
    return [(h, t) for t in range(len(js)) for h in range(2)]


def _rowsum(x):
    return jnp.sum(x, axis=1, keepdims=True)


def _attn_consts():
    row = lax.broadcasted_iota(jnp.int32, (BQ, BK), 0)
    col = lax.broadcasted_iota(jnp.int32, (BQ, BK), 1)
    trow = lax.broadcasted_iota(jnp.int32, (BK, BK), 0)
    tcol = lax.broadcasted_iota(jnp.int32, (BK, BK), 1)
    lane = lax.broadcasted_iota(jnp.int32, (BQ, LANES), 1)
    klane = lax.broadcasted_iota(jnp.int32, (BK, LANES), 1)
    return row, col, trow, tcol, lane, klane


DIAG = BQ // BK
assert BQ == DIAG * BK


def _diag_block(i):
    return i * DIAG


def _diag_blocks(jd, descending):
    places = list(reversed(range(DIAG))) if descending else list(range(DIAG))
    return [jd + g for g in places], places


def _key_slice(j):
    return pl.ds(pl.multiple_of(j * BK, BK), BK)


def _sb_forward(proj, s_len):
    nq = s_len // BQ
    assert s_len // BK <= HD
    qc, kc, vc = O_SBQ // LANES, O_SBK // LANES, O_SBV // LANES

    def body(q_ref, k_ref, v_ref, o_ref, tails_ref):
        i = pl.program_id(1)
        jd = _diag_block(i)
        row, col, trow, tcol, lane, klane = _attn_consts()
        strict = [col + g * BK < row for g in range(DIAG)]
        tri = (trow >= tcol).astype(MXU_DTYPE)
        q = q_ref[...] * 0.125
        qh = [jnp.where(lane < HD, q, 0.0).astype(MXU_DTYPE), jnp.where(lane >= HD, q, 0.0).astype(MXU_DTYPE)]

        km = [klane < HD, klane >= HD]

        def step(js, st, diag):
            carry, acc, tail = st
            chains = _chains(js)
            kj = [k_ref[_key_slice(j), :].astype(MXU_DTYPE) for j in js]
            vj = [v_ref[_key_slice(j), :].astype(MXU_DTYPE) for j in js]
            z = {(h, t): _dot_nt(qh[h], kj[t]) for h, t in chains}
            run = list(carry)
            suf, carry_in = {}, {}
            for h, t in chains:
                lom = _neg_softplus(z[h, t])
                if diag:
                    lom = jnp.where(strict[diag[t]], lom, 0.0)
                suf[h, t] = _running_sum(lom, tri)
                carry_in[h, t] = run[h]
                run[h] = run[h] + _rowsum(lom)
            for h, t in chains:
                a = jnp.exp(z[h, t] + suf[h, t] + carry_in[h, t])
                if diag:
                    a = jnp.where(strict[diag[t]], a, 0.0)
                acc = acc + _dot(a.astype(MXU_DTYPE), jnp.where(km[h], vj[t], 0))
                tail = jnp.where(lane == h * HD + js[t], carry_in[h, t], tail)
            return tuple(run), acc, tail

        zero = jnp.zeros((BQ, LANES), F32)
        diag_js, places = _diag_blocks(jd, True)
        st = step(diag_js, ((jnp.zeros((BQ, 1), F32),) * 2, zero, zero), places)
        st = _walk_blocks(lambda js, s: step(js, s, False), st, jd, True)
        o_ref[...] = st[1]
        tails_ref[...] = st[2]

    blk = pl.BlockSpec((BQ, LANES), lambda p, i: (i, p))
    out = jax.ShapeDtypeStruct((s_len, SB_W), F32)
    return pl.pallas_call(
        body, name="sb_fwd", grid=(HEADS // 2, nq),
        in_specs=[pl.BlockSpec((BQ, LANES), lambda p, i: (i, qc + p)),
                  pl.BlockSpec((s_len, LANES), lambda p, i: (0, kc + p)),
                  pl.BlockSpec((s_len, LANES), lambda p, i: (0, vc + p))],
        out_specs=[blk, blk], out_shape=[out, out],
        compiler_params=_cparams("parallel", "parallel"),
    )(proj, proj, proj)


def _sb_backward(proj, tails, do, s_len, exchange):
    nq = s_len // BQ
    qc, kc, vc = O_SBQ // LANES, O_SBK // LANES, O_SBV // LANES
    n_ex = len(exchange.arrays)

    def body(q_ref, k_ref, v_ref, tails_ref, do_ref, *rest):
        x_refs, (dq_ref, dk_ref, dv_ref) = rest[:n_ex], rest[n_ex:n_ex + 3]
        out_refs, sems = rest[n_ex + 3:2 * n_ex + 3], rest[2 * n_ex + 3:]
        p = pl.program_id(0)
        i = pl.program_id(1)
        jd = _diag_block(i)

        @pl.when((p == 0) & (i == 0))
        def _():
            exchange.start(x_refs, out_refs, sems)

        @pl.when(i == 0)
        def _():
            dk_ref[...] = jnp.zeros(dk_ref.shape, F32)
            dv_ref[...] = jnp.zeros(dv_ref.shape, F32)

        row, col, trow, tcol, lane, klane = _attn_consts()
        strict = [col + g * BK < row for g in range(DIAG)]
        tri = (trow >= tcol).astype(MXU_DTYPE)
        tri_p = (trow <= tcol).astype(MXU_DTYPE)
        q = q_ref[...] * 0.125
        tails_blk = tails_ref[...]
        do_blk = do_ref[...]
        hm = [lane < HD, lane >= HD]
        km = [klane < HD, klane >= HD]
        qh = [jnp.where(m, q, 0.0).astype(MXU_DTYPE) for m in hm]
        doh = [jnp.where(m, do_blk, 0.0).astype(MXU_DTYPE) for m in hm]

        def step(js, st, diag):
            before, dq = st
            chains = _chains(js)
            kj = [k_ref[_key_slice(j), :].astype(MXU_DTYPE) for j in js]
            vj = [v_ref[_key_slice(j), :].astype(MXU_DTYPE) for j in js]
            z = {(h, t): _dot_nt(qh[h], kj[t]) for h, t in chains}
            da = {(h, t): _dot_nt(doh[h], vj[t]) for h, t in chains}
            suf, sig = {}, {}
            for h, t in chains:
                lom = _neg_softplus(z[h, t])
                if diag:
                    lom = jnp.where(strict[diag[t]], lom, 0.0)
                suf[h, t] = _running_sum(lom, tri)
                sig[h, t] = jnp.exp(z[h, t] + lom)
            run = list(before)
            dl, pre, before_in = {}, {}, {}
            dk_add, dv_add = [None] * len(js), [None] * len(js)
            for h, t in chains:
                tail = _rowsum(jnp.where(lane == h * HD + js[t], tails_blk, 0.0))
                a = jnp.exp(z[h, t] + suf[h, t] + tail)
                if diag:
                    a = jnp.where(strict[diag[t]], a, 0.0)
                dl[h, t] = da[h, t] * a
                pre[h, t] = _dot(dl[h, t].astype(MXU_DTYPE), tri_p)
                dv_h = _dot_tn(a.astype(MXU_DTYPE), doh[h])
                dv_add[t] = dv_h if dv_add[t] is None else dv_add[t] + dv_h
                before_in[h, t] = run[h]
                run[h] = run[h] + _rowsum(dl[h, t])
            for h, t in chains:
                upto = before_in[h, t] + pre[h, t]
                dz = dl[h, t] - sig[h, t] * upto
                if diag:
                    dz = jnp.where(strict[diag[t]], dz, 0.0)
                dzb = dz.astype(MXU_DTYPE)
                dq = dq + _dot(dzb, jnp.where(km[h], kj[t], 0))
                dk_h = _dot_tn(dzb, qh[h])
                dk_add[t] = dk_h if dk_add[t] is None else dk_add[t] + dk_h
            for t, j in enumerate(js):
                dk_ref[_key_slice(j), :] += dk_add[t]
                dv_ref[_key_slice(j), :] += dv_add[t]
            return tuple(run), dq

        st = ((jnp.zeros((BQ, 1), F32),) * 2, jnp.zeros((BQ, LANES), F32))
        st = _walk_blocks(lambda js, s: step(js, s, False), st, jd, False)
        diag_js, places = _diag_blocks(jd, False)
        st = step(diag_js, st, places)
        dq_ref[...] = st[1] * 0.125

        @pl.when((p == HEADS // 2 - 1) & (i == nq - 1))
        def _():
            exchange.wait(x_refs, out_refs, sems)

    blk = pl.BlockSpec((BQ, LANES), lambda p, i: (i, p))
    full = pl.BlockSpec((s_len, LANES), lambda p, i: (0, p))
    out = jax.ShapeDtypeStruct((s_len, SB_W), F32)
    res = pl.pallas_call(
        body, name="sb_bwd", grid=(HEADS // 2, nq),
        in_specs=[pl.BlockSpec((BQ, LANES), lambda p, i: (i, qc + p)),
                  pl.BlockSpec((s_len, LANES), lambda p, i: (0, kc + p)),
                  pl.BlockSpec((s_len, LANES), lambda p, i: (0, vc + p)),
                  blk, blk] + exchange.in_specs,
        out_specs=[blk, full, full] + exchange.out_specs, out_shape=[out, out, out] + exchange.out_shape,
        scratch_shapes=exchange.scratch,
        compiler_params=_cparams("arbitrary", "arbitrary"),
    )(proj, proj, proj, tails, do, *exchange.arrays)
    return res[:3], res[3:]


def _pair_mask(rows, h):
    lane = lax.broadcasted_iota(jnp.int32, (rows, 2 * LANES), 1)
    rot = lane - LANES
    return (((lane < LANES) & (lane // HD == h))
            | ((lane >= LANES) & (rot < 2 * ROPE) & ((rot // (ROPE // 2)) % 2 == h)))


def _mla_forward(q_cat, k_cat, kv, s_len):
    nq = s_len // BQ
    scale = 1.0 / math.sqrt(QK_DIM)

    def body(q_ref, k_ref, v_ref, o_ref, lse_ref):
        i = pl.program_id(1)
        jd = _diag_block(i)
        row, col, trow, tcol, lane, klane = _attn_consts()
        causal = [col + g * BK <= row for g in range(DIAG)]
        q = q_ref[...]
        hm = [lane < HD, lane >= HD]
        km = [klane < HD, klane >= HD]
        qh = [jnp.where(_pair_mask(BQ, h), q, 0) for h in range(2)]

        def step(js, st, diag):
            m_run, l_run, acc = st
            chains = _chains(js)
            kj = [k_ref[_key_slice(j), :] for j in js]
            vj = [v_ref[_key_slice(j), :].astype(MXU_DTYPE) for j in js]
            s = {}
            for h, t in chains:
                s[h, t] = _dot_nt(qh[h], kj[t]) * scale
                if diag:
                    s[h, t] = jnp.where(causal[diag[t]], s[h, t], -jnp.inf)
            m_new, alpha, l_new = [], [], []
            for h in range(2):
                top = m_run[h]
                for t in range(len(js)):
                    top = jnp.maximum(top, jnp.max(s[h, t], axis=1, keepdims=True))
                m_new.append(top)
                alpha.append(jnp.exp(m_run[h] - top))
                l_new.append(alpha[h] * l_run[h])
            add = None
            for h, t in chains:
                pr = jnp.exp(s[h, t] - m_new[h])
                l_new[h] = l_new[h] + _rowsum(pr)
                part = _dot(pr.astype(MXU_DTYPE), jnp.where(km[h], vj[t], 0))
                add = part if add is None else add + part
            acc = jnp.where(hm[0], alpha[0], alpha[1]) * acc + add
            return tuple(m_new), tuple(l_new), acc

        st = ((jnp.full((BQ, 1), -1e30, F32),) * 2, (jnp.zeros((BQ, 1), F32),) * 2, jnp.zeros((BQ, LANES), F32))
        diag_js, places = _diag_blocks(jd, True)
        st = step(diag_js, st, places)
        m_run, l_run, acc = _walk_blocks(lambda js, s: step(js, s, False), st, jd, True)
        o_ref[...] = acc / jnp.where(hm[0], l_run[0], l_run[1])
        lse_ref[...] = jnp.where(hm[0], m_run[0] + jnp.log(l_run[0]), m_run[1] + jnp.log(l_run[1]))

    blk = pl.BlockSpec((BQ, LANES), lambda p, i: (i, p))
    out = jax.ShapeDtypeStruct((s_len, MLA_W), F32)
    return pl.pallas_call(
        body, name="mla_fwd", grid=(HEADS // 2, nq),
        in_specs=[pl.BlockSpec((BQ, 2 * LANES), lambda p, i: (i, p)),
                  pl.BlockSpec((s_len, 2 * LANES), lambda p, i: (0, p)),
                  pl.BlockSpec((s_len, LANES), lambda p, i: (0, MLA_W // LANES + p))],
        out_specs=[blk, blk], out_shape=[out, out],
        compiler_params=_cparams("parallel", "parallel"),
    )(q_cat, k_cat, kv)


def _mla_backward(q_cat, k_cat, kv, o, lse, do, s_len):
    nq = s_len // BQ
    scale = 1.0 / math.sqrt(QK_DIM)

    def body(q_ref, k_ref, v_ref, o_ref, lse_ref, do_ref, dq_ref, dk_ref, dv_ref):
        i = pl.program_id(1)

        @pl.when(i == 0)
        def _():
            dk_ref[...] = jnp.zeros(dk_ref.shape, F32)
            dv_ref[...] = jnp.zeros(dv_ref.shape, F32)

        jd = _diag_block(i)
        row, col, trow, tcol, lane, klane = _attn_consts()
        causal = [col + g * BK <= row for g in range(DIAG)]
        q = q_ref[...]
        o_blk = o_ref[...]
        do_blk = do_ref[...]
        lse_blk = lse_ref[...]
        hm = [lane < HD, lane >= HD]
        kpm = [_pair_mask(BK, h) for h in range(2)]
        qh = [jnp.where(_pair_mask(BQ, h), q, 0) for h in range(2)]
        doh_f = [jnp.where(m, do_blk, 0.0) for m in hm]
        doh = [d.astype(MXU_DTYPE) for d in doh_f]
        delta = [jnp.sum(d * o_blk, axis=1, keepdims=True) for d in doh_f]
        lse_h = [jnp.sum(jnp.where(lane == h * HD, lse_blk, 0.0), axis=1, keepdims=True) for h in range(2)]

        def step(js, st, diag):
            dq = st
            chains = _chains(js)
            kj = [k_ref[_key_slice(j), :] for j in js]
            vj = [v_ref[_key_slice(j), :].astype(MXU_DTYPE) for j in js]
            s = {(h, t): _dot_nt(qh[h], kj[t]) for h, t in chains}
            dp = {(h, t): _dot_nt(doh[h], vj[t]) for h, t in chains}
            adds = [[None] * len(js) for _ in range(2)]

            def accumulate(slot, t, part):
                adds[slot][t] = part if adds[slot][t] is None else adds[slot][t] + part

            for h, t in chains:
                pr = jnp.exp(s[h, t] * scale - lse_h[h])
                if diag:
                    pr = jnp.where(causal[diag[t]], pr, 0.0)
                dsb = (pr * (dp[h, t] - delta[h]) * scale).astype(MXU_DTYPE)
                dq = dq + _dot(dsb, jnp.where(kpm[h], kj[t], 0))
                accumulate(0, t, _dot_tn(dsb, qh[h]))
                accumulate(1, t, _dot_tn(pr.astype(MXU_DTYPE), doh[h]))
            for t, j in enumerate(js):
                dk_ref[_key_slice(j), :] += adds[0][t]
                dv_ref[_key_slice(j), :] += adds[1][t]
            return dq

        diag_js, places = _diag_blocks(jd, True)
        st = step(diag_js, jnp.zeros((BQ, 2 * LANES), F32), places)
        dq_ref[...] = _walk_blocks(lambda js, s: step(js, s, False), st, jd, True)

    blk = pl.BlockSpec((BQ, LANES), lambda p, i: (i, p))
    full = pl.BlockSpec((s_len, LANES), lambda p, i: (0, p))
    out = jax.ShapeDtypeStruct((s_len, MLA_W), F32)
    out_cat = jax.ShapeDtypeStruct((s_len, 2 * MLA_W), F32)
    return pl.pallas_call(
        body, name="mla_bwd", grid=(HEADS // 2, nq),
        in_specs=[pl.BlockSpec((BQ, 2 * LANES), lambda p, i: (i, p)),
                  pl.BlockSpec((s_len, 2 * LANES), lambda p, i: (0, p)),
                  pl.BlockSpec((s_len, LANES), lambda p, i: (0, MLA_W // LANES + p)),
                  blk, blk, blk],
        out_specs=[pl.BlockSpec((BQ, 2 * LANES), lambda p, i: (i, p)),
                   pl.BlockSpec((s_len, 2 * LANES), lambda p, i: (0, p)), full],
        out_shape=[out_cat, out_cat, out],
        compiler_params=_cparams("arbitrary", "arbitrary"),
    )(q_cat, k_cat, kv, o, lse, do)


def _mesh_pos():
    return lax.axis_index("x"), lax.axis_index("y"), lax.axis_index("c")


def _dev_index(px, py, pc):
    return 4 * px + 2 * py + pc


def _all_gather(block, name):
    return _all_gather_parts([block], name)[0]


def _all_gather_parts(blocks, name):
    n = len(blocks)

    def body(*refs):
        x_refs, out_refs = refs[:n], refs[n:2 * n]
        send_sems, recv_sems, local_sems = refs[2 * n:]
        x, y, c = _mesh_pos()
        me, sibling = (x, y, c), (x, y, 1 - c)
        chips = [(1 - x, y), (x, 1 - y), (1 - x, 1 - y)]

        def copy(a, k, blockpos, to, src=None):
            slot = out_refs[a].at[_dev_index(*blockpos)]
            return pltpu.make_async_remote_copy(
                src_ref=slot if src is None else src, dst_ref=slot,
                send_sem=send_sems.at[7 * a + k], recv_sem=recv_sems.at[7 * a + k],
                device_id=to, device_id_type=pl.DeviceIdType.MESH)

        mine = [pltpu.make_async_copy(x_refs[a], out_refs[a].at[_dev_index(*me)], local_sems.at[a]) for a in range(n)]
        for cp in mine:
            cp.start()
        first = []
        for a in range(n):
            first.append(copy(a, 0, me, sibling, src=x_refs[a]))
            first += [copy(a, 1 + j, me, (*chip, c), src=x_refs[a]) for j, chip in enumerate(chips)]
        for cp in first:
            cp.start()
        passed = []
        for j, chip in enumerate(chips):
            for a in range(n):
                copy(a, 1 + j, (*chip, c), me).wait_recv()
                passed.append(copy(a, 4 + j, (*chip, c), sibling))
                passed[-1].start()
        for a in range(n):
            copy(a, 0, sibling, me).wait_recv()
            for j, chip in enumerate(chips):
                copy(a, 4 + j, (*chip, 1 - c), me).wait_recv()
        for cp in first + passed:
            cp.wait_send()
        for cp in mine:
            cp.wait()

    return pl.pallas_call(
        body, name=name,
        out_shape=[jax.ShapeDtypeStruct((N_DEV,) + b.shape, b.dtype) for b in blocks],
        in_specs=[pl.BlockSpec(memory_space=pl.ANY)] * n, out_specs=[pl.BlockSpec(memory_space=pl.ANY)] * n,
        scratch_shapes=[pltpu.SemaphoreType.DMA((7 * n,)), pltpu.SemaphoreType.DMA((7 * n,)),
                        pltpu.SemaphoreType.DMA((n,))],
    )(*blocks)


class _Exchange:
    def __init__(self, arrays):
        self.arrays = list(arrays)
        n = len(self.arrays)
        self.in_specs = [pl.BlockSpec(memory_space=pl.ANY)] * n
        self.out_specs = [pl.BlockSpec(memory_space=pl.ANY)] * n
        self.out_shape = [jax.ShapeDtypeStruct(a.shape, a.dtype) for a in self.arrays]
        self.scratch = [pltpu.SemaphoreType.DMA((7 * n,)), pltpu.SemaphoreType.DMA((7 * n,)),
                        pltpu.SemaphoreType.DMA((n,))]

    def _copies(self, x_refs, out_refs, sems, with_arrivals):
        send_sems, recv_sems, local_sems = sems
        x, y, c = _mesh_pos()
        me = _dev_index(x, y, c)
        flips = [(fx, fy, fc) for fx in (0, 1) for fy in (0, 1) for fc in (0, 1)][1:]
        peers = [(1 - x if fx else x, 1 - y if fy else y, 1 - c if fc else c) for fx, fy, fc in flips]
        mine, sends, arrivals = [], [], []
        for a in range(len(self.arrays)):
            mine.append(pltpu.make_async_copy(x_refs[a].at[me], out_refs[a].at[me], local_sems.at[a]))
            for k, peer in enumerate(peers):
                sends.append(pltpu.make_async_remote_copy(
                    src_ref=x_refs[a].at[_dev_index(*peer)], dst_ref=out_refs[a].at[me],
                    send_sem=send_sems.at[7 * a + k], recv_sem=recv_sems.at[7 * a + k],
                    device_id=peer, device_id_type=pl.DeviceIdType.MESH))
                if not with_arrivals:
                    continue
                arrivals.append(pltpu.make_async_remote_copy(
                    src_ref=x_refs[a].at[me], dst_ref=out_refs[a].at[_dev_index(*peer)],
                    send_sem=send_sems.at[7 * a + k], recv_sem=recv_sems.at[7 * a + k],
                    device_id=peer, device_id_type=pl.DeviceIdType.MESH))
        return mine, sends, arrivals

    def start(self, x_refs, out_refs, sems):
        mine, sends, _ = self._copies(x_refs, out_refs, sems, False)
        for cp in mine + sends:
            cp.start()

    def wait(self, x_refs, out_refs, sems):
        mine, sends, arrivals = self._copies(x_refs, out_refs, sems, True)
        for cp in arrivals:
            cp.wait_recv()
        for cp in sends:
            cp.wait_send()
        for cp in mine:
            cp.wait()


def _sum_blocks(parts, name):
    n, r, c = parts.shape
    row_tiles = [t for t in range(16, min(r, 2048) + 1, 16) if r % t == 0]
    if row_tiles:
        tr, tc = max(row_tiles), c
    else:
        tr, tc = r, 2 * LANES
    assert c % tc == 0

    def body(p_ref, o_ref):
        acc = p_ref[0].astype(F32)
        for s in range(1, n):
            acc = acc + p_ref[s].astype(F32)
        o_ref[...] = acc

    return pl.pallas_call(
        body, name=name, grid=(r // tr, c // tc),
        in_specs=[pl.BlockSpec((n, tr, tc), lambda i, j: (0, i, j))],
        out_specs=pl.BlockSpec((tr, tc), lambda i, j: (i, j)),
        out_shape=jax.ShapeDtypeStruct((r, c), F32),
        compiler_params=_cparams("parallel", "parallel"),
    )(parts)


def _sigmoid(x):
    return 1.0 / (1.0 + jnp.exp(-x))


def _silu(x):
    return x * _sigmoid(x)


def _silu_grad(x):
    s = _sigmoid(x)
    return s * (1.0 + x * (1.0 - s))


def _colsum(x):
    return jnp.sum(x, axis=0, keepdims=True)


def _rms(x):
    return lax.rsqrt(jnp.mean(x * x, axis=-1, keepdims=True) + EPS)


def _rms_bwd(xn, r, dxn):
    return r * (dxn - xn * jnp.mean(dxn * xn, axis=-1, keepdims=True))


def _adamw(w, g, m, v):
    m = ADAM_B1 * m + (1.0 - ADAM_B1) * g
    v = ADAM_B2 * v + (1.0 - ADAM_B2) * jnp.square(g)
    m_hat = m / (1.0 - ADAM_B1 ** ADAM_STEP)
    v_hat = v / (1.0 - ADAM_B2 ** ADAM_STEP)
    delta = -ADAM_LR * (m_hat / (jnp.sqrt(v_hat) + ADAM_EPS) + ADAM_WD * w)
    return delta, m, v


def _adamw_call(w, g, m, v, name):
    r, c = w.shape
    if r % 256 == 0:
        tr, tc = 256, c
    elif r * c <= 256 * 1024 or c % (2 * LANES):
        tr, tc = r, c
    else:
        tr, tc = r, 2 * LANES

    def body(w_ref, g_ref, m_ref, v_ref, d_out, m_out, v_out):
        d_out[...], m_out[...], v_out[...] = _adamw(w_ref[...], g_ref[...], m_ref[...], v_ref[...])

    spec = pl.BlockSpec((tr, tc), lambda i, j: (i, j))
    return pl.pallas_call(
        body, name=name, grid=(r // tr, c // tc), in_specs=[spec] * 4, out_specs=[spec] * 3,
        out_shape=[jax.ShapeDtypeStruct((r, c), F32)] * 3, compiler_params=_cparams("parallel", "parallel"),
    )(w, g, m, v)


def _uq_to_kernel_layout(w):
    lead = w.shape[:-1]
    t = w.reshape(lead + (HEADS, QK_DIM))
    return jnp.concatenate([t[..., :NOPE].reshape(lead + (HEADS * NOPE,)),
                            t[..., NOPE:NOPE + ROPE // 2].reshape(lead + (LANES,)),
                            t[..., NOPE + ROPE // 2:].reshape(lead + (LANES,))], axis=-1)


def _uq_from_kernel_layout(w):
    lead = w.shape[:-1]
    nope = w[..., :HEADS * NOPE].reshape(lead + (HEADS, NOPE))
    r1 = w[..., HEADS * NOPE:HEADS * NOPE + LANES].reshape(lead + (HEADS, ROPE // 2))
    r2 = w[..., HEADS * NOPE + LANES:].reshape(lead + (HEADS, ROPE // 2))
    return jnp.concatenate([nope, r1, r2], axis=-1).reshape(lead + (HEADS * QK_DIM,))


def _ukv_to_kernel_layout(w):
    lead = w.shape[:-1]
    t = w.reshape(lead + (HEADS, NOPE + HD))
    return jnp.concatenate([t[..., :NOPE].reshape(lead + (HEADS * NOPE,)),
                            t[..., NOPE:].reshape(lead + (HEADS * HD,))], axis=-1)


def _ukv_from_kernel_layout(w):
    lead = w.shape[:-1]
    kn = w[..., :HEADS * NOPE].reshape(lead + (HEADS, NOPE))
    vv = w[..., HEADS * NOPE:].reshape(lead + (HEADS, HD))
    return jnp.concatenate([kn, vv], axis=-1).reshape(lead + (HEADS * (NOPE + HD),))


def _w_in_t_to_kernel_layout(wt):
    sb = wt[0:2048]
    c_q = wt[2048:2432]
    c_kv = wt[2432:2688]
    k_rot = wt[2688:2720]
    mla_z = wt[2720:3232]
    gates = wt[3232:5280]
    zeros = jnp.zeros((LANES, wt.shape[1]), wt.dtype)
    k1 = jnp.tile(k_rot[:ROPE // 2], (HEADS, 1))
    k2 = jnp.tile(k_rot[ROPE // 2:], (HEADS, 1))
    return jnp.concatenate([gates, sb, mla_z, c_q, zeros, c_kv, k1, k2], axis=0)


def _w_in_t_from_kernel_layout(gt, g_rot):
    return jnp.concatenate([gt[O_SBQ:O_SBQ + 2048], gt[O_CQ:O_CQ + Q_RANK], gt[O_CKV:O_CKV + KV_RANK], g_rot,
                            gt[O_MLAZ:O_MLAZ + MLA_W], gt[O_GA:O_GA + 2 * D]], axis=0)


def kernel(x, c, positions, w_ada, b_ada, norm_gain, w_in, q_norm_gain, w_uq, kv_norm_gain, w_ukv, w_branch_a, w_branch_b, w_out, final_norm_gain, loss_target, m_w_ada, m_b_ada, m_norm_gain, m_w_in, m_q_norm_gain, m_w_uq, m_kv_norm_gain, m_w_ukv, m_w_branch_a, m_w_branch_b, m_w_out, m_final_norm_gain, v_w_ada, v_b_ada, v_norm_gain, v_w_in, v_q_norm_gain, v_w_uq, v_kv_norm_gain, v_w_ukv, v_w_branch_a, v_w_branch_b, v_w_out, v_final_norm_gain):
    s_len = x.shape[1]
    me = _dev_index(*_mesh_pos())
    x2d = x[0]
    tgt = loss_target[0]

    w_in_t = w_in[0].T.astype(BF16)
    big = [w_uq[0], w_ukv[0], w_branch_a[0], w_branch_b[0], w_out[0]]
    big_sizes = [int(w.size) for w in big]
    packed = jnp.concatenate([w.astype(BF16).reshape(-1, LANES) for w in big], axis=0)
    g_in_t, gathered = _all_gather_parts([w_in_t, packed], "gather_weights")
    offs = [0]
    for n in big_sizes:
        offs.append(offs[-1] + n // LANES)

    def unpack(t, shape):
        return gathered[:, offs[t]:offs[t + 1], :].reshape((N_DEV,) + shape)

    def cols(t, shape):
        return unpack(t, shape).transpose(1, 0, 2).reshape(shape[0], N_DEV * shape[1])

    w_in_kt = _w_in_t_to_kernel_layout(g_in_t.reshape(N_DEV * w_in_t.shape[0], D))
    w_uq_k = _uq_to_kernel_layout(cols(0, big[0].shape))
    w_ukv_k = _ukv_to_kernel_layout(cols(1, big[1].shape))
    w_a_f = cols(2, big[2].shape)
    w_b_f = cols(3, big[3].shape)
    w_out_f = unpack(4, big[4].shape).reshape(D, D)

    c_all = _all_gather(c.reshape(8, LANES), "gather_c").reshape(N_DEV, D)
    mod_cols = _mm(c_all, w_ada[0], name="ada_mod")
    mod_all = _all_gather(mod_cols, "gather_mod")
    mod = lax.dynamic_index_in_dim(mod_all, me, axis=1, keepdims=False).reshape(1, 3 * D)
    mod_shift, mod_scale, mod_gate = mod[:, :D], mod[:, D:2 * D], mod[:, 2 * D:]
    b_shift, b_scale, b_gate = b_ada[:, :D], b_ada[:, D:2 * D], b_ada[:, 2 * D:]
    g1 = norm_gain
    gq, gkv = q_norm_gain, kv_norm_gain
    gf = final_norm_gain.reshape(1, D)

    def f_h(x_, g1_, ms, bs, msc, bsc):
        xn = x_ * _rms(x_)
        return (xn * g1_ * (1.0 + (msc + bsc)) + (ms + bs),), ()

    (h,) = _rowwise(f_h, [x2d], [g1, mod_shift, b_shift, mod_scale, b_scale], [(D, BF16)], name="ada_norm")
    proj = _mm(h, w_in_kt, tb=True, name="proj_in")

    o_a, sb_tails = _sb_forward(proj, s_len)

    def f_lat(cq, ckv, gq_, gkv_):
        return (cq * _rms(cq) * gq_, ckv * _rms(ckv) * gkv_), ()

    cq_n, ckv_n = _rowwise(f_lat, [(proj, O_CQ // Q_RANK, Q_RANK), (proj, O_CKV // KV_RANK, KV_RANK)], [gq, gkv],
                           [(Q_RANK, BF16), (KV_RANK, BF16)], name="latent_norm")
    q_mla = _mm(cq_n, w_uq_k, name="q_up")
    kv = _mm(ckv_n, w_ukv_k, name="kv_up")

    inv_freq = ROPE_BASE ** (-jnp.arange(0, ROPE, 2, dtype=F32) / ROPE)
    inv_freq_t = jnp.tile(inv_freq, HEADS).reshape(1, LANES)
    pos_col = positions.reshape(s_len, 1).astype(F32)

    pairs = HEADS // 2

    def f_rope(pos, qn, q1, q2, kn, k1, k2, freq):
        ang = pos * freq
        cs, sn = jnp.cos(ang), jnp.sin(ang)
        q1r, q2r = q1 * cs - q2 * sn, q1 * sn + q2 * cs
        k1r, k2r = k1 * cs - k2 * sn, k1 * sn + k2 * cs
        lane = lax.broadcasted_iota(jnp.int32, q1.shape, 1)
        first, second = lane < ROPE, (lane >= ROPE) & (lane < 2 * ROPE)
        k_rot = jnp.where(first, k1r, jnp.where(second, k2r, 0.0))
        q_parts, k_parts = [], []
        for p in range(pairs):
            q_rot = jnp.where(first, pltpu.roll(q1r, (LANES - ROPE * p) % LANES, 1),
                              jnp.where(second, pltpu.roll(q2r, (LANES + ROPE - ROPE * p) % LANES, 1), 0.0))
            q_parts += [qn[:, LANES * p:LANES * (p + 1)], q_rot]
            k_parts += [kn[:, LANES * p:LANES * (p + 1)], k_rot]
        return (jnp.concatenate(q_parts, axis=1), jnp.concatenate(k_parts, axis=1), cs, sn), ()

    q_cat, k_cat, cos_t, sin_t = _rowwise(
        f_rope, [pos_col, (q_mla, 0, MLA_W), (q_mla, 4, LANES), (q_mla, 5, LANES), (kv, 0, MLA_W),
                 (proj, O_KROT // LANES, LANES), (proj, O_KROT // LANES + 1, LANES)], [inv_freq_t],
        [(2 * MLA_W, BF16), (2 * MLA_W, BF16), (LANES, F32), (LANES, F32)], name="rope")

    o_b, lse = _mla_forward(q_cat, k_cat, kv, s_len)

    def f_gate(oa, za, ob, zb):
        return (oa * _silu(za), ob * _silu(zb)), ()

    ya_in, yb_in = _rowwise(f_gate, [o_a, (proj, O_SBZ // SB_W, SB_W), o_b, (proj, O_MLAZ // MLA_W, MLA_W)], [],
                            [(SB_W, BF16), (MLA_W, BF16)], name="branch_gate")
    y_a = _mm(ya_in, w_a_f, name="branch_a")
    y_b = _mm(yb_in, w_b_f, name="branch_b")

    def f_merge(ga, gb, ya, yb):
        return (_sigmoid(ga) * ya + _sigmoid(gb) * yb,), ()

    (merged,) = _rowwise(f_merge, [(proj, O_GA // D, D), (proj, O_GB // D, D), y_a, y_b], [], [(D, BF16)], name="merge")
    out = _mm(merged, w_out_f, name="out_proj")

    def f_loss(x_, out_, t_, mg, bg, gf_):
        gate = mg + bg
        x2 = x_ + gate * out_
        r2 = _rms(x2)
        xn2 = x2 * r2
        err = xn2 * gf_ - t_
        loss = jnp.full((1, LANES), 0.5 / D, F32) * jnp.sum(err * err)
        dy = err * (1.0 / D)
        dx2 = _rms_bwd(xn2, r2, dy * gf_)
        return (dx2, dx2 * gate), (loss, _colsum(dy * xn2), _colsum(dx2 * out_))

    dx2, d_out, loss_part, d_gf, d_gate = _rowwise(
        f_loss, [x2d, out, tgt], [mod_gate, b_gate, gf], [(D, F32), (D, BF16)], [LANES, D, D], name="loss_head")

    d_merged = _mm(d_out, w_out_f, tb=True, name="d_merged")
    dw_out = _mm(merged, d_out, ta=True, name="dw_out")

    def f_dmerge(dm, ga, gb, ya, yb):
        sa, sb = _sigmoid(ga), _sigmoid(gb)
        return (dm * sa, dm * sb, dm * ya * sa * (1.0 - sa), dm * yb * sb * (1.0 - sb)), ()

    d_ya, d_yb, d_ga, d_gb = _rowwise(f_dmerge, [d_merged, (proj, O_GA // D, D), (proj, O_GB // D, D), y_a, y_b], [],
                                      [(D, BF16)] * 4, name="d_merge")
    dw_a = _mm(ya_in, d_ya, ta=True, name="dw_branch_a")
    dw_b = _mm(yb_in, d_yb, ta=True, name="dw_branch_b")
    d_ya_in = _mm(d_ya, w_a_f, tb=True, name="d_branch_a")
    d_yb_in = _mm(d_yb, w_b_f, tb=True, name="d_branch_b")

    def f_dgate(da, oa, za, db, ob, zb):
        return (da * _silu(za), da * oa * _silu_grad(za), db * _silu(zb), db * ob * _silu_grad(zb)), ()

    d_oa, d_sbz, d_ob, d_mlaz = _rowwise(
        f_dgate, [d_ya_in, o_a, (proj, O_SBZ // SB_W, SB_W), d_yb_in, o_b, (proj, O_MLAZ // MLA_W, MLA_W)], [],
        [(SB_W, F32), (SB_W, BF16), (MLA_W, F32), (MLA_W, BF16)], name="d_branch_gate")

    dq_cat, dk_cat, dv_b = _mla_backward(q_cat, k_cat, kv, o_b, lse, d_ob, s_len)

    def f_drope(dq, dk, dv_, cs, sn):
        lane = lax.broadcasted_iota(jnp.int32, cs.shape, 1)
        first, second = lane < ROPE, (lane >= ROPE) & (lane < 2 * ROPE)
        dq1 = dq2 = dk1 = dk2 = None
        for p in range(pairs):
            q_rot = dq[:, LANES * (2 * p + 1):LANES * (2 * p + 2)]
            k_rot = dk[:, LANES * (2 * p + 1):LANES * (2 * p + 2)]
            parts = (pltpu.roll(jnp.where(first, q_rot, 0.0), (ROPE * p) % LANES, 1),
                     pltpu.roll(jnp.where(second, q_rot, 0.0), (LANES - ROPE + ROPE * p) % LANES, 1),
                     jnp.where(first, k_rot, 0.0), jnp.where(second, k_rot, 0.0))
            if p == 0:
                dq1, dq2, dk1, dk2 = parts
            else:
                dq1, dq2, dk1, dk2 = dq1 + parts[0], dq2 + parts[1], dk1 + parts[2], dk2 + parts[3]
        dqn_ = [dq[:, 2 * LANES * p:2 * LANES * p + LANES] for p in range(pairs)]
        dkn_ = [dk[:, 2 * LANES * p:2 * LANES * p + LANES] for p in range(pairs)]
        return (jnp.concatenate(dqn_ + [dq1 * cs + dq2 * sn, dq2 * cs - dq1 * sn], axis=1),
                jnp.concatenate(dkn_ + [dv_], axis=1),
                jnp.concatenate([dk1 * cs + dk2 * sn, dk2 * cs - dk1 * sn], axis=1)), ()

    dq_k, dkv_k, d_krot = _rowwise(f_drope, [dq_cat, dk_cat, dv_b, cos_t, sin_t], [],
                                   [(HEADS * QK_DIM, BF16), (2 * MLA_W, BF16), (2 * LANES, BF16)], name="d_rope")
    dw_uq_k = _mm(cq_n, dq_k, ta=True, name="dw_uq")
    dw_ukv_k = _mm(ckv_n, dkv_k, ta=True, name="dw_ukv")
    d_cqn = _mm(dq_k, w_uq_k, tb=True, name="d_cq_norm")
    d_ckvn = _mm(dkv_k, w_ukv_k, tb=True, name="d_ckv_norm")

    def f_dlat(cq, dcqn, ckv, dckvn, gq_, gkv_):
        rq, rkv = _rms(cq), _rms(ckv)
        cqn, ckvn = cq * rq, ckv * rkv
        return ((_rms_bwd(cqn, rq, dcqn * gq_), _rms_bwd(ckvn, rkv, dckvn * gkv_)),
                (_colsum(dcqn * cqn), _colsum(dckvn * ckvn)))

    d_cq, d_ckv, d_gq, d_gkv = _rowwise(
        f_dlat, [(proj, O_CQ // Q_RANK, Q_RANK), d_cqn, (proj, O_CKV // KV_RANK, KV_RANK), d_ckvn], [gq, gkv],
        [(Q_RANK, BF16), (KV_RANK, BF16)], [Q_RANK, KV_RANK], name="d_latent_norm")

    def col_blocks(g):
        kdim, n8 = g.shape
        return g.astype(BF16).reshape(kdim, N_DEV, n8 // N_DEV).transpose(1, 0, 2).reshape(N_DEV, -1, LANES)

    g_blocks = jnp.concatenate([col_blocks(_uq_from_kernel_layout(dw_uq_k)), col_blocks(_ukv_from_kernel_layout(dw_ukv_k)),
                                col_blocks(dw_a), col_blocks(dw_b), dw_out.astype(BF16).reshape(N_DEV, -1, LANES)], axis=1)
    (d_sbq, d_sbk, d_sbv), (g_recv,) = _sb_backward(proj, sb_tails, d_oa, s_len, _Exchange([g_blocks]))

    d_proj = jnp.concatenate([d_ga, d_gb, d_sbq.astype(BF16), d_sbk.astype(BF16), d_sbv.astype(BF16), d_sbz, d_mlaz,
                              d_cq, jnp.zeros((s_len, LANES), BF16), d_ckv, d_krot], axis=1)
    dw_in_kt = _mm(d_proj, h, ta=True, name="dw_in")

    def krot_body(t_ref, o_ref):
        half = ROPE // 2
        for part in range(2):
            acc = t_ref[part * LANES:part * LANES + half, :]
            for hh in range(1, HEADS):
                acc = acc + t_ref[part * LANES + hh * half:part * LANES + (hh + 1) * half, :]
            o_ref[part * half:(part + 1) * half, :] = acc

    dw_krot = pl.pallas_call(krot_body, name="dw_krot_sum", out_shape=jax.ShapeDtypeStruct((ROPE, D), F32))(
        dw_in_kt[O_KROT:O_KROT + 2 * LANES])

    dw_in_t = _w_in_t_from_kernel_layout(dw_in_kt, dw_krot)
    g_in_blocks = dw_in_t.astype(BF16).reshape(N_DEV, -1, D)
    dh, (g_in_recv,) = _mm(d_proj, w_in_kt, name="d_h", exchange=_Exchange([g_in_blocks]))

    def f_dx(x_, dh_, dx2_, g1_, msc, bsc):
        r = _rms(x_)
        xn = x_ * r
        dn1 = dh_ * (1.0 + (msc + bsc))
        return ((dx2_ + _rms_bwd(xn, r, dn1 * g1_),),
                (_colsum(dh_), _colsum(dh_ * (xn * g1_)), _colsum(dn1 * xn)))

    grad_x2d, d_shift, d_scale, d_g1 = _rowwise(f_dx, [x2d, dh, dx2], [g1, mod_scale, b_scale], [(D, F32)],
                                                [D, D, D], name="d_ada_norm")

    g_in_sum_t = _sum_blocks(g_in_recv, "sum_grads_w_in")
    g_sum = _sum_blocks(g_recv, "sum_grads")
    g_big = [g_sum[offs[t]:offs[t + 1]].reshape(big[t].shape) for t in range(5)]

    small = jnp.concatenate([d_shift, d_scale, d_gate, d_g1, d_gq, d_gkv, d_gf, loss_part], axis=1)
    n_small = small.shape[1]
    pad = (-n_small) % (8 * LANES)
    small = jnp.pad(small, ((0, 0), (0, pad))).reshape(-1, LANES)
    small_all = _all_gather(small, "gather_small")
    small_sum = _sum_blocks(small_all, "sum_small").reshape(1, -1)
    g_b_ada = small_sum[:, :3 * D]
    g_g1 = small_sum[:, 3 * D:4 * D]
    g_gq = small_sum[:, 4 * D:4 * D + Q_RANK]
    g_gkv = small_sum[:, 4 * D + Q_RANK:4 * D + Q_RANK + KV_RANK]
    g_gf = small_sum[:, 4 * D + Q_RANK + KV_RANK:4 * D + Q_RANK + KV_RANK + D]

    dmod_all = small_all.reshape(N_DEV, -1)[:, :3 * D]
    dmod_cols = lax.dynamic_slice_in_dim(dmod_all, me * (3 * D // N_DEV), 3 * D // N_DEV, axis=1)
    g_w_ada = _mm(c_all, dmod_cols, ta=True, name="dw_ada")

    loss = small_sum[0, n_small - LANES]

    names = ["w_ada", "b_ada", "norm_gain", "w_in", "q_norm_gain", "w_uq", "kv_norm_gain", "w_ukv",
             "w_branch_a", "w_branch_b", "w_out", "final_norm_gain"]
    weights = dict(w_ada=w_ada, b_ada=b_ada, norm_gain=norm_gain, w_in=w_in, q_norm_gain=q_norm_gain, w_uq=w_uq,
                   kv_norm_gain=kv_norm_gain, w_ukv=w_ukv, w_branch_a=w_branch_a, w_branch_b=w_branch_b, w_out=w_out,
                   final_norm_gain=final_norm_gain)
    moms = dict(w_ada=m_w_ada, b_ada=m_b_ada, norm_gain=m_norm_gain, w_in=m_w_in, q_norm_gain=m_q_norm_gain,
                w_uq=m_w_uq, kv_norm_gain=m_kv_norm_gain, w_ukv=m_w_ukv, w_branch_a=m_w_branch_a,
                w_branch_b=m_w_branch_b, w_out=m_w_out, final_norm_gain=m_final_norm_gain)
    vels = dict(w_ada=v_w_ada, b_ada=v_b_ada, norm_gain=v_norm_gain, w_in=v_w_in, q_norm_gain=v_q_norm_gain,
                w_uq=v_w_uq, kv_norm_gain=v_kv_norm_gain, w_ukv=v_w_ukv, w_branch_a=v_w_branch_a,
                w_branch_b=v_w_branch_b, w_out=v_w_out, final_norm_gain=v_final_norm_gain)
    grads2d = dict(w_ada=g_w_ada, b_ada=g_b_ada, norm_gain=g_g1, w_in=g_in_sum_t, q_norm_gain=g_gq, w_uq=g_big[0],
                   kv_norm_gain=g_gkv, w_ukv=g_big[1], w_branch_a=g_big[2], w_branch_b=g_big[3], w_out=g_big[4],
                   final_norm_gain=g_gf)

    grads, deltas, new_m, new_v = [], [], [], []
    for n in names:
        w = weights[n]
        if n == "w_in":
            to2d = lambda t: t[0].T
            back = lambda t: t.T[None]
        else:
            shape2d = grads2d[n].shape
            to2d = lambda t, s=shape2d: t.reshape(s)
            back = lambda t, s=w.shape: t.reshape(s)
        d_, m_, v_ = _adamw_call(to2d(w), grads2d[n], to2d(moms[n]), to2d(vels[n]), "adamw_" + n)
        grads.append(back(grads2d[n]))
        deltas.append(back(d_))
        new_m.append(back(m_))
        new_v.append(back(v_))

    return (loss, grad_x2d.reshape(x.shape), *grads, *deltas, *new_m, *new_v)
```

```python
import functools
import math

import jax
import jax.numpy as jnp
from jax import lax
from jax.experimental import pallas as pl
from jax.experimental.pallas import tpu as pltpu

F32 = jnp.float32
BF16 = jnp.bfloat16
MXU_DTYPE = jnp.bfloat16

N_DEV = 8
D = 1024
HEADS = 8
HD = 64
SB_W = 512
MLA_W = 512
Q_RANK = 384
KV_RANK = 256
ROPE = 32
NOPE = 64
QK_DIM = NOPE + ROPE
EPS = 1e-6
ROPE_BASE = 10000.0

ADAM_LR = 0.001
ADAM_B1 = 0.9
ADAM_B2 = 0.999
ADAM_EPS = 1e-08
ADAM_WD = 0.01
ADAM_STEP = 10

LANES = 128
VMEM_LIMIT = 48 * 1024 * 1024

O_GA, O_GB = 0, 1024
O_SBQ, O_SBK, O_SBV, O_SBZ = 2048, 2560, 3072, 3584
O_MLAZ = 4096
O_CQ = 4608
O_CKV = 5120
O_KROT = 5376
IN_PAD = 5632

BQ = 256
BK = 256


def _cparams(*sem):
    return pltpu.CompilerParams(dimension_semantics=sem, vmem_limit_bytes=VMEM_LIMIT)


def _tile_of(n, cap=512):
    if n <= cap:
        return n
    for t in (1024, 768, 512, 384, 256, 128):
        if t <= cap and n % t == 0:
            return t
    raise ValueError(n)


def _rowwise(fn, rows, vecs, outs, reds=(), *, name, tile=512):
    norm = []
    for r in rows:
        if isinstance(r, tuple):
            arr, cb, w = r[:3]
            ro = r[3] if len(r) > 3 else 0
        else:
            arr, cb, w, ro = r, 0, r.shape[1], 0
        norm.append((arr, cb, w, ro))
    s_len = norm[0][0].shape[0]
    tile = min(tile, s_len)
    assert s_len % tile == 0
    n_row, n_vec, n_out, n_red = len(norm), len(vecs), len(outs), len(reds)

    def body(*refs):
        step = pl.program_id(0)
        row_refs = refs[:n_row]
        vec_refs = refs[n_row:n_row + n_vec]
        out_refs = refs[n_row + n_vec:n_row + n_vec + n_out]
        red_refs = refs[n_row + n_vec + n_out:]
        row_res, red_res = fn(*[r[...] for r in row_refs], *[v[...] for v in vec_refs])
        for o, val in zip(out_refs, row_res):
            o[...] = val.astype(o.dtype)
        if n_red:
            @pl.when(step == 0)
            def _():
                for r in red_refs:
                    r[...] = jnp.zeros(r.shape, r.dtype)
            for r, val in zip(red_refs, red_res):
                r[...] += val

    in_specs = []
    for arr, cb, w, ro in norm:
        in_specs.append(pl.BlockSpec((tile, w), functools.partial(lambda i, cb, rb: (i + rb, cb), cb=cb, rb=ro // tile)))
        assert ro % tile == 0
    for v in vecs:
        in_specs.append(pl.BlockSpec(v.shape, lambda i: (0, 0)))
    out_shape = [jax.ShapeDtypeStruct((s_len, w), dt) for w, dt in outs]
    out_specs = [pl.BlockSpec((tile, w), lambda i: (i, 0)) for w, _ in outs]
    out_shape += [jax.ShapeDtypeStruct((1, w), F32) for w in reds]
    out_specs += [pl.BlockSpec((1, w), lambda i: (0, 0)) for w in reds]
    res = pl.pallas_call(
        body, name=name, grid=(s_len // tile,), in_specs=in_specs, out_specs=out_specs, out_shape=out_shape,
        compiler_params=_cparams("arbitrary" if n_red else "parallel"),
    )(*[a for a, _, _, _ in norm], *vecs)
    return res


def _mm(a, b, *, ta=False, tb=False, out_dtype=F32, name, exchange=None):
    m, k = (a.shape[1], a.shape[0]) if ta else a.shape
    n = b.shape[0] if tb else b.shape[1]
    assert (b.shape[1] if tb else b.shape[0]) == k
    tm, tn, tk = _tile_of(m, 1024), _tile_of(n, 1024 if n <= 1024 else 512), _tile_of(k, 1024)
    ni, nj, nk = m // tm, n // tn, k // tk
    dims = (((0 if ta else 1,), (1 if tb else 0,)), ((), ()))
    n_ex = len(exchange.arrays) if exchange else 0

    def body(*refs):
        a_ref, b_ref = refs[:2]
        x_refs, o_ref, out_refs = refs[2:2 + n_ex], refs[2 + n_ex], refs[3 + n_ex:3 + 2 * n_ex]
        acc_ref, sems = refs[3 + 2 * n_ex], refs[4 + 2 * n_ex:]
        i, j, kk = pl.program_id(0), pl.program_id(1), pl.program_id(2)

        if exchange:
            @pl.when((i == 0) & (j == 0) & (kk == 0))
            def _():
                exchange.start(x_refs, out_refs, sems)

        @pl.when(kk == 0)
        def _():
            acc_ref[...] = jnp.zeros(acc_ref.shape, F32)

        acc_ref[...] += lax.dot_general(a_ref[...].astype(MXU_DTYPE), b_ref[...].astype(MXU_DTYPE), dims,
                                        preferred_element_type=F32)

        @pl.when(kk == nk - 1)
        def _():
            o_ref[...] = acc_ref[...].astype(o_ref.dtype)

        if exchange:
            @pl.when((i == ni - 1) & (j == nj - 1) & (kk == nk - 1))
            def _():
                exchange.wait(x_refs, out_refs, sems)

    a_spec = pl.BlockSpec((tk, tm), lambda i, j, kk: (kk, i)) if ta else pl.BlockSpec((tm, tk), lambda i, j, kk: (i, kk))
    b_spec = pl.BlockSpec((tn, tk), lambda i, j, kk: (j, kk)) if tb else pl.BlockSpec((tk, tn), lambda i, j, kk: (kk, j))
    o_spec = pl.BlockSpec((tm, tn), lambda i, j, kk: (i, j))
    o_shape = jax.ShapeDtypeStruct((m, n), out_dtype)
    if not exchange:
        return pl.pallas_call(
            body, name=name, grid=(ni, nj, nk), in_specs=[a_spec, b_spec], out_specs=o_spec, out_shape=o_shape,
            scratch_shapes=[pltpu.VMEM((tm, tn), F32)],
            compiler_params=_cparams("parallel", "parallel", "arbitrary"),
        )(a, b)
    res = pl.pallas_call(
        body, name=name, grid=(ni, nj, nk), in_specs=[a_spec, b_spec] + exchange.in_specs,
        out_specs=[o_spec] + exchange.out_specs, out_shape=[o_shape] + exchange.out_shape,
        scratch_shapes=[pltpu.VMEM((tm, tn), F32)] + exchange.scratch,
        compiler_params=_cparams("arbitrary", "arbitrary", "arbitrary"),
    )(a, b, *exchange.arrays)
    return res[0], res[1:]


_NT = (((1,), (1,)), ((), ()))
_TN = (((0,), (0,)), ((), ()))


def _dot(a, b):
    return jnp.dot(a, b, preferred_element_type=F32)


def _dot_nt(a, b):
    return lax.dot_general(a, b, _NT, preferred_element_type=F32)


def _dot_tn(a, b):
    return lax.dot_general(a, b, _TN, preferred_element_type=F32)


def _running_sum(x, tri):
    return _dot(x.astype(MXU_DTYPE), tri)


def _neg_softplus(z):
    u = jnp.exp2(jnp.abs(z) * (-1.0 / math.log(2.0)))
    return -jnp.maximum(z, 0.0) - jnp.log(1.0 + u)


GROUP = 2


def _walk_blocks(step, st, n, descending):
    groups = n // GROUP

    def many(t, s):
        first = (n - 1 - GROUP * t) if descending else GROUP * t
        return step([first - g if descending else first + g for g in range(GROUP)], s)

    def one(r, s):
        done = GROUP * groups + r
        return step([n - 1 - done if descending else done], s)

    st = lax.fori_loop(0, groups, many, st)
    if (BQ // BK) % GROUP == 0:
        return st
    return lax.fori_loop(0, n - GROUP * groups, one, st)


def _chains(js):
    return [(h, t) for t in range(len(js)) for h in range(2)]


def _rowsum(x):
    return jnp.sum(x, axis=1, keepdims=True)


def _attn_consts():
    row = lax.broadcasted_iota(jnp.int32, (BQ, BK), 0)
    col = lax.broadcasted_iota(jnp.int32, (BQ, BK), 1)
    trow = lax.broadcasted_iota(jnp.int32, (BK, BK), 0)
    tcol = lax.broadcasted_iota(jnp.int32, (BK, BK), 1)
    lane = lax.broadcasted_iota(jnp.int32, (BQ, LANES), 1)
    klane = lax.broadcasted_iota(jnp.int32, (BK, LANES), 1)
    return row, col, trow, tcol, lane, klane


DIAG = BQ // BK
assert BQ == DIAG * BK


def _diag_block(i):
    return i * DIAG


def _diag_blocks(jd, descending):
    places = list(reversed(range(DIAG))) if descending else list(range(DIAG))
    return [jd + g for g in places], places


def _key_slice(j):
    return pl.ds(pl.multiple_of(j * BK, BK), BK)


def _sb_forward(proj, s_len):
    nq = s_len // BQ
    assert s_len // BK <= HD
    qc, kc, vc = O_SBQ // LANES, O_SBK // LANES, O_SBV // LANES

    def body(q_ref, k_ref, v_ref, o_ref, tails_ref):
        i = pl.program_id(1)
        jd = _diag_block(i)
        row, col, trow, tcol, lane, klane = _attn_consts()
        strict = [col + g * BK < row for g in range(DIAG)]
        tri = (trow >= tcol).astype(MXU_DTYPE)
        q = q_ref[...] * 0.125
        qh = [jnp.where(lane < HD, q, 0.0).astype(MXU_DTYPE), jnp.where(lane >= HD, q, 0.0).astype(MXU_DTYPE)]

        km = [klane < HD, klane >= HD]

        def step(js, st, diag):
            carry, acc, tail = st
            chains = _chains(js)
            kj = [k_ref[_key_slice(j), :].astype(MXU_DTYPE) for j in js]
            vj = [v_ref[_key_slice(j), :].astype(MXU_DTYPE) for j in js]
            z = {(h, t): _dot_nt(qh[h], kj[t]) for h, t in chains}
            run = list(carry)
            suf, carry_in = {}, {}
            for h, t in chains:
                lom = _neg_softplus(z[h, t])
                if diag:
                    lom = jnp.where(strict[diag[t]], lom, 0.0)
                suf[h, t] = _running_sum(lom, tri)
                carry_in[h, t] = run[h]
                run[h] = run[h] + _rowsum(lom)
            for h, t in chains:
                a = jnp.exp(z[h, t] + suf[h, t] + carry_in[h, t])
                if diag:
                    a = jnp.where(strict[diag[t]], a, 0.0)
                acc = acc + _dot(a.astype(MXU_DTYPE), jnp.where(km[h], vj[t], 0))
                tail = jnp.where(lane == h * HD + js[t], carry_in[h, t], tail)
            return tuple(run), acc, tail

        zero = jnp.zeros((BQ, LANES), F32)
        diag_js, places = _diag_blocks(jd, True)
        st = step(diag_js, ((jnp.zeros((BQ, 1), F32),) * 2, zero, zero), places)
        st = _walk_blocks(lambda js, s: step(js, s, False), st, jd, True)
        o_ref[...] = st[1]
        tails_ref[...] = st[2]

    blk = pl.BlockSpec((BQ, LANES), lambda p, i: (i, p))
    out = jax.ShapeDtypeStruct((s_len, SB_W), F32)
    return pl.pallas_call(
        body, name="sb_fwd", grid=(HEADS // 2, nq),
        in_specs=[pl.BlockSpec((BQ, LANES), lambda p, i: (i, qc + p)),
                  pl.BlockSpec((s_len, LANES), lambda p, i: (0, kc + p)),
                  pl.BlockSpec((s_len, LANES), lambda p, i: (0, vc + p))],
        out_specs=[blk, blk], out_shape=[out, out],
        compiler_params=_cparams("parallel", "parallel"),
    )(proj, proj, proj)


def _sb_backward(proj, tails, do, s_len, exchange):
    nq = s_len // BQ
    qc, kc, vc = O_SBQ // LANES, O_SBK // LANES, O_SBV // LANES
    n_ex = len(exchange.arrays)

    def body(q_ref, k_ref, v_ref, tails_ref, do_ref, *rest):
        x_refs, (dq_ref, dk_ref, dv_ref) = rest[:n_ex], rest[n_ex:n_ex + 3]
        out_refs, sems = rest[n_ex + 3:2 * n_ex + 3], rest[2 * n_ex + 3:]
        p = pl.program_id(0)
        i = pl.program_id(1)
        jd = _diag_block(i)

        @pl.when((p == 0) & (i == 0))
        def _():
            exchange.start(x_refs, out_refs, sems)

        @pl.when(i == 0)
        def _():
            dk_ref[...] = jnp.zeros(dk_ref.shape, F32)
            dv_ref[...] = jnp.zeros(dv_ref.shape, F32)

        row, col, trow, tcol, lane, klane = _attn_consts()
        strict = [col + g * BK < row for g in range(DIAG)]
        tri = (trow >= tcol).astype(MXU_DTYPE)
        tri_p = (trow <= tcol).astype(MXU_DTYPE)
        q = q_ref[...] * 0.125
        tails_blk = tails_ref[...]
        do_blk = do_ref[...]
        hm = [lane < HD, lane >= HD]
        km = [klane < HD, klane >= HD]
        qh = [jnp.where(m, q, 0.0).astype(MXU_DTYPE) for m in hm]
        doh = [jnp.where(m, do_blk, 0.0).astype(MXU_DTYPE) for m in hm]

        def step(js, st, diag):
            before, dq = st
            chains = _chains(js)
            kj = [k_ref[_key_slice(j), :].astype(MXU_DTYPE) for j in js]
            vj = [v_ref[_key_slice(j), :].astype(MXU_DTYPE) for j in js]
            z = {(h, t): _dot_nt(qh[h], kj[t]) for h, t in chains}
            da = {(h, t): _dot_nt(doh[h], vj[t]) for h, t in chains}
            suf, sig = {}, {}
            for h, t in chains:
                lom = _neg_softplus(z[h, t])
                if diag:
                    lom = jnp.where(strict[diag[t]], lom, 0.0)
                suf[h, t] = _running_sum(lom, tri)
                sig[h, t] = jnp.exp(z[h, t] + lom)
            run = list(before)
            dl, pre, before_in = {}, {}, {}
            dk_add, dv_add = [None] * len(js), [None] * len(js)
            for h, t in chains:
                tail = _rowsum(jnp.where(lane == h * HD + js[t], tails_blk, 0.0))
                a = jnp.exp(z[h, t] + suf[h, t] + tail)
                if diag:
                    a = jnp.where(strict[diag[t]], a, 0.0)
                dl[h, t] = da[h, t] * a
                pre[h, t] = _dot(dl[h, t].astype(MXU_DTYPE), tri_p)
                dv_h = _dot_tn(a.astype(MXU_DTYPE), doh[h])
                dv_add[t] = dv_h if dv_add[t] is None else dv_add[t] + dv_h
                before_in[h, t] = run[h]
                run[h] = run[h] + _rowsum(dl[h, t])
            for h, t in chains:
                upto = before_in[h, t] + pre[h, t]
                dz = dl[h, t] - sig[h, t] * upto
                if diag:
                    dz = jnp.where(strict[diag[t]], dz, 0.0)
                dzb = dz.astype(MXU_DTYPE)
                dq = dq + _dot(dzb, jnp.where(km[h], kj[t], 0))
                dk_h = _dot_tn(dzb, qh[h])
                dk_add[t] = dk_h if dk_add[t] is None else dk_add[t] + dk_h
            for t, j in enumerate(js):
                dk_ref[_key_slice(j), :] += dk_add[t]
                dv_ref[_key_slice(j), :] += dv_add[t]
            return tuple(run), dq

        st = ((jnp.zeros((BQ, 1), F32),) * 2, jnp.zeros((BQ, LANES), F32))
        st = _walk_blocks(lambda js, s: step(js, s, False), st, jd, False)
        diag_js, places = _diag_blocks(jd, False)
        st = step(diag_js, st, places)
        dq_ref[...] = st[1] * 0.125

        @pl.when((p == HEADS // 2 - 1) & (i == nq - 1))
        def _():
            exchange.wait(x_refs, out_refs, sems)

    blk = pl.BlockSpec((BQ, LANES), lambda p, i: (i, p))
    full = pl.BlockSpec((s_len, LANES), lambda p, i: (0, p))
    out = jax.ShapeDtypeStruct((s_len, SB_W), F32)
    res = pl.pallas_call(
        body, name="sb_bwd", grid=(HEADS // 2, nq),
        in_specs=[pl.BlockSpec((BQ, LANES), lambda p, i: (i, qc + p)),
                  pl.BlockSpec((s_len, LANES), lambda p, i: (0, kc + p)),
                  pl.BlockSpec((s_len, LANES), lambda p, i: (0, vc + p)),
                  blk, blk] + exchange.in_specs,
        out_specs=[blk, full, full] + exchange.out_specs, out_shape=[out, out, out] + exchange.out_shape,
        scratch_shapes=exchange.scratch,
        compiler_params=_cparams("arbitrary", "arbitrary"),
    )(proj, proj, proj, tails, do, *exchange.arrays)
    return res[:3], res[3:]


def _pair_mask(rows, h):
    lane = lax.broadcasted_iota(jnp.int32, (rows, 2 * LANES), 1)
    rot = lane - LANES
    return (((lane < LANES) & (lane // HD == h))
            | ((lane >= LANES) & (rot < 2 * ROPE) & ((rot // (ROPE // 2)) % 2 == h)))


def _mla_forward(q_cat, k_cat, kv, s_len):
    nq = s_len // BQ
    scale = 1.0 / math.sqrt(QK_DIM)

    def body(q_ref, k_ref, v_ref, o_ref, lse_ref):
        i = pl.program_id(1)
        jd = _diag_block(i)
        row, col, trow, tcol, lane, klane = _attn_consts()
        causal = [col + g * BK <= row for g in range(DIAG)]
        q = q_ref[...]
        hm = [lane < HD, lane >= HD]
        km = [klane < HD, klane >= HD]
        qh = [jnp.where(_pair_mask(BQ, h), q, 0) for h in range(2)]

        def step(js, st, diag):
            m_run, l_run, acc = st
            chains = _chains(js)
            kj = [k_ref[_key_slice(j), :] for j in js]
            vj = [v_ref[_key_slice(j), :].astype(MXU_DTYPE) for j in js]
            s = {}
            for h, t in chains:
                s[h, t] = _dot_nt(qh[h], kj[t]) * scale
                if diag:
                    s[h, t] = jnp.where(causal[diag[t]], s[h, t], -jnp.inf)
            m_new, alpha, l_new = [], [], []
            for h in range(2):
                top = m_run[h]
                for t in range(len(js)):
                    top = jnp.maximum(top, jnp.max(s[h, t], axis=1, keepdims=True))
                m_new.append(top)
                alpha.append(jnp.exp(m_run[h] - top))
                l_new.append(alpha[h] * l_run[h])
            add = None
            for h, t in chains:
                pr = jnp.exp(s[h, t] - m_new[h])
                l_new[h] = l_new[h] + _rowsum(pr)
                part = _dot(pr.astype(MXU_DTYPE), jnp.where(km[h], vj[t], 0))
                add = part if add is None else add + part
            acc = jnp.where(hm[0], alpha[0], alpha[1]) * acc + add
            return tuple(m_new), tuple(l_new), acc

        st = ((jnp.full((BQ, 1), -1e30, F32),) * 2, (jnp.zeros((BQ, 1), F32),) * 2, jnp.zeros((BQ, LANES), F32))
        diag_js, places = _diag_blocks(jd, True)
        st = step(diag_js, st, places)
        m_run, l_run, acc = _walk_blocks(lambda js, s: step(js, s, False), st, jd, True)
        o_ref[...] = acc / jnp.where(hm[0], l_run[0], l_run[1])
        lse_ref[...] = jnp.where(hm[0], m_run[0] + jnp.log(l_run[0]), m_run[1] + jnp.log(l_run[1]))

    blk = pl.BlockSpec((BQ, LANES), lambda p, i: (i, p))
    out = jax.ShapeDtypeStruct((s_len, MLA_W), F32)
    return pl.pallas_call(
        body, name="mla_fwd", grid=(HEADS // 2, nq),
        in_specs=[pl.BlockSpec((BQ, 2 * LANES), lambda p, i: (i, p)),
                  pl.BlockSpec((s_len, 2 * LANES), lambda p, i: (0, p)),
                  pl.BlockSpec((s_len, LANES), lambda p, i: (0, MLA_W // LANES + p))],
        out_specs=[blk, blk], out_shape=[out, out],
        compiler_params=_cparams("parallel", "parallel"),
    )(q_cat, k_cat, kv)


def _mla_backward(q_cat, k_cat, kv, o, lse, do, s_len):
    nq = s_len // BQ
    scale = 1.0 / math.sqrt(QK_DIM)

    def body(q_ref, k_ref, v_ref, o_ref, lse_ref, do_ref, dq_ref, dk_ref, dv_ref):
        i = pl.program_id(1)

        @pl.when(i == 0)
        def _():
            dk_ref[...] = jnp.zeros(dk_ref.shape, F32)
            dv_ref[...] = jnp.zeros(dv_ref.shape, F32)

        jd = _diag_block(i)
        row, col, trow, tcol, lane, klane = _attn_consts()
        causal = [col + g * BK <= row for g in range(DIAG)]
        q = q_ref[...]
        o_blk = o_ref[...]
        do_blk = do_ref[...]
        lse_blk = lse_ref[...]
        hm = [lane < HD, lane >= HD]
        kpm = [_pair_mask(BK, h) for h in range(2)]
        qh = [jnp.where(_pair_mask(BQ, h), q, 0) for h in range(2)]
        doh_f = [jnp.where(m, do_blk, 0.0) for m in hm]
        doh = [d.astype(MXU_DTYPE) for d in doh_f]
        delta = [jnp.sum(d * o_blk, axis=1, keepdims=True) for d in doh_f]
        lse_h = [jnp.sum(jnp.where(lane == h * HD, lse_blk, 0.0), axis=1, keepdims=True) for h in range(2)]

        def step(js, st, diag):
            dq = st
            chains = _chains(js)
            kj = [k_ref[_key_slice(j), :] for j in js]
            vj = [v_ref[_key_slice(j), :].astype(MXU_DTYPE) for j in js]
            s = {(h, t): _dot_nt(qh[h], kj[t]) for h, t in chains}
            dp = {(h, t): _dot_nt(doh[h], vj[t]) for h, t in chains}
            adds = [[None] * len(js) for _ in range(2)]

            def accumulate(slot, t, part):
                adds[slot][t] = part if adds[slot][t] is None else adds[slot][t] + part

            for h, t in chains:
                pr = jnp.exp(s[h, t] * scale - lse_h[h])
                if diag:
                    pr = jnp.where(causal[diag[t]], pr, 0.0)
                dsb = (pr * (dp[h, t] - delta[h]) * scale).astype(MXU_DTYPE)
                dq = dq + _dot(dsb, jnp.where(kpm[h], kj[t], 0))
                accumulate(0, t, _dot_tn(dsb, qh[h]))
                accumulate(1, t, _dot_tn(pr.astype(MXU_DTYPE), doh[h]))
            for t, j in enumerate(js):
                dk_ref[_key_slice(j), :] += adds[0][t]
                dv_ref[_key_slice(j), :] += adds[1][t]
            return dq

        diag_js, places = _diag_blocks(jd, True)
        st = step(diag_js, jnp.zeros((BQ, 2 * LANES), F32), places)
        dq_ref[...] = _walk_blocks(lambda js, s: step(js, s, False), st, jd, True)

    blk = pl.BlockSpec((BQ, LANES), lambda p, i: (i, p))
    full = pl.BlockSpec((s_len, LANES), lambda p, i: (0, p))
    out = jax.ShapeDtypeStruct((s_len, MLA_W), F32)
    out_cat = jax.ShapeDtypeStruct((s_len, 2 * MLA_W), F32)
    return pl.pallas_call(
        body, name="mla_bwd", grid=(HEADS // 2, nq),
        in_specs=[pl.BlockSpec((BQ, 2 * LANES), lambda p, i: (i, p)),
                  pl.BlockSpec((s_len, 2 * LANES), lambda p, i: (0, p)),
                  pl.BlockSpec((s_len, LANES), lambda p, i: (0, MLA_W // LANES + p)),
                  blk, blk, blk],
        out_specs=[pl.BlockSpec((BQ, 2 * LANES), lambda p, i: (i, p)),
                   pl.BlockSpec((s_len, 2 * LANES), lambda p, i: (0, p)), full],
        out_shape=[out_cat, out_cat, out],
        compiler_params=_cparams("arbitrary", "arbitrary"),
    )(q_cat, k_cat, kv, o, lse, do)


def _mesh_pos():
    return lax.axis_index("x"), lax.axis_index("y"), lax.axis_index("c")


def _dev_index(px, py, pc):
    return 4 * px + 2 * py + pc


def _all_gather(block, name):
    return _all_gather_parts([block], name)[0]


def _all_gather_parts(blocks, name):
    n = len(blocks)

    def body(*refs):
        x_refs, out_refs = refs[:n], refs[n:2 * n]
        send_sems, recv_sems, local_sems = refs[2 * n:]
        x, y, c = _mesh_pos()
        me, sibling = (x, y, c), (x, y, 1 - c)
        chips = [(1 - x, y), (x, 1 - y), (1 - x, 1 - y)]

        def copy(a, k, blockpos, to, src=None):
            slot = out_refs[a].at[_dev_index(*blockpos)]
            return pltpu.make_async_remote_copy(
                src_ref=slot if src is None else src, dst_ref=slot,
                send_sem=send_sems.at[7 * a + k], recv_sem=recv_sems.at[7 * a + k],
                device_id=to, device_id_type=pl.DeviceIdType.MESH)

        mine = [pltpu.make_async_copy(x_refs[a], out_refs[a].at[_dev_index(*me)], local_sems.at[a]) for a in range(n)]
        for cp in mine:
            cp.start()
        first = []
        for a in range(n):
            first.append(copy(a, 0, me, sibling, src=x_refs[a]))
            first += [copy(a, 1 + j, me, (*chip, c), src=x_refs[a]) for j, chip in enumerate(chips)]
        for cp in first:
            cp.start()
        passed = []
        for j, chip in enumerate(chips):
            for a in range(n):
                copy(a, 1 + j, (*chip, c), me).wait_recv()
                passed.append(copy(a, 4 + j, (*chip, c), sibling))
                passed[-1].start()
        for a in range(n):
            copy(a, 0, sibling, me).wait_recv()
            for j, chip in enumerate(chips):
                copy(a, 4 + j, (*chip, 1 - c), me).wait_recv()
        for cp in first + passed:
            cp.wait_send()
        for cp in mine:
            cp.wait()

    return pl.pallas_call(
        body, name=name,
        out_shape=[jax.ShapeDtypeStruct((N_DEV,) + b.shape, b.dtype) for b in blocks],
        in_specs=[pl.BlockSpec(memory_space=pl.ANY)] * n, out_specs=[pl.BlockSpec(memory_space=pl.ANY)] * n,
        scratch_shapes=[pltpu.SemaphoreType.DMA((7 * n,)), pltpu.SemaphoreType.DMA((7 * n,)),
                        pltpu.SemaphoreType.DMA((n,))],
    )(*blocks)


class _Exchange:
    def __init__(self, arrays):
        self.arrays = list(arrays)
        n = len(self.arrays)
        self.in_specs = [pl.BlockSpec(memory_space=pl.ANY)] * n
        self.out_specs = [pl.BlockSpec(memory_space=pl.ANY)] * n
        self.out_shape = [jax.ShapeDtypeStruct(a.shape, a.dtype) for a in self.arrays]
        self.scratch = [pltpu.SemaphoreType.DMA((7 * n,)), pltpu.SemaphoreType.DMA((7 * n,)),
                        pltpu.SemaphoreType.DMA((n,))]

    def _copies(self, x_refs, out_refs, sems, with_arrivals):
        send_sems, recv_sems, local_sems = sems
        x, y, c = _mesh_pos()
        me = _dev_index(x, y, c)
        flips = [(fx, fy, fc) for fx in (0, 1) for fy in (0, 1) for fc in (0, 1)][1:]
        peers = [(1 - x if fx else x, 1 - y if fy else y, 1 - c if fc else c) for fx, fy, fc in flips]
        mine, sends, arrivals = [], [], []
        for a in range(len(self.arrays)):
            mine.append(pltpu.make_async_copy(x_refs[a].at[me], out_refs[a].at[me], local_sems.at[a]))
            for k, peer in enumerate(peers):
                sends.append(pltpu.make_async_remote_copy(
                    src_ref=x_refs[a].at[_dev_index(*peer)], dst_ref=out_refs[a].at[me],
                    send_sem=send_sems.at[7 * a + k], recv_sem=recv_sems.at[7 * a + k],
                    device_id=peer, device_id_type=pl.DeviceIdType.MESH))
                if not with_arrivals:
                    continue
                arrivals.append(pltpu.make_async_remote_copy(
                    src_ref=x_refs[a].at[me], dst_ref=out_refs[a].at[_dev_index(*peer)],
                    send_sem=send_sems.at[7 * a + k], recv_sem=recv_sems.at[7 * a + k],
                    device_id=peer, device_id_type=pl.DeviceIdType.MESH))
        return mine, sends, arrivals

    def start(self, x_refs, out_refs, sems):
        mine, sends, _ = self._copies(x_refs, out_refs, sems, False)
        for cp in mine + sends:
            cp.start()

    def wait(self, x_refs, out_refs, sems):
        mine, sends, arrivals = self._copies(x_refs, out_refs, sems, True)
        for cp in arrivals:
            cp.wait_recv()
        for cp in sends:
            cp.wait_send()
        for cp in mine:
            cp.wait()


def _sum_blocks(parts, name):
    n, r, c = parts.shape
    row_tiles = [t for t in range(16, min(r, 2048) + 1, 16) if r % t == 0]
    if row_tiles:
        tr, tc = max(row_tiles), c
    else:
        tr, tc = r, 2 * LANES
    assert c % tc == 0

    def body(p_ref, o_ref):
        acc = p_ref[0].astype(F32)
        for s in range(1, n):
            acc = acc + p_ref[s].astype(F32)
        o_ref[...] = acc

    return pl.pallas_call(
        body, name=name, grid=(r // tr, c // tc),
        in_specs=[pl.BlockSpec((n, tr, tc), lambda i, j: (0, i, j))],
        out_specs=pl.BlockSpec((tr, tc), lambda i, j: (i, j)),
        out_shape=jax.ShapeDtypeStruct((r, c), F32),
        compiler_params=_cparams("parallel", "parallel"),
    )(parts)


def _sigmoid(x):
    return 1.0 / (1.0 + jnp.exp(-x))


def _silu(x):
    return x * _sigmoid(x)


def _silu_grad(x):
    s = _sigmoid(x)
    return s * (1.0 + x * (1.0 - s))


def _colsum(x):
    return jnp.sum(x, axis=0, keepdims=True)


def _rms(x):
    return lax.rsqrt(jnp.mean(x * x, axis=-1, keepdims=True) + EPS)


def _rms_bwd(xn, r, dxn):
    return r * (dxn - xn * jnp.mean(dxn * xn, axis=-1, keepdims=True))


def _adamw(w, g, m, v):
    m = ADAM_B1 * m + (1.0 - ADAM_B1) * g
    v = ADAM_B2 * v + (1.0 - ADAM_B2) * jnp.square(g)
    m_hat = m / (1.0 - ADAM_B1 ** ADAM_STEP)
    v_hat = v / (1.0 - ADAM_B2 ** ADAM_STEP)
    delta = -ADAM_LR * (m_hat / (jnp.sqrt(v_hat) + ADAM_EPS) + ADAM_WD * w)
    return delta, m, v


def _adamw_call(w, g, m, v, name):
    r, c = w.shape
    if r % 256 == 0:
        tr, tc = 256, c
    elif r * c <= 256 * 1024 or c % (2 * LANES):
        tr, tc = r, c
    else:
        tr, tc = r, 2 * LANES

    def body(w_ref, g_ref, m_ref, v_ref, d_out, m_out, v_out):
        d_out[...], m_out[...], v_out[...] = _adamw(w_ref[...], g_ref[...], m_ref[...], v_ref[...])

    spec = pl.BlockSpec((tr, tc), lambda i, j: (i, j))
    return pl.pallas_call(
        body, name=name, grid=(r // tr, c // tc), in_specs=[spec] * 4, out_specs=[spec] * 3,
        out_shape=[jax.ShapeDtypeStruct((r, c), F32)] * 3, compiler_params=_cparams("parallel", "parallel"),
    )(w, g, m, v)


def _uq_to_kernel_layout(w):
    lead = w.shape[:-1]
    t = w.reshape(lead + (HEADS, QK_DIM))
    return jnp.concatenate([t[..., :NOPE].reshape(lead + (HEADS * NOPE,)),
                            t[..., NOPE:NOPE + ROPE // 2].reshape(lead + (LANES,)),
                            t[..., NOPE + ROPE // 2:].reshape(lead + (LANES,))], axis=-1)


def _uq_from_kernel_layout(w):
    lead = w.shape[:-1]
    nope = w[..., :HEADS * NOPE].reshape(lead + (HEADS, NOPE))
    r1 = w[..., HEADS * NOPE:HEADS * NOPE + LANES].reshape(lead + (HEADS, ROPE // 2))
    r2 = w[..., HEADS * NOPE + LANES:].reshape(lead + (HEADS, ROPE // 2))
    return jnp.concatenate([nope, r1, r2], axis=-1).reshape(lead + (HEADS * QK_DIM,))


def _ukv_to_kernel_layout(w):
    lead = w.shape[:-1]
    t = w.reshape(lead + (HEADS, NOPE + HD))
    return jnp.concatenate([t[..., :NOPE].reshape(lead + (HEADS * NOPE,)),
                            t[..., NOPE:].reshape(lead + (HEADS * HD,))], axis=-1)


def _ukv_from_kernel_layout(w):
    lead = w.shape[:-1]
    kn = w[..., :HEADS * NOPE].reshape(lead + (HEADS, NOPE))
    vv = w[..., HEADS * NOPE:].reshape(lead + (HEADS, HD))
    return jnp.concatenate([kn, vv], axis=-1).reshape(lead + (HEADS * (NOPE + HD),))


def _w_in_t_to_kernel_layout(wt):
    sb = wt[0:2048]
    c_q = wt[2048:2432]
    c_kv = wt[2432:2688]
    k_rot = wt[2688:2720]
    mla_z = wt[2720:3232]
    gates = wt[3232:5280]
    zeros = jnp.zeros((LANES, wt.shape[1]), wt.dtype)
    k1 = jnp.tile(k_rot[:ROPE // 2], (HEADS, 1))
    k2 = jnp.tile(k_rot[ROPE // 2:], (HEADS, 1))
    return jnp.concatenate([gates, sb, mla_z, c_q, zeros, c_kv, k1, k2], axis=0)


def _w_in_t_from_kernel_layout(gt, g_rot):
    return jnp.concatenate([gt[O_SBQ:O_SBQ + 2048], gt[O_CQ:O_CQ + Q_RANK], gt[O_CKV:O_CKV + KV_RANK], g_rot,
                            gt[O_MLAZ:O_MLAZ + MLA_W], gt[O_GA:O_GA + 2 * D]], axis=0)


def kernel(x, c, positions, w_ada, b_ada, norm_gain, w_in, q_norm_gain, w_uq, kv_norm_gain, w_ukv, w_branch_a, w_branch_b, w_out, final_norm_gain, loss_target, m_w_ada, m_b_ada, m_norm_gain, m_w_in, m_q_norm_gain, m_w_uq, m_kv_norm_gain, m_w_ukv, m_w_branch_a, m_w_branch_b, m_w_out, m_final_norm_gain, v_w_ada, v_b_ada, v_norm_gain, v_w_in, v_q_norm_gain, v_w_uq, v_kv_norm_gain, v_w_ukv, v_w_branch_a, v_w_branch_b, v_w_out, v_final_norm_gain):
    s_len = x.shape[1]
    me = _dev_index(*_mesh_pos())
    x2d = x[0]
    tgt = loss_target[0]

    w_in_t = w_in[0].T.astype(BF16)
    big = [w_uq[0], w_ukv[0], w_branch_a[0], w_branch_b[0], w_out[0]]
    big_sizes = [int(w.size) for w in big]
    packed = jnp.concatenate([w.astype(BF16).reshape(-1, LANES) for w in big], axis=0)
    g_in_t, c_all = _all_gather_parts([w_in_t, c.reshape(8, LANES)], "gather_w_in")
    c_all = c_all.reshape(N_DEV, D)
    w_in_kt = _w_in_t_to_kernel_layout(g_in_t.reshape(N_DEV * w_in_t.shape[0], D))

    mod_cols = _mm(c_all, w_ada[0], name="ada_mod")
    mod_all = _all_gather(mod_cols, "gather_mod")
    mod = lax.dynamic_index_in_dim(mod_all, me, axis=1, keepdims=False).reshape(1, 3 * D)
    mod_shift, mod_scale, mod_gate = mod[:, :D], mod[:, D:2 * D], mod[:, 2 * D:]
    b_shift, b_scale, b_gate = b_ada[:, :D], b_ada[:, D:2 * D], b_ada[:, 2 * D:]
    g1 = norm_gain
    gq, gkv = q_norm_gain, kv_norm_gain
    gf = final_norm_gain.reshape(1, D)

    def f_h(x_, g1_, ms, bs, msc, bsc):
        xn = x_ * _rms(x_)
        return (xn * g1_ * (1.0 + (msc + bsc)) + (ms + bs),), ()

    (h,) = _rowwise(f_h, [x2d], [g1, mod_shift, b_shift, mod_scale, b_scale], [(D, BF16)], name="ada_norm")
    proj, (gathered,) = _mm(h, w_in_kt, tb=True, name="proj_in",
                            exchange=_Exchange([jnp.broadcast_to(packed[None], (N_DEV,) + packed.shape)]))
    offs = [0]
    for n in big_sizes:
        offs.append(offs[-1] + n // LANES)

    def unpack(t, shape):
        return gathered[:, offs[t]:offs[t + 1], :].reshape((N_DEV,) + shape)

    def cols(t, shape):
        return unpack(t, shape).transpose(1, 0, 2).reshape(shape[0], N_DEV * shape[1])

    w_uq_k = _uq_to_kernel_layout(cols(0, big[0].shape))
    w_ukv_k = _ukv_to_kernel_layout(cols(1, big[1].shape))
    w_a_f = cols(2, big[2].shape)
    w_b_f = cols(3, big[3].shape)
    w_out_f = unpack(4, big[4].shape).reshape(D, D)

    o_a, sb_tails = _sb_forward(proj, s_len)

    def f_lat(cq, ckv, gq_, gkv_):
        return (cq * _rms(cq) * gq_, ckv * _rms(ckv) * gkv_), ()

    cq_n, ckv_n = _rowwise(f_lat, [(proj, O_CQ // Q_RANK, Q_RANK), (proj, O_CKV // KV_RANK, KV_RANK)], [gq, gkv],
                           [(Q_RANK, BF16), (KV_RANK, BF16)], name="latent_norm")
    q_mla = _mm(cq_n, w_uq_k, name="q_up")
    kv = _mm(ckv_n, w_ukv_k, name="kv_up")

    inv_freq = ROPE_BASE ** (-jnp.arange(0, ROPE, 2, dtype=F32) / ROPE)
    inv_freq_t = jnp.tile(inv_freq, HEADS).reshape(1, LANES)
    pos_col = positions.reshape(s_len, 1).astype(F32)

    pairs = HEADS // 2

    def f_rope(pos, qn, q1, q2, kn, k1, k2, freq):
        ang = pos * freq
        cs, sn = jnp.cos(ang), jnp.sin(ang)
        q1r, q2r = q1 * cs - q2 * sn, q1 * sn + q2 * cs
        k1r, k2r = k1 * cs - k2 * sn, k1 * sn + k2 * cs
        lane = lax.broadcasted_iota(jnp.int32, q1.shape, 1)
        first, second = lane < ROPE, (lane >= ROPE) & (lane < 2 * ROPE)
        k_rot = jnp.where(first, k1r, jnp.where(second, k2r, 0.0))
        q_parts, k_parts = [], []
        for p in range(pairs):
            q_rot = jnp.where(first, pltpu.roll(q1r, (LANES - ROPE * p) % LANES, 1),
                              jnp.where(second, pltpu.roll(q2r, (LANES + ROPE - ROPE * p) % LANES, 1), 0.0))
            q_parts += [qn[:, LANES * p:LANES * (p + 1)], q_rot]
            k_parts += [kn[:, LANES * p:LANES * (p + 1)], k_rot]
        return (jnp.concatenate(q_parts, axis=1), jnp.concatenate(k_parts, axis=1), cs, sn), ()

    q_cat, k_cat, cos_t, sin_t = _rowwise(
        f_rope, [pos_col, (q_mla, 0, MLA_W), (q_mla, 4, LANES), (q_mla, 5, LANES), (kv, 0, MLA_W),
                 (proj, O_KROT // LANES, LANES), (proj, O_KROT // LANES + 1, LANES)], [inv_freq_t],
        [(2 * MLA_W, BF16), (2 * MLA_W, BF16), (LANES, F32), (LANES, F32)], name="rope")

    o_b, lse = _mla_forward(q_cat, k_cat, kv, s_len)

    def f_gate(oa, za, ob, zb):
        return (oa * _silu(za), ob * _silu(zb)), ()

    ya_in, yb_in = _rowwise(f_gate, [o_a, (proj, O_SBZ // SB_W, SB_W), o_b, (proj, O_MLAZ // MLA_W, MLA_W)], [],
                            [(SB_W, BF16), (MLA_W, BF16)], name="branch_gate")
    y_a = _mm(ya_in, w_a_f, name="branch_a")
    y_b = _mm(yb_in, w_b_f, name="branch_b")

    def f_merge(ga, gb, ya, yb):
        return (_sigmoid(ga) * ya + _sigmoid(gb) * yb,), ()

    (merged,) = _rowwise(f_merge, [(proj, O_GA // D, D), (proj, O_GB // D, D), y_a, y_b], [], [(D, BF16)], name="merge")
    out = _mm(merged, w_out_f, name="out_proj")

    def f_loss(x_, out_, t_, mg, bg, gf_):
        gate = mg + bg
        x2 = x_ + gate * out_
        r2 = _rms(x2)
        xn2 = x2 * r2
        err = xn2 * gf_ - t_
        loss = jnp.full((1, LANES), 0.5 / D, F32) * jnp.sum(err * err)
        dy = err * (1.0 / D)
        dx2 = _rms_bwd(xn2, r2, dy * gf_)
        return (dx2, dx2 * gate), (loss, _colsum(dy * xn2), _colsum(dx2 * out_))

    dx2, d_out, loss_part, d_gf, d_gate = _rowwise(
        f_loss, [x2d, out, tgt], [mod_gate, b_gate, gf], [(D, F32), (D, BF16)], [LANES, D, D], name="loss_head")

    d_merged = _mm(d_out, w_out_f, tb=True, name="d_merged")
    dw_out = _mm(merged, d_out, ta=True, name="dw_out")

    def f_dmerge(dm, ga, gb, ya, yb):
        sa, sb = _sigmoid(ga), _sigmoid(gb)
        return (dm * sa, dm * sb, dm * ya * sa * (1.0 - sa), dm * yb * sb * (1.0 - sb)), ()

    d_ya, d_yb, d_ga, d_gb = _rowwise(f_dmerge, [d_merged, (proj, O_GA // D, D), (proj, O_GB // D, D), y_a, y_b], [],
                                      [(D, BF16)] * 4, name="d_merge")
    dw_a = _mm(ya_in, d_ya, ta=True, name="dw_branch_a")
    dw_b = _mm(yb_in, d_yb, ta=True, name="dw_branch_b")
    d_ya_in = _mm(d_ya, w_a_f, tb=True, name="d_branch_a")
    d_yb_in = _mm(d_yb, w_b_f, tb=True, name="d_branch_b")

    def f_dgate(da, oa, za, db, ob, zb):
        return (da * _silu(za), da * oa * _silu_grad(za), db * _silu(zb), db * ob * _silu_grad(zb)), ()

    d_oa, d_sbz, d_ob, d_mlaz = _rowwise(
        f_dgate, [d_ya_in, o_a, (proj, O_SBZ // SB_W, SB_W), d_yb_in, o_b, (proj, O_MLAZ // MLA_W, MLA_W)], [],
        [(SB_W, F32), (SB_W, BF16), (MLA_W, F32), (MLA_W, BF16)], name="d_branch_gate")

    dq_cat, dk_cat, dv_b = _mla_backward(q_cat, k_cat, kv, o_b, lse, d_ob, s_len)

    def f_drope(dq, dk, dv_, cs, sn):
        lane = lax.broadcasted_iota(jnp.int32, cs.shape, 1)
        first, second = lane < ROPE, (lane >= ROPE) & (lane < 2 * ROPE)
        dq1 = dq2 = dk1 = dk2 = None
        for p in range(pairs):
            q_rot = dq[:, LANES * (2 * p + 1):LANES * (2 * p + 2)]
            k_rot = dk[:, LANES * (2 * p + 1):LANES * (2 * p + 2)]
            parts = (pltpu.roll(jnp.where(first, q_rot, 0.0), (ROPE * p) % LANES, 1),
                     pltpu.roll(jnp.where(second, q_rot, 0.0), (LANES - ROPE + ROPE * p) % LANES, 1),
                     jnp.where(first, k_rot, 0.0), jnp.where(second, k_rot, 0.0))
            if p == 0:
                dq1, dq2, dk1, dk2 = parts
            else:
                dq1, dq2, dk1, dk2 = dq1 + parts[0], dq2 + parts[1], dk1 + parts[2], dk2 + parts[3]
        dqn_ = [dq[:, 2 * LANES * p:2 * LANES * p + LANES] for p in range(pairs)]
        dkn_ = [dk[:, 2 * LANES * p:2 * LANES * p + LANES] for p in range(pairs)]
        return (jnp.concatenate(dqn_ + [dq1 * cs + dq2 * sn, dq2 * cs - dq1 * sn], axis=1),
                jnp.concatenate(dkn_ + [dv_], axis=1),
                jnp.concatenate([dk1 * cs + dk2 * sn, dk2 * cs - dk1 * sn], axis=1)), ()

    dq_k, dkv_k, d_krot = _rowwise(f_drope, [dq_cat, dk_cat, dv_b, cos_t, sin_t], [],
                                   [(HEADS * QK_DIM, BF16), (2 * MLA_W, BF16), (2 * LANES, BF16)], name="d_rope")
    dw_uq_k = _mm(cq_n, dq_k, ta=True, name="dw_uq")
    dw_ukv_k = _mm(ckv_n, dkv_k, ta=True, name="dw_ukv")
    d_cqn = _mm(dq_k, w_uq_k, tb=True, name="d_cq_norm")
    d_ckvn = _mm(dkv_k, w_ukv_k, tb=True, name="d_ckv_norm")

    def f_dlat(cq, dcqn, ckv, dckvn, gq_, gkv_):
        rq, rkv = _rms(cq), _rms(ckv)
        cqn, ckvn = cq * rq, ckv * rkv
        return ((_rms_bwd(cqn, rq, dcqn * gq_), _rms_bwd(ckvn, rkv, dckvn * gkv_)),
                (_colsum(dcqn * cqn), _colsum(dckvn * ckvn)))

    d_cq, d_ckv, d_gq, d_gkv = _rowwise(
        f_dlat, [(proj, O_CQ // Q_RANK, Q_RANK), d_cqn, (proj, O_CKV // KV_RANK, KV_RANK), d_ckvn], [gq, gkv],
        [(Q_RANK, BF16), (KV_RANK, BF16)], [Q_RANK, KV_RANK], name="d_latent_norm")

    def col_blocks(g):
        kdim, n8 = g.shape
        return g.astype(BF16).reshape(kdim, N_DEV, n8 // N_DEV).transpose(1, 0, 2).reshape(N_DEV, -1, LANES)

    g_blocks = jnp.concatenate([col_blocks(_uq_from_kernel_layout(dw_uq_k)), col_blocks(_ukv_from_kernel_layout(dw_ukv_k)),
                                col_blocks(dw_a), col_blocks(dw_b), dw_out.astype(BF16).reshape(N_DEV, -1, LANES)], axis=1)
    (d_sbq, d_sbk, d_sbv), (g_recv,) = _sb_backward(proj, sb_tails, d_oa, s_len, _Exchange([g_blocks]))

    d_proj = jnp.concatenate([d_ga, d_gb, d_sbq.astype(BF16), d_sbk.astype(BF16), d_sbv.astype(BF16), d_sbz, d_mlaz,
                              d_cq, jnp.zeros((s_len, LANES), BF16), d_ckv, d_krot], axis=1)
    dw_in_kt = _mm(d_proj, h, ta=True, name="dw_in")

    def krot_body(t_ref, o_ref):
        half = ROPE // 2
        for part in range(2):
            acc = t_ref[part * LANES:part * LANES + half, :]
            for hh in range(1, HEADS):
                acc = acc + t_ref[part * LANES + hh * half:part * LANES + (hh + 1) * half, :]
            o_ref[part * half:(part + 1) * half, :] = acc

    dw_krot = pl.pallas_call(krot_body, name="dw_krot_sum", out_shape=jax.ShapeDtypeStruct((ROPE, D), F32))(
        dw_in_kt[O_KROT:O_KROT + 2 * LANES])

    dw_in_t = _w_in_t_from_kernel_layout(dw_in_kt, dw_krot)
    g_in_blocks = dw_in_t.astype(BF16).reshape(N_DEV, -1, D)
    dh, (g_in_recv,) = _mm(d_proj, w_in_kt, name="d_h", exchange=_Exchange([g_in_blocks]))

    def f_dx(x_, dh_, dx2_, g1_, msc, bsc):
        r = _rms(x_)
        xn = x_ * r
        dn1 = dh_ * (1.0 + (msc + bsc))
        return ((dx2_ + _rms_bwd(xn, r, dn1 * g1_),),
                (_colsum(dh_), _colsum(dh_ * (xn * g1_)), _colsum(dn1 * xn)))

    grad_x2d, d_shift, d_scale, d_g1 = _rowwise(f_dx, [x2d, dh, dx2], [g1, mod_scale, b_scale], [(D, F32)],
                                                [D, D, D], name="d_ada_norm")

    g_in_sum_t = _sum_blocks(g_in_recv, "sum_grads_w_in")
    g_sum = _sum_blocks(g_recv, "sum_grads")
    g_big = [g_sum[offs[t]:offs[t + 1]].reshape(big[t].shape) for t in range(5)]

    small = jnp.concatenate([d_shift, d_scale, d_gate, d_g1, d_gq, d_gkv, d_gf, loss_part], axis=1)
    n_small = small.shape[1]
    pad = (-n_small) % (8 * LANES)
    small = jnp.pad(small, ((0, 0), (0, pad))).reshape(-1, LANES)
    small_all = _all_gather(small, "gather_small")
    small_sum = _sum_blocks(small_all, "sum_small").reshape(1, -1)
    g_b_ada = small_sum[:, :3 * D]
    g_g1 = small_sum[:, 3 * D:4 * D]
    g_gq = small_sum[:, 4 * D:4 * D + Q_RANK]
    g_gkv = small_sum[:, 4 * D + Q_RANK:4 * D + Q_RANK + KV_RANK]
    g_gf = small_sum[:, 4 * D + Q_RANK + KV_RANK:4 * D + Q_RANK + KV_RANK + D]

    dmod_all = small_all.reshape(N_DEV, -1)[:, :3 * D]
    dmod_cols = lax.dynamic_slice_in_dim(dmod_all, me * (3 * D // N_DEV), 3 * D // N_DEV, axis=1)
    g_w_ada = _mm(c_all, dmod_cols, ta=True, name="dw_ada")

    loss = small_sum[0, n_small - LANES]

    names = ["w_ada", "b_ada", "norm_gain", "w_in", "q_norm_gain", "w_uq", "kv_norm_gain", "w_ukv",
             "w_branch_a", "w_branch_b", "w_out", "final_norm_gain"]
    weights = dict(w_ada=w_ada, b_ada=b_ada, norm_gain=norm_gain, w_in=w_in, q_norm_gain=q_norm_gain, w_uq=w_uq,
                   kv_norm_gain=kv_norm_gain, w_ukv=w_ukv, w_branch_a=w_branch_a, w_branch_b=w_branch_b, w_out=w_out,
                   final_norm_gain=final_norm_gain)
    moms = dict(w_ada=m_w_ada, b_ada=m_b_ada, norm_gain=m_norm_gain, w_in=m_w_in, q_norm_gain=m_q_norm_gain,
                w_uq=m_w_uq, kv_norm_gain=m_kv_norm_gain, w_ukv=m_w_ukv, w_branch_a=m_w_branch_a,
                w_branch_b=m_w_branch_b, w_out=m_w_out, final_norm_gain=m_final_norm_gain)
    vels = dict(w_ada=v_w_ada, b_ada=v_b_ada, norm_gain=v_norm_gain, w_in=v_w_in, q_norm_gain=v_q_norm_gain,
                w_uq=v_w_uq, kv_norm_gain=v_kv_norm_gain, w_ukv=v_w_ukv, w_branch_a=v_w_branch_a,
                w_branch_b=v_w_branch_b, w_out=v_w_out, final_norm_gain=v_final_norm_gain)
    grads2d = dict(w_ada=g_w_ada, b_ada=g_b_ada, norm_gain=g_g1, w_in=g_in_sum_t, q_norm_gain=g_gq, w_uq=g_big[0],
                   kv_norm_gain=g_gkv, w_ukv=g_big[1], w_branch_a=g_big[2], w_branch_b=g_big[3], w_out=g_big[4],
                   final_norm_gain=g_gf)

    grads, deltas, new_m, new_v = [], [], [], []
    for n in names:
        w = weights[n]
        if n == "w_in":
            to2d = lambda t: t[0].T
            back = lambda t: t.T[None]
        else:
            shape2d = grads2d[n].shape
            to2d = lambda t, s=shape2d: t.reshape(s)
            back = lambda t, s=w.shape: t.reshape(s)
        d_, m_, v_ = _adamw_call(to2d(w), grads2d[n], to2d(moms[n]), to2d(vels[n]), "adamw_" + n)
        grads.append(back(grads2d[n]))
        deltas.append(back(d_))
        new_m.append(back(m_))
        new_v.append(back(v_))

    return (loss, grad_x2d.reshape(x.shape), *grads, *deltas, *new_m, *new_v)
```

```python
import functools
import math

import jax
import jax.numpy as jnp
from jax import lax
from jax.experimental import pallas as pl
from jax.experimental.pallas import tpu as pltpu

F32 = jnp.float32
BF16 = jnp.bfloat16
MXU_DTYPE = jnp.bfloat16

N_DEV = 8
D = 1024
HEADS = 8
HD = 64
SB_W = 512
MLA_W = 512
Q_RANK = 384
KV_RANK = 256
ROPE = 32
NOPE = 64
QK_DIM = NOPE + ROPE
EPS = 1e-6
ROPE_BASE = 10000.0

ADAM_LR = 0.001
ADAM_B1 = 0.9
ADAM_B2 = 0.999
ADAM_EPS = 1e-08
ADAM_WD = 0.01
ADAM_STEP = 10

LANES = 128
VMEM_LIMIT = 48 * 1024 * 1024

O_GA, O_GB = 0, 1024
O_SBQ, O_SBK, O_SBV, O_SBZ = 2048, 2560, 3072, 3584
O_MLAZ = 4096
O_CQ = 4608
O_CKV = 5120
O_KROT = 5376
IN_PAD = 5632

BQ = 256
BK = 256


def _cparams(*sem):
    return pltpu.CompilerParams(dimension_semantics=sem, vmem_limit_bytes=VMEM_LIMIT)


def _tile_of(n, cap=512):
    if n <= cap:
        return n
    for t in (1024, 768, 512, 384, 256, 128):
        if t <= cap and n % t == 0:
            return t
    raise ValueError(n)


def _rowwise(fn, rows, vecs, outs, reds=(), *, name, tile=512):
    norm = []
    for r in rows:
        if isinstance(r, tuple):
            arr, cb, w = r[:3]
            ro = r[3] if len(r) > 3 else 0
        else:
            arr, cb, w, ro = r, 0, r.shape[1], 0
        norm.append((arr, cb, w, ro))
    s_len = norm[0][0].shape[0]
    tile = min(tile, s_len)
    assert s_len % tile == 0
    n_row, n_vec, n_out, n_red = len(norm), len(vecs), len(outs), len(reds)

    def body(*refs):
        step = pl.program_id(0)
        row_refs = refs[:n_row]
        vec_refs = refs[n_row:n_row + n_vec]
        out_refs = refs[n_row + n_vec:n_row + n_vec + n_out]
        red_refs = refs[n_row + n_vec + n_out:]
        row_res, red_res = fn(*[r[...] for r in row_refs], *[v[...] for v in vec_refs])
        for o, val in zip(out_refs, row_res):
            o[...] = val.astype(o.dtype)
        if n_red:
            @pl.when(step == 0)
            def _():
                for r in red_refs:
                    r[...] = jnp.zeros(r.shape, r.dtype)
            for r, val in zip(red_refs, red_res):
                r[...] += val

    in_specs = []
    for arr, cb, w, ro in norm:
        in_specs.append(pl.BlockSpec((tile, w), functools.partial(lambda i, cb, rb: (i + rb, cb), cb=cb, rb=ro // tile)))
        assert ro % tile == 0
    for v in vecs:
        in_specs.append(pl.BlockSpec(v.shape, lambda i: (0, 0)))
    out_shape = [jax.ShapeDtypeStruct((s_len, w), dt) for w, dt in outs]
    out_specs = [pl.BlockSpec((tile, w), lambda i: (i, 0)) for w, _ in outs]
    out_shape += [jax.ShapeDtypeStruct((1, w), F32) for w in reds]
    out_specs += [pl.BlockSpec((1, w), lambda i: (0, 0)) for w in reds]
    res = pl.pallas_call(
        body, name=name, grid=(s_len // tile,), in_specs=in_specs, out_specs=out_specs, out_shape=out_shape,
        compiler_params=_cparams("arbitrary" if n_red else "parallel"),
    )(*[a for a, _, _, _ in norm], *vecs)
    return res


def _mm(a, b, *, ta=False, tb=False, out_dtype=F32, name, exchange=None, tiles=None):
    m, k = (a.shape[1], a.shape[0]) if ta else a.shape
    n = b.shape[0] if tb else b.shape[1]
    assert (b.shape[1] if tb else b.shape[0]) == k
    tm, tn, tk = tiles or (_tile_of(m, 1024), _tile_of(n, 1024 if n <= 1024 else 512), _tile_of(k, 1024))
    assert m % tm == 0 and n % tn == 0 and k % tk == 0
    ni, nj, nk = m // tm, n // tn, k // tk
    dims = (((0 if ta else 1,), (1 if tb else 0,)), ((), ()))
    n_ex = len(exchange.arrays) if exchange else 0

    def body(*refs):
        a_ref, b_ref = refs[:2]
        x_refs, o_ref, out_refs = refs[2:2 + n_ex], refs[2 + n_ex], refs[3 + n_ex:3 + 2 * n_ex]
        acc_ref, sems = refs[3 + 2 * n_ex], refs[4 + 2 * n_ex:]
        i, j, kk = pl.program_id(0), pl.program_id(1), pl.program_id(2)

        if exchange:
            @pl.when((i == 0) & (j == 0) & (kk == 0))
            def _():
                exchange.start(x_refs, out_refs, sems)

        @pl.when(kk == 0)
        def _():
            acc_ref[...] = jnp.zeros(acc_ref.shape, F32)

        acc_ref[...] += lax.dot_general(a_ref[...].astype(MXU_DTYPE), b_ref[...].astype(MXU_DTYPE), dims,
                                        preferred_element_type=F32)

        @pl.when(kk == nk - 1)
        def _():
            o_ref[...] = acc_ref[...].astype(o_ref.dtype)

        if exchange:
            @pl.when((i == ni - 1) & (j == nj - 1) & (kk == nk - 1))
            def _():
                exchange.wait(x_refs, out_refs, sems)

    a_spec = pl.BlockSpec((tk, tm), lambda i, j, kk: (kk, i)) if ta else pl.BlockSpec((tm, tk), lambda i, j, kk: (i, kk))
    b_spec = pl.BlockSpec((tn, tk), lambda i, j, kk: (j, kk)) if tb else pl.BlockSpec((tk, tn), lambda i, j, kk: (kk, j))
    o_spec = pl.BlockSpec((tm, tn), lambda i, j, kk: (i, j))
    o_shape = jax.ShapeDtypeStruct((m, n), out_dtype)
    if not exchange:
        return pl.pallas_call(
            body, name=name, grid=(ni, nj, nk), in_specs=[a_spec, b_spec], out_specs=o_spec, out_shape=o_shape,
            scratch_shapes=[pltpu.VMEM((tm, tn), F32)],
            compiler_params=_cparams("parallel", "parallel", "arbitrary"),
        )(a, b)
    res = pl.pallas_call(
        body, name=name, grid=(ni, nj, nk), in_specs=[a_spec, b_spec] + exchange.in_specs,
        out_specs=[o_spec] + exchange.out_specs, out_shape=[o_shape] + exchange.out_shape,
        scratch_shapes=[pltpu.VMEM((tm, tn), F32)] + exchange.scratch,
        compiler_params=_cparams("arbitrary", "arbitrary", "arbitrary"),
    )(a, b, *exchange.arrays)
    return res[0], res[1:]


_NT = (((1,), (1,)), ((), ()))
_TN = (((0,), (0,)), ((), ()))


def _dot(a, b):
    return jnp.dot(a, b, preferred_element_type=F32)


def _dot_nt(a, b):
    return lax.dot_general(a, b, _NT, preferred_element_type=F32)


def _dot_tn(a, b):
    return lax.dot_general(a, b, _TN, preferred_element_type=F32)


def _running_sum(x, tri):
    return _dot(x.astype(MXU_DTYPE), tri)


def _neg_softplus(z):
    u = jnp.exp2(jnp.abs(z) * (-1.0 / math.log(2.0)))
    return -jnp.maximum(z, 0.0) - jnp.log(1.0 + u)


GROUP = 2


def _walk_blocks(step, st, n, descending):
    groups = n // GROUP

    def many(t, s):
        first = (n - 1 - GROUP * t) if descending else GROUP * t
        return step([first - g if descending else first + g for g in range(GROUP)], s)

    def one(r, s):
        done = GROUP * groups + r
        return step([n - 1 - done if descending else done], s)

    st = lax.fori_loop(0, groups, many, st)
    if (BQ // BK) % GROUP == 0:
        return st
    return lax.fori_loop(0, n - GROUP * groups, one, st)


def _chains(js):
    return [(h, t) for t in range(len(js)) for h in range(2)]


def _rowsum(x):
    return jnp.sum(x, axis=1, keepdims=True)


def _attn_consts():
    row = lax.broadcasted_iota(jnp.int32, (BQ, BK), 0)
    col = lax.broadcasted_iota(jnp.int32, (BQ, BK), 1)
    trow = lax.broadcasted_iota(jnp.int32, (BK, BK), 0)
    tcol = lax.broadcasted_iota(jnp.int32, (BK, BK), 1)
    lane = lax.broadcasted_iota(jnp.int32, (BQ, LANES), 1)
    klane = lax.broadcasted_iota(jnp.int32, (BK, LANES), 1)
    return row, col, trow, tcol, lane, klane


DIAG = BQ // BK
assert BQ == DIAG * BK


def _diag_block(i):
    return i * DIAG


def _diag_blocks(jd, descending):
    places = list(reversed(range(DIAG))) if descending else list(range(DIAG))
    return [jd + g for g in places], places


def _key_slice(j):
    return pl.ds(pl.multiple_of(j * BK, BK), BK)


def _sb_forward(proj, s_len):
    nq = s_len // BQ
    assert s_len // BK <= HD
    qc, kc, vc = O_SBQ // LANES, O_SBK // LANES, O_SBV // LANES

    def body(q_ref, k_ref, v_ref, o_ref, tails_ref):
        i = pl.program_id(1)
        jd = _diag_block(i)
        row, col, trow, tcol, lane, klane = _attn_consts()
        strict = [col + g * BK < row for g in range(DIAG)]
        tri = (trow >= tcol).astype(MXU_DTYPE)
        q = q_ref[...] * 0.125
        qh = [jnp.where(lane < HD, q, 0.0).astype(MXU_DTYPE), jnp.where(lane >= HD, q, 0.0).astype(MXU_DTYPE)]

        km = [klane < HD, klane >= HD]

        def step(js, st, diag):
            carry, acc, tail = st
            chains = _chains(js)
            kj = [k_ref[_key_slice(j), :].astype(MXU_DTYPE) for j in js]
            vj = [v_ref[_key_slice(j), :].astype(MXU_DTYPE) for j in js]
            z = {(h, t): _dot_nt(qh[h], kj[t]) for h, t in chains}
            run = list(carry)
            suf, carry_in = {}, {}
            for h, t in chains:
                lom = _neg_softplus(z[h, t])
                if diag:
                    lom = jnp.where(strict[diag[t]], lom, 0.0)
                suf[h, t] = _running_sum(lom, tri)
                carry_in[h, t] = run[h]
                run[h] = run[h] + _rowsum(lom)
            for h, t in chains:
                a = jnp.exp(z[h, t] + suf[h, t] + carry_in[h, t])
                if diag:
                    a = jnp.where(strict[diag[t]], a, 0.0)
                acc = acc + _dot(a.astype(MXU_DTYPE), jnp.where(km[h], vj[t], 0))
                tail = jnp.where(lane == h * HD + js[t], carry_in[h, t], tail)
            return tuple(run), acc, tail

        zero = jnp.zeros((BQ, LANES), F32)
        diag_js, places = _diag_blocks(jd, True)
        st = step(diag_js, ((jnp.zeros((BQ, 1), F32),) * 2, zero, zero), places)
        st = _walk_blocks(lambda js, s: step(js, s, False), st, jd, True)
        o_ref[...] = st[1]
        tails_ref[...] = st[2]

    blk = pl.BlockSpec((BQ, LANES), lambda p, i: (i, p))
    out = jax.ShapeDtypeStruct((s_len, SB_W), F32)
    return pl.pallas_call(
        body, name="sb_fwd", grid=(HEADS // 2, nq),
        in_specs=[pl.BlockSpec((BQ, LANES), lambda p, i: (i, qc + p)),
                  pl.BlockSpec((s_len, LANES), lambda p, i: (0, kc + p)),
                  pl.BlockSpec((s_len, LANES), lambda p, i: (0, vc + p))],
        out_specs=[blk, blk], out_shape=[out, out],
        compiler_params=_cparams("parallel", "parallel"),
    )(proj, proj, proj)


def _sb_backward(proj, tails, do, s_len, exchange):
    nq = s_len // BQ
    qc, kc, vc = O_SBQ // LANES, O_SBK // LANES, O_SBV // LANES
    n_ex = len(exchange.arrays)

    def body(q_ref, k_ref, v_ref, tails_ref, do_ref, *rest):
        x_refs, (dq_ref, dk_ref, dv_ref) = rest[:n_ex], rest[n_ex:n_ex + 3]
        out_refs, sems = rest[n_ex + 3:2 * n_ex + 3], rest[2 * n_ex + 3:]
        p = pl.program_id(0)
        i = pl.program_id(1)
        jd = _diag_block(i)

        @pl.when((p == 0) & (i == 0))
        def _():
            exchange.start(x_refs, out_refs, sems)

        @pl.when(i == 0)
        def _():
            dk_ref[...] = jnp.zeros(dk_ref.shape, F32)
            dv_ref[...] = jnp.zeros(dv_ref.shape, F32)

        row, col, trow, tcol, lane, klane = _attn_consts()
        strict = [col + g * BK < row for g in range(DIAG)]
        tri = (trow >= tcol).astype(MXU_DTYPE)
        tri_p = (trow <= tcol).astype(MXU_DTYPE)
        q = q_ref[...] * 0.125
        tails_blk = tails_ref[...]
        do_blk = do_ref[...]
        hm = [lane < HD, lane >= HD]
        km = [klane < HD, klane >= HD]
        qh = [jnp.where(m, q, 0.0).astype(MXU_DTYPE) for m in hm]
        doh = [jnp.where(m, do_blk, 0.0).astype(MXU_DTYPE) for m in hm]

        def step(js, st, diag):
            before, dq = st
            chains = _chains(js)
            kj = [k_ref[_key_slice(j), :].astype(MXU_DTYPE) for j in js]
            vj = [v_ref[_key_slice(j), :].astype(MXU_DTYPE) for j in js]
            z = {(h, t): _dot_nt(qh[h], kj[t]) for h, t in chains}
            da = {(h, t): _dot_nt(doh[h], vj[t]) for h, t in chains}
            suf, sig = {}, {}
            for h, t in chains:
                lom = _neg_softplus(z[h, t])
                if diag:
                    lom = jnp.where(strict[diag[t]], lom, 0.0)
                suf[h, t] = _running_sum(lom, tri)
                sig[h, t] = jnp.exp(z[h, t] + lom)
            run = list(before)
            dl, pre, before_in = {}, {}, {}
            dk_add, dv_add = [None] * len(js), [None] * len(js)
            for h, t in chains:
                tail = _rowsum(jnp.where(lane == h * HD + js[t], tails_blk, 0.0))
                a = jnp.exp(z[h, t] + suf[h, t] + tail)
                if diag:
                    a = jnp.where(strict[diag[t]], a, 0.0)
                dl[h, t] = da[h, t] * a
                pre[h, t] = _dot(dl[h, t].astype(MXU_DTYPE), tri_p)
                dv_h = _dot_tn(a.astype(MXU_DTYPE), doh[h])
                dv_add[t] = dv_h if dv_add[t] is None else dv_add[t] + dv_h
                before_in[h, t] = run[h]
                run[h] = run[h] + _rowsum(dl[h, t])
            for h, t in chains:
                upto = before_in[h, t] + pre[h, t]
                dz = dl[h, t] - sig[h, t] * upto
                if diag:
                    dz = jnp.where(strict[diag[t]], dz, 0.0)
                dzb = dz.astype(MXU_DTYPE)
                dq = dq + _dot(dzb, jnp.where(km[h], kj[t], 0))
                dk_h = _dot_tn(dzb, qh[h])
                dk_add[t] = dk_h if dk_add[t] is None else dk_add[t] + dk_h
            for t, j in enumerate(js):
                dk_ref[_key_slice(j), :] += dk_add[t]
                dv_ref[_key_slice(j), :] += dv_add[t]
            return tuple(run), dq

        st = ((jnp.zeros((BQ, 1), F32),) * 2, jnp.zeros((BQ, LANES), F32))
        st = _walk_blocks(lambda js, s: step(js, s, False), st, jd, False)
        diag_js, places = _diag_blocks(jd, False)
        st = step(diag_js, st, places)
        dq_ref[...] = st[1] * 0.125

        @pl.when((p == HEADS // 2 - 1) & (i == nq - 1))
        def _():
            exchange.wait(x_refs, out_refs, sems)

    blk = pl.BlockSpec((BQ, LANES), lambda p, i: (i, p))
    full = pl.BlockSpec((s_len, LANES), lambda p, i: (0, p))
    out = jax.ShapeDtypeStruct((s_len, SB_W), F32)
    res = pl.pallas_call(
        body, name="sb_bwd", grid=(HEADS // 2, nq),
        in_specs=[pl.BlockSpec((BQ, LANES), lambda p, i: (i, qc + p)),
                  pl.BlockSpec((s_len, LANES), lambda p, i: (0, kc + p)),
                  pl.BlockSpec((s_len, LANES), lambda p, i: (0, vc + p)),
                  blk, blk] + exchange.in_specs,
        out_specs=[blk, full, full] + exchange.out_specs, out_shape=[out, out, out] + exchange.out_shape,
        scratch_shapes=exchange.scratch,
        compiler_params=_cparams("arbitrary", "arbitrary"),
    )(proj, proj, proj, tails, do, *exchange.arrays)
    return res[:3], res[3:]


def _pair_mask(rows, h):
    lane = lax.broadcasted_iota(jnp.int32, (rows, 2 * LANES), 1)
    rot = lane - LANES
    return (((lane < LANES) & (lane // HD == h))
            | ((lane >= LANES) & (rot < 2 * ROPE) & ((rot // (ROPE // 2)) % 2 == h)))


def _mla_forward(q_cat, k_cat, kv, s_len):
    nq = s_len // BQ
    scale = 1.0 / math.sqrt(QK_DIM)

    def body(q_ref, k_ref, v_ref, o_ref, lse_ref):
        i = pl.program_id(1)
        jd = _diag_block(i)
        row, col, trow, tcol, lane, klane = _attn_consts()
        causal = [col + g * BK <= row for g in range(DIAG)]
        q = q_ref[...]
        hm = [lane < HD, lane >= HD]
        km = [klane < HD, klane >= HD]
        qh = [jnp.where(_pair_mask(BQ, h), q, 0) for h in range(2)]

        def step(js, st, diag):
            m_run, l_run, acc = st
            chains = _chains(js)
            kj = [k_ref[_key_slice(j), :] for j in js]
            vj = [v_ref[_key_slice(j), :].astype(MXU_DTYPE) for j in js]
            s = {}
            for h, t in chains:
                s[h, t] = _dot_nt(qh[h], kj[t]) * scale
                if diag:
                    s[h, t] = jnp.where(causal[diag[t]], s[h, t], -jnp.inf)
            m_new, alpha, l_new = [], [], []
            for h in range(2):
                top = m_run[h]
                for t in range(len(js)):
                    top = jnp.maximum(top, jnp.max(s[h, t], axis=1, keepdims=True))
                m_new.append(top)
                alpha.append(jnp.exp(m_run[h] - top))
                l_new.append(alpha[h] * l_run[h])
            add = None
            for h, t in chains:
                pr = jnp.exp(s[h, t] - m_new[h])
                l_new[h] = l_new[h] + _rowsum(pr)
                part = _dot(pr.astype(MXU_DTYPE), jnp.where(km[h], vj[t], 0))
                add = part if add is None else add + part
            acc = jnp.where(hm[0], alpha[0], alpha[1]) * acc + add
            return tuple(m_new), tuple(l_new), acc

        st = ((jnp.full((BQ, 1), -1e30, F32),) * 2, (jnp.zeros((BQ, 1), F32),) * 2, jnp.zeros((BQ, LANES), F32))
        diag_js, places = _diag_blocks(jd, True)
        st = step(diag_js, st, places)
        m_run, l_run, acc = _walk_blocks(lambda js, s: step(js, s, False), st, jd, True)
        o_ref[...] = acc / jnp.where(hm[0], l_run[0], l_run[1])
        lse_ref[...] = jnp.where(hm[0], m_run[0] + jnp.log(l_run[0]), m_run[1] + jnp.log(l_run[1]))

    blk = pl.BlockSpec((BQ, LANES), lambda p, i: (i, p))
    out = jax.ShapeDtypeStruct((s_len, MLA_W), F32)
    return pl.pallas_call(
        body, name="mla_fwd", grid=(HEADS // 2, nq),
        in_specs=[pl.BlockSpec((BQ, 2 * LANES), lambda p, i: (i, p)),
                  pl.BlockSpec((s_len, 2 * LANES), lambda p, i: (0, p)),
                  pl.BlockSpec((s_len, LANES), lambda p, i: (0, MLA_W // LANES + p))],
        out_specs=[blk, blk], out_shape=[out, out],
        compiler_params=_cparams("parallel", "parallel"),
    )(q_cat, k_cat, kv)


def _mla_backward(q_cat, k_cat, kv, o, lse, do, s_len):
    nq = s_len // BQ
    scale = 1.0 / math.sqrt(QK_DIM)

    def body(q_ref, k_ref, v_ref, o_ref, lse_ref, do_ref, dq_ref, dk_ref, dv_ref):
        i = pl.program_id(1)

        @pl.when(i == 0)
        def _():
            dk_ref[...] = jnp.zeros(dk_ref.shape, F32)
            dv_ref[...] = jnp.zeros(dv_ref.shape, F32)

        jd = _diag_block(i)
        row, col, trow, tcol, lane, klane = _attn_consts()
        causal = [col + g * BK <= row for g in range(DIAG)]
        q = q_ref[...]
        o_blk = o_ref[...]
        do_blk = do_ref[...]
        lse_blk = lse_ref[...]
        hm = [lane < HD, lane >= HD]
        kpm = [_pair_mask(BK, h) for h in range(2)]
        qh = [jnp.where(_pair_mask(BQ, h), q, 0) for h in range(2)]
        doh_f = [jnp.where(m, do_blk, 0.0) for m in hm]
        doh = [d.astype(MXU_DTYPE) for d in doh_f]
        delta = [jnp.sum(d * o_blk, axis=1, keepdims=True) for d in doh_f]
        lse_h = [jnp.sum(jnp.where(lane == h * HD, lse_blk, 0.0), axis=1, keepdims=True) for h in range(2)]

        def step(js, st, diag):
            dq = st
            chains = _chains(js)
            kj = [k_ref[_key_slice(j), :] for j in js]
            vj = [v_ref[_key_slice(j), :].astype(MXU_DTYPE) for j in js]
            s = {(h, t): _dot_nt(qh[h], kj[t]) for h, t in chains}
            dp = {(h, t): _dot_nt(doh[h], vj[t]) for h, t in chains}
            adds = [[None] * len(js) for _ in range(2)]

            def accumulate(slot, t, part):
                adds[slot][t] = part if adds[slot][t] is None else adds[slot][t] + part

            for h, t in chains:
                pr = jnp.exp(s[h, t] * scale - lse_h[h])
                if diag:
                    pr = jnp.where(causal[diag[t]], pr, 0.0)
                dsb = (pr * (dp[h, t] - delta[h]) * scale).astype(MXU_DTYPE)
                dq = dq + _dot(dsb, jnp.where(kpm[h], kj[t], 0))
                accumulate(0, t, _dot_tn(dsb, qh[h]))
                accumulate(1, t, _dot_tn(pr.astype(MXU_DTYPE), doh[h]))
            for t, j in enumerate(js):
                dk_ref[_key_slice(j), :] += adds[0][t]
                dv_ref[_key_slice(j), :] += adds[1][t]
            return dq

        diag_js, places = _diag_blocks(jd, True)
        st = step(diag_js, jnp.zeros((BQ, 2 * LANES), F32), places)
        dq_ref[...] = _walk_blocks(lambda js, s: step(js, s, False), st, jd, True)

    blk = pl.BlockSpec((BQ, LANES), lambda p, i: (i, p))
    full = pl.BlockSpec((s_len, LANES), lambda p, i: (0, p))
    out = jax.ShapeDtypeStruct((s_len, MLA_W), F32)
    out_cat = jax.ShapeDtypeStruct((s_len, 2 * MLA_W), F32)
    return pl.pallas_call(
        body, name="mla_bwd", grid=(HEADS // 2, nq),
        in_specs=[pl.BlockSpec((BQ, 2 * LANES), lambda p, i: (i, p)),
                  pl.BlockSpec((s_len, 2 * LANES), lambda p, i: (0, p)),
                  pl.BlockSpec((s_len, LANES), lambda p, i: (0, MLA_W // LANES + p)),
                  blk, blk, blk],
        out_specs=[pl.BlockSpec((BQ, 2 * LANES), lambda p, i: (i, p)),
                   pl.BlockSpec((s_len, 2 * LANES), lambda p, i: (0, p)), full],
        out_shape=[out_cat, out_cat, out],
        compiler_params=_cparams("arbitrary", "arbitrary"),
    )(q_cat, k_cat, kv, o, lse, do)


def _mesh_pos():
    return lax.axis_index("x"), lax.axis_index("y"), lax.axis_index("c")


def _dev_index(px, py, pc):
    return 4 * px + 2 * py + pc


def _all_gather(block, name):
    return _all_gather_parts([block], name)[0]


def _all_gather_parts(blocks, name):
    n = len(blocks)

    def body(*refs):
        x_refs, out_refs = refs[:n], refs[n:2 * n]
        send_sems, recv_sems, local_sems = refs[2 * n:]
        x, y, c = _mesh_pos()
        me, sibling = (x, y, c), (x, y, 1 - c)
        chips = [(1 - x, y), (x, 1 - y), (1 - x, 1 - y)]

        def copy(a, k, blockpos, to, src=None):
            slot = out_refs[a].at[_dev_index(*blockpos)]
            return pltpu.make_async_remote_copy(
                src_ref=slot if src is None else src, dst_ref=slot,
                send_sem=send_sems.at[7 * a + k], recv_sem=recv_sems.at[7 * a + k],
                device_id=to, device_id_type=pl.DeviceIdType.MESH)

        mine = [pltpu.make_async_copy(x_refs[a], out_refs[a].at[_dev_index(*me)], local_sems.at[a]) for a in range(n)]
        for cp in mine:
            cp.start()
        first = []
        for a in range(n):
            first.append(copy(a, 0, me, sibling, src=x_refs[a]))
            first += [copy(a, 1 + j, me, (*chip, c), src=x_refs[a]) for j, chip in enumerate(chips)]
        for cp in first:
            cp.start()
        passed = []
        for j, chip in enumerate(chips):
            for a in range(n):
                copy(a, 1 + j, (*chip, c), me).wait_recv()
                passed.append(copy(a, 4 + j, (*chip, c), sibling))
                passed[-1].start()
        for a in range(n):
            copy(a, 0, sibling, me).wait_recv()
            for j, chip in enumerate(chips):
                copy(a, 4 + j, (*chip, 1 - c), me).wait_recv()
        for cp in first + passed:
            cp.wait_send()
        for cp in mine:
            cp.wait()

    return pl.pallas_call(
        body, name=name,
        out_shape=[jax.ShapeDtypeStruct((N_DEV,) + b.shape, b.dtype) for b in blocks],
        in_specs=[pl.BlockSpec(memory_space=pl.ANY)] * n, out_specs=[pl.BlockSpec(memory_space=pl.ANY)] * n,
        scratch_shapes=[pltpu.SemaphoreType.DMA((7 * n,)), pltpu.SemaphoreType.DMA((7 * n,)),
                        pltpu.SemaphoreType.DMA((n,))],
    )(*blocks)


class _Exchange:
    def __init__(self, arrays):
        self.arrays = list(arrays)
        n = len(self.arrays)
        self.in_specs = [pl.BlockSpec(memory_space=pl.ANY)] * n
        self.out_specs = [pl.BlockSpec(memory_space=pl.ANY)] * n
        self.out_shape = [jax.ShapeDtypeStruct(a.shape, a.dtype) for a in self.arrays]
        self.scratch = [pltpu.SemaphoreType.DMA((7 * n,)), pltpu.SemaphoreType.DMA((7 * n,)),
                        pltpu.SemaphoreType.DMA((n,))]

    def _copies(self, x_refs, out_refs, sems, with_arrivals):
        send_sems, recv_sems, local_sems = sems
        x, y, c = _mesh_pos()
        me = _dev_index(x, y, c)
        flips = [(fx, fy, fc) for fx in (0, 1) for fy in (0, 1) for fc in (0, 1)][1:]
        peers = [(1 - x if fx else x, 1 - y if fy else y, 1 - c if fc else c) for fx, fy, fc in flips]
        mine, sends, arrivals = [], [], []
        for a in range(len(self.arrays)):
            mine.append(pltpu.make_async_copy(x_refs[a].at[me], out_refs[a].at[me], local_sems.at[a]))
            for k, peer in enumerate(peers):
                sends.append(pltpu.make_async_remote_copy(
                    src_ref=x_refs[a].at[_dev_index(*peer)], dst_ref=out_refs[a].at[me],
                    send_sem=send_sems.at[7 * a + k], recv_sem=recv_sems.at[7 * a + k],
                    device_id=peer, device_id_type=pl.DeviceIdType.MESH))
                if not with_arrivals:
                    continue
                arrivals.append(pltpu.make_async_remote_copy(
                    src_ref=x_refs[a].at[me], dst_ref=out_refs[a].at[_dev_index(*peer)],
                    send_sem=send_sems.at[7 * a + k], recv_sem=recv_sems.at[7 * a + k],
                    device_id=peer, device_id_type=pl.DeviceIdType.MESH))
        return mine, sends, arrivals

    def start(self, x_refs, out_refs, sems):
        mine, sends, _ = self._copies(x_refs, out_refs, sems, False)
        for cp in mine + sends:
            cp.start()

    def wait(self, x_refs, out_refs, sems):
        mine, sends, arrivals = self._copies(x_refs, out_refs, sems, True)
        for cp in arrivals:
            cp.wait_recv()
        for cp in sends:
            cp.wait_send()
        for cp in mine:
            cp.wait()


def _sum_blocks(parts, name):
    n, r, c = parts.shape
    row_tiles = [t for t in range(16, min(r, 2048) + 1, 16) if r % t == 0]
    if row_tiles:
        tr, tc = max(row_tiles), c
    else:
        tr, tc = r, 2 * LANES
    assert c % tc == 0

    def body(p_ref, o_ref):
        acc = p_ref[0].astype(F32)
        for s in range(1, n):
            acc = acc + p_ref[s].astype(F32)
        o_ref[...] = acc

    return pl.pallas_call(
        body, name=name, grid=(r // tr, c // tc),
        in_specs=[pl.BlockSpec((n, tr, tc), lambda i, j: (0, i, j))],
        out_specs=pl.BlockSpec((tr, tc), lambda i, j: (i, j)),
        out_shape=jax.ShapeDtypeStruct((r, c), F32),
        compiler_params=_cparams("parallel", "parallel"),
    )(parts)


def _sigmoid(x):
    return 1.0 / (1.0 + jnp.exp(-x))


def _silu(x):
    return x * _sigmoid(x)


def _silu_grad(x):
    s = _sigmoid(x)
    return s * (1.0 + x * (1.0 - s))


def _colsum(x):
    return jnp.sum(x, axis=0, keepdims=True)


def _rms(x):
    return lax.rsqrt(jnp.mean(x * x, axis=-1, keepdims=True) + EPS)


def _rms_bwd(xn, r, dxn):
    return r * (dxn - xn * jnp.mean(dxn * xn, axis=-1, keepdims=True))


def _adamw(w, g, m, v):
    m = ADAM_B1 * m + (1.0 - ADAM_B1) * g
    v = ADAM_B2 * v + (1.0 - ADAM_B2) * jnp.square(g)
    m_hat = m / (1.0 - ADAM_B1 ** ADAM_STEP)
    v_hat = v / (1.0 - ADAM_B2 ** ADAM_STEP)
    delta = -ADAM_LR * (m_hat / (jnp.sqrt(v_hat) + ADAM_EPS) + ADAM_WD * w)
    return delta, m, v


def _adamw_call(w, g, m, v, name):
    r, c = w.shape
    if r % 256 == 0:
        tr, tc = 256, c
    elif r * c <= 256 * 1024 or c % (2 * LANES):
        tr, tc = r, c
    else:
        tr, tc = r, 2 * LANES

    def body(w_ref, g_ref, m_ref, v_ref, d_out, m_out, v_out):
        d_out[...], m_out[...], v_out[...] = _adamw(w_ref[...], g_ref[...], m_ref[...], v_ref[...])

    spec = pl.BlockSpec((tr, tc), lambda i, j: (i, j))
    return pl.pallas_call(
        body, name=name, grid=(r // tr, c // tc), in_specs=[spec] * 4, out_specs=[spec] * 3,
        out_shape=[jax.ShapeDtypeStruct((r, c), F32)] * 3, compiler_params=_cparams("parallel", "parallel"),
    )(w, g, m, v)


def _uq_to_kernel_layout(w):
    lead = w.shape[:-1]
    t = w.reshape(lead + (HEADS, QK_DIM))
    return jnp.concatenate([t[..., :NOPE].reshape(lead + (HEADS * NOPE,)),
                            t[..., NOPE:NOPE + ROPE // 2].reshape(lead + (LANES,)),
                            t[..., NOPE + ROPE // 2:].reshape(lead + (LANES,))], axis=-1)


def _uq_from_kernel_layout(w):
    lead = w.shape[:-1]
    nope = w[..., :HEADS * NOPE].reshape(lead + (HEADS, NOPE))
    r1 = w[..., HEADS * NOPE:HEADS * NOPE + LANES].reshape(lead + (HEADS, ROPE // 2))
    r2 = w[..., HEADS * NOPE + LANES:].reshape(lead + (HEADS, ROPE // 2))
    return jnp.concatenate([nope, r1, r2], axis=-1).reshape(lead + (HEADS * QK_DIM,))


def _ukv_to_kernel_layout(w):
    lead = w.shape[:-1]
    t = w.reshape(lead + (HEADS, NOPE + HD))
    return jnp.concatenate([t[..., :NOPE].reshape(lead + (HEADS * NOPE,)),
                            t[..., NOPE:].reshape(lead + (HEADS * HD,))], axis=-1)


def _ukv_from_kernel_layout(w):
    lead = w.shape[:-1]
    kn = w[..., :HEADS * NOPE].reshape(lead + (HEADS, NOPE))
    vv = w[..., HEADS * NOPE:].reshape(lead + (HEADS, HD))
    return jnp.concatenate([kn, vv], axis=-1).reshape(lead + (HEADS * (NOPE + HD),))


def _w_in_t_to_kernel_layout(wt):
    sb = wt[0:2048]
    c_q = wt[2048:2432]
    c_kv = wt[2432:2688]
    k_rot = wt[2688:2720]
    mla_z = wt[2720:3232]
    gates = wt[3232:5280]
    zeros = jnp.zeros((LANES, wt.shape[1]), wt.dtype)
    k1 = jnp.tile(k_rot[:ROPE // 2], (HEADS, 1))
    k2 = jnp.tile(k_rot[ROPE // 2:], (HEADS, 1))
    return jnp.concatenate([gates, sb, mla_z, c_q, zeros, c_kv, k1, k2], axis=0)


def _w_in_t_from_kernel_layout(gt, g_rot):
    return jnp.concatenate([gt[O_SBQ:O_SBQ + 2048], gt[O_CQ:O_CQ + Q_RANK], gt[O_CKV:O_CKV + KV_RANK], g_rot,
                            gt[O_MLAZ:O_MLAZ + MLA_W], gt[O_GA:O_GA + 2 * D]], axis=0)


def kernel(x, c, positions, w_ada, b_ada, norm_gain, w_in, q_norm_gain, w_uq, kv_norm_gain, w_ukv, w_branch_a, w_branch_b, w_out, final_norm_gain, loss_target, m_w_ada, m_b_ada, m_norm_gain, m_w_in, m_q_norm_gain, m_w_uq, m_kv_norm_gain, m_w_ukv, m_w_branch_a, m_w_branch_b, m_w_out, m_final_norm_gain, v_w_ada, v_b_ada, v_norm_gain, v_w_in, v_q_norm_gain, v_w_uq, v_kv_norm_gain, v_w_ukv, v_w_branch_a, v_w_branch_b, v_w_out, v_final_norm_gain):
    s_len = x.shape[1]
    me = _dev_index(*_mesh_pos())
    x2d = x[0]
    tgt = loss_target[0]

    w_in_t = w_in[0].T.astype(BF16)
    big = [w_uq[0], w_ukv[0], w_branch_a[0], w_branch_b[0], w_out[0]]
    big_sizes = [int(w.size) for w in big]
    packed = jnp.concatenate([w.astype(BF16).reshape(-1, LANES) for w in big], axis=0)
    g_in_t, c_all = _all_gather_parts([w_in_t, c.reshape(8, LANES)], "gather_w_in")
    c_all = c_all.reshape(N_DEV, D)
    w_in_kt = _w_in_t_to_kernel_layout(g_in_t.reshape(N_DEV * w_in_t.shape[0], D))

    mod_cols = _mm(c_all, w_ada[0], name="ada_mod")
    mod_all = _all_gather(mod_cols, "gather_mod")
    mod = lax.dynamic_index_in_dim(mod_all, me, axis=1, keepdims=False).reshape(1, 3 * D)
    mod_shift, mod_scale, mod_gate = mod[:, :D], mod[:, D:2 * D], mod[:, 2 * D:]
    b_shift, b_scale, b_gate = b_ada[:, :D], b_ada[:, D:2 * D], b_ada[:, 2 * D:]
    g1 = norm_gain
    gq, gkv = q_norm_gain, kv_norm_gain
    gf = final_norm_gain.reshape(1, D)

    def f_h(x_, g1_, ms, bs, msc, bsc):
        xn = x_ * _rms(x_)
        return (xn * g1_ * (1.0 + (msc + bsc)) + (ms + bs),), ()

    (h,) = _rowwise(f_h, [x2d], [g1, mod_shift, b_shift, mod_scale, b_scale], [(D, BF16)], name="ada_norm")
    proj, (gathered,) = _mm(h, w_in_kt, tb=True, name="proj_in", tiles=(min(s_len, 1024), IN_PAD // 2, D),
                            exchange=_Exchange([jnp.broadcast_to(packed[None], (N_DEV,) + packed.shape)]))
    offs = [0]
    for n in big_sizes:
        offs.append(offs[-1] + n // LANES)

    def unpack(t, shape):
        return gathered[:, offs[t]:offs[t + 1], :].reshape((N_DEV,) + shape)

    def cols(t, shape):
        return unpack(t, shape).transpose(1, 0, 2).reshape(shape[0], N_DEV * shape[1])

    w_uq_k = _uq_to_kernel_layout(cols(0, big[0].shape))
    w_ukv_k = _ukv_to_kernel_layout(cols(1, big[1].shape))
    w_a_f = cols(2, big[2].shape)
    w_b_f = cols(3, big[3].shape)
    w_out_f = unpack(4, big[4].shape).reshape(D, D)

    o_a, sb_tails = _sb_forward(proj, s_len)

    def f_lat(cq, ckv, gq_, gkv_):
        return (cq * _rms(cq) * gq_, ckv * _rms(ckv) * gkv_), ()

    cq_n, ckv_n = _rowwise(f_lat, [(proj, O_CQ // Q_RANK, Q_RANK), (proj, O_CKV // KV_RANK, KV_RANK)], [gq, gkv],
                           [(Q_RANK, BF16), (KV_RANK, BF16)], name="latent_norm")
    q_mla = _mm(cq_n, w_uq_k, name="q_up")
    kv = _mm(ckv_n, w_ukv_k, name="kv_up")

    inv_freq = ROPE_BASE ** (-jnp.arange(0, ROPE, 2, dtype=F32) / ROPE)
    inv_freq_t = jnp.tile(inv_freq, HEADS).reshape(1, LANES)
    pos_col = positions.reshape(s_len, 1).astype(F32)

    pairs = HEADS // 2

    def f_rope(pos, qn, q1, q2, kn, k1, k2, freq):
        ang = pos * freq
        cs, sn = jnp.cos(ang), jnp.sin(ang)
        q1r, q2r = q1 * cs - q2 * sn, q1 * sn + q2 * cs
        k1r, k2r = k1 * cs - k2 * sn, k1 * sn + k2 * cs
        lane = lax.broadcasted_iota(jnp.int32, q1.shape, 1)
        first, second = lane < ROPE, (lane >= ROPE) & (lane < 2 * ROPE)
        k_rot = jnp.where(first, k1r, jnp.where(second, k2r, 0.0))
        q_parts, k_parts = [], []
        for p in range(pairs):
            q_rot = jnp.where(first, pltpu.roll(q1r, (LANES - ROPE * p) % LANES, 1),
                              jnp.where(second, pltpu.roll(q2r, (LANES + ROPE - ROPE * p) % LANES, 1), 0.0))
            q_parts += [qn[:, LANES * p:LANES * (p + 1)], q_rot]
            k_parts += [kn[:, LANES * p:LANES * (p + 1)], k_rot]
        return (jnp.concatenate(q_parts, axis=1), jnp.concatenate(k_parts, axis=1), cs, sn), ()

    q_cat, k_cat, cos_t, sin_t = _rowwise(
        f_rope, [pos_col, (q_mla, 0, MLA_W), (q_mla, 4, LANES), (q_mla, 5, LANES), (kv, 0, MLA_W),
                 (proj, O_KROT // LANES, LANES), (proj, O_KROT // LANES + 1, LANES)], [inv_freq_t],
        [(2 * MLA_W, BF16), (2 * MLA_W, BF16), (LANES, F32), (LANES, F32)], name="rope")

    o_b, lse = _mla_forward(q_cat, k_cat, kv, s_len)

    def f_gate(oa, za, ob, zb):
        return (oa * _silu(za), ob * _silu(zb)), ()

    ya_in, yb_in = _rowwise(f_gate, [o_a, (proj, O_SBZ // SB_W, SB_W), o_b, (proj, O_MLAZ // MLA_W, MLA_W)], [],
                            [(SB_W, BF16), (MLA_W, BF16)], name="branch_gate")
    y_a = _mm(ya_in, w_a_f, name="branch_a")
    y_b = _mm(yb_in, w_b_f, name="branch_b")

    def f_merge(ga, gb, ya, yb):
        return (_sigmoid(ga) * ya + _sigmoid(gb) * yb,), ()

    (merged,) = _rowwise(f_merge, [(proj, O_GA // D, D), (proj, O_GB // D, D), y_a, y_b], [], [(D, BF16)], name="merge")
    out = _mm(merged, w_out_f, name="out_proj")

    def f_loss(x_, out_, t_, mg, bg, gf_):
        gate = mg + bg
        x2 = x_ + gate * out_
        r2 = _rms(x2)
        xn2 = x2 * r2
        err = xn2 * gf_ - t_
        loss = jnp.full((1, LANES), 0.5 / D, F32) * jnp.sum(err * err)
        dy = err * (1.0 / D)
        dx2 = _rms_bwd(xn2, r2, dy * gf_)
        return (dx2, dx2 * gate), (loss, _colsum(dy * xn2), _colsum(dx2 * out_))

    dx2, d_out, loss_part, d_gf, d_gate = _rowwise(
        f_loss, [x2d, out, tgt], [mod_gate, b_gate, gf], [(D, F32), (D, BF16)], [LANES, D, D], name="loss_head")

    d_merged = _mm(d_out, w_out_f, tb=True, name="d_merged")
    dw_out = _mm(merged, d_out, ta=True, name="dw_out")

    def f_dmerge(dm, ga, gb, ya, yb):
        sa, sb = _sigmoid(ga), _sigmoid(gb)
        return (dm * sa, dm * sb, dm * ya * sa * (1.0 - sa), dm * yb * sb * (1.0 - sb)), ()

    d_ya, d_yb, d_ga, d_gb = _rowwise(f_dmerge, [d_merged, (proj, O_GA // D, D), (proj, O_GB // D, D), y_a, y_b], [],
                                      [(D, BF16)] * 4, name="d_merge")
    dw_a = _mm(ya_in, d_ya, ta=True, name="dw_branch_a")
    dw_b = _mm(yb_in, d_yb, ta=True, name="dw_branch_b")
    d_ya_in = _mm(d_ya, w_a_f, tb=True, name="d_branch_a")
    d_yb_in = _mm(d_yb, w_b_f, tb=True, name="d_branch_b")

    def f_dgate(da, oa, za, db, ob, zb):
        return (da * _silu(za), da * oa * _silu_grad(za), db * _silu(zb), db * ob * _silu_grad(zb)), ()

    d_oa, d_sbz, d_ob, d_mlaz = _rowwise(
        f_dgate, [d_ya_in, o_a, (proj, O_SBZ // SB_W, SB_W), d_yb_in, o_b, (proj, O_MLAZ // MLA_W, MLA_W)], [],
        [(SB_W, F32), (SB_W, BF16), (MLA_W, F32), (MLA_W, BF16)], name="d_branch_gate")

    dq_cat, dk_cat, dv_b = _mla_backward(q_cat, k_cat, kv, o_b, lse, d_ob, s_len)

    def f_drope(dq, dk, dv_, cs, sn):
        lane = lax.broadcasted_iota(jnp.int32, cs.shape, 1)
        first, second = lane < ROPE, (lane >= ROPE) & (lane < 2 * ROPE)
        dq1 = dq2 = dk1 = dk2 = None
        for p in range(pairs):
            q_rot = dq[:, LANES * (2 * p + 1):LANES * (2 * p + 2)]
            k_rot = dk[:, LANES * (2 * p + 1):LANES * (2 * p + 2)]
            parts = (pltpu.roll(jnp.where(first, q_rot, 0.0), (ROPE * p) % LANES, 1),
                     pltpu.roll(jnp.where(second, q_rot, 0.0), (LANES - ROPE + ROPE * p) % LANES, 1),
                     jnp.where(first, k_rot, 0.0), jnp.where(second, k_rot, 0.0))
            if p == 0:
                dq1, dq2, dk1, dk2 = parts
            else:
                dq1, dq2, dk1, dk2 = dq1 + parts[0], dq2 + parts[1], dk1 + parts[2], dk2 + parts[3]
        dqn_ = [dq[:, 2 * LANES * p:2 * LANES * p + LANES] for p in range(pairs)]
        dkn_ = [dk[:, 2 * LANES * p:2 * LANES * p + LANES] for p in range(pairs)]
        return (jnp.concatenate(dqn_ + [dq1 * cs + dq2 * sn, dq2 * cs - dq1 * sn], axis=1),
                jnp.concatenate(dkn_ + [dv_], axis=1),
                jnp.concatenate([dk1 * cs + dk2 * sn, dk2 * cs - dk1 * sn], axis=1)), ()

    dq_k, dkv_k, d_krot = _rowwise(f_drope, [dq_cat, dk_cat, dv_b, cos_t, sin_t], [],
                                   [(HEADS * QK_DIM, BF16), (2 * MLA_W, BF16), (2 * LANES, BF16)], name="d_rope")
    dw_uq_k = _mm(cq_n, dq_k, ta=True, name="dw_uq")
    dw_ukv_k = _mm(ckv_n, dkv_k, ta=True, name="dw_ukv")
    d_cqn = _mm(dq_k, w_uq_k, tb=True, name="d_cq_norm")
    d_ckvn = _mm(dkv_k, w_ukv_k, tb=True, name="d_ckv_norm")

    def f_dlat(cq, dcqn, ckv, dckvn, gq_, gkv_):
        rq, rkv = _rms(cq), _rms(ckv)
        cqn, ckvn = cq * rq, ckv * rkv
        return ((_rms_bwd(cqn, rq, dcqn * gq_), _rms_bwd(ckvn, rkv, dckvn * gkv_)),
                (_colsum(dcqn * cqn), _colsum(dckvn * ckvn)))

    d_cq, d_ckv, d_gq, d_gkv = _rowwise(
        f_dlat, [(proj, O_CQ // Q_RANK, Q_RANK), d_cqn, (proj, O_CKV // KV_RANK, KV_RANK), d_ckvn], [gq, gkv],
        [(Q_RANK, BF16), (KV_RANK, BF16)], [Q_RANK, KV_RANK], name="d_latent_norm")

    def col_blocks(g):
        kdim, n8 = g.shape
        return g.astype(BF16).reshape(kdim, N_DEV, n8 // N_DEV).transpose(1, 0, 2).reshape(N_DEV, -1, LANES)

    g_blocks = jnp.concatenate([col_blocks(_uq_from_kernel_layout(dw_uq_k)), col_blocks(_ukv_from_kernel_layout(dw_ukv_k)),
                                col_blocks(dw_a), col_blocks(dw_b), dw_out.astype(BF16).reshape(N_DEV, -1, LANES)], axis=1)
    (d_sbq, d_sbk, d_sbv), (g_recv,) = _sb_backward(proj, sb_tails, d_oa, s_len, _Exchange([g_blocks]))

    d_proj = jnp.concatenate([d_ga, d_gb, d_sbq.astype(BF16), d_sbk.astype(BF16), d_sbv.astype(BF16), d_sbz, d_mlaz,
                              d_cq, jnp.zeros((s_len, LANES), BF16), d_ckv, d_krot], axis=1)
    dw_in_kt = _mm(d_proj, h, ta=True, name="dw_in", tiles=(512, D, s_len))

    def krot_body(t_ref, o_ref):
        half = ROPE // 2
        for part in range(2):
            acc = t_ref[part * LANES:part * LANES + half, :]
            for hh in range(1, HEADS):
                acc = acc + t_ref[part * LANES + hh * half:part * LANES + (hh + 1) * half, :]
            o_ref[part * half:(part + 1) * half, :] = acc

    dw_krot = pl.pallas_call(krot_body, name="dw_krot_sum", out_shape=jax.ShapeDtypeStruct((ROPE, D), F32))(
        dw_in_kt[O_KROT:O_KROT + 2 * LANES])

    dw_in_t = _w_in_t_from_kernel_layout(dw_in_kt, dw_krot)
    g_in_blocks = dw_in_t.astype(BF16).reshape(N_DEV, -1, D)
    dh, (g_in_recv,) = _mm(d_proj, w_in_kt, name="d_h", exchange=_Exchange([g_in_blocks]))

    def f_dx(x_, dh_, dx2_, g1_, msc, bsc):
        r = _rms(x_)
        xn = x_ * r
        dn1 = dh_ * (1.0 + (msc + bsc))
        return ((dx2_ + _rms_bwd(xn, r, dn1 * g1_),),
                (_colsum(dh_), _colsum(dh_ * (xn * g1_)), _colsum(dn1 * xn)))

    grad_x2d, d_shift, d_scale, d_g1 = _rowwise(f_dx, [x2d, dh, dx2], [g1, mod_scale, b_scale], [(D, F32)],
                                                [D, D, D], name="d_ada_norm")

    g_in_sum_t = _sum_blocks(g_in_recv, "sum_grads_w_in")
    g_sum = _sum_blocks(g_recv, "sum_grads")
    g_big = [g_sum[offs[t]:offs[t + 1]].reshape(big[t].shape) for t in range(5)]

    small = jnp.concatenate([d_shift, d_scale, d_gate, d_g1, d_gq, d_gkv, d_gf, loss_part], axis=1)
    n_small = small.shape[1]
    pad = (-n_small) % (8 * LANES)
    small = jnp.pad(small, ((0, 0), (0, pad))).reshape(-1, LANES)
    small_all = _all_gather(small, "gather_small")
    small_sum = _sum_blocks(small_all, "sum_small").reshape(1, -1)
    g_b_ada = small_sum[:, :3 * D]
    g_g1 = small_sum[:, 3 * D:4 * D]
    g_gq = small_sum[:, 4 * D:4 * D + Q_RANK]
    g_gkv = small_sum[:, 4 * D + Q_RANK:4 * D + Q_RANK + KV_RANK]
    g_gf = small_sum[:, 4 * D + Q_RANK + KV_RANK:4 * D + Q_RANK + KV_RANK + D]

    dmod_all = small_all.reshape(N_DEV, -1)[:, :3 * D]
    dmod_cols = lax.dynamic_slice_in_dim(dmod_all, me * (3 * D // N_DEV), 3 * D // N_DEV, axis=1)
    g_w_ada = _mm(c_all, dmod_cols, ta=True, name="dw_ada")

    loss = small_sum[0, n_small - LANES]

    names = ["w_ada", "b_ada", "norm_gain", "w_in", "q_norm_gain", "w_uq", "kv_norm_gain", "w_ukv",
             "w_branch_a", "w_branch_b", "w_out", "final_norm_gain"]
    weights = dict(w_ada=w_ada, b_ada=b_ada, norm_gain=norm_gain, w_in=w_in, q_norm_gain=q_norm_gain, w_uq=w_uq,
                   kv_norm_gain=kv_norm_gain, w_ukv=w_ukv, w_branch_a=w_branch_a, w_branch_b=w_branch_b, w_out=w_out,
                   final_norm_gain=final_norm_gain)
    moms = dict(w_ada=m_w_ada, b_ada=m_b_ada, norm_gain=m_norm_gain, w_in=m_w_in, q_norm_gain=m_q_norm_gain,
                w_uq=m_w_uq, kv_norm_gain=m_kv_norm_gain, w_ukv=m_w_ukv, w_branch_a=m_w_branch_a,
                w_branch_b=m_w_branch_b, w_out=m_w_out, final_norm_gain=m_final_norm_gain)
    vels = dict(w_ada=v_w_ada, b_ada=v_b_ada, norm_gain=v_norm_gain, w_in=v_w_in, q_norm_gain=v_q_norm_gain,
                w_uq=v_w_uq, kv_norm_gain=v_kv_norm_gain, w_ukv=v_w_ukv, w_branch_a=v_w_branch_a,
                w_branch_b=v_w_branch_b, w_out=v_w_out, final_norm_gain=v_final_norm_gain)
    grads2d = dict(w_ada=g_w_ada, b_ada=g_b_ada, norm_gain=g_g1, w_in=g_in_sum_t, q_norm_gain=g_gq, w_uq=g_big[0],
                   kv_norm_gain=g_gkv, w_ukv=g_big[1], w_branch_a=g_big[2], w_branch_b=g_big[3], w_out=g_big[4],
                   final_norm_gain=g_gf)

    grads, deltas, new_m, new_v = [], [], [], []
    for n in names:
        w = weights[n]
        if n == "w_in":
            to2d = lambda t: t[0].T
            back = lambda t: t.T[None]
        else:
            shape2d = grads2d[n].shape
            to2d = lambda t, s=shape2d: t.reshape(s)
            back = lambda t, s=w.shape: t.reshape(s)
        d_, m_, v_ = _adamw_call(to2d(w), grads2d[n], to2d(moms[n]), to2d(vels[n]), "adamw_" + n)
        grads.append(back(grads2d[n]))
        deltas.append(back(d_))
        new_m.append(back(m_))
        new_v.append(back(v_))

    return (loss, grad_x2d.reshape(x.shape), *grads, *deltas, *new_m, *new_v)
```

```python
import functools
import math

import jax
import jax.numpy as jnp
from jax import lax
from jax.experimental import pallas as pl
from jax.experimental.pallas import tpu as pltpu

F32 = jnp.float32
BF16 = jnp.bfloat16
MXU_DTYPE = jnp.bfloat16

N_DEV = 8
D = 1024
HEADS = 8
HD = 64
SB_W = 512
MLA_W = 512
Q_RANK = 384
KV_RANK = 256
ROPE = 32
NOPE = 64
QK_DIM = NOPE + ROPE
EPS = 1e-6
ROPE_BASE = 10000.0

ADAM_LR = 0.001
ADAM_B1 = 0.9
ADAM_B2 = 0.999
ADAM_EPS = 1e-08
ADAM_WD = 0.01
ADAM_STEP = 10

LANES = 128
VMEM_LIMIT = 48 * 1024 * 1024

O_GA, O_GB = 0, 1024
O_SBQ, O_SBK, O_SBV, O_SBZ = 2048, 2560, 3072, 3584
O_MLAZ = 4096
O_CQ = 4608
O_CKV = 5120
O_KROT = 5376
IN_PAD = 5632

BQ = 256
BK = 256


def _cparams(*sem):
    return pltpu.CompilerParams(dimension_semantics=sem, vmem_limit_bytes=VMEM_LIMIT)


def _tile_of(n, cap=512):
    if n <= cap:
        return n
    for t in (1024, 768, 512, 384, 256, 128):
        if t <= cap and n % t == 0:
            return t
    raise ValueError(n)


def _rowwise(fn, rows, vecs, outs, reds=(), *, name, tile=512):
    norm = []
    for r in rows:
        if isinstance(r, tuple):
            arr, cb, w = r[:3]
            ro = r[3] if len(r) > 3 else 0
        else:
            arr, cb, w, ro = r, 0, r.shape[1], 0
        norm.append((arr, cb, w, ro))
    s_len = norm[0][0].shape[0]
    tile = min(tile, s_len)
    assert s_len % tile == 0
    n_row, n_vec, n_out, n_red = len(norm), len(vecs), len(outs), len(reds)

    def body(*refs):
        step = pl.program_id(0)
        row_refs = refs[:n_row]
        vec_refs = refs[n_row:n_row + n_vec]
        out_refs = refs[n_row + n_vec:n_row + n_vec + n_out]
        red_refs = refs[n_row + n_vec + n_out:]
        row_res, red_res = fn(*[r[...] for r in row_refs], *[v[...] for v in vec_refs])
        for o, val in zip(out_refs, row_res):
            o[...] = val.astype(o.dtype)
        if n_red:
            @pl.when(step == 0)
            def _():
                for r in red_refs:
                    r[...] = jnp.zeros(r.shape, r.dtype)
            for r, val in zip(red_refs, red_res):
                r[...] += val

    in_specs = []
    for arr, cb, w, ro in norm:
        in_specs.append(pl.BlockSpec((tile, w), functools.partial(lambda i, cb, rb: (i + rb, cb), cb=cb, rb=ro // tile)))
        assert ro % tile == 0
    for v in vecs:
        in_specs.append(pl.BlockSpec(v.shape, lambda i: (0, 0)))
    out_shape = [jax.ShapeDtypeStruct((s_len, w), dt) for w, dt in outs]
    out_specs = [pl.BlockSpec((tile, w), lambda i: (i, 0)) for w, _ in outs]
    out_shape += [jax.ShapeDtypeStruct((1, w), F32) for w in reds]
    out_specs += [pl.BlockSpec((1, w), lambda i: (0, 0)) for w in reds]
    res = pl.pallas_call(
        body, name=name, grid=(s_len // tile,), in_specs=in_specs, out_specs=out_specs, out_shape=out_shape,
        compiler_params=_cparams("arbitrary" if n_red else "parallel"),
    )(*[a for a, _, _, _ in norm], *vecs)
    return res


def _mm(a, b, *, ta=False, tb=False, out_dtype=F32, name, exchange=None, tiles=None):
    m, k = (a.shape[1], a.shape[0]) if ta else a.shape
    n = b.shape[0] if tb else b.shape[1]
    assert (b.shape[1] if tb else b.shape[0]) == k
    tm, tn, tk = tiles or (_tile_of(m, 1024), _tile_of(n, 1024 if n <= 1024 else 512), _tile_of(k, 1024))
    assert m % tm == 0 and n % tn == 0 and k % tk == 0
    ni, nj, nk = m // tm, n // tn, k // tk
    dims = (((0 if ta else 1,), (1 if tb else 0,)), ((), ()))
    n_ex = len(exchange.arrays) if exchange else 0

    def body(*refs):
        a_ref, b_ref = refs[:2]
        x_refs, o_ref, out_refs = refs[2:2 + n_ex], refs[2 + n_ex], refs[3 + n_ex:3 + 2 * n_ex]
        acc_ref, sems = refs[3 + 2 * n_ex], refs[4 + 2 * n_ex:]
        i, j, kk = pl.program_id(0), pl.program_id(1), pl.program_id(2)

        if exchange:
            @pl.when((i == 0) & (j == 0) & (kk == 0))
            def _():
                exchange.start(x_refs, out_refs, sems)

        @pl.when(kk == 0)
        def _():
            acc_ref[...] = jnp.zeros(acc_ref.shape, F32)

        acc_ref[...] += lax.dot_general(a_ref[...].astype(MXU_DTYPE), b_ref[...].astype(MXU_DTYPE), dims,
                                        preferred_element_type=F32)

        @pl.when(kk == nk - 1)
        def _():
            o_ref[...] = acc_ref[...].astype(o_ref.dtype)

        if exchange:
            @pl.when((i == ni - 1) & (j == nj - 1) & (kk == nk - 1))
            def _():
                exchange.wait(x_refs, out_refs, sems)

    a_spec = pl.BlockSpec((tk, tm), lambda i, j, kk: (kk, i)) if ta else pl.BlockSpec((tm, tk), lambda i, j, kk: (i, kk))
    b_spec = pl.BlockSpec((tn, tk), lambda i, j, kk: (j, kk)) if tb else pl.BlockSpec((tk, tn), lambda i, j, kk: (kk, j))
    o_spec = pl.BlockSpec((tm, tn), lambda i, j, kk: (i, j))
    o_shape = jax.ShapeDtypeStruct((m, n), out_dtype)
    if not exchange:
        return pl.pallas_call(
            body, name=name, grid=(ni, nj, nk), in_specs=[a_spec, b_spec], out_specs=o_spec, out_shape=o_shape,
            scratch_shapes=[pltpu.VMEM((tm, tn), F32)],
            compiler_params=_cparams("parallel", "parallel", "arbitrary"),
        )(a, b)
    res = pl.pallas_call(
        body, name=name, grid=(ni, nj, nk), in_specs=[a_spec, b_spec] + exchange.in_specs,
        out_specs=[o_spec] + exchange.out_specs, out_shape=[o_shape] + exchange.out_shape,
        scratch_shapes=[pltpu.VMEM((tm, tn), F32)] + exchange.scratch,
        compiler_params=_cparams("arbitrary", "arbitrary", "arbitrary"),
    )(a, b, *exchange.arrays)
    return res[0], res[1:]


_NT = (((1,), (1,)), ((), ()))
_TN = (((0,), (0,)), ((), ()))


def _dot(a, b):
    return jnp.dot(a, b, preferred_element_type=F32)


def _dot_nt(a, b):
    return lax.dot_general(a, b, _NT, preferred_element_type=F32)


def _dot_tn(a, b):
    return lax.dot_general(a, b, _TN, preferred_element_type=F32)


def _running_sum(x, tri):
    return _dot(x.astype(MXU_DTYPE), tri)


def _neg_softplus(z):
    u = jnp.exp2(jnp.abs(z) * (-1.0 / math.log(2.0)))
    return -jnp.maximum(z, 0.0) - jnp.log(1.0 + u)


def _walk_blocks(step, st, n, descending, group=2):
    done = 0
    size = group
    while size >= 1:
        def trip(t, s, size=size, done=done):
            js = [done + size * t + g for g in range(size)]
            return step([n - 1 - j for j in js] if descending else js, s)

        trips = (n - done) // size
        st = lax.fori_loop(0, trips, trip, st)
        done = done + size * trips
        size //= 2
    return st


def _chains(js):
    return [(h, t) for t in range(len(js)) for h in range(2)]


def _rowsum(x):
    return jnp.sum(x, axis=1, keepdims=True)


def _attn_consts():
    row = lax.broadcasted_iota(jnp.int32, (BQ, BK), 0)
    col = lax.broadcasted_iota(jnp.int32, (BQ, BK), 1)
    trow = lax.broadcasted_iota(jnp.int32, (BK, BK), 0)
    tcol = lax.broadcasted_iota(jnp.int32, (BK, BK), 1)
    lane = lax.broadcasted_iota(jnp.int32, (BQ, LANES), 1)
    klane = lax.broadcasted_iota(jnp.int32, (BK, LANES), 1)
    return row, col, trow, tcol, lane, klane


DIAG = BQ // BK
assert BQ == DIAG * BK


def _diag_block(i):
    return i * DIAG


def _diag_blocks(jd, descending):
    places = list(reversed(range(DIAG))) if descending else list(range(DIAG))
    return [jd + g for g in places], places


def _key_slice(j):
    return pl.ds(pl.multiple_of(j * BK, BK), BK)


def _sb_forward(proj, s_len):
    nq = s_len // BQ
    assert s_len // BK <= HD
    qc, kc, vc = O_SBQ // LANES, O_SBK // LANES, O_SBV // LANES

    def body(q_ref, k_ref, v_ref, o_ref, tails_ref):
        i = pl.program_id(1)
        jd = _diag_block(i)
        row, col, trow, tcol, lane, klane = _attn_consts()
        strict = [col + g * BK < row for g in range(DIAG)]
        tri = (trow >= tcol).astype(MXU_DTYPE)
        q = q_ref[...] * 0.125
        qh = [jnp.where(lane < HD, q, 0.0).astype(MXU_DTYPE), jnp.where(lane >= HD, q, 0.0).astype(MXU_DTYPE)]

        km = [klane < HD, klane >= HD]

        def step(js, st, diag):
            carry, acc, tail = st
            chains = _chains(js)
            kj = [k_ref[_key_slice(j), :].astype(MXU_DTYPE) for j in js]
            vj = [v_ref[_key_slice(j), :].astype(MXU_DTYPE) for j in js]
            z = {(h, t): _dot_nt(qh[h], kj[t]) for h, t in chains}
            run = list(carry)
            suf, carry_in = {}, {}
            for h, t in chains:
                lom = _neg_softplus(z[h, t])
                if diag:
                    lom = jnp.where(strict[diag[t]], lom, 0.0)
                suf[h, t] = _running_sum(lom, tri)
                carry_in[h, t] = run[h]
                run[h] = run[h] + _rowsum(lom)
            for h, t in chains:
                a = jnp.exp(z[h, t] + suf[h, t] + carry_in[h, t])
                if diag:
                    a = jnp.where(strict[diag[t]], a, 0.0)
                acc = acc + _dot(a.astype(MXU_DTYPE), jnp.where(km[h], vj[t], 0))
                tail = jnp.where(lane == h * HD + js[t], carry_in[h, t], tail)
            return tuple(run), acc, tail

        zero = jnp.zeros((BQ, LANES), F32)
        diag_js, places = _diag_blocks(jd, True)
        st = step(diag_js, ((jnp.zeros((BQ, 1), F32),) * 2, zero, zero), places)
        st = _walk_blocks(lambda js, s: step(js, s, False), st, jd, True, group=4)
        o_ref[...] = st[1]
        tails_ref[...] = st[2]

    blk = pl.BlockSpec((BQ, LANES), lambda p, i: (i, p))
    out = jax.ShapeDtypeStruct((s_len, SB_W), F32)
    return pl.pallas_call(
        body, name="sb_fwd", grid=(HEADS // 2, nq),
        in_specs=[pl.BlockSpec((BQ, LANES), lambda p, i: (i, qc + p)),
                  pl.BlockSpec((s_len, LANES), lambda p, i: (0, kc + p)),
                  pl.BlockSpec((s_len, LANES), lambda p, i: (0, vc + p))],
        out_specs=[blk, blk], out_shape=[out, out],
        compiler_params=_cparams("parallel", "parallel"),
    )(proj, proj, proj)


def _sb_backward(proj, tails, do, s_len, exchange):
    nq = s_len // BQ
    qc, kc, vc = O_SBQ // LANES, O_SBK // LANES, O_SBV // LANES
    n_ex = len(exchange.arrays)

    def body(q_ref, k_ref, v_ref, tails_ref, do_ref, *rest):
        x_refs, (dq_ref, dk_ref, dv_ref) = rest[:n_ex], rest[n_ex:n_ex + 3]
        out_refs, sems = rest[n_ex + 3:2 * n_ex + 3], rest[2 * n_ex + 3:]
        p = pl.program_id(0)
        i = pl.program_id(1)
        jd = _diag_block(i)

        @pl.when((p == 0) & (i == 0))
        def _():
            exchange.start(x_refs, out_refs, sems)

        @pl.when(i == 0)
        def _():
            dk_ref[...] = jnp.zeros(dk_ref.shape, F32)
            dv_ref[...] = jnp.zeros(dv_ref.shape, F32)

        row, col, trow, tcol, lane, klane = _attn_consts()
        strict = [col + g * BK < row for g in range(DIAG)]
        tri = (trow >= tcol).astype(MXU_DTYPE)
        tri_p = (trow <= tcol).astype(MXU_DTYPE)
        q = q_ref[...] * 0.125
        tails_blk = tails_ref[...]
        do_blk = do_ref[...]
        hm = [lane < HD, lane >= HD]
        km = [klane < HD, klane >= HD]
        qh = [jnp.where(m, q, 0.0).astype(MXU_DTYPE) for m in hm]
        doh = [jnp.where(m, do_blk, 0.0).astype(MXU_DTYPE) for m in hm]

        def step(js, st, diag):
            before, dq = st
            chains = _chains(js)
            kj = [k_ref[_key_slice(j), :].astype(MXU_DTYPE) for j in js]
            vj = [v_ref[_key_slice(j), :].astype(MXU_DTYPE) for j in js]
            z = {(h, t): _dot_nt(qh[h], kj[t]) for h, t in chains}
            da = {(h, t): _dot_nt(doh[h], vj[t]) for h, t in chains}
            suf, sig = {}, {}
            for h, t in chains:
                lom = _neg_softplus(z[h, t])
                if diag:
                    lom = jnp.where(strict[diag[t]], lom, 0.0)
                suf[h, t] = _running_sum(lom, tri)
                sig[h, t] = jnp.exp(z[h, t] + lom)
            run = list(before)
            dl, pre, before_in = {}, {}, {}
            dk_add, dv_add = [None] * len(js), [None] * len(js)
            for h, t in chains:
                tail = _rowsum(jnp.where(lane == h * HD + js[t], tails_blk, 0.0))
                a = jnp.exp(z[h, t] + suf[h, t] + tail)
                if diag:
                    a = jnp.where(strict[diag[t]], a, 0.0)
                dl[h, t] = da[h, t] * a
                pre[h, t] = _dot(dl[h, t].astype(MXU_DTYPE), tri_p)
                dv_h = _dot_tn(a.astype(MXU_DTYPE), doh[h])
                dv_add[t] = dv_h if dv_add[t] is None else dv_add[t] + dv_h
                before_in[h, t] = run[h]
                run[h] = run[h] + _rowsum(dl[h, t])
            for h, t in chains:
                upto = before_in[h, t] + pre[h, t]
                dz = dl[h, t] - sig[h, t] * upto
                if diag:
                    dz = jnp.where(strict[diag[t]], dz, 0.0)
                dzb = dz.astype(MXU_DTYPE)
                dq = dq + _dot(dzb, jnp.where(km[h], kj[t], 0))
                dk_h = _dot_tn(dzb, qh[h])
                dk_add[t] = dk_h if dk_add[t] is None else dk_add[t] + dk_h
            for t, j in enumerate(js):
                dk_ref[_key_slice(j), :] += dk_add[t]
                dv_ref[_key_slice(j), :] += dv_add[t]
            return tuple(run), dq

        st = ((jnp.zeros((BQ, 1), F32),) * 2, jnp.zeros((BQ, LANES), F32))
        st = _walk_blocks(lambda js, s: step(js, s, False), st, jd, False, group=4)
        diag_js, places = _diag_blocks(jd, False)
        st = step(diag_js, st, places)
        dq_ref[...] = st[1] * 0.125

        @pl.when((p == HEADS // 2 - 1) & (i == nq - 1))
        def _():
            exchange.wait(x_refs, out_refs, sems)

    blk = pl.BlockSpec((BQ, LANES), lambda p, i: (i, p))
    full = pl.BlockSpec((s_len, LANES), lambda p, i: (0, p))
    out = jax.ShapeDtypeStruct((s_len, SB_W), F32)
    res = pl.pallas_call(
        body, name="sb_bwd", grid=(HEADS // 2, nq),
        in_specs=[pl.BlockSpec((BQ, LANES), lambda p, i: (i, qc + p)),
                  pl.BlockSpec((s_len, LANES), lambda p, i: (0, kc + p)),
                  pl.BlockSpec((s_len, LANES), lambda p, i: (0, vc + p)),
                  blk, blk] + exchange.in_specs,
        out_specs=[blk, full, full] + exchange.out_specs, out_shape=[out, out, out] + exchange.out_shape,
        scratch_shapes=exchange.scratch,
        compiler_params=_cparams("arbitrary", "arbitrary"),
    )(proj, proj, proj, tails, do, *exchange.arrays)
    return res[:3], res[3:]


def _pair_mask(rows, h):
    lane = lax.broadcasted_iota(jnp.int32, (rows, 2 * LANES), 1)
    rot = lane - LANES
    return (((lane < LANES) & (lane // HD == h))
            | ((lane >= LANES) & (rot < 2 * ROPE) & ((rot // (ROPE // 2)) % 2 == h)))


def _mla_forward(q_cat, k_cat, kv, s_len):
    nq = s_len // BQ
    scale = 1.0 / math.sqrt(QK_DIM)

    def body(q_ref, k_ref, v_ref, o_ref, lse_ref):
        i = pl.program_id(1)
        jd = _diag_block(i)
        row, col, trow, tcol, lane, klane = _attn_consts()
        causal = [col + g * BK <= row for g in range(DIAG)]
        q = q_ref[...]
        hm = [lane < HD, lane >= HD]
        km = [klane < HD, klane >= HD]
        qh = [jnp.where(_pair_mask(BQ, h), q, 0) for h in range(2)]

        def step(js, st, diag):
            m_run, l_run, acc = st
            chains = _chains(js)
            kj = [k_ref[_key_slice(j), :] for j in js]
            vj = [v_ref[_key_slice(j), :].astype(MXU_DTYPE) for j in js]
            s = {}
            for h, t in chains:
                s[h, t] = _dot_nt(qh[h], kj[t]) * scale
                if diag:
                    s[h, t] = jnp.where(causal[diag[t]], s[h, t], -jnp.inf)
            m_new, alpha, l_new = [], [], []
            for h in range(2):
                top = m_run[h]
                for t in range(len(js)):
                    top = jnp.maximum(top, jnp.max(s[h, t], axis=1, keepdims=True))
                m_new.append(top)
                alpha.append(jnp.exp(m_run[h] - top))
                l_new.append(alpha[h] * l_run[h])
            add = None
            for h, t in chains:
                pr = jnp.exp(s[h, t] - m_new[h])
                l_new[h] = l_new[h] + _rowsum(pr)
                part = _dot(pr.astype(MXU_DTYPE), jnp.where(km[h], vj[t], 0))
                add = part if add is None else add + part
            acc = jnp.where(hm[0], alpha[0], alpha[1]) * acc + add
            return tuple(m_new), tuple(l_new), acc

        st = ((jnp.full((BQ, 1), -1e30, F32),) * 2, (jnp.zeros((BQ, 1), F32),) * 2, jnp.zeros((BQ, LANES), F32))
        diag_js, places = _diag_blocks(jd, True)
        st = step(diag_js, st, places)
        m_run, l_run, acc = _walk_blocks(lambda js, s: step(js, s, False), st, jd, True, group=4)
        o_ref[...] = acc / jnp.where(hm[0], l_run[0], l_run[1])
        lse_ref[...] = jnp.where(hm[0], m_run[0] + jnp.log(l_run[0]), m_run[1] + jnp.log(l_run[1]))

    blk = pl.BlockSpec((BQ, LANES), lambda p, i: (i, p))
    out = jax.ShapeDtypeStruct((s_len, MLA_W), F32)
    return pl.pallas_call(
        body, name="mla_fwd", grid=(HEADS // 2, nq),
        in_specs=[pl.BlockSpec((BQ, 2 * LANES), lambda p, i: (i, p)),
                  pl.BlockSpec((s_len, 2 * LANES), lambda p, i: (0, p)),
                  pl.BlockSpec((s_len, LANES), lambda p, i: (0, MLA_W // LANES + p))],
        out_specs=[blk, blk], out_shape=[out, out],
        compiler_params=_cparams("parallel", "parallel"),
    )(q_cat, k_cat, kv)


def _mla_backward(q_cat, k_cat, kv, o, lse, do, s_len):
    nq = s_len // BQ
    scale = 1.0 / math.sqrt(QK_DIM)

    def body(q_ref, k_ref, v_ref, o_ref, lse_ref, do_ref, dq_ref, dk_ref, dv_ref):
        i = pl.program_id(1)

        @pl.when(i == 0)
        def _():
            dk_ref[...] = jnp.zeros(dk_ref.shape, F32)
            dv_ref[...] = jnp.zeros(dv_ref.shape, F32)

        jd = _diag_block(i)
        row, col, trow, tcol, lane, klane = _attn_consts()
        causal = [col + g * BK <= row for g in range(DIAG)]
        q = q_ref[...]
        o_blk = o_ref[...]
        do_blk = do_ref[...]
        lse_blk = lse_ref[...]
        hm = [lane < HD, lane >= HD]
        kpm = [_pair_mask(BK, h) for h in range(2)]
        qh = [jnp.where(_pair_mask(BQ, h), q, 0) for h in range(2)]
        doh_f = [jnp.where(m, do_blk, 0.0) for m in hm]
        doh = [d.astype(MXU_DTYPE) for d in doh_f]
        delta = [jnp.sum(d * o_blk, axis=1, keepdims=True) for d in doh_f]
        lse_h = [jnp.sum(jnp.where(lane == h * HD, lse_blk, 0.0), axis=1, keepdims=True) for h in range(2)]

        def step(js, st, diag):
            dq = st
            chains = _chains(js)
            kj = [k_ref[_key_slice(j), :] for j in js]
            vj = [v_ref[_key_slice(j), :].astype(MXU_DTYPE) for j in js]
            s = {(h, t): _dot_nt(qh[h], kj[t]) for h, t in chains}
            dp = {(h, t): _dot_nt(doh[h], vj[t]) for h, t in chains}
            adds = [[None] * len(js) for _ in range(2)]

            def accumulate(slot, t, part):
                adds[slot][t] = part if adds[slot][t] is None else adds[slot][t] + part

            for h, t in chains:
                pr = jnp.exp(s[h, t] * scale - lse_h[h])
                if diag:
                    pr = jnp.where(causal[diag[t]], pr, 0.0)
                dsb = (pr * (dp[h, t] - delta[h]) * scale).astype(MXU_DTYPE)
                dq = dq + _dot(dsb, jnp.where(kpm[h], kj[t], 0))
                accumulate(0, t, _dot_tn(dsb, qh[h]))
                accumulate(1, t, _dot_tn(pr.astype(MXU_DTYPE), doh[h]))
            for t, j in enumerate(js):
                dk_ref[_key_slice(j), :] += adds[0][t]
                dv_ref[_key_slice(j), :] += adds[1][t]
            return dq

        diag_js, places = _diag_blocks(jd, True)
        st = step(diag_js, jnp.zeros((BQ, 2 * LANES), F32), places)
        dq_ref[...] = _walk_blocks(lambda js, s: step(js, s, False), st, jd, True, group=4)

    blk = pl.BlockSpec((BQ, LANES), lambda p, i: (i, p))
    full = pl.BlockSpec((s_len, LANES), lambda p, i: (0, p))
    out = jax.ShapeDtypeStruct((s_len, MLA_W), F32)
    out_cat = jax.ShapeDtypeStruct((s_len, 2 * MLA_W), F32)
    return pl.pallas_call(
        body, name="mla_bwd", grid=(HEADS // 2, nq),
        in_specs=[pl.BlockSpec((BQ, 2 * LANES), lambda p, i: (i, p)),
                  pl.BlockSpec((s_len, 2 * LANES), lambda p, i: (0, p)),
                  pl.BlockSpec((s_len, LANES), lambda p, i: (0, MLA_W // LANES + p)),
                  blk, blk, blk],
        out_specs=[pl.BlockSpec((BQ, 2 * LANES), lambda p, i: (i, p)),
                   pl.BlockSpec((s_len, 2 * LANES), lambda p, i: (0, p)), full],
        out_shape=[out_cat, out_cat, out],
        compiler_params=_cparams("arbitrary", "arbitrary"),
    )(q_cat, k_cat, kv, o, lse, do)


def _mesh_pos():
    return lax.axis_index("x"), lax.axis_index("y"), lax.axis_index("c")


def _dev_index(px, py, pc):
    return 4 * px + 2 * py + pc


def _all_gather(block, name):
    return _all_gather_parts([block], name)[0]


def _all_gather_parts(blocks, name):
    n = len(blocks)

    def body(*refs):
        x_refs, out_refs = refs[:n], refs[n:2 * n]
        send_sems, recv_sems, local_sems = refs[2 * n:]
        x, y, c = _mesh_pos()
        me, sibling = (x, y, c), (x, y, 1 - c)
        chips = [(1 - x, y), (x, 1 - y), (1 - x, 1 - y)]

        def copy(a, k, blockpos, to, src=None):
            slot = out_refs[a].at[_dev_index(*blockpos)]
            return pltpu.make_async_remote_copy(
                src_ref=slot if src is None else src, dst_ref=slot,
                send_sem=send_sems.at[7 * a + k], recv_sem=recv_sems.at[7 * a + k],
                device_id=to, device_id_type=pl.DeviceIdType.MESH)

        mine = [pltpu.make_async_copy(x_refs[a], out_refs[a].at[_dev_index(*me)], local_sems.at[a]) for a in range(n)]
        for cp in mine:
            cp.start()
        first = []
        for a in range(n):
            first.append(copy(a, 0, me, sibling, src=x_refs[a]))
            first += [copy(a, 1 + j, me, (*chip, c), src=x_refs[a]) for j, chip in enumerate(chips)]
        for cp in first:
            cp.start()
        passed = []
        for j, chip in enumerate(chips):
            for a in range(n):
                copy(a, 1 + j, (*chip, c), me).wait_recv()
                passed.append(copy(a, 4 + j, (*chip, c), sibling))
                passed[-1].start()
        for a in range(n):
            copy(a, 0, sibling, me).wait_recv()
            for j, chip in enumerate(chips):
                copy(a, 4 + j, (*chip, 1 - c), me).wait_recv()
        for cp in first + passed:
            cp.wait_send()
        for cp in mine:
            cp.wait()

    return pl.pallas_call(
        body, name=name,
        out_shape=[jax.ShapeDtypeStruct((N_DEV,) + b.shape, b.dtype) for b in blocks],
        in_specs=[pl.BlockSpec(memory_space=pl.ANY)] * n, out_specs=[pl.BlockSpec(memory_space=pl.ANY)] * n,
        scratch_shapes=[pltpu.SemaphoreType.DMA((7 * n,)), pltpu.SemaphoreType.DMA((7 * n,)),
                        pltpu.SemaphoreType.DMA((n,))],
    )(*blocks)


class _Exchange:
    def __init__(self, arrays):
        self.arrays = list(arrays)
        n = len(self.arrays)
        self.in_specs = [pl.BlockSpec(memory_space=pl.ANY)] * n
        self.out_specs = [pl.BlockSpec(memory_space=pl.ANY)] * n
        self.out_shape = [jax.ShapeDtypeStruct(a.shape, a.dtype) for a in self.arrays]
        self.scratch = [pltpu.SemaphoreType.DMA((7 * n,)), pltpu.SemaphoreType.DMA((7 * n,)),
                        pltpu.SemaphoreType.DMA((n,))]

    def _copies(self, x_refs, out_refs, sems, with_arrivals):
        send_sems, recv_sems, local_sems = sems
        x, y, c = _mesh_pos()
        me = _dev_index(x, y, c)
        flips = [(fx, fy, fc) for fx in (0, 1) for fy in (0, 1) for fc in (0, 1)][1:]
        peers = [(1 - x if fx else x, 1 - y if fy else y, 1 - c if fc else c) for fx, fy, fc in flips]
        mine, sends, arrivals = [], [], []
        for a in range(len(self.arrays)):
            mine.append(pltpu.make_async_copy(x_refs[a].at[me], out_refs[a].at[me], local_sems.at[a]))
            for k, peer in enumerate(peers):
                sends.append(pltpu.make_async_remote_copy(
                    src_ref=x_refs[a].at[_dev_index(*peer)], dst_ref=out_refs[a].at[me],
                    send_sem=send_sems.at[7 * a + k], recv_sem=recv_sems.at[7 * a + k],
                    device_id=peer, device_id_type=pl.DeviceIdType.MESH))
                if not with_arrivals:
                    continue
                arrivals.append(pltpu.make_async_remote_copy(
                    src_ref=x_refs[a].at[me], dst_ref=out_refs[a].at[_dev_index(*peer)],
                    send_sem=send_sems.at[7 * a + k], recv_sem=recv_sems.at[7 * a + k],
                    device_id=peer, device_id_type=pl.DeviceIdType.MESH))
        return mine, sends, arrivals

    def start(self, x_refs, out_refs, sems):
        mine, sends, _ = self._copies(x_refs, out_refs, sems, False)
        for cp in mine + sends:
            cp.start()

    def wait(self, x_refs, out_refs, sems):
        mine, sends, arrivals = self._copies(x_refs, out_refs, sems, True)
        for cp in arrivals:
            cp.wait_recv()
        for cp in sends:
            cp.wait_send()
        for cp in mine:
            cp.wait()


def _sum_blocks(parts, name):
    n, r, c = parts.shape
    row_tiles = [t for t in range(16, min(r, 2048) + 1, 16) if r % t == 0]
    if row_tiles:
        tr, tc = max(row_tiles), c
    else:
        tr, tc = r, 2 * LANES
    assert c % tc == 0

    def body(p_ref, o_ref):
        acc = p_ref[0].astype(F32)
        for s in range(1, n):
            acc = acc + p_ref[s].astype(F32)
        o_ref[...] = acc

    return pl.pallas_call(
        body, name=name, grid=(r // tr, c // tc),
        in_specs=[pl.BlockSpec((n, tr, tc), lambda i, j: (0, i, j))],
        out_specs=pl.BlockSpec((tr, tc), lambda i, j: (i, j)),
        out_shape=jax.ShapeDtypeStruct((r, c), F32),
        compiler_params=_cparams("parallel", "parallel"),
    )(parts)


def _sigmoid(x):
    return 1.0 / (1.0 + jnp.exp(-x))


def _silu(x):
    return x * _sigmoid(x)


def _silu_grad(x):
    s = _sigmoid(x)
    return s * (1.0 + x * (1.0 - s))


def _colsum(x):
    return jnp.sum(x, axis=0, keepdims=True)


def _rms(x):
    return lax.rsqrt(jnp.mean(x * x, axis=-1, keepdims=True) + EPS)


def _rms_bwd(xn, r, dxn):
    return r * (dxn - xn * jnp.mean(dxn * xn, axis=-1, keepdims=True))


def _adamw(w, g, m, v):
    m = ADAM_B1 * m + (1.0 - ADAM_B1) * g
    v = ADAM_B2 * v + (1.0 - ADAM_B2) * jnp.square(g)
    m_hat = m / (1.0 - ADAM_B1 ** ADAM_STEP)
    v_hat = v / (1.0 - ADAM_B2 ** ADAM_STEP)
    delta = -ADAM_LR * (m_hat / (jnp.sqrt(v_hat) + ADAM_EPS) + ADAM_WD * w)
    return delta, m, v


def _adamw_call(w, g, m, v, name):
    r, c = w.shape
    if r % 256 == 0:
        tr, tc = 256, c
    elif r * c <= 256 * 1024 or c % (2 * LANES):
        tr, tc = r, c
    else:
        tr, tc = r, 2 * LANES

    def body(w_ref, g_ref, m_ref, v_ref, d_out, m_out, v_out):
        d_out[...], m_out[...], v_out[...] = _adamw(w_ref[...], g_ref[...], m_ref[...], v_ref[...])

    spec = pl.BlockSpec((tr, tc), lambda i, j: (i, j))
    return pl.pallas_call(
        body, name=name, grid=(r // tr, c // tc), in_specs=[spec] * 4, out_specs=[spec] * 3,
        out_shape=[jax.ShapeDtypeStruct((r, c), F32)] * 3, compiler_params=_cparams("parallel", "parallel"),
    )(w, g, m, v)


def _uq_to_kernel_layout(w):
    lead = w.shape[:-1]
    t = w.reshape(lead + (HEADS, QK_DIM))
    return jnp.concatenate([t[..., :NOPE].reshape(lead + (HEADS * NOPE,)),
                            t[..., NOPE:NOPE + ROPE // 2].reshape(lead + (LANES,)),
                            t[..., NOPE + ROPE // 2:].reshape(lead + (LANES,))], axis=-1)


def _uq_from_kernel_layout(w):
    lead = w.shape[:-1]
    nope = w[..., :HEADS * NOPE].reshape(lead + (HEADS, NOPE))
    r1 = w[..., HEADS * NOPE:HEADS * NOPE + LANES].reshape(lead + (HEADS, ROPE // 2))
    r2 = w[..., HEADS * NOPE + LANES:].reshape(lead + (HEADS, ROPE // 2))
    return jnp.concatenate([nope, r1, r2], axis=-1).reshape(lead + (HEADS * QK_DIM,))


def _ukv_to_kernel_layout(w):
    lead = w.shape[:-1]
    t = w.reshape(lead + (HEADS, NOPE + HD))
    return jnp.concatenate([t[..., :NOPE].reshape(lead + (HEADS * NOPE,)),
                            t[..., NOPE:].reshape(lead + (HEADS * HD,))], axis=-1)


def _ukv_from_kernel_layout(w):
    lead = w.shape[:-1]
    kn = w[..., :HEADS * NOPE].reshape(lead + (HEADS, NOPE))
    vv = w[..., HEADS * NOPE:].reshape(lead + (HEADS, HD))
    return jnp.concatenate([kn, vv], axis=-1).reshape(lead + (HEADS * (NOPE + HD),))


def _w_in_t_to_kernel_layout(wt):
    sb = wt[0:2048]
    c_q = wt[2048:2432]
    c_kv = wt[2432:2688]
    k_rot = wt[2688:2720]
    mla_z = wt[2720:3232]
    gates = wt[3232:5280]
    zeros = jnp.zeros((LANES, wt.shape[1]), wt.dtype)
    k1 = jnp.tile(k_rot[:ROPE // 2], (HEADS, 1))
    k2 = jnp.tile(k_rot[ROPE // 2:], (HEADS, 1))
    return jnp.concatenate([gates, sb, mla_z, c_q, zeros, c_kv, k1, k2], axis=0)


def _w_in_t_from_kernel_layout(gt, g_rot):
    return jnp.concatenate([gt[O_SBQ:O_SBQ + 2048], gt[O_CQ:O_CQ + Q_RANK], gt[O_CKV:O_CKV + KV_RANK], g_rot,
                            gt[O_MLAZ:O_MLAZ + MLA_W], gt[O_GA:O_GA + 2 * D]], axis=0)


def kernel(x, c, positions, w_ada, b_ada, norm_gain, w_in, q_norm_gain, w_uq, kv_norm_gain, w_ukv, w_branch_a, w_branch_b, w_out, final_norm_gain, loss_target, m_w_ada, m_b_ada, m_norm_gain, m_w_in, m_q_norm_gain, m_w_uq, m_kv_norm_gain, m_w_ukv, m_w_branch_a, m_w_branch_b, m_w_out, m_final_norm_gain, v_w_ada, v_b_ada, v_norm_gain, v_w_in, v_q_norm_gain, v_w_uq, v_kv_norm_gain, v_w_ukv, v_w_branch_a, v_w_branch_b, v_w_out, v_final_norm_gain):
    s_len = x.shape[1]
    me = _dev_index(*_mesh_pos())
    x2d = x[0]
    tgt = loss_target[0]

    w_in_t = w_in[0].T.astype(BF16)
    big = [w_uq[0], w_ukv[0], w_branch_a[0], w_branch_b[0], w_out[0]]
    big_sizes = [int(w.size) for w in big]
    packed = jnp.concatenate([w.astype(BF16).reshape(-1, LANES) for w in big], axis=0)
    g_in_t, c_all = _all_gather_parts([w_in_t, c.reshape(8, LANES)], "gather_w_in")
    c_all = c_all.reshape(N_DEV, D)
    w_in_kt = _w_in_t_to_kernel_layout(g_in_t.reshape(N_DEV * w_in_t.shape[0], D))

    mod_cols = _mm(c_all, w_ada[0], name="ada_mod")
    mod_all = _all_gather(mod_cols, "gather_mod")
    mod = lax.dynamic_index_in_dim(mod_all, me, axis=1, keepdims=False).reshape(1, 3 * D)
    mod_shift, mod_scale, mod_gate = mod[:, :D], mod[:, D:2 * D], mod[:, 2 * D:]
    b_shift, b_scale, b_gate = b_ada[:, :D], b_ada[:, D:2 * D], b_ada[:, 2 * D:]
    g1 = norm_gain
    gq, gkv = q_norm_gain, kv_norm_gain
    gf = final_norm_gain.reshape(1, D)

    def f_h(x_, g1_, ms, bs, msc, bsc):
        xn = x_ * _rms(x_)
        return (xn * g1_ * (1.0 + (msc + bsc)) + (ms + bs),), ()

    (h,) = _rowwise(f_h, [x2d], [g1, mod_shift, b_shift, mod_scale, b_scale], [(D, BF16)], name="ada_norm")
    proj, (gathered,) = _mm(h, w_in_kt, tb=True, name="proj_in", tiles=(min(s_len, 1024), IN_PAD // 2, D),
                            exchange=_Exchange([jnp.broadcast_to(packed[None], (N_DEV,) + packed.shape)]))
    offs = [0]
    for n in big_sizes:
        offs.append(offs[-1] + n // LANES)

    def unpack(t, shape):
        return gathered[:, offs[t]:offs[t + 1], :].reshape((N_DEV,) + shape)

    def cols(t, shape):
        return unpack(t, shape).transpose(1, 0, 2).reshape(shape[0], N_DEV * shape[1])

    w_uq_k = _uq_to_kernel_layout(cols(0, big[0].shape))
    w_ukv_k = _ukv_to_kernel_layout(cols(1, big[1].shape))
    w_a_f = cols(2, big[2].shape)
    w_b_f = cols(3, big[3].shape)
    w_out_f = unpack(4, big[4].shape).reshape(D, D)

    o_a, sb_tails = _sb_forward(proj, s_len)

    def f_lat(cq, ckv, gq_, gkv_):
        return (cq * _rms(cq) * gq_, ckv * _rms(ckv) * gkv_), ()

    cq_n, ckv_n = _rowwise(f_lat, [(proj, O_CQ // Q_RANK, Q_RANK), (proj, O_CKV // KV_RANK, KV_RANK)], [gq, gkv],
                           [(Q_RANK, BF16), (KV_RANK, BF16)], name="latent_norm")
    q_mla = _mm(cq_n, w_uq_k, name="q_up")
    kv = _mm(ckv_n, w_ukv_k, name="kv_up")

    inv_freq = ROPE_BASE ** (-jnp.arange(0, ROPE, 2, dtype=F32) / ROPE)
    inv_freq_t = jnp.tile(inv_freq, HEADS).reshape(1, LANES)
    pos_col = positions.reshape(s_len, 1).astype(F32)

    pairs = HEADS // 2

    def f_rope(pos, qn, q1, q2, kn, k1, k2, freq):
        ang = pos * freq
        cs, sn = jnp.cos(ang), jnp.sin(ang)
        q1r, q2r = q1 * cs - q2 * sn, q1 * sn + q2 * cs
        k1r, k2r = k1 * cs - k2 * sn, k1 * sn + k2 * cs
        lane = lax.broadcasted_iota(jnp.int32, q1.shape, 1)
        first, second = lane < ROPE, (lane >= ROPE) & (lane < 2 * ROPE)
        k_rot = jnp.where(first, k1r, jnp.where(second, k2r, 0.0))
        q_parts, k_parts = [], []
        for p in range(pairs):
            q_rot = jnp.where(first, pltpu.roll(q1r, (LANES - ROPE * p) % LANES, 1),
                              jnp.where(second, pltpu.roll(q2r, (LANES + ROPE - ROPE * p) % LANES, 1), 0.0))
            q_parts += [qn[:, LANES * p:LANES * (p + 1)], q_rot]
            k_parts += [kn[:, LANES * p:LANES * (p + 1)], k_rot]
        return (jnp.concatenate(q_parts, axis=1), jnp.concatenate(k_parts, axis=1), cs, sn), ()

    q_cat, k_cat, cos_t, sin_t = _rowwise(
        f_rope, [pos_col, (q_mla, 0, MLA_W), (q_mla, 4, LANES), (q_mla, 5, LANES), (kv, 0, MLA_W),
                 (proj, O_KROT // LANES, LANES), (proj, O_KROT // LANES + 1, LANES)], [inv_freq_t],
        [(2 * MLA_W, BF16), (2 * MLA_W, BF16), (LANES, F32), (LANES, F32)], name="rope")

    o_b, lse = _mla_forward(q_cat, k_cat, kv, s_len)

    def f_gate(oa, za, ob, zb):
        return (oa * _silu(za), ob * _silu(zb)), ()

    ya_in, yb_in = _rowwise(f_gate, [o_a, (proj, O_SBZ // SB_W, SB_W), o_b, (proj, O_MLAZ // MLA_W, MLA_W)], [],
                            [(SB_W, BF16), (MLA_W, BF16)], name="branch_gate")
    y_a = _mm(ya_in, w_a_f, name="branch_a")
    y_b = _mm(yb_in, w_b_f, name="branch_b")

    def f_merge(ga, gb, ya, yb):
        return (_sigmoid(ga) * ya + _sigmoid(gb) * yb,), ()

    (merged,) = _rowwise(f_merge, [(proj, O_GA // D, D), (proj, O_GB // D, D), y_a, y_b], [], [(D, BF16)], name="merge")
    out = _mm(merged, w_out_f, name="out_proj")

    def f_loss(x_, out_, t_, mg, bg, gf_):
        gate = mg + bg
        x2 = x_ + gate * out_
        r2 = _rms(x2)
        xn2 = x2 * r2
        err = xn2 * gf_ - t_
        loss = jnp.full((1, LANES), 0.5 / D, F32) * jnp.sum(err * err)
        dy = err * (1.0 / D)
        dx2 = _rms_bwd(xn2, r2, dy * gf_)
        return (dx2, dx2 * gate), (loss, _colsum(dy * xn2), _colsum(dx2 * out_))

    dx2, d_out, loss_part, d_gf, d_gate = _rowwise(
        f_loss, [x2d, out, tgt], [mod_gate, b_gate, gf], [(D, F32), (D, BF16)], [LANES, D, D], name="loss_head")

    d_merged = _mm(d_out, w_out_f, tb=True, name="d_merged")
    dw_out = _mm(merged, d_out, ta=True, name="dw_out")

    def f_dmerge(dm, ga, gb, ya, yb):
        sa, sb = _sigmoid(ga), _sigmoid(gb)
        return (dm * sa, dm * sb, dm * ya * sa * (1.0 - sa), dm * yb * sb * (1.0 - sb)), ()

    d_ya, d_yb, d_ga, d_gb = _rowwise(f_dmerge, [d_merged, (proj, O_GA // D, D), (proj, O_GB // D, D), y_a, y_b], [],
                                      [(D, BF16)] * 4, name="d_merge")
    dw_a = _mm(ya_in, d_ya, ta=True, name="dw_branch_a")
    dw_b = _mm(yb_in, d_yb, ta=True, name="dw_branch_b")
    d_ya_in = _mm(d_ya, w_a_f, tb=True, name="d_branch_a")
    d_yb_in = _mm(d_yb, w_b_f, tb=True, name="d_branch_b")

    def f_dgate(da, oa, za, db, ob, zb):
        return (da * _silu(za), da * oa * _silu_grad(za), db * _silu(zb), db * ob * _silu_grad(zb)), ()

    d_oa, d_sbz, d_ob, d_mlaz = _rowwise(
        f_dgate, [d_ya_in, o_a, (proj, O_SBZ // SB_W, SB_W), d_yb_in, o_b, (proj, O_MLAZ // MLA_W, MLA_W)], [],
        [(SB_W, F32), (SB_W, BF16), (MLA_W, F32), (MLA_W, BF16)], name="d_branch_gate")

    dq_cat, dk_cat, dv_b = _mla_backward(q_cat, k_cat, kv, o_b, lse, d_ob, s_len)

    def f_drope(dq, dk, dv_, cs, sn):
        lane = lax.broadcasted_iota(jnp.int32, cs.shape, 1)
        first, second = lane < ROPE, (lane >= ROPE) & (lane < 2 * ROPE)
        dq1 = dq2 = dk1 = dk2 = None
        for p in range(pairs):
            q_rot = dq[:, LANES * (2 * p + 1):LANES * (2 * p + 2)]
            k_rot = dk[:, LANES * (2 * p + 1):LANES * (2 * p + 2)]
            parts = (pltpu.roll(jnp.where(first, q_rot, 0.0), (ROPE * p) % LANES, 1),
                     pltpu.roll(jnp.where(second, q_rot, 0.0), (LANES - ROPE + ROPE * p) % LANES, 1),
                     jnp.where(first, k_rot, 0.0), jnp.where(second, k_rot, 0.0))
            if p == 0:
                dq1, dq2, dk1, dk2 = parts
            else:
                dq1, dq2, dk1, dk2 = dq1 + parts[0], dq2 + parts[1], dk1 + parts[2], dk2 + parts[3]
        dqn_ = [dq[:, 2 * LANES * p:2 * LANES * p + LANES] for p in range(pairs)]
        dkn_ = [dk[:, 2 * LANES * p:2 * LANES * p + LANES] for p in range(pairs)]
        return (jnp.concatenate(dqn_ + [dq1 * cs + dq2 * sn, dq2 * cs - dq1 * sn], axis=1),
                jnp.concatenate(dkn_ + [dv_], axis=1),
                jnp.concatenate([dk1 * cs + dk2 * sn, dk2 * cs - dk1 * sn], axis=1)), ()

    dq_k, dkv_k, d_krot = _rowwise(f_drope, [dq_cat, dk_cat, dv_b, cos_t, sin_t], [],
                                   [(HEADS * QK_DIM, BF16), (2 * MLA_W, BF16), (2 * LANES, BF16)], name="d_rope")
    dw_uq_k = _mm(cq_n, dq_k, ta=True, name="dw_uq")
    dw_ukv_k = _mm(ckv_n, dkv_k, ta=True, name="dw_ukv")
    d_cqn = _mm(dq_k, w_uq_k, tb=True, name="d_cq_norm")
    d_ckvn = _mm(dkv_k, w_ukv_k, tb=True, name="d_ckv_norm")

    def f_dlat(cq, dcqn, ckv, dckvn, gq_, gkv_):
        rq, rkv = _rms(cq), _rms(ckv)
        cqn, ckvn = cq * rq, ckv * rkv
        return ((_rms_bwd(cqn, rq, dcqn * gq_), _rms_bwd(ckvn, rkv, dckvn * gkv_)),
                (_colsum(dcqn * cqn), _colsum(dckvn * ckvn)))

    d_cq, d_ckv, d_gq, d_gkv = _rowwise(
        f_dlat, [(proj, O_CQ // Q_RANK, Q_RANK), d_cqn, (proj, O_CKV // KV_RANK, KV_RANK), d_ckvn], [gq, gkv],
        [(Q_RANK, BF16), (KV_RANK, BF16)], [Q_RANK, KV_RANK], name="d_latent_norm")

    def col_blocks(g):
        kdim, n8 = g.shape
        return g.astype(BF16).reshape(kdim, N_DEV, n8 // N_DEV).transpose(1, 0, 2).reshape(N_DEV, -1, LANES)

    g_blocks = jnp.concatenate([col_blocks(_uq_from_kernel_layout(dw_uq_k)), col_blocks(_ukv_from_kernel_layout(dw_ukv_k)),
                                col_blocks(dw_a), col_blocks(dw_b), dw_out.astype(BF16).reshape(N_DEV, -1, LANES)], axis=1)
    (d_sbq, d_sbk, d_sbv), (g_recv,) = _sb_backward(proj, sb_tails, d_oa, s_len, _Exchange([g_blocks]))

    d_proj = jnp.concatenate([d_ga, d_gb, d_sbq.astype(BF16), d_sbk.astype(BF16), d_sbv.astype(BF16), d_sbz, d_mlaz,
                              d_cq, jnp.zeros((s_len, LANES), BF16), d_ckv, d_krot], axis=1)
    dw_in_kt = _mm(d_proj, h, ta=True, name="dw_in", tiles=(512, D, s_len))

    def krot_body(t_ref, o_ref):
        half = ROPE // 2
        for part in range(2):
            acc = t_ref[part * LANES:part * LANES + half, :]
            for hh in range(1, HEADS):
                acc = acc + t_ref[part * LANES + hh * half:part * LANES + (hh + 1) * half, :]
            o_ref[part * half:(part + 1) * half, :] = acc

    dw_krot = pl.pallas_call(krot_body, name="dw_krot_sum", out_shape=jax.ShapeDtypeStruct((ROPE, D), F32))(
        dw_in_kt[O_KROT:O_KROT + 2 * LANES])

    dw_in_t = _w_in_t_from_kernel_layout(dw_in_kt, dw_krot)
    g_in_blocks = dw_in_t.astype(BF16).reshape(N_DEV, -1, D)
    dh, (g_in_recv,) = _mm(d_proj, w_in_kt, name="d_h", exchange=_Exchange([g_in_blocks]))

    def f_dx(x_, dh_, dx2_, g1_, msc, bsc):
        r = _rms(x_)
        xn = x_ * r
        dn1 = dh_ * (1.0 + (msc + bsc))
        return ((dx2_ + _rms_bwd(xn, r, dn1 * g1_),),
                (_colsum(dh_), _colsum(dh_ * (xn * g1_)), _colsum(dn1 * xn)))

    grad_x2d, d_shift, d_scale, d_g1 = _rowwise(f_dx, [x2d, dh, dx2], [g1, mod_scale, b_scale], [(D, F32)],
                                                [D, D, D], name="d_ada_norm")

    g_in_sum_t = _sum_blocks(g_in_recv, "sum_grads_w_in")
    g_sum = _sum_blocks(g_recv, "sum_grads")
    g_big = [g_sum[offs[t]:offs[t + 1]].reshape(big[t].shape) for t in range(5)]

    small = jnp.concatenate([d_shift, d_scale, d_gate, d_g1, d_gq, d_gkv, d_gf, loss_part], axis=1)
    n_small = small.shape[1]
    pad = (-n_small) % (8 * LANES)
    small = jnp.pad(small, ((0, 0), (0, pad))).reshape(-1, LANES)
    small_all = _all_gather(small, "gather_small")
    small_sum = _sum_blocks(small_all, "sum_small").reshape(1, -1)
    g_b_ada = small_sum[:, :3 * D]
    g_g1 = small_sum[:, 3 * D:4 * D]
    g_gq = small_sum[:, 4 * D:4 * D + Q_RANK]
    g_gkv = small_sum[:, 4 * D + Q_RANK:4 * D + Q_RANK + KV_RANK]
    g_gf = small_sum[:, 4 * D + Q_RANK + KV_RANK:4 * D + Q_RANK + KV_RANK + D]

    dmod_all = small_all.reshape(N_DEV, -1)[:, :3 * D]
    dmod_cols = lax.dynamic_slice_in_dim(dmod_all, me * (3 * D // N_DEV), 3 * D // N_DEV, axis=1)
    g_w_ada = _mm(c_all, dmod_cols, ta=True, name="dw_ada")

    loss = small_sum[0, n_small - LANES]

    names = ["w_ada", "b_ada", "norm_gain", "w_in", "q_norm_gain", "w_uq", "kv_norm_gain", "w_ukv",
             "w_branch_a", "w_branch_b", "w_out", "final_norm_gain"]
    weights = dict(w_ada=w_ada, b_ada=b_ada, norm_gain=norm_gain, w_in=w_in, q_norm_gain=q_norm_gain, w_uq=w_uq,
                   kv_norm_gain=kv_norm_gain, w_ukv=w_ukv, w_branch_a=w_branch_a, w_branch_b=w_branch_b, w_out=w_out,
                   final_norm_gain=final_norm_gain)
    moms = dict(w_ada=m_w_ada, b_ada=m_b_ada, norm_gain=m_norm_gain, w_in=m_w_in, q_norm_gain=m_q_norm_gain,
                w_uq=m_w_uq, kv_norm_gain=m_kv_norm_gain, w_ukv=m_w_ukv, w_branch_a=m_w_branch_a,
                w_branch_b=m_w_branch_b, w_out=m_w_out, final_norm_gain=m_final_norm_gain)
    vels = dict(w_ada=v_w_ada, b_ada=v_b_ada, norm_gain=v_norm_gain, w_in=v_w_in, q_norm_gain=v_q_norm_gain,
                w_uq=v_w_uq, kv_norm_gain=v_kv_norm_gain, w_ukv=v_w_ukv, w_branch_a=v_w_branch_a,
                w_branch_b=v_w_branch_b, w_out=v_w_out, final_norm_gain=v_final_norm_gain)
    grads2d = dict(w_ada=g_w_ada, b_ada=g_b_ada, norm_gain=g_g1, w_in=g_in_sum_t, q_norm_gain=g_gq, w_uq=g_big[0],
                   kv_norm_gain=g_gkv, w_ukv=g_big[1], w_branch_a=g_big[2], w_branch_b=g_big[3], w_out=g_big[4],
                   final_norm_gain=g_gf)

    grads, deltas, new_m, new_v = [], [], [], []
    for n in names:
        w = weights[n]
        if n == "w_in":
            to2d = lambda t: t[0].T
            back = lambda t: t.T[None]
        else:
            shape2d = grads2d[n].shape
            to2d = lambda t, s=shape2d: t.reshape(s)
            back = lambda t, s=w.shape: t.reshape(s)
        d_, m_, v_ = _adamw_call(to2d(w), grads2d[n], to2d(moms[n]), to2d(vels[n]), "adamw_" + n)
        grads.append(back(grads2d[n]))
        deltas.append(back(d_))
        new_m.append(back(m_))
        new_v.append(back(v_))

    return (loss, grad_x2d.reshape(x.shape), *grads, *deltas, *new_m, *new_v)
```

```python
import functools
import math

import jax
import jax.numpy as jnp
from jax import lax
from jax.experimental import pallas as pl
from jax.experimental.pallas import tpu as pltpu

F32 = jnp.float32
BF16 = jnp.bfloat16
MXU_DTYPE = jnp.bfloat16

N_DEV = 8
D = 1024
HEADS = 8
HD = 64
SB_W = 512
MLA_W = 512
Q_RANK = 384
KV_RANK = 256
ROPE = 32
NOPE = 64
QK_DIM = NOPE + ROPE
EPS = 1e-6
ROPE_BASE = 10000.0

ADAM_LR = 0.001
ADAM_B1 = 0.9
ADAM_B2 = 0.999
ADAM_EPS = 1e-08
ADAM_WD = 0.01
ADAM_STEP = 10

LANES = 128
VMEM_LIMIT = 48 * 1024 * 1024

O_GA, O_GB = 0, 1024
O_SBQ, O_SBK, O_SBV, O_SBZ = 2048, 2560, 3072, 3584
O_MLAZ = 4096
O_CQ = 4608
O_CKV = 5120
O_KROT = 5376
IN_PAD = 5632

BQ = 256
BK = 256


def _cparams(*sem):
    return pltpu.CompilerParams(dimension_semantics=sem, vmem_limit_bytes=VMEM_LIMIT)


def _tile_of(n, cap=512):
    if n <= cap:
        return n
    for t in (1024, 768, 512, 384, 256, 128):
        if t <= cap and n % t == 0:
            return t
    raise ValueError(n)


def _rowwise(fn, rows, vecs, outs, reds=(), *, name, tile=512):
    norm = []
    for r in rows:
        if isinstance(r, tuple):
            arr, cb, w = r[:3]
            ro = r[3] if len(r) > 3 else 0
        else:
            arr, cb, w, ro = r, 0, r.shape[1], 0
        norm.append((arr, cb, w, ro))
    s_len = norm[0][0].shape[0]
    tile = min(tile, s_len)
    assert s_len % tile == 0
    n_row, n_vec, n_out, n_red = len(norm), len(vecs), len(outs), len(reds)

    def body(*refs):
        step = pl.program_id(0)
        row_refs = refs[:n_row]
        vec_refs = refs[n_row:n_row + n_vec]
        out_refs = refs[n_row + n_vec:n_row + n_vec + n_out]
        red_refs = refs[n_row + n_vec + n_out:]
        row_res, red_res = fn(*[r[...] for r in row_refs], *[v[...] for v in vec_refs])
        for o, val in zip(out_refs, row_res):
            o[...] = val.astype(o.dtype)
        if n_red:
            @pl.when(step == 0)
            def _():
                for r in red_refs:
                    r[...] = jnp.zeros(r.shape, r.dtype)
            for r, val in zip(red_refs, red_res):
                r[...] += val

    in_specs = []
    for arr, cb, w, ro in norm:
        in_specs.append(pl.BlockSpec((tile, w), functools.partial(lambda i, cb, rb: (i + rb, cb), cb=cb, rb=ro // tile)))
        assert ro % tile == 0
    for v in vecs:
        in_specs.append(pl.BlockSpec(v.shape, lambda i: (0, 0)))
    out_shape = [jax.ShapeDtypeStruct((s_len, w), dt) for w, dt in outs]
    out_specs = [pl.BlockSpec((tile, w), lambda i: (i, 0)) for w, _ in outs]
    out_shape += [jax.ShapeDtypeStruct((1, w), F32) for w in reds]
    out_specs += [pl.BlockSpec((1, w), lambda i: (0, 0)) for w in reds]
    res = pl.pallas_call(
        body, name=name, grid=(s_len // tile,), in_specs=in_specs, out_specs=out_specs, out_shape=out_shape,
        compiler_params=_cparams("arbitrary" if n_red else "parallel"),
    )(*[a for a, _, _, _ in norm], *vecs)
    return res


def _mm(a, b, *, ta=False, tb=False, out_dtype=F32, name, exchange=None, tiles=None):
    m, k = (a.shape[1], a.shape[0]) if ta else a.shape
    n = b.shape[0] if tb else b.shape[1]
    assert (b.shape[1] if tb else b.shape[0]) == k
    tm, tn, tk = tiles or (_tile_of(m, 1024), _tile_of(n, 1024 if n <= 1024 else 512), _tile_of(k, 1024))
    assert m % tm == 0 and n % tn == 0 and k % tk == 0
    ni, nj, nk = m // tm, n // tn, k // tk
    dims = (((0 if ta else 1,), (1 if tb else 0,)), ((), ()))
    n_ex = len(exchange.arrays) if exchange else 0

    def body(*refs):
        a_ref, b_ref = refs[:2]
        x_refs, o_ref, out_refs = refs[2:2 + n_ex], refs[2 + n_ex], refs[3 + n_ex:3 + 2 * n_ex]
        acc_ref, sems = refs[3 + 2 * n_ex], refs[4 + 2 * n_ex:]
        i, j, kk = pl.program_id(0), pl.program_id(1), pl.program_id(2)

        if exchange:
            @pl.when((i == 0) & (j == 0) & (kk == 0))
            def _():
                exchange.start(x_refs, out_refs, sems)

        @pl.when(kk == 0)
        def _():
            acc_ref[...] = jnp.zeros(acc_ref.shape, F32)

        acc_ref[...] += lax.dot_general(a_ref[...].astype(MXU_DTYPE), b_ref[...].astype(MXU_DTYPE), dims,
                                        preferred_element_type=F32)

        @pl.when(kk == nk - 1)
        def _():
            o_ref[...] = acc_ref[...].astype(o_ref.dtype)

        if exchange:
            @pl.when((i == ni - 1) & (j == nj - 1) & (kk == nk - 1))
            def _():
                exchange.wait(x_refs, out_refs, sems)

    a_spec = pl.BlockSpec((tk, tm), lambda i, j, kk: (kk, i)) if ta else pl.BlockSpec((tm, tk), lambda i, j, kk: (i, kk))
    b_spec = pl.BlockSpec((tn, tk), lambda i, j, kk: (j, kk)) if tb else pl.BlockSpec((tk, tn), lambda i, j, kk: (kk, j))
    o_spec = pl.BlockSpec((tm, tn), lambda i, j, kk: (i, j))
    o_shape = jax.ShapeDtypeStruct((m, n), out_dtype)
    if not exchange:
        return pl.pallas_call(
            body, name=name, grid=(ni, nj, nk), in_specs=[a_spec, b_spec], out_specs=o_spec, out_shape=o_shape,
            scratch_shapes=[pltpu.VMEM((tm, tn), F32)],
            compiler_params=_cparams("parallel", "parallel", "arbitrary"),
        )(a, b)
    res = pl.pallas_call(
        body, name=name, grid=(ni, nj, nk), in_specs=[a_spec, b_spec] + exchange.in_specs,
        out_specs=[o_spec] + exchange.out_specs, out_shape=[o_shape] + exchange.out_shape,
        scratch_shapes=[pltpu.VMEM((tm, tn), F32)] + exchange.scratch,
        compiler_params=_cparams("arbitrary", "arbitrary", "arbitrary"),
    )(a, b, *exchange.arrays)
    return res[0], res[1:]


_NT = (((1,), (1,)), ((), ()))
_TN = (((0,), (0,)), ((), ()))


def _dot(a, b):
    return jnp.dot(a, b, preferred_element_type=F32)


def _dot_nt(a, b):
    return lax.dot_general(a, b, _NT, preferred_element_type=F32)


def _dot_tn(a, b):
    return lax.dot_general(a, b, _TN, preferred_element_type=F32)


def _running_sum(x, tri):
    return _dot(x.astype(MXU_DTYPE), tri)


def _neg_softplus(z):
    u = jnp.exp2(jnp.abs(z) * (-1.0 / math.log(2.0)))
    return -jnp.maximum(z, 0.0) - jnp.log(1.0 + u)


def _walk_blocks(step, st, n, descending, group=2):
    done = 0
    size = group
    while size >= 1:
        def trip(t, s, size=size, done=done):
            js = [done + size * t + g for g in range(size)]
            return step([n - 1 - j for j in js] if descending else js, s)

        trips = (n - done) // size
        st = lax.fori_loop(0, trips, trip, st)
        done = done + size * trips
        size //= 2
    return st


def _chains(js):
    return [(h, t) for t in range(len(js)) for h in range(2)]


def _rowsum(x):
    return jnp.sum(x, axis=1, keepdims=True)


def _attn_consts():
    row = lax.broadcasted_iota(jnp.int32, (BQ, BK), 0)
    col = lax.broadcasted_iota(jnp.int32, (BQ, BK), 1)
    trow = lax.broadcasted_iota(jnp.int32, (BK, BK), 0)
    tcol = lax.broadcasted_iota(jnp.int32, (BK, BK), 1)
    lane = lax.broadcasted_iota(jnp.int32, (BQ, LANES), 1)
    klane = lax.broadcasted_iota(jnp.int32, (BK, LANES), 1)
    return row, col, trow, tcol, lane, klane


DIAG = BQ // BK
assert BQ == DIAG * BK


def _diag_block(i):
    return i * DIAG


def _diag_blocks(jd, descending):
    places = list(reversed(range(DIAG))) if descending else list(range(DIAG))
    return [jd + g for g in places], places


def _key_slice(j):
    return pl.ds(pl.multiple_of(j * BK, BK), BK)


def _sb_forward(proj, s_len):
    nq = s_len // BQ
    assert s_len // BK <= HD
    qc, kc, vc = O_SBQ // LANES, O_SBK // LANES, O_SBV // LANES

    def body(q_ref, k_ref, v_ref, o_ref, tails_ref):
        i = pl.program_id(1)
        jd = _diag_block(i)
        row, col, trow, tcol, lane, klane = _attn_consts()
        strict = [col + g * BK < row for g in range(DIAG)]
        tri = (trow >= tcol).astype(MXU_DTYPE)
        q = q_ref[...] * 0.125
        qh = [jnp.where(lane < HD, q, 0.0).astype(MXU_DTYPE), jnp.where(lane >= HD, q, 0.0).astype(MXU_DTYPE)]

        km = [klane < HD, klane >= HD]

        def step(js, st, diag):
            carry, acc, tail = st
            chains = _chains(js)
            kj = [k_ref[_key_slice(j), :].astype(MXU_DTYPE) for j in js]
            vj = [v_ref[_key_slice(j), :].astype(MXU_DTYPE) for j in js]
            z = {(h, t): _dot_nt(qh[h], kj[t]) for h, t in chains}
            run = list(carry)
            suf, carry_in = {}, {}
            for h, t in chains:
                lom = _neg_softplus(z[h, t])
                if diag:
                    lom = jnp.where(strict[diag[t]], lom, 0.0)
                suf[h, t] = _running_sum(lom, tri)
                carry_in[h, t] = run[h]
                run[h] = run[h] + _rowsum(lom)
            for h, t in chains:
                a = jnp.exp(z[h, t] + suf[h, t] + carry_in[h, t])
                if diag:
                    a = jnp.where(strict[diag[t]], a, 0.0)
                acc = acc + _dot(a.astype(MXU_DTYPE), jnp.where(km[h], vj[t], 0))
                tail = jnp.where(lane == h * HD + js[t], carry_in[h, t], tail)
            return tuple(run), acc, tail

        zero = jnp.zeros((BQ, LANES), F32)
        diag_js, places = _diag_blocks(jd, True)
        st = step(diag_js, ((jnp.zeros((BQ, 1), F32),) * 2, zero, zero), places)
        st = _walk_blocks(lambda js, s: step(js, s, False), st, jd, True, group=8)
        o_ref[...] = st[1]
        tails_ref[...] = st[2]

    blk = pl.BlockSpec((BQ, LANES), lambda p, i: (i, p))
    out = jax.ShapeDtypeStruct((s_len, SB_W), F32)
    return pl.pallas_call(
        body, name="sb_fwd", grid=(HEADS // 2, nq),
        in_specs=[pl.BlockSpec((BQ, LANES), lambda p, i: (i, qc + p)),
                  pl.BlockSpec((s_len, LANES), lambda p, i: (0, kc + p)),
                  pl.BlockSpec((s_len, LANES), lambda p, i: (0, vc + p))],
        out_specs=[blk, blk], out_shape=[out, out],
        compiler_params=_cparams("parallel", "parallel"),
    )(proj, proj, proj)


def _sb_backward(proj, tails, do, s_len, exchange):
    nq = s_len // BQ
    qc, kc, vc = O_SBQ // LANES, O_SBK // LANES, O_SBV // LANES
    n_ex = len(exchange.arrays)

    def body(q_ref, k_ref, v_ref, tails_ref, do_ref, *rest):
        x_refs, (dq_ref, dk_ref, dv_ref) = rest[:n_ex], rest[n_ex:n_ex + 3]
        out_refs, sems = rest[n_ex + 3:2 * n_ex + 3], rest[2 * n_ex + 3:]
        p = pl.program_id(0)
        i = pl.program_id(1)
        jd = _diag_block(i)

        @pl.when((p == 0) & (i == 0))
        def _():
            exchange.start(x_refs, out_refs, sems)

        @pl.when(i == 0)
        def _():
            dk_ref[...] = jnp.zeros(dk_ref.shape, F32)
            dv_ref[...] = jnp.zeros(dv_ref.shape, F32)

        row, col, trow, tcol, lane, klane = _attn_consts()
        strict = [col + g * BK < row for g in range(DIAG)]
        tri = (trow >= tcol).astype(MXU_DTYPE)
        tri_p = (trow <= tcol).astype(MXU_DTYPE)
        q = q_ref[...] * 0.125
        tails_blk = tails_ref[...]
        do_blk = do_ref[...]
        hm = [lane < HD, lane >= HD]
        km = [klane < HD, klane >= HD]
        qh = [jnp.where(m, q, 0.0).astype(MXU_DTYPE) for m in hm]
        doh = [jnp.where(m, do_blk, 0.0).astype(MXU_DTYPE) for m in hm]

        def step(js, st, diag):
            before, dq = st
            chains = _chains(js)
            kj = [k_ref[_key_slice(j), :].astype(MXU_DTYPE) for j in js]
            vj = [v_ref[_key_slice(j), :].astype(MXU_DTYPE) for j in js]
            z = {(h, t): _dot_nt(qh[h], kj[t]) for h, t in chains}
            da = {(h, t): _dot_nt(doh[h], vj[t]) for h, t in chains}
            suf, sig = {}, {}
            for h, t in chains:
                lom = _neg_softplus(z[h, t])
                if diag:
                    lom = jnp.where(strict[diag[t]], lom, 0.0)
                suf[h, t] = _running_sum(lom, tri)
                sig[h, t] = jnp.exp(z[h, t] + lom)
            run = list(before)
            dl, pre, before_in = {}, {}, {}
            dk_add, dv_add = [None] * len(js), [None] * len(js)
            for h, t in chains:
                tail = _rowsum(jnp.where(lane == h * HD + js[t], tails_blk, 0.0))
                a = jnp.exp(z[h, t] + suf[h, t] + tail)
                if diag:
                    a = jnp.where(strict[diag[t]], a, 0.0)
                dl[h, t] = da[h, t] * a
                pre[h, t] = _dot(dl[h, t].astype(MXU_DTYPE), tri_p)
                dv_h = _dot_tn(a.astype(MXU_DTYPE), doh[h])
                dv_add[t] = dv_h if dv_add[t] is None else dv_add[t] + dv_h
                before_in[h, t] = run[h]
                run[h] = run[h] + _rowsum(dl[h, t])
            for h, t in chains:
                upto = before_in[h, t] + pre[h, t]
                dz = dl[h, t] - sig[h, t] * upto
                if diag:
                    dz = jnp.where(strict[diag[t]], dz, 0.0)
                dzb = dz.astype(MXU_DTYPE)
                dq = dq + _dot(dzb, jnp.where(km[h], kj[t], 0))
                dk_h = _dot_tn(dzb, qh[h])
                dk_add[t] = dk_h if dk_add[t] is None else dk_add[t] + dk_h
            for t, j in enumerate(js):
                dk_ref[_key_slice(j), :] += dk_add[t]
                dv_ref[_key_slice(j), :] += dv_add[t]
            return tuple(run), dq

        st = ((jnp.zeros((BQ, 1), F32),) * 2, jnp.zeros((BQ, LANES), F32))
        st = _walk_blocks(lambda js, s: step(js, s, False), st, jd, False, group=4)
        diag_js, places = _diag_blocks(jd, False)
        st = step(diag_js, st, places)
        dq_ref[...] = st[1] * 0.125

        @pl.when((p == HEADS // 2 - 1) & (i == nq - 1))
        def _():
            exchange.wait(x_refs, out_refs, sems)

    blk = pl.BlockSpec((BQ, LANES), lambda p, i: (i, p))
    full = pl.BlockSpec((s_len, LANES), lambda p, i: (0, p))
    out = jax.ShapeDtypeStruct((s_len, SB_W), F32)
    res = pl.pallas_call(
        body, name="sb_bwd", grid=(HEADS // 2, nq),
        in_specs=[pl.BlockSpec((BQ, LANES), lambda p, i: (i, qc + p)),
                  pl.BlockSpec((s_len, LANES), lambda p, i: (0, kc + p)),
                  pl.BlockSpec((s_len, LANES), lambda p, i: (0, vc + p)),
                  blk, blk] + exchange.in_specs,
        out_specs=[blk, full, full] + exchange.out_specs, out_shape=[out, out, out] + exchange.out_shape,
        scratch_shapes=exchange.scratch,
        compiler_params=_cparams("arbitrary", "arbitrary"),
    )(proj, proj, proj, tails, do, *exchange.arrays)
    return res[:3], res[3:]


def _pair_mask(rows, h):
    lane = lax.broadcasted_iota(jnp.int32, (rows, 2 * LANES), 1)
    rot = lane - LANES
    return (((lane < LANES) & (lane // HD == h))
            | ((lane >= LANES) & (rot < 2 * ROPE) & ((rot // (ROPE // 2)) % 2 == h)))


def _mla_forward(q_cat, k_cat, kv, s_len):
    nq = s_len // BQ
    scale = 1.0 / math.sqrt(QK_DIM)

    def body(q_ref, k_ref, v_ref, o_ref, lse_ref):
        i = pl.program_id(1)
        jd = _diag_block(i)
        row, col, trow, tcol, lane, klane = _attn_consts()
        causal = [col + g * BK <= row for g in range(DIAG)]
        q = q_ref[...]
        hm = [lane < HD, lane >= HD]
        km = [klane < HD, klane >= HD]
        qh = [jnp.where(_pair_mask(BQ, h), q, 0) for h in range(2)]

        def step(js, st, diag):
            m_run, l_run, acc = st
            chains = _chains(js)
            kj = [k_ref[_key_slice(j), :] for j in js]
            vj = [v_ref[_key_slice(j), :].astype(MXU_DTYPE) for j in js]
            s = {}
            for h, t in chains:
                s[h, t] = _dot_nt(qh[h], kj[t]) * scale
                if diag:
                    s[h, t] = jnp.where(causal[diag[t]], s[h, t], -jnp.inf)
            m_new, alpha, l_new = [], [], []
            for h in range(2):
                top = m_run[h]
                for t in range(len(js)):
                    top = jnp.maximum(top, jnp.max(s[h, t], axis=1, keepdims=True))
                m_new.append(top)
                alpha.append(jnp.exp(m_run[h] - top))
                l_new.append(alpha[h] * l_run[h])
            add = None
            for h, t in chains:
                pr = jnp.exp(s[h, t] - m_new[h])
                l_new[h] = l_new[h] + _rowsum(pr)
                part = _dot(pr.astype(MXU_DTYPE), jnp.where(km[h], vj[t], 0))
                add = part if add is None else add + part
            acc = jnp.where(hm[0], alpha[0], alpha[1]) * acc + add
            return tuple(m_new), tuple(l_new), acc

        st = ((jnp.full((BQ, 1), -1e30, F32),) * 2, (jnp.zeros((BQ, 1), F32),) * 2, jnp.zeros((BQ, LANES), F32))
        diag_js, places = _diag_blocks(jd, True)
        st = step(diag_js, st, places)
        m_run, l_run, acc = _walk_blocks(lambda js, s: step(js, s, False), st, jd, True, group=8)
        o_ref[...] = acc / jnp.where(hm[0], l_run[0], l_run[1])
        lse_ref[...] = jnp.where(hm[0], m_run[0] + jnp.log(l_run[0]), m_run[1] + jnp.log(l_run[1]))

    blk = pl.BlockSpec((BQ, LANES), lambda p, i: (i, p))
    out = jax.ShapeDtypeStruct((s_len, MLA_W), F32)
    return pl.pallas_call(
        body, name="mla_fwd", grid=(HEADS // 2, nq),
        in_specs=[pl.BlockSpec((BQ, 2 * LANES), lambda p, i: (i, p)),
                  pl.BlockSpec((s_len, 2 * LANES), lambda p, i: (0, p)),
                  pl.BlockSpec((s_len, LANES), lambda p, i: (0, MLA_W // LANES + p))],
        out_specs=[blk, blk], out_shape=[out, out],
        compiler_params=_cparams("parallel", "parallel"),
    )(q_cat, k_cat, kv)


def _mla_backward(q_cat, k_cat, kv, o, lse, do, s_len):
    nq = s_len // BQ
    scale = 1.0 / math.sqrt(QK_DIM)

    def body(q_ref, k_ref, v_ref, o_ref, lse_ref, do_ref, dq_ref, dk_ref, dv_ref):
        i = pl.program_id(1)

        @pl.when(i == 0)
        def _():
            dk_ref[...] = jnp.zeros(dk_ref.shape, F32)
            dv_ref[...] = jnp.zeros(dv_ref.shape, F32)

        jd = _diag_block(i)
        row, col, trow, tcol, lane, klane = _attn_consts()
        causal = [col + g * BK <= row for g in range(DIAG)]
        q = q_ref[...]
        o_blk = o_ref[...]
        do_blk = do_ref[...]
        lse_blk = lse_ref[...]
        hm = [lane < HD, lane >= HD]
        kpm = [_pair_mask(BK, h) for h in range(2)]
        qh = [jnp.where(_pair_mask(BQ, h), q, 0) for h in range(2)]
        doh_f = [jnp.where(m, do_blk, 0.0) for m in hm]
        doh = [d.astype(MXU_DTYPE) for d in doh_f]
        delta = [jnp.sum(d * o_blk, axis=1, keepdims=True) for d in doh_f]
        lse_h = [jnp.sum(jnp.where(lane == h * HD, lse_blk, 0.0), axis=1, keepdims=True) for h in range(2)]

        def step(js, st, diag):
            dq = st
            chains = _chains(js)
            kj = [k_ref[_key_slice(j), :] for j in js]
            vj = [v_ref[_key_slice(j), :].astype(MXU_DTYPE) for j in js]
            s = {(h, t): _dot_nt(qh[h], kj[t]) for h, t in chains}
            dp = {(h, t): _dot_nt(doh[h], vj[t]) for h, t in chains}
            adds = [[None] * len(js) for _ in range(2)]

            def accumulate(slot, t, part):
                adds[slot][t] = part if adds[slot][t] is None else adds[slot][t] + part

            for h, t in chains:
                pr = jnp.exp(s[h, t] * scale - lse_h[h])
                if diag:
                    pr = jnp.where(causal[diag[t]], pr, 0.0)
                dsb = (pr * (dp[h, t] - delta[h]) * scale).astype(MXU_DTYPE)
                dq = dq + _dot(dsb, jnp.where(kpm[h], kj[t], 0))
                accumulate(0, t, _dot_tn(dsb, qh[h]))
                accumulate(1, t, _dot_tn(pr.astype(MXU_DTYPE), doh[h]))
            for t, j in enumerate(js):
                dk_ref[_key_slice(j), :] += adds[0][t]
                dv_ref[_key_slice(j), :] += adds[1][t]
            return dq

        diag_js, places = _diag_blocks(jd, True)
        st = step(diag_js, jnp.zeros((BQ, 2 * LANES), F32), places)
        dq_ref[...] = _walk_blocks(lambda js, s: step(js, s, False), st, jd, True, group=4)

    blk = pl.BlockSpec((BQ, LANES), lambda p, i: (i, p))
    full = pl.BlockSpec((s_len, LANES), lambda p, i: (0, p))
    out = jax.ShapeDtypeStruct((s_len, MLA_W), F32)
    out_cat = jax.ShapeDtypeStruct((s_len, 2 * MLA_W), F32)
    return pl.pallas_call(
        body, name="mla_bwd", grid=(HEADS // 2, nq),
        in_specs=[pl.BlockSpec((BQ, 2 * LANES), lambda p, i: (i, p)),
                  pl.BlockSpec((s_len, 2 * LANES), lambda p, i: (0, p)),
                  pl.BlockSpec((s_len, LANES), lambda p, i: (0, MLA_W // LANES + p)),
                  blk, blk, blk],
        out_specs=[pl.BlockSpec((BQ, 2 * LANES), lambda p, i: (i, p)),
                   pl.BlockSpec((s_len, 2 * LANES), lambda p, i: (0, p)), full],
        out_shape=[out_cat, out_cat, out],
        compiler_params=_cparams("arbitrary", "arbitrary"),
    )(q_cat, k_cat, kv, o, lse, do)


def _mesh_pos():
    return lax.axis_index("x"), lax.axis_index("y"), lax.axis_index("c")


def _dev_index(px, py, pc):
    return 4 * px + 2 * py + pc


def _all_gather(block, name):
    return _all_gather_parts([block], name)[0]


def _all_gather_parts(blocks, name):
    n = len(blocks)

    def body(*refs):
        x_refs, out_refs = refs[:n], refs[n:2 * n]
        send_sems, recv_sems, local_sems = refs[2 * n:]
        x, y, c = _mesh_pos()
        me, sibling = (x, y, c), (x, y, 1 - c)
        chips = [(1 - x, y), (x, 1 - y), (1 - x, 1 - y)]

        def copy(a, k, blockpos, to, src=None):
            slot = out_refs[a].at[_dev_index(*blockpos)]
            return pltpu.make_async_remote_copy(
                src_ref=slot if src is None else src, dst_ref=slot,
                send_sem=send_sems.at[7 * a + k], recv_sem=recv_sems.at[7 * a + k],
                device_id=to, device_id_type=pl.DeviceIdType.MESH)

        mine = [pltpu.make_async_copy(x_refs[a], out_refs[a].at[_dev_index(*me)], local_sems.at[a]) for a in range(n)]
        for cp in mine:
            cp.start()
        first = []
        for a in range(n):
            first.append(copy(a, 0, me, sibling, src=x_refs[a]))
            first += [copy(a, 1 + j, me, (*chip, c), src=x_refs[a]) for j, chip in enumerate(chips)]
        for cp in first:
            cp.start()
        passed = []
        for j, chip in enumerate(chips):
            for a in range(n):
                copy(a, 1 + j, (*chip, c), me).wait_recv()
                passed.append(copy(a, 4 + j, (*chip, c), sibling))
                passed[-1].start()
        for a in range(n):
            copy(a, 0, sibling, me).wait_recv()
            for j, chip in enumerate(chips):
                copy(a, 4 + j, (*chip, 1 - c), me).wait_recv()
        for cp in first + passed:
            cp.wait_send()
        for cp in mine:
            cp.wait()

    return pl.pallas_call(
        body, name=name,
        out_shape=[jax.ShapeDtypeStruct((N_DEV,) + b.shape, b.dtype) for b in blocks],
        in_specs=[pl.BlockSpec(memory_space=pl.ANY)] * n, out_specs=[pl.BlockSpec(memory_space=pl.ANY)] * n,
        scratch_shapes=[pltpu.SemaphoreType.DMA((7 * n,)), pltpu.SemaphoreType.DMA((7 * n,)),
                        pltpu.SemaphoreType.DMA((n,))],
    )(*blocks)


class _Exchange:
    def __init__(self, arrays):
        self.arrays = list(arrays)
        n = len(self.arrays)
        self.in_specs = [pl.BlockSpec(memory_space=pl.ANY)] * n
        self.out_specs = [pl.BlockSpec(memory_space=pl.ANY)] * n
        self.out_shape = [jax.ShapeDtypeStruct(a.shape, a.dtype) for a in self.arrays]
        self.scratch = [pltpu.SemaphoreType.DMA((7 * n,)), pltpu.SemaphoreType.DMA((7 * n,)),
                        pltpu.SemaphoreType.DMA((n,))]

    def _copies(self, x_refs, out_refs, sems, with_arrivals):
        send_sems, recv_sems, local_sems = sems
        x, y, c = _mesh_pos()
        me = _dev_index(x, y, c)
        flips = [(fx, fy, fc) for fx in (0, 1) for fy in (0, 1) for fc in (0, 1)][1:]
        peers = [(1 - x if fx else x, 1 - y if fy else y, 1 - c if fc else c) for fx, fy, fc in flips]
        mine, sends, arrivals = [], [], []
        for a in range(len(self.arrays)):
            mine.append(pltpu.make_async_copy(x_refs[a].at[me], out_refs[a].at[me], local_sems.at[a]))
            for k, peer in enumerate(peers):
                sends.append(pltpu.make_async_remote_copy(
                    src_ref=x_refs[a].at[_dev_index(*peer)], dst_ref=out_refs[a].at[me],
                    send_sem=send_sems.at[7 * a + k], recv_sem=recv_sems.at[7 * a + k],
                    device_id=peer, device_id_type=pl.DeviceIdType.MESH))
                if not with_arrivals:
                    continue
                arrivals.append(pltpu.make_async_remote_copy(
                    src_ref=x_refs[a].at[me], dst_ref=out_refs[a].at[_dev_index(*peer)],
                    send_sem=send_sems.at[7 * a + k], recv_sem=recv_sems.at[7 * a + k],
                    device_id=peer, device_id_type=pl.DeviceIdType.MESH))
        return mine, sends, arrivals

    def start(self, x_refs, out_refs, sems):
        mine, sends, _ = self._copies(x_refs, out_refs, sems, False)
        for cp in mine + sends:
            cp.start()

    def wait(self, x_refs, out_refs, sems):
        mine, sends, arrivals = self._copies(x_refs, out_refs, sems, True)
        for cp in arrivals:
            cp.wait_recv()
        for cp in sends:
            cp.wait_send()
        for cp in mine:
            cp.wait()


def _sum_blocks(parts, name):
    n, r, c = parts.shape
    row_tiles = [t for t in range(16, min(r, 2048) + 1, 16) if r % t == 0]
    if row_tiles:
        tr, tc = max(row_tiles), c
    else:
        tr, tc = r, 2 * LANES
    assert c % tc == 0

    def body(p_ref, o_ref):
        acc = p_ref[0].astype(F32)
        for s in range(1, n):
            acc = acc + p_ref[s].astype(F32)
        o_ref[...] = acc

    return pl.pallas_call(
        body, name=name, grid=(r // tr, c // tc),
        in_specs=[pl.BlockSpec((n, tr, tc), lambda i, j: (0, i, j))],
        out_specs=pl.BlockSpec((tr, tc), lambda i, j: (i, j)),
        out_shape=jax.ShapeDtypeStruct((r, c), F32),
        compiler_params=_cparams("parallel", "parallel"),
    )(parts)


def _sigmoid(x):
    return 1.0 / (1.0 + jnp.exp(-x))


def _silu(x):
    return x * _sigmoid(x)


def _silu_grad(x):
    s = _sigmoid(x)
    return s * (1.0 + x * (1.0 - s))


def _colsum(x):
    return jnp.sum(x, axis=0, keepdims=True)


def _rms(x):
    return lax.rsqrt(jnp.mean(x * x, axis=-1, keepdims=True) + EPS)


def _rms_bwd(xn, r, dxn):
    return r * (dxn - xn * jnp.mean(dxn * xn, axis=-1, keepdims=True))


def _adamw(w, g, m, v):
    m = ADAM_B1 * m + (1.0 - ADAM_B1) * g
    v = ADAM_B2 * v + (1.0 - ADAM_B2) * jnp.square(g)
    m_hat = m / (1.0 - ADAM_B1 ** ADAM_STEP)
    v_hat = v / (1.0 - ADAM_B2 ** ADAM_STEP)
    delta = -ADAM_LR * (m_hat / (jnp.sqrt(v_hat) + ADAM_EPS) + ADAM_WD * w)
    return delta, m, v


def _adamw_call(w, g, m, v, name):
    r, c = w.shape
    if r % 256 == 0:
        tr, tc = 256, c
    elif r * c <= 256 * 1024 or c % (2 * LANES):
        tr, tc = r, c
    else:
        tr, tc = r, 2 * LANES

    def body(w_ref, g_ref, m_ref, v_ref, d_out, m_out, v_out):
        d_out[...], m_out[...], v_out[...] = _adamw(w_ref[...], g_ref[...], m_ref[...], v_ref[...])

    spec = pl.BlockSpec((tr, tc), lambda i, j: (i, j))
    return pl.pallas_call(
        body, name=name, grid=(r // tr, c // tc), in_specs=[spec] * 4, out_specs=[spec] * 3,
        out_shape=[jax.ShapeDtypeStruct((r, c), F32)] * 3, compiler_params=_cparams("parallel", "parallel"),
    )(w, g, m, v)


def _uq_to_kernel_layout(w):
    lead = w.shape[:-1]
    t = w.reshape(lead + (HEADS, QK_DIM))
    return jnp.concatenate([t[..., :NOPE].reshape(lead + (HEADS * NOPE,)),
                            t[..., NOPE:NOPE + ROPE // 2].reshape(lead + (LANES,)),
                            t[..., NOPE + ROPE // 2:].reshape(lead + (LANES,))], axis=-1)


def _uq_from_kernel_layout(w):
    lead = w.shape[:-1]
    nope = w[..., :HEADS * NOPE].reshape(lead + (HEADS, NOPE))
    r1 = w[..., HEADS * NOPE:HEADS * NOPE + LANES].reshape(lead + (HEADS, ROPE // 2))
    r2 = w[..., HEADS * NOPE + LANES:].reshape(lead + (HEADS, ROPE // 2))
    return jnp.concatenate([nope, r1, r2], axis=-1).reshape(lead + (HEADS * QK_DIM,))


def _ukv_to_kernel_layout(w):
    lead = w.shape[:-1]
    t = w.reshape(lead + (HEADS, NOPE + HD))
    return jnp.concatenate([t[..., :NOPE].reshape(lead + (HEADS * NOPE,)),
                            t[..., NOPE:].reshape(lead + (HEADS * HD,))], axis=-1)


def _ukv_from_kernel_layout(w):
    lead = w.shape[:-1]
    kn = w[..., :HEADS * NOPE].reshape(lead + (HEADS, NOPE))
    vv = w[..., HEADS * NOPE:].reshape(lead + (HEADS, HD))
    return jnp.concatenate([kn, vv], axis=-1).reshape(lead + (HEADS * (NOPE + HD),))


def _w_in_t_to_kernel_layout(wt):
    sb = wt[0:2048]
    c_q = wt[2048:2432]
    c_kv = wt[2432:2688]
    k_rot = wt[2688:2720]
    mla_z = wt[2720:3232]
    gates = wt[3232:5280]
    zeros = jnp.zeros((LANES, wt.shape[1]), wt.dtype)
    k1 = jnp.tile(k_rot[:ROPE // 2], (HEADS, 1))
    k2 = jnp.tile(k_rot[ROPE // 2:], (HEADS, 1))
    return jnp.concatenate([gates, sb, mla_z, c_q, zeros, c_kv, k1, k2], axis=0)


def _w_in_t_from_kernel_layout(gt, g_rot):
    return jnp.concatenate([gt[O_SBQ:O_SBQ + 2048], gt[O_CQ:O_CQ + Q_RANK], gt[O_CKV:O_CKV + KV_RANK],
                            g_rot.astype(gt.dtype), gt[O_MLAZ:O_MLAZ + MLA_W], gt[O_GA:O_GA + 2 * D]], axis=0)


def kernel(x, c, positions, w_ada, b_ada, norm_gain, w_in, q_norm_gain, w_uq, kv_norm_gain, w_ukv, w_branch_a, w_branch_b, w_out, final_norm_gain, loss_target, m_w_ada, m_b_ada, m_norm_gain, m_w_in, m_q_norm_gain, m_w_uq, m_kv_norm_gain, m_w_ukv, m_w_branch_a, m_w_branch_b, m_w_out, m_final_norm_gain, v_w_ada, v_b_ada, v_norm_gain, v_w_in, v_q_norm_gain, v_w_uq, v_kv_norm_gain, v_w_ukv, v_w_branch_a, v_w_branch_b, v_w_out, v_final_norm_gain):
    s_len = x.shape[1]
    me = _dev_index(*_mesh_pos())
    x2d = x[0]
    tgt = loss_target[0]

    w_in_t = w_in[0].T.astype(BF16)
    big = [w_uq[0], w_ukv[0], w_branch_a[0], w_branch_b[0], w_out[0]]
    big_sizes = [int(w.size) for w in big]
    packed = jnp.concatenate([w.astype(BF16).reshape(-1, LANES) for w in big], axis=0)
    g_in_t, c_all = _all_gather_parts([w_in_t, c.reshape(8, LANES)], "gather_w_in")
    c_all = c_all.reshape(N_DEV, D)
    w_in_kt = _w_in_t_to_kernel_layout(g_in_t.reshape(N_DEV * w_in_t.shape[0], D))

    mod_cols = _mm(c_all, w_ada[0], name="ada_mod")
    mod_all = _all_gather(mod_cols, "gather_mod")
    mod = lax.dynamic_index_in_dim(mod_all, me, axis=1, keepdims=False).reshape(1, 3 * D)
    mod_shift, mod_scale, mod_gate = mod[:, :D], mod[:, D:2 * D], mod[:, 2 * D:]
    b_shift, b_scale, b_gate = b_ada[:, :D], b_ada[:, D:2 * D], b_ada[:, 2 * D:]
    g1 = norm_gain
    gq, gkv = q_norm_gain, kv_norm_gain
    gf = final_norm_gain.reshape(1, D)

    def f_h(x_, g1_, ms, bs, msc, bsc):
        xn = x_ * _rms(x_)
        return (xn * g1_ * (1.0 + (msc + bsc)) + (ms + bs),), ()

    (h,) = _rowwise(f_h, [x2d], [g1, mod_shift, b_shift, mod_scale, b_scale], [(D, BF16)], name="ada_norm")
    proj, (gathered,) = _mm(h, w_in_kt, tb=True, name="proj_in", tiles=(min(s_len, 1024), IN_PAD // 2, D),
                            exchange=_Exchange([jnp.broadcast_to(packed[None], (N_DEV,) + packed.shape)]))
    offs = [0]
    for n in big_sizes:
        offs.append(offs[-1] + n // LANES)

    def unpack(t, shape):
        return gathered[:, offs[t]:offs[t + 1], :].reshape((N_DEV,) + shape)

    def cols(t, shape):
        return unpack(t, shape).transpose(1, 0, 2).reshape(shape[0], N_DEV * shape[1])

    w_uq_k = _uq_to_kernel_layout(cols(0, big[0].shape))
    w_ukv_k = _ukv_to_kernel_layout(cols(1, big[1].shape))
    w_a_f = cols(2, big[2].shape)
    w_b_f = cols(3, big[3].shape)
    w_out_f = unpack(4, big[4].shape).reshape(D, D)

    o_a, sb_tails = _sb_forward(proj, s_len)

    def f_lat(cq, ckv, gq_, gkv_):
        return (cq * _rms(cq) * gq_, ckv * _rms(ckv) * gkv_), ()

    cq_n, ckv_n = _rowwise(f_lat, [(proj, O_CQ // Q_RANK, Q_RANK), (proj, O_CKV // KV_RANK, KV_RANK)], [gq, gkv],
                           [(Q_RANK, BF16), (KV_RANK, BF16)], name="latent_norm")
    q_mla = _mm(cq_n, w_uq_k, name="q_up")
    kv = _mm(ckv_n, w_ukv_k, name="kv_up")

    inv_freq = ROPE_BASE ** (-jnp.arange(0, ROPE, 2, dtype=F32) / ROPE)
    inv_freq_t = jnp.tile(inv_freq, HEADS).reshape(1, LANES)
    pos_col = positions.reshape(s_len, 1).astype(F32)

    pairs = HEADS // 2

    def f_rope(pos, qn, q1, q2, kn, k1, k2, freq):
        ang = pos * freq
        cs, sn = jnp.cos(ang), jnp.sin(ang)
        q1r, q2r = q1 * cs - q2 * sn, q1 * sn + q2 * cs
        k1r, k2r = k1 * cs - k2 * sn, k1 * sn + k2 * cs
        lane = lax.broadcasted_iota(jnp.int32, q1.shape, 1)
        first, second = lane < ROPE, (lane >= ROPE) & (lane < 2 * ROPE)
        k_rot = jnp.where(first, k1r, jnp.where(second, k2r, 0.0))
        q_parts, k_parts = [], []
        for p in range(pairs):
            q_rot = jnp.where(first, pltpu.roll(q1r, (LANES - ROPE * p) % LANES, 1),
                              jnp.where(second, pltpu.roll(q2r, (LANES + ROPE - ROPE * p) % LANES, 1), 0.0))
            q_parts += [qn[:, LANES * p:LANES * (p + 1)], q_rot]
            k_parts += [kn[:, LANES * p:LANES * (p + 1)], k_rot]
        return (jnp.concatenate(q_parts, axis=1), jnp.concatenate(k_parts, axis=1), cs, sn), ()

    q_cat, k_cat, cos_t, sin_t = _rowwise(
        f_rope, [pos_col, (q_mla, 0, MLA_W), (q_mla, 4, LANES), (q_mla, 5, LANES), (kv, 0, MLA_W),
                 (proj, O_KROT // LANES, LANES), (proj, O_KROT // LANES + 1, LANES)], [inv_freq_t],
        [(2 * MLA_W, BF16), (2 * MLA_W, BF16), (LANES, F32), (LANES, F32)], name="rope")

    o_b, lse = _mla_forward(q_cat, k_cat, kv, s_len)

    def f_gate(oa, za, ob, zb):
        return (oa * _silu(za), ob * _silu(zb)), ()

    ya_in, yb_in = _rowwise(f_gate, [o_a, (proj, O_SBZ // SB_W, SB_W), o_b, (proj, O_MLAZ // MLA_W, MLA_W)], [],
                            [(SB_W, BF16), (MLA_W, BF16)], name="branch_gate")
    y_a = _mm(ya_in, w_a_f, name="branch_a")
    y_b = _mm(yb_in, w_b_f, name="branch_b")

    def f_merge(ga, gb, ya, yb):
        return (_sigmoid(ga) * ya + _sigmoid(gb) * yb,), ()

    (merged,) = _rowwise(f_merge, [(proj, O_GA // D, D), (proj, O_GB // D, D), y_a, y_b], [], [(D, BF16)], name="merge")
    out = _mm(merged, w_out_f, name="out_proj")

    def f_loss(x_, out_, t_, mg, bg, gf_):
        gate = mg + bg
        x2 = x_ + gate * out_
        r2 = _rms(x2)
        xn2 = x2 * r2
        err = xn2 * gf_ - t_
        loss = jnp.full((1, LANES), 0.5 / D, F32) * jnp.sum(err * err)
        dy = err * (1.0 / D)
        dx2 = _rms_bwd(xn2, r2, dy * gf_)
        return (dx2, dx2 * gate), (loss, _colsum(dy * xn2), _colsum(dx2 * out_))

    dx2, d_out, loss_part, d_gf, d_gate = _rowwise(
        f_loss, [x2d, out, tgt], [mod_gate, b_gate, gf], [(D, F32), (D, BF16)], [LANES, D, D], name="loss_head")

    d_merged = _mm(d_out, w_out_f, tb=True, name="d_merged")
    dw_out = _mm(merged, d_out, ta=True, name="dw_out")

    def f_dmerge(dm, ga, gb, ya, yb):
        sa, sb = _sigmoid(ga), _sigmoid(gb)
        return (dm * sa, dm * sb, dm * ya * sa * (1.0 - sa), dm * yb * sb * (1.0 - sb)), ()

    d_ya, d_yb, d_ga, d_gb = _rowwise(f_dmerge, [d_merged, (proj, O_GA // D, D), (proj, O_GB // D, D), y_a, y_b], [],
                                      [(D, BF16)] * 4, name="d_merge")
    dw_a = _mm(ya_in, d_ya, ta=True, name="dw_branch_a")
    dw_b = _mm(yb_in, d_yb, ta=True, name="dw_branch_b")
    d_ya_in = _mm(d_ya, w_a_f, tb=True, name="d_branch_a")
    d_yb_in = _mm(d_yb, w_b_f, tb=True, name="d_branch_b")

    def f_dgate(da, oa, za, db, ob, zb):
        return (da * _silu(za), da * oa * _silu_grad(za), db * _silu(zb), db * ob * _silu_grad(zb)), ()

    d_oa, d_sbz, d_ob, d_mlaz = _rowwise(
        f_dgate, [d_ya_in, o_a, (proj, O_SBZ // SB_W, SB_W), d_yb_in, o_b, (proj, O_MLAZ // MLA_W, MLA_W)], [],
        [(SB_W, F32), (SB_W, BF16), (MLA_W, F32), (MLA_W, BF16)], name="d_branch_gate")

    dq_cat, dk_cat, dv_b = _mla_backward(q_cat, k_cat, kv, o_b, lse, d_ob, s_len)

    def f_drope(dq, dk, dv_, cs, sn):
        lane = lax.broadcasted_iota(jnp.int32, cs.shape, 1)
        first, second = lane < ROPE, (lane >= ROPE) & (lane < 2 * ROPE)
        dq1 = dq2 = dk1 = dk2 = None
        for p in range(pairs):
            q_rot = dq[:, LANES * (2 * p + 1):LANES * (2 * p + 2)]
            k_rot = dk[:, LANES * (2 * p + 1):LANES * (2 * p + 2)]
            parts = (pltpu.roll(jnp.where(first, q_rot, 0.0), (ROPE * p) % LANES, 1),
                     pltpu.roll(jnp.where(second, q_rot, 0.0), (LANES - ROPE + ROPE * p) % LANES, 1),
                     jnp.where(first, k_rot, 0.0), jnp.where(second, k_rot, 0.0))
            if p == 0:
                dq1, dq2, dk1, dk2 = parts
            else:
                dq1, dq2, dk1, dk2 = dq1 + parts[0], dq2 + parts[1], dk1 + parts[2], dk2 + parts[3]
        dqn_ = [dq[:, 2 * LANES * p:2 * LANES * p + LANES] for p in range(pairs)]
        dkn_ = [dk[:, 2 * LANES * p:2 * LANES * p + LANES] for p in range(pairs)]
        return (jnp.concatenate(dqn_ + [dq1 * cs + dq2 * sn, dq2 * cs - dq1 * sn], axis=1),
                jnp.concatenate(dkn_ + [dv_], axis=1),
                jnp.concatenate([dk1 * cs + dk2 * sn, dk2 * cs - dk1 * sn], axis=1)), ()

    dq_k, dkv_k, d_krot = _rowwise(f_drope, [dq_cat, dk_cat, dv_b, cos_t, sin_t], [],
                                   [(HEADS * QK_DIM, BF16), (2 * MLA_W, BF16), (2 * LANES, BF16)], name="d_rope")
    dw_uq_k = _mm(cq_n, dq_k, ta=True, name="dw_uq")
    dw_ukv_k = _mm(ckv_n, dkv_k, ta=True, name="dw_ukv")
    d_cqn = _mm(dq_k, w_uq_k, tb=True, name="d_cq_norm")
    d_ckvn = _mm(dkv_k, w_ukv_k, tb=True, name="d_ckv_norm")

    def f_dlat(cq, dcqn, ckv, dckvn, gq_, gkv_):
        rq, rkv = _rms(cq), _rms(ckv)
        cqn, ckvn = cq * rq, ckv * rkv
        return ((_rms_bwd(cqn, rq, dcqn * gq_), _rms_bwd(ckvn, rkv, dckvn * gkv_)),
                (_colsum(dcqn * cqn), _colsum(dckvn * ckvn)))

    d_cq, d_ckv, d_gq, d_gkv = _rowwise(
        f_dlat, [(proj, O_CQ // Q_RANK, Q_RANK), d_cqn, (proj, O_CKV // KV_RANK, KV_RANK), d_ckvn], [gq, gkv],
        [(Q_RANK, BF16), (KV_RANK, BF16)], [Q_RANK, KV_RANK], name="d_latent_norm")

    def col_blocks(g):
        kdim, n8 = g.shape
        return g.astype(BF16).reshape(kdim, N_DEV, n8 // N_DEV).transpose(1, 0, 2).reshape(N_DEV, -1, LANES)

    g_blocks = jnp.concatenate([col_blocks(_uq_from_kernel_layout(dw_uq_k)), col_blocks(_ukv_from_kernel_layout(dw_ukv_k)),
                                col_blocks(dw_a), col_blocks(dw_b), dw_out.astype(BF16).reshape(N_DEV, -1, LANES)], axis=1)
    (d_sbq, d_sbk, d_sbv), (g_recv,) = _sb_backward(proj, sb_tails, d_oa, s_len, _Exchange([g_blocks]))

    d_proj = jnp.concatenate([d_ga, d_gb, d_sbq.astype(BF16), d_sbk.astype(BF16), d_sbv.astype(BF16), d_sbz, d_mlaz,
                              d_cq, jnp.zeros((s_len, LANES), BF16), d_ckv, d_krot], axis=1)
    dw_in_kt = _mm(d_proj, h, ta=True, out_dtype=BF16, name="dw_in", tiles=(512, D, s_len))

    def krot_body(t_ref, o_ref):
        half = ROPE // 2
        for part in range(2):
            acc = t_ref[part * LANES:part * LANES + half, :].astype(F32)
            for hh in range(1, HEADS):
                acc = acc + t_ref[part * LANES + hh * half:part * LANES + (hh + 1) * half, :].astype(F32)
            o_ref[part * half:(part + 1) * half, :] = acc

    dw_krot = pl.pallas_call(krot_body, name="dw_krot_sum", out_shape=jax.ShapeDtypeStruct((ROPE, D), F32))(
        dw_in_kt[O_KROT:O_KROT + 2 * LANES])

    g_in_blocks = _w_in_t_from_kernel_layout(dw_in_kt, dw_krot).reshape(N_DEV, -1, D)
    dh, (g_in_recv,) = _mm(d_proj, w_in_kt, name="d_h", exchange=_Exchange([g_in_blocks]))

    def f_dx(x_, dh_, dx2_, g1_, msc, bsc):
        r = _rms(x_)
        xn = x_ * r
        dn1 = dh_ * (1.0 + (msc + bsc))
        return ((dx2_ + _rms_bwd(xn, r, dn1 * g1_),),
                (_colsum(dh_), _colsum(dh_ * (xn * g1_)), _colsum(dn1 * xn)))

    grad_x2d, d_shift, d_scale, d_g1 = _rowwise(f_dx, [x2d, dh, dx2], [g1, mod_scale, b_scale], [(D, F32)],
                                                [D, D, D], name="d_ada_norm")

    g_in_sum_t = _sum_blocks(g_in_recv, "sum_grads_w_in")
    g_sum = _sum_blocks(g_recv, "sum_grads")
    g_big = [g_sum[offs[t]:offs[t + 1]].reshape(big[t].shape) for t in range(5)]

    small = jnp.concatenate([d_shift, d_scale, d_gate, d_g1, d_gq, d_gkv, d_gf, loss_part], axis=1)
    n_small = small.shape[1]
    pad = (-n_small) % (8 * LANES)
    small = jnp.pad(small, ((0, 0), (0, pad))).reshape(-1, LANES)
    small_all = _all_gather(small, "gather_small")
    small_sum = _sum_blocks(small_all, "sum_small").reshape(1, -1)
    g_b_ada = small_sum[:, :3 * D]
    g_g1 = small_sum[:, 3 * D:4 * D]
    g_gq = small_sum[:, 4 * D:4 * D + Q_RANK]
    g_gkv = small_sum[:, 4 * D + Q_RANK:4 * D + Q_RANK + KV_RANK]
    g_gf = small_sum[:, 4 * D + Q_RANK + KV_RANK:4 * D + Q_RANK + KV_RANK + D]

    dmod_all = small_all.reshape(N_DEV, -1)[:, :3 * D]
    dmod_cols = lax.dynamic_slice_in_dim(dmod_all, me * (3 * D // N_DEV), 3 * D // N_DEV, axis=1)
    g_w_ada = _mm(c_all, dmod_cols, ta=True, name="dw_ada")

    loss = small_sum[0, n_small - LANES]

    names = ["w_ada", "b_ada", "norm_gain", "w_in", "q_norm_gain", "w_uq", "kv_norm_gain", "w_ukv",
             "w_branch_a", "w_branch_b", "w_out", "final_norm_gain"]
    weights = dict(w_ada=w_ada, b_ada=b_ada, norm_gain=norm_gain, w_in=w_in, q_norm_gain=q_norm_gain, w_uq=w_uq,
                   kv_norm_gain=kv_norm_gain, w_ukv=w_ukv, w_branch_a=w_branch_a, w_branch_b=w_branch_b, w_out=w_out,
                   final_norm_gain=final_norm_gain)
    moms = dict(w_ada=m_w_ada, b_ada=m_b_ada, norm_gain=m_norm_gain, w_in=m_w_in, q_norm_gain=m_q_norm_gain,
                w_uq=m_w_uq, kv_norm_gain=m_kv_norm_gain, w_ukv=m_w_ukv, w_branch_a=m_w_branch_a,
                w_branch_b=m_w_branch_b, w_out=m_w_out, final_norm_gain=m_final_norm_gain)
    vels = dict(w_ada=v_w_ada, b_ada=v_b_ada, norm_gain=v_norm_gain, w_in=v_w_in, q_norm_gain=v_q_norm_gain,
                w_uq=v_w_uq, kv_norm_gain=v_kv_norm_gain, w_ukv=v_w_ukv, w_branch_a=v_w_branch_a,
                w_branch_b=v_w_branch_b, w_out=v_w_out, final_norm_gain=v_final_norm_gain)
    grads2d = dict(w_ada=g_w_ada, b_ada=g_b_ada, norm_gain=g_g1, w_in=g_in_sum_t, q_norm_gain=g_gq, w_uq=g_big[0],
                   kv_norm_gain=g_gkv, w_ukv=g_big[1], w_branch_a=g_big[2], w_branch_b=g_big[3], w_out=g_big[4],
                   final_norm_gain=g_gf)

    grads, deltas, new_m, new_v = [], [], [], []
    for n in names:
        w = weights[n]
        if n == "w_in":
            to2d = lambda t: t[0].T
            back = lambda t: t.T[None]
        else:
            shape2d = grads2d[n].shape
            to2d = lambda t, s=shape2d: t.reshape(s)
            back = lambda t, s=w.shape: t.reshape(s)
        d_, m_, v_ = _adamw_call(to2d(w), grads2d[n], to2d(moms[n]), to2d(vels[n]), "adamw_" + n)
        grads.append(back(grads2d[n]))
        deltas.append(back(d_))
        new_m.append(back(m_))
        new_v.append(back(v_))

    return (loss, grad_x2d.reshape(x.shape), *grads, *deltas, *new_m, *new_v)
```

```python
import functools
import math

import jax
import jax.numpy as jnp
from jax import lax
from jax.experimental import pallas as pl
from jax.experimental.pallas import tpu as pltpu

F32 = jnp.float32
BF16 = jnp.bfloat16
MXU_DTYPE = jnp.bfloat16

N_DEV = 8
D = 1024
HEADS = 8
HD = 64
SB_W = 512
MLA_W = 512
Q_RANK = 384
KV_RANK = 256
ROPE = 32
NOPE = 64
QK_DIM = NOPE + ROPE
EPS = 1e-6
ROPE_BASE = 10000.0

ADAM_LR = 0.001
ADAM_B1 = 0.9
ADAM_B2 = 0.999
ADAM_EPS = 1e-08
ADAM_WD = 0.01
ADAM_STEP = 10

LANES = 128
VMEM_LIMIT = 48 * 1024 * 1024

O_GA, O_GB = 0, 1024
O_SBQ, O_SBK, O_SBV, O_SBZ = 2048, 2560, 3072, 3584
O_MLAZ = 4096
O_CQ = 4608
O_CKV = 5120
O_KROT = 5376
IN_PAD = 5632

BQ = 256
BK = 256


def _cparams(*sem):
    return pltpu.CompilerParams(dimension_semantics=sem, vmem_limit_bytes=VMEM_LIMIT)


def _tile_of(n, cap=512):
    if n <= cap:
        return n
    for t in (1024, 768, 512, 384, 256, 128):
        if t <= cap and n % t == 0:
            return t
    raise ValueError(n)


def _rowwise(fn, rows, vecs, outs, reds=(), *, name, tile=512):
    norm = []
    for r in rows:
        if isinstance(r, tuple):
            arr, cb, w = r[:3]
            ro = r[3] if len(r) > 3 else 0
        else:
            arr, cb, w, ro = r, 0, r.shape[1], 0
        norm.append((arr, cb, w, ro))
    s_len = norm[0][0].shape[0]
    tile = min(tile, s_len)
    assert s_len % tile == 0
    n_row, n_vec, n_out, n_red = len(norm), len(vecs), len(outs), len(reds)

    def body(*refs):
        step = pl.program_id(0)
        row_refs = refs[:n_row]
        vec_refs = refs[n_row:n_row + n_vec]
        out_refs = refs[n_row + n_vec:n_row + n_vec + n_out]
        red_refs = refs[n_row + n_vec + n_out:]
        row_res, red_res = fn(*[r[...] for r in row_refs], *[v[...] for v in vec_refs])
        for o, val in zip(out_refs, row_res):
            o[...] = val.astype(o.dtype)
        if n_red:
            @pl.when(step == 0)
            def _():
                for r in red_refs:
                    r[...] = jnp.zeros(r.shape, r.dtype)
            for r, val in zip(red_refs, red_res):
                r[...] += val

    in_specs = []
    for arr, cb, w, ro in norm:
        in_specs.append(pl.BlockSpec((tile, w), functools.partial(lambda i, cb, rb: (i + rb, cb), cb=cb, rb=ro // tile)))
        assert ro % tile == 0
    for v in vecs:
        in_specs.append(pl.BlockSpec(v.shape, lambda i: (0, 0)))
    out_shape = [jax.ShapeDtypeStruct((s_len, w), dt) for w, dt in outs]
    out_specs = [pl.BlockSpec((tile, w), lambda i: (i, 0)) for w, _ in outs]
    out_shape += [jax.ShapeDtypeStruct((1, w), F32) for w in reds]
    out_specs += [pl.BlockSpec((1, w), lambda i: (0, 0)) for w in reds]
    res = pl.pallas_call(
        body, name=name, grid=(s_len // tile,), in_specs=in_specs, out_specs=out_specs, out_shape=out_shape,
        compiler_params=_cparams("arbitrary" if n_red else "parallel"),
    )(*[a for a, _, _, _ in norm], *vecs)
    return res


def _mm(a, b, *, ta=False, tb=False, out_dtype=F32, name, exchange=None, tiles=None, epilogue=None):
    m, k = (a.shape[1], a.shape[0]) if ta else a.shape
    n = b.shape[0] if tb else b.shape[1]
    assert (b.shape[1] if tb else b.shape[0]) == k
    tm, tn, tk = tiles or (_tile_of(m, 1024), _tile_of(n, 1024 if n <= 1024 else 512), _tile_of(k, 1024))
    assert m % tm == 0 and n % tn == 0 and k % tk == 0
    ni, nj, nk = m // tm, n // tn, k // tk
    dims = (((0 if ta else 1,), (1 if tb else 0,)), ((), ()))
    n_ex = len(exchange.arrays) if exchange else 0
    fn, rows, vecs, outs, reds = epilogue or (None, (), (), (), ())
    rows = [r if isinstance(r, tuple) else (r, 0, r.shape[1]) for r in rows]
    assert not epilogue or tn == n
    n_res = len(outs) + len(reds) if epilogue else 1

    def body(*refs):
        a_ref, b_ref = refs[:2]
        row_refs, refs = refs[2:2 + len(rows)], refs[2 + len(rows):]
        vec_refs, refs = refs[:len(vecs)], refs[len(vecs):]
        x_refs, refs = refs[:n_ex], refs[n_ex:]
        res_refs, refs = refs[:n_res], refs[n_res:]
        out_refs, acc_ref, sems = refs[:n_ex], refs[n_ex], refs[n_ex + 1:]
        i, j, kk = pl.program_id(0), pl.program_id(1), pl.program_id(2)
        first = (i == 0) & (j == 0) & (kk == 0)

        if exchange:
            @pl.when(first)
            def _():
                exchange.start(x_refs, out_refs, sems)

        if reds:
            @pl.when(first)
            def _():
                for r in res_refs[len(outs):]:
                    r[...] = jnp.zeros(r.shape, r.dtype)

        @pl.when(kk == 0)
        def _():
            acc_ref[...] = jnp.zeros(acc_ref.shape, F32)

        acc_ref[...] += lax.dot_general(a_ref[...].astype(MXU_DTYPE), b_ref[...].astype(MXU_DTYPE), dims,
                                        preferred_element_type=F32)

        @pl.when(kk == nk - 1)
        def _():
            if not epilogue:
                res_refs[0][...] = acc_ref[...].astype(res_refs[0].dtype)
                return
            row_res, red_res = fn(acc_ref[...], *[r[...] for r in row_refs], *[v[...] for v in vec_refs])
            for o, val in zip(res_refs, row_res):
                o[...] = val.astype(o.dtype)
            for r, val in zip(res_refs[len(outs):], red_res):
                r[...] += val

        if exchange:
            @pl.when((i == ni - 1) & (j == nj - 1) & (kk == nk - 1))
            def _():
                exchange.wait(x_refs, out_refs, sems)

    a_spec = pl.BlockSpec((tk, tm), lambda i, j, kk: (kk, i)) if ta else pl.BlockSpec((tm, tk), lambda i, j, kk: (i, kk))
    b_spec = pl.BlockSpec((tn, tk), lambda i, j, kk: (j, kk)) if tb else pl.BlockSpec((tk, tn), lambda i, j, kk: (kk, j))
    in_specs = [a_spec, b_spec]
    in_specs += [pl.BlockSpec((tm, w), functools.partial(lambda i, j, kk, cb: (i, cb), cb=cb)) for _, cb, w in rows]
    in_specs += [pl.BlockSpec(v.shape, lambda i, j, kk: (0, 0)) for v in vecs]
    if epilogue:
        res_specs = [pl.BlockSpec((tm, w), lambda i, j, kk: (i, 0)) for w, _ in outs]
        res_specs += [pl.BlockSpec((1, w), lambda i, j, kk: (0, 0)) for w in reds]
        res_shape = [jax.ShapeDtypeStruct((m, w), dt) for w, dt in outs] + [jax.ShapeDtypeStruct((1, w), F32) for w in reds]
    else:
        res_specs = [pl.BlockSpec((tm, tn), lambda i, j, kk: (i, j))]
        res_shape = [jax.ShapeDtypeStruct((m, n), out_dtype)]
    ordered = bool(exchange or reds)
    res = pl.pallas_call(
        body, name=name, grid=(ni, nj, nk),
        in_specs=in_specs + (exchange.in_specs if exchange else []),
        out_specs=res_specs + (exchange.out_specs if exchange else []),
        out_shape=res_shape + (exchange.out_shape if exchange else []),
        scratch_shapes=[pltpu.VMEM((tm, tn), F32)] + (exchange.scratch if exchange else []),
        compiler_params=_cparams(*(("arbitrary",) * 3 if ordered else ("parallel", "parallel", "arbitrary"))),
    )(a, b, *[r[0] for r in rows], *vecs, *(exchange.arrays if exchange else []))
    main = res[:n_res] if epilogue else res[0]
    return (main, res[n_res:]) if exchange else main


_NT = (((1,), (1,)), ((), ()))
_TN = (((0,), (0,)), ((), ()))


def _dot(a, b):
    return jnp.dot(a, b, preferred_element_type=F32)


def _dot_nt(a, b):
    return lax.dot_general(a, b, _NT, preferred_element_type=F32)


def _dot_tn(a, b):
    return lax.dot_general(a, b, _TN, preferred_element_type=F32)


def _running_sum(x, tri):
    return _dot(x.astype(MXU_DTYPE), tri)


def _neg_softplus(z):
    u = jnp.exp2(jnp.abs(z) * (-1.0 / math.log(2.0)))
    return -jnp.maximum(z, 0.0) - jnp.log(1.0 + u)


def _walk_blocks(step, st, n, descending, group=2):
    done = 0
    size = group
    while size >= 1:
        def trip(t, s, size=size, done=done):
            js = [done + size * t + g for g in range(size)]
            return step([n - 1 - j for j in js] if descending else js, s)

        trips = (n - done) // size
        st = lax.fori_loop(0, trips, trip, st)
        done = done + size * trips
        size //= 2
    return st


def _chains(js):
    return [(h, t) for t in range(len(js)) for h in range(2)]


def _rowsum(x):
    return jnp.sum(x, axis=1, keepdims=True)


def _attn_consts():
    row = lax.broadcasted_iota(jnp.int32, (BQ, BK), 0)
    col = lax.broadcasted_iota(jnp.int32, (BQ, BK), 1)
    trow = lax.broadcasted_iota(jnp.int32, (BK, BK), 0)
    tcol = lax.broadcasted_iota(jnp.int32, (BK, BK), 1)
    lane = lax.broadcasted_iota(jnp.int32, (BQ, LANES), 1)
    klane = lax.broadcasted_iota(jnp.int32, (BK, LANES), 1)
    return row, col, trow, tcol, lane, klane


DIAG = BQ // BK
assert BQ == DIAG * BK


def _diag_block(i):
    return i * DIAG


def _diag_blocks(jd, descending):
    places = list(reversed(range(DIAG))) if descending else list(range(DIAG))
    return [jd + g for g in places], places


def _key_slice(j):
    return pl.ds(pl.multiple_of(j * BK, BK), BK)


def _sb_forward(proj, s_len):
    nq = s_len // BQ
    assert s_len // BK <= HD
    qc, kc, vc = O_SBQ // LANES, O_SBK // LANES, O_SBV // LANES

    def body(q_ref, k_ref, v_ref, o_ref, tails_ref):
        i = pl.program_id(1)
        jd = _diag_block(i)
        row, col, trow, tcol, lane, klane = _attn_consts()
        strict = [col + g * BK < row for g in range(DIAG)]
        tri = (trow >= tcol).astype(MXU_DTYPE)
        q = q_ref[...] * 0.125
        qh = [jnp.where(lane < HD, q, 0.0).astype(MXU_DTYPE), jnp.where(lane >= HD, q, 0.0).astype(MXU_DTYPE)]

        km = [klane < HD, klane >= HD]

        def step(js, st, diag):
            carry, acc, tail = st
            chains = _chains(js)
            kj = [k_ref[_key_slice(j), :].astype(MXU_DTYPE) for j in js]
            vj = [v_ref[_key_slice(j), :].astype(MXU_DTYPE) for j in js]
            z = {(h, t): _dot_nt(qh[h], kj[t]) for h, t in chains}
            run = list(carry)
            suf, carry_in = {}, {}
            for h, t in chains:
                lom = _neg_softplus(z[h, t])
                if diag:
                    lom = jnp.where(strict[diag[t]], lom, 0.0)
                suf[h, t] = _running_sum(lom, tri)
                carry_in[h, t] = run[h]
                run[h] = run[h] + _rowsum(lom)
            for h, t in chains:
                a = jnp.exp(z[h, t] + suf[h, t] + carry_in[h, t])
                if diag:
                    a = jnp.where(strict[diag[t]], a, 0.0)
                acc = acc + _dot(a.astype(MXU_DTYPE), jnp.where(km[h], vj[t], 0))
                tail = jnp.where(lane == h * HD + js[t], carry_in[h, t], tail)
            return tuple(run), acc, tail

        zero = jnp.zeros((BQ, LANES), F32)
        diag_js, places = _diag_blocks(jd, True)
        st = step(diag_js, ((jnp.zeros((BQ, 1), F32),) * 2, zero, zero), places)
        st = _walk_blocks(lambda js, s: step(js, s, False), st, jd, True, group=8)
        o_ref[...] = st[1]
        tails_ref[...] = st[2]

    blk = pl.BlockSpec((BQ, LANES), lambda p, i: (i, p))
    out = jax.ShapeDtypeStruct((s_len, SB_W), F32)
    return pl.pallas_call(
        body, name="sb_fwd", grid=(HEADS // 2, nq),
        in_specs=[pl.BlockSpec((BQ, LANES), lambda p, i: (i, qc + p)),
                  pl.BlockSpec((s_len, LANES), lambda p, i: (0, kc + p)),
                  pl.BlockSpec((s_len, LANES), lambda p, i: (0, vc + p))],
        out_specs=[blk, blk], out_shape=[out, out],
        compiler_params=_cparams("parallel", "parallel"),
    )(proj, proj, proj)


def _sb_backward(proj, tails, do, s_len, exchange):
    nq = s_len // BQ
    qc, kc, vc = O_SBQ // LANES, O_SBK // LANES, O_SBV // LANES
    n_ex = len(exchange.arrays)

    def body(q_ref, k_ref, v_ref, tails_ref, do_ref, *rest):
        x_refs, (dq_ref, dk_ref, dv_ref) = rest[:n_ex], rest[n_ex:n_ex + 3]
        out_refs, sems = rest[n_ex + 3:2 * n_ex + 3], rest[2 * n_ex + 3:]
        p = pl.program_id(0)
        i = pl.program_id(1)
        jd = _diag_block(i)

        @pl.when((p == 0) & (i == 0))
        def _():
            exchange.start(x_refs, out_refs, sems)

        @pl.when(i == 0)
        def _():
            dk_ref[...] = jnp.zeros(dk_ref.shape, F32)
            dv_ref[...] = jnp.zeros(dv_ref.shape, F32)

        row, col, trow, tcol, lane, klane = _attn_consts()
        strict = [col + g * BK < row for g in range(DIAG)]
        tri = (trow >= tcol).astype(MXU_DTYPE)
        tri_p = (trow <= tcol).astype(MXU_DTYPE)
        q = q_ref[...] * 0.125
        tails_blk = tails_ref[...]
        do_blk = do_ref[...]
        hm = [lane < HD, lane >= HD]
        km = [klane < HD, klane >= HD]
        qh = [jnp.where(m, q, 0.0).astype(MXU_DTYPE) for m in hm]
        doh = [jnp.where(m, do_blk, 0.0).astype(MXU_DTYPE) for m in hm]

        def step(js, st, diag):
            before, dq = st
            chains = _chains(js)
            kj = [k_ref[_key_slice(j), :].astype(MXU_DTYPE) for j in js]
            vj = [v_ref[_key_slice(j), :].astype(MXU_DTYPE) for j in js]
            z = {(h, t): _dot_nt(qh[h], kj[t]) for h, t in chains}
            da = {(h, t): _dot_nt(doh[h], vj[t]) for h, t in chains}
            suf, sig = {}, {}
            for h, t in chains:
                lom = _neg_softplus(z[h, t])
                if diag:
                    lom = jnp.where(strict[diag[t]], lom, 0.0)
                suf[h, t] = _running_sum(lom, tri)
                sig[h, t] = jnp.exp(z[h, t] + lom)
            run = list(before)
            dl, pre, before_in = {}, {}, {}
            dk_add, dv_add = [None] * len(js), [None] * len(js)
            for h, t in chains:
                tail = _rowsum(jnp.where(lane == h * HD + js[t], tails_blk, 0.0))
                a = jnp.exp(z[h, t] + suf[h, t] + tail)
                if diag:
                    a = jnp.where(strict[diag[t]], a, 0.0)
                dl[h, t] = da[h, t] * a
                pre[h, t] = _dot(dl[h, t].astype(MXU_DTYPE), tri_p)
                dv_h = _dot_tn(a.astype(MXU_DTYPE), doh[h])
                dv_add[t] = dv_h if dv_add[t] is None else dv_add[t] + dv_h
                before_in[h, t] = run[h]
                run[h] = run[h] + _rowsum(dl[h, t])
            for h, t in chains:
                upto = before_in[h, t] + pre[h, t]
                dz = dl[h, t] - sig[h, t] * upto
                if diag:
                    dz = jnp.where(strict[diag[t]], dz, 0.0)
                dzb = dz.astype(MXU_DTYPE)
                dq = dq + _dot(dzb, jnp.where(km[h], kj[t], 0))
                dk_h = _dot_tn(dzb, qh[h])
                dk_add[t] = dk_h if dk_add[t] is None else dk_add[t] + dk_h
            for t, j in enumerate(js):
                dk_ref[_key_slice(j), :] += dk_add[t]
                dv_ref[_key_slice(j), :] += dv_add[t]
            return tuple(run), dq

        st = ((jnp.zeros((BQ, 1), F32),) * 2, jnp.zeros((BQ, LANES), F32))
        st = _walk_blocks(lambda js, s: step(js, s, False), st, jd, False, group=4)
        diag_js, places = _diag_blocks(jd, False)
        st = step(diag_js, st, places)
        dq_ref[...] = st[1] * 0.125

        @pl.when((p == HEADS // 2 - 1) & (i == nq - 1))
        def _():
            exchange.wait(x_refs, out_refs, sems)

    blk = pl.BlockSpec((BQ, LANES), lambda p, i: (i, p))
    full = pl.BlockSpec((s_len, LANES), lambda p, i: (0, p))
    out = jax.ShapeDtypeStruct((s_len, SB_W), F32)
    res = pl.pallas_call(
        body, name="sb_bwd", grid=(HEADS // 2, nq),
        in_specs=[pl.BlockSpec((BQ, LANES), lambda p, i: (i, qc + p)),
                  pl.BlockSpec((s_len, LANES), lambda p, i: (0, kc + p)),
                  pl.BlockSpec((s_len, LANES), lambda p, i: (0, vc + p)),
                  blk, blk] + exchange.in_specs,
        out_specs=[blk, full, full] + exchange.out_specs, out_shape=[out, out, out] + exchange.out_shape,
        scratch_shapes=exchange.scratch,
        compiler_params=_cparams("arbitrary", "arbitrary"),
    )(proj, proj, proj, tails, do, *exchange.arrays)
    return res[:3], res[3:]


def _pair_mask(rows, h):
    lane = lax.broadcasted_iota(jnp.int32, (rows, 2 * LANES), 1)
    rot = lane - LANES
    return (((lane < LANES) & (lane // HD == h))
            | ((lane >= LANES) & (rot < 2 * ROPE) & ((rot // (ROPE // 2)) % 2 == h)))


def _mla_forward(q_cat, k_cat, kv, s_len):
    nq = s_len // BQ
    scale = 1.0 / math.sqrt(QK_DIM)

    def body(q_ref, k_ref, v_ref, o_ref, lse_ref):
        i = pl.program_id(1)
        jd = _diag_block(i)
        row, col, trow, tcol, lane, klane = _attn_consts()
        causal = [col + g * BK <= row for g in range(DIAG)]
        q = q_ref[...]
        hm = [lane < HD, lane >= HD]
        km = [klane < HD, klane >= HD]
        qh = [jnp.where(_pair_mask(BQ, h), q, 0) for h in range(2)]

        def step(js, st, diag):
            m_run, l_run, acc = st
            chains = _chains(js)
            kj = [k_ref[_key_slice(j), :] for j in js]
            vj = [v_ref[_key_slice(j), :].astype(MXU_DTYPE) for j in js]
            s = {}
            for h, t in chains:
                s[h, t] = _dot_nt(qh[h], kj[t]) * scale
                if diag:
                    s[h, t] = jnp.where(causal[diag[t]], s[h, t], -jnp.inf)
            m_new, alpha, l_new = [], [], []
            for h in range(2):
                top = m_run[h]
                for t in range(len(js)):
                    top = jnp.maximum(top, jnp.max(s[h, t], axis=1, keepdims=True))
                m_new.append(top)
                alpha.append(jnp.exp(m_run[h] - top))
                l_new.append(alpha[h] * l_run[h])
            add = None
            for h, t in chains:
                pr = jnp.exp(s[h, t] - m_new[h])
                l_new[h] = l_new[h] + _rowsum(pr)
                part = _dot(pr.astype(MXU_DTYPE), jnp.where(km[h], vj[t], 0))
                add = part if add is None else add + part
            acc = jnp.where(hm[0], alpha[0], alpha[1]) * acc + add
            return tuple(m_new), tuple(l_new), acc

        st = ((jnp.full((BQ, 1), -1e30, F32),) * 2, (jnp.zeros((BQ, 1), F32),) * 2, jnp.zeros((BQ, LANES), F32))
        diag_js, places = _diag_blocks(jd, True)
        st = step(diag_js, st, places)
        m_run, l_run, acc = _walk_blocks(lambda js, s: step(js, s, False), st, jd, True, group=8)
        o_ref[...] = acc / jnp.where(hm[0], l_run[0], l_run[1])
        lse_ref[...] = jnp.where(hm[0], m_run[0] + jnp.log(l_run[0]), m_run[1] + jnp.log(l_run[1]))

    blk = pl.BlockSpec((BQ, LANES), lambda p, i: (i, p))
    out = jax.ShapeDtypeStruct((s_len, MLA_W), F32)
    return pl.pallas_call(
        body, name="mla_fwd", grid=(HEADS // 2, nq),
        in_specs=[pl.BlockSpec((BQ, 2 * LANES), lambda p, i: (i, p)),
                  pl.BlockSpec((s_len, 2 * LANES), lambda p, i: (0, p)),
                  pl.BlockSpec((s_len, LANES), lambda p, i: (0, MLA_W // LANES + p))],
        out_specs=[blk, blk], out_shape=[out, out],
        compiler_params=_cparams("parallel", "parallel"),
    )(q_cat, k_cat, kv)


def _mla_backward(q_cat, k_cat, kv, o, lse, do, s_len):
    nq = s_len // BQ
    scale = 1.0 / math.sqrt(QK_DIM)

    def body(q_ref, k_ref, v_ref, o_ref, lse_ref, do_ref, dq_ref, dk_ref, dv_ref):
        i = pl.program_id(1)

        @pl.when(i == 0)
        def _():
            dk_ref[...] = jnp.zeros(dk_ref.shape, F32)
            dv_ref[...] = jnp.zeros(dv_ref.shape, F32)

        jd = _diag_block(i)
        row, col, trow, tcol, lane, klane = _attn_consts()
        causal = [col + g * BK <= row for g in range(DIAG)]
        q = q_ref[...]
        o_blk = o_ref[...]
        do_blk = do_ref[...]
        lse_blk = lse_ref[...]
        hm = [lane < HD, lane >= HD]
        kpm = [_pair_mask(BK, h) for h in range(2)]
        qh = [jnp.where(_pair_mask(BQ, h), q, 0) for h in range(2)]
        doh_f = [jnp.where(m, do_blk, 0.0) for m in hm]
        doh = [d.astype(MXU_DTYPE) for d in doh_f]
        delta = [jnp.sum(d * o_blk, axis=1, keepdims=True) for d in doh_f]
        lse_h = [jnp.sum(jnp.where(lane == h * HD, lse_blk, 0.0), axis=1, keepdims=True) for h in range(2)]

        def step(js, st, diag):
            dq = st
            chains = _chains(js)
            kj = [k_ref[_key_slice(j), :] for j in js]
            vj = [v_ref[_key_slice(j), :].astype(MXU_DTYPE) for j in js]
            s = {(h, t): _dot_nt(qh[h], kj[t]) for h, t in chains}
            dp = {(h, t): _dot_nt(doh[h], vj[t]) for h, t in chains}
            adds = [[None] * len(js) for _ in range(2)]

            def accumulate(slot, t, part):
                adds[slot][t] = part if adds[slot][t] is None else adds[slot][t] + part

            for h, t in chains:
                pr = jnp.exp(s[h, t] * scale - lse_h[h])
                if diag:
                    pr = jnp.where(causal[diag[t]], pr, 0.0)
                dsb = (pr * (dp[h, t] - delta[h]) * scale).astype(MXU_DTYPE)
                dq = dq + _dot(dsb, jnp.where(kpm[h], kj[t], 0))
                accumulate(0, t, _dot_tn(dsb, qh[h]))
                accumulate(1, t, _dot_tn(pr.astype(MXU_DTYPE), doh[h]))
            for t, j in enumerate(js):
                dk_ref[_key_slice(j), :] += adds[0][t]
                dv_ref[_key_slice(j), :] += adds[1][t]
            return dq

        diag_js, places = _diag_blocks(jd, True)
        st = step(diag_js, jnp.zeros((BQ, 2 * LANES), F32), places)
        dq_ref[...] = _walk_blocks(lambda js, s: step(js, s, False), st, jd, True, group=4)

    blk = pl.BlockSpec((BQ, LANES), lambda p, i: (i, p))
    full = pl.BlockSpec((s_len, LANES), lambda p, i: (0, p))
    out = jax.ShapeDtypeStruct((s_len, MLA_W), F32)
    out_cat = jax.ShapeDtypeStruct((s_len, 2 * MLA_W), F32)
    return pl.pallas_call(
        body, name="mla_bwd", grid=(HEADS // 2, nq),
        in_specs=[pl.BlockSpec((BQ, 2 * LANES), lambda p, i: (i, p)),
                  pl.BlockSpec((s_len, 2 * LANES), lambda p, i: (0, p)),
                  pl.BlockSpec((s_len, LANES), lambda p, i: (0, MLA_W // LANES + p)),
                  blk, blk, blk],
        out_specs=[pl.BlockSpec((BQ, 2 * LANES), lambda p, i: (i, p)),
                   pl.BlockSpec((s_len, 2 * LANES), lambda p, i: (0, p)), full],
        out_shape=[out_cat, out_cat, out],
        compiler_params=_cparams("arbitrary", "arbitrary"),
    )(q_cat, k_cat, kv, o, lse, do)


def _mesh_pos():
    return lax.axis_index("x"), lax.axis_index("y"), lax.axis_index("c")


def _dev_index(px, py, pc):
    return 4 * px + 2 * py + pc


def _all_gather(block, name):
    return _all_gather_parts([block], name)[0]


def _all_gather_parts(blocks, name):
    n = len(blocks)

    def body(*refs):
        x_refs, out_refs = refs[:n], refs[n:2 * n]
        send_sems, recv_sems, local_sems = refs[2 * n:]
        x, y, c = _mesh_pos()
        me, sibling = (x, y, c), (x, y, 1 - c)
        chips = [(1 - x, y), (x, 1 - y), (1 - x, 1 - y)]

        def copy(a, k, blockpos, to, src=None):
            slot = out_refs[a].at[_dev_index(*blockpos)]
            return pltpu.make_async_remote_copy(
                src_ref=slot if src is None else src, dst_ref=slot,
                send_sem=send_sems.at[7 * a + k], recv_sem=recv_sems.at[7 * a + k],
                device_id=to, device_id_type=pl.DeviceIdType.MESH)

        mine = [pltpu.make_async_copy(x_refs[a], out_refs[a].at[_dev_index(*me)], local_sems.at[a]) for a in range(n)]
        for cp in mine:
            cp.start()
        first = []
        for a in range(n):
            first.append(copy(a, 0, me, sibling, src=x_refs[a]))
            first += [copy(a, 1 + j, me, (*chip, c), src=x_refs[a]) for j, chip in enumerate(chips)]
        for cp in first:
            cp.start()
        passed = []
        for j, chip in enumerate(chips):
            for a in range(n):
                copy(a, 1 + j, (*chip, c), me).wait_recv()
                passed.append(copy(a, 4 + j, (*chip, c), sibling))
                passed[-1].start()
        for a in range(n):
            copy(a, 0, sibling, me).wait_recv()
            for j, chip in enumerate(chips):
                copy(a, 4 + j, (*chip, 1 - c), me).wait_recv()
        for cp in first + passed:
            cp.wait_send()
        for cp in mine:
            cp.wait()

    return pl.pallas_call(
        body, name=name,
        out_shape=[jax.ShapeDtypeStruct((N_DEV,) + b.shape, b.dtype) for b in blocks],
        in_specs=[pl.BlockSpec(memory_space=pl.ANY)] * n, out_specs=[pl.BlockSpec(memory_space=pl.ANY)] * n,
        scratch_shapes=[pltpu.SemaphoreType.DMA((7 * n,)), pltpu.SemaphoreType.DMA((7 * n,)),
                        pltpu.SemaphoreType.DMA((n,))],
    )(*blocks)


class _Exchange:
    def __init__(self, arrays):
        self.arrays = list(arrays)
        n = len(self.arrays)
        self.in_specs = [pl.BlockSpec(memory_space=pl.ANY)] * n
        self.out_specs = [pl.BlockSpec(memory_space=pl.ANY)] * n
        self.out_shape = [jax.ShapeDtypeStruct(a.shape, a.dtype) for a in self.arrays]
        self.scratch = [pltpu.SemaphoreType.DMA((7 * n,)), pltpu.SemaphoreType.DMA((7 * n,)),
                        pltpu.SemaphoreType.DMA((n,))]

    def _copies(self, x_refs, out_refs, sems, with_arrivals):
        send_sems, recv_sems, local_sems = sems
        x, y, c = _mesh_pos()
        me = _dev_index(x, y, c)
        flips = [(fx, fy, fc) for fx in (0, 1) for fy in (0, 1) for fc in (0, 1)][1:]
        peers = [(1 - x if fx else x, 1 - y if fy else y, 1 - c if fc else c) for fx, fy, fc in flips]
        mine, sends, arrivals = [], [], []
        for a in range(len(self.arrays)):
            mine.append(pltpu.make_async_copy(x_refs[a].at[me], out_refs[a].at[me], local_sems.at[a]))
            for k, peer in enumerate(peers):
                sends.append(pltpu.make_async_remote_copy(
                    src_ref=x_refs[a].at[_dev_index(*peer)], dst_ref=out_refs[a].at[me],
                    send_sem=send_sems.at[7 * a + k], recv_sem=recv_sems.at[7 * a + k],
                    device_id=peer, device_id_type=pl.DeviceIdType.MESH))
                if not with_arrivals:
                    continue
                arrivals.append(pltpu.make_async_remote_copy(
                    src_ref=x_refs[a].at[me], dst_ref=out_refs[a].at[_dev_index(*peer)],
                    send_sem=send_sems.at[7 * a + k], recv_sem=recv_sems.at[7 * a + k],
                    device_id=peer, device_id_type=pl.DeviceIdType.MESH))
        return mine, sends, arrivals

    def start(self, x_refs, out_refs, sems):
        mine, sends, _ = self._copies(x_refs, out_refs, sems, False)
        for cp in mine + sends:
            cp.start()

    def wait(self, x_refs, out_refs, sems):
        mine, sends, arrivals = self._copies(x_refs, out_refs, sems, True)
        for cp in arrivals:
            cp.wait_recv()
        for cp in sends:
            cp.wait_send()
        for cp in mine:
            cp.wait()


def _sum_blocks(parts, name):
    n, r, c = parts.shape
    row_tiles = [t for t in range(16, min(r, 2048) + 1, 16) if r % t == 0]
    if row_tiles:
        tr, tc = max(row_tiles), c
    else:
        tr, tc = r, 2 * LANES
    assert c % tc == 0

    def body(p_ref, o_ref):
        acc = p_ref[0].astype(F32)
        for s in range(1, n):
            acc = acc + p_ref[s].astype(F32)
        o_ref[...] = acc

    return pl.pallas_call(
        body, name=name, grid=(r // tr, c // tc),
        in_specs=[pl.BlockSpec((n, tr, tc), lambda i, j: (0, i, j))],
        out_specs=pl.BlockSpec((tr, tc), lambda i, j: (i, j)),
        out_shape=jax.ShapeDtypeStruct((r, c), F32),
        compiler_params=_cparams("parallel", "parallel"),
    )(parts)


def _sigmoid(x):
    return 1.0 / (1.0 + jnp.exp(-x))


def _silu(x):
    return x * _sigmoid(x)


def _silu_grad(x):
    s = _sigmoid(x)
    return s * (1.0 + x * (1.0 - s))


def _colsum(x):
    return jnp.sum(x, axis=0, keepdims=True)


def _rms(x):
    return lax.rsqrt(jnp.mean(x * x, axis=-1, keepdims=True) + EPS)


def _rms_bwd(xn, r, dxn):
    return r * (dxn - xn * jnp.mean(dxn * xn, axis=-1, keepdims=True))


def _adamw(w, g, m, v):
    m = ADAM_B1 * m + (1.0 - ADAM_B1) * g
    v = ADAM_B2 * v + (1.0 - ADAM_B2) * jnp.square(g)
    m_hat = m / (1.0 - ADAM_B1 ** ADAM_STEP)
    v_hat = v / (1.0 - ADAM_B2 ** ADAM_STEP)
    delta = -ADAM_LR * (m_hat / (jnp.sqrt(v_hat) + ADAM_EPS) + ADAM_WD * w)
    return delta, m, v


def _adamw_call(w, g, m, v, name):
    r, c = w.shape
    if r % 256 == 0:
        tr, tc = 256, c
    elif r * c <= 256 * 1024 or c % (2 * LANES):
        tr, tc = r, c
    else:
        tr, tc = r, 2 * LANES

    def body(w_ref, g_ref, m_ref, v_ref, d_out, m_out, v_out):
        d_out[...], m_out[...], v_out[...] = _adamw(w_ref[...], g_ref[...], m_ref[...], v_ref[...])

    spec = pl.BlockSpec((tr, tc), lambda i, j: (i, j))
    return pl.pallas_call(
        body, name=name, grid=(r // tr, c // tc), in_specs=[spec] * 4, out_specs=[spec] * 3,
        out_shape=[jax.ShapeDtypeStruct((r, c), F32)] * 3, compiler_params=_cparams("parallel", "parallel"),
    )(w, g, m, v)


def _uq_to_kernel_layout(w):
    lead = w.shape[:-1]
    t = w.reshape(lead + (HEADS, QK_DIM))
    return jnp.concatenate([t[..., :NOPE].reshape(lead + (HEADS * NOPE,)),
                            t[..., NOPE:NOPE + ROPE // 2].reshape(lead + (LANES,)),
                            t[..., NOPE + ROPE // 2:].reshape(lead + (LANES,))], axis=-1)


def _uq_from_kernel_layout(w):
    lead = w.shape[:-1]
    nope = w[..., :HEADS * NOPE].reshape(lead + (HEADS, NOPE))
    r1 = w[..., HEADS * NOPE:HEADS * NOPE + LANES].reshape(lead + (HEADS, ROPE // 2))
    r2 = w[..., HEADS * NOPE + LANES:].reshape(lead + (HEADS, ROPE // 2))
    return jnp.concatenate([nope, r1, r2], axis=-1).reshape(lead + (HEADS * QK_DIM,))


def _ukv_to_kernel_layout(w):
    lead = w.shape[:-1]
    t = w.reshape(lead + (HEADS, NOPE + HD))
    return jnp.concatenate([t[..., :NOPE].reshape(lead + (HEADS * NOPE,)),
                            t[..., NOPE:].reshape(lead + (HEADS * HD,))], axis=-1)


def _ukv_from_kernel_layout(w):
    lead = w.shape[:-1]
    kn = w[..., :HEADS * NOPE].reshape(lead + (HEADS, NOPE))
    vv = w[..., HEADS * NOPE:].reshape(lead + (HEADS, HD))
    return jnp.concatenate([kn, vv], axis=-1).reshape(lead + (HEADS * (NOPE + HD),))


def _w_in_t_to_kernel_layout(wt):
    sb = wt[0:2048]
    c_q = wt[2048:2432]
    c_kv = wt[2432:2688]
    k_rot = wt[2688:2720]
    mla_z = wt[2720:3232]
    gates = wt[3232:5280]
    zeros = jnp.zeros((LANES, wt.shape[1]), wt.dtype)
    k1 = jnp.tile(k_rot[:ROPE // 2], (HEADS, 1))
    k2 = jnp.tile(k_rot[ROPE // 2:], (HEADS, 1))
    return jnp.concatenate([gates, sb, mla_z, c_q, zeros, c_kv, k1, k2], axis=0)


def _w_in_t_from_kernel_layout(gt, g_rot):
    return jnp.concatenate([gt[O_SBQ:O_SBQ + 2048], gt[O_CQ:O_CQ + Q_RANK], gt[O_CKV:O_CKV + KV_RANK],
                            g_rot.astype(gt.dtype), gt[O_MLAZ:O_MLAZ + MLA_W], gt[O_GA:O_GA + 2 * D]], axis=0)


def kernel(x, c, positions, w_ada, b_ada, norm_gain, w_in, q_norm_gain, w_uq, kv_norm_gain, w_ukv, w_branch_a, w_branch_b, w_out, final_norm_gain, loss_target, m_w_ada, m_b_ada, m_norm_gain, m_w_in, m_q_norm_gain, m_w_uq, m_kv_norm_gain, m_w_ukv, m_w_branch_a, m_w_branch_b, m_w_out, m_final_norm_gain, v_w_ada, v_b_ada, v_norm_gain, v_w_in, v_q_norm_gain, v_w_uq, v_kv_norm_gain, v_w_ukv, v_w_branch_a, v_w_branch_b, v_w_out, v_final_norm_gain):
    s_len = x.shape[1]
    me = _dev_index(*_mesh_pos())
    x2d = x[0]
    tgt = loss_target[0]

    w_in_t = w_in[0].T.astype(BF16)
    big = [w_uq[0], w_ukv[0], w_branch_a[0], w_branch_b[0], w_out[0]]
    big_sizes = [int(w.size) for w in big]
    packed = jnp.concatenate([w.astype(BF16).reshape(-1, LANES) for w in big], axis=0)
    g_in_t, c_all = _all_gather_parts([w_in_t, c.reshape(8, LANES)], "gather_w_in")
    c_all = c_all.reshape(N_DEV, D)
    w_in_kt = _w_in_t_to_kernel_layout(g_in_t.reshape(N_DEV * w_in_t.shape[0], D))

    mod_cols = _mm(c_all, w_ada[0], name="ada_mod")
    mod_all = _all_gather(mod_cols, "gather_mod")
    mod = lax.dynamic_index_in_dim(mod_all, me, axis=1, keepdims=False).reshape(1, 3 * D)
    mod_shift, mod_scale, mod_gate = mod[:, :D], mod[:, D:2 * D], mod[:, 2 * D:]
    b_shift, b_scale, b_gate = b_ada[:, :D], b_ada[:, D:2 * D], b_ada[:, 2 * D:]
    g1 = norm_gain
    gq, gkv = q_norm_gain, kv_norm_gain
    gf = final_norm_gain.reshape(1, D)

    def f_h(x_, g1_, ms, bs, msc, bsc):
        xn = x_ * _rms(x_)
        return (xn * g1_ * (1.0 + (msc + bsc)) + (ms + bs),), ()

    (h,) = _rowwise(f_h, [x2d], [g1, mod_shift, b_shift, mod_scale, b_scale], [(D, BF16)], name="ada_norm")
    proj, (gathered,) = _mm(h, w_in_kt, tb=True, name="proj_in", tiles=(min(s_len, 1024), IN_PAD // 2, D),
                            exchange=_Exchange([jnp.broadcast_to(packed[None], (N_DEV,) + packed.shape)]))
    offs = [0]
    for n in big_sizes:
        offs.append(offs[-1] + n // LANES)

    def unpack(t, shape):
        return gathered[:, offs[t]:offs[t + 1], :].reshape((N_DEV,) + shape)

    def cols(t, shape):
        return unpack(t, shape).transpose(1, 0, 2).reshape(shape[0], N_DEV * shape[1])

    w_uq_k = _uq_to_kernel_layout(cols(0, big[0].shape))
    w_ukv_k = _ukv_to_kernel_layout(cols(1, big[1].shape))
    w_a_f = cols(2, big[2].shape)
    w_b_f = cols(3, big[3].shape)
    w_out_f = unpack(4, big[4].shape).reshape(D, D)

    o_a, sb_tails = _sb_forward(proj, s_len)

    def f_lat(cq, ckv, gq_, gkv_):
        return (cq * _rms(cq) * gq_, ckv * _rms(ckv) * gkv_), ()

    cq_n, ckv_n = _rowwise(f_lat, [(proj, O_CQ // Q_RANK, Q_RANK), (proj, O_CKV // KV_RANK, KV_RANK)], [gq, gkv],
                           [(Q_RANK, BF16), (KV_RANK, BF16)], name="latent_norm")
    q_mla = _mm(cq_n, w_uq_k, name="q_up")
    kv = _mm(ckv_n, w_ukv_k, name="kv_up")

    inv_freq = ROPE_BASE ** (-jnp.arange(0, ROPE, 2, dtype=F32) / ROPE)
    inv_freq_t = jnp.tile(inv_freq, HEADS).reshape(1, LANES)
    pos_col = positions.reshape(s_len, 1).astype(F32)

    pairs = HEADS // 2

    def f_rope(pos, qn, q1, q2, kn, k1, k2, freq):
        ang = pos * freq
        cs, sn = jnp.cos(ang), jnp.sin(ang)
        q1r, q2r = q1 * cs - q2 * sn, q1 * sn + q2 * cs
        k1r, k2r = k1 * cs - k2 * sn, k1 * sn + k2 * cs
        lane = lax.broadcasted_iota(jnp.int32, q1.shape, 1)
        first, second = lane < ROPE, (lane >= ROPE) & (lane < 2 * ROPE)
        k_rot = jnp.where(first, k1r, jnp.where(second, k2r, 0.0))
        q_parts, k_parts = [], []
        for p in range(pairs):
            q_rot = jnp.where(first, pltpu.roll(q1r, (LANES - ROPE * p) % LANES, 1),
                              jnp.where(second, pltpu.roll(q2r, (LANES + ROPE - ROPE * p) % LANES, 1), 0.0))
            q_parts += [qn[:, LANES * p:LANES * (p + 1)], q_rot]
            k_parts += [kn[:, LANES * p:LANES * (p + 1)], k_rot]
        return (jnp.concatenate(q_parts, axis=1), jnp.concatenate(k_parts, axis=1), cs, sn), ()

    q_cat, k_cat, cos_t, sin_t = _rowwise(
        f_rope, [pos_col, (q_mla, 0, MLA_W), (q_mla, 4, LANES), (q_mla, 5, LANES), (kv, 0, MLA_W),
                 (proj, O_KROT // LANES, LANES), (proj, O_KROT // LANES + 1, LANES)], [inv_freq_t],
        [(2 * MLA_W, BF16), (2 * MLA_W, BF16), (LANES, F32), (LANES, F32)], name="rope")

    o_b, lse = _mla_forward(q_cat, k_cat, kv, s_len)

    def f_gate(oa, za, ob, zb):
        return (oa * _silu(za), ob * _silu(zb)), ()

    ya_in, yb_in = _rowwise(f_gate, [o_a, (proj, O_SBZ // SB_W, SB_W), o_b, (proj, O_MLAZ // MLA_W, MLA_W)], [],
                            [(SB_W, BF16), (MLA_W, BF16)], name="branch_gate")
    y_a = _mm(ya_in, w_a_f, name="branch_a")
    y_b = _mm(yb_in, w_b_f, name="branch_b")

    def f_merge(ga, gb, ya, yb):
        return (_sigmoid(ga) * ya + _sigmoid(gb) * yb,), ()

    (merged,) = _rowwise(f_merge, [(proj, O_GA // D, D), (proj, O_GB // D, D), y_a, y_b], [], [(D, BF16)], name="merge")
    def f_loss(out_, x_, t_, mg, bg, gf_):
        gate = mg + bg
        x2 = x_ + gate * out_
        r2 = _rms(x2)
        xn2 = x2 * r2
        err = xn2 * gf_ - t_
        loss = jnp.full((1, LANES), 0.5 / D, F32) * jnp.sum(err * err)
        dy = err * (1.0 / D)
        dx2 = _rms_bwd(xn2, r2, dy * gf_)
        return (dx2, dx2 * gate), (loss, _colsum(dy * xn2), _colsum(dx2 * out_))

    dx2, d_out, loss_part, d_gf, d_gate = _mm(
        merged, w_out_f, name="out_proj_loss", tiles=(min(s_len, 512), D, D),
        epilogue=(f_loss, [x2d, tgt], [mod_gate, b_gate, gf], [(D, F32), (D, BF16)], [LANES, D, D]))

    d_merged = _mm(d_out, w_out_f, tb=True, name="d_merged")
    dw_out = _mm(merged, d_out, ta=True, name="dw_out")

    def f_dmerge(dm, ga, gb, ya, yb):
        sa, sb = _sigmoid(ga), _sigmoid(gb)
        return (dm * sa, dm * sb, dm * ya * sa * (1.0 - sa), dm * yb * sb * (1.0 - sb)), ()

    d_ya, d_yb, d_ga, d_gb = _rowwise(f_dmerge, [d_merged, (proj, O_GA // D, D), (proj, O_GB // D, D), y_a, y_b], [],
                                      [(D, BF16)] * 4, name="d_merge")
    dw_a = _mm(ya_in, d_ya, ta=True, name="dw_branch_a")
    dw_b = _mm(yb_in, d_yb, ta=True, name="dw_branch_b")
    d_ya_in = _mm(d_ya, w_a_f, tb=True, name="d_branch_a")
    d_yb_in = _mm(d_yb, w_b_f, tb=True, name="d_branch_b")

    def f_dgate(da, oa, za, db, ob, zb):
        return (da * _silu(za), da * oa * _silu_grad(za), db * _silu(zb), db * ob * _silu_grad(zb)), ()

    d_oa, d_sbz, d_ob, d_mlaz = _rowwise(
        f_dgate, [d_ya_in, o_a, (proj, O_SBZ // SB_W, SB_W), d_yb_in, o_b, (proj, O_MLAZ // MLA_W, MLA_W)], [],
        [(SB_W, F32), (SB_W, BF16), (MLA_W, F32), (MLA_W, BF16)], name="d_branch_gate")

    dq_cat, dk_cat, dv_b = _mla_backward(q_cat, k_cat, kv, o_b, lse, d_ob, s_len)

    def f_drope(dq, dk, dv_, cs, sn):
        lane = lax.broadcasted_iota(jnp.int32, cs.shape, 1)
        first, second = lane < ROPE, (lane >= ROPE) & (lane < 2 * ROPE)
        dq1 = dq2 = dk1 = dk2 = None
        for p in range(pairs):
            q_rot = dq[:, LANES * (2 * p + 1):LANES * (2 * p + 2)]
            k_rot = dk[:, LANES * (2 * p + 1):LANES * (2 * p + 2)]
            parts = (pltpu.roll(jnp.where(first, q_rot, 0.0), (ROPE * p) % LANES, 1),
                     pltpu.roll(jnp.where(second, q_rot, 0.0), (LANES - ROPE + ROPE * p) % LANES, 1),
                     jnp.where(first, k_rot, 0.0), jnp.where(second, k_rot, 0.0))
            if p == 0:
                dq1, dq2, dk1, dk2 = parts
            else:
                dq1, dq2, dk1, dk2 = dq1 + parts[0], dq2 + parts[1], dk1 + parts[2], dk2 + parts[3]
        dqn_ = [dq[:, 2 * LANES * p:2 * LANES * p + LANES] for p in range(pairs)]
        dkn_ = [dk[:, 2 * LANES * p:2 * LANES * p + LANES] for p in range(pairs)]
        return (jnp.concatenate(dqn_ + [dq1 * cs + dq2 * sn, dq2 * cs - dq1 * sn], axis=1),
                jnp.concatenate(dkn_ + [dv_], axis=1),
                jnp.concatenate([dk1 * cs + dk2 * sn, dk2 * cs - dk1 * sn], axis=1)), ()

    dq_k, dkv_k, d_krot = _rowwise(f_drope, [dq_cat, dk_cat, dv_b, cos_t, sin_t], [],
                                   [(HEADS * QK_DIM, BF16), (2 * MLA_W, BF16), (2 * LANES, BF16)], name="d_rope")
    dw_uq_k = _mm(cq_n, dq_k, ta=True, name="dw_uq")
    dw_ukv_k = _mm(ckv_n, dkv_k, ta=True, name="dw_ukv")
    d_cqn = _mm(dq_k, w_uq_k, tb=True, name="d_cq_norm")
    d_ckvn = _mm(dkv_k, w_ukv_k, tb=True, name="d_ckv_norm")

    def f_dlat(cq, dcqn, ckv, dckvn, gq_, gkv_):
        rq, rkv = _rms(cq), _rms(ckv)
        cqn, ckvn = cq * rq, ckv * rkv
        return ((_rms_bwd(cqn, rq, dcqn * gq_), _rms_bwd(ckvn, rkv, dckvn * gkv_)),
                (_colsum(dcqn * cqn), _colsum(dckvn * ckvn)))

    d_cq, d_ckv, d_gq, d_gkv = _rowwise(
        f_dlat, [(proj, O_CQ // Q_RANK, Q_RANK), d_cqn, (proj, O_CKV // KV_RANK, KV_RANK), d_ckvn], [gq, gkv],
        [(Q_RANK, BF16), (KV_RANK, BF16)], [Q_RANK, KV_RANK], name="d_latent_norm")

    def col_blocks(g):
        kdim, n8 = g.shape
        return g.astype(BF16).reshape(kdim, N_DEV, n8 // N_DEV).transpose(1, 0, 2).reshape(N_DEV, -1, LANES)

    g_blocks = jnp.concatenate([col_blocks(_uq_from_kernel_layout(dw_uq_k)), col_blocks(_ukv_from_kernel_layout(dw_ukv_k)),
                                col_blocks(dw_a), col_blocks(dw_b), dw_out.astype(BF16).reshape(N_DEV, -1, LANES)], axis=1)
    (d_sbq, d_sbk, d_sbv), (g_recv,) = _sb_backward(proj, sb_tails, d_oa, s_len, _Exchange([g_blocks]))

    d_proj = jnp.concatenate([d_ga, d_gb, d_sbq.astype(BF16), d_sbk.astype(BF16), d_sbv.astype(BF16), d_sbz, d_mlaz,
                              d_cq, jnp.zeros((s_len, LANES), BF16), d_ckv, d_krot], axis=1)
    dw_in_kt = _mm(d_proj, h, ta=True, out_dtype=BF16, name="dw_in", tiles=(512, D, s_len))

    def krot_body(t_ref, o_ref):
        half = ROPE // 2
        for part in range(2):
            acc = t_ref[part * LANES:part * LANES + half, :].astype(F32)
            for hh in range(1, HEADS):
                acc = acc + t_ref[part * LANES + hh * half:part * LANES + (hh + 1) * half, :].astype(F32)
            o_ref[part * half:(part + 1) * half, :] = acc

    dw_krot = pl.pallas_call(krot_body, name="dw_krot_sum", out_shape=jax.ShapeDtypeStruct((ROPE, D), F32))(
        dw_in_kt[O_KROT:O_KROT + 2 * LANES])

    g_in_blocks = _w_in_t_from_kernel_layout(dw_in_kt, dw_krot).reshape(N_DEV, -1, D)
    def f_dx(dh_, x_, dx2_, g1_, msc, bsc):
        r = _rms(x_)
        xn = x_ * r
        dn1 = dh_ * (1.0 + (msc + bsc))
        return ((dx2_ + _rms_bwd(xn, r, dn1 * g1_),),
                (_colsum(dh_), _colsum(dh_ * (xn * g1_)), _colsum(dn1 * xn)))

    (grad_x2d, d_shift, d_scale, d_g1), (g_in_recv,) = _mm(
        d_proj, w_in_kt, name="d_h", tiles=(min(s_len, 512), D, 512), exchange=_Exchange([g_in_blocks]),
        epilogue=(f_dx, [x2d, dx2], [g1, mod_scale, b_scale], [(D, F32)], [D, D, D]))

    g_in_sum_t = _sum_blocks(g_in_recv, "sum_grads_w_in")
    g_sum = _sum_blocks(g_recv, "sum_grads")
    g_big = [g_sum[offs[t]:offs[t + 1]].reshape(big[t].shape) for t in range(5)]

    small = jnp.concatenate([d_shift, d_scale, d_gate, d_g1, d_gq, d_gkv, d_gf, loss_part], axis=1)
    n_small = small.shape[1]
    pad = (-n_small) % (8 * LANES)
    small = jnp.pad(small, ((0, 0), (0, pad))).reshape(-1, LANES)
    small_all = _all_gather(small, "gather_small")
    small_sum = _sum_blocks(small_all, "sum_small").reshape(1, -1)
    g_b_ada = small_sum[:, :3 * D]
    g_g1 = small_sum[:, 3 * D:4 * D]
    g_gq = small_sum[:, 4 * D:4 * D + Q_RANK]
    g_gkv = small_sum[:, 4 * D + Q_RANK:4 * D + Q_RANK + KV_RANK]
    g_gf = small_sum[:, 4 * D + Q_RANK + KV_RANK:4 * D + Q_RANK + KV_RANK + D]

    dmod_all = small_all.reshape(N_DEV, -1)[:, :3 * D]
    dmod_cols = lax.dynamic_slice_in_dim(dmod_all, me * (3 * D // N_DEV), 3 * D // N_DEV, axis=1)
    g_w_ada = _mm(c_all, dmod_cols, ta=True, name="dw_ada")

    loss = small_sum[0, n_small - LANES]

    names = ["w_ada", "b_ada", "norm_gain", "w_in", "q_norm_gain", "w_uq", "kv_norm_gain", "w_ukv",
             "w_branch_a", "w_branch_b", "w_out", "final_norm_gain"]
    weights = dict(w_ada=w_ada, b_ada=b_ada, norm_gain=norm_gain, w_in=w_in, q_norm_gain=q_norm_gain, w_uq=w_uq,
                   kv_norm_gain=kv_norm_gain, w_ukv=w_ukv, w_branch_a=w_branch_a, w_branch_b=w_branch_b, w_out=w_out,
                   final_norm_gain=final_norm_gain)
    moms = dict(w_ada=m_w_ada, b_ada=m_b_ada, norm_gain=m_norm_gain, w_in=m_w_in, q_norm_gain=m_q_norm_gain,
                w_uq=m_w_uq, kv_norm_gain=m_kv_norm_gain, w_ukv=m_w_ukv, w_branch_a=m_w_branch_a,
                w_branch_b=m_w_branch_b, w_out=m_w_out, final_norm_gain=m_final_norm_gain)
    vels = dict(w_ada=v_w_ada, b_ada=v_b_ada, norm_gain=v_norm_gain, w_in=v_w_in, q_norm_gain=v_q_norm_gain,
                w_uq=v_w_uq, kv_norm_gain=v_kv_norm_gain, w_ukv=v_w_ukv, w_branch_a=v_w_branch_a,
                w_branch_b=v_w_branch_b, w_out=v_w_out, final_norm_gain=v_final_norm_gain)
    grads2d = dict(w_ada=g_w_ada, b_ada=g_b_ada, norm_gain=g_g1, w_in=g_in_sum_t, q_norm_gain=g_gq, w_uq=g_big[0],
                   kv_norm_gain=g_gkv, w_ukv=g_big[1], w_branch_a=g_big[2], w_branch_b=g_big[3], w_out=g_big[4],
                   final_norm_gain=g_gf)

    grads, deltas, new_m, new_v = [], [], [], []
    for n in names:
        w = weights[n]
        if n == "w_in":
            to2d = lambda t: t[0].T
            back = lambda t: t.T[None]
        else:
            shape2d = grads2d[n].shape
            to2d = lambda t, s=shape2d: t.reshape(s)
            back = lambda t, s=w.shape: t.reshape(s)
        d_, m_, v_ = _adamw_call(to2d(w), grads2d[n], to2d(moms[n]), to2d(vels[n]), "adamw_" + n)
        grads.append(back(grads2d[n]))
        deltas.append(back(d_))
        new_m.append(back(m_))
        new_v.append(back(v_))

    return (loss, grad_x2d.reshape(x.shape), *grads, *deltas, *new_m, *new_v)
```

```python
import functools
import math

import jax
import jax.numpy as jnp
from jax import lax
from jax.experimental import pallas as pl
from jax.experimental.pallas import tpu as pltpu

F32 = jnp.float32
BF16 = jnp.bfloat16
MXU_DTYPE = jnp.bfloat16

N_DEV = 8
D = 1024
HEADS = 8
HD = 64
SB_W = 512
MLA_W = 512
Q_RANK = 384
KV_RANK = 256
ROPE = 32
NOPE = 64
QK_DIM = NOPE + ROPE
EPS = 1e-6
ROPE_BASE = 10000.0

ADAM_LR = 0.001
ADAM_B1 = 0.9
ADAM_B2 = 0.999
ADAM_EPS = 1e-08
ADAM_WD = 0.01
ADAM_STEP = 10

LANES = 128
VMEM_LIMIT = 48 * 1024 * 1024

O_GA, O_GB = 0, 1024
O_SBQ, O_SBK, O_SBV, O_SBZ = 2048, 2560, 3072, 3584
O_MLAZ = 4096
O_CQ = 4608
O_CKV = 5120
O_KROT = 5376
IN_PAD = 5632

BQ = 256
BK = 256


def _cparams(*sem):
    return pltpu.CompilerParams(dimension_semantics=sem, vmem_limit_bytes=VMEM_LIMIT)


def _tile_of(n, cap=512):
    if n <= cap:
        return n
    for t in (1024, 768, 512, 384, 256, 128):
        if t <= cap and n % t == 0:
            return t
    raise ValueError(n)


def _rowwise(fn, rows, vecs, outs, reds=(), *, name, tile=512):
    norm = []
    for r in rows:
        if isinstance(r, tuple):
            arr, cb, w = r[:3]
            ro = r[3] if len(r) > 3 else 0
        else:
            arr, cb, w, ro = r, 0, r.shape[1], 0
        norm.append((arr, cb, w, ro))
    s_len = norm[0][0].shape[0]
    tile = min(tile, s_len)
    assert s_len % tile == 0
    n_row, n_vec, n_out, n_red = len(norm), len(vecs), len(outs), len(reds)

    def body(*refs):
        step = pl.program_id(0)
        row_refs = refs[:n_row]
        vec_refs = refs[n_row:n_row + n_vec]
        out_refs = refs[n_row + n_vec:n_row + n_vec + n_out]
        red_refs = refs[n_row + n_vec + n_out:]
        row_res, red_res = fn(*[r[...] for r in row_refs], *[v[...] for v in vec_refs])
        for o, val in zip(out_refs, row_res):
            o[...] = val.astype(o.dtype)
        if n_red:
            @pl.when(step == 0)
            def _():
                for r in red_refs:
                    r[...] = jnp.zeros(r.shape, r.dtype)
            for r, val in zip(red_refs, red_res):
                r[...] += val

    in_specs = []
    for arr, cb, w, ro in norm:
        in_specs.append(pl.BlockSpec((tile, w), functools.partial(lambda i, cb, rb: (i + rb, cb), cb=cb, rb=ro // tile)))
        assert ro % tile == 0
    for v in vecs:
        in_specs.append(pl.BlockSpec(v.shape, lambda i: (0, 0)))
    out_shape = [jax.ShapeDtypeStruct((s_len, w), dt) for w, dt in outs]
    out_specs = [pl.BlockSpec((tile, w), lambda i: (i, 0)) for w, _ in outs]
    out_shape += [jax.ShapeDtypeStruct((1, w), F32) for w in reds]
    out_specs += [pl.BlockSpec((1, w), lambda i: (0, 0)) for w in reds]
    res = pl.pallas_call(
        body, name=name, grid=(s_len // tile,), in_specs=in_specs, out_specs=out_specs, out_shape=out_shape,
        compiler_params=_cparams("arbitrary" if n_red else "parallel"),
    )(*[a for a, _, _, _ in norm], *vecs)
    return res


def _mm(a, b, *, ta=False, tb=False, out_dtype=F32, name, exchange=None, tiles=None, epilogue=None):
    m, k = (a.shape[1], a.shape[0]) if ta else a.shape
    n = b.shape[0] if tb else b.shape[1]
    assert (b.shape[1] if tb else b.shape[0]) == k
    tm, tn, tk = tiles or (_tile_of(m, 1024), _tile_of(n, 1024 if n <= 1024 else 512), _tile_of(k, 1024))
    assert m % tm == 0 and n % tn == 0 and k % tk == 0
    ni, nj, nk = m // tm, n // tn, k // tk
    dims = (((0 if ta else 1,), (1 if tb else 0,)), ((), ()))
    n_ex = len(exchange.arrays) if exchange else 0
    fn, rows, vecs, outs, reds = epilogue or (None, (), (), (), ())
    rows = [r if isinstance(r, tuple) else (r, 0, r.shape[1]) for r in rows]
    assert not epilogue or tn == n
    n_res = len(outs) + len(reds) if epilogue else 1

    def body(*refs):
        a_ref, b_ref = refs[:2]
        row_refs, refs = refs[2:2 + len(rows)], refs[2 + len(rows):]
        vec_refs, refs = refs[:len(vecs)], refs[len(vecs):]
        x_refs, refs = refs[:n_ex], refs[n_ex:]
        res_refs, refs = refs[:n_res], refs[n_res:]
        out_refs, acc_ref, sems = refs[:n_ex], refs[n_ex], refs[n_ex + 1:]
        i, j, kk = pl.program_id(0), pl.program_id(1), pl.program_id(2)
        first = (i == 0) & (j == 0) & (kk == 0)

        if exchange:
            @pl.when(first)
            def _():
                exchange.start(x_refs, out_refs, sems)

        if reds:
            @pl.when(first)
            def _():
                for r in res_refs[len(outs):]:
                    r[...] = jnp.zeros(r.shape, r.dtype)

        @pl.when(kk == 0)
        def _():
            acc_ref[...] = jnp.zeros(acc_ref.shape, F32)

        acc_ref[...] += lax.dot_general(a_ref[...].astype(MXU_DTYPE), b_ref[...].astype(MXU_DTYPE), dims,
                                        preferred_element_type=F32)

        @pl.when(kk == nk - 1)
        def _():
            if not epilogue:
                res_refs[0][...] = acc_ref[...].astype(res_refs[0].dtype)
                return
            row_res, red_res = fn(acc_ref[...], *[r[...] for r in row_refs], *[v[...] for v in vec_refs])
            for o, val in zip(res_refs, row_res):
                o[...] = val.astype(o.dtype)
            for r, val in zip(res_refs[len(outs):], red_res):
                r[...] += val

        if exchange:
            @pl.when((i == ni - 1) & (j == nj - 1) & (kk == nk - 1))
            def _():
                exchange.wait(x_refs, out_refs, sems)

    a_spec = pl.BlockSpec((tk, tm), lambda i, j, kk: (kk, i)) if ta else pl.BlockSpec((tm, tk), lambda i, j, kk: (i, kk))
    b_spec = pl.BlockSpec((tn, tk), lambda i, j, kk: (j, kk)) if tb else pl.BlockSpec((tk, tn), lambda i, j, kk: (kk, j))
    in_specs = [a_spec, b_spec]
    in_specs += [pl.BlockSpec((tm, w), functools.partial(lambda i, j, kk, cb: (i, cb), cb=cb)) for _, cb, w in rows]
    in_specs += [pl.BlockSpec(v.shape, lambda i, j, kk: (0, 0)) for v in vecs]
    if epilogue:
        res_specs = [pl.BlockSpec((tm, w), lambda i, j, kk: (i, 0)) for w, _ in outs]
        res_specs += [pl.BlockSpec((1, w), lambda i, j, kk: (0, 0)) for w in reds]
        res_shape = [jax.ShapeDtypeStruct((m, w), dt) for w, dt in outs] + [jax.ShapeDtypeStruct((1, w), F32) for w in reds]
    else:
        res_specs = [pl.BlockSpec((tm, tn), lambda i, j, kk: (i, j))]
        res_shape = [jax.ShapeDtypeStruct((m, n), out_dtype)]
    ordered = bool(exchange or reds)
    res = pl.pallas_call(
        body, name=name, grid=(ni, nj, nk),
        in_specs=in_specs + (exchange.in_specs if exchange else []),
        out_specs=res_specs + (exchange.out_specs if exchange else []),
        out_shape=res_shape + (exchange.out_shape if exchange else []),
        scratch_shapes=[pltpu.VMEM((tm, tn), F32)] + (exchange.scratch if exchange else []),
        compiler_params=_cparams(*(("arbitrary",) * 3 if ordered else ("parallel", "parallel", "arbitrary"))),
    )(a, b, *[r[0] for r in rows], *vecs, *(exchange.arrays if exchange else []))
    main = res[:n_res] if epilogue else res[0]
    return (main, res[n_res:]) if exchange else main


_NT = (((1,), (1,)), ((), ()))
_TN = (((0,), (0,)), ((), ()))


def _dot(a, b):
    return jnp.dot(a, b, preferred_element_type=F32)


def _dot_nt(a, b):
    return lax.dot_general(a, b, _NT, preferred_element_type=F32)


def _dot_tn(a, b):
    return lax.dot_general(a, b, _TN, preferred_element_type=F32)


def _running_sum(x, tri):
    return _dot(x.astype(MXU_DTYPE), tri)


def _neg_softplus(z):
    u = jnp.exp2(jnp.abs(z) * (-1.0 / math.log(2.0)))
    return -jnp.maximum(z, 0.0) - jnp.log(1.0 + u)


def _walk_blocks(step, st, n, descending, group=2):
    done = 0
    size = group
    while size >= 1:
        def trip(t, s, size=size, done=done):
            js = [done + size * t + g for g in range(size)]
            return step([n - 1 - j for j in js] if descending else js, s)

        trips = (n - done) // size
        st = lax.fori_loop(0, trips, trip, st)
        done = done + size * trips
        size //= 2
    return st


def _chains(js):
    return [(h, t) for t in range(len(js)) for h in range(2)]


def _rowsum(x):
    return jnp.sum(x, axis=1, keepdims=True)


def _attn_consts():
    row = lax.broadcasted_iota(jnp.int32, (BQ, BK), 0)
    col = lax.broadcasted_iota(jnp.int32, (BQ, BK), 1)
    trow = lax.broadcasted_iota(jnp.int32, (BK, BK), 0)
    tcol = lax.broadcasted_iota(jnp.int32, (BK, BK), 1)
    lane = lax.broadcasted_iota(jnp.int32, (BQ, LANES), 1)
    klane = lax.broadcasted_iota(jnp.int32, (BK, LANES), 1)
    return row, col, trow, tcol, lane, klane


DIAG = BQ // BK
assert BQ == DIAG * BK


def _diag_block(i):
    return i * DIAG


def _diag_blocks(jd, descending):
    places = list(reversed(range(DIAG))) if descending else list(range(DIAG))
    return [jd + g for g in places], places


def _key_slice(j):
    return pl.ds(pl.multiple_of(j * BK, BK), BK)


def _sb_forward(proj, s_len):
    nq = s_len // BQ
    assert s_len // BK <= HD
    qc, kc, vc = O_SBQ // LANES, O_SBK // LANES, O_SBV // LANES

    def body(q_ref, k_ref, v_ref, o_ref, tails_ref):
        i = pl.program_id(1)
        jd = _diag_block(i)
        row, col, trow, tcol, lane, klane = _attn_consts()
        strict = [col + g * BK < row for g in range(DIAG)]
        tri = (trow >= tcol).astype(MXU_DTYPE)
        q = q_ref[...] * 0.125
        qh = [jnp.where(lane < HD, q, 0.0).astype(MXU_DTYPE), jnp.where(lane >= HD, q, 0.0).astype(MXU_DTYPE)]

        km = [klane < HD, klane >= HD]

        def step(js, st, diag):
            carry, acc, tail = st
            chains = _chains(js)
            kj = [k_ref[_key_slice(j), :].astype(MXU_DTYPE) for j in js]
            vj = [v_ref[_key_slice(j), :].astype(MXU_DTYPE) for j in js]
            z = {(h, t): _dot_nt(qh[h], kj[t]) for h, t in chains}
            run = list(carry)
            suf, carry_in = {}, {}
            for h, t in chains:
                lom = _neg_softplus(z[h, t])
                if diag:
                    lom = jnp.where(strict[diag[t]], lom, 0.0)
                suf[h, t] = _running_sum(lom, tri)
                carry_in[h, t] = run[h]
                run[h] = run[h] + _rowsum(lom)
            for h, t in chains:
                a = jnp.exp(z[h, t] + suf[h, t] + carry_in[h, t])
                if diag:
                    a = jnp.where(strict[diag[t]], a, 0.0)
                acc = acc + _dot(a.astype(MXU_DTYPE), jnp.where(km[h], vj[t], 0))
                tail = jnp.where(lane == h * HD + js[t], carry_in[h, t], tail)
            return tuple(run), acc, tail

        zero = jnp.zeros((BQ, LANES), F32)
        diag_js, places = _diag_blocks(jd, True)
        st = step(diag_js, ((jnp.zeros((BQ, 1), F32),) * 2, zero, zero), places)
        st = _walk_blocks(lambda js, s: step(js, s, False), st, jd, True, group=8)
        o_ref[...] = st[1]
        tails_ref[...] = st[2]

    blk = pl.BlockSpec((BQ, LANES), lambda p, i: (i, p))
    out = jax.ShapeDtypeStruct((s_len, SB_W), F32)
    return pl.pallas_call(
        body, name="sb_fwd", grid=(HEADS // 2, nq),
        in_specs=[pl.BlockSpec((BQ, LANES), lambda p, i: (i, qc + p)),
                  pl.BlockSpec((s_len, LANES), lambda p, i: (0, kc + p)),
                  pl.BlockSpec((s_len, LANES), lambda p, i: (0, vc + p))],
        out_specs=[blk, blk], out_shape=[out, out],
        compiler_params=_cparams("parallel", "parallel"),
    )(proj, proj, proj)


def _sb_backward(proj, tails, do, s_len, exchange):
    nq = s_len // BQ
    qc, kc, vc = O_SBQ // LANES, O_SBK // LANES, O_SBV // LANES
    n_ex = len(exchange.arrays)

    def body(q_ref, k_ref, v_ref, tails_ref, do_ref, *rest):
        x_refs, (dq_ref, dk_ref, dv_ref) = rest[:n_ex], rest[n_ex:n_ex + 3]
        out_refs, sems = rest[n_ex + 3:2 * n_ex + 3], rest[2 * n_ex + 3:]
        p = pl.program_id(0)
        i = pl.program_id(1)
        jd = _diag_block(i)

        @pl.when((p == 0) & (i == 0))
        def _():
            exchange.start(x_refs, out_refs, sems)

        @pl.when(i == 0)
        def _():
            dk_ref[...] = jnp.zeros(dk_ref.shape, F32)
            dv_ref[...] = jnp.zeros(dv_ref.shape, F32)

        row, col, trow, tcol, lane, klane = _attn_consts()
        strict = [col + g * BK < row for g in range(DIAG)]
        tri = (trow >= tcol).astype(MXU_DTYPE)
        tri_p = (trow <= tcol).astype(MXU_DTYPE)
        q = q_ref[...] * 0.125
        tails_blk = tails_ref[...]
        do_blk = do_ref[...]
        hm = [lane < HD, lane >= HD]
        km = [klane < HD, klane >= HD]
        qh = [jnp.where(m, q, 0.0).astype(MXU_DTYPE) for m in hm]
        doh = [jnp.where(m, do_blk, 0.0).astype(MXU_DTYPE) for m in hm]

        def step(js, st, diag):
            before, dq = st
            chains = _chains(js)
            kj = [k_ref[_key_slice(j), :].astype(MXU_DTYPE) for j in js]
            vj = [v_ref[_key_slice(j), :].astype(MXU_DTYPE) for j in js]
            z = {(h, t): _dot_nt(qh[h], kj[t]) for h, t in chains}
            da = {(h, t): _dot_nt(doh[h], vj[t]) for h, t in chains}
            suf, sig = {}, {}
            for h, t in chains:
                lom = _neg_softplus(z[h, t])
                if diag:
                    lom = jnp.where(strict[diag[t]], lom, 0.0)
                suf[h, t] = _running_sum(lom, tri)
                sig[h, t] = jnp.exp(z[h, t] + lom)
            run = list(before)
            dl, pre, before_in = {}, {}, {}
            dk_add, dv_add = [None] * len(js), [None] * len(js)
            for h, t in chains:
                tail = _rowsum(jnp.where(lane == h * HD + js[t], tails_blk, 0.0))
                a = jnp.exp(z[h, t] + suf[h, t] + tail)
                if diag:
                    a = jnp.where(strict[diag[t]], a, 0.0)
                dl[h, t] = da[h, t] * a
                pre[h, t] = _dot(dl[h, t].astype(MXU_DTYPE), tri_p)
                dv_h = _dot_tn(a.astype(MXU_DTYPE), doh[h])
                dv_add[t] = dv_h if dv_add[t] is None else dv_add[t] + dv_h
                before_in[h, t] = run[h]
                run[h] = run[h] + _rowsum(dl[h, t])
            for h, t in chains:
                upto = before_in[h, t] + pre[h, t]
                dz = dl[h, t] - sig[h, t] * upto
                if diag:
                    dz = jnp.where(strict[diag[t]], dz, 0.0)
                dzb = dz.astype(MXU_DTYPE)
                dq = dq + _dot(dzb, jnp.where(km[h], kj[t], 0))
                dk_h = _dot_tn(dzb, qh[h])
                dk_add[t] = dk_h if dk_add[t] is None else dk_add[t] + dk_h
            for t, j in enumerate(js):
                dk_ref[_key_slice(j), :] += dk_add[t]
                dv_ref[_key_slice(j), :] += dv_add[t]
            return tuple(run), dq

        st = ((jnp.zeros((BQ, 1), F32),) * 2, jnp.zeros((BQ, LANES), F32))
        st = _walk_blocks(lambda js, s: step(js, s, False), st, jd, False, group=4)
        diag_js, places = _diag_blocks(jd, False)
        st = step(diag_js, st, places)
        dq_ref[...] = st[1] * 0.125

        @pl.when((p == HEADS // 2 - 1) & (i == nq - 1))
        def _():
            exchange.wait(x_refs, out_refs, sems)

    blk = pl.BlockSpec((BQ, LANES), lambda p, i: (i, p))
    full = pl.BlockSpec((s_len, LANES), lambda p, i: (0, p))
    out = jax.ShapeDtypeStruct((s_len, SB_W), F32)
    res = pl.pallas_call(
        body, name="sb_bwd", grid=(HEADS // 2, nq),
        in_specs=[pl.BlockSpec((BQ, LANES), lambda p, i: (i, qc + p)),
                  pl.BlockSpec((s_len, LANES), lambda p, i: (0, kc + p)),
                  pl.BlockSpec((s_len, LANES), lambda p, i: (0, vc + p)),
                  blk, blk] + exchange.in_specs,
        out_specs=[blk, full, full] + exchange.out_specs, out_shape=[out, out, out] + exchange.out_shape,
        scratch_shapes=exchange.scratch,
        compiler_params=_cparams("arbitrary", "arbitrary"),
    )(proj, proj, proj, tails, do, *exchange.arrays)
    return res[:3], res[3:]


def _pair_mask(rows, h):
    lane = lax.broadcasted_iota(jnp.int32, (rows, 2 * LANES), 1)
    rot = lane - LANES
    return (((lane < LANES) & (lane // HD == h))
            | ((lane >= LANES) & (rot < 2 * ROPE) & ((rot // (ROPE // 2)) % 2 == h)))


def _mla_forward(q_cat, k_cat, kv, s_len):
    nq = s_len // BQ
    scale = 1.0 / math.sqrt(QK_DIM)

    def body(q_ref, k_ref, v_ref, o_ref, lse_ref):
        i = pl.program_id(1)
        jd = _diag_block(i)
        row, col, trow, tcol, lane, klane = _attn_consts()
        causal = [col + g * BK <= row for g in range(DIAG)]
        q = q_ref[...]
        hm = [lane < HD, lane >= HD]
        km = [klane < HD, klane >= HD]
        qh = [jnp.where(_pair_mask(BQ, h), q, 0) for h in range(2)]

        def step(js, st, diag):
            m_run, l_run, acc = st
            chains = _chains(js)
            kj = [k_ref[_key_slice(j), :] for j in js]
            vj = [v_ref[_key_slice(j), :].astype(MXU_DTYPE) for j in js]
            s = {}
            for h, t in chains:
                s[h, t] = _dot_nt(qh[h], kj[t]) * scale
                if diag:
                    s[h, t] = jnp.where(causal[diag[t]], s[h, t], -jnp.inf)
            m_new, alpha, l_new = [], [], []
            for h in range(2):
                top = m_run[h]
                for t in range(len(js)):
                    top = jnp.maximum(top, jnp.max(s[h, t], axis=1, keepdims=True))
                m_new.append(top)
                alpha.append(jnp.exp(m_run[h] - top))
                l_new.append(alpha[h] * l_run[h])
            add = None
            for h, t in chains:
                pr = jnp.exp(s[h, t] - m_new[h])
                l_new[h] = l_new[h] + _rowsum(pr)
                part = _dot(pr.astype(MXU_DTYPE), jnp.where(km[h], vj[t], 0))
                add = part if add is None else add + part
            acc = jnp.where(hm[0], alpha[0], alpha[1]) * acc + add
            return tuple(m_new), tuple(l_new), acc

        st = ((jnp.full((BQ, 1), -1e30, F32),) * 2, (jnp.zeros((BQ, 1), F32),) * 2, jnp.zeros((BQ, LANES), F32))
        diag_js, places = _diag_blocks(jd, True)
        st = step(diag_js, st, places)
        m_run, l_run, acc = _walk_blocks(lambda js, s: step(js, s, False), st, jd, True, group=8)
        o_ref[...] = acc / jnp.where(hm[0], l_run[0], l_run[1])
        lse_ref[...] = jnp.where(hm[0], m_run[0] + jnp.log(l_run[0]), m_run[1] + jnp.log(l_run[1]))

    blk = pl.BlockSpec((BQ, LANES), lambda p, i: (i, p))
    out = jax.ShapeDtypeStruct((s_len, MLA_W), F32)
    return pl.pallas_call(
        body, name="mla_fwd", grid=(HEADS // 2, nq),
        in_specs=[pl.BlockSpec((BQ, 2 * LANES), lambda p, i: (i, p)),
                  pl.BlockSpec((s_len, 2 * LANES), lambda p, i: (0, p)),
                  pl.BlockSpec((s_len, LANES), lambda p, i: (0, MLA_W // LANES + p))],
        out_specs=[blk, blk], out_shape=[out, out],
        compiler_params=_cparams("parallel", "parallel"),
    )(q_cat, k_cat, kv)


def _mla_backward(q_cat, k_cat, kv, o, lse, do, s_len):
    nq = s_len // BQ
    scale = 1.0 / math.sqrt(QK_DIM)

    def body(q_ref, k_ref, v_ref, o_ref, lse_ref, do_ref, dq_ref, dk_ref, dv_ref):
        i = pl.program_id(1)

        @pl.when(i == 0)
        def _():
            dk_ref[...] = jnp.zeros(dk_ref.shape, F32)
            dv_ref[...] = jnp.zeros(dv_ref.shape, F32)

        jd = _diag_block(i)
        row, col, trow, tcol, lane, klane = _attn_consts()
        causal = [col + g * BK <= row for g in range(DIAG)]
        q = q_ref[...]
        o_blk = o_ref[...]
        do_blk = do_ref[...]
        lse_blk = lse_ref[...]
        hm = [lane < HD, lane >= HD]
        kpm = [_pair_mask(BK, h) for h in range(2)]
        qh = [jnp.where(_pair_mask(BQ, h), q, 0) for h in range(2)]
        doh_f = [jnp.where(m, do_blk, 0.0) for m in hm]
        doh = [d.astype(MXU_DTYPE) for d in doh_f]
        delta = [jnp.sum(d * o_blk, axis=1, keepdims=True) for d in doh_f]
        lse_h = [jnp.sum(jnp.where(lane == h * HD, lse_blk, 0.0), axis=1, keepdims=True) for h in range(2)]

        def step(js, st, diag):
            dq = st
            chains = _chains(js)
            kj = [k_ref[_key_slice(j), :] for j in js]
            vj = [v_ref[_key_slice(j), :].astype(MXU_DTYPE) for j in js]
            s = {(h, t): _dot_nt(qh[h], kj[t]) for h, t in chains}
            dp = {(h, t): _dot_nt(doh[h], vj[t]) for h, t in chains}
            adds = [[None] * len(js) for _ in range(2)]

            def accumulate(slot, t, part):
                adds[slot][t] = part if adds[slot][t] is None else adds[slot][t] + part

            for h, t in chains:
                pr = jnp.exp(s[h, t] * scale - lse_h[h])
                if diag:
                    pr = jnp.where(causal[diag[t]], pr, 0.0)
                dsb = (pr * (dp[h, t] - delta[h]) * scale).astype(MXU_DTYPE)
                dq = dq + _dot(dsb, jnp.where(kpm[h], kj[t], 0))
                accumulate(0, t, _dot_tn(dsb, qh[h]))
                accumulate(1, t, _dot_tn(pr.astype(MXU_DTYPE), doh[h]))
            for t, j in enumerate(js):
                dk_ref[_key_slice(j), :] += adds[0][t]
                dv_ref[_key_slice(j), :] += adds[1][t]
            return dq

        diag_js, places = _diag_blocks(jd, True)
        st = step(diag_js, jnp.zeros((BQ, 2 * LANES), F32), places)
        dq_ref[...] = _walk_blocks(lambda js, s: step(js, s, False), st, jd, True, group=4)

    blk = pl.BlockSpec((BQ, LANES), lambda p, i: (i, p))
    full = pl.BlockSpec((s_len, LANES), lambda p, i: (0, p))
    out = jax.ShapeDtypeStruct((s_len, MLA_W), F32)
    out_cat = jax.ShapeDtypeStruct((s_len, 2 * MLA_W), F32)
    return pl.pallas_call(
        body, name="mla_bwd", grid=(HEADS // 2, nq),
        in_specs=[pl.BlockSpec((BQ, 2 * LANES), lambda p, i: (i, p)),
                  pl.BlockSpec((s_len, 2 * LANES), lambda p, i: (0, p)),
                  pl.BlockSpec((s_len, LANES), lambda p, i: (0, MLA_W // LANES + p)),
                  blk, blk, blk],
        out_specs=[pl.BlockSpec((BQ, 2 * LANES), lambda p, i: (i, p)),
                   pl.BlockSpec((s_len, 2 * LANES), lambda p, i: (0, p)), full],
        out_shape=[out_cat, out_cat, out],
        compiler_params=_cparams("arbitrary", "arbitrary"),
    )(q_cat, k_cat, kv, o, lse, do)


def _mesh_pos():
    return lax.axis_index("x"), lax.axis_index("y"), lax.axis_index("c")


def _dev_index(px, py, pc):
    return 4 * px + 2 * py + pc


def _all_gather(block, name):
    return _all_gather_parts([block], name)[0]


def _all_gather_parts(blocks, name):
    n = len(blocks)

    def body(*refs):
        x_refs, out_refs = refs[:n], refs[n:2 * n]
        send_sems, recv_sems, local_sems = refs[2 * n:]
        x, y, c = _mesh_pos()
        me, sibling = (x, y, c), (x, y, 1 - c)
        chips = [(1 - x, y), (x, 1 - y), (1 - x, 1 - y)]

        def copy(a, k, blockpos, to, src=None):
            slot = out_refs[a].at[_dev_index(*blockpos)]
            return pltpu.make_async_remote_copy(
                src_ref=slot if src is None else src, dst_ref=slot,
                send_sem=send_sems.at[7 * a + k], recv_sem=recv_sems.at[7 * a + k],
                device_id=to, device_id_type=pl.DeviceIdType.MESH)

        mine = [pltpu.make_async_copy(x_refs[a], out_refs[a].at[_dev_index(*me)], local_sems.at[a]) for a in range(n)]
        for cp in mine:
            cp.start()
        first = []
        for a in range(n):
            first.append(copy(a, 0, me, sibling, src=x_refs[a]))
            first += [copy(a, 1 + j, me, (*chip, c), src=x_refs[a]) for j, chip in enumerate(chips)]
        for cp in first:
            cp.start()
        passed = []
        for j, chip in enumerate(chips):
            for a in range(n):
                copy(a, 1 + j, (*chip, c), me).wait_recv()
                passed.append(copy(a, 4 + j, (*chip, c), sibling))
                passed[-1].start()
        for a in range(n):
            copy(a, 0, sibling, me).wait_recv()
            for j, chip in enumerate(chips):
                copy(a, 4 + j, (*chip, 1 - c), me).wait_recv()
        for cp in first + passed:
            cp.wait_send()
        for cp in mine:
            cp.wait()

    return pl.pallas_call(
        body, name=name,
        out_shape=[jax.ShapeDtypeStruct((N_DEV,) + b.shape, b.dtype) for b in blocks],
        in_specs=[pl.BlockSpec(memory_space=pl.ANY)] * n, out_specs=[pl.BlockSpec(memory_space=pl.ANY)] * n,
        scratch_shapes=[pltpu.SemaphoreType.DMA((7 * n,)), pltpu.SemaphoreType.DMA((7 * n,)),
                        pltpu.SemaphoreType.DMA((n,))],
    )(*blocks)


class _Exchange:
    def __init__(self, arrays):
        self.arrays = list(arrays)
        n = len(self.arrays)
        self.in_specs = [pl.BlockSpec(memory_space=pl.ANY)] * n
        self.out_specs = [pl.BlockSpec(memory_space=pl.ANY)] * n
        self.out_shape = [jax.ShapeDtypeStruct(a.shape, a.dtype) for a in self.arrays]
        self.scratch = [pltpu.SemaphoreType.DMA((7 * n,)), pltpu.SemaphoreType.DMA((7 * n,)),
                        pltpu.SemaphoreType.DMA((n,))]

    def _copies(self, x_refs, out_refs, sems, with_arrivals):
        send_sems, recv_sems, local_sems = sems
        x, y, c = _mesh_pos()
        me = _dev_index(x, y, c)
        flips = [(fx, fy, fc) for fx in (0, 1) for fy in (0, 1) for fc in (0, 1)][1:]
        peers = [(1 - x if fx else x, 1 - y if fy else y, 1 - c if fc else c) for fx, fy, fc in flips]
        mine, sends, arrivals = [], [], []
        for a in range(len(self.arrays)):
            mine.append(pltpu.make_async_copy(x_refs[a].at[me], out_refs[a].at[me], local_sems.at[a]))
            for k, peer in enumerate(peers):
                sends.append(pltpu.make_async_remote_copy(
                    src_ref=x_refs[a].at[_dev_index(*peer)], dst_ref=out_refs[a].at[me],
                    send_sem=send_sems.at[7 * a + k], recv_sem=recv_sems.at[7 * a + k],
                    device_id=peer, device_id_type=pl.DeviceIdType.MESH))
                if not with_arrivals:
                    continue
                arrivals.append(pltpu.make_async_remote_copy(
                    src_ref=x_refs[a].at[me], dst_ref=out_refs[a].at[_dev_index(*peer)],
                    send_sem=send_sems.at[7 * a + k], recv_sem=recv_sems.at[7 * a + k],
                    device_id=peer, device_id_type=pl.DeviceIdType.MESH))
        return mine, sends, arrivals

    def start(self, x_refs, out_refs, sems):
        mine, sends, _ = self._copies(x_refs, out_refs, sems, False)
        for cp in mine + sends:
            cp.start()

    def wait(self, x_refs, out_refs, sems):
        mine, sends, arrivals = self._copies(x_refs, out_refs, sems, True)
        for cp in arrivals:
            cp.wait_recv()
        for cp in sends:
            cp.wait_send()
        for cp in mine:
            cp.wait()


def _sum_blocks(parts, name):
    n, r, c = parts.shape
    row_tiles = [t for t in range(16, min(r, 2048) + 1, 16) if r % t == 0]
    if row_tiles:
        tr, tc = max(row_tiles), c
    else:
        tr, tc = r, 2 * LANES
    assert c % tc == 0

    def body(p_ref, o_ref):
        acc = p_ref[0].astype(F32)
        for s in range(1, n):
            acc = acc + p_ref[s].astype(F32)
        o_ref[...] = acc

    return pl.pallas_call(
        body, name=name, grid=(r // tr, c // tc),
        in_specs=[pl.BlockSpec((n, tr, tc), lambda i, j: (0, i, j))],
        out_specs=pl.BlockSpec((tr, tc), lambda i, j: (i, j)),
        out_shape=jax.ShapeDtypeStruct((r, c), F32),
        compiler_params=_cparams("parallel", "parallel"),
    )(parts)


def _sigmoid(x):
    return 1.0 / (1.0 + jnp.exp(-x))


def _silu(x):
    return x * _sigmoid(x)


def _silu_grad(x):
    s = _sigmoid(x)
    return s * (1.0 + x * (1.0 - s))


def _colsum(x):
    return jnp.sum(x, axis=0, keepdims=True)


def _rms(x):
    return lax.rsqrt(jnp.mean(x * x, axis=-1, keepdims=True) + EPS)


def _rms_bwd(xn, r, dxn):
    return r * (dxn - xn * jnp.mean(dxn * xn, axis=-1, keepdims=True))


def _adamw(w, g, m, v):
    m = ADAM_B1 * m + (1.0 - ADAM_B1) * g
    v = ADAM_B2 * v + (1.0 - ADAM_B2) * jnp.square(g)
    m_hat = m / (1.0 - ADAM_B1 ** ADAM_STEP)
    v_hat = v / (1.0 - ADAM_B2 ** ADAM_STEP)
    delta = -ADAM_LR * (m_hat / (jnp.sqrt(v_hat) + ADAM_EPS) + ADAM_WD * w)
    return delta, m, v


def _adamw_call(w, g, m, v, name):
    r, c = w.shape
    if r % 256 == 0:
        tr, tc = 256, c
    elif r * c <= 256 * 1024 or c % (2 * LANES):
        tr, tc = r, c
    else:
        tr, tc = r, 2 * LANES

    def body(w_ref, g_ref, m_ref, v_ref, d_out, m_out, v_out):
        d_out[...], m_out[...], v_out[...] = _adamw(w_ref[...], g_ref[...], m_ref[...], v_ref[...])

    spec = pl.BlockSpec((tr, tc), lambda i, j: (i, j))
    return pl.pallas_call(
        body, name=name, grid=(r // tr, c // tc), in_specs=[spec] * 4, out_specs=[spec] * 3,
        out_shape=[jax.ShapeDtypeStruct((r, c), F32)] * 3, compiler_params=_cparams("parallel", "parallel"),
    )(w, g, m, v)


def _uq_to_kernel_layout(w):
    lead = w.shape[:-1]
    t = w.reshape(lead + (HEADS, QK_DIM))
    return jnp.concatenate([t[..., :NOPE].reshape(lead + (HEADS * NOPE,)),
                            t[..., NOPE:NOPE + ROPE // 2].reshape(lead + (LANES,)),
                            t[..., NOPE + ROPE // 2:].reshape(lead + (LANES,))], axis=-1)


def _uq_from_kernel_layout(w):
    lead = w.shape[:-1]
    nope = w[..., :HEADS * NOPE].reshape(lead + (HEADS, NOPE))
    r1 = w[..., HEADS * NOPE:HEADS * NOPE + LANES].reshape(lead + (HEADS, ROPE // 2))
    r2 = w[..., HEADS * NOPE + LANES:].reshape(lead + (HEADS, ROPE // 2))
    return jnp.concatenate([nope, r1, r2], axis=-1).reshape(lead + (HEADS * QK_DIM,))


def _ukv_to_kernel_layout(w):
    lead = w.shape[:-1]
    t = w.reshape(lead + (HEADS, NOPE + HD))
    return jnp.concatenate([t[..., :NOPE].reshape(lead + (HEADS * NOPE,)),
                            t[..., NOPE:].reshape(lead + (HEADS * HD,))], axis=-1)


def _ukv_from_kernel_layout(w):
    lead = w.shape[:-1]
    kn = w[..., :HEADS * NOPE].reshape(lead + (HEADS, NOPE))
    vv = w[..., HEADS * NOPE:].reshape(lead + (HEADS, HD))
    return jnp.concatenate([kn, vv], axis=-1).reshape(lead + (HEADS * (NOPE + HD),))


def _w_in_t_to_kernel_layout(wt):
    sb = wt[0:2048]
    c_q = wt[2048:2432]
    c_kv = wt[2432:2688]
    k_rot = wt[2688:2720]
    mla_z = wt[2720:3232]
    gates = wt[3232:5280]
    zeros = jnp.zeros((LANES, wt.shape[1]), wt.dtype)
    k1 = jnp.tile(k_rot[:ROPE // 2], (HEADS, 1))
    k2 = jnp.tile(k_rot[ROPE // 2:], (HEADS, 1))
    return jnp.concatenate([gates, sb, mla_z, c_q, zeros, c_kv, k1, k2], axis=0)


def _w_in_t_from_kernel_layout(gt, g_rot):
    return jnp.concatenate([gt[O_SBQ:O_SBQ + 2048], gt[O_CQ:O_CQ + Q_RANK], gt[O_CKV:O_CKV + KV_RANK],
                            g_rot.astype(gt.dtype), gt[O_MLAZ:O_MLAZ + MLA_W], gt[O_GA:O_GA + 2 * D]], axis=0)


def kernel(x, c, positions, w_ada, b_ada, norm_gain, w_in, q_norm_gain, w_uq, kv_norm_gain, w_ukv, w_branch_a, w_branch_b, w_out, final_norm_gain, loss_target, m_w_ada, m_b_ada, m_norm_gain, m_w_in, m_q_norm_gain, m_w_uq, m_kv_norm_gain, m_w_ukv, m_w_branch_a, m_w_branch_b, m_w_out, m_final_norm_gain, v_w_ada, v_b_ada, v_norm_gain, v_w_in, v_q_norm_gain, v_w_uq, v_kv_norm_gain, v_w_ukv, v_w_branch_a, v_w_branch_b, v_w_out, v_final_norm_gain):
    s_len = x.shape[1]
    me = _dev_index(*_mesh_pos())
    x2d = x[0]
    tgt = loss_target[0]

    w_in_t = w_in[0].T.astype(BF16)
    big = [w_uq[0], w_ukv[0], w_branch_a[0], w_branch_b[0], w_out[0]]
    big_sizes = [int(w.size) for w in big]
    packed = jnp.concatenate([w.astype(BF16).reshape(-1, LANES) for w in big], axis=0)
    g_in_t, c_all = _all_gather_parts([w_in_t, c.reshape(8, LANES)], "gather_w_in")
    c_all = c_all.reshape(N_DEV, D)
    w_in_kt = _w_in_t_to_kernel_layout(g_in_t.reshape(N_DEV * w_in_t.shape[0], D))

    mod_cols = _mm(c_all, w_ada[0], name="ada_mod")
    mod_all = _all_gather(mod_cols, "gather_mod")
    mod = lax.dynamic_index_in_dim(mod_all, me, axis=1, keepdims=False).reshape(1, 3 * D)
    mod_shift, mod_scale, mod_gate = mod[:, :D], mod[:, D:2 * D], mod[:, 2 * D:]
    b_shift, b_scale, b_gate = b_ada[:, :D], b_ada[:, D:2 * D], b_ada[:, 2 * D:]
    g1 = norm_gain
    gq, gkv = q_norm_gain, kv_norm_gain
    gf = final_norm_gain.reshape(1, D)

    def f_h(x_, g1_, ms, bs, msc, bsc):
        xn = x_ * _rms(x_)
        return (xn * g1_ * (1.0 + (msc + bsc)) + (ms + bs),), ()

    (h,) = _rowwise(f_h, [x2d], [g1, mod_shift, b_shift, mod_scale, b_scale], [(D, BF16)], name="ada_norm")
    proj, (gathered,) = _mm(h, w_in_kt, tb=True, name="proj_in", tiles=(min(s_len, 1024), IN_PAD // 2, D),
                            exchange=_Exchange([jnp.broadcast_to(packed[None], (N_DEV,) + packed.shape)]))
    offs = [0]
    for n in big_sizes:
        offs.append(offs[-1] + n // LANES)

    def unpack(t, shape):
        return gathered[:, offs[t]:offs[t + 1], :].reshape((N_DEV,) + shape)

    def cols(t, shape):
        return unpack(t, shape).transpose(1, 0, 2).reshape(shape[0], N_DEV * shape[1])

    w_uq_k = _uq_to_kernel_layout(cols(0, big[0].shape))
    w_ukv_k = _ukv_to_kernel_layout(cols(1, big[1].shape))
    w_a_f = cols(2, big[2].shape)
    w_b_f = cols(3, big[3].shape)
    w_out_f = unpack(4, big[4].shape).reshape(D, D)

    o_a, sb_tails = _sb_forward(proj, s_len)

    def f_lat(cq, ckv, gq_, gkv_):
        return (cq * _rms(cq) * gq_, ckv * _rms(ckv) * gkv_), ()

    cq_n, ckv_n = _rowwise(f_lat, [(proj, O_CQ // Q_RANK, Q_RANK), (proj, O_CKV // KV_RANK, KV_RANK)], [gq, gkv],
                           [(Q_RANK, BF16), (KV_RANK, BF16)], name="latent_norm")
    q_mla = _mm(cq_n, w_uq_k, name="q_up")
    kv = _mm(ckv_n, w_ukv_k, name="kv_up")

    inv_freq = ROPE_BASE ** (-jnp.arange(0, ROPE, 2, dtype=F32) / ROPE)
    inv_freq_t = jnp.tile(inv_freq, HEADS).reshape(1, LANES)
    pos_col = positions.reshape(s_len, 1).astype(F32)

    pairs = HEADS // 2

    def f_rope(pos, qn, q1, q2, kn, k1, k2, freq):
        ang = pos * freq
        cs, sn = jnp.cos(ang), jnp.sin(ang)
        q1r, q2r = q1 * cs - q2 * sn, q1 * sn + q2 * cs
        k1r, k2r = k1 * cs - k2 * sn, k1 * sn + k2 * cs
        lane = lax.broadcasted_iota(jnp.int32, q1.shape, 1)
        first, second = lane < ROPE, (lane >= ROPE) & (lane < 2 * ROPE)
        k_rot = jnp.where(first, k1r, jnp.where(second, k2r, 0.0))
        q_parts, k_parts = [], []
        for p in range(pairs):
            q_rot = jnp.where(first, pltpu.roll(q1r, (LANES - ROPE * p) % LANES, 1),
                              jnp.where(second, pltpu.roll(q2r, (LANES + ROPE - ROPE * p) % LANES, 1), 0.0))
            q_parts += [qn[:, LANES * p:LANES * (p + 1)], q_rot]
            k_parts += [kn[:, LANES * p:LANES * (p + 1)], k_rot]
        return (jnp.concatenate(q_parts, axis=1), jnp.concatenate(k_parts, axis=1), cs, sn), ()

    q_cat, k_cat, cos_t, sin_t = _rowwise(
        f_rope, [pos_col, (q_mla, 0, MLA_W), (q_mla, 4, LANES), (q_mla, 5, LANES), (kv, 0, MLA_W),
                 (proj, O_KROT // LANES, LANES), (proj, O_KROT // LANES + 1, LANES)], [inv_freq_t],
        [(2 * MLA_W, BF16), (2 * MLA_W, BF16), (LANES, F32), (LANES, F32)], name="rope")

    o_b, lse = _mla_forward(q_cat, k_cat, kv, s_len)

    def f_gate(oa, za, ob, zb):
        return (oa * _silu(za), ob * _silu(zb)), ()

    ya_in, yb_in = _rowwise(f_gate, [o_a, (proj, O_SBZ // SB_W, SB_W), o_b, (proj, O_MLAZ // MLA_W, MLA_W)], [],
                            [(SB_W, BF16), (MLA_W, BF16)], name="branch_gate")
    y_a = _mm(ya_in, w_a_f, name="branch_a")

    def f_merge(yb, ga, gb, ya):
        return (yb, _sigmoid(ga) * ya + _sigmoid(gb) * yb), ()

    y_b, merged = _mm(yb_in, w_b_f, name="branch_b_merge", tiles=(min(s_len, 512), D, MLA_W),
                      epilogue=(f_merge, [(proj, O_GA // D, D), (proj, O_GB // D, D), y_a], [], [(D, F32), (D, BF16)], []))

    def f_loss(out_, x_, t_, mg, bg, gf_):
        gate = mg + bg
        x2 = x_ + gate * out_
        r2 = _rms(x2)
        xn2 = x2 * r2
        err = xn2 * gf_ - t_
        loss = jnp.full((1, LANES), 0.5 / D, F32) * jnp.sum(err * err)
        dy = err * (1.0 / D)
        dx2 = _rms_bwd(xn2, r2, dy * gf_)
        return (dx2, dx2 * gate), (loss, _colsum(dy * xn2), _colsum(dx2 * out_))

    dx2, d_out, loss_part, d_gf, d_gate = _mm(
        merged, w_out_f, name="out_proj_loss", tiles=(min(s_len, 512), D, D),
        epilogue=(f_loss, [x2d, tgt], [mod_gate, b_gate, gf], [(D, F32), (D, BF16)], [LANES, D, D]))

    dw_out = _mm(merged, d_out, ta=True, name="dw_out")

    def f_dmerge(dm, ga, gb, ya, yb):
        sa, sb = _sigmoid(ga), _sigmoid(gb)
        return (dm * sa, dm * sb, dm * ya * sa * (1.0 - sa), dm * yb * sb * (1.0 - sb)), ()

    d_ya, d_yb, d_ga, d_gb = _mm(
        d_out, w_out_f, tb=True, name="d_merge", tiles=(min(s_len, 256), D, D),
        epilogue=(f_dmerge, [(proj, O_GA // D, D), (proj, O_GB // D, D), y_a, y_b], [], [(D, BF16)] * 4, []))
    dw_a = _mm(ya_in, d_ya, ta=True, name="dw_branch_a")
    dw_b = _mm(yb_in, d_yb, ta=True, name="dw_branch_b")

    def f_dgate(d_in, o_, z_):
        return (d_in * _silu(z_), d_in * o_ * _silu_grad(z_)), ()

    d_oa, d_sbz = _mm(d_ya, w_a_f, tb=True, name="d_branch_a",
                      epilogue=(f_dgate, [o_a, (proj, O_SBZ // SB_W, SB_W)], [], [(SB_W, F32), (SB_W, BF16)], []))
    d_ob, d_mlaz = _mm(d_yb, w_b_f, tb=True, name="d_branch_b",
                       epilogue=(f_dgate, [o_b, (proj, O_MLAZ // MLA_W, MLA_W)], [], [(MLA_W, F32), (MLA_W, BF16)], []))

    dq_cat, dk_cat, dv_b = _mla_backward(q_cat, k_cat, kv, o_b, lse, d_ob, s_len)

    def f_drope(dq, dk, dv_, cs, sn):
        lane = lax.broadcasted_iota(jnp.int32, cs.shape, 1)
        first, second = lane < ROPE, (lane >= ROPE) & (lane < 2 * ROPE)
        dq1 = dq2 = dk1 = dk2 = None
        for p in range(pairs):
            q_rot = dq[:, LANES * (2 * p + 1):LANES * (2 * p + 2)]
            k_rot = dk[:, LANES * (2 * p + 1):LANES * (2 * p + 2)]
            parts = (pltpu.roll(jnp.where(first, q_rot, 0.0), (ROPE * p) % LANES, 1),
                     pltpu.roll(jnp.where(second, q_rot, 0.0), (LANES - ROPE + ROPE * p) % LANES, 1),
                     jnp.where(first, k_rot, 0.0), jnp.where(second, k_rot, 0.0))
            if p == 0:
                dq1, dq2, dk1, dk2 = parts
            else:
                dq1, dq2, dk1, dk2 = dq1 + parts[0], dq2 + parts[1], dk1 + parts[2], dk2 + parts[3]
        dqn_ = [dq[:, 2 * LANES * p:2 * LANES * p + LANES] for p in range(pairs)]
        dkn_ = [dk[:, 2 * LANES * p:2 * LANES * p + LANES] for p in range(pairs)]
        return (jnp.concatenate(dqn_ + [dq1 * cs + dq2 * sn, dq2 * cs - dq1 * sn], axis=1),
                jnp.concatenate(dkn_ + [dv_], axis=1),
                jnp.concatenate([dk1 * cs + dk2 * sn, dk2 * cs - dk1 * sn], axis=1)), ()

    dq_k, dkv_k, d_krot = _rowwise(f_drope, [dq_cat, dk_cat, dv_b, cos_t, sin_t], [],
                                   [(HEADS * QK_DIM, BF16), (2 * MLA_W, BF16), (2 * LANES, BF16)], name="d_rope")
    dw_uq_k = _mm(cq_n, dq_k, ta=True, name="dw_uq")
    dw_ukv_k = _mm(ckv_n, dkv_k, ta=True, name="dw_ukv")

    def f_dlat(d_normed, latent, gain):
        r = _rms(latent)
        normed = latent * r
        return (_rms_bwd(normed, r, d_normed * gain),), (_colsum(d_normed * normed),)

    d_cq, d_gq = _mm(dq_k, w_uq_k, tb=True, name="d_cq_norm",
                     epilogue=(f_dlat, [(proj, O_CQ // Q_RANK, Q_RANK)], [gq], [(Q_RANK, BF16)], [Q_RANK]))
    d_ckv, d_gkv = _mm(dkv_k, w_ukv_k, tb=True, name="d_ckv_norm",
                       epilogue=(f_dlat, [(proj, O_CKV // KV_RANK, KV_RANK)], [gkv], [(KV_RANK, BF16)], [KV_RANK]))

    def col_blocks(g):
        kdim, n8 = g.shape
        return g.astype(BF16).reshape(kdim, N_DEV, n8 // N_DEV).transpose(1, 0, 2).reshape(N_DEV, -1, LANES)

    g_blocks = jnp.concatenate([col_blocks(_uq_from_kernel_layout(dw_uq_k)), col_blocks(_ukv_from_kernel_layout(dw_ukv_k)),
                                col_blocks(dw_a), col_blocks(dw_b), dw_out.astype(BF16).reshape(N_DEV, -1, LANES)], axis=1)
    (d_sbq, d_sbk, d_sbv), (g_recv,) = _sb_backward(proj, sb_tails, d_oa, s_len, _Exchange([g_blocks]))

    d_proj = jnp.concatenate([d_ga, d_gb, d_sbq.astype(BF16), d_sbk.astype(BF16), d_sbv.astype(BF16), d_sbz, d_mlaz,
                              d_cq, jnp.zeros((s_len, LANES), BF16), d_ckv, d_krot], axis=1)
    dw_in_kt = _mm(d_proj, h, ta=True, out_dtype=BF16, name="dw_in", tiles=(512, D, s_len))

    def krot_body(t_ref, o_ref):
        half = ROPE // 2
        for part in range(2):
            acc = t_ref[part * LANES:part * LANES + half, :].astype(F32)
            for hh in range(1, HEADS):
                acc = acc + t_ref[part * LANES + hh * half:part * LANES + (hh + 1) * half, :].astype(F32)
            o_ref[part * half:(part + 1) * half, :] = acc

    dw_krot = pl.pallas_call(krot_body, name="dw_krot_sum", out_shape=jax.ShapeDtypeStruct((ROPE, D), F32))(
        dw_in_kt[O_KROT:O_KROT + 2 * LANES])

    g_in_blocks = _w_in_t_from_kernel_layout(dw_in_kt, dw_krot).reshape(N_DEV, -1, D)
    def f_dx(dh_, x_, dx2_, g1_, msc, bsc):
        r = _rms(x_)
        xn = x_ * r
        dn1 = dh_ * (1.0 + (msc + bsc))
        return ((dx2_ + _rms_bwd(xn, r, dn1 * g1_),),
                (_colsum(dh_), _colsum(dh_ * (xn * g1_)), _colsum(dn1 * xn)))

    (grad_x2d, d_shift, d_scale, d_g1), (g_in_recv,) = _mm(
        d_proj, w_in_kt, name="d_h", tiles=(min(s_len, 512), D, 512), exchange=_Exchange([g_in_blocks]),
        epilogue=(f_dx, [x2d, dx2], [g1, mod_scale, b_scale], [(D, F32)], [D, D, D]))

    g_in_sum_t = _sum_blocks(g_in_recv, "sum_grads_w_in")
    g_sum = _sum_blocks(g_recv, "sum_grads")
    g_big = [g_sum[offs[t]:offs[t + 1]].reshape(big[t].shape) for t in range(5)]

    small = jnp.concatenate([d_shift, d_scale, d_gate, d_g1, d_gq, d_gkv, d_gf, loss_part], axis=1)
    n_small = small.shape[1]
    pad = (-n_small) % (8 * LANES)
    small = jnp.pad(small, ((0, 0), (0, pad))).reshape(-1, LANES)
    small_all = _all_gather(small, "gather_small")
    small_sum = _sum_blocks(small_all, "sum_small").reshape(1, -1)
    g_b_ada = small_sum[:, :3 * D]
    g_g1 = small_sum[:, 3 * D:4 * D]
    g_gq = small_sum[:, 4 * D:4 * D + Q_RANK]
    g_gkv = small_sum[:, 4 * D + Q_RANK:4 * D + Q_RANK + KV_RANK]
    g_gf = small_sum[:, 4 * D + Q_RANK + KV_RANK:4 * D + Q_RANK + KV_RANK + D]

    dmod_all = small_all.reshape(N_DEV, -1)[:, :3 * D]
    dmod_cols = lax.dynamic_slice_in_dim(dmod_all, me * (3 * D // N_DEV), 3 * D // N_DEV, axis=1)
    g_w_ada = _mm(c_all, dmod_cols, ta=True, name="dw_ada")

    loss = small_sum[0, n_small - LANES]

    names = ["w_ada", "b_ada", "norm_gain", "w_in", "q_norm_gain", "w_uq", "kv_norm_gain", "w_ukv",
             "w_branch_a", "w_branch_b", "w_out", "final_norm_gain"]
    weights = dict(w_ada=w_ada, b_ada=b_ada, norm_gain=norm_gain, w_in=w_in, q_norm_gain=q_norm_gain, w_uq=w_uq,
                   kv_norm_gain=kv_norm_gain, w_ukv=w_ukv, w_branch_a=w_branch_a, w_branch_b=w_branch_b, w_out=w_out,
                   final_norm_gain=final_norm_gain)
    moms = dict(w_ada=m_w_ada, b_ada=m_b_ada, norm_gain=m_norm_gain, w_in=m_w_in, q_norm_gain=m_q_norm_gain,
                w_uq=m_w_uq, kv_norm_gain=m_kv_norm_gain, w_ukv=m_w_ukv, w_branch_a=m_w_branch_a,
                w_branch_b=m_w_branch_b, w_out=m_w_out, final_norm_gain=m_final_norm_gain)
    vels = dict(w_ada=v_w_ada, b_ada=v_b_ada, norm_gain=v_norm_gain, w_in=v_w_in, q_norm_gain=v_q_norm_gain,
                w_uq=v_w_uq, kv_norm_gain=v_kv_norm_gain, w_ukv=v_w_ukv, w_branch_a=v_w_branch_a,
                w_branch_b=v_w_branch_b, w_out=v_w_out, final_norm_gain=v_final_norm_gain)
    grads2d = dict(w_ada=g_w_ada, b_ada=g_b_ada, norm_gain=g_g1, w_in=g_in_sum_t, q_norm_gain=g_gq, w_uq=g_big[0],
                   kv_norm_gain=g_gkv, w_ukv=g_big[1], w_branch_a=g_big[2], w_branch_b=g_big[3], w_out=g_big[4],
                   final_norm_gain=g_gf)

    grads, deltas, new_m, new_v = [], [], [], []
    for n in names:
        w = weights[n]
        if n == "w_in":
            to2d = lambda t: t[0].T
            back = lambda t: t.T[None]
        else:
            shape2d = grads2d[n].shape
            to2d = lambda t, s=shape2d: t.reshape(s)
            back = lambda t, s=w.shape: t.reshape(s)
        d_, m_, v_ = _adamw_call(to2d(w), grads2d[n], to2d(moms[n]), to2d(vels[n]), "adamw_" + n)
        grads.append(back(grads2d[n]))
        deltas.append(back(d_))
        new_m.append(back(m_))
        new_v.append(back(v_))

    return (loss, grad_x2d.reshape(x.shape), *grads, *deltas, *new_m, *new_v)
```

```python
import functools
import math

import jax
import jax.numpy as jnp
from jax import lax
from jax.experimental import pallas as pl
from jax.experimental.pallas import tpu as pltpu

F32 = jnp.float32
BF16 = jnp.bfloat16
MXU_DTYPE = jnp.bfloat16

N_DEV = 8
D = 1024
HEADS = 8
HD = 64
SB_W = 512
MLA_W = 512
Q_RANK = 384
KV_RANK = 256
ROPE = 32
NOPE = 64
QK_DIM = NOPE + ROPE
EPS = 1e-6
ROPE_BASE = 10000.0

ADAM_LR = 0.001
ADAM_B1 = 0.9
ADAM_B2 = 0.999
ADAM_EPS = 1e-08
ADAM_WD = 0.01
ADAM_STEP = 10

LANES = 128
VMEM_LIMIT = 48 * 1024 * 1024

O_GA, O_GB = 0, 1024
O_SBQ, O_SBK, O_SBV, O_SBZ = 2048, 2560, 3072, 3584
O_MLAZ = 4096
O_CQ = 4608
O_CKV = 5120
O_KROT = 5376
IN_PAD = 5632

BQ = 256
BK = 256


def _cparams(*sem):
    return pltpu.CompilerParams(dimension_semantics=sem, vmem_limit_bytes=VMEM_LIMIT)


def _tile_of(n, cap=512):
    if n <= cap:
        return n
    for t in (1024, 768, 512, 384, 256, 128):
        if t <= cap and n % t == 0:
            return t
    raise ValueError(n)


def _rowwise(fn, rows, vecs, outs, reds=(), *, name, tile=512):
    norm = []
    for r in rows:
        if isinstance(r, tuple):
            arr, cb, w = r[:3]
            ro = r[3] if len(r) > 3 else 0
        else:
            arr, cb, w, ro = r, 0, r.shape[1], 0
        norm.append((arr, cb, w, ro))
    s_len = norm[0][0].shape[0]
    tile = min(tile, s_len)
    assert s_len % tile == 0
    n_row, n_vec, n_out, n_red = len(norm), len(vecs), len(outs), len(reds)

    def body(*refs):
        step = pl.program_id(0)
        row_refs = refs[:n_row]
        vec_refs = refs[n_row:n_row + n_vec]
        out_refs = refs[n_row + n_vec:n_row + n_vec + n_out]
        red_refs = refs[n_row + n_vec + n_out:]
        row_res, red_res = fn(*[r[...] for r in row_refs], *[v[...] for v in vec_refs])
        for o, val in zip(out_refs, row_res):
            o[...] = val.astype(o.dtype)
        if n_red:
            @pl.when(step == 0)
            def _():
                for r in red_refs:
                    r[...] = jnp.zeros(r.shape, r.dtype)
            for r, val in zip(red_refs, red_res):
                r[...] += val

    in_specs = []
    for arr, cb, w, ro in norm:
        in_specs.append(pl.BlockSpec((tile, w), functools.partial(lambda i, cb, rb: (i + rb, cb), cb=cb, rb=ro // tile)))
        assert ro % tile == 0
    for v in vecs:
        in_specs.append(pl.BlockSpec(v.shape, lambda i: (0, 0)))
    out_shape = [jax.ShapeDtypeStruct((s_len, w), dt) for w, dt in outs]
    out_specs = [pl.BlockSpec((tile, w), lambda i: (i, 0)) for w, _ in outs]
    out_shape += [jax.ShapeDtypeStruct((1, w), F32) for w in reds]
    out_specs += [pl.BlockSpec((1, w), lambda i: (0, 0)) for w in reds]
    res = pl.pallas_call(
        body, name=name, grid=(s_len // tile,), in_specs=in_specs, out_specs=out_specs, out_shape=out_shape,
        compiler_params=_cparams("arbitrary" if n_red else "parallel"),
    )(*[a for a, _, _, _ in norm], *vecs)
    return res


def _mm(a, b, *, ta=False, tb=False, out_dtype=F32, name, exchange=None, tiles=None, epilogue=None):
    m, k = (a.shape[1], a.shape[0]) if ta else a.shape
    n = b.shape[0] if tb else b.shape[1]
    assert (b.shape[1] if tb else b.shape[0]) == k
    tm, tn, tk = tiles or (_tile_of(m, 1024), _tile_of(n, 1024 if n <= 1024 else 512), _tile_of(k, 1024))
    assert m % tm == 0 and n % tn == 0 and k % tk == 0
    ni, nj, nk = m // tm, n // tn, k // tk
    dims = (((0 if ta else 1,), (1 if tb else 0,)), ((), ()))
    n_ex = len(exchange.arrays) if exchange else 0
    fn, rows, vecs, outs, reds = epilogue or (None, (), (), (), ())
    rows = [r if isinstance(r, tuple) else (r, 0, r.shape[1]) for r in rows]
    assert not epilogue or tn == n
    n_res = len(outs) + len(reds) if epilogue else 1

    def body(*refs):
        a_ref, b_ref = refs[:2]
        row_refs, refs = refs[2:2 + len(rows)], refs[2 + len(rows):]
        vec_refs, refs = refs[:len(vecs)], refs[len(vecs):]
        x_refs, refs = refs[:n_ex], refs[n_ex:]
        res_refs, refs = refs[:n_res], refs[n_res:]
        out_refs, acc_ref, sems = refs[:n_ex], refs[n_ex], refs[n_ex + 1:]
        i, j, kk = pl.program_id(0), pl.program_id(1), pl.program_id(2)
        first = (i == 0) & (j == 0) & (kk == 0)

        if exchange:
            @pl.when(first)
            def _():
                exchange.start(x_refs, out_refs, sems)

        if reds:
            @pl.when(first)
            def _():
                for r in res_refs[len(outs):]:
                    r[...] = jnp.zeros(r.shape, r.dtype)

        @pl.when(kk == 0)
        def _():
            acc_ref[...] = jnp.zeros(acc_ref.shape, F32)

        acc_ref[...] += lax.dot_general(a_ref[...].astype(MXU_DTYPE), b_ref[...].astype(MXU_DTYPE), dims,
                                        preferred_element_type=F32)

        @pl.when(kk == nk - 1)
        def _():
            if not epilogue:
                res_refs[0][...] = acc_ref[...].astype(res_refs[0].dtype)
                return
            row_res, red_res = fn(acc_ref[...], *[r[...] for r in row_refs], *[v[...] for v in vec_refs])
            for o, val in zip(res_refs, row_res):
                o[...] = val.astype(o.dtype)
            for r, val in zip(res_refs[len(outs):], red_res):
                r[...] += val

        if exchange:
            @pl.when((i == ni - 1) & (j == nj - 1) & (kk == nk - 1))
            def _():
                exchange.wait(x_refs, out_refs, sems)

    a_spec = pl.BlockSpec((tk, tm), lambda i, j, kk: (kk, i)) if ta else pl.BlockSpec((tm, tk), lambda i, j, kk: (i, kk))
    b_spec = pl.BlockSpec((tn, tk), lambda i, j, kk: (j, kk)) if tb else pl.BlockSpec((tk, tn), lambda i, j, kk: (kk, j))
    in_specs = [a_spec, b_spec]
    in_specs += [pl.BlockSpec((tm, w), functools.partial(lambda i, j, kk, cb: (i, cb), cb=cb)) for _, cb, w in rows]
    in_specs += [pl.BlockSpec(v.shape, lambda i, j, kk: (0, 0)) for v in vecs]
    if epilogue:
        res_specs = [pl.BlockSpec((tm, w), lambda i, j, kk: (i, 0)) for w, _ in outs]
        res_specs += [pl.BlockSpec((1, w), lambda i, j, kk: (0, 0)) for w in reds]
        res_shape = [jax.ShapeDtypeStruct((m, w), dt) for w, dt in outs] + [jax.ShapeDtypeStruct((1, w), F32) for w in reds]
    else:
        res_specs = [pl.BlockSpec((tm, tn), lambda i, j, kk: (i, j))]
        res_shape = [jax.ShapeDtypeStruct((m, n), out_dtype)]
    ordered = bool(exchange or reds)
    res = pl.pallas_call(
        body, name=name, grid=(ni, nj, nk),
        in_specs=in_specs + (exchange.in_specs if exchange else []),
        out_specs=res_specs + (exchange.out_specs if exchange else []),
        out_shape=res_shape + (exchange.out_shape if exchange else []),
        scratch_shapes=[pltpu.VMEM((tm, tn), F32)] + (exchange.scratch if exchange else []),
        compiler_params=_cparams(*(("arbitrary",) * 3 if ordered else ("parallel", "parallel", "arbitrary"))),
    )(a, b, *[r[0] for r in rows], *vecs, *(exchange.arrays if exchange else []))
    main = res[:n_res] if epilogue else res[0]
    return (main, res[n_res:]) if exchange else main


_NT = (((1,), (1,)), ((), ()))
_TN = (((0,), (0,)), ((), ()))


def _dot(a, b):
    return jnp.dot(a, b, preferred_element_type=F32)


def _dot_nt(a, b):
    return lax.dot_general(a, b, _NT, preferred_element_type=F32)


def _dot_tn(a, b):
    return lax.dot_general(a, b, _TN, preferred_element_type=F32)


def _running_sum(x, tri):
    return _dot(x.astype(MXU_DTYPE), tri)


def _neg_softplus(z):
    u = jnp.exp2(jnp.abs(z) * (-1.0 / math.log(2.0)))
    return -jnp.maximum(z, 0.0) - jnp.log(1.0 + u)


def _walk_blocks(step, st, n, descending, group=2):
    done = 0
    size = group
    while size >= 1:
        def trip(t, s, size=size, done=done):
            js = [done + size * t + g for g in range(size)]
            return step([n - 1 - j for j in js] if descending else js, s)

        trips = (n - done) // size
        st = lax.fori_loop(0, trips, trip, st)
        done = done + size * trips
        size //= 2
    return st


def _chains(js):
    return [(h, t) for t in range(len(js)) for h in range(2)]


def _rowsum(x):
    return jnp.sum(x, axis=1, keepdims=True)


def _attn_consts():
    row = lax.broadcasted_iota(jnp.int32, (BQ, BK), 0)
    col = lax.broadcasted_iota(jnp.int32, (BQ, BK), 1)
    trow = lax.broadcasted_iota(jnp.int32, (BK, BK), 0)
    tcol = lax.broadcasted_iota(jnp.int32, (BK, BK), 1)
    lane = lax.broadcasted_iota(jnp.int32, (BQ, LANES), 1)
    klane = lax.broadcasted_iota(jnp.int32, (BK, LANES), 1)
    return row, col, trow, tcol, lane, klane


DIAG = BQ // BK
assert BQ == DIAG * BK


def _diag_block(i):
    return i * DIAG


def _diag_blocks(jd, descending):
    places = list(reversed(range(DIAG))) if descending else list(range(DIAG))
    return [jd + g for g in places], places


def _key_slice(j):
    return pl.ds(pl.multiple_of(j * BK, BK), BK)


def _sb_forward(proj, s_len, exchange):
    nq = s_len // BQ
    assert s_len // BK <= HD
    qc, kc, vc = O_SBQ // LANES, O_SBK // LANES, O_SBV // LANES
    n_ex = len(exchange.arrays)

    def body(q_ref, k_ref, v_ref, *rest):
        x_refs, (o_ref, tails_ref) = rest[:n_ex], rest[n_ex:n_ex + 2]
        out_refs, sems = rest[n_ex + 2:2 * n_ex + 2], rest[2 * n_ex + 2:]
        p = pl.program_id(0)
        i = pl.program_id(1)
        jd = _diag_block(i)

        @pl.when((p == 0) & (i == 0))
        def _():
            exchange.start(x_refs, out_refs, sems)

        row, col, trow, tcol, lane, klane = _attn_consts()
        strict = [col + g * BK < row for g in range(DIAG)]
        tri = (trow >= tcol).astype(MXU_DTYPE)
        q = q_ref[...] * 0.125
        qh = [jnp.where(lane < HD, q, 0.0).astype(MXU_DTYPE), jnp.where(lane >= HD, q, 0.0).astype(MXU_DTYPE)]

        km = [klane < HD, klane >= HD]

        def step(js, st, diag):
            carry, acc, tail = st
            chains = _chains(js)
            kj = [k_ref[_key_slice(j), :].astype(MXU_DTYPE) for j in js]
            vj = [v_ref[_key_slice(j), :].astype(MXU_DTYPE) for j in js]
            z = {(h, t): _dot_nt(qh[h], kj[t]) for h, t in chains}
            run = list(carry)
            suf, carry_in = {}, {}
            for h, t in chains:
                lom = _neg_softplus(z[h, t])
                if diag:
                    lom = jnp.where(strict[diag[t]], lom, 0.0)
                suf[h, t] = _running_sum(lom, tri)
                carry_in[h, t] = run[h]
                run[h] = run[h] + _rowsum(lom)
            for h, t in chains:
                a = jnp.exp(z[h, t] + suf[h, t] + carry_in[h, t])
                if diag:
                    a = jnp.where(strict[diag[t]], a, 0.0)
                acc = acc + _dot(a.astype(MXU_DTYPE), jnp.where(km[h], vj[t], 0))
                tail = jnp.where(lane == h * HD + js[t], carry_in[h, t], tail)
            return tuple(run), acc, tail

        zero = jnp.zeros((BQ, LANES), F32)
        diag_js, places = _diag_blocks(jd, True)
        st = step(diag_js, ((jnp.zeros((BQ, 1), F32),) * 2, zero, zero), places)
        st = _walk_blocks(lambda js, s: step(js, s, False), st, jd, True, group=8)
        o_ref[...] = st[1]
        tails_ref[...] = st[2]

        @pl.when((p == HEADS // 2 - 1) & (i == nq - 1))
        def _():
            exchange.wait(x_refs, out_refs, sems)

    blk = pl.BlockSpec((BQ, LANES), lambda p, i: (i, p))
    out = jax.ShapeDtypeStruct((s_len, SB_W), F32)
    res = pl.pallas_call(
        body, name="sb_fwd", grid=(HEADS // 2, nq),
        in_specs=[pl.BlockSpec((BQ, LANES), lambda p, i: (i, qc + p)),
                  pl.BlockSpec((s_len, LANES), lambda p, i: (0, kc + p)),
                  pl.BlockSpec((s_len, LANES), lambda p, i: (0, vc + p))] + exchange.in_specs,
        out_specs=[blk, blk] + exchange.out_specs, out_shape=[out, out] + exchange.out_shape,
        scratch_shapes=exchange.scratch,
        compiler_params=_cparams("arbitrary", "arbitrary"),
    )(proj, proj, proj, *exchange.arrays)
    return res[:2], res[2:]


def _sb_backward(proj, tails, do, s_len, exchange):
    nq = s_len // BQ
    qc, kc, vc = O_SBQ // LANES, O_SBK // LANES, O_SBV // LANES
    n_ex = len(exchange.arrays)

    def body(q_ref, k_ref, v_ref, tails_ref, do_ref, *rest):
        x_refs, (dq_ref, dk_ref, dv_ref) = rest[:n_ex], rest[n_ex:n_ex + 3]
        out_refs, sems = rest[n_ex + 3:2 * n_ex + 3], rest[2 * n_ex + 3:]
        p = pl.program_id(0)
        i = pl.program_id(1)
        jd = _diag_block(i)

        @pl.when((p == 0) & (i == 0))
        def _():
            exchange.start(x_refs, out_refs, sems)

        @pl.when(i == 0)
        def _():
            dk_ref[...] = jnp.zeros(dk_ref.shape, F32)
            dv_ref[...] = jnp.zeros(dv_ref.shape, F32)

        row, col, trow, tcol, lane, klane = _attn_consts()
        strict = [col + g * BK < row for g in range(DIAG)]
        tri = (trow >= tcol).astype(MXU_DTYPE)
        tri_p = (trow <= tcol).astype(MXU_DTYPE)
        q = q_ref[...] * 0.125
        tails_blk = tails_ref[...]
        do_blk = do_ref[...]
        hm = [lane < HD, lane >= HD]
        km = [klane < HD, klane >= HD]
        qh = [jnp.where(m, q, 0.0).astype(MXU_DTYPE) for m in hm]
        doh = [jnp.where(m, do_blk, 0.0).astype(MXU_DTYPE) for m in hm]

        def step(js, st, diag):
            before, dq = st
            chains = _chains(js)
            kj = [k_ref[_key_slice(j), :].astype(MXU_DTYPE) for j in js]
            vj = [v_ref[_key_slice(j), :].astype(MXU_DTYPE) for j in js]
            z = {(h, t): _dot_nt(qh[h], kj[t]) for h, t in chains}
            da = {(h, t): _dot_nt(doh[h], vj[t]) for h, t in chains}
            suf, sig = {}, {}
            for h, t in chains:
                lom = _neg_softplus(z[h, t])
                if diag:
                    lom = jnp.where(strict[diag[t]], lom, 0.0)
                suf[h, t] = _running_sum(lom, tri)
                sig[h, t] = jnp.exp(z[h, t] + lom)
            run = list(before)
            dl, pre, before_in = {}, {}, {}
            dk_add, dv_add = [None] * len(js), [None] * len(js)
            for h, t in chains:
                tail = _rowsum(jnp.where(lane == h * HD + js[t], tails_blk, 0.0))
                a = jnp.exp(z[h, t] + suf[h, t] + tail)
                if diag:
                    a = jnp.where(strict[diag[t]], a, 0.0)
                dl[h, t] = da[h, t] * a
                pre[h, t] = _dot(dl[h, t].astype(MXU_DTYPE), tri_p)
                dv_h = _dot_tn(a.astype(MXU_DTYPE), doh[h])
                dv_add[t] = dv_h if dv_add[t] is None else dv_add[t] + dv_h
                before_in[h, t] = run[h]
                run[h] = run[h] + _rowsum(dl[h, t])
            for h, t in chains:
                upto = before_in[h, t] + pre[h, t]
                dz = dl[h, t] - sig[h, t] * upto
                if diag:
                    dz = jnp.where(strict[diag[t]], dz, 0.0)
                dzb = dz.astype(MXU_DTYPE)
                dq = dq + _dot(dzb, jnp.where(km[h], kj[t], 0))
                dk_h = _dot_tn(dzb, qh[h])
                dk_add[t] = dk_h if dk_add[t] is None else dk_add[t] + dk_h
            for t, j in enumerate(js):
                dk_ref[_key_slice(j), :] += dk_add[t]
                dv_ref[_key_slice(j), :] += dv_add[t]
            return tuple(run), dq

        st = ((jnp.zeros((BQ, 1), F32),) * 2, jnp.zeros((BQ, LANES), F32))
        st = _walk_blocks(lambda js, s: step(js, s, False), st, jd, False, group=4)
        diag_js, places = _diag_blocks(jd, False)
        st = step(diag_js, st, places)
        dq_ref[...] = st[1] * 0.125

        @pl.when((p == HEADS // 2 - 1) & (i == nq - 1))
        def _():
            exchange.wait(x_refs, out_refs, sems)

    blk = pl.BlockSpec((BQ, LANES), lambda p, i: (i, p))
    full = pl.BlockSpec((s_len, LANES), lambda p, i: (0, p))
    out = jax.ShapeDtypeStruct((s_len, SB_W), F32)
    res = pl.pallas_call(
        body, name="sb_bwd", grid=(HEADS // 2, nq),
        in_specs=[pl.BlockSpec((BQ, LANES), lambda p, i: (i, qc + p)),
                  pl.BlockSpec((s_len, LANES), lambda p, i: (0, kc + p)),
                  pl.BlockSpec((s_len, LANES), lambda p, i: (0, vc + p)),
                  blk, blk] + exchange.in_specs,
        out_specs=[blk, full, full] + exchange.out_specs, out_shape=[out, out, out] + exchange.out_shape,
        scratch_shapes=exchange.scratch,
        compiler_params=_cparams("arbitrary", "arbitrary"),
    )(proj, proj, proj, tails, do, *exchange.arrays)
    return res[:3], res[3:]


def _pair_mask(rows, h):
    lane = lax.broadcasted_iota(jnp.int32, (rows, 2 * LANES), 1)
    rot = lane - LANES
    return (((lane < LANES) & (lane // HD == h))
            | ((lane >= LANES) & (rot < 2 * ROPE) & ((rot // (ROPE // 2)) % 2 == h)))


def _mla_forward(q_cat, k_cat, kv, s_len):
    nq = s_len // BQ
    scale = 1.0 / math.sqrt(QK_DIM)

    def body(q_ref, k_ref, v_ref, o_ref, lse_ref):
        i = pl.program_id(1)
        jd = _diag_block(i)
        row, col, trow, tcol, lane, klane = _attn_consts()
        causal = [col + g * BK <= row for g in range(DIAG)]
        q = q_ref[...]
        hm = [lane < HD, lane >= HD]
        km = [klane < HD, klane >= HD]
        qh = [jnp.where(_pair_mask(BQ, h), q, 0) for h in range(2)]

        def step(js, st, diag):
            m_run, l_run, acc = st
            chains = _chains(js)
            kj = [k_ref[_key_slice(j), :] for j in js]
            vj = [v_ref[_key_slice(j), :].astype(MXU_DTYPE) for j in js]
            s = {}
            for h, t in chains:
                s[h, t] = _dot_nt(qh[h], kj[t]) * scale
                if diag:
                    s[h, t] = jnp.where(causal[diag[t]], s[h, t], -jnp.inf)
            m_new, alpha, l_new = [], [], []
            for h in range(2):
                top = m_run[h]
                for t in range(len(js)):
                    top = jnp.maximum(top, jnp.max(s[h, t], axis=1, keepdims=True))
                m_new.append(top)
                alpha.append(jnp.exp(m_run[h] - top))
                l_new.append(alpha[h] * l_run[h])
            add = None
            for h, t in chains:
                pr = jnp.exp(s[h, t] - m_new[h])
                l_new[h] = l_new[h] + _rowsum(pr)
                part = _dot(pr.astype(MXU_DTYPE), jnp.where(km[h], vj[t], 0))
                add = part if add is None else add + part
            acc = jnp.where(hm[0], alpha[0], alpha[1]) * acc + add
            return tuple(m_new), tuple(l_new), acc

        st = ((jnp.full((BQ, 1), -1e30, F32),) * 2, (jnp.zeros((BQ, 1), F32),) * 2, jnp.zeros((BQ, LANES), F32))
        diag_js, places = _diag_blocks(jd, True)
        st = step(diag_js, st, places)
        m_run, l_run, acc = _walk_blocks(lambda js, s: step(js, s, False), st, jd, True, group=8)
        o_ref[...] = acc / jnp.where(hm[0], l_run[0], l_run[1])
        lse_ref[...] = jnp.where(hm[0], m_run[0] + jnp.log(l_run[0]), m_run[1] + jnp.log(l_run[1]))

    blk = pl.BlockSpec((BQ, LANES), lambda p, i: (i, p))
    out = jax.ShapeDtypeStruct((s_len, MLA_W), F32)
    return pl.pallas_call(
        body, name="mla_fwd", grid=(HEADS // 2, nq),
        in_specs=[pl.BlockSpec((BQ, 2 * LANES), lambda p, i: (i, p)),
                  pl.BlockSpec((s_len, 2 * LANES), lambda p, i: (0, p)),
                  pl.BlockSpec((s_len, LANES), lambda p, i: (0, MLA_W // LANES + p))],
        out_specs=[blk, blk], out_shape=[out, out],
        compiler_params=_cparams("parallel", "parallel"),
    )(q_cat, k_cat, kv)


def _mla_backward(q_cat, k_cat, kv, o, lse, do, s_len):
    nq = s_len // BQ
    scale = 1.0 / math.sqrt(QK_DIM)

    def body(q_ref, k_ref, v_ref, o_ref, lse_ref, do_ref, dq_ref, dk_ref, dv_ref):
        i = pl.program_id(1)

        @pl.when(i == 0)
        def _():
            dk_ref[...] = jnp.zeros(dk_ref.shape, F32)
            dv_ref[...] = jnp.zeros(dv_ref.shape, F32)

        jd = _diag_block(i)
        row, col, trow, tcol, lane, klane = _attn_consts()
        causal = [col + g * BK <= row for g in range(DIAG)]
        q = q_ref[...]
        o_blk = o_ref[...]
        do_blk = do_ref[...]
        lse_blk = lse_ref[...]
        hm = [lane < HD, lane >= HD]
        kpm = [_pair_mask(BK, h) for h in range(2)]
        qh = [jnp.where(_pair_mask(BQ, h), q, 0) for h in range(2)]
        doh_f = [jnp.where(m, do_blk, 0.0) for m in hm]
        doh = [d.astype(MXU_DTYPE) for d in doh_f]
        delta = [jnp.sum(d * o_blk, axis=1, keepdims=True) for d in doh_f]
        lse_h = [jnp.sum(jnp.where(lane == h * HD, lse_blk, 0.0), axis=1, keepdims=True) for h in range(2)]

        def step(js, st, diag):
            dq = st
            chains = _chains(js)
            kj = [k_ref[_key_slice(j), :] for j in js]
            vj = [v_ref[_key_slice(j), :].astype(MXU_DTYPE) for j in js]
            s = {(h, t): _dot_nt(qh[h], kj[t]) for h, t in chains}
            dp = {(h, t): _dot_nt(doh[h], vj[t]) for h, t in chains}
            adds = [[None] * len(js) for _ in range(2)]

            def accumulate(slot, t, part):
                adds[slot][t] = part if adds[slot][t] is None else adds[slot][t] + part

            for h, t in chains:
                pr = jnp.exp(s[h, t] * scale - lse_h[h])
                if diag:
                    pr = jnp.where(causal[diag[t]], pr, 0.0)
                dsb = (pr * (dp[h, t] - delta[h]) * scale).astype(MXU_DTYPE)
                dq = dq + _dot(dsb, jnp.where(kpm[h], kj[t], 0))
                accumulate(0, t, _dot_tn(dsb, qh[h]))
                accumulate(1, t, _dot_tn(pr.astype(MXU_DTYPE), doh[h]))
            for t, j in enumerate(js):
                dk_ref[_key_slice(j), :] += adds[0][t]
                dv_ref[_key_slice(j), :] += adds[1][t]
            return dq

        diag_js, places = _diag_blocks(jd, True)
        st = step(diag_js, jnp.zeros((BQ, 2 * LANES), F32), places)
        dq_ref[...] = _walk_blocks(lambda js, s: step(js, s, False), st, jd, True, group=4)

    blk = pl.BlockSpec((BQ, LANES), lambda p, i: (i, p))
    full = pl.BlockSpec((s_len, LANES), lambda p, i: (0, p))
    out = jax.ShapeDtypeStruct((s_len, MLA_W), F32)
    out_cat = jax.ShapeDtypeStruct((s_len, 2 * MLA_W), F32)
    return pl.pallas_call(
        body, name="mla_bwd", grid=(HEADS // 2, nq),
        in_specs=[pl.BlockSpec((BQ, 2 * LANES), lambda p, i: (i, p)),
                  pl.BlockSpec((s_len, 2 * LANES), lambda p, i: (0, p)),
                  pl.BlockSpec((s_len, LANES), lambda p, i: (0, MLA_W // LANES + p)),
                  blk, blk, blk],
        out_specs=[pl.BlockSpec((BQ, 2 * LANES), lambda p, i: (i, p)),
                   pl.BlockSpec((s_len, 2 * LANES), lambda p, i: (0, p)), full],
        out_shape=[out_cat, out_cat, out],
        compiler_params=_cparams("arbitrary", "arbitrary"),
    )(q_cat, k_cat, kv, o, lse, do)


def _mesh_pos():
    return lax.axis_index("x"), lax.axis_index("y"), lax.axis_index("c")


def _dev_index(px, py, pc):
    return 4 * px + 2 * py + pc


def _all_gather(block, name):
    return _all_gather_parts([block], name)[0]


def _all_gather_parts(blocks, name):
    n = len(blocks)

    def body(*refs):
        x_refs, out_refs = refs[:n], refs[n:2 * n]
        send_sems, recv_sems, local_sems = refs[2 * n:]
        x, y, c = _mesh_pos()
        me, sibling = (x, y, c), (x, y, 1 - c)
        chips = [(1 - x, y), (x, 1 - y), (1 - x, 1 - y)]

        def copy(a, k, blockpos, to, src=None):
            slot = out_refs[a].at[_dev_index(*blockpos)]
            return pltpu.make_async_remote_copy(
                src_ref=slot if src is None else src, dst_ref=slot,
                send_sem=send_sems.at[7 * a + k], recv_sem=recv_sems.at[7 * a + k],
                device_id=to, device_id_type=pl.DeviceIdType.MESH)

        mine = [pltpu.make_async_copy(x_refs[a], out_refs[a].at[_dev_index(*me)], local_sems.at[a]) for a in range(n)]
        for cp in mine:
            cp.start()
        first = []
        for a in range(n):
            first.append(copy(a, 0, me, sibling, src=x_refs[a]))
            first += [copy(a, 1 + j, me, (*chip, c), src=x_refs[a]) for j, chip in enumerate(chips)]
        for cp in first:
            cp.start()
        passed = []
        for j, chip in enumerate(chips):
            for a in range(n):
                copy(a, 1 + j, (*chip, c), me).wait_recv()
                passed.append(copy(a, 4 + j, (*chip, c), sibling))
                passed[-1].start()
        for a in range(n):
            copy(a, 0, sibling, me).wait_recv()
            for j, chip in enumerate(chips):
                copy(a, 4 + j, (*chip, 1 - c), me).wait_recv()
        for cp in first + passed:
            cp.wait_send()
        for cp in mine:
            cp.wait()

    return pl.pallas_call(
        body, name=name,
        out_shape=[jax.ShapeDtypeStruct((N_DEV,) + b.shape, b.dtype) for b in blocks],
        in_specs=[pl.BlockSpec(memory_space=pl.ANY)] * n, out_specs=[pl.BlockSpec(memory_space=pl.ANY)] * n,
        scratch_shapes=[pltpu.SemaphoreType.DMA((7 * n,)), pltpu.SemaphoreType.DMA((7 * n,)),
                        pltpu.SemaphoreType.DMA((n,))],
    )(*blocks)


class _Exchange:
    def __init__(self, arrays):
        self.arrays = list(arrays)
        n = len(self.arrays)
        self.in_specs = [pl.BlockSpec(memory_space=pl.ANY)] * n
        self.out_specs = [pl.BlockSpec(memory_space=pl.ANY)] * n
        self.out_shape = [jax.ShapeDtypeStruct(a.shape, a.dtype) for a in self.arrays]
        self.scratch = [pltpu.SemaphoreType.DMA((7 * n,)), pltpu.SemaphoreType.DMA((7 * n,)),
                        pltpu.SemaphoreType.DMA((n,))]

    def _copies(self, x_refs, out_refs, sems, with_arrivals):
        send_sems, recv_sems, local_sems = sems
        x, y, c = _mesh_pos()
        me = _dev_index(x, y, c)
        flips = [(fx, fy, fc) for fx in (0, 1) for fy in (0, 1) for fc in (0, 1)][1:]
        peers = [(1 - x if fx else x, 1 - y if fy else y, 1 - c if fc else c) for fx, fy, fc in flips]
        mine, sends, arrivals = [], [], []
        for a in range(len(self.arrays)):
            mine.append(pltpu.make_async_copy(x_refs[a].at[me], out_refs[a].at[me], local_sems.at[a]))
            for k, peer in enumerate(peers):
                sends.append(pltpu.make_async_remote_copy(
                    src_ref=x_refs[a].at[_dev_index(*peer)], dst_ref=out_refs[a].at[me],
                    send_sem=send_sems.at[7 * a + k], recv_sem=recv_sems.at[7 * a + k],
                    device_id=peer, device_id_type=pl.DeviceIdType.MESH))
                if not with_arrivals:
                    continue
                arrivals.append(pltpu.make_async_remote_copy(
                    src_ref=x_refs[a].at[me], dst_ref=out_refs[a].at[_dev_index(*peer)],
                    send_sem=send_sems.at[7 * a + k], recv_sem=recv_sems.at[7 * a + k],
                    device_id=peer, device_id_type=pl.DeviceIdType.MESH))
        return mine, sends, arrivals

    def start(self, x_refs, out_refs, sems):
        mine, sends, _ = self._copies(x_refs, out_refs, sems, False)
        for cp in mine + sends:
            cp.start()

    def wait(self, x_refs, out_refs, sems):
        mine, sends, arrivals = self._copies(x_refs, out_refs, sems, True)
        for cp in arrivals:
            cp.wait_recv()
        for cp in sends:
            cp.wait_send()
        for cp in mine:
            cp.wait()


def _sum_blocks(parts, name):
    n, r, c = parts.shape
    row_tiles = [t for t in range(16, min(r, 2048) + 1, 16) if r % t == 0]
    if row_tiles:
        tr, tc = max(row_tiles), c
    else:
        tr, tc = r, 2 * LANES
    assert c % tc == 0

    def body(p_ref, o_ref):
        acc = p_ref[0].astype(F32)
        for s in range(1, n):
            acc = acc + p_ref[s].astype(F32)
        o_ref[...] = acc

    return pl.pallas_call(
        body, name=name, grid=(r // tr, c // tc),
        in_specs=[pl.BlockSpec((n, tr, tc), lambda i, j: (0, i, j))],
        out_specs=pl.BlockSpec((tr, tc), lambda i, j: (i, j)),
        out_shape=jax.ShapeDtypeStruct((r, c), F32),
        compiler_params=_cparams("parallel", "parallel"),
    )(parts)


def _sigmoid(x):
    return 1.0 / (1.0 + jnp.exp(-x))


def _silu(x):
    return x * _sigmoid(x)


def _silu_grad(x):
    s = _sigmoid(x)
    return s * (1.0 + x * (1.0 - s))


def _colsum(x):
    return jnp.sum(x, axis=0, keepdims=True)


def _rms(x):
    return lax.rsqrt(jnp.mean(x * x, axis=-1, keepdims=True) + EPS)


def _rms_bwd(xn, r, dxn):
    return r * (dxn - xn * jnp.mean(dxn * xn, axis=-1, keepdims=True))


def _adamw(w, g, m, v):
    m = ADAM_B1 * m + (1.0 - ADAM_B1) * g
    v = ADAM_B2 * v + (1.0 - ADAM_B2) * jnp.square(g)
    m_hat = m / (1.0 - ADAM_B1 ** ADAM_STEP)
    v_hat = v / (1.0 - ADAM_B2 ** ADAM_STEP)
    delta = -ADAM_LR * (m_hat / (jnp.sqrt(v_hat) + ADAM_EPS) + ADAM_WD * w)
    return delta, m, v


def _adamw_call(w, g, m, v, name):
    r, c = w.shape
    if r % 256 == 0:
        tr, tc = 256, c
    elif r * c <= 256 * 1024 or c % (2 * LANES):
        tr, tc = r, c
    else:
        tr, tc = r, 2 * LANES

    def body(w_ref, g_ref, m_ref, v_ref, d_out, m_out, v_out):
        d_out[...], m_out[...], v_out[...] = _adamw(w_ref[...], g_ref[...], m_ref[...], v_ref[...])

    spec = pl.BlockSpec((tr, tc), lambda i, j: (i, j))
    return pl.pallas_call(
        body, name=name, grid=(r // tr, c // tc), in_specs=[spec] * 4, out_specs=[spec] * 3,
        out_shape=[jax.ShapeDtypeStruct((r, c), F32)] * 3, compiler_params=_cparams("parallel", "parallel"),
    )(w, g, m, v)


def _uq_to_kernel_layout(w):
    lead = w.shape[:-1]
    t = w.reshape(lead + (HEADS, QK_DIM))
    return jnp.concatenate([t[..., :NOPE].reshape(lead + (HEADS * NOPE,)),
                            t[..., NOPE:NOPE + ROPE // 2].reshape(lead + (LANES,)),
                            t[..., NOPE + ROPE // 2:].reshape(lead + (LANES,))], axis=-1)


def _uq_from_kernel_layout(w):
    lead = w.shape[:-1]
    nope = w[..., :HEADS * NOPE].reshape(lead + (HEADS, NOPE))
    r1 = w[..., HEADS * NOPE:HEADS * NOPE + LANES].reshape(lead + (HEADS, ROPE // 2))
    r2 = w[..., HEADS * NOPE + LANES:].reshape(lead + (HEADS, ROPE // 2))
    return jnp.concatenate([nope, r1, r2], axis=-1).reshape(lead + (HEADS * QK_DIM,))


def _ukv_to_kernel_layout(w):
    lead = w.shape[:-1]
    t = w.reshape(lead + (HEADS, NOPE + HD))
    return jnp.concatenate([t[..., :NOPE].reshape(lead + (HEADS * NOPE,)),
                            t[..., NOPE:].reshape(lead + (HEADS * HD,))], axis=-1)


def _ukv_from_kernel_layout(w):
    lead = w.shape[:-1]
    kn = w[..., :HEADS * NOPE].reshape(lead + (HEADS, NOPE))
    vv = w[..., HEADS * NOPE:].reshape(lead + (HEADS, HD))
    return jnp.concatenate([kn, vv], axis=-1).reshape(lead + (HEADS * (NOPE + HD),))


def _w_in_t_to_kernel_layout(wt):
    sb = wt[0:2048]
    c_q = wt[2048:2432]
    c_kv = wt[2432:2688]
    k_rot = wt[2688:2720]
    mla_z = wt[2720:3232]
    gates = wt[3232:5280]
    zeros = jnp.zeros((LANES, wt.shape[1]), wt.dtype)
    k1 = jnp.tile(k_rot[:ROPE // 2], (HEADS, 1))
    k2 = jnp.tile(k_rot[ROPE // 2:], (HEADS, 1))
    return jnp.concatenate([gates, sb, mla_z, c_q, zeros, c_kv, k1, k2], axis=0)


def _w_in_t_from_kernel_layout(gt, g_rot):
    return jnp.concatenate([gt[O_SBQ:O_SBQ + 2048], gt[O_CQ:O_CQ + Q_RANK], gt[O_CKV:O_CKV + KV_RANK],
                            g_rot.astype(gt.dtype), gt[O_MLAZ:O_MLAZ + MLA_W], gt[O_GA:O_GA + 2 * D]], axis=0)


def kernel(x, c, positions, w_ada, b_ada, norm_gain, w_in, q_norm_gain, w_uq, kv_norm_gain, w_ukv, w_branch_a, w_branch_b, w_out, final_norm_gain, loss_target, m_w_ada, m_b_ada, m_norm_gain, m_w_in, m_q_norm_gain, m_w_uq, m_kv_norm_gain, m_w_ukv, m_w_branch_a, m_w_branch_b, m_w_out, m_final_norm_gain, v_w_ada, v_b_ada, v_norm_gain, v_w_in, v_q_norm_gain, v_w_uq, v_kv_norm_gain, v_w_ukv, v_w_branch_a, v_w_branch_b, v_w_out, v_final_norm_gain):
    s_len = x.shape[1]
    me = _dev_index(*_mesh_pos())
    x2d = x[0]
    tgt = loss_target[0]

    w_in_t = w_in[0].T.astype(BF16)
    big = [w_uq[0], w_ukv[0], w_branch_a[0], w_branch_b[0], w_out[0]]
    big_sizes = [int(w.size) for w in big]
    packed = jnp.concatenate([w.astype(BF16).reshape(-1, LANES) for w in big], axis=0)
    g_in_t, c_all = _all_gather_parts([w_in_t, c.reshape(8, LANES)], "gather_w_in")
    c_all = c_all.reshape(N_DEV, D)
    w_in_kt = _w_in_t_to_kernel_layout(g_in_t.reshape(N_DEV * w_in_t.shape[0], D))

    mod_cols = _mm(c_all, w_ada[0], name="ada_mod")
    mod_all = _all_gather(mod_cols, "gather_mod")
    mod = lax.dynamic_index_in_dim(mod_all, me, axis=1, keepdims=False).reshape(1, 3 * D)
    mod_shift, mod_scale, mod_gate = mod[:, :D], mod[:, D:2 * D], mod[:, 2 * D:]
    b_shift, b_scale, b_gate = b_ada[:, :D], b_ada[:, D:2 * D], b_ada[:, 2 * D:]
    g1 = norm_gain
    gq, gkv = q_norm_gain, kv_norm_gain
    gf = final_norm_gain.reshape(1, D)

    def f_h(x_, g1_, ms, bs, msc, bsc):
        xn = x_ * _rms(x_)
        return (xn * g1_ * (1.0 + (msc + bsc)) + (ms + bs),), ()

    (h,) = _rowwise(f_h, [x2d], [g1, mod_shift, b_shift, mod_scale, b_scale], [(D, BF16)], name="ada_norm")
    proj = _mm(h, w_in_kt, tb=True, name="proj_in", tiles=(min(s_len, 1024), IN_PAD // 2, D))

    (o_a, sb_tails), (gathered,) = _sb_forward(
        proj, s_len, _Exchange([jnp.broadcast_to(packed[None], (N_DEV,) + packed.shape)]))
    offs = [0]
    for n in big_sizes:
        offs.append(offs[-1] + n // LANES)

    def unpack(t, shape):
        return gathered[:, offs[t]:offs[t + 1], :].reshape((N_DEV,) + shape)

    def cols(t, shape):
        return unpack(t, shape).transpose(1, 0, 2).reshape(shape[0], N_DEV * shape[1])

    w_uq_k = _uq_to_kernel_layout(cols(0, big[0].shape))
    w_ukv_k = _ukv_to_kernel_layout(cols(1, big[1].shape))
    w_a_f = cols(2, big[2].shape)
    w_b_f = cols(3, big[3].shape)
    w_out_f = unpack(4, big[4].shape).reshape(D, D)

    def f_lat(cq, ckv, gq_, gkv_):
        return (cq * _rms(cq) * gq_, ckv * _rms(ckv) * gkv_), ()

    cq_n, ckv_n = _rowwise(f_lat, [(proj, O_CQ // Q_RANK, Q_RANK), (proj, O_CKV // KV_RANK, KV_RANK)], [gq, gkv],
                           [(Q_RANK, BF16), (KV_RANK, BF16)], name="latent_norm")
    q_mla = _mm(cq_n, w_uq_k, name="q_up")
    kv = _mm(ckv_n, w_ukv_k, name="kv_up")

    inv_freq = ROPE_BASE ** (-jnp.arange(0, ROPE, 2, dtype=F32) / ROPE)
    inv_freq_t = jnp.tile(inv_freq, HEADS).reshape(1, LANES)
    pos_col = positions.reshape(s_len, 1).astype(F32)

    pairs = HEADS // 2

    def f_rope(pos, qn, q1, q2, kn, k1, k2, freq):
        ang = pos * freq
        cs, sn = jnp.cos(ang), jnp.sin(ang)
        q1r, q2r = q1 * cs - q2 * sn, q1 * sn + q2 * cs
        k1r, k2r = k1 * cs - k2 * sn, k1 * sn + k2 * cs
        lane = lax.broadcasted_iota(jnp.int32, q1.shape, 1)
        first, second = lane < ROPE, (lane >= ROPE) & (lane < 2 * ROPE)
        k_rot = jnp.where(first, k1r, jnp.where(second, k2r, 0.0))
        q_parts, k_parts = [], []
        for p in range(pairs):
            q_rot = jnp.where(first, pltpu.roll(q1r, (LANES - ROPE * p) % LANES, 1),
                              jnp.where(second, pltpu.roll(q2r, (LANES + ROPE - ROPE * p) % LANES, 1), 0.0))
            q_parts += [qn[:, LANES * p:LANES * (p + 1)], q_rot]
            k_parts += [kn[:, LANES * p:LANES * (p + 1)], k_rot]
        return (jnp.concatenate(q_parts, axis=1), jnp.concatenate(k_parts, axis=1), cs, sn), ()

    q_cat, k_cat, cos_t, sin_t = _rowwise(
        f_rope, [pos_col, (q_mla, 0, MLA_W), (q_mla, 4, LANES), (q_mla, 5, LANES), (kv, 0, MLA_W),
                 (proj, O_KROT // LANES, LANES), (proj, O_KROT // LANES + 1, LANES)], [inv_freq_t],
        [(2 * MLA_W, BF16), (2 * MLA_W, BF16), (LANES, F32), (LANES, F32)], name="rope")

    o_b, lse = _mla_forward(q_cat, k_cat, kv, s_len)

    def f_gate(oa, za, ob, zb):
        return (oa * _silu(za), ob * _silu(zb)), ()

    ya_in, yb_in = _rowwise(f_gate, [o_a, (proj, O_SBZ // SB_W, SB_W), o_b, (proj, O_MLAZ // MLA_W, MLA_W)], [],
                            [(SB_W, BF16), (MLA_W, BF16)], name="branch_gate")
    y_a = _mm(ya_in, w_a_f, name="branch_a")

    def f_merge(yb, ga, gb, ya):
        return (yb, _sigmoid(ga) * ya + _sigmoid(gb) * yb), ()

    y_b, merged = _mm(yb_in, w_b_f, name="branch_b_merge", tiles=(min(s_len, 512), D, MLA_W),
                      epilogue=(f_merge, [(proj, O_GA // D, D), (proj, O_GB // D, D), y_a], [], [(D, F32), (D, BF16)], []))

    def f_loss(out_, x_, t_, mg, bg, gf_):
        gate = mg + bg
        x2 = x_ + gate * out_
        r2 = _rms(x2)
        xn2 = x2 * r2
        err = xn2 * gf_ - t_
        loss = jnp.full((1, LANES), 0.5 / D, F32) * jnp.sum(err * err)
        dy = err * (1.0 / D)
        dx2 = _rms_bwd(xn2, r2, dy * gf_)
        return (dx2, dx2 * gate), (loss, _colsum(dy * xn2), _colsum(dx2 * out_))

    dx2, d_out, loss_part, d_gf, d_gate = _mm(
        merged, w_out_f, name="out_proj_loss", tiles=(min(s_len, 512), D, D),
        epilogue=(f_loss, [x2d, tgt], [mod_gate, b_gate, gf], [(D, F32), (D, BF16)], [LANES, D, D]))

    dw_out = _mm(merged, d_out, ta=True, name="dw_out")

    def f_dmerge(dm, ga, gb, ya, yb):
        sa, sb = _sigmoid(ga), _sigmoid(gb)
        return (dm * sa, dm * sb, dm * ya * sa * (1.0 - sa), dm * yb * sb * (1.0 - sb)), ()

    d_ya, d_yb, d_ga, d_gb = _mm(
        d_out, w_out_f, tb=True, name="d_merge", tiles=(min(s_len, 256), D, D),
        epilogue=(f_dmerge, [(proj, O_GA // D, D), (proj, O_GB // D, D), y_a, y_b], [], [(D, BF16)] * 4, []))
    dw_a = _mm(ya_in, d_ya, ta=True, name="dw_branch_a")
    dw_b = _mm(yb_in, d_yb, ta=True, name="dw_branch_b")

    def f_dgate(d_in, o_, z_):
        return (d_in * _silu(z_), d_in * o_ * _silu_grad(z_)), ()

    d_oa, d_sbz = _mm(d_ya, w_a_f, tb=True, name="d_branch_a",
                      epilogue=(f_dgate, [o_a, (proj, O_SBZ // SB_W, SB_W)], [], [(SB_W, F32), (SB_W, BF16)], []))
    d_ob, d_mlaz = _mm(d_yb, w_b_f, tb=True, name="d_branch_b",
                       epilogue=(f_dgate, [o_b, (proj, O_MLAZ // MLA_W, MLA_W)], [], [(MLA_W, F32), (MLA_W, BF16)], []))

    dq_cat, dk_cat, dv_b = _mla_backward(q_cat, k_cat, kv, o_b, lse, d_ob, s_len)

    def f_drope(dq, dk, dv_, cs, sn):
        lane = lax.broadcasted_iota(jnp.int32, cs.shape, 1)
        first, second = lane < ROPE, (lane >= ROPE) & (lane < 2 * ROPE)
        dq1 = dq2 = dk1 = dk2 = None
        for p in range(pairs):
            q_rot = dq[:, LANES * (2 * p + 1):LANES * (2 * p + 2)]
            k_rot = dk[:, LANES * (2 * p + 1):LANES * (2 * p + 2)]
            parts = (pltpu.roll(jnp.where(first, q_rot, 0.0), (ROPE * p) % LANES, 1),
                     pltpu.roll(jnp.where(second, q_rot, 0.0), (LANES - ROPE + ROPE * p) % LANES, 1),
                     jnp.where(first, k_rot, 0.0), jnp.where(second, k_rot, 0.0))
            if p == 0:
                dq1, dq2, dk1, dk2 = parts
            else:
                dq1, dq2, dk1, dk2 = dq1 + parts[0], dq2 + parts[1], dk1 + parts[2], dk2 + parts[3]
        dqn_ = [dq[:, 2 * LANES * p:2 * LANES * p + LANES] for p in range(pairs)]
        dkn_ = [dk[:, 2 * LANES * p:2 * LANES * p + LANES] for p in range(pairs)]
        return (jnp.concatenate(dqn_ + [dq1 * cs + dq2 * sn, dq2 * cs - dq1 * sn], axis=1),
                jnp.concatenate(dkn_ + [dv_], axis=1),
                jnp.concatenate([dk1 * cs + dk2 * sn, dk2 * cs - dk1 * sn], axis=1)), ()

    dq_k, dkv_k, d_krot = _rowwise(f_drope, [dq_cat, dk_cat, dv_b, cos_t, sin_t], [],
                                   [(HEADS * QK_DIM, BF16), (2 * MLA_W, BF16), (2 * LANES, BF16)], name="d_rope")
    dw_uq_k = _mm(cq_n, dq_k, ta=True, name="dw_uq")
    dw_ukv_k = _mm(ckv_n, dkv_k, ta=True, name="dw_ukv")

    def f_dlat(d_normed, latent, gain):
        r = _rms(latent)
        normed = latent * r
        return (_rms_bwd(normed, r, d_normed * gain),), (_colsum(d_normed * normed),)

    d_cq, d_gq = _mm(dq_k, w_uq_k, tb=True, name="d_cq_norm",
                     epilogue=(f_dlat, [(proj, O_CQ // Q_RANK, Q_RANK)], [gq], [(Q_RANK, BF16)], [Q_RANK]))
    d_ckv, d_gkv = _mm(dkv_k, w_ukv_k, tb=True, name="d_ckv_norm",
                       epilogue=(f_dlat, [(proj, O_CKV // KV_RANK, KV_RANK)], [gkv], [(KV_RANK, BF16)], [KV_RANK]))

    def col_blocks(g):
        kdim, n8 = g.shape
        return g.astype(BF16).reshape(kdim, N_DEV, n8 // N_DEV).transpose(1, 0, 2).reshape(N_DEV, -1, LANES)

    g_blocks = jnp.concatenate([col_blocks(_uq_from_kernel_layout(dw_uq_k)), col_blocks(_ukv_from_kernel_layout(dw_ukv_k)),
                                col_blocks(dw_a), col_blocks(dw_b), dw_out.astype(BF16).reshape(N_DEV, -1, LANES)], axis=1)
    (d_sbq, d_sbk, d_sbv), (g_recv,) = _sb_backward(proj, sb_tails, d_oa, s_len, _Exchange([g_blocks]))

    d_proj = jnp.concatenate([d_ga, d_gb, d_sbq.astype(BF16), d_sbk.astype(BF16), d_sbv.astype(BF16), d_sbz, d_mlaz,
                              d_cq, jnp.zeros((s_len, LANES), BF16), d_ckv, d_krot], axis=1)
    dw_in_kt = _mm(d_proj, h, ta=True, out_dtype=BF16, name="dw_in", tiles=(512, D, s_len))

    def krot_body(t_ref, o_ref):
        half = ROPE // 2
        for part in range(2):
            acc = t_ref[part * LANES:part * LANES + half, :].astype(F32)
            for hh in range(1, HEADS):
                acc = acc + t_ref[part * LANES + hh * half:part * LANES + (hh + 1) * half, :].astype(F32)
            o_ref[part * half:(part + 1) * half, :] = acc

    dw_krot = pl.pallas_call(krot_body, name="dw_krot_sum", out_shape=jax.ShapeDtypeStruct((ROPE, D), F32))(
        dw_in_kt[O_KROT:O_KROT + 2 * LANES])

    g_in_blocks = _w_in_t_from_kernel_layout(dw_in_kt, dw_krot).reshape(N_DEV, -1, D)
    def f_dx(dh_, x_, dx2_, g1_, msc, bsc):
        r = _rms(x_)
        xn = x_ * r
        dn1 = dh_ * (1.0 + (msc + bsc))
        return ((dx2_ + _rms_bwd(xn, r, dn1 * g1_),),
                (_colsum(dh_), _colsum(dh_ * (xn * g1_)), _colsum(dn1 * xn)))

    (grad_x2d, d_shift, d_scale, d_g1), (g_in_recv,) = _mm(
        d_proj, w_in_kt, name="d_h", tiles=(min(s_len, 512), D, 512), exchange=_Exchange([g_in_blocks]),
        epilogue=(f_dx, [x2d, dx2], [g1, mod_scale, b_scale], [(D, F32)], [D, D, D]))

    g_in_sum_t = _sum_blocks(g_in_recv, "sum_grads_w_in")
    g_sum = _sum_blocks(g_recv, "sum_grads")
    g_big = [g_sum[offs[t]:offs[t + 1]].reshape(big[t].shape) for t in range(5)]

    small = jnp.concatenate([d_shift, d_scale, d_gate, d_g1, d_gq, d_gkv, d_gf, loss_part], axis=1)
    n_small = small.shape[1]
    pad = (-n_small) % (8 * LANES)
    small = jnp.pad(small, ((0, 0), (0, pad))).reshape(-1, LANES)
    small_all = _all_gather(small, "gather_small")
    small_sum = _sum_blocks(small_all, "sum_small").reshape(1, -1)
    g_b_ada = small_sum[:, :3 * D]
    g_g1 = small_sum[:, 3 * D:4 * D]
    g_gq = small_sum[:, 4 * D:4 * D + Q_RANK]
    g_gkv = small_sum[:, 4 * D + Q_RANK:4 * D + Q_RANK + KV_RANK]
    g_gf = small_sum[:, 4 * D + Q_RANK + KV_RANK:4 * D + Q_RANK + KV_RANK + D]

    dmod_all = small_all.reshape(N_DEV, -1)[:, :3 * D]
    dmod_cols = lax.dynamic_slice_in_dim(dmod_all, me * (3 * D // N_DEV), 3 * D // N_DEV, axis=1)
    g_w_ada = _mm(c_all, dmod_cols, ta=True, name="dw_ada")

    loss = small_sum[0, n_small - LANES]

    names = ["w_ada", "b_ada", "norm_gain", "w_in", "q_norm_gain", "w_uq", "kv_norm_gain", "w_ukv",
             "w_branch_a", "w_branch_b", "w_out", "final_norm_gain"]
    weights = dict(w_ada=w_ada, b_ada=b_ada, norm_gain=norm_gain, w_in=w_in, q_norm_gain=q_norm_gain, w_uq=w_uq,
                   kv_norm_gain=kv_norm_gain, w_ukv=w_ukv, w_branch_a=w_branch_a, w_branch_b=w_branch_b, w_out=w_out,
                   final_norm_gain=final_norm_gain)
    moms = dict(w_ada=m_w_ada, b_ada=m_b_ada, norm_gain=m_norm_gain, w_in=m_w_in, q_norm_gain=m_q_norm_gain,
                w_uq=m_w_uq, kv_norm_gain=m_kv_norm_gain, w_ukv=m_w_ukv, w_branch_a=m_w_branch_a,
                w_branch_b=m_w_branch_b, w_out=m_w_out, final_norm_gain=m_final_norm_gain)
    vels = dict(w_ada=v_w_ada, b_ada=v_b_ada, norm_gain=v_norm_gain, w_in=v_w_in, q_norm_gain=v_q_norm_gain,
                w_uq=v_w_uq, kv_norm_gain=v_kv_norm_gain, w_ukv=v_w_ukv, w_branch_a=v_w_branch_a,
                w_branch_b=v_w_branch_b, w_out=v_w_out, final_norm_gain=v_final_norm_gain)
    grads2d = dict(w_ada=g_w_ada, b_ada=g_b_ada, norm_gain=g_g1, w_in=g_in_sum_t, q_norm_gain=g_gq, w_uq=g_big[0],
                   kv_norm_gain=g_gkv, w_ukv=g_big[1], w_branch_a=g_big[2], w_branch_b=g_big[3], w_out=g_big[4],
                   final_norm_gain=g_gf)

    grads, deltas, new_m, new_v = [], [], [], []
    for n in names:
        w = weights[n]
        if n == "w_in":
            to2d = lambda t: t[0].T
            back = lambda t: t.T[None]
        else:
            shape2d = grads2d[n].shape
            to2d = lambda t, s=shape2d: t.reshape(s)
            back = lambda t, s=w.shape: t.reshape(s)
        d_, m_, v_ = _adamw_call(to2d(w), grads2d[n], to2d(moms[n]), to2d(vels[n]), "adamw_" + n)
        grads.append(back(grads2d[n]))
        deltas.append(back(d_))
        new_m.append(back(m_))
        new_v.append(back(v_))

    return (loss, grad_x2d.reshape(x.shape), *grads, *deltas, *new_m, *new_v)
```

```python
import functools
import math

import jax
import jax.numpy as jnp
from jax import lax
from jax.experimental import pallas as pl
from jax.experimental.pallas import tpu as pltpu

F32 = jnp.float32
BF16 = jnp.bfloat16
MXU_DTYPE = jnp.bfloat16

N_DEV = 8
D = 1024
HEADS = 8
HD = 64
SB_W = 512
MLA_W = 512
Q_RANK = 384
KV_RANK = 256
ROPE = 32
NOPE = 64
QK_DIM = NOPE + ROPE
EPS = 1e-6
ROPE_BASE = 10000.0

ADAM_LR = 0.001
ADAM_B1 = 0.9
ADAM_B2 = 0.999
ADAM_EPS = 1e-08
ADAM_WD = 0.01
ADAM_STEP = 10

LANES = 128
VMEM_LIMIT = 48 * 1024 * 1024

O_GA, O_GB = 0, 1024
O_SBQ, O_SBK, O_SBV, O_SBZ = 2048, 2560, 3072, 3584
O_MLAZ = 4096
O_CQ = 4608
O_CKV = 5120
O_KROT = 5376
IN_PAD = 5632

BQ = 256
BK = 256


def _cparams(*sem):
    return pltpu.CompilerParams(dimension_semantics=sem, vmem_limit_bytes=VMEM_LIMIT)


def _tile_of(n, cap=512):
    if n <= cap:
        return n
    for t in (1024, 768, 512, 384, 256, 128):
        if t <= cap and n % t == 0:
            return t
    raise ValueError(n)


def _rowwise(fn, rows, vecs, outs, reds=(), *, name, tile=512):
    norm = []
    for r in rows:
        if isinstance(r, tuple):
            arr, cb, w = r[:3]
            ro = r[3] if len(r) > 3 else 0
        else:
            arr, cb, w, ro = r, 0, r.shape[1], 0
        norm.append((arr, cb, w, ro))
    s_len = norm[0][0].shape[0]
    tile = min(tile, s_len)
    assert s_len % tile == 0
    n_row, n_vec, n_out, n_red = len(norm), len(vecs), len(outs), len(reds)

    def body(*refs):
        step = pl.program_id(0)
        row_refs = refs[:n_row]
        vec_refs = refs[n_row:n_row + n_vec]
        out_refs = refs[n_row + n_vec:n_row + n_vec + n_out]
        red_refs = refs[n_row + n_vec + n_out:]
        row_res, red_res = fn(*[r[...] for r in row_refs], *[v[...] for v in vec_refs])
        for o, val in zip(out_refs, row_res):
            o[...] = val.astype(o.dtype)
        if n_red:
            @pl.when(step == 0)
            def _():
                for r in red_refs:
                    r[...] = jnp.zeros(r.shape, r.dtype)
            for r, val in zip(red_refs, red_res):
                r[...] += val

    in_specs = []
    for arr, cb, w, ro in norm:
        in_specs.append(pl.BlockSpec((tile, w), functools.partial(lambda i, cb, rb: (i + rb, cb), cb=cb, rb=ro // tile)))
        assert ro % tile == 0
    for v in vecs:
        in_specs.append(pl.BlockSpec(v.shape, lambda i: (0, 0)))
    out_shape = [jax.ShapeDtypeStruct((s_len, w), dt) for w, dt in outs]
    out_specs = [pl.BlockSpec((tile, w), lambda i: (i, 0)) for w, _ in outs]
    out_shape += [jax.ShapeDtypeStruct((1, w), F32) for w in reds]
    out_specs += [pl.BlockSpec((1, w), lambda i: (0, 0)) for w in reds]
    res = pl.pallas_call(
        body, name=name, grid=(s_len // tile,), in_specs=in_specs, out_specs=out_specs, out_shape=out_shape,
        compiler_params=_cparams("arbitrary" if n_red else "parallel"),
    )(*[a for a, _, _, _ in norm], *vecs)
    return res


def _mm(a, b, *, ta=False, tb=False, out_dtype=F32, name, exchange=None, tiles=None, epilogue=None):
    m, k = (a.shape[1], a.shape[0]) if ta else a.shape
    n = b.shape[0] if tb else b.shape[1]
    assert (b.shape[1] if tb else b.shape[0]) == k
    tm, tn, tk = tiles or (_tile_of(m, 1024), _tile_of(n, 1024 if n <= 1024 else 512), _tile_of(k, 1024))
    assert m % tm == 0 and n % tn == 0 and k % tk == 0
    ni, nj, nk = m // tm, n // tn, k // tk
    dims = (((0 if ta else 1,), (1 if tb else 0,)), ((), ()))
    n_ex = len(exchange.arrays) if exchange else 0
    fn, rows, vecs, outs, reds = epilogue or (None, (), (), (), ())
    rows = [r if isinstance(r, tuple) else (r, 0, r.shape[1]) for r in rows]
    assert not epilogue or tn == n
    n_res = len(outs) + len(reds) if epilogue else 1

    def body(*refs):
        a_ref, b_ref = refs[:2]
        row_refs, refs = refs[2:2 + len(rows)], refs[2 + len(rows):]
        vec_refs, refs = refs[:len(vecs)], refs[len(vecs):]
        x_refs, refs = refs[:n_ex], refs[n_ex:]
        res_refs, refs = refs[:n_res], refs[n_res:]
        out_refs, acc_ref, sems = refs[:n_ex], refs[n_ex], refs[n_ex + 1:]
        i, j, kk = pl.program_id(0), pl.program_id(1), pl.program_id(2)
        first = (i == 0) & (j == 0) & (kk == 0)

        if exchange:
            @pl.when(first)
            def _():
                exchange.start(x_refs, out_refs, sems)

        if reds:
            @pl.when(first)
            def _():
                for r in res_refs[len(outs):]:
                    r[...] = jnp.zeros(r.shape, r.dtype)

        @pl.when(kk == 0)
        def _():
            acc_ref[...] = jnp.zeros(acc_ref.shape, F32)

        acc_ref[...] += lax.dot_general(a_ref[...].astype(MXU_DTYPE), b_ref[...].astype(MXU_DTYPE), dims,
                                        preferred_element_type=F32)

        @pl.when(kk == nk - 1)
        def _():
            if not epilogue:
                res_refs[0][...] = acc_ref[...].astype(res_refs[0].dtype)
                return
            row_res, red_res = fn(acc_ref[...], *[r[...] for r in row_refs], *[v[...] for v in vec_refs])
            for o, val in zip(res_refs, row_res):
                o[...] = val.astype(o.dtype)
            for r, val in zip(res_refs[len(outs):], red_res):
                r[...] += val

        if exchange:
            @pl.when((i == ni - 1) & (j == nj - 1) & (kk == nk - 1))
            def _():
                exchange.wait(x_refs, out_refs, sems)

    a_spec = pl.BlockSpec((tk, tm), lambda i, j, kk: (kk, i)) if ta else pl.BlockSpec((tm, tk), lambda i, j, kk: (i, kk))
    b_spec = pl.BlockSpec((tn, tk), lambda i, j, kk: (j, kk)) if tb else pl.BlockSpec((tk, tn), lambda i, j, kk: (kk, j))
    in_specs = [a_spec, b_spec]
    in_specs += [pl.BlockSpec((tm, w), functools.partial(lambda i, j, kk, cb: (i, cb), cb=cb)) for _, cb, w in rows]
    in_specs += [pl.BlockSpec(v.shape, lambda i, j, kk: (0, 0)) for v in vecs]
    if epilogue:
        res_specs = [pl.BlockSpec((tm, w), lambda i, j, kk: (i, 0)) for w, _ in outs]
        res_specs += [pl.BlockSpec((1, w), lambda i, j, kk: (0, 0)) for w in reds]
        res_shape = [jax.ShapeDtypeStruct((m, w), dt) for w, dt in outs] + [jax.ShapeDtypeStruct((1, w), F32) for w in reds]
    else:
        res_specs = [pl.BlockSpec((tm, tn), lambda i, j, kk: (i, j))]
        res_shape = [jax.ShapeDtypeStruct((m, n), out_dtype)]
    ordered = bool(exchange or reds)
    res = pl.pallas_call(
        body, name=name, grid=(ni, nj, nk),
        in_specs=in_specs + (exchange.in_specs if exchange else []),
        out_specs=res_specs + (exchange.out_specs if exchange else []),
        out_shape=res_shape + (exchange.out_shape if exchange else []),
        scratch_shapes=[pltpu.VMEM((tm, tn), F32)] + (exchange.scratch if exchange else []),
        compiler_params=_cparams(*(("arbitrary",) * 3 if ordered else ("parallel", "parallel", "arbitrary"))),
    )(a, b, *[r[0] for r in rows], *vecs, *(exchange.arrays if exchange else []))
    main = res[:n_res] if epilogue else res[0]
    return (main, res[n_res:]) if exchange else main


_NT = (((1,), (1,)), ((), ()))
_TN = (((0,), (0,)), ((), ()))


def _dot(a, b):
    return jnp.dot(a, b, preferred_element_type=F32)


def _dot_nt(a, b):
    return lax.dot_general(a, b, _NT, preferred_element_type=F32)


def _dot_tn(a, b):
    return lax.dot_general(a, b, _TN, preferred_element_type=F32)


def _running_sum(x, tri):
    return _dot(x.astype(MXU_DTYPE), tri)


def _neg_softplus(z):
    u = jnp.exp2(jnp.abs(z) * (-1.0 / math.log(2.0)))
    return -jnp.maximum(z, 0.0) - jnp.log(1.0 + u)


def _walk_blocks(step, st, n, descending, group=2):
    done = 0
    size = group
    while size >= 1:
        def trip(t, s, size=size, done=done):
            js = [done + size * t + g for g in range(size)]
            return step([n - 1 - j for j in js] if descending else js, s)

        trips = (n - done) // size
        st = lax.fori_loop(0, trips, trip, st)
        done = done + size * trips
        size //= 2
    return st


def _chains(js):
    return [(h, t) for t in range(len(js)) for h in range(2)]


def _rowsum(x):
    return jnp.sum(x, axis=1, keepdims=True)


def _attn_consts():
    row = lax.broadcasted_iota(jnp.int32, (BQ, BK), 0)
    col = lax.broadcasted_iota(jnp.int32, (BQ, BK), 1)
    trow = lax.broadcasted_iota(jnp.int32, (BK, BK), 0)
    tcol = lax.broadcasted_iota(jnp.int32, (BK, BK), 1)
    lane = lax.broadcasted_iota(jnp.int32, (BQ, LANES), 1)
    klane = lax.broadcasted_iota(jnp.int32, (BK, LANES), 1)
    return row, col, trow, tcol, lane, klane


DIAG = BQ // BK
assert BQ == DIAG * BK


def _diag_block(i):
    return i * DIAG


DIAG_GROUP = 4
assert DIAG == 1


def _diag_step(jd, descending, diag_mask):
    below = list(range(DIAG_GROUP)) if descending else list(reversed(range(DIAG_GROUP)))
    js = [jnp.maximum(jd - o, 0) for o in below]
    masks = [diag_mask if o == 0 else jd - o >= 0 for o in below]
    return js, masks, jnp.maximum(jd - (DIAG_GROUP - 1), 0)


def _key_slice(j):
    return pl.ds(pl.multiple_of(j * BK, BK), BK)


def _sb_forward(proj, s_len, exchange):
    nq = s_len // BQ
    assert s_len // BK <= HD
    qc, kc, vc = O_SBQ // LANES, O_SBK // LANES, O_SBV // LANES
    n_ex = len(exchange.arrays)

    def body(q_ref, k_ref, v_ref, *rest):
        x_refs, (o_ref, tails_ref) = rest[:n_ex], rest[n_ex:n_ex + 2]
        out_refs, sems = rest[n_ex + 2:2 * n_ex + 2], rest[2 * n_ex + 2:]
        p = pl.program_id(0)
        i = pl.program_id(1)
        jd = _diag_block(i)

        @pl.when((p == 0) & (i == 0))
        def _():
            exchange.start(x_refs, out_refs, sems)

        row, col, trow, tcol, lane, klane = _attn_consts()
        tri = (trow >= tcol).astype(MXU_DTYPE)
        q = q_ref[...] * 0.125
        qh = [jnp.where(lane < HD, q, 0.0).astype(MXU_DTYPE), jnp.where(lane >= HD, q, 0.0).astype(MXU_DTYPE)]

        km = [klane < HD, klane >= HD]

        def step(js, st, masks):
            carry, acc, tail = st
            chains = _chains(js)
            kj = [k_ref[_key_slice(j), :].astype(MXU_DTYPE) for j in js]
            vj = [v_ref[_key_slice(j), :].astype(MXU_DTYPE) for j in js]
            z = {(h, t): _dot_nt(qh[h], kj[t]) for h, t in chains}
            run = list(carry)
            suf, carry_in = {}, {}
            for h, t in chains:
                lom = _neg_softplus(z[h, t])
                if masks:
                    lom = jnp.where(masks[t], lom, 0.0)
                suf[h, t] = _running_sum(lom, tri)
                carry_in[h, t] = run[h]
                run[h] = run[h] + _rowsum(lom)
            for h, t in chains:
                a = jnp.exp(z[h, t] + suf[h, t] + carry_in[h, t])
                if masks:
                    a = jnp.where(masks[t], a, 0.0)
                acc = acc + _dot(a.astype(MXU_DTYPE), jnp.where(km[h], vj[t], 0))
                tail_lane = js[t] if not masks or masks[t].ndim else jnp.where(masks[t], js[t], -LANES)
                tail = jnp.where(lane == h * HD + tail_lane, carry_in[h, t], tail)
            return tuple(run), acc, tail

        zero = jnp.zeros((BQ, LANES), F32)
        diag_js, masks, left = _diag_step(jd, True, col < row)
        st = step(diag_js, ((jnp.zeros((BQ, 1), F32),) * 2, zero, zero), masks)
        st = _walk_blocks(lambda js, s: step(js, s, None), st, left, True, group=8)
        o_ref[...] = st[1]
        tails_ref[...] = st[2]

        @pl.when((p == HEADS // 2 - 1) & (i == nq - 1))
        def _():
            exchange.wait(x_refs, out_refs, sems)

    blk = pl.BlockSpec((BQ, LANES), lambda p, i: (i, p))
    out = jax.ShapeDtypeStruct((s_len, SB_W), F32)
    res = pl.pallas_call(
        body, name="sb_fwd", grid=(HEADS // 2, nq),
        in_specs=[pl.BlockSpec((BQ, LANES), lambda p, i: (i, qc + p)),
                  pl.BlockSpec((s_len, LANES), lambda p, i: (0, kc + p)),
                  pl.BlockSpec((s_len, LANES), lambda p, i: (0, vc + p))] + exchange.in_specs,
        out_specs=[blk, blk] + exchange.out_specs, out_shape=[out, out] + exchange.out_shape,
        scratch_shapes=exchange.scratch,
        compiler_params=_cparams("arbitrary", "arbitrary"),
    )(proj, proj, proj, *exchange.arrays)
    return res[:2], res[2:]


def _sb_backward(proj, tails, do, s_len, exchange):
    nq = s_len // BQ
    qc, kc, vc = O_SBQ // LANES, O_SBK // LANES, O_SBV // LANES
    n_ex = len(exchange.arrays)

    def body(q_ref, k_ref, v_ref, tails_ref, do_ref, *rest):
        x_refs, (dq_ref, dk_ref, dv_ref) = rest[:n_ex], rest[n_ex:n_ex + 3]
        out_refs, sems = rest[n_ex + 3:2 * n_ex + 3], rest[2 * n_ex + 3:]
        p = pl.program_id(0)
        i = pl.program_id(1)
        jd = _diag_block(i)

        @pl.when((p == 0) & (i == 0))
        def _():
            exchange.start(x_refs, out_refs, sems)

        @pl.when(i == 0)
        def _():
            dk_ref[...] = jnp.zeros(dk_ref.shape, F32)
            dv_ref[...] = jnp.zeros(dv_ref.shape, F32)

        row, col, trow, tcol, lane, klane = _attn_consts()
        tri = (trow >= tcol).astype(MXU_DTYPE)
        tri_p = (trow <= tcol).astype(MXU_DTYPE)
        q = q_ref[...] * 0.125
        tails_blk = tails_ref[...]
        do_blk = do_ref[...]
        hm = [lane < HD, lane >= HD]
        km = [klane < HD, klane >= HD]
        qh = [jnp.where(m, q, 0.0).astype(MXU_DTYPE) for m in hm]
        doh = [jnp.where(m, do_blk, 0.0).astype(MXU_DTYPE) for m in hm]

        def step(js, st, masks):
            before, dq = st
            chains = _chains(js)
            kj = [k_ref[_key_slice(j), :].astype(MXU_DTYPE) for j in js]
            vj = [v_ref[_key_slice(j), :].astype(MXU_DTYPE) for j in js]
            z = {(h, t): _dot_nt(qh[h], kj[t]) for h, t in chains}
            da = {(h, t): _dot_nt(doh[h], vj[t]) for h, t in chains}
            suf, sig = {}, {}
            for h, t in chains:
                lom = _neg_softplus(z[h, t])
                if masks:
                    lom = jnp.where(masks[t], lom, 0.0)
                suf[h, t] = _running_sum(lom, tri)
                sig[h, t] = jnp.exp(z[h, t] + lom)
            run = list(before)
            dl, pre, before_in = {}, {}, {}
            dk_add, dv_add = [None] * len(js), [None] * len(js)
            for h, t in chains:
                tail = _rowsum(jnp.where(lane == h * HD + js[t], tails_blk, 0.0))
                a = jnp.exp(z[h, t] + suf[h, t] + tail)
                if masks:
                    a = jnp.where(masks[t], a, 0.0)
                dl[h, t] = da[h, t] * a
                pre[h, t] = _dot(dl[h, t].astype(MXU_DTYPE), tri_p)
                dv_h = _dot_tn(a.astype(MXU_DTYPE), doh[h])
                dv_add[t] = dv_h if dv_add[t] is None else dv_add[t] + dv_h
                before_in[h, t] = run[h]
                run[h] = run[h] + _rowsum(dl[h, t])
            for h, t in chains:
                upto = before_in[h, t] + pre[h, t]
                dz = dl[h, t] - sig[h, t] * upto
                if masks:
                    dz = jnp.where(masks[t], dz, 0.0)
                dzb = dz.astype(MXU_DTYPE)
                dq = dq + _dot(dzb, jnp.where(km[h], kj[t], 0))
                dk_h = _dot_tn(dzb, qh[h])
                dk_add[t] = dk_h if dk_add[t] is None else dk_add[t] + dk_h
            for t, j in enumerate(js):
                dk_ref[_key_slice(j), :] += dk_add[t]
                dv_ref[_key_slice(j), :] += dv_add[t]
            return tuple(run), dq

        st = ((jnp.zeros((BQ, 1), F32),) * 2, jnp.zeros((BQ, LANES), F32))
        diag_js, masks, left = _diag_step(jd, False, col < row)
        st = _walk_blocks(lambda js, s: step(js, s, None), st, left, False, group=4)
        st = step(diag_js, st, masks)
        dq_ref[...] = st[1] * 0.125

        @pl.when((p == HEADS // 2 - 1) & (i == nq - 1))
        def _():
            exchange.wait(x_refs, out_refs, sems)

    blk = pl.BlockSpec((BQ, LANES), lambda p, i: (i, p))
    full = pl.BlockSpec((s_len, LANES), lambda p, i: (0, p))
    out = jax.ShapeDtypeStruct((s_len, SB_W), F32)
    res = pl.pallas_call(
        body, name="sb_bwd", grid=(HEADS // 2, nq),
        in_specs=[pl.BlockSpec((BQ, LANES), lambda p, i: (i, qc + p)),
                  pl.BlockSpec((s_len, LANES), lambda p, i: (0, kc + p)),
                  pl.BlockSpec((s_len, LANES), lambda p, i: (0, vc + p)),
                  blk, blk] + exchange.in_specs,
        out_specs=[blk, full, full] + exchange.out_specs, out_shape=[out, out, out] + exchange.out_shape,
        scratch_shapes=exchange.scratch,
        compiler_params=_cparams("arbitrary", "arbitrary"),
    )(proj, proj, proj, tails, do, *exchange.arrays)
    return res[:3], res[3:]


def _pair_mask(rows, h):
    lane = lax.broadcasted_iota(jnp.int32, (rows, 2 * LANES), 1)
    rot = lane - LANES
    return (((lane < LANES) & (lane // HD == h))
            | ((lane >= LANES) & (rot < 2 * ROPE) & ((rot // (ROPE // 2)) % 2 == h)))


def _mla_forward(q_cat, k_cat, kv, s_len):
    nq = s_len // BQ
    scale = 1.0 / math.sqrt(QK_DIM)

    def body(q_ref, k_ref, v_ref, o_ref, lse_ref):
        i = pl.program_id(1)
        jd = _diag_block(i)
        row, col, trow, tcol, lane, klane = _attn_consts()
        q = q_ref[...]
        hm = [lane < HD, lane >= HD]
        km = [klane < HD, klane >= HD]
        qh = [jnp.where(_pair_mask(BQ, h), q, 0) for h in range(2)]

        def step(js, st, masks):
            m_run, l_run, acc = st
            chains = _chains(js)
            kj = [k_ref[_key_slice(j), :] for j in js]
            vj = [v_ref[_key_slice(j), :].astype(MXU_DTYPE) for j in js]
            s = {}
            for h, t in chains:
                s[h, t] = _dot_nt(qh[h], kj[t]) * scale
                if masks:
                    s[h, t] = jnp.where(masks[t], s[h, t], -jnp.inf)
            m_new, alpha, l_new = [], [], []
            for h in range(2):
                top = m_run[h]
                for t in range(len(js)):
                    top = jnp.maximum(top, jnp.max(s[h, t], axis=1, keepdims=True))
                m_new.append(top)
                alpha.append(jnp.exp(m_run[h] - top))
                l_new.append(alpha[h] * l_run[h])
            add = None
            for h, t in chains:
                pr = jnp.exp(s[h, t] - m_new[h])
                l_new[h] = l_new[h] + _rowsum(pr)
                part = _dot(pr.astype(MXU_DTYPE), jnp.where(km[h], vj[t], 0))
                add = part if add is None else add + part
            acc = jnp.where(hm[0], alpha[0], alpha[1]) * acc + add
            return tuple(m_new), tuple(l_new), acc

        st = ((jnp.full((BQ, 1), -1e30, F32),) * 2, (jnp.zeros((BQ, 1), F32),) * 2, jnp.zeros((BQ, LANES), F32))
        diag_js, masks, left = _diag_step(jd, True, col <= row)
        st = step(diag_js, st, masks)
        m_run, l_run, acc = _walk_blocks(lambda js, s: step(js, s, None), st, left, True, group=8)
        o_ref[...] = acc / jnp.where(hm[0], l_run[0], l_run[1])
        lse_ref[...] = jnp.where(hm[0], m_run[0] + jnp.log(l_run[0]), m_run[1] + jnp.log(l_run[1]))

    blk = pl.BlockSpec((BQ, LANES), lambda p, i: (i, p))
    out = jax.ShapeDtypeStruct((s_len, MLA_W), F32)
    return pl.pallas_call(
        body, name="mla_fwd", grid=(HEADS // 2, nq),
        in_specs=[pl.BlockSpec((BQ, 2 * LANES), lambda p, i: (i, p)),
                  pl.BlockSpec((s_len, 2 * LANES), lambda p, i: (0, p)),
                  pl.BlockSpec((s_len, LANES), lambda p, i: (0, MLA_W // LANES + p))],
        out_specs=[blk, blk], out_shape=[out, out],
        compiler_params=_cparams("parallel", "parallel"),
    )(q_cat, k_cat, kv)


def _mla_backward(q_cat, k_cat, kv, o, lse, do, s_len):
    nq = s_len // BQ
    scale = 1.0 / math.sqrt(QK_DIM)

    def body(q_ref, k_ref, v_ref, o_ref, lse_ref, do_ref, dq_ref, dk_ref, dv_ref):
        i = pl.program_id(1)

        @pl.when(i == 0)
        def _():
            dk_ref[...] = jnp.zeros(dk_ref.shape, F32)
            dv_ref[...] = jnp.zeros(dv_ref.shape, F32)

        jd = _diag_block(i)
        row, col, trow, tcol, lane, klane = _attn_consts()
        q = q_ref[...]
        o_blk = o_ref[...]
        do_blk = do_ref[...]
        lse_blk = lse_ref[...]
        hm = [lane < HD, lane >= HD]
        kpm = [_pair_mask(BK, h) for h in range(2)]
        qh = [jnp.where(_pair_mask(BQ, h), q, 0) for h in range(2)]
        doh_f = [jnp.where(m, do_blk, 0.0) for m in hm]
        doh = [d.astype(MXU_DTYPE) for d in doh_f]
        delta = [jnp.sum(d * o_blk, axis=1, keepdims=True) for d in doh_f]
        lse_h = [jnp.sum(jnp.where(lane == h * HD, lse_blk, 0.0), axis=1, keepdims=True) for h in range(2)]

        def step(js, st, masks):
            dq = st
            chains = _chains(js)
            kj = [k_ref[_key_slice(j), :] for j in js]
            vj = [v_ref[_key_slice(j), :].astype(MXU_DTYPE) for j in js]
            s = {(h, t): _dot_nt(qh[h], kj[t]) for h, t in chains}
            dp = {(h, t): _dot_nt(doh[h], vj[t]) for h, t in chains}
            adds = [[None] * len(js) for _ in range(2)]

            def accumulate(slot, t, part):
                adds[slot][t] = part if adds[slot][t] is None else adds[slot][t] + part

            for h, t in chains:
                pr = jnp.exp(s[h, t] * scale - lse_h[h])
                if masks:
                    pr = jnp.where(masks[t], pr, 0.0)
                dsb = (pr * (dp[h, t] - delta[h]) * scale).astype(MXU_DTYPE)
                dq = dq + _dot(dsb, jnp.where(kpm[h], kj[t], 0))
                accumulate(0, t, _dot_tn(dsb, qh[h]))
                accumulate(1, t, _dot_tn(pr.astype(MXU_DTYPE), doh[h]))
            for t, j in enumerate(js):
                dk_ref[_key_slice(j), :] += adds[0][t]
                dv_ref[_key_slice(j), :] += adds[1][t]
            return dq

        diag_js, masks, left = _diag_step(jd, True, col <= row)
        st = step(diag_js, jnp.zeros((BQ, 2 * LANES), F32), masks)
        dq_ref[...] = _walk_blocks(lambda js, s: step(js, s, None), st, left, True, group=4)

    blk = pl.BlockSpec((BQ, LANES), lambda p, i: (i, p))
    full = pl.BlockSpec((s_len, LANES), lambda p, i: (0, p))
    out = jax.ShapeDtypeStruct((s_len, MLA_W), F32)
    out_cat = jax.ShapeDtypeStruct((s_len, 2 * MLA_W), F32)
    return pl.pallas_call(
        body, name="mla_bwd", grid=(HEADS // 2, nq),
        in_specs=[pl.BlockSpec((BQ, 2 * LANES), lambda p, i: (i, p)),
                  pl.BlockSpec((s_len, 2 * LANES), lambda p, i: (0, p)),
                  pl.BlockSpec((s_len, LANES), lambda p, i: (0, MLA_W // LANES + p)),
                  blk, blk, blk],
        out_specs=[pl.BlockSpec((BQ, 2 * LANES), lambda p, i: (i, p)),
                   pl.BlockSpec((s_len, 2 * LANES), lambda p, i: (0, p)), full],
        out_shape=[out_cat, out_cat, out],
        compiler_params=_cparams("arbitrary", "arbitrary"),
    )(q_cat, k_cat, kv, o, lse, do)


def _mesh_pos():
    return lax.axis_index("x"), lax.axis_index("y"), lax.axis_index("c")


def _dev_index(px, py, pc):
    return 4 * px + 2 * py + pc


def _all_gather(block, name):
    return _all_gather_parts([block], name)[0]


def _all_gather_parts(blocks, name):
    n = len(blocks)

    def body(*refs):
        x_refs, out_refs = refs[:n], refs[n:2 * n]
        send_sems, recv_sems, local_sems = refs[2 * n:]
        x, y, c = _mesh_pos()
        me, sibling = (x, y, c), (x, y, 1 - c)
        chips = [(1 - x, y), (x, 1 - y), (1 - x, 1 - y)]

        def copy(a, k, blockpos, to, src=None):
            slot = out_refs[a].at[_dev_index(*blockpos)]
            return pltpu.make_async_remote_copy(
                src_ref=slot if src is None else src, dst_ref=slot,
                send_sem=send_sems.at[7 * a + k], recv_sem=recv_sems.at[7 * a + k],
                device_id=to, device_id_type=pl.DeviceIdType.MESH)

        mine = [pltpu.make_async_copy(x_refs[a], out_refs[a].at[_dev_index(*me)], local_sems.at[a]) for a in range(n)]
        for cp in mine:
            cp.start()
        first = []
        for a in range(n):
            first.append(copy(a, 0, me, sibling, src=x_refs[a]))
            first += [copy(a, 1 + j, me, (*chip, c), src=x_refs[a]) for j, chip in enumerate(chips)]
        for cp in first:
            cp.start()
        passed = []
        for j, chip in enumerate(chips):
            for a in range(n):
                copy(a, 1 + j, (*chip, c), me).wait_recv()
                passed.append(copy(a, 4 + j, (*chip, c), sibling))
                passed[-1].start()
        for a in range(n):
            copy(a, 0, sibling, me).wait_recv()
            for j, chip in enumerate(chips):
                copy(a, 4 + j, (*chip, 1 - c), me).wait_recv()
        for cp in first + passed:
            cp.wait_send()
        for cp in mine:
            cp.wait()

    return pl.pallas_call(
        body, name=name,
        out_shape=[jax.ShapeDtypeStruct((N_DEV,) + b.shape, b.dtype) for b in blocks],
        in_specs=[pl.BlockSpec(memory_space=pl.ANY)] * n, out_specs=[pl.BlockSpec(memory_space=pl.ANY)] * n,
        scratch_shapes=[pltpu.SemaphoreType.DMA((7 * n,)), pltpu.SemaphoreType.DMA((7 * n,)),
                        pltpu.SemaphoreType.DMA((n,))],
    )(*blocks)


class _Exchange:
    def __init__(self, arrays):
        self.arrays = list(arrays)
        n = len(self.arrays)
        self.in_specs = [pl.BlockSpec(memory_space=pl.ANY)] * n
        self.out_specs = [pl.BlockSpec(memory_space=pl.ANY)] * n
        self.out_shape = [jax.ShapeDtypeStruct(a.shape, a.dtype) for a in self.arrays]
        self.scratch = [pltpu.SemaphoreType.DMA((7 * n,)), pltpu.SemaphoreType.DMA((7 * n,)),
                        pltpu.SemaphoreType.DMA((n,))]

    def _copies(self, x_refs, out_refs, sems, with_arrivals):
        send_sems, recv_sems, local_sems = sems
        x, y, c = _mesh_pos()
        me = _dev_index(x, y, c)
        flips = [(fx, fy, fc) for fx in (0, 1) for fy in (0, 1) for fc in (0, 1)][1:]
        peers = [(1 - x if fx else x, 1 - y if fy else y, 1 - c if fc else c) for fx, fy, fc in flips]
        mine, sends, arrivals = [], [], []
        for a in range(len(self.arrays)):
            mine.append(pltpu.make_async_copy(x_refs[a].at[me], out_refs[a].at[me], local_sems.at[a]))
            for k, peer in enumerate(peers):
                sends.append(pltpu.make_async_remote_copy(
                    src_ref=x_refs[a].at[_dev_index(*peer)], dst_ref=out_refs[a].at[me],
                    send_sem=send_sems.at[7 * a + k], recv_sem=recv_sems.at[7 * a + k],
                    device_id=peer, device_id_type=pl.DeviceIdType.MESH))
                if not with_arrivals:
                    continue
                arrivals.append(pltpu.make_async_remote_copy(
                    src_ref=x_refs[a].at[me], dst_ref=out_refs[a].at[_dev_index(*peer)],
                    send_sem=send_sems.at[7 * a + k], recv_sem=recv_sems.at[7 * a + k],
                    device_id=peer, device_id_type=pl.DeviceIdType.MESH))
        return mine, sends, arrivals

    def start(self, x_refs, out_refs, sems):
        mine, sends, _ = self._copies(x_refs, out_refs, sems, False)
        for cp in mine + sends:
            cp.start()

    def wait(self, x_refs, out_refs, sems):
        mine, sends, arrivals = self._copies(x_refs, out_refs, sems, True)
        for cp in arrivals:
            cp.wait_recv()
        for cp in sends:
            cp.wait_send()
        for cp in mine:
            cp.wait()


def _sum_blocks(parts, name):
    n, r, c = parts.shape
    row_tiles = [t for t in range(16, min(r, 2048) + 1, 16) if r % t == 0]
    if row_tiles:
        tr, tc = max(row_tiles), c
    else:
        tr, tc = r, 2 * LANES
    assert c % tc == 0

    def body(p_ref, o_ref):
        acc = p_ref[0].astype(F32)
        for s in range(1, n):
            acc = acc + p_ref[s].astype(F32)
        o_ref[...] = acc

    return pl.pallas_call(
        body, name=name, grid=(r // tr, c // tc),
        in_specs=[pl.BlockSpec((n, tr, tc), lambda i, j: (0, i, j))],
        out_specs=pl.BlockSpec((tr, tc), lambda i, j: (i, j)),
        out_shape=jax.ShapeDtypeStruct((r, c), F32),
        compiler_params=_cparams("parallel", "parallel"),
    )(parts)


def _sigmoid(x):
    return 1.0 / (1.0 + jnp.exp(-x))


def _silu(x):
    return x * _sigmoid(x)


def _silu_grad(x):
    s = _sigmoid(x)
    return s * (1.0 + x * (1.0 - s))


def _colsum(x):
    return jnp.sum(x, axis=0, keepdims=True)


def _rms(x):
    return lax.rsqrt(jnp.mean(x * x, axis=-1, keepdims=True) + EPS)


def _rms_bwd(xn, r, dxn):
    return r * (dxn - xn * jnp.mean(dxn * xn, axis=-1, keepdims=True))


def _adamw(w, g, m, v):
    m = ADAM_B1 * m + (1.0 - ADAM_B1) * g
    v = ADAM_B2 * v + (1.0 - ADAM_B2) * jnp.square(g)
    m_hat = m / (1.0 - ADAM_B1 ** ADAM_STEP)
    v_hat = v / (1.0 - ADAM_B2 ** ADAM_STEP)
    delta = -ADAM_LR * (m_hat / (jnp.sqrt(v_hat) + ADAM_EPS) + ADAM_WD * w)
    return delta, m, v


def _adamw_call(w, g, m, v, name):
    r, c = w.shape
    if r % 256 == 0:
        tr, tc = 256, c
    elif r * c <= 256 * 1024 or c % (2 * LANES):
        tr, tc = r, c
    else:
        tr, tc = r, 2 * LANES

    def body(w_ref, g_ref, m_ref, v_ref, d_out, m_out, v_out):
        d_out[...], m_out[...], v_out[...] = _adamw(w_ref[...], g_ref[...], m_ref[...], v_ref[...])

    spec = pl.BlockSpec((tr, tc), lambda i, j: (i, j))
    return pl.pallas_call(
        body, name=name, grid=(r // tr, c // tc), in_specs=[spec] * 4, out_specs=[spec] * 3,
        out_shape=[jax.ShapeDtypeStruct((r, c), F32)] * 3, compiler_params=_cparams("parallel", "parallel"),
    )(w, g, m, v)


def _uq_to_kernel_layout(w):
    lead = w.shape[:-1]
    t = w.reshape(lead + (HEADS, QK_DIM))
    return jnp.concatenate([t[..., :NOPE].reshape(lead + (HEADS * NOPE,)),
                            t[..., NOPE:NOPE + ROPE // 2].reshape(lead + (LANES,)),
                            t[..., NOPE + ROPE // 2:].reshape(lead + (LANES,))], axis=-1)


def _uq_from_kernel_layout(w):
    lead = w.shape[:-1]
    nope = w[..., :HEADS * NOPE].reshape(lead + (HEADS, NOPE))
    r1 = w[..., HEADS * NOPE:HEADS * NOPE + LANES].reshape(lead + (HEADS, ROPE // 2))
    r2 = w[..., HEADS * NOPE + LANES:].reshape(lead + (HEADS, ROPE // 2))
    return jnp.concatenate([nope, r1, r2], axis=-1).reshape(lead + (HEADS * QK_DIM,))


def _ukv_to_kernel_layout(w):
    lead = w.shape[:-1]
    t = w.reshape(lead + (HEADS, NOPE + HD))
    return jnp.concatenate([t[..., :NOPE].reshape(lead + (HEADS * NOPE,)),
                            t[..., NOPE:].reshape(lead + (HEADS * HD,))], axis=-1)


def _ukv_from_kernel_layout(w):
    lead = w.shape[:-1]
    kn = w[..., :HEADS * NOPE].reshape(lead + (HEADS, NOPE))
    vv = w[..., HEADS * NOPE:].reshape(lead + (HEADS, HD))
    return jnp.concatenate([kn, vv], axis=-1).reshape(lead + (HEADS * (NOPE + HD),))


def _w_in_t_to_kernel_layout(wt):
    sb = wt[0:2048]
    c_q = wt[2048:2432]
    c_kv = wt[2432:2688]
    k_rot = wt[2688:2720]
    mla_z = wt[2720:3232]
    gates = wt[3232:5280]
    zeros = jnp.zeros((LANES, wt.shape[1]), wt.dtype)
    k1 = jnp.tile(k_rot[:ROPE // 2], (HEADS, 1))
    k2 = jnp.tile(k_rot[ROPE // 2:], (HEADS, 1))
    return jnp.concatenate([gates, sb, mla_z, c_q, zeros, c_kv, k1, k2], axis=0)


def _w_in_t_from_kernel_layout(gt, g_rot):
    return jnp.concatenate([gt[O_SBQ:O_SBQ + 2048], gt[O_CQ:O_CQ + Q_RANK], gt[O_CKV:O_CKV + KV_RANK],
                            g_rot.astype(gt.dtype), gt[O_MLAZ:O_MLAZ + MLA_W], gt[O_GA:O_GA + 2 * D]], axis=0)


def kernel(x, c, positions, w_ada, b_ada, norm_gain, w_in, q_norm_gain, w_uq, kv_norm_gain, w_ukv, w_branch_a, w_branch_b, w_out, final_norm_gain, loss_target, m_w_ada, m_b_ada, m_norm_gain, m_w_in, m_q_norm_gain, m_w_uq, m_kv_norm_gain, m_w_ukv, m_w_branch_a, m_w_branch_b, m_w_out, m_final_norm_gain, v_w_ada, v_b_ada, v_norm_gain, v_w_in, v_q_norm_gain, v_w_uq, v_kv_norm_gain, v_w_ukv, v_w_branch_a, v_w_branch_b, v_w_out, v_final_norm_gain):
    s_len = x.shape[1]
    me = _dev_index(*_mesh_pos())
    x2d = x[0]
    tgt = loss_target[0]

    w_in_t = w_in[0].T.astype(BF16)
    big = [w_uq[0], w_ukv[0], w_branch_a[0], w_branch_b[0], w_out[0]]
    big_sizes = [int(w.size) for w in big]
    packed = jnp.concatenate([w.astype(BF16).reshape(-1, LANES) for w in big], axis=0)
    g_in_t, c_all = _all_gather_parts([w_in_t, c.reshape(8, LANES)], "gather_w_in")
    c_all = c_all.reshape(N_DEV, D)
    w_in_kt = _w_in_t_to_kernel_layout(g_in_t.reshape(N_DEV * w_in_t.shape[0], D))

    mod_cols = _mm(c_all, w_ada[0], name="ada_mod")
    mod_all = _all_gather(mod_cols, "gather_mod")
    mod = lax.dynamic_index_in_dim(mod_all, me, axis=1, keepdims=False).reshape(1, 3 * D)
    mod_shift, mod_scale, mod_gate = mod[:, :D], mod[:, D:2 * D], mod[:, 2 * D:]
    b_shift, b_scale, b_gate = b_ada[:, :D], b_ada[:, D:2 * D], b_ada[:, 2 * D:]
    g1 = norm_gain
    gq, gkv = q_norm_gain, kv_norm_gain
    gf = final_norm_gain.reshape(1, D)

    def f_h(x_, g1_, ms, bs, msc, bsc):
        xn = x_ * _rms(x_)
        return (xn * g1_ * (1.0 + (msc + bsc)) + (ms + bs),), ()

    (h,) = _rowwise(f_h, [x2d], [g1, mod_shift, b_shift, mod_scale, b_scale], [(D, BF16)], name="ada_norm")
    proj = _mm(h, w_in_kt, tb=True, name="proj_in", tiles=(min(s_len, 1024), IN_PAD // 2, D))

    (o_a, sb_tails), (gathered,) = _sb_forward(
        proj, s_len, _Exchange([jnp.broadcast_to(packed[None], (N_DEV,) + packed.shape)]))
    offs = [0]
    for n in big_sizes:
        offs.append(offs[-1] + n // LANES)

    def unpack(t, shape):
        return gathered[:, offs[t]:offs[t + 1], :].reshape((N_DEV,) + shape)

    def cols(t, shape):
        return unpack(t, shape).transpose(1, 0, 2).reshape(shape[0], N_DEV * shape[1])

    w_uq_k = _uq_to_kernel_layout(cols(0, big[0].shape))
    w_ukv_k = _ukv_to_kernel_layout(cols(1, big[1].shape))
    w_a_f = cols(2, big[2].shape)
    w_b_f = cols(3, big[3].shape)
    w_out_f = unpack(4, big[4].shape).reshape(D, D)

    def f_lat(cq, ckv, gq_, gkv_):
        return (cq * _rms(cq) * gq_, ckv * _rms(ckv) * gkv_), ()

    cq_n, ckv_n = _rowwise(f_lat, [(proj, O_CQ // Q_RANK, Q_RANK), (proj, O_CKV // KV_RANK, KV_RANK)], [gq, gkv],
                           [(Q_RANK, BF16), (KV_RANK, BF16)], name="latent_norm")
    q_mla = _mm(cq_n, w_uq_k, name="q_up")
    kv = _mm(ckv_n, w_ukv_k, name="kv_up")

    inv_freq = ROPE_BASE ** (-jnp.arange(0, ROPE, 2, dtype=F32) / ROPE)
    inv_freq_t = jnp.tile(inv_freq, HEADS).reshape(1, LANES)
    pos_col = positions.reshape(s_len, 1).astype(F32)

    pairs = HEADS // 2

    def f_rope(pos, qn, q1, q2, kn, k1, k2, freq):
        ang = pos * freq
        cs, sn = jnp.cos(ang), jnp.sin(ang)
        q1r, q2r = q1 * cs - q2 * sn, q1 * sn + q2 * cs
        k1r, k2r = k1 * cs - k2 * sn, k1 * sn + k2 * cs
        lane = lax.broadcasted_iota(jnp.int32, q1.shape, 1)
        first, second = lane < ROPE, (lane >= ROPE) & (lane < 2 * ROPE)
        k_rot = jnp.where(first, k1r, jnp.where(second, k2r, 0.0))
        q_parts, k_parts = [], []
        for p in range(pairs):
            q_rot = jnp.where(first, pltpu.roll(q1r, (LANES - ROPE * p) % LANES, 1),
                              jnp.where(second, pltpu.roll(q2r, (LANES + ROPE - ROPE * p) % LANES, 1), 0.0))
            q_parts += [qn[:, LANES * p:LANES * (p + 1)], q_rot]
            k_parts += [kn[:, LANES * p:LANES * (p + 1)], k_rot]
        return (jnp.concatenate(q_parts, axis=1), jnp.concatenate(k_parts, axis=1), cs, sn), ()

    q_cat, k_cat, cos_t, sin_t = _rowwise(
        f_rope, [pos_col, (q_mla, 0, MLA_W), (q_mla, 4, LANES), (q_mla, 5, LANES), (kv, 0, MLA_W),
                 (proj, O_KROT // LANES, LANES), (proj, O_KROT // LANES + 1, LANES)], [inv_freq_t],
        [(2 * MLA_W, BF16), (2 * MLA_W, BF16), (LANES, F32), (LANES, F32)], name="rope")

    o_b, lse = _mla_forward(q_cat, k_cat, kv, s_len)

    def f_gate(oa, za, ob, zb):
        return (oa * _silu(za), ob * _silu(zb)), ()

    ya_in, yb_in = _rowwise(f_gate, [o_a, (proj, O_SBZ // SB_W, SB_W), o_b, (proj, O_MLAZ // MLA_W, MLA_W)], [],
                            [(SB_W, BF16), (MLA_W, BF16)], name="branch_gate")
    y_a = _mm(ya_in, w_a_f, name="branch_a")

    def f_merge(yb, ga, gb, ya):
        return (yb, _sigmoid(ga) * ya + _sigmoid(gb) * yb), ()

    y_b, merged = _mm(yb_in, w_b_f, name="branch_b_merge", tiles=(min(s_len, 512), D, MLA_W),
                      epilogue=(f_merge, [(proj, O_GA // D, D), (proj, O_GB // D, D), y_a], [], [(D, F32), (D, BF16)], []))

    def f_loss(out_, x_, t_, mg, bg, gf_):
        gate = mg + bg
        x2 = x_ + gate * out_
        r2 = _rms(x2)
        xn2 = x2 * r2
        err = xn2 * gf_ - t_
        loss = jnp.full((1, LANES), 0.5 / D, F32) * jnp.sum(err * err)
        dy = err * (1.0 / D)
        dx2 = _rms_bwd(xn2, r2, dy * gf_)
        return (dx2, dx2 * gate), (loss, _colsum(dy * xn2), _colsum(dx2 * out_))

    dx2, d_out, loss_part, d_gf, d_gate = _mm(
        merged, w_out_f, name="out_proj_loss", tiles=(min(s_len, 512), D, D),
        epilogue=(f_loss, [x2d, tgt], [mod_gate, b_gate, gf], [(D, F32), (D, BF16)], [LANES, D, D]))

    dw_out = _mm(merged, d_out, ta=True, name="dw_out")

    def f_dmerge(dm, ga, gb, ya, yb):
        sa, sb = _sigmoid(ga), _sigmoid(gb)
        return (dm * sa, dm * sb, dm * ya * sa * (1.0 - sa), dm * yb * sb * (1.0 - sb)), ()

    d_ya, d_yb, d_ga, d_gb = _mm(
        d_out, w_out_f, tb=True, name="d_merge", tiles=(min(s_len, 256), D, D),
        epilogue=(f_dmerge, [(proj, O_GA // D, D), (proj, O_GB // D, D), y_a, y_b], [], [(D, BF16)] * 4, []))
    dw_a = _mm(ya_in, d_ya, ta=True, name="dw_branch_a")
    dw_b = _mm(yb_in, d_yb, ta=True, name="dw_branch_b")

    def f_dgate(d_in, o_, z_):
        return (d_in * _silu(z_), d_in * o_ * _silu_grad(z_)), ()

    d_oa, d_sbz = _mm(d_ya, w_a_f, tb=True, name="d_branch_a",
                      epilogue=(f_dgate, [o_a, (proj, O_SBZ // SB_W, SB_W)], [], [(SB_W, F32), (SB_W, BF16)], []))
    d_ob, d_mlaz = _mm(d_yb, w_b_f, tb=True, name="d_branch_b",
                       epilogue=(f_dgate, [o_b, (proj, O_MLAZ // MLA_W, MLA_W)], [], [(MLA_W, F32), (MLA_W, BF16)], []))

    dq_cat, dk_cat, dv_b = _mla_backward(q_cat, k_cat, kv, o_b, lse, d_ob, s_len)

    def f_drope(dq, dk, dv_, cs, sn):
        lane = lax.broadcasted_iota(jnp.int32, cs.shape, 1)
        first, second = lane < ROPE, (lane >= ROPE) & (lane < 2 * ROPE)
        dq1 = dq2 = dk1 = dk2 = None
        for p in range(pairs):
            q_rot = dq[:, LANES * (2 * p + 1):LANES * (2 * p + 2)]
            k_rot = dk[:, LANES * (2 * p + 1):LANES * (2 * p + 2)]
            parts = (pltpu.roll(jnp.where(first, q_rot, 0.0), (ROPE * p) % LANES, 1),
                     pltpu.roll(jnp.where(second, q_rot, 0.0), (LANES - ROPE + ROPE * p) % LANES, 1),
                     jnp.where(first, k_rot, 0.0), jnp.where(second, k_rot, 0.0))
            if p == 0:
                dq1, dq2, dk1, dk2 = parts
            else:
                dq1, dq2, dk1, dk2 = dq1 + parts[0], dq2 + parts[1], dk1 + parts[2], dk2 + parts[3]
        dqn_ = [dq[:, 2 * LANES * p:2 * LANES * p + LANES] for p in range(pairs)]
        dkn_ = [dk[:, 2 * LANES * p:2 * LANES * p + LANES] for p in range(pairs)]
        return (jnp.concatenate(dqn_ + [dq1 * cs + dq2 * sn, dq2 * cs - dq1 * sn], axis=1),
                jnp.concatenate(dkn_ + [dv_], axis=1),
                jnp.concatenate([dk1 * cs + dk2 * sn, dk2 * cs - dk1 * sn], axis=1)), ()

    dq_k, dkv_k, d_krot = _rowwise(f_drope, [dq_cat, dk_cat, dv_b, cos_t, sin_t], [],
                                   [(HEADS * QK_DIM, BF16), (2 * MLA_W, BF16), (2 * LANES, BF16)], name="d_rope")
    dw_uq_k = _mm(cq_n, dq_k, ta=True, name="dw_uq")
    dw_ukv_k = _mm(ckv_n, dkv_k, ta=True, name="dw_ukv")

    def f_dlat(d_normed, latent, gain):
        r = _rms(latent)
        normed = latent * r
        return (_rms_bwd(normed, r, d_normed * gain),), (_colsum(d_normed * normed),)

    d_cq, d_gq = _mm(dq_k, w_uq_k, tb=True, name="d_cq_norm",
                     epilogue=(f_dlat, [(proj, O_CQ // Q_RANK, Q_RANK)], [gq], [(Q_RANK, BF16)], [Q_RANK]))
    d_ckv, d_gkv = _mm(dkv_k, w_ukv_k, tb=True, name="d_ckv_norm",
                       epilogue=(f_dlat, [(proj, O_CKV // KV_RANK, KV_RANK)], [gkv], [(KV_RANK, BF16)], [KV_RANK]))

    def col_blocks(g):
        kdim, n8 = g.shape
        return g.astype(BF16).reshape(kdim, N_DEV, n8 // N_DEV).transpose(1, 0, 2).reshape(N_DEV, -1, LANES)

    g_blocks = jnp.concatenate([col_blocks(_uq_from_kernel_layout(dw_uq_k)), col_blocks(_ukv_from_kernel_layout(dw_ukv_k)),
                                col_blocks(dw_a), col_blocks(dw_b), dw_out.astype(BF16).reshape(N_DEV, -1, LANES)], axis=1)
    (d_sbq, d_sbk, d_sbv), (g_recv,) = _sb_backward(proj, sb_tails, d_oa, s_len, _Exchange([g_blocks]))

    d_proj = jnp.concatenate([d_ga, d_gb, d_sbq.astype(BF16), d_sbk.astype(BF16), d_sbv.astype(BF16), d_sbz, d_mlaz,
                              d_cq, jnp.zeros((s_len, LANES), BF16), d_ckv, d_krot], axis=1)
    dw_in_kt = _mm(d_proj, h, ta=True, out_dtype=BF16, name="dw_in", tiles=(512, D, s_len))

    def krot_body(t_ref, o_ref):
        half = ROPE // 2
        for part in range(2):
            acc = t_ref[part * LANES:part * LANES + half, :].astype(F32)
            for hh in range(1, HEADS):
                acc = acc + t_ref[part * LANES + hh * half:part * LANES + (hh + 1) * half, :].astype(F32)
            o_ref[part * half:(part + 1) * half, :] = acc

    dw_krot = pl.pallas_call(krot_body, name="dw_krot_sum", out_shape=jax.ShapeDtypeStruct((ROPE, D), F32))(
        dw_in_kt[O_KROT:O_KROT + 2 * LANES])

    g_in_blocks = _w_in_t_from_kernel_layout(dw_in_kt, dw_krot).reshape(N_DEV, -1, D)
    def f_dx(dh_, x_, dx2_, g1_, msc, bsc):
        r = _rms(x_)
        xn = x_ * r
        dn1 = dh_ * (1.0 + (msc + bsc))
        return ((dx2_ + _rms_bwd(xn, r, dn1 * g1_),),
                (_colsum(dh_), _colsum(dh_ * (xn * g1_)), _colsum(dn1 * xn)))

    (grad_x2d, d_shift, d_scale, d_g1), (g_in_recv,) = _mm(
        d_proj, w_in_kt, name="d_h", tiles=(min(s_len, 512), D, 512), exchange=_Exchange([g_in_blocks]),
        epilogue=(f_dx, [x2d, dx2], [g1, mod_scale, b_scale], [(D, F32)], [D, D, D]))

    g_in_sum_t = _sum_blocks(g_in_recv, "sum_grads_w_in")
    g_sum = _sum_blocks(g_recv, "sum_grads")
    g_big = [g_sum[offs[t]:offs[t + 1]].reshape(big[t].shape) for t in range(5)]

    small = jnp.concatenate([d_shift, d_scale, d_gate, d_g1, d_gq, d_gkv, d_gf, loss_part], axis=1)
    n_small = small.shape[1]
    pad = (-n_small) % (8 * LANES)
    small = jnp.pad(small, ((0, 0), (0, pad))).reshape(-1, LANES)
    small_all = _all_gather(small, "gather_small")
    small_sum = _sum_blocks(small_all, "sum_small").reshape(1, -1)
    g_b_ada = small_sum[:, :3 * D]
    g_g1 = small_sum[:, 3 * D:4 * D]
    g_gq = small_sum[:, 4 * D:4 * D + Q_RANK]
    g_gkv = small_sum[:, 4 * D + Q_RANK:4 * D + Q_RANK + KV_RANK]
    g_gf = small_sum[:, 4 * D + Q_RANK + KV_RANK:4 * D + Q_RANK + KV_RANK + D]

    dmod_all = small_all.reshape(N_DEV, -1)[:, :3 * D]
    dmod_cols = lax.dynamic_slice_in_dim(dmod_all, me * (3 * D // N_DEV), 3 * D // N_DEV, axis=1)
    g_w_ada = _mm(c_all, dmod_cols, ta=True, name="dw_ada")

    loss = small_sum[0, n_small - LANES]

    names = ["w_ada", "b_ada", "norm_gain", "w_in", "q_norm_gain", "w_uq", "kv_norm_gain", "w_ukv",
             "w_branch_a", "w_branch_b", "w_out", "final_norm_gain"]
    weights = dict(w_ada=w_ada, b_ada=b_ada, norm_gain=norm_gain, w_in=w_in, q_norm_gain=q_norm_gain, w_uq=w_uq,
                   kv_norm_gain=kv_norm_gain, w_ukv=w_ukv, w_branch_a=w_branch_a, w_branch_b=w_branch_b, w_out=w_out,
                   final_norm_gain=final_norm_gain)
    moms = dict(w_ada=m_w_ada, b_ada=m_b_ada, norm_gain=m_norm_gain, w_in=m_w_in, q_norm_gain=m_q_norm_gain,
                w_uq=m_w_uq, kv_norm_gain=m_kv_norm_gain, w_ukv=m_w_ukv, w_branch_a=m_w_branch_a,
                w_branch_b=m_w_branch_b, w_out=m_w_out, final_norm_gain=m_final_norm_gain)
    vels = dict(w_ada=v_w_ada, b_ada=v_b_ada, norm_gain=v_norm_gain, w_in=v_w_in, q_norm_gain=v_q_norm_gain,
                w_uq=v_w_uq, kv_norm_gain=v_kv_norm_gain, w_ukv=v_w_ukv, w_branch_a=v_w_branch_a,
                w_branch_b=v_w_branch_b, w_out=v_w_out, final_norm_gain=v_final_norm_gain)
    grads2d = dict(w_ada=g_w_ada, b_ada=g_b_ada, norm_gain=g_g1, w_in=g_in_sum_t, q_norm_gain=g_gq, w_uq=g_big[0],
                   kv_norm_gain=g_gkv, w_ukv=g_big[1], w_branch_a=g_big[2], w_branch_b=g_big[3], w_out=g_big[4],
                   final_norm_gain=g_gf)

    grads, deltas, new_m, new_v = [], [], [], []
    for n in names:
        w = weights[n]
        if n == "w_in":
            to2d = lambda t: t[0].T
            back = lambda t: t.T[None]
        else:
            shape2d = grads2d[n].shape
            to2d = lambda t, s=shape2d: t.reshape(s)
            back = lambda t, s=w.shape: t.reshape(s)
        d_, m_, v_ = _adamw_call(to2d(w), grads2d[n], to2d(moms[n]), to2d(vels[n]), "adamw_" + n)
        grads.append(back(grads2d[n]))
        deltas.append(back(d_))
        new_m.append(back(m_))
        new_v.append(back(v_))

    return (loss, grad_x2d.reshape(x.shape), *grads, *deltas, *new_m, *new_v)
```

```python
import functools
import math

import jax
import jax.numpy as jnp
from jax import lax
from jax.experimental import pallas as pl
from jax.experimental.pallas import tpu as pltpu

F32 = jnp.float32
BF16 = jnp.bfloat16
MXU_DTYPE = jnp.bfloat16

N_DEV = 8
D = 1024
HEADS = 8
HD = 64
SB_W = 512
MLA_W = 512
Q_RANK = 384
KV_RANK = 256
ROPE = 32
NOPE = 64
QK_DIM = NOPE + ROPE
EPS = 1e-6
ROPE_BASE = 10000.0

ADAM_LR = 0.001
ADAM_B1 = 0.9
ADAM_B2 = 0.999
ADAM_EPS = 1e-08
ADAM_WD = 0.01
ADAM_STEP = 10

LANES = 128
VMEM_LIMIT = 48 * 1024 * 1024

O_GA, O_GB = 0, 1024
O_SBQ, O_SBK, O_SBV, O_SBZ = 2048, 2560, 3072, 3584
O_MLAZ = 4096
O_CQ = 4608
O_CKV = 5120
O_KROT = 5376
IN_PAD = 5632

BQ = 256
BK = 256


def _cparams(*sem):
    return pltpu.CompilerParams(dimension_semantics=sem, vmem_limit_bytes=VMEM_LIMIT)


def _tile_of(n, cap=512):
    if n <= cap:
        return n
    for t in (1024, 768, 512, 384, 256, 128):
        if t <= cap and n % t == 0:
            return t
    raise ValueError(n)


def _rowwise(fn, rows, vecs, outs, reds=(), *, name, tile=512):
    norm = []
    for r in rows:
        if isinstance(r, tuple):
            arr, cb, w = r[:3]
            ro = r[3] if len(r) > 3 else 0
        else:
            arr, cb, w, ro = r, 0, r.shape[1], 0
        norm.append((arr, cb, w, ro))
    s_len = norm[0][0].shape[0]
    tile = min(tile, s_len)
    assert s_len % tile == 0
    n_row, n_vec, n_out, n_red = len(norm), len(vecs), len(outs), len(reds)

    def body(*refs):
        step = pl.program_id(0)
        row_refs = refs[:n_row]
        vec_refs = refs[n_row:n_row + n_vec]
        out_refs = refs[n_row + n_vec:n_row + n_vec + n_out]
        red_refs = refs[n_row + n_vec + n_out:]
        row_res, red_res = fn(*[r[...] for r in row_refs], *[v[...] for v in vec_refs])
        for o, val in zip(out_refs, row_res):
            o[...] = val.astype(o.dtype)
        if n_red:
            @pl.when(step == 0)
            def _():
                for r in red_refs:
                    r[...] = jnp.zeros(r.shape, r.dtype)
            for r, val in zip(red_refs, red_res):
                r[...] += val

    in_specs = []
    for arr, cb, w, ro in norm:
        in_specs.append(pl.BlockSpec((tile, w), functools.partial(lambda i, cb, rb: (i + rb, cb), cb=cb, rb=ro // tile)))
        assert ro % tile == 0
    for v in vecs:
        in_specs.append(pl.BlockSpec(v.shape, lambda i: (0, 0)))
    out_shape = [jax.ShapeDtypeStruct((s_len, w), dt) for w, dt in outs]
    out_specs = [pl.BlockSpec((tile, w), lambda i: (i, 0)) for w, _ in outs]
    out_shape += [jax.ShapeDtypeStruct((1, w), F32) for w in reds]
    out_specs += [pl.BlockSpec((1, w), lambda i: (0, 0)) for w in reds]
    res = pl.pallas_call(
        body, name=name, grid=(s_len // tile,), in_specs=in_specs, out_specs=out_specs, out_shape=out_shape,
        compiler_params=_cparams("arbitrary" if n_red else "parallel"),
    )(*[a for a, _, _, _ in norm], *vecs)
    return res


def _mm(a, b, *, ta=False, tb=False, out_dtype=F32, name, exchange=None, tiles=None, epilogue=None):
    m, k = (a.shape[1], a.shape[0]) if ta else a.shape
    n = b.shape[0] if tb else b.shape[1]
    assert (b.shape[1] if tb else b.shape[0]) == k
    tm, tn, tk = tiles or (_tile_of(m, 1024), _tile_of(n, 1024 if n <= 1024 else 512), _tile_of(k, 1024))
    assert m % tm == 0 and n % tn == 0 and k % tk == 0
    ni, nj, nk = m // tm, n // tn, k // tk
    dims = (((0 if ta else 1,), (1 if tb else 0,)), ((), ()))
    n_ex = len(exchange.arrays) if exchange else 0
    fn, rows, vecs, outs, reds = epilogue or (None, (), (), (), ())
    rows = [r if isinstance(r, tuple) else (r, 0, r.shape[1]) for r in rows]
    assert not epilogue or tn == n
    n_res = len(outs) + len(reds) if epilogue else 1

    def body(*refs):
        a_ref, b_ref = refs[:2]
        row_refs, refs = refs[2:2 + len(rows)], refs[2 + len(rows):]
        vec_refs, refs = refs[:len(vecs)], refs[len(vecs):]
        x_refs, refs = refs[:n_ex], refs[n_ex:]
        res_refs, refs = refs[:n_res], refs[n_res:]
        out_refs, acc_ref, sems = refs[:n_ex], refs[n_ex], refs[n_ex + 1:]
        i, j, kk = pl.program_id(0), pl.program_id(1), pl.program_id(2)
        first = (i == 0) & (j == 0) & (kk == 0)

        if exchange:
            @pl.when(first)
            def _():
                exchange.start(x_refs, out_refs, sems)

        if reds:
            @pl.when(first)
            def _():
                for r in res_refs[len(outs):]:
                    r[...] = jnp.zeros(r.shape, r.dtype)

        @pl.when(kk == 0)
        def _():
            acc_ref[...] = jnp.zeros(acc_ref.shape, F32)

        acc_ref[...] += lax.dot_general(a_ref[...].astype(MXU_DTYPE), b_ref[...].astype(MXU_DTYPE), dims,
                                        preferred_element_type=F32)

        @pl.when(kk == nk - 1)
        def _():
            if not epilogue:
                res_refs[0][...] = acc_ref[...].astype(res_refs[0].dtype)
                return
            row_res, red_res = fn(acc_ref[...], *[r[...] for r in row_refs], *[v[...] for v in vec_refs])
            for o, val in zip(res_refs, row_res):
                o[...] = val.astype(o.dtype)
            for r, val in zip(res_refs[len(outs):], red_res):
                r[...] += val

        if exchange:
            @pl.when((i == ni - 1) & (j == nj - 1) & (kk == nk - 1))
            def _():
                exchange.wait(x_refs, out_refs, sems)

    a_spec = pl.BlockSpec((tk, tm), lambda i, j, kk: (kk, i)) if ta else pl.BlockSpec((tm, tk), lambda i, j, kk: (i, kk))
    b_spec = pl.BlockSpec((tn, tk), lambda i, j, kk: (j, kk)) if tb else pl.BlockSpec((tk, tn), lambda i, j, kk: (kk, j))
    in_specs = [a_spec, b_spec]
    in_specs += [pl.BlockSpec((tm, w), functools.partial(lambda i, j, kk, cb: (i, cb), cb=cb)) for _, cb, w in rows]
    in_specs += [pl.BlockSpec(v.shape, lambda i, j, kk: (0, 0)) for v in vecs]
    if epilogue:
        res_specs = [pl.BlockSpec((tm, w), lambda i, j, kk: (i, 0)) for w, _ in outs]
        res_specs += [pl.BlockSpec((1, w), lambda i, j, kk: (0, 0)) for w in reds]
        res_shape = [jax.ShapeDtypeStruct((m, w), dt) for w, dt in outs] + [jax.ShapeDtypeStruct((1, w), F32) for w in reds]
    else:
        res_specs = [pl.BlockSpec((tm, tn), lambda i, j, kk: (i, j))]
        res_shape = [jax.ShapeDtypeStruct((m, n), out_dtype)]
    ordered = bool(exchange or reds)
    res = pl.pallas_call(
        body, name=name, grid=(ni, nj, nk),
        in_specs=in_specs + (exchange.in_specs if exchange else []),
        out_specs=res_specs + (exchange.out_specs if exchange else []),
        out_shape=res_shape + (exchange.out_shape if exchange else []),
        scratch_shapes=[pltpu.VMEM((tm, tn), F32)] + (exchange.scratch if exchange else []),
        compiler_params=_cparams(*(("arbitrary",) * 3 if ordered else ("parallel", "parallel", "arbitrary"))),
    )(a, b, *[r[0] for r in rows], *vecs, *(exchange.arrays if exchange else []))
    main = res[:n_res] if epilogue else res[0]
    return (main, res[n_res:]) if exchange else main


_NT = (((1,), (1,)), ((), ()))
_TN = (((0,), (0,)), ((), ()))


def _dot(a, b):
    return jnp.dot(a, b, preferred_element_type=F32)


def _dot_nt(a, b):
    return lax.dot_general(a, b, _NT, preferred_element_type=F32)


def _dot_tn(a, b):
    return lax.dot_general(a, b, _TN, preferred_element_type=F32)


def _running_sum(x, tri):
    return _dot(x.astype(MXU_DTYPE), tri)


def _neg_softplus(z):
    u = jnp.exp2(jnp.abs(z) * (-1.0 / math.log(2.0)))
    return -jnp.maximum(z, 0.0) - jnp.log(1.0 + u)


def _walk_blocks(step, st, n, descending, group=2):
    done = 0
    size = group
    while size >= 1:
        def trip(t, s, size=size, done=done):
            js = [done + size * t + g for g in range(size)]
            return step([n - 1 - j for j in js] if descending else js, s)

        trips = (n - done) // size
        st = lax.fori_loop(0, trips, trip, st)
        done = done + size * trips
        size //= 2
    return st


def _chains(js):
    return [(h, t) for t in range(len(js)) for h in range(2)]


def _rowsum(x):
    return jnp.sum(x, axis=1, keepdims=True)


def _attn_consts():
    row = lax.broadcasted_iota(jnp.int32, (BQ, BK), 0)
    col = lax.broadcasted_iota(jnp.int32, (BQ, BK), 1)
    trow = lax.broadcasted_iota(jnp.int32, (BK, BK), 0)
    tcol = lax.broadcasted_iota(jnp.int32, (BK, BK), 1)
    lane = lax.broadcasted_iota(jnp.int32, (BQ, LANES), 1)
    klane = lax.broadcasted_iota(jnp.int32, (BK, LANES), 1)
    return row, col, trow, tcol, lane, klane


DIAG = BQ // BK
assert BQ == DIAG * BK


def _diag_block(i):
    return i * DIAG


assert DIAG == 1


def _diag_step(jd, descending, diag_mask, group=4):
    below = list(range(group)) if descending else list(reversed(range(group)))
    js = [jnp.maximum(jd - o, 0) for o in below]
    masks = [diag_mask if o == 0 else jd - o >= 0 for o in below]
    return js, masks, jnp.maximum(jd - (group - 1), 0)


def _key_slice(j):
    return pl.ds(pl.multiple_of(j * BK, BK), BK)


def _sb_forward(proj, s_len, exchange):
    nq = s_len // BQ
    assert s_len // BK <= HD
    qc, kc, vc = O_SBQ // LANES, O_SBK // LANES, O_SBV // LANES
    n_ex = len(exchange.arrays)

    def body(q_ref, k_ref, v_ref, *rest):
        x_refs, (o_ref, tails_ref) = rest[:n_ex], rest[n_ex:n_ex + 2]
        out_refs, sems = rest[n_ex + 2:2 * n_ex + 2], rest[2 * n_ex + 2:]
        p = pl.program_id(0)
        i = pl.program_id(1)
        jd = _diag_block(i)

        @pl.when((p == 0) & (i == 0))
        def _():
            exchange.start(x_refs, out_refs, sems)

        row, col, trow, tcol, lane, klane = _attn_consts()
        tri = (trow >= tcol).astype(MXU_DTYPE)
        q = q_ref[...] * 0.125
        qh = [jnp.where(lane < HD, q, 0.0).astype(MXU_DTYPE), jnp.where(lane >= HD, q, 0.0).astype(MXU_DTYPE)]

        km = [klane < HD, klane >= HD]

        def step(js, st, masks):
            carry, acc, tail = st
            chains = _chains(js)
            kj = [k_ref[_key_slice(j), :].astype(MXU_DTYPE) for j in js]
            vj = [v_ref[_key_slice(j), :].astype(MXU_DTYPE) for j in js]
            z = {(h, t): _dot_nt(qh[h], kj[t]) for h, t in chains}
            run = list(carry)
            suf, carry_in = {}, {}
            for h, t in chains:
                lom = _neg_softplus(z[h, t])
                if masks:
                    lom = jnp.where(masks[t], lom, 0.0)
                suf[h, t] = _running_sum(lom, tri)
                carry_in[h, t] = run[h]
                run[h] = run[h] + _rowsum(lom)
            for h, t in chains:
                a = jnp.exp(z[h, t] + suf[h, t] + carry_in[h, t])
                if masks:
                    a = jnp.where(masks[t], a, 0.0)
                acc = acc + _dot(a.astype(MXU_DTYPE), jnp.where(km[h], vj[t], 0))
                tail_lane = js[t] if not masks or masks[t].ndim else jnp.where(masks[t], js[t], -LANES)
                tail = jnp.where(lane == h * HD + tail_lane, carry_in[h, t], tail)
            return tuple(run), acc, tail

        zero = jnp.zeros((BQ, LANES), F32)
        diag_js, masks, left = _diag_step(jd, True, col < row)
        st = step(diag_js, ((jnp.zeros((BQ, 1), F32),) * 2, zero, zero), masks)
        st = _walk_blocks(lambda js, s: step(js, s, None), st, left, True, group=8)
        o_ref[...] = st[1]
        tails_ref[...] = st[2]

        @pl.when((p == HEADS // 2 - 1) & (i == nq - 1))
        def _():
            exchange.wait(x_refs, out_refs, sems)

    blk = pl.BlockSpec((BQ, LANES), lambda p, i: (i, p))
    out = jax.ShapeDtypeStruct((s_len, SB_W), F32)
    res = pl.pallas_call(
        body, name="sb_fwd", grid=(HEADS // 2, nq),
        in_specs=[pl.BlockSpec((BQ, LANES), lambda p, i: (i, qc + p)),
                  pl.BlockSpec((s_len, LANES), lambda p, i: (0, kc + p)),
                  pl.BlockSpec((s_len, LANES), lambda p, i: (0, vc + p))] + exchange.in_specs,
        out_specs=[blk, blk] + exchange.out_specs, out_shape=[out, out] + exchange.out_shape,
        scratch_shapes=exchange.scratch,
        compiler_params=_cparams("arbitrary", "arbitrary"),
    )(proj, proj, proj, *exchange.arrays)
    return res[:2], res[2:]


def _sb_backward(proj, tails, do, s_len, exchange):
    nq = s_len // BQ
    qc, kc, vc = O_SBQ // LANES, O_SBK // LANES, O_SBV // LANES
    n_ex = len(exchange.arrays)

    def body(q_ref, k_ref, v_ref, tails_ref, do_ref, *rest):
        x_refs, (dq_ref, dk_ref, dv_ref) = rest[:n_ex], rest[n_ex:n_ex + 3]
        out_refs, sems = rest[n_ex + 3:2 * n_ex + 3], rest[2 * n_ex + 3:]
        p = pl.program_id(0)
        i = pl.program_id(1)
        jd = _diag_block(i)

        @pl.when((p == 0) & (i == 0))
        def _():
            exchange.start(x_refs, out_refs, sems)

        @pl.when(i == 0)
        def _():
            dk_ref[...] = jnp.zeros(dk_ref.shape, F32)
            dv_ref[...] = jnp.zeros(dv_ref.shape, F32)

        row, col, trow, tcol, lane, klane = _attn_consts()
        tri = (trow >= tcol).astype(MXU_DTYPE)
        tri_p = (trow <= tcol).astype(MXU_DTYPE)
        q = q_ref[...] * 0.125
        tails_blk = tails_ref[...]
        do_blk = do_ref[...]
        hm = [lane < HD, lane >= HD]
        km = [klane < HD, klane >= HD]
        qh = [jnp.where(m, q, 0.0).astype(MXU_DTYPE) for m in hm]
        doh = [jnp.where(m, do_blk, 0.0).astype(MXU_DTYPE) for m in hm]

        def step(js, st, masks):
            before, dq = st
            chains = _chains(js)
            kj = [k_ref[_key_slice(j), :].astype(MXU_DTYPE) for j in js]
            vj = [v_ref[_key_slice(j), :].astype(MXU_DTYPE) for j in js]
            z = {(h, t): _dot_nt(qh[h], kj[t]) for h, t in chains}
            da = {(h, t): _dot_nt(doh[h], vj[t]) for h, t in chains}
            suf, sig = {}, {}
            for h, t in chains:
                lom = _neg_softplus(z[h, t])
                if masks:
                    lom = jnp.where(masks[t], lom, 0.0)
                suf[h, t] = _running_sum(lom, tri)
                sig[h, t] = jnp.exp(z[h, t] + lom)
            run = list(before)
            dl, pre, before_in = {}, {}, {}
            dk_add, dv_add = [None] * len(js), [None] * len(js)
            for h, t in chains:
                tail = _rowsum(jnp.where(lane == h * HD + js[t], tails_blk, 0.0))
                a = jnp.exp(z[h, t] + suf[h, t] + tail)
                if masks:
                    a = jnp.where(masks[t], a, 0.0)
                dl[h, t] = da[h, t] * a
                pre[h, t] = _dot(dl[h, t].astype(MXU_DTYPE), tri_p)
                dv_h = _dot_tn(a.astype(MXU_DTYPE), doh[h])
                dv_add[t] = dv_h if dv_add[t] is None else dv_add[t] + dv_h
                before_in[h, t] = run[h]
                run[h] = run[h] + _rowsum(dl[h, t])
            for h, t in chains:
                upto = before_in[h, t] + pre[h, t]
                dz = dl[h, t] - sig[h, t] * upto
                if masks:
                    dz = jnp.where(masks[t], dz, 0.0)
                dzb = dz.astype(MXU_DTYPE)
                dq = dq + _dot(dzb, jnp.where(km[h], kj[t], 0))
                dk_h = _dot_tn(dzb, qh[h])
                dk_add[t] = dk_h if dk_add[t] is None else dk_add[t] + dk_h
            for t, j in enumerate(js):
                dk_ref[_key_slice(j), :] += dk_add[t]
                dv_ref[_key_slice(j), :] += dv_add[t]
            return tuple(run), dq

        st = ((jnp.zeros((BQ, 1), F32),) * 2, jnp.zeros((BQ, LANES), F32))
        diag_js, masks, left = _diag_step(jd, False, col < row, group=1)
        st = _walk_blocks(lambda js, s: step(js, s, None), st, left, False, group=4)
        st = step(diag_js, st, masks)
        dq_ref[...] = st[1] * 0.125

        @pl.when((p == HEADS // 2 - 1) & (i == nq - 1))
        def _():
            exchange.wait(x_refs, out_refs, sems)

    blk = pl.BlockSpec((BQ, LANES), lambda p, i: (i, p))
    full = pl.BlockSpec((s_len, LANES), lambda p, i: (0, p))
    out = jax.ShapeDtypeStruct((s_len, SB_W), F32)
    res = pl.pallas_call(
        body, name="sb_bwd", grid=(HEADS // 2, nq),
        in_specs=[pl.BlockSpec((BQ, LANES), lambda p, i: (i, qc + p)),
                  pl.BlockSpec((s_len, LANES), lambda p, i: (0, kc + p)),
                  pl.BlockSpec((s_len, LANES), lambda p, i: (0, vc + p)),
                  blk, blk] + exchange.in_specs,
        out_specs=[blk, full, full] + exchange.out_specs, out_shape=[out, out, out] + exchange.out_shape,
        scratch_shapes=exchange.scratch,
        compiler_params=_cparams("arbitrary", "arbitrary"),
    )(proj, proj, proj, tails, do, *exchange.arrays)
    return res[:3], res[3:]


def _pair_mask(rows, h):
    lane = lax.broadcasted_iota(jnp.int32, (rows, 2 * LANES), 1)
    rot = lane - LANES
    return (((lane < LANES) & (lane // HD == h))
            | ((lane >= LANES) & (rot < 2 * ROPE) & ((rot // (ROPE // 2)) % 2 == h)))


def _mla_forward(q_cat, k_cat, kv, s_len):
    nq = s_len // BQ
    scale = 1.0 / math.sqrt(QK_DIM)

    def body(q_ref, k_ref, v_ref, o_ref, lse_ref):
        i = pl.program_id(1)
        jd = _diag_block(i)
        row, col, trow, tcol, lane, klane = _attn_consts()
        q = q_ref[...]
        hm = [lane < HD, lane >= HD]
        km = [klane < HD, klane >= HD]
        qh = [jnp.where(_pair_mask(BQ, h), q, 0) for h in range(2)]

        def step(js, st, masks):
            m_run, l_run, acc = st
            chains = _chains(js)
            kj = [k_ref[_key_slice(j), :] for j in js]
            vj = [v_ref[_key_slice(j), :].astype(MXU_DTYPE) for j in js]
            s = {}
            for h, t in chains:
                s[h, t] = _dot_nt(qh[h], kj[t]) * scale
                if masks:
                    s[h, t] = jnp.where(masks[t], s[h, t], -jnp.inf)
            m_new, alpha, l_new = [], [], []
            for h in range(2):
                top = m_run[h]
                for t in range(len(js)):
                    top = jnp.maximum(top, jnp.max(s[h, t], axis=1, keepdims=True))
                m_new.append(top)
                alpha.append(jnp.exp(m_run[h] - top))
                l_new.append(alpha[h] * l_run[h])
            add = None
            for h, t in chains:
                pr = jnp.exp(s[h, t] - m_new[h])
                l_new[h] = l_new[h] + _rowsum(pr)
                part = _dot(pr.astype(MXU_DTYPE), jnp.where(km[h], vj[t], 0))
                add = part if add is None else add + part
            acc = jnp.where(hm[0], alpha[0], alpha[1]) * acc + add
            return tuple(m_new), tuple(l_new), acc

        st = ((jnp.full((BQ, 1), -1e30, F32),) * 2, (jnp.zeros((BQ, 1), F32),) * 2, jnp.zeros((BQ, LANES), F32))
        diag_js, masks, left = _diag_step(jd, True, col <= row)
        st = step(diag_js, st, masks)
        m_run, l_run, acc = _walk_blocks(lambda js, s: step(js, s, None), st, left, True, group=8)
        o_ref[...] = acc / jnp.where(hm[0], l_run[0], l_run[1])
        lse_ref[...] = jnp.where(hm[0], m_run[0] + jnp.log(l_run[0]), m_run[1] + jnp.log(l_run[1]))

    blk = pl.BlockSpec((BQ, LANES), lambda p, i: (i, p))
    out = jax.ShapeDtypeStruct((s_len, MLA_W), F32)
    return pl.pallas_call(
        body, name="mla_fwd", grid=(HEADS // 2, nq),
        in_specs=[pl.BlockSpec((BQ, 2 * LANES), lambda p, i: (i, p)),
                  pl.BlockSpec((s_len, 2 * LANES), lambda p, i: (0, p)),
                  pl.BlockSpec((s_len, LANES), lambda p, i: (0, MLA_W // LANES + p))],
        out_specs=[blk, blk], out_shape=[out, out],
        compiler_params=_cparams("parallel", "parallel"),
    )(q_cat, k_cat, kv)


def _mla_backward(q_cat, k_cat, kv, o, lse, do, s_len):
    nq = s_len // BQ
    scale = 1.0 / math.sqrt(QK_DIM)

    def body(q_ref, k_ref, v_ref, o_ref, lse_ref, do_ref, dq_ref, dk_ref, dv_ref):
        i = pl.program_id(1)

        @pl.when(i == 0)
        def _():
            dk_ref[...] = jnp.zeros(dk_ref.shape, F32)
            dv_ref[...] = jnp.zeros(dv_ref.shape, F32)

        jd = _diag_block(i)
        row, col, trow, tcol, lane, klane = _attn_consts()
        q = q_ref[...]
        o_blk = o_ref[...]
        do_blk = do_ref[...]
        lse_blk = lse_ref[...]
        hm = [lane < HD, lane >= HD]
        kpm = [_pair_mask(BK, h) for h in range(2)]
        qh = [jnp.where(_pair_mask(BQ, h), q, 0) for h in range(2)]
        doh_f = [jnp.where(m, do_blk, 0.0) for m in hm]
        doh = [d.astype(MXU_DTYPE) for d in doh_f]
        delta = [jnp.sum(d * o_blk, axis=1, keepdims=True) for d in doh_f]
        lse_h = [jnp.sum(jnp.where(lane == h * HD, lse_blk, 0.0), axis=1, keepdims=True) for h in range(2)]

        def step(js, st, masks):
            dq = st
            chains = _chains(js)
            kj = [k_ref[_key_slice(j), :] for j in js]
            vj = [v_ref[_key_slice(j), :].astype(MXU_DTYPE) for j in js]
            s = {(h, t): _dot_nt(qh[h], kj[t]) for h, t in chains}
            dp = {(h, t): _dot_nt(doh[h], vj[t]) for h, t in chains}
            adds = [[None] * len(js) for _ in range(2)]

            def accumulate(slot, t, part):
                adds[slot][t] = part if adds[slot][t] is None else adds[slot][t] + part

            for h, t in chains:
                pr = jnp.exp(s[h, t] * scale - lse_h[h])
                if masks:
                    pr = jnp.where(masks[t], pr, 0.0)
                dsb = (pr * (dp[h, t] - delta[h]) * scale).astype(MXU_DTYPE)
                dq = dq + _dot(dsb, jnp.where(kpm[h], kj[t], 0))
                accumulate(0, t, _dot_tn(dsb, qh[h]))
                accumulate(1, t, _dot_tn(pr.astype(MXU_DTYPE), doh[h]))
            for t, j in enumerate(js):
                dk_ref[_key_slice(j), :] += adds[0][t]
                dv_ref[_key_slice(j), :] += adds[1][t]
            return dq

        diag_js, masks, left = _diag_step(jd, True, col <= row)
        st = step(diag_js, jnp.zeros((BQ, 2 * LANES), F32), masks)
        dq_ref[...] = _walk_blocks(lambda js, s: step(js, s, None), st, left, True, group=8)

    blk = pl.BlockSpec((BQ, LANES), lambda p, i: (i, p))
    full = pl.BlockSpec((s_len, LANES), lambda p, i: (0, p))
    out = jax.ShapeDtypeStruct((s_len, MLA_W), F32)
    out_cat = jax.ShapeDtypeStruct((s_len, 2 * MLA_W), F32)
    return pl.pallas_call(
        body, name="mla_bwd", grid=(HEADS // 2, nq),
        in_specs=[pl.BlockSpec((BQ, 2 * LANES), lambda p, i: (i, p)),
                  pl.BlockSpec((s_len, 2 * LANES), lambda p, i: (0, p)),
                  pl.BlockSpec((s_len, LANES), lambda p, i: (0, MLA_W // LANES + p)),
                  blk, blk, blk],
        out_specs=[pl.BlockSpec((BQ, 2 * LANES), lambda p, i: (i, p)),
                   pl.BlockSpec((s_len, 2 * LANES), lambda p, i: (0, p)), full],
        out_shape=[out_cat, out_cat, out],
        compiler_params=_cparams("arbitrary", "arbitrary"),
    )(q_cat, k_cat, kv, o, lse, do)


def _mesh_pos():
    return lax.axis_index("x"), lax.axis_index("y"), lax.axis_index("c")


def _dev_index(px, py, pc):
    return 4 * px + 2 * py + pc


def _all_gather(block, name):
    return _all_gather_parts([block], name)[0]


def _all_gather_parts(blocks, name):
    n = len(blocks)

    def body(*refs):
        x_refs, out_refs = refs[:n], refs[n:2 * n]
        send_sems, recv_sems, local_sems = refs[2 * n:]
        x, y, c = _mesh_pos()
        me, sibling = (x, y, c), (x, y, 1 - c)
        chips = [(1 - x, y), (x, 1 - y), (1 - x, 1 - y)]

        def copy(a, k, blockpos, to, src=None):
            slot = out_refs[a].at[_dev_index(*blockpos)]
            return pltpu.make_async_remote_copy(
                src_ref=slot if src is None else src, dst_ref=slot,
                send_sem=send_sems.at[7 * a + k], recv_sem=recv_sems.at[7 * a + k],
                device_id=to, device_id_type=pl.DeviceIdType.MESH)

        mine = [pltpu.make_async_copy(x_refs[a], out_refs[a].at[_dev_index(*me)], local_sems.at[a]) for a in range(n)]
        for cp in mine:
            cp.start()
        first = []
        for a in range(n):
            first.append(copy(a, 0, me, sibling, src=x_refs[a]))
            first += [copy(a, 1 + j, me, (*chip, c), src=x_refs[a]) for j, chip in enumerate(chips)]
        for cp in first:
            cp.start()
        passed = []
        for j, chip in enumerate(chips):
            for a in range(n):
                copy(a, 1 + j, (*chip, c), me).wait_recv()
                passed.append(copy(a, 4 + j, (*chip, c), sibling))
                passed[-1].start()
        for a in range(n):
            copy(a, 0, sibling, me).wait_recv()
            for j, chip in enumerate(chips):
                copy(a, 4 + j, (*chip, 1 - c), me).wait_recv()
        for cp in first + passed:
            cp.wait_send()
        for cp in mine:
            cp.wait()

    return pl.pallas_call(
        body, name=name,
        out_shape=[jax.ShapeDtypeStruct((N_DEV,) + b.shape, b.dtype) for b in blocks],
        in_specs=[pl.BlockSpec(memory_space=pl.ANY)] * n, out_specs=[pl.BlockSpec(memory_space=pl.ANY)] * n,
        scratch_shapes=[pltpu.SemaphoreType.DMA((7 * n,)), pltpu.SemaphoreType.DMA((7 * n,)),
                        pltpu.SemaphoreType.DMA((n,))],
    )(*blocks)


class _Exchange:
    def __init__(self, arrays):
        self.arrays = list(arrays)
        n = len(self.arrays)
        self.in_specs = [pl.BlockSpec(memory_space=pl.ANY)] * n
        self.out_specs = [pl.BlockSpec(memory_space=pl.ANY)] * n
        self.out_shape = [jax.ShapeDtypeStruct(a.shape, a.dtype) for a in self.arrays]
        self.scratch = [pltpu.SemaphoreType.DMA((7 * n,)), pltpu.SemaphoreType.DMA((7 * n,)),
                        pltpu.SemaphoreType.DMA((n,))]

    def _copies(self, x_refs, out_refs, sems, with_arrivals):
        send_sems, recv_sems, local_sems = sems
        x, y, c = _mesh_pos()
        me = _dev_index(x, y, c)
        flips = [(fx, fy, fc) for fx in (0, 1) for fy in (0, 1) for fc in (0, 1)][1:]
        peers = [(1 - x if fx else x, 1 - y if fy else y, 1 - c if fc else c) for fx, fy, fc in flips]
        mine, sends, arrivals = [], [], []
        for a in range(len(self.arrays)):
            mine.append(pltpu.make_async_copy(x_refs[a].at[me], out_refs[a].at[me], local_sems.at[a]))
            for k, peer in enumerate(peers):
                sends.append(pltpu.make_async_remote_copy(
                    src_ref=x_refs[a].at[_dev_index(*peer)], dst_ref=out_refs[a].at[me],
                    send_sem=send_sems.at[7 * a + k], recv_sem=recv_sems.at[7 * a + k],
                    device_id=peer, device_id_type=pl.DeviceIdType.MESH))
                if not with_arrivals:
                    continue
                arrivals.append(pltpu.make_async_remote_copy(
                    src_ref=x_refs[a].at[me], dst_ref=out_refs[a].at[_dev_index(*peer)],
                    send_sem=send_sems.at[7 * a + k], recv_sem=recv_sems.at[7 * a + k],
                    device_id=peer, device_id_type=pl.DeviceIdType.MESH))
        return mine, sends, arrivals

    def start(self, x_refs, out_refs, sems):
        mine, sends, _ = self._copies(x_refs, out_refs, sems, False)
        for cp in mine + sends:
            cp.start()

    def wait(self, x_refs, out_refs, sems):
        mine, sends, arrivals = self._copies(x_refs, out_refs, sems, True)
        for cp in arrivals:
            cp.wait_recv()
        for cp in sends:
            cp.wait_send()
        for cp in mine:
            cp.wait()


def _sum_blocks(parts, name):
    n, r, c = parts.shape
    row_tiles = [t for t in range(16, min(r, 2048) + 1, 16) if r % t == 0]
    if row_tiles:
        tr, tc = max(row_tiles), c
    else:
        tr, tc = r, 2 * LANES
    assert c % tc == 0

    def body(p_ref, o_ref):
        acc = p_ref[0].astype(F32)
        for s in range(1, n):
            acc = acc + p_ref[s].astype(F32)
        o_ref[...] = acc

    return pl.pallas_call(
        body, name=name, grid=(r // tr, c // tc),
        in_specs=[pl.BlockSpec((n, tr, tc), lambda i, j: (0, i, j))],
        out_specs=pl.BlockSpec((tr, tc), lambda i, j: (i, j)),
        out_shape=jax.ShapeDtypeStruct((r, c), F32),
        compiler_params=_cparams("parallel", "parallel"),
    )(parts)


def _sigmoid(x):
    return 1.0 / (1.0 + jnp.exp(-x))


def _silu(x):
    return x * _sigmoid(x)


def _silu_grad(x):
    s = _sigmoid(x)
    return s * (1.0 + x * (1.0 - s))


def _colsum(x):
    return jnp.sum(x, axis=0, keepdims=True)


def _rms(x):
    return lax.rsqrt(jnp.mean(x * x, axis=-1, keepdims=True) + EPS)


def _rms_bwd(xn, r, dxn):
    return r * (dxn - xn * jnp.mean(dxn * xn, axis=-1, keepdims=True))


def _adamw(w, g, m, v):
    m = ADAM_B1 * m + (1.0 - ADAM_B1) * g
    v = ADAM_B2 * v + (1.0 - ADAM_B2) * jnp.square(g)
    m_hat = m / (1.0 - ADAM_B1 ** ADAM_STEP)
    v_hat = v / (1.0 - ADAM_B2 ** ADAM_STEP)
    delta = -ADAM_LR * (m_hat / (jnp.sqrt(v_hat) + ADAM_EPS) + ADAM_WD * w)
    return delta, m, v


def _adamw_call(w, g, m, v, name):
    r, c = w.shape
    if r % 256 == 0:
        tr, tc = 256, c
    elif r * c <= 256 * 1024 or c % (2 * LANES):
        tr, tc = r, c
    else:
        tr, tc = r, 2 * LANES

    def body(w_ref, g_ref, m_ref, v_ref, d_out, m_out, v_out):
        d_out[...], m_out[...], v_out[...] = _adamw(w_ref[...], g_ref[...], m_ref[...], v_ref[...])

    spec = pl.BlockSpec((tr, tc), lambda i, j: (i, j))
    return pl.pallas_call(
        body, name=name, grid=(r // tr, c // tc), in_specs=[spec] * 4, out_specs=[spec] * 3,
        out_shape=[jax.ShapeDtypeStruct((r, c), F32)] * 3, compiler_params=_cparams("parallel", "parallel"),
    )(w, g, m, v)


def _uq_to_kernel_layout(w):
    lead = w.shape[:-1]
    t = w.reshape(lead + (HEADS, QK_DIM))
    return jnp.concatenate([t[..., :NOPE].reshape(lead + (HEADS * NOPE,)),
                            t[..., NOPE:NOPE + ROPE // 2].reshape(lead + (LANES,)),
                            t[..., NOPE + ROPE // 2:].reshape(lead + (LANES,))], axis=-1)


def _uq_from_kernel_layout(w):
    lead = w.shape[:-1]
    nope = w[..., :HEADS * NOPE].reshape(lead + (HEADS, NOPE))
    r1 = w[..., HEADS * NOPE:HEADS * NOPE + LANES].reshape(lead + (HEADS, ROPE // 2))
    r2 = w[..., HEADS * NOPE + LANES:].reshape(lead + (HEADS, ROPE // 2))
    return jnp.concatenate([nope, r1, r2], axis=-1).reshape(lead + (HEADS * QK_DIM,))


def _ukv_to_kernel_layout(w):
    lead = w.shape[:-1]
    t = w.reshape(lead + (HEADS, NOPE + HD))
    return jnp.concatenate([t[..., :NOPE].reshape(lead + (HEADS * NOPE,)),
                            t[..., NOPE:].reshape(lead + (HEADS * HD,))], axis=-1)


def _ukv_from_kernel_layout(w):
    lead = w.shape[:-1]
    kn = w[..., :HEADS * NOPE].reshape(lead + (HEADS, NOPE))
    vv = w[..., HEADS * NOPE:].reshape(lead + (HEADS, HD))
    return jnp.concatenate([kn, vv], axis=-1).reshape(lead + (HEADS * (NOPE + HD),))


def _w_in_t_to_kernel_layout(wt):
    sb = wt[0:2048]
    c_q = wt[2048:2432]
    c_kv = wt[2432:2688]
    k_rot = wt[2688:2720]
    mla_z = wt[2720:3232]
    gates = wt[3232:5280]
    zeros = jnp.zeros((LANES, wt.shape[1]), wt.dtype)
    k1 = jnp.tile(k_rot[:ROPE // 2], (HEADS, 1))
    k2 = jnp.tile(k_rot[ROPE // 2:], (HEADS, 1))
    return jnp.concatenate([gates, sb, mla_z, c_q, zeros, c_kv, k1, k2], axis=0)


def _w_in_t_from_kernel_layout(gt, g_rot):
    return jnp.concatenate([gt[O_SBQ:O_SBQ + 2048], gt[O_CQ:O_CQ + Q_RANK], gt[O_CKV:O_CKV + KV_RANK],
                            g_rot.astype(gt.dtype), gt[O_MLAZ:O_MLAZ + MLA_W], gt[O_GA:O_GA + 2 * D]], axis=0)


def kernel(x, c, positions, w_ada, b_ada, norm_gain, w_in, q_norm_gain, w_uq, kv_norm_gain, w_ukv, w_branch_a, w_branch_b, w_out, final_norm_gain, loss_target, m_w_ada, m_b_ada, m_norm_gain, m_w_in, m_q_norm_gain, m_w_uq, m_kv_norm_gain, m_w_ukv, m_w_branch_a, m_w_branch_b, m_w_out, m_final_norm_gain, v_w_ada, v_b_ada, v_norm_gain, v_w_in, v_q_norm_gain, v_w_uq, v_kv_norm_gain, v_w_ukv, v_w_branch_a, v_w_branch_b, v_w_out, v_final_norm_gain):
    s_len = x.shape[1]
    me = _dev_index(*_mesh_pos())
    x2d = x[0]
    tgt = loss_target[0]

    w_in_t = w_in[0].T.astype(BF16)
    big = [w_uq[0], w_ukv[0], w_branch_a[0], w_branch_b[0], w_out[0]]
    big_sizes = [int(w.size) for w in big]
    packed = jnp.concatenate([w.astype(BF16).reshape(-1, LANES) for w in big], axis=0)
    g_in_t, c_all = _all_gather_parts([w_in_t, c.reshape(8, LANES)], "gather_w_in")
    c_all = c_all.reshape(N_DEV, D)
    w_in_kt = _w_in_t_to_kernel_layout(g_in_t.reshape(N_DEV * w_in_t.shape[0], D))

    mod_cols = _mm(c_all, w_ada[0], name="ada_mod")
    mod_all = _all_gather(mod_cols, "gather_mod")
    mod = lax.dynamic_index_in_dim(mod_all, me, axis=1, keepdims=False).reshape(1, 3 * D)
    mod_shift, mod_scale, mod_gate = mod[:, :D], mod[:, D:2 * D], mod[:, 2 * D:]
    b_shift, b_scale, b_gate = b_ada[:, :D], b_ada[:, D:2 * D], b_ada[:, 2 * D:]
    g1 = norm_gain
    gq, gkv = q_norm_gain, kv_norm_gain
    gf = final_norm_gain.reshape(1, D)

    def f_h(x_, g1_, ms, bs, msc, bsc):
        xn = x_ * _rms(x_)
        return (xn * g1_ * (1.0 + (msc + bsc)) + (ms + bs),), ()

    (h,) = _rowwise(f_h, [x2d], [g1, mod_shift, b_shift, mod_scale, b_scale], [(D, BF16)], name="ada_norm")
    proj = _mm(h, w_in_kt, tb=True, name="proj_in", tiles=(min(s_len, 1024), IN_PAD // 2, D))

    (o_a, sb_tails), (gathered,) = _sb_forward(
        proj, s_len, _Exchange([jnp.broadcast_to(packed[None], (N_DEV,) + packed.shape)]))
    offs = [0]
    for n in big_sizes:
        offs.append(offs[-1] + n // LANES)

    def unpack(t, shape):
        return gathered[:, offs[t]:offs[t + 1], :].reshape((N_DEV,) + shape)

    def cols(t, shape):
        return unpack(t, shape).transpose(1, 0, 2).reshape(shape[0], N_DEV * shape[1])

    w_uq_k = _uq_to_kernel_layout(cols(0, big[0].shape))
    w_ukv_k = _ukv_to_kernel_layout(cols(1, big[1].shape))
    w_a_f = cols(2, big[2].shape)
    w_b_f = cols(3, big[3].shape)
    w_out_f = unpack(4, big[4].shape).reshape(D, D)

    def f_lat(cq, ckv, gq_, gkv_):
        return (cq * _rms(cq) * gq_, ckv * _rms(ckv) * gkv_), ()

    cq_n, ckv_n = _rowwise(f_lat, [(proj, O_CQ // Q_RANK, Q_RANK), (proj, O_CKV // KV_RANK, KV_RANK)], [gq, gkv],
                           [(Q_RANK, BF16), (KV_RANK, BF16)], name="latent_norm")
    q_mla = _mm(cq_n, w_uq_k, name="q_up")
    kv = _mm(ckv_n, w_ukv_k, name="kv_up")

    inv_freq = ROPE_BASE ** (-jnp.arange(0, ROPE, 2, dtype=F32) / ROPE)
    inv_freq_t = jnp.tile(inv_freq, HEADS).reshape(1, LANES)
    pos_col = positions.reshape(s_len, 1).astype(F32)

    pairs = HEADS // 2

    def f_rope(pos, qn, q1, q2, kn, k1, k2, freq):
        ang = pos * freq
        cs, sn = jnp.cos(ang), jnp.sin(ang)
        q1r, q2r = q1 * cs - q2 * sn, q1 * sn + q2 * cs
        k1r, k2r = k1 * cs - k2 * sn, k1 * sn + k2 * cs
        lane = lax.broadcasted_iota(jnp.int32, q1.shape, 1)
        first, second = lane < ROPE, (lane >= ROPE) & (lane < 2 * ROPE)
        k_rot = jnp.where(first, k1r, jnp.where(second, k2r, 0.0))
        q_parts, k_parts = [], []
        for p in range(pairs):
            q_rot = jnp.where(first, pltpu.roll(q1r, (LANES - ROPE * p) % LANES, 1),
                              jnp.where(second, pltpu.roll(q2r, (LANES + ROPE - ROPE * p) % LANES, 1), 0.0))
            q_parts += [qn[:, LANES * p:LANES * (p + 1)], q_rot]
            k_parts += [kn[:, LANES * p:LANES * (p + 1)], k_rot]
        return (jnp.concatenate(q_parts, axis=1), jnp.concatenate(k_parts, axis=1), cs, sn), ()

    q_cat, k_cat, cos_t, sin_t = _rowwise(
        f_rope, [pos_col, (q_mla, 0, MLA_W), (q_mla, 4, LANES), (q_mla, 5, LANES), (kv, 0, MLA_W),
                 (proj, O_KROT // LANES, LANES), (proj, O_KROT // LANES + 1, LANES)], [inv_freq_t],
        [(2 * MLA_W, BF16), (2 * MLA_W, BF16), (LANES, F32), (LANES, F32)], name="rope")

    o_b, lse = _mla_forward(q_cat, k_cat, kv, s_len)

    def f_gate(oa, za, ob, zb):
        return (oa * _silu(za), ob * _silu(zb)), ()

    ya_in, yb_in = _rowwise(f_gate, [o_a, (proj, O_SBZ // SB_W, SB_W), o_b, (proj, O_MLAZ // MLA_W, MLA_W)], [],
                            [(SB_W, BF16), (MLA_W, BF16)], name="branch_gate")
    y_a = _mm(ya_in, w_a_f, name="branch_a")

    def f_merge(yb, ga, gb, ya):
        return (yb, _sigmoid(ga) * ya + _sigmoid(gb) * yb), ()

    y_b, merged = _mm(yb_in, w_b_f, name="branch_b_merge", tiles=(min(s_len, 512), D, MLA_W),
                      epilogue=(f_merge, [(proj, O_GA // D, D), (proj, O_GB // D, D), y_a], [], [(D, F32), (D, BF16)], []))

    def f_loss(out_, x_, t_, mg, bg, gf_):
        gate = mg + bg
        x2 = x_ + gate * out_
        r2 = _rms(x2)
        xn2 = x2 * r2
        err = xn2 * gf_ - t_
        loss = jnp.full((1, LANES), 0.5 / D, F32) * jnp.sum(err * err)
        dy = err * (1.0 / D)
        dx2 = _rms_bwd(xn2, r2, dy * gf_)
        return (dx2, dx2 * gate), (loss, _colsum(dy * xn2), _colsum(dx2 * out_))

    dx2, d_out, loss_part, d_gf, d_gate = _mm(
        merged, w_out_f, name="out_proj_loss", tiles=(min(s_len, 512), D, D),
        epilogue=(f_loss, [x2d, tgt], [mod_gate, b_gate, gf], [(D, F32), (D, BF16)], [LANES, D, D]))

    dw_out = _mm(merged, d_out, ta=True, name="dw_out")

    def f_dmerge(dm, ga, gb, ya, yb):
        sa, sb = _sigmoid(ga), _sigmoid(gb)
        return (dm * sa, dm * sb, dm * ya * sa * (1.0 - sa), dm * yb * sb * (1.0 - sb)), ()

    d_ya, d_yb, d_ga, d_gb = _mm(
        d_out, w_out_f, tb=True, name="d_merge", tiles=(min(s_len, 256), D, D),
        epilogue=(f_dmerge, [(proj, O_GA // D, D), (proj, O_GB // D, D), y_a, y_b], [], [(D, BF16)] * 4, []))
    dw_a = _mm(ya_in, d_ya, ta=True, name="dw_branch_a")
    dw_b = _mm(yb_in, d_yb, ta=True, name="dw_branch_b")

    def f_dgate(d_in, o_, z_):
        return (d_in * _silu(z_), d_in * o_ * _silu_grad(z_)), ()

    d_oa, d_sbz = _mm(d_ya, w_a_f, tb=True, name="d_branch_a",
                      epilogue=(f_dgate, [o_a, (proj, O_SBZ // SB_W, SB_W)], [], [(SB_W, F32), (SB_W, BF16)], []))
    d_ob, d_mlaz = _mm(d_yb, w_b_f, tb=True, name="d_branch_b",
                       epilogue=(f_dgate, [o_b, (proj, O_MLAZ // MLA_W, MLA_W)], [], [(MLA_W, F32), (MLA_W, BF16)], []))

    dq_cat, dk_cat, dv_b = _mla_backward(q_cat, k_cat, kv, o_b, lse, d_ob, s_len)

    def f_drope(dq, dk, dv_, cs, sn):
        lane = lax.broadcasted_iota(jnp.int32, cs.shape, 1)
        first, second = lane < ROPE, (lane >= ROPE) & (lane < 2 * ROPE)
        dq1 = dq2 = dk1 = dk2 = None
        for p in range(pairs):
            q_rot = dq[:, LANES * (2 * p + 1):LANES * (2 * p + 2)]
            k_rot = dk[:, LANES * (2 * p + 1):LANES * (2 * p + 2)]
            parts = (pltpu.roll(jnp.where(first, q_rot, 0.0), (ROPE * p) % LANES, 1),
                     pltpu.roll(jnp.where(second, q_rot, 0.0), (LANES - ROPE + ROPE * p) % LANES, 1),
                     jnp.where(first, k_rot, 0.0), jnp.where(second, k_rot, 0.0))
            if p == 0:
                dq1, dq2, dk1, dk2 = parts
            else:
                dq1, dq2, dk1, dk2 = dq1 + parts[0], dq2 + parts[1], dk1 + parts[2], dk2 + parts[3]
        dqn_ = [dq[:, 2 * LANES * p:2 * LANES * p + LANES] for p in range(pairs)]
        dkn_ = [dk[:, 2 * LANES * p:2 * LANES * p + LANES] for p in range(pairs)]
        return (jnp.concatenate(dqn_ + [dq1 * cs + dq2 * sn, dq2 * cs - dq1 * sn], axis=1),
                jnp.concatenate(dkn_ + [dv_], axis=1),
                jnp.concatenate([dk1 * cs + dk2 * sn, dk2 * cs - dk1 * sn], axis=1)), ()

    dq_k, dkv_k, d_krot = _rowwise(f_drope, [dq_cat, dk_cat, dv_b, cos_t, sin_t], [],
                                   [(HEADS * QK_DIM, BF16), (2 * MLA_W, BF16), (2 * LANES, BF16)], name="d_rope")
    dw_uq_k = _mm(cq_n, dq_k, ta=True, name="dw_uq")
    dw_ukv_k = _mm(ckv_n, dkv_k, ta=True, name="dw_ukv")

    def f_dlat(d_normed, latent, gain):
        r = _rms(latent)
        normed = latent * r
        return (_rms_bwd(normed, r, d_normed * gain),), (_colsum(d_normed * normed),)

    d_cq, d_gq = _mm(dq_k, w_uq_k, tb=True, name="d_cq_norm",
                     epilogue=(f_dlat, [(proj, O_CQ // Q_RANK, Q_RANK)], [gq], [(Q_RANK, BF16)], [Q_RANK]))
    d_ckv, d_gkv = _mm(dkv_k, w_ukv_k, tb=True, name="d_ckv_norm",
                       epilogue=(f_dlat, [(proj, O_CKV // KV_RANK, KV_RANK)], [gkv], [(KV_RANK, BF16)], [KV_RANK]))

    def col_blocks(g):
        kdim, n8 = g.shape
        return g.astype(BF16).reshape(kdim, N_DEV, n8 // N_DEV).transpose(1, 0, 2).reshape(N_DEV, -1, LANES)

    g_blocks = jnp.concatenate([col_blocks(_uq_from_kernel_layout(dw_uq_k)), col_blocks(_ukv_from_kernel_layout(dw_ukv_k)),
                                col_blocks(dw_a), col_blocks(dw_b), dw_out.astype(BF16).reshape(N_DEV, -1, LANES)], axis=1)
    (d_sbq, d_sbk, d_sbv), (g_recv,) = _sb_backward(proj, sb_tails, d_oa, s_len, _Exchange([g_blocks]))

    d_proj = jnp.concatenate([d_ga, d_gb, d_sbq.astype(BF16), d_sbk.astype(BF16), d_sbv.astype(BF16), d_sbz, d_mlaz,
                              d_cq, jnp.zeros((s_len, LANES), BF16), d_ckv, d_krot], axis=1)
    dw_in_kt = _mm(d_proj, h, ta=True, out_dtype=BF16, name="dw_in", tiles=(512, D, s_len))

    def krot_body(t_ref, o_ref):
        half = ROPE // 2
        for part in range(2):
            acc = t_ref[part * LANES:part * LANES + half, :].astype(F32)
            for hh in range(1, HEADS):
                acc = acc + t_ref[part * LANES + hh * half:part * LANES + (hh + 1) * half, :].astype(F32)
            o_ref[part * half:(part + 1) * half, :] = acc

    dw_krot = pl.pallas_call(krot_body, name="dw_krot_sum", out_shape=jax.ShapeDtypeStruct((ROPE, D), F32))(
        dw_in_kt[O_KROT:O_KROT + 2 * LANES])

    g_in_blocks = _w_in_t_from_kernel_layout(dw_in_kt, dw_krot).reshape(N_DEV, -1, D)
    def f_dx(dh_, x_, dx2_, g1_, msc, bsc):
        r = _rms(x_)
        xn = x_ * r
        dn1 = dh_ * (1.0 + (msc + bsc))
        return ((dx2_ + _rms_bwd(xn, r, dn1 * g1_),),
                (_colsum(dh_), _colsum(dh_ * (xn * g1_)), _colsum(dn1 * xn)))

    (grad_x2d, d_shift, d_scale, d_g1), (g_in_recv,) = _mm(
        d_proj, w_in_kt, name="d_h", tiles=(min(s_len, 512), D, 512), exchange=_Exchange([g_in_blocks]),
        epilogue=(f_dx, [x2d, dx2], [g1, mod_scale, b_scale], [(D, F32)], [D, D, D]))

    g_in_sum_t = _sum_blocks(g_in_recv, "sum_grads_w_in")
    g_sum = _sum_blocks(g_recv, "sum_grads")
    g_big = [g_sum[offs[t]:offs[t + 1]].reshape(big[t].shape) for t in range(5)]

    small = jnp.concatenate([d_shift, d_scale, d_gate, d_g1, d_gq, d_gkv, d_gf, loss_part], axis=1)
    n_small = small.shape[1]
    pad = (-n_small) % (8 * LANES)
    small = jnp.pad(small, ((0, 0), (0, pad))).reshape(-1, LANES)
    small_all = _all_gather(small, "gather_small")
    small_sum = _sum_blocks(small_all, "sum_small").reshape(1, -1)
    g_b_ada = small_sum[:, :3 * D]
    g_g1 = small_sum[:, 3 * D:4 * D]
    g_gq = small_sum[:, 4 * D:4 * D + Q_RANK]
    g_gkv = small_sum[:, 4 * D + Q_RANK:4 * D + Q_RANK + KV_RANK]
    g_gf = small_sum[:, 4 * D + Q_RANK + KV_RANK:4 * D + Q_RANK + KV_RANK + D]

    dmod_all = small_all.reshape(N_DEV, -1)[:, :3 * D]
    dmod_cols = lax.dynamic_slice_in_dim(dmod_all, me * (3 * D // N_DEV), 3 * D // N_DEV, axis=1)
    g_w_ada = _mm(c_all, dmod_cols, ta=True, name="dw_ada")

    loss = small_sum[0, n_small - LANES]

    names = ["w_ada", "b_ada", "norm_gain", "w_in", "q_norm_gain", "w_uq", "kv_norm_gain", "w_ukv",
             "w_branch_a", "w_branch_b", "w_out", "final_norm_gain"]
    weights = dict(w_ada=w_ada, b_ada=b_ada, norm_gain=norm_gain, w_in=w_in, q_norm_gain=q_norm_gain, w_uq=w_uq,
                   kv_norm_gain=kv_norm_gain, w_ukv=w_ukv, w_branch_a=w_branch_a, w_branch_b=w_branch_b, w_out=w_out,
                   final_norm_gain=final_norm_gain)
    moms = dict(w_ada=m_w_ada, b_ada=m_b_ada, norm_gain=m_norm_gain, w_in=m_w_in, q_norm_gain=m_q_norm_gain,
                w_uq=m_w_uq, kv_norm_gain=m_kv_norm_gain, w_ukv=m_w_ukv, w_branch_a=m_w_branch_a,
                w_branch_b=m_w_branch_b, w_out=m_w_out, final_norm_gain=m_final_norm_gain)
    vels = dict(w_ada=v_w_ada, b_ada=v_b_ada, norm_gain=v_norm_gain, w_in=v_w_in, q_norm_gain=v_q_norm_gain,
                w_uq=v_w_uq, kv_norm_gain=v_kv_norm_gain, w_ukv=v_w_ukv, w_branch_a=v_w_branch_a,
                w_branch_b=v_w_branch_b, w_out=v_w_out, final_norm_gain=v_final_norm_gain)
    grads2d = dict(w_ada=g_w_ada, b_ada=g_b_ada, norm_gain=g_g1, w_in=g_in_sum_t, q_norm_gain=g_gq, w_uq=g_big[0],
                   kv_norm_gain=g_gkv, w_ukv=g_big[1], w_branch_a=g_big[2], w_branch_b=g_big[3], w_out=g_big[4],
                   final_norm_gain=g_gf)

    grads, deltas, new_m, new_v = [], [], [], []
    for n in names:
        w = weights[n]
        if n == "w_in":
            to2d = lambda t: t[0].T
            back = lambda t: t.T[None]
        else:
            shape2d = grads2d[n].shape
            to2d = lambda t, s=shape2d: t.reshape(s)
            back = lambda t, s=w.shape: t.reshape(s)
        d_, m_, v_ = _adamw_call(to2d(w), grads2d[n], to2d(moms[n]), to2d(vels[n]), "adamw_" + n)
        grads.append(back(grads2d[n]))
        deltas.append(back(d_))
        new_m.append(back(m_))
        new_v.append(back(v_))

    return (loss, grad_x2d.reshape(x.shape), *grads, *deltas, *new_m, *new_v)
```

```python
import functools
import math

import jax
import jax.numpy as jnp
from jax import lax
from jax.experimental import pallas as pl
from jax.experimental.pallas import tpu as pltpu

F32 = jnp.float32
BF16 = jnp.bfloat16
MXU_DTYPE = jnp.bfloat16

N_DEV = 8
D = 1024
HEADS = 8
HD = 64
SB_W = 512
MLA_W = 512
Q_RANK = 384
KV_RANK = 256
ROPE = 32
NOPE = 64
QK_DIM = NOPE + ROPE
EPS = 1e-6
ROPE_BASE = 10000.0

ADAM_LR = 0.001
ADAM_B1 = 0.9
ADAM_B2 = 0.999
ADAM_EPS = 1e-08
ADAM_WD = 0.01
ADAM_STEP = 10

LANES = 128
VMEM_LIMIT = 48 * 1024 * 1024

O_GA, O_GB = 0, 1024
O_SBQ, O_SBK, O_SBV, O_SBZ = 2048, 2560, 3072, 3584
O_MLAZ = 4096
O_CQ = 4608
O_CKV = 5120
O_KROT = 5376
IN_PAD = 5632

BQ = 256
BK = 256


def _cparams(*sem):
    return pltpu.CompilerParams(dimension_semantics=sem, vmem_limit_bytes=VMEM_LIMIT)


def _tile_of(n, cap=512):
    if n <= cap:
        return n
    for t in (1024, 768, 512, 384, 256, 128):
        if t <= cap and n % t == 0:
            return t
    raise ValueError(n)


def _rowwise(fn, rows, vecs, outs, reds=(), *, name, tile=512):
    norm = []
    for r in rows:
        if isinstance(r, tuple):
            arr, cb, w = r[:3]
            ro = r[3] if len(r) > 3 else 0
        else:
            arr, cb, w, ro = r, 0, r.shape[1], 0
        norm.append((arr, cb, w, ro))
    s_len = norm[0][0].shape[0]
    tile = min(tile, s_len)
    assert s_len % tile == 0
    n_row, n_vec, n_out, n_red = len(norm), len(vecs), len(outs), len(reds)

    def body(*refs):
        step = pl.program_id(0)
        row_refs = refs[:n_row]
        vec_refs = refs[n_row:n_row + n_vec]
        out_refs = refs[n_row + n_vec:n_row + n_vec + n_out]
        red_refs = refs[n_row + n_vec + n_out:]
        row_res, red_res = fn(*[r[...] for r in row_refs], *[v[...] for v in vec_refs])
        for o, val in zip(out_refs, row_res):
            o[...] = val.astype(o.dtype)
        if n_red:
            @pl.when(step == 0)
            def _():
                for r in red_refs:
                    r[...] = jnp.zeros(r.shape, r.dtype)
            for r, val in zip(red_refs, red_res):
                r[...] += val

    in_specs = []
    for arr, cb, w, ro in norm:
        in_specs.append(pl.BlockSpec((tile, w), functools.partial(lambda i, cb, rb: (i + rb, cb), cb=cb, rb=ro // tile)))
        assert ro % tile == 0
    for v in vecs:
        in_specs.append(pl.BlockSpec(v.shape, lambda i: (0, 0)))
    out_shape = [jax.ShapeDtypeStruct((s_len, w), dt) for w, dt in outs]
    out_specs = [pl.BlockSpec((tile, w), lambda i: (i, 0)) for w, _ in outs]
    out_shape += [jax.ShapeDtypeStruct((1, w), F32) for w in reds]
    out_specs += [pl.BlockSpec((1, w), lambda i: (0, 0)) for w in reds]
    res = pl.pallas_call(
        body, name=name, grid=(s_len // tile,), in_specs=in_specs, out_specs=out_specs, out_shape=out_shape,
        compiler_params=_cparams("arbitrary" if n_red else "parallel"),
    )(*[a for a, _, _, _ in norm], *vecs)
    return res


def _mm(a, b, *, ta=False, tb=False, out_dtype=F32, name, exchange=None, tiles=None, epilogue=None):
    m, k = (a.shape[1], a.shape[0]) if ta else a.shape
    n = b.shape[0] if tb else b.shape[1]
    assert (b.shape[1] if tb else b.shape[0]) == k
    tm, tn, tk = tiles or (_tile_of(m, 1024), _tile_of(n, 1024 if n <= 1024 else 512), _tile_of(k, 1024))
    assert m % tm == 0 and n % tn == 0 and k % tk == 0
    ni, nj, nk = m // tm, n // tn, k // tk
    dims = (((0 if ta else 1,), (1 if tb else 0,)), ((), ()))
    n_ex = len(exchange.arrays) if exchange else 0
    fn, rows, vecs, outs, reds = epilogue or (None, (), (), (), ())
    rows = [r if isinstance(r, tuple) else (r, 0, r.shape[1]) for r in rows]
    assert not epilogue or tn == n
    n_res = len(outs) + len(reds) if epilogue else 1

    def body(*refs):
        a_ref, b_ref = refs[:2]
        row_refs, refs = refs[2:2 + len(rows)], refs[2 + len(rows):]
        vec_refs, refs = refs[:len(vecs)], refs[len(vecs):]
        x_refs, refs = refs[:n_ex], refs[n_ex:]
        res_refs, refs = refs[:n_res], refs[n_res:]
        out_refs, acc_ref, sems = refs[:n_ex], refs[n_ex], refs[n_ex + 1:]
        i, j, kk = pl.program_id(0), pl.program_id(1), pl.program_id(2)
        first = (i == 0) & (j == 0) & (kk == 0)

        if exchange:
            @pl.when(first)
            def _():
                exchange.start(x_refs, out_refs, sems)

        if reds:
            @pl.when(first)
            def _():
                for r in res_refs[len(outs):]:
                    r[...] = jnp.zeros(r.shape, r.dtype)

        @pl.when(kk == 0)
        def _():
            acc_ref[...] = jnp.zeros(acc_ref.shape, F32)

        acc_ref[...] += lax.dot_general(a_ref[...].astype(MXU_DTYPE), b_ref[...].astype(MXU_DTYPE), dims,
                                        preferred_element_type=F32)

        @pl.when(kk == nk - 1)
        def _():
            if not epilogue:
                res_refs[0][...] = acc_ref[...].astype(res_refs[0].dtype)
                return
            row_res, red_res = fn(acc_ref[...], *[r[...] for r in row_refs], *[v[...] for v in vec_refs])
            for o, val in zip(res_refs, row_res):
                o[...] = val.astype(o.dtype)
            for r, val in zip(res_refs[len(outs):], red_res):
                r[...] += val

        if exchange:
            @pl.when((i == ni - 1) & (j == nj - 1) & (kk == nk - 1))
            def _():
                exchange.wait(x_refs, out_refs, sems)

    a_spec = pl.BlockSpec((tk, tm), lambda i, j, kk: (kk, i)) if ta else pl.BlockSpec((tm, tk), lambda i, j, kk: (i, kk))
    b_spec = pl.BlockSpec((tn, tk), lambda i, j, kk: (j, kk)) if tb else pl.BlockSpec((tk, tn), lambda i, j, kk: (kk, j))
    in_specs = [a_spec, b_spec]
    in_specs += [pl.BlockSpec((tm, w), functools.partial(lambda i, j, kk, cb: (i, cb), cb=cb)) for _, cb, w in rows]
    in_specs += [pl.BlockSpec(v.shape, lambda i, j, kk: (0, 0)) for v in vecs]
    if epilogue:
        res_specs = [pl.BlockSpec((tm, w), lambda i, j, kk: (i, 0)) for w, _ in outs]
        res_specs += [pl.BlockSpec((1, w), lambda i, j, kk: (0, 0)) for w in reds]
        res_shape = [jax.ShapeDtypeStruct((m, w), dt) for w, dt in outs] + [jax.ShapeDtypeStruct((1, w), F32) for w in reds]
    else:
        res_specs = [pl.BlockSpec((tm, tn), lambda i, j, kk: (i, j))]
        res_shape = [jax.ShapeDtypeStruct((m, n), out_dtype)]
    ordered = bool(exchange or reds)
    res = pl.pallas_call(
        body, name=name, grid=(ni, nj, nk),
        in_specs=in_specs + (exchange.in_specs if exchange else []),
        out_specs=res_specs + (exchange.out_specs if exchange else []),
        out_shape=res_shape + (exchange.out_shape if exchange else []),
        scratch_shapes=[pltpu.VMEM((tm, tn), F32)] + (exchange.scratch if exchange else []),
        compiler_params=_cparams(*(("arbitrary",) * 3 if ordered else ("parallel", "parallel", "arbitrary"))),
    )(a, b, *[r[0] for r in rows], *vecs, *(exchange.arrays if exchange else []))
    main = res[:n_res] if epilogue else res[0]
    return (main, res[n_res:]) if exchange else main


_NT = (((1,), (1,)), ((), ()))
_TN = (((0,), (0,)), ((), ()))


def _dot(a, b):
    return jnp.dot(a, b, preferred_element_type=F32)


def _dot_nt(a, b):
    return lax.dot_general(a, b, _NT, preferred_element_type=F32)


def _dot_tn(a, b):
    return lax.dot_general(a, b, _TN, preferred_element_type=F32)


def _running_sum(x, tri):
    return _dot(x.astype(MXU_DTYPE), tri)


def _neg_softplus(z):
    u = jnp.exp2(jnp.abs(z) * (-1.0 / math.log(2.0)))
    return -jnp.maximum(z, 0.0) - jnp.log(1.0 + u)


def _walk_blocks(step, st, n, descending, group=2):
    done = 0
    size = group
    while size >= 1:
        def trip(t, s, size=size, done=done):
            js = [done + size * t + g for g in range(size)]
            return step([n - 1 - j for j in js] if descending else js, s)

        trips = (n - done) // size
        st = lax.fori_loop(0, trips, trip, st)
        done = done + size * trips
        size //= 2
    return st


def _chains(js):
    return [(h, t) for t in range(len(js)) for h in range(2)]


def _rowsum(x):
    return jnp.sum(x, axis=1, keepdims=True)


def _attn_consts():
    row = lax.broadcasted_iota(jnp.int32, (BQ, BK), 0)
    col = lax.broadcasted_iota(jnp.int32, (BQ, BK), 1)
    trow = lax.broadcasted_iota(jnp.int32, (BK, BK), 0)
    tcol = lax.broadcasted_iota(jnp.int32, (BK, BK), 1)
    lane = lax.broadcasted_iota(jnp.int32, (BQ, LANES), 1)
    klane = lax.broadcasted_iota(jnp.int32, (BK, LANES), 1)
    return row, col, trow, tcol, lane, klane


assert BQ == BK


def _diag_step(jd, descending, diag_mask, group=4):
    below = list(range(group)) if descending else list(reversed(range(group)))
    js = [jnp.maximum(jd - o, 0) for o in below]
    masks = [diag_mask if o == 0 else jd - o >= 0 for o in below]
    return js, masks, jnp.maximum(jd - (group - 1), 0)


def _key_slice(j):
    return pl.ds(pl.multiple_of(j * BK, BK), BK)


def _sb_forward(proj, s_len, exchange):
    nq = s_len // BQ
    assert s_len // BK <= HD
    qc, kc, vc = O_SBQ // LANES, O_SBK // LANES, O_SBV // LANES
    n_ex = len(exchange.arrays)

    def body(q_ref, k_ref, v_ref, *rest):
        x_refs, (o_ref, tails_ref) = rest[:n_ex], rest[n_ex:n_ex + 2]
        out_refs, sems = rest[n_ex + 2:2 * n_ex + 2], rest[2 * n_ex + 2:]
        p = pl.program_id(0)
        i = pl.program_id(1)
        jd = i

        @pl.when((p == 0) & (i == 0))
        def _():
            exchange.start(x_refs, out_refs, sems)

        row, col, trow, tcol, lane, klane = _attn_consts()
        tri = (trow >= tcol).astype(MXU_DTYPE)
        q = q_ref[...] * 0.125
        qh = [jnp.where(lane < HD, q, 0.0).astype(MXU_DTYPE), jnp.where(lane >= HD, q, 0.0).astype(MXU_DTYPE)]

        km = [klane < HD, klane >= HD]

        def step(js, st, masks):
            carry, acc, tail = st
            chains = _chains(js)
            kj = [k_ref[_key_slice(j), :].astype(MXU_DTYPE) for j in js]
            vj = [v_ref[_key_slice(j), :].astype(MXU_DTYPE) for j in js]
            z = {(h, t): _dot_nt(qh[h], kj[t]) for h, t in chains}
            run = list(carry)
            suf, carry_in = {}, {}
            for h, t in chains:
                lom = _neg_softplus(z[h, t])
                if masks:
                    lom = jnp.where(masks[t], lom, 0.0)
                suf[h, t] = _running_sum(lom, tri)
                carry_in[h, t] = run[h]
                run[h] = run[h] + _rowsum(lom)
            for h, t in chains:
                a = jnp.exp(z[h, t] + suf[h, t] + carry_in[h, t])
                if masks:
                    a = jnp.where(masks[t], a, 0.0)
                acc = acc + _dot(a.astype(MXU_DTYPE), jnp.where(km[h], vj[t], 0))
                tail_lane = js[t] if not masks or masks[t].ndim else jnp.where(masks[t], js[t], -LANES)
                tail = jnp.where(lane == h * HD + tail_lane, carry_in[h, t], tail)
            return tuple(run), acc, tail

        zero = jnp.zeros((BQ, LANES), F32)
        diag_js, masks, left = _diag_step(jd, True, col < row)
        st = step(diag_js, ((jnp.zeros((BQ, 1), F32),) * 2, zero, zero), masks)
        st = _walk_blocks(lambda js, s: step(js, s, None), st, left, True, group=8)
        o_ref[...] = st[1]
        tails_ref[...] = st[2]

        @pl.when((p == HEADS // 2 - 1) & (i == nq - 1))
        def _():
            exchange.wait(x_refs, out_refs, sems)

    blk = pl.BlockSpec((BQ, LANES), lambda p, i: (i, p))
    out = jax.ShapeDtypeStruct((s_len, SB_W), F32)
    res = pl.pallas_call(
        body, name="sb_fwd", grid=(HEADS // 2, nq),
        in_specs=[pl.BlockSpec((BQ, LANES), lambda p, i: (i, qc + p)),
                  pl.BlockSpec((s_len, LANES), lambda p, i: (0, kc + p)),
                  pl.BlockSpec((s_len, LANES), lambda p, i: (0, vc + p))] + exchange.in_specs,
        out_specs=[blk, blk] + exchange.out_specs, out_shape=[out, out] + exchange.out_shape,
        scratch_shapes=exchange.scratch,
        compiler_params=_cparams("arbitrary", "arbitrary"),
    )(proj, proj, proj, *exchange.arrays)
    return res[:2], res[2:]


def _sb_backward(proj, tails, do, s_len, exchange):
    nq = s_len // BQ
    qc, kc, vc = O_SBQ // LANES, O_SBK // LANES, O_SBV // LANES
    n_ex = len(exchange.arrays)

    def body(q_ref, k_ref, v_ref, tails_ref, do_ref, *rest):
        x_refs, (dq_ref, dk_ref, dv_ref) = rest[:n_ex], rest[n_ex:n_ex + 3]
        out_refs, sems = rest[n_ex + 3:2 * n_ex + 3], rest[2 * n_ex + 3:]
        p = pl.program_id(0)
        i = pl.program_id(1)
        jd = i

        @pl.when((p == 0) & (i == 0))
        def _():
            exchange.start(x_refs, out_refs, sems)

        @pl.when(i == 0)
        def _():
            dk_ref[...] = jnp.zeros(dk_ref.shape, F32)
            dv_ref[...] = jnp.zeros(dv_ref.shape, F32)

        row, col, trow, tcol, lane, klane = _attn_consts()
        tri = (trow >= tcol).astype(MXU_DTYPE)
        tri_p = (trow <= tcol).astype(MXU_DTYPE)
        q = q_ref[...] * 0.125
        tails_blk = tails_ref[...]
        do_blk = do_ref[...]
        hm = [lane < HD, lane >= HD]
        km = [klane < HD, klane >= HD]
        qh = [jnp.where(m, q, 0.0).astype(MXU_DTYPE) for m in hm]
        doh = [jnp.where(m, do_blk, 0.0).astype(MXU_DTYPE) for m in hm]

        def step(js, st, masks):
            before, dq = st
            chains = _chains(js)
            kj = [k_ref[_key_slice(j), :].astype(MXU_DTYPE) for j in js]
            vj = [v_ref[_key_slice(j), :].astype(MXU_DTYPE) for j in js]
            z = {(h, t): _dot_nt(qh[h], kj[t]) for h, t in chains}
            da = {(h, t): _dot_nt(doh[h], vj[t]) for h, t in chains}
            suf, sig = {}, {}
            for h, t in chains:
                lom = _neg_softplus(z[h, t])
                if masks:
                    lom = jnp.where(masks[t], lom, 0.0)
                suf[h, t] = _running_sum(lom, tri)
                sig[h, t] = jnp.exp(z[h, t] + lom)
            run = list(before)
            dl, pre, before_in = {}, {}, {}
            dk_add, dv_add = [None] * len(js), [None] * len(js)
            for h, t in chains:
                tail = _rowsum(jnp.where(lane == h * HD + js[t], tails_blk, 0.0))
                a = jnp.exp(z[h, t] + suf[h, t] + tail)
                if masks:
                    a = jnp.where(masks[t], a, 0.0)
                dl[h, t] = da[h, t] * a
                pre[h, t] = _dot(dl[h, t].astype(MXU_DTYPE), tri_p)
                dv_h = _dot_tn(a.astype(MXU_DTYPE), doh[h])
                dv_add[t] = dv_h if dv_add[t] is None else dv_add[t] + dv_h
                before_in[h, t] = run[h]
                run[h] = run[h] + _rowsum(dl[h, t])
            for h, t in chains:
                upto = before_in[h, t] + pre[h, t]
                dz = dl[h, t] - sig[h, t] * upto
                if masks:
                    dz = jnp.where(masks[t], dz, 0.0)
                dzb = dz.astype(MXU_DTYPE)
                dq = dq + _dot(dzb, jnp.where(km[h], kj[t], 0))
                dk_h = _dot_tn(dzb, qh[h])
                dk_add[t] = dk_h if dk_add[t] is None else dk_add[t] + dk_h
            for t, j in enumerate(js):
                dk_ref[_key_slice(j), :] += dk_add[t]
                dv_ref[_key_slice(j), :] += dv_add[t]
            return tuple(run), dq

        st = ((jnp.zeros((BQ, 1), F32),) * 2, jnp.zeros((BQ, LANES), F32))
        diag_js, masks, left = _diag_step(jd, False, col < row, group=1)
        st = _walk_blocks(lambda js, s: step(js, s, None), st, left, False, group=4)
        st = step(diag_js, st, masks)
        dq_ref[...] = st[1] * 0.125

        @pl.when((p == HEADS // 2 - 1) & (i == nq - 1))
        def _():
            exchange.wait(x_refs, out_refs, sems)

    blk = pl.BlockSpec((BQ, LANES), lambda p, i: (i, p))
    full = pl.BlockSpec((s_len, LANES), lambda p, i: (0, p))
    out = jax.ShapeDtypeStruct((s_len, SB_W), F32)
    res = pl.pallas_call(
        body, name="sb_bwd", grid=(HEADS // 2, nq),
        in_specs=[pl.BlockSpec((BQ, LANES), lambda p, i: (i, qc + p)),
                  pl.BlockSpec((s_len, LANES), lambda p, i: (0, kc + p)),
                  pl.BlockSpec((s_len, LANES), lambda p, i: (0, vc + p)),
                  blk, blk] + exchange.in_specs,
        out_specs=[blk, full, full] + exchange.out_specs, out_shape=[out, out, out] + exchange.out_shape,
        scratch_shapes=exchange.scratch,
        compiler_params=_cparams("arbitrary", "arbitrary"),
    )(proj, proj, proj, tails, do, *exchange.arrays)
    return res[:3], res[3:]


def _pair_mask(rows, h):
    lane = lax.broadcasted_iota(jnp.int32, (rows, 2 * LANES), 1)
    rot = lane - LANES
    return (((lane < LANES) & (lane // HD == h))
            | ((lane >= LANES) & (rot < 2 * ROPE) & ((rot // (ROPE // 2)) % 2 == h)))


def _mla_forward(q_cat, k_cat, kv, s_len):
    nq = s_len // BQ
    scale = 1.0 / math.sqrt(QK_DIM)

    def body(q_ref, k_ref, v_ref, o_ref, lse_ref):
        i = pl.program_id(1)
        jd = i
        row, col, trow, tcol, lane, klane = _attn_consts()
        q = q_ref[...]
        hm = [lane < HD, lane >= HD]
        km = [klane < HD, klane >= HD]
        qh = [jnp.where(_pair_mask(BQ, h), q, 0) for h in range(2)]

        def step(js, st, masks):
            m_run, l_run, acc = st
            chains = _chains(js)
            kj = [k_ref[_key_slice(j), :] for j in js]
            vj = [v_ref[_key_slice(j), :].astype(MXU_DTYPE) for j in js]
            s = {}
            for h, t in chains:
                s[h, t] = _dot_nt(qh[h], kj[t]) * scale
                if masks:
                    s[h, t] = jnp.where(masks[t], s[h, t], -jnp.inf)
            m_new, alpha, l_new = [], [], []
            for h in range(2):
                top = m_run[h]
                for t in range(len(js)):
                    top = jnp.maximum(top, jnp.max(s[h, t], axis=1, keepdims=True))
                m_new.append(top)
                alpha.append(jnp.exp(m_run[h] - top))
                l_new.append(alpha[h] * l_run[h])
            add = None
            for h, t in chains:
                pr = jnp.exp(s[h, t] - m_new[h])
                l_new[h] = l_new[h] + _rowsum(pr)
                part = _dot(pr.astype(MXU_DTYPE), jnp.where(km[h], vj[t], 0))
                add = part if add is None else add + part
            acc = jnp.where(hm[0], alpha[0], alpha[1]) * acc + add
            return tuple(m_new), tuple(l_new), acc

        st = ((jnp.full((BQ, 1), -1e30, F32),) * 2, (jnp.zeros((BQ, 1), F32),) * 2, jnp.zeros((BQ, LANES), F32))
        diag_js, masks, left = _diag_step(jd, True, col <= row)
        st = step(diag_js, st, masks)
        m_run, l_run, acc = _walk_blocks(lambda js, s: step(js, s, None), st, left, True, group=8)
        o_ref[...] = acc / jnp.where(hm[0], l_run[0], l_run[1])
        lse_ref[...] = jnp.where(hm[0], m_run[0] + jnp.log(l_run[0]), m_run[1] + jnp.log(l_run[1]))

    blk = pl.BlockSpec((BQ, LANES), lambda p, i: (i, p))
    out = jax.ShapeDtypeStruct((s_len, MLA_W), F32)
    return pl.pallas_call(
        body, name="mla_fwd", grid=(HEADS // 2, nq),
        in_specs=[pl.BlockSpec((BQ, 2 * LANES), lambda p, i: (i, p)),
                  pl.BlockSpec((s_len, 2 * LANES), lambda p, i: (0, p)),
                  pl.BlockSpec((s_len, LANES), lambda p, i: (0, MLA_W // LANES + p))],
        out_specs=[blk, blk], out_shape=[out, out],
        compiler_params=_cparams("parallel", "parallel"),
    )(q_cat, k_cat, kv)


def _mla_backward(q_cat, k_cat, kv, o, lse, do, s_len):
    nq = s_len // BQ
    scale = 1.0 / math.sqrt(QK_DIM)

    def body(q_ref, k_ref, v_ref, o_ref, lse_ref, do_ref, dq_ref, dk_ref, dv_ref):
        i = pl.program_id(1)

        @pl.when(i == 0)
        def _():
            dk_ref[...] = jnp.zeros(dk_ref.shape, F32)
            dv_ref[...] = jnp.zeros(dv_ref.shape, F32)

        jd = i
        row, col, trow, tcol, lane, klane = _attn_consts()
        q = q_ref[...]
        o_blk = o_ref[...]
        do_blk = do_ref[...]
        lse_blk = lse_ref[...]
        hm = [lane < HD, lane >= HD]
        kpm = [_pair_mask(BK, h) for h in range(2)]
        qh = [jnp.where(_pair_mask(BQ, h), q, 0) for h in range(2)]
        doh_f = [jnp.where(m, do_blk, 0.0) for m in hm]
        doh = [d.astype(MXU_DTYPE) for d in doh_f]
        delta = [jnp.sum(d * o_blk, axis=1, keepdims=True) for d in doh_f]
        lse_h = [jnp.sum(jnp.where(lane == h * HD, lse_blk, 0.0), axis=1, keepdims=True) for h in range(2)]

        def step(js, st, masks):
            dq = st
            chains = _chains(js)
            kj = [k_ref[_key_slice(j), :] for j in js]
            vj = [v_ref[_key_slice(j), :].astype(MXU_DTYPE) for j in js]
            s = {(h, t): _dot_nt(qh[h], kj[t]) for h, t in chains}
            dp = {(h, t): _dot_nt(doh[h], vj[t]) for h, t in chains}
            adds = [[None] * len(js) for _ in range(2)]

            def accumulate(slot, t, part):
                adds[slot][t] = part if adds[slot][t] is None else adds[slot][t] + part

            for h, t in chains:
                pr = jnp.exp(s[h, t] * scale - lse_h[h])
                if masks:
                    pr = jnp.where(masks[t], pr, 0.0)
                dsb = (pr * (dp[h, t] - delta[h]) * scale).astype(MXU_DTYPE)
                dq = dq + _dot(dsb, jnp.where(kpm[h], kj[t], 0))
                accumulate(0, t, _dot_tn(dsb, qh[h]))
                accumulate(1, t, _dot_tn(pr.astype(MXU_DTYPE), doh[h]))
            for t, j in enumerate(js):
                dk_ref[_key_slice(j), :] += adds[0][t]
                dv_ref[_key_slice(j), :] += adds[1][t]
            return dq

        diag_js, masks, left = _diag_step(jd, True, col <= row)
        st = step(diag_js, jnp.zeros((BQ, 2 * LANES), F32), masks)
        dq_ref[...] = _walk_blocks(lambda js, s: step(js, s, None), st, left, True, group=8)

    blk = pl.BlockSpec((BQ, LANES), lambda p, i: (i, p))
    full = pl.BlockSpec((s_len, LANES), lambda p, i: (0, p))
    out = jax.ShapeDtypeStruct((s_len, MLA_W), F32)
    out_cat = jax.ShapeDtypeStruct((s_len, 2 * MLA_W), F32)
    return pl.pallas_call(
        body, name="mla_bwd", grid=(HEADS // 2, nq),
        in_specs=[pl.BlockSpec((BQ, 2 * LANES), lambda p, i: (i, p)),
                  pl.BlockSpec((s_len, 2 * LANES), lambda p, i: (0, p)),
                  pl.BlockSpec((s_len, LANES), lambda p, i: (0, MLA_W // LANES + p)),
                  blk, blk, blk],
        out_specs=[pl.BlockSpec((BQ, 2 * LANES), lambda p, i: (i, p)),
                   pl.BlockSpec((s_len, 2 * LANES), lambda p, i: (0, p)), full],
        out_shape=[out_cat, out_cat, out],
        compiler_params=_cparams("arbitrary", "arbitrary"),
    )(q_cat, k_cat, kv, o, lse, do)


def _mesh_pos():
    return lax.axis_index("x"), lax.axis_index("y"), lax.axis_index("c")


def _dev_index(px, py, pc):
    return 4 * px + 2 * py + pc


def _all_gather(block, name):
    return _all_gather_parts([block], name)[0]


def _all_gather_parts(blocks, name):
    n = len(blocks)

    def body(*refs):
        x_refs, out_refs = refs[:n], refs[n:2 * n]
        send_sems, recv_sems, local_sems = refs[2 * n:]
        x, y, c = _mesh_pos()
        me, sibling = (x, y, c), (x, y, 1 - c)
        chips = [(1 - x, y), (x, 1 - y), (1 - x, 1 - y)]

        def copy(a, k, blockpos, to, src=None):
            slot = out_refs[a].at[_dev_index(*blockpos)]
            return pltpu.make_async_remote_copy(
                src_ref=slot if src is None else src, dst_ref=slot,
                send_sem=send_sems.at[7 * a + k], recv_sem=recv_sems.at[7 * a + k],
                device_id=to, device_id_type=pl.DeviceIdType.MESH)

        mine = [pltpu.make_async_copy(x_refs[a], out_refs[a].at[_dev_index(*me)], local_sems.at[a]) for a in range(n)]
        for cp in mine:
            cp.start()
        first = []
        for a in range(n):
            first.append(copy(a, 0, me, sibling, src=x_refs[a]))
            first += [copy(a, 1 + j, me, (*chip, c), src=x_refs[a]) for j, chip in enumerate(chips)]
        for cp in first:
            cp.start()
        passed = []
        for j, chip in enumerate(chips):
            for a in range(n):
                copy(a, 1 + j, (*chip, c), me).wait_recv()
                passed.append(copy(a, 4 + j, (*chip, c), sibling))
                passed[-1].start()
        for a in range(n):
            copy(a, 0, sibling, me).wait_recv()
            for j, chip in enumerate(chips):
                copy(a, 4 + j, (*chip, 1 - c), me).wait_recv()
        for cp in first + passed:
            cp.wait_send()
        for cp in mine:
            cp.wait()

    return pl.pallas_call(
        body, name=name,
        out_shape=[jax.ShapeDtypeStruct((N_DEV,) + b.shape, b.dtype) for b in blocks],
        in_specs=[pl.BlockSpec(memory_space=pl.ANY)] * n, out_specs=[pl.BlockSpec(memory_space=pl.ANY)] * n,
        scratch_shapes=[pltpu.SemaphoreType.DMA((7 * n,)), pltpu.SemaphoreType.DMA((7 * n,)),
                        pltpu.SemaphoreType.DMA((n,))],
    )(*blocks)


class _Exchange:
    def __init__(self, arrays):
        self.arrays = list(arrays)
        n = len(self.arrays)
        self.in_specs = [pl.BlockSpec(memory_space=pl.ANY)] * n
        self.out_specs = [pl.BlockSpec(memory_space=pl.ANY)] * n
        self.out_shape = [jax.ShapeDtypeStruct(a.shape, a.dtype) for a in self.arrays]
        self.scratch = [pltpu.SemaphoreType.DMA((7 * n,)), pltpu.SemaphoreType.DMA((7 * n,)),
                        pltpu.SemaphoreType.DMA((n,))]

    def _copies(self, x_refs, out_refs, sems, with_arrivals):
        send_sems, recv_sems, local_sems = sems
        x, y, c = _mesh_pos()
        me = _dev_index(x, y, c)
        flips = [(fx, fy, fc) for fx in (0, 1) for fy in (0, 1) for fc in (0, 1)][1:]
        peers = [(1 - x if fx else x, 1 - y if fy else y, 1 - c if fc else c) for fx, fy, fc in flips]
        mine, sends, arrivals = [], [], []
        for a in range(len(self.arrays)):
            mine.append(pltpu.make_async_copy(x_refs[a].at[me], out_refs[a].at[me], local_sems.at[a]))
            for k, peer in enumerate(peers):
                sends.append(pltpu.make_async_remote_copy(
                    src_ref=x_refs[a].at[_dev_index(*peer)], dst_ref=out_refs[a].at[me],
                    send_sem=send_sems.at[7 * a + k], recv_sem=recv_sems.at[7 * a + k],
                    device_id=peer, device_id_type=pl.DeviceIdType.MESH))
                if not with_arrivals:
                    continue
                arrivals.append(pltpu.make_async_remote_copy(
                    src_ref=x_refs[a].at[me], dst_ref=out_refs[a].at[_dev_index(*peer)],
                    send_sem=send_sems.at[7 * a + k], recv_sem=recv_sems.at[7 * a + k],
                    device_id=peer, device_id_type=pl.DeviceIdType.MESH))
        return mine, sends, arrivals

    def start(self, x_refs, out_refs, sems):
        mine, sends, _ = self._copies(x_refs, out_refs, sems, False)
        for cp in mine + sends:
            cp.start()

    def wait(self, x_refs, out_refs, sems):
        mine, sends, arrivals = self._copies(x_refs, out_refs, sems, True)
        for cp in arrivals:
            cp.wait_recv()
        for cp in sends:
            cp.wait_send()
        for cp in mine:
            cp.wait()


def _sum_blocks(parts, name):
    n, r, c = parts.shape
    row_tiles = [t for t in range(16, min(r, 2048) + 1, 16) if r % t == 0]
    if row_tiles:
        tr, tc = max(row_tiles), c
    else:
        tr, tc = r, 2 * LANES
    assert c % tc == 0

    def body(p_ref, o_ref):
        acc = p_ref[0].astype(F32)
        for s in range(1, n):
            acc = acc + p_ref[s].astype(F32)
        o_ref[...] = acc

    return pl.pallas_call(
        body, name=name, grid=(r // tr, c // tc),
        in_specs=[pl.BlockSpec((n, tr, tc), lambda i, j: (0, i, j))],
        out_specs=pl.BlockSpec((tr, tc), lambda i, j: (i, j)),
        out_shape=jax.ShapeDtypeStruct((r, c), F32),
        compiler_params=_cparams("parallel", "parallel"),
    )(parts)


def _sigmoid(x):
    return 1.0 / (1.0 + jnp.exp(-x))


def _silu(x):
    return x * _sigmoid(x)


def _silu_grad(x):
    s = _sigmoid(x)
    return s * (1.0 + x * (1.0 - s))


def _colsum(x):
    return jnp.sum(x, axis=0, keepdims=True)


def _rms(x):
    return lax.rsqrt(jnp.mean(x * x, axis=-1, keepdims=True) + EPS)


def _rms_bwd(xn, r, dxn):
    return r * (dxn - xn * jnp.mean(dxn * xn, axis=-1, keepdims=True))


def _adamw(w, g, m, v):
    m = ADAM_B1 * m + (1.0 - ADAM_B1) * g
    v = ADAM_B2 * v + (1.0 - ADAM_B2) * jnp.square(g)
    m_hat = m / (1.0 - ADAM_B1 ** ADAM_STEP)
    v_hat = v / (1.0 - ADAM_B2 ** ADAM_STEP)
    delta = -ADAM_LR * (m_hat / (jnp.sqrt(v_hat) + ADAM_EPS) + ADAM_WD * w)
    return delta, m, v


def _adamw_call(w, g, m, v, name):
    r, c = w.shape
    row_tiles = [t for t in range(8, min(r, 1024) + 1, 8) if r % t == 0]
    tr = max(row_tiles) if row_tiles and r * c > 256 * 1024 else r

    def body(w_ref, g_ref, m_ref, v_ref, d_out, m_out, v_out):
        d_out[...], m_out[...], v_out[...] = _adamw(w_ref[...], g_ref[...], m_ref[...], v_ref[...])

    spec = pl.BlockSpec((tr, c), lambda i: (i, 0))
    return pl.pallas_call(
        body, name=name, grid=(r // tr,), in_specs=[spec] * 4, out_specs=[spec] * 3,
        out_shape=[jax.ShapeDtypeStruct((r, c), F32)] * 3, compiler_params=_cparams("parallel"),
    )(w, g, m, v)


def _uq_to_kernel_layout(w):
    lead = w.shape[:-1]
    t = w.reshape(lead + (HEADS, QK_DIM))
    return jnp.concatenate([t[..., :NOPE].reshape(lead + (HEADS * NOPE,)),
                            t[..., NOPE:NOPE + ROPE // 2].reshape(lead + (LANES,)),
                            t[..., NOPE + ROPE // 2:].reshape(lead + (LANES,))], axis=-1)


def _uq_from_kernel_layout(w):
    lead = w.shape[:-1]
    nope = w[..., :HEADS * NOPE].reshape(lead + (HEADS, NOPE))
    r1 = w[..., HEADS * NOPE:HEADS * NOPE + LANES].reshape(lead + (HEADS, ROPE // 2))
    r2 = w[..., HEADS * NOPE + LANES:].reshape(lead + (HEADS, ROPE // 2))
    return jnp.concatenate([nope, r1, r2], axis=-1).reshape(lead + (HEADS * QK_DIM,))


def _ukv_to_kernel_layout(w):
    lead = w.shape[:-1]
    t = w.reshape(lead + (HEADS, NOPE + HD))
    return jnp.concatenate([t[..., :NOPE].reshape(lead + (HEADS * NOPE,)),
                            t[..., NOPE:].reshape(lead + (HEADS * HD,))], axis=-1)


def _ukv_from_kernel_layout(w):
    lead = w.shape[:-1]
    kn = w[..., :HEADS * NOPE].reshape(lead + (HEADS, NOPE))
    vv = w[..., HEADS * NOPE:].reshape(lead + (HEADS, HD))
    return jnp.concatenate([kn, vv], axis=-1).reshape(lead + (HEADS * (NOPE + HD),))


def _w_in_t_to_kernel_layout(wt):
    sb = wt[0:2048]
    c_q = wt[2048:2432]
    c_kv = wt[2432:2688]
    k_rot = wt[2688:2720]
    mla_z = wt[2720:3232]
    gates = wt[3232:5280]
    zeros = jnp.zeros((LANES, wt.shape[1]), wt.dtype)
    k1 = jnp.tile(k_rot[:ROPE // 2], (HEADS, 1))
    k2 = jnp.tile(k_rot[ROPE // 2:], (HEADS, 1))
    return jnp.concatenate([gates, sb, mla_z, c_q, zeros, c_kv, k1, k2], axis=0)


def _w_in_t_from_kernel_layout(gt, g_rot):
    return jnp.concatenate([gt[O_SBQ:O_SBQ + 2048], gt[O_CQ:O_CQ + Q_RANK], gt[O_CKV:O_CKV + KV_RANK],
                            g_rot.astype(gt.dtype), gt[O_MLAZ:O_MLAZ + MLA_W], gt[O_GA:O_GA + 2 * D]], axis=0)


def kernel(x, c, positions, w_ada, b_ada, norm_gain, w_in, q_norm_gain, w_uq, kv_norm_gain, w_ukv, w_branch_a, w_branch_b, w_out, final_norm_gain, loss_target, m_w_ada, m_b_ada, m_norm_gain, m_w_in, m_q_norm_gain, m_w_uq, m_kv_norm_gain, m_w_ukv, m_w_branch_a, m_w_branch_b, m_w_out, m_final_norm_gain, v_w_ada, v_b_ada, v_norm_gain, v_w_in, v_q_norm_gain, v_w_uq, v_kv_norm_gain, v_w_ukv, v_w_branch_a, v_w_branch_b, v_w_out, v_final_norm_gain):
    s_len = x.shape[1]
    me = _dev_index(*_mesh_pos())
    x2d = x[0]
    tgt = loss_target[0]

    w_in_t = w_in[0].T.astype(BF16)
    big = [w_uq[0], w_ukv[0], w_branch_a[0], w_branch_b[0], w_out[0]]
    big_sizes = [int(w.size) for w in big]
    packed = jnp.concatenate([w.astype(BF16).reshape(-1, LANES) for w in big], axis=0)
    g_in_t, c_all = _all_gather_parts([w_in_t, c.reshape(8, LANES)], "gather_w_in")
    c_all = c_all.reshape(N_DEV, D)
    w_in_kt = _w_in_t_to_kernel_layout(g_in_t.reshape(N_DEV * w_in_t.shape[0], D))

    mod_cols = _mm(c_all, w_ada[0], name="ada_mod")
    mod_all = _all_gather(mod_cols, "gather_mod")
    mod = lax.dynamic_index_in_dim(mod_all, me, axis=1, keepdims=False).reshape(1, 3 * D)
    mod_shift, mod_scale, mod_gate = mod[:, :D], mod[:, D:2 * D], mod[:, 2 * D:]
    b_shift, b_scale, b_gate = b_ada[:, :D], b_ada[:, D:2 * D], b_ada[:, 2 * D:]
    g1 = norm_gain
    gq, gkv = q_norm_gain, kv_norm_gain
    gf = final_norm_gain.reshape(1, D)

    def f_h(x_, g1_, ms, bs, msc, bsc):
        xn = x_ * _rms(x_)
        return (xn * g1_ * (1.0 + (msc + bsc)) + (ms + bs),), ()

    (h,) = _rowwise(f_h, [x2d], [g1, mod_shift, b_shift, mod_scale, b_scale], [(D, BF16)], name="ada_norm")
    proj = _mm(h, w_in_kt, tb=True, name="proj_in", tiles=(min(s_len, 1024), IN_PAD // 2, D))

    (o_a, sb_tails), (gathered,) = _sb_forward(
        proj, s_len, _Exchange([jnp.broadcast_to(packed[None], (N_DEV,) + packed.shape)]))
    offs = [0]
    for n in big_sizes:
        offs.append(offs[-1] + n // LANES)

    def unpack(t, shape):
        return gathered[:, offs[t]:offs[t + 1], :].reshape((N_DEV,) + shape)

    def cols(t, shape):
        return unpack(t, shape).transpose(1, 0, 2).reshape(shape[0], N_DEV * shape[1])

    w_uq_k = _uq_to_kernel_layout(cols(0, big[0].shape))
    w_ukv_k = _ukv_to_kernel_layout(cols(1, big[1].shape))
    w_a_f = cols(2, big[2].shape)
    w_b_f = cols(3, big[3].shape)
    w_out_f = unpack(4, big[4].shape).reshape(D, D)

    def f_lat(cq, ckv, gq_, gkv_):
        return (cq * _rms(cq) * gq_, ckv * _rms(ckv) * gkv_), ()

    cq_n, ckv_n = _rowwise(f_lat, [(proj, O_CQ // Q_RANK, Q_RANK), (proj, O_CKV // KV_RANK, KV_RANK)], [gq, gkv],
                           [(Q_RANK, BF16), (KV_RANK, BF16)], name="latent_norm")
    q_mla = _mm(cq_n, w_uq_k, name="q_up")
    kv = _mm(ckv_n, w_ukv_k, name="kv_up")

    inv_freq = ROPE_BASE ** (-jnp.arange(0, ROPE, 2, dtype=F32) / ROPE)
    inv_freq_t = jnp.tile(inv_freq, HEADS).reshape(1, LANES)
    pos_col = positions.reshape(s_len, 1).astype(F32)

    pairs = HEADS // 2

    def f_rope(pos, qn, q1, q2, kn, k1, k2, freq):
        ang = pos * freq
        cs, sn = jnp.cos(ang), jnp.sin(ang)
        q1r, q2r = q1 * cs - q2 * sn, q1 * sn + q2 * cs
        k1r, k2r = k1 * cs - k2 * sn, k1 * sn + k2 * cs
        lane = lax.broadcasted_iota(jnp.int32, q1.shape, 1)
        first, second = lane < ROPE, (lane >= ROPE) & (lane < 2 * ROPE)
        k_rot = jnp.where(first, k1r, jnp.where(second, k2r, 0.0))
        q_parts, k_parts = [], []
        for p in range(pairs):
            q_rot = jnp.where(first, pltpu.roll(q1r, (LANES - ROPE * p) % LANES, 1),
                              jnp.where(second, pltpu.roll(q2r, (LANES + ROPE - ROPE * p) % LANES, 1), 0.0))
            q_parts += [qn[:, LANES * p:LANES * (p + 1)], q_rot]
            k_parts += [kn[:, LANES * p:LANES * (p + 1)], k_rot]
        return (jnp.concatenate(q_parts, axis=1), jnp.concatenate(k_parts, axis=1), cs, sn), ()

    q_cat, k_cat, cos_t, sin_t = _rowwise(
        f_rope, [pos_col, (q_mla, 0, MLA_W), (q_mla, 4, LANES), (q_mla, 5, LANES), (kv, 0, MLA_W),
                 (proj, O_KROT // LANES, LANES), (proj, O_KROT // LANES + 1, LANES)], [inv_freq_t],
        [(2 * MLA_W, BF16), (2 * MLA_W, BF16), (LANES, F32), (LANES, F32)], name="rope")

    o_b, lse = _mla_forward(q_cat, k_cat, kv, s_len)

    def f_gate(oa, za, ob, zb):
        return (oa * _silu(za), ob * _silu(zb)), ()

    ya_in, yb_in = _rowwise(f_gate, [o_a, (proj, O_SBZ // SB_W, SB_W), o_b, (proj, O_MLAZ // MLA_W, MLA_W)], [],
                            [(SB_W, BF16), (MLA_W, BF16)], name="branch_gate")
    y_a = _mm(ya_in, w_a_f, name="branch_a")

    def f_merge(yb, ga, gb, ya):
        return (yb, _sigmoid(ga) * ya + _sigmoid(gb) * yb), ()

    y_b, merged = _mm(yb_in, w_b_f, name="branch_b_merge", tiles=(min(s_len, 512), D, MLA_W),
                      epilogue=(f_merge, [(proj, O_GA // D, D), (proj, O_GB // D, D), y_a], [], [(D, F32), (D, BF16)], []))

    def f_loss(out_, x_, t_, mg, bg, gf_):
        gate = mg + bg
        x2 = x_ + gate * out_
        r2 = _rms(x2)
        xn2 = x2 * r2
        err = xn2 * gf_ - t_
        loss = jnp.full((1, LANES), 0.5 / D, F32) * jnp.sum(err * err)
        dy = err * (1.0 / D)
        dx2 = _rms_bwd(xn2, r2, dy * gf_)
        return (dx2, dx2 * gate), (loss, _colsum(dy * xn2), _colsum(dx2 * out_))

    dx2, d_out, loss_part, d_gf, d_gate = _mm(
        merged, w_out_f, name="out_proj_loss", tiles=(min(s_len, 512), D, D),
        epilogue=(f_loss, [x2d, tgt], [mod_gate, b_gate, gf], [(D, F32), (D, BF16)], [LANES, D, D]))

    dw_out = _mm(merged, d_out, ta=True, name="dw_out")

    def f_dmerge(dm, ga, gb, ya, yb):
        sa, sb = _sigmoid(ga), _sigmoid(gb)
        return (dm * sa, dm * sb, dm * ya * sa * (1.0 - sa), dm * yb * sb * (1.0 - sb)), ()

    d_ya, d_yb, d_ga, d_gb = _mm(
        d_out, w_out_f, tb=True, name="d_merge", tiles=(min(s_len, 256), D, D),
        epilogue=(f_dmerge, [(proj, O_GA // D, D), (proj, O_GB // D, D), y_a, y_b], [], [(D, BF16)] * 4, []))
    dw_a = _mm(ya_in, d_ya, ta=True, name="dw_branch_a")
    dw_b = _mm(yb_in, d_yb, ta=True, name="dw_branch_b")

    def f_dgate(d_in, o_, z_):
        return (d_in * _silu(z_), d_in * o_ * _silu_grad(z_)), ()

    d_oa, d_sbz = _mm(d_ya, w_a_f, tb=True, name="d_branch_a",
                      epilogue=(f_dgate, [o_a, (proj, O_SBZ // SB_W, SB_W)], [], [(SB_W, F32), (SB_W, BF16)], []))
    d_ob, d_mlaz = _mm(d_yb, w_b_f, tb=True, name="d_branch_b",
                       epilogue=(f_dgate, [o_b, (proj, O_MLAZ // MLA_W, MLA_W)], [], [(MLA_W, F32), (MLA_W, BF16)], []))

    dq_cat, dk_cat, dv_b = _mla_backward(q_cat, k_cat, kv, o_b, lse, d_ob, s_len)

    def f_drope(dq, dk, dv_, cs, sn):
        lane = lax.broadcasted_iota(jnp.int32, cs.shape, 1)
        first, second = lane < ROPE, (lane >= ROPE) & (lane < 2 * ROPE)
        dq1 = dq2 = dk1 = dk2 = None
        for p in range(pairs):
            q_rot = dq[:, LANES * (2 * p + 1):LANES * (2 * p + 2)]
            k_rot = dk[:, LANES * (2 * p + 1):LANES * (2 * p + 2)]
            parts = (pltpu.roll(jnp.where(first, q_rot, 0.0), (ROPE * p) % LANES, 1),
                     pltpu.roll(jnp.where(second, q_rot, 0.0), (LANES - ROPE + ROPE * p) % LANES, 1),
                     jnp.where(first, k_rot, 0.0), jnp.where(second, k_rot, 0.0))
            if p == 0:
                dq1, dq2, dk1, dk2 = parts
            else:
                dq1, dq2, dk1, dk2 = dq1 + parts[0], dq2 + parts[1], dk1 + parts[2], dk2 + parts[3]
        dqn_ = [dq[:, 2 * LANES * p:2 * LANES * p + LANES] for p in range(pairs)]
        dkn_ = [dk[:, 2 * LANES * p:2 * LANES * p + LANES] for p in range(pairs)]
        return (jnp.concatenate(dqn_ + [dq1 * cs + dq2 * sn, dq2 * cs - dq1 * sn], axis=1),
                jnp.concatenate(dkn_ + [dv_], axis=1),
                jnp.concatenate([dk1 * cs + dk2 * sn, dk2 * cs - dk1 * sn], axis=1)), ()

    dq_k, dkv_k, d_krot = _rowwise(f_drope, [dq_cat, dk_cat, dv_b, cos_t, sin_t], [],
                                   [(HEADS * QK_DIM, BF16), (2 * MLA_W, BF16), (2 * LANES, BF16)], name="d_rope")
    dw_uq_k = _mm(cq_n, dq_k, ta=True, name="dw_uq")
    dw_ukv_k = _mm(ckv_n, dkv_k, ta=True, name="dw_ukv")

    def f_dlat(d_normed, latent, gain):
        r = _rms(latent)
        normed = latent * r
        return (_rms_bwd(normed, r, d_normed * gain),), (_colsum(d_normed * normed),)

    d_cq, d_gq = _mm(dq_k, w_uq_k, tb=True, name="d_cq_norm",
                     epilogue=(f_dlat, [(proj, O_CQ // Q_RANK, Q_RANK)], [gq], [(Q_RANK, BF16)], [Q_RANK]))
    d_ckv, d_gkv = _mm(dkv_k, w_ukv_k, tb=True, name="d_ckv_norm",
                       epilogue=(f_dlat, [(proj, O_CKV // KV_RANK, KV_RANK)], [gkv], [(KV_RANK, BF16)], [KV_RANK]))

    def col_blocks(g):
        kdim, n8 = g.shape
        return g.astype(BF16).reshape(kdim, N_DEV, n8 // N_DEV).transpose(1, 0, 2).reshape(N_DEV, -1, LANES)

    g_blocks = jnp.concatenate([col_blocks(_uq_from_kernel_layout(dw_uq_k)), col_blocks(_ukv_from_kernel_layout(dw_ukv_k)),
                                col_blocks(dw_a), col_blocks(dw_b), dw_out.astype(BF16).reshape(N_DEV, -1, LANES)], axis=1)
    (d_sbq, d_sbk, d_sbv), (g_recv,) = _sb_backward(proj, sb_tails, d_oa, s_len, _Exchange([g_blocks]))

    d_proj = jnp.concatenate([d_ga, d_gb, d_sbq.astype(BF16), d_sbk.astype(BF16), d_sbv.astype(BF16), d_sbz, d_mlaz,
                              d_cq, jnp.zeros((s_len, LANES), BF16), d_ckv, d_krot], axis=1)
    dw_in_kt = _mm(d_proj, h, ta=True, out_dtype=BF16, name="dw_in", tiles=(512, D, s_len))

    def krot_body(t_ref, o_ref):
        half = ROPE // 2
        for part in range(2):
            acc = t_ref[part * LANES:part * LANES + half, :].astype(F32)
            for hh in range(1, HEADS):
                acc = acc + t_ref[part * LANES + hh * half:part * LANES + (hh + 1) * half, :].astype(F32)
            o_ref[part * half:(part + 1) * half, :] = acc

    dw_krot = pl.pallas_call(krot_body, name="dw_krot_sum", out_shape=jax.ShapeDtypeStruct((ROPE, D), F32))(
        dw_in_kt[O_KROT:O_KROT + 2 * LANES])

    g_in_blocks = _w_in_t_from_kernel_layout(dw_in_kt, dw_krot).reshape(N_DEV, -1, D)
    def f_dx(dh_, x_, dx2_, g1_, msc, bsc):
        r = _rms(x_)
        xn = x_ * r
        dn1 = dh_ * (1.0 + (msc + bsc))
        return ((dx2_ + _rms_bwd(xn, r, dn1 * g1_),),
                (_colsum(dh_), _colsum(dh_ * (xn * g1_)), _colsum(dn1 * xn)))

    (grad_x2d, d_shift, d_scale, d_g1), (g_in_recv,) = _mm(
        d_proj, w_in_kt, name="d_h", tiles=(min(s_len, 512), D, 512), exchange=_Exchange([g_in_blocks]),
        epilogue=(f_dx, [x2d, dx2], [g1, mod_scale, b_scale], [(D, F32)], [D, D, D]))

    g_in_sum_t = _sum_blocks(g_in_recv, "sum_grads_w_in")
    g_sum = _sum_blocks(g_recv, "sum_grads")
    g_big = [g_sum[offs[t]:offs[t + 1]].reshape(big[t].shape) for t in range(5)]

    small = jnp.concatenate([d_shift, d_scale, d_gate, d_g1, d_gq, d_gkv, d_gf, loss_part], axis=1)
    n_small = small.shape[1]
    pad = (-n_small) % (8 * LANES)
    small = jnp.pad(small, ((0, 0), (0, pad))).reshape(-1, LANES)
    small_all = _all_gather(small, "gather_small")
    small_sum = _sum_blocks(small_all, "sum_small").reshape(1, -1)
    g_b_ada = small_sum[:, :3 * D]
    g_g1 = small_sum[:, 3 * D:4 * D]
    g_gq = small_sum[:, 4 * D:4 * D + Q_RANK]
    g_gkv = small_sum[:, 4 * D + Q_RANK:4 * D + Q_RANK + KV_RANK]
    g_gf = small_sum[:, 4 * D + Q_RANK + KV_RANK:4 * D + Q_RANK + KV_RANK + D]

    dmod_all = small_all.reshape(N_DEV, -1)[:, :3 * D]
    dmod_cols = lax.dynamic_slice_in_dim(dmod_all, me * (3 * D // N_DEV), 3 * D // N_DEV, axis=1)
    g_w_ada = _mm(c_all, dmod_cols, ta=True, name="dw_ada")

    loss = small_sum[0, n_small - LANES]

    names = ["w_ada", "b_ada", "norm_gain", "w_in", "q_norm_gain", "w_uq", "kv_norm_gain", "w_ukv",
             "w_branch_a", "w_branch_b", "w_out", "final_norm_gain"]
    weights = dict(w_ada=w_ada, b_ada=b_ada, norm_gain=norm_gain, w_in=w_in, q_norm_gain=q_norm_gain, w_uq=w_uq,
                   kv_norm_gain=kv_norm_gain, w_ukv=w_ukv, w_branch_a=w_branch_a, w_branch_b=w_branch_b, w_out=w_out,
                   final_norm_gain=final_norm_gain)
    moms = dict(w_ada=m_w_ada, b_ada=m_b_ada, norm_gain=m_norm_gain, w_in=m_w_in, q_norm_gain=m_q_norm_gain,
                w_uq=m_w_uq, kv_norm_gain=m_kv_norm_gain, w_ukv=m_w_ukv, w_branch_a=m_w_branch_a,
                w_branch_b=m_w_branch_b, w_out=m_w_out, final_norm_gain=m_final_norm_gain)
    vels = dict(w_ada=v_w_ada, b_ada=v_b_ada, norm_gain=v_norm_gain, w_in=v_w_in, q_norm_gain=v_q_norm_gain,
                w_uq=v_w_uq, kv_norm_gain=v_kv_norm_gain, w_ukv=v_w_ukv, w_branch_a=v_w_branch_a,
                w_branch_b=v_w_branch_b, w_out=v_w_out, final_norm_gain=v_final_norm_gain)
    grads2d = dict(w_ada=g_w_ada, b_ada=g_b_ada, norm_gain=g_g1, w_in=g_in_sum_t, q_norm_gain=g_gq, w_uq=g_big[0],
                   kv_norm_gain=g_gkv, w_ukv=g_big[1], w_branch_a=g_big[2], w_branch_b=g_big[3], w_out=g_big[4],
                   final_norm_gain=g_gf)

    grads, deltas, new_m, new_v = [], [], [], []
    for n in names:
        w = weights[n]
        if n == "w_in":
            cols_in = w.shape[2]
            to2d = lambda t: (t[0].T if t.ndim == 3 else t).reshape(-1, LANES)
            back = lambda t, cols_in=cols_in: t.reshape(cols_in, D).T[None]
        else:
            shape2d = grads2d[n].shape
            to2d = lambda t, s=shape2d: t.reshape(s)
            back = lambda t, s=w.shape: t.reshape(s)
        g2d = to2d(grads2d[n])
        d_, m_, v_ = _adamw_call(to2d(w), g2d, to2d(moms[n]), to2d(vels[n]), "adamw_" + n)
        grads.append(back(g2d))
        deltas.append(back(d_))
        new_m.append(back(m_))
        new_v.append(back(v_))

    return (loss, grad_x2d.reshape(x.shape), *grads, *deltas, *new_m, *new_v)
```

```python
import functools
import math

import jax
import jax.numpy as jnp
from jax import lax
from jax.experimental import pallas as pl
from jax.experimental.pallas import tpu as pltpu

F32 = jnp.float32
BF16 = jnp.bfloat16
MXU_DTYPE = jnp.bfloat16

N_DEV = 8
D = 1024
HEADS = 8
HD = 64
SB_W = 512
MLA_W = 512
Q_RANK = 384
KV_RANK = 256
ROPE = 32
NOPE = 64
QK_DIM = NOPE + ROPE
EPS = 1e-6
ROPE_BASE = 10000.0

ADAM_LR = 0.001
ADAM_B1 = 0.9
ADAM_B2 = 0.999
ADAM_EPS = 1e-08
ADAM_WD = 0.01
ADAM_STEP = 10

LANES = 128
VMEM_LIMIT = 48 * 1024 * 1024

O_GA, O_GB = 0, 1024
O_SBQ, O_SBK, O_SBV, O_SBZ = 2048, 2560, 3072, 3584
O_MLAZ = 4096
O_CQ = 4608
O_CKV = 5120
O_KROT = 5376
IN_PAD = 5632

BQ = 256
BK = 256


def _cparams(*sem):
    return pltpu.CompilerParams(dimension_semantics=sem, vmem_limit_bytes=VMEM_LIMIT)


def _tile_of(n, cap=512):
    if n <= cap:
        return n
    for t in (1024, 768, 512, 384, 256, 128):
        if t <= cap and n % t == 0:
            return t
    raise ValueError(n)


def _rowwise(fn, rows, vecs, outs, reds=(), *, name, tile=512):
    norm = []
    for r in rows:
        if isinstance(r, tuple):
            arr, cb, w = r[:3]
            ro = r[3] if len(r) > 3 else 0
        else:
            arr, cb, w, ro = r, 0, r.shape[1], 0
        norm.append((arr, cb, w, ro))
    s_len = norm[0][0].shape[0]
    tile = min(tile, s_len)
    assert s_len % tile == 0
    n_row, n_vec, n_out, n_red = len(norm), len(vecs), len(outs), len(reds)

    def body(*refs):
        step = pl.program_id(0)
        row_refs = refs[:n_row]
        vec_refs = refs[n_row:n_row + n_vec]
        out_refs = refs[n_row + n_vec:n_row + n_vec + n_out]
        red_refs = refs[n_row + n_vec + n_out:]
        row_res, red_res = fn(*[r[...] for r in row_refs], *[v[...] for v in vec_refs])
        for o, val in zip(out_refs, row_res):
            o[...] = val.astype(o.dtype)
        if n_red:
            @pl.when(step == 0)
            def _():
                for r in red_refs:
                    r[...] = jnp.zeros(r.shape, r.dtype)
            for r, val in zip(red_refs, red_res):
                r[...] += val

    in_specs = []
    for arr, cb, w, ro in norm:
        in_specs.append(pl.BlockSpec((tile, w), functools.partial(lambda i, cb, rb: (i + rb, cb), cb=cb, rb=ro // tile)))
        assert ro % tile == 0
    for v in vecs:
        in_specs.append(pl.BlockSpec(v.shape, lambda i: (0, 0)))
    out_shape = [jax.ShapeDtypeStruct((s_len, w), dt) for w, dt in outs]
    out_specs = [pl.BlockSpec((tile, w), lambda i: (i, 0)) for w, _ in outs]
    out_shape += [jax.ShapeDtypeStruct((1, w), F32) for w in reds]
    out_specs += [pl.BlockSpec((1, w), lambda i: (0, 0)) for w in reds]
    res = pl.pallas_call(
        body, name=name, grid=(s_len // tile,), in_specs=in_specs, out_specs=out_specs, out_shape=out_shape,
        compiler_params=_cparams("arbitrary" if n_red else "parallel"),
    )(*[a for a, _, _, _ in norm], *vecs)
    return res


def _mm(a, b, *, ta=False, tb=False, out_dtype=F32, name, exchange=None, tiles=None, epilogue=None):
    m, k = (a.shape[1], a.shape[0]) if ta else a.shape
    n = b.shape[0] if tb else b.shape[1]
    assert (b.shape[1] if tb else b.shape[0]) == k
    tm, tn, tk = tiles or (_tile_of(m, 1024), _tile_of(n, 1024 if n <= 1024 else 512), _tile_of(k, 1024))
    assert m % tm == 0 and n % tn == 0 and k % tk == 0
    ni, nj, nk = m // tm, n // tn, k // tk
    dims = (((0 if ta else 1,), (1 if tb else 0,)), ((), ()))
    n_ex = len(exchange.arrays) if exchange else 0
    fn, rows, vecs, outs, reds = epilogue or (None, (), (), (), ())
    rows = [r if isinstance(r, tuple) else (r, 0, r.shape[1]) for r in rows]
    assert not epilogue or tn == n
    n_res = len(outs) + len(reds) if epilogue else 1

    def body(*refs):
        a_ref, b_ref = refs[:2]
        row_refs, refs = refs[2:2 + len(rows)], refs[2 + len(rows):]
        vec_refs, refs = refs[:len(vecs)], refs[len(vecs):]
        x_refs, refs = refs[:n_ex], refs[n_ex:]
        res_refs, refs = refs[:n_res], refs[n_res:]
        out_refs, acc_ref, sems = refs[:n_ex], refs[n_ex], refs[n_ex + 1:]
        i, j, kk = pl.program_id(0), pl.program_id(1), pl.program_id(2)
        first = (i == 0) & (j == 0) & (kk == 0)

        if exchange:
            @pl.when(first)
            def _():
                exchange.start(x_refs, out_refs, sems)

        if reds:
            @pl.when(first)
            def _():
                for r in res_refs[len(outs):]:
                    r[...] = jnp.zeros(r.shape, r.dtype)

        @pl.when(kk == 0)
        def _():
            acc_ref[...] = jnp.zeros(acc_ref.shape, F32)

        acc_ref[...] += lax.dot_general(a_ref[...].astype(MXU_DTYPE), b_ref[...].astype(MXU_DTYPE), dims,
                                        preferred_element_type=F32)

        @pl.when(kk == nk - 1)
        def _():
            if not epilogue:
                res_refs[0][...] = acc_ref[...].astype(res_refs[0].dtype)
                return
            row_res, red_res = fn(acc_ref[...], *[r[...] for r in row_refs], *[v[...] for v in vec_refs])
            for o, val in zip(res_refs, row_res):
                o[...] = val.astype(o.dtype)
            for r, val in zip(res_refs[len(outs):], red_res):
                r[...] += val

        if exchange:
            @pl.when((i == ni - 1) & (j == nj - 1) & (kk == nk - 1))
            def _():
                exchange.wait(x_refs, out_refs, sems)

    a_spec = pl.BlockSpec((tk, tm), lambda i, j, kk: (kk, i)) if ta else pl.BlockSpec((tm, tk), lambda i, j, kk: (i, kk))
    b_spec = pl.BlockSpec((tn, tk), lambda i, j, kk: (j, kk)) if tb else pl.BlockSpec((tk, tn), lambda i, j, kk: (kk, j))
    in_specs = [a_spec, b_spec]
    in_specs += [pl.BlockSpec((tm, w), functools.partial(lambda i, j, kk, cb: (i, cb), cb=cb)) for _, cb, w in rows]
    in_specs += [pl.BlockSpec(v.shape, lambda i, j, kk: (0, 0)) for v in vecs]
    if epilogue:
        res_specs = [pl.BlockSpec((tm, w), lambda i, j, kk: (i, 0)) for w, _ in outs]
        res_specs += [pl.BlockSpec((1, w), lambda i, j, kk: (0, 0)) for w in reds]
        res_shape = [jax.ShapeDtypeStruct((m, w), dt) for w, dt in outs] + [jax.ShapeDtypeStruct((1, w), F32) for w in reds]
    else:
        res_specs = [pl.BlockSpec((tm, tn), lambda i, j, kk: (i, j))]
        res_shape = [jax.ShapeDtypeStruct((m, n), out_dtype)]
    ordered = bool(exchange or reds)
    res = pl.pallas_call(
        body, name=name, grid=(ni, nj, nk),
        in_specs=in_specs + (exchange.in_specs if exchange else []),
        out_specs=res_specs + (exchange.out_specs if exchange else []),
        out_shape=res_shape + (exchange.out_shape if exchange else []),
        scratch_shapes=[pltpu.VMEM((tm, tn), F32)] + (exchange.scratch if exchange else []),
        compiler_params=_cparams(*(("arbitrary",) * 3 if ordered else ("parallel", "parallel", "arbitrary"))),
    )(a, b, *[r[0] for r in rows], *vecs, *(exchange.arrays if exchange else []))
    main = res[:n_res] if epilogue else res[0]
    return (main, res[n_res:]) if exchange else main


_NT = (((1,), (1,)), ((), ()))
_TN = (((0,), (0,)), ((), ()))


def _dot(a, b):
    return jnp.dot(a, b, preferred_element_type=F32)


def _dot_nt(a, b):
    return lax.dot_general(a, b, _NT, preferred_element_type=F32)


def _dot_tn(a, b):
    return lax.dot_general(a, b, _TN, preferred_element_type=F32)


def _running_sum(x, tri):
    return _dot(x.astype(MXU_DTYPE), tri)


def _neg_softplus(z):
    u = jnp.exp2(jnp.abs(z) * (-1.0 / math.log(2.0)))
    return -jnp.maximum(z, 0.0) - jnp.log(1.0 + u)


def _walk_blocks(step, st, n, descending, group=2):
    done = 0
    size = group
    while size >= 1:
        def trip(t, s, size=size, done=done):
            js = [done + size * t + g for g in range(size)]
            return step([n - 1 - j for j in js] if descending else js, s)

        trips = (n - done) // size
        st = lax.fori_loop(0, trips, trip, st)
        done = done + size * trips
        size //= 2
    return st


def _chains(js):
    return [(h, t) for t in range(len(js)) for h in range(2)]


def _rowsum(x):
    return jnp.sum(x, axis=1, keepdims=True)


def _attn_consts():
    row = lax.broadcasted_iota(jnp.int32, (BQ, BK), 0)
    col = lax.broadcasted_iota(jnp.int32, (BQ, BK), 1)
    trow = lax.broadcasted_iota(jnp.int32, (BK, BK), 0)
    tcol = lax.broadcasted_iota(jnp.int32, (BK, BK), 1)
    lane = lax.broadcasted_iota(jnp.int32, (BQ, LANES), 1)
    klane = lax.broadcasted_iota(jnp.int32, (BK, LANES), 1)
    return row, col, trow, tcol, lane, klane


assert BQ == BK


def _diag_step(jd, descending, diag_mask, group=4):
    below = list(range(group)) if descending else list(reversed(range(group)))
    js = [jnp.maximum(jd - o, 0) for o in below]
    masks = [diag_mask if o == 0 else jd - o >= 0 for o in below]
    return js, masks, jnp.maximum(jd - (group - 1), 0)


def _key_slice(j):
    return pl.ds(pl.multiple_of(j * BK, BK), BK)


def _sb_forward(proj, s_len, exchange):
    nq = s_len // BQ
    assert s_len // BK <= HD
    qc, kc, vc = O_SBQ // LANES, O_SBK // LANES, O_SBV // LANES
    n_ex = len(exchange.arrays)

    def body(q_ref, k_ref, v_ref, *rest):
        x_refs, (o_ref, tails_ref) = rest[:n_ex], rest[n_ex:n_ex + 2]
        out_refs, sems = rest[n_ex + 2:2 * n_ex + 2], rest[2 * n_ex + 2:]
        p = pl.program_id(0)
        i = pl.program_id(1)
        jd = i

        @pl.when((p == 0) & (i == 0))
        def _():
            exchange.start(x_refs, out_refs, sems)

        row, col, trow, tcol, lane, klane = _attn_consts()
        tri = (trow >= tcol).astype(MXU_DTYPE)
        q = q_ref[...] * 0.125
        qh = [jnp.where(lane < HD, q, 0.0).astype(MXU_DTYPE), jnp.where(lane >= HD, q, 0.0).astype(MXU_DTYPE)]

        km = [klane < HD, klane >= HD]

        def step(js, st, masks):
            carry, acc, tail = st
            chains = _chains(js)
            kj = [k_ref[_key_slice(j), :].astype(MXU_DTYPE) for j in js]
            vj = [v_ref[_key_slice(j), :].astype(MXU_DTYPE) for j in js]
            z = {(h, t): _dot_nt(qh[h], kj[t]) for h, t in chains}
            run = list(carry)
            suf, carry_in = {}, {}
            for h, t in chains:
                lom = _neg_softplus(z[h, t])
                if masks:
                    lom = jnp.where(masks[t], lom, 0.0)
                suf[h, t] = _running_sum(lom, tri)
                carry_in[h, t] = run[h]
                run[h] = run[h] + _rowsum(lom)
            for h, t in chains:
                a = jnp.exp(z[h, t] + suf[h, t] + carry_in[h, t])
                if masks:
                    a = jnp.where(masks[t], a, 0.0)
                acc = acc + _dot(a.astype(MXU_DTYPE), jnp.where(km[h], vj[t], 0))
                tail_lane = js[t] if not masks or masks[t].ndim else jnp.where(masks[t], js[t], -LANES)
                tail = jnp.where(lane == h * HD + tail_lane, carry_in[h, t], tail)
            return tuple(run), acc, tail

        zero = jnp.zeros((BQ, LANES), F32)
        diag_js, masks, left = _diag_step(jd, True, col < row, group=2)
        st = step(diag_js, ((jnp.zeros((BQ, 1), F32),) * 2, zero, zero), masks)
        st = _walk_blocks(lambda js, s: step(js, s, None), st, left, True, group=8)
        o_ref[...] = st[1]
        tails_ref[...] = st[2]

        @pl.when((p == HEADS // 2 - 1) & (i == nq - 1))
        def _():
            exchange.wait(x_refs, out_refs, sems)

    blk = pl.BlockSpec((BQ, LANES), lambda p, i: (i, p))
    out = jax.ShapeDtypeStruct((s_len, SB_W), F32)
    res = pl.pallas_call(
        body, name="sb_fwd", grid=(HEADS // 2, nq),
        in_specs=[pl.BlockSpec((BQ, LANES), lambda p, i: (i, qc + p)),
                  pl.BlockSpec((s_len, LANES), lambda p, i: (0, kc + p)),
                  pl.BlockSpec((s_len, LANES), lambda p, i: (0, vc + p))] + exchange.in_specs,
        out_specs=[blk, blk] + exchange.out_specs, out_shape=[out, out] + exchange.out_shape,
        scratch_shapes=exchange.scratch,
        compiler_params=_cparams("arbitrary", "arbitrary"),
    )(proj, proj, proj, *exchange.arrays)
    return res[:2], res[2:]


def _sb_backward(proj, tails, do, s_len, exchange):
    nq = s_len // BQ
    qc, kc, vc = O_SBQ // LANES, O_SBK // LANES, O_SBV // LANES
    n_ex = len(exchange.arrays)

    def body(q_ref, k_ref, v_ref, tails_ref, do_ref, *rest):
        x_refs, (dq_ref, dk_ref, dv_ref) = rest[:n_ex], rest[n_ex:n_ex + 3]
        out_refs, sems = rest[n_ex + 3:2 * n_ex + 3], rest[2 * n_ex + 3:]
        p = pl.program_id(0)
        i = pl.program_id(1)
        jd = i

        @pl.when((p == 0) & (i == 0))
        def _():
            exchange.start(x_refs, out_refs, sems)

        @pl.when(i == 0)
        def _():
            dk_ref[...] = jnp.zeros(dk_ref.shape, F32)
            dv_ref[...] = jnp.zeros(dv_ref.shape, F32)

        row, col, trow, tcol, lane, klane = _attn_consts()
        tri = (trow >= tcol).astype(MXU_DTYPE)
        tri_p = (trow <= tcol).astype(MXU_DTYPE)
        q = q_ref[...] * 0.125
        tails_blk = tails_ref[...]
        do_blk = do_ref[...]
        hm = [lane < HD, lane >= HD]
        km = [klane < HD, klane >= HD]
        qh = [jnp.where(m, q, 0.0).astype(MXU_DTYPE) for m in hm]
        doh = [jnp.where(m, do_blk, 0.0).astype(MXU_DTYPE) for m in hm]

        def step(js, st, masks):
            before, dq = st
            chains = _chains(js)
            kj = [k_ref[_key_slice(j), :].astype(MXU_DTYPE) for j in js]
            vj = [v_ref[_key_slice(j), :].astype(MXU_DTYPE) for j in js]
            z = {(h, t): _dot_nt(qh[h], kj[t]) for h, t in chains}
            da = {(h, t): _dot_nt(doh[h], vj[t]) for h, t in chains}
            suf, sig = {}, {}
            for h, t in chains:
                lom = _neg_softplus(z[h, t])
                if masks:
                    lom = jnp.where(masks[t], lom, 0.0)
                suf[h, t] = _running_sum(lom, tri)
                sig[h, t] = jnp.exp(z[h, t] + lom)
            run = list(before)
            dl, pre, before_in = {}, {}, {}
            dk_add, dv_add = [None] * len(js), [None] * len(js)
            for h, t in chains:
                tail = _rowsum(jnp.where(lane == h * HD + js[t], tails_blk, 0.0))
                a = jnp.exp(z[h, t] + suf[h, t] + tail)
                if masks:
                    a = jnp.where(masks[t], a, 0.0)
                dl[h, t] = da[h, t] * a
                pre[h, t] = _dot(dl[h, t].astype(MXU_DTYPE), tri_p)
                dv_h = _dot_tn(a.astype(MXU_DTYPE), doh[h])
                dv_add[t] = dv_h if dv_add[t] is None else dv_add[t] + dv_h
                before_in[h, t] = run[h]
                run[h] = run[h] + _rowsum(dl[h, t])
            for h, t in chains:
                upto = before_in[h, t] + pre[h, t]
                dz = dl[h, t] - sig[h, t] * upto
                if masks:
                    dz = jnp.where(masks[t], dz, 0.0)
                dzb = dz.astype(MXU_DTYPE)
                dq = dq + _dot(dzb, jnp.where(km[h], kj[t], 0))
                dk_h = _dot_tn(dzb, qh[h])
                dk_add[t] = dk_h if dk_add[t] is None else dk_add[t] + dk_h
            for t, j in enumerate(js):
                dk_ref[_key_slice(j), :] += dk_add[t]
                dv_ref[_key_slice(j), :] += dv_add[t]
            return tuple(run), dq

        st = ((jnp.zeros((BQ, 1), F32),) * 2, jnp.zeros((BQ, LANES), F32))
        diag_js, masks, left = _diag_step(jd, False, col < row, group=2)
        st = _walk_blocks(lambda js, s: step(js, s, None), st, left, False, group=4)
        st = step(diag_js, st, masks)
        dq_ref[...] = st[1] * 0.125

        @pl.when((p == HEADS // 2 - 1) & (i == nq - 1))
        def _():
            exchange.wait(x_refs, out_refs, sems)

    blk = pl.BlockSpec((BQ, LANES), lambda p, i: (i, p))
    full = pl.BlockSpec((s_len, LANES), lambda p, i: (0, p))
    out = jax.ShapeDtypeStruct((s_len, SB_W), F32)
    res = pl.pallas_call(
        body, name="sb_bwd", grid=(HEADS // 2, nq),
        in_specs=[pl.BlockSpec((BQ, LANES), lambda p, i: (i, qc + p)),
                  pl.BlockSpec((s_len, LANES), lambda p, i: (0, kc + p)),
                  pl.BlockSpec((s_len, LANES), lambda p, i: (0, vc + p)),
                  blk, blk] + exchange.in_specs,
        out_specs=[blk, full, full] + exchange.out_specs, out_shape=[out, out, out] + exchange.out_shape,
        scratch_shapes=exchange.scratch,
        compiler_params=_cparams("arbitrary", "arbitrary"),
    )(proj, proj, proj, tails, do, *exchange.arrays)
    return res[:3], res[3:]


def _pair_mask(rows, h):
    lane = lax.broadcasted_iota(jnp.int32, (rows, 2 * LANES), 1)
    rot = lane - LANES
    return (((lane < LANES) & (lane // HD == h))
            | ((lane >= LANES) & (rot < 2 * ROPE) & ((rot // (ROPE // 2)) % 2 == h)))


def _mla_forward(q_cat, k_cat, kv, s_len):
    nq = s_len // BQ
    scale = 1.0 / math.sqrt(QK_DIM)

    def body(q_ref, k_ref, v_ref, o_ref, lse_ref):
        i = pl.program_id(1)
        jd = i
        row, col, trow, tcol, lane, klane = _attn_consts()
        q = q_ref[...]
        hm = [lane < HD, lane >= HD]
        km = [klane < HD, klane >= HD]
        qh = [jnp.where(_pair_mask(BQ, h), q, 0) for h in range(2)]

        def step(js, st, masks):
            m_run, l_run, acc = st
            chains = _chains(js)
            kj = [k_ref[_key_slice(j), :] for j in js]
            vj = [v_ref[_key_slice(j), :].astype(MXU_DTYPE) for j in js]
            s = {}
            for h, t in chains:
                s[h, t] = _dot_nt(qh[h], kj[t]) * scale
                if masks:
                    s[h, t] = jnp.where(masks[t], s[h, t], -jnp.inf)
            m_new, alpha, l_new = [], [], []
            for h in range(2):
                top = m_run[h]
                for t in range(len(js)):
                    top = jnp.maximum(top, jnp.max(s[h, t], axis=1, keepdims=True))
                m_new.append(top)
                alpha.append(jnp.exp(m_run[h] - top))
                l_new.append(alpha[h] * l_run[h])
            add = None
            for h, t in chains:
                pr = jnp.exp(s[h, t] - m_new[h])
                l_new[h] = l_new[h] + _rowsum(pr)
                part = _dot(pr.astype(MXU_DTYPE), jnp.where(km[h], vj[t], 0))
                add = part if add is None else add + part
            acc = jnp.where(hm[0], alpha[0], alpha[1]) * acc + add
            return tuple(m_new), tuple(l_new), acc

        st = ((jnp.full((BQ, 1), -1e30, F32),) * 2, (jnp.zeros((BQ, 1), F32),) * 2, jnp.zeros((BQ, LANES), F32))
        diag_js, masks, left = _diag_step(jd, True, col <= row)
        st = step(diag_js, st, masks)
        m_run, l_run, acc = _walk_blocks(lambda js, s: step(js, s, None), st, left, True, group=8)
        o_ref[...] = acc / jnp.where(hm[0], l_run[0], l_run[1])
        lse_ref[...] = jnp.where(hm[0], m_run[0] + jnp.log(l_run[0]), m_run[1] + jnp.log(l_run[1]))

    blk = pl.BlockSpec((BQ, LANES), lambda p, i: (i, p))
    out = jax.ShapeDtypeStruct((s_len, MLA_W), F32)
    return pl.pallas_call(
        body, name="mla_fwd", grid=(HEADS // 2, nq),
        in_specs=[pl.BlockSpec((BQ, 2 * LANES), lambda p, i: (i, p)),
                  pl.BlockSpec((s_len, 2 * LANES), lambda p, i: (0, p)),
                  pl.BlockSpec((s_len, LANES), lambda p, i: (0, MLA_W // LANES + p))],
        out_specs=[blk, blk], out_shape=[out, out],
        compiler_params=_cparams("parallel", "parallel"),
    )(q_cat, k_cat, kv)


def _mla_backward(q_cat, k_cat, kv, o, lse, do, s_len):
    nq = s_len // BQ
    scale = 1.0 / math.sqrt(QK_DIM)

    def body(q_ref, k_ref, v_ref, o_ref, lse_ref, do_ref, dq_ref, dk_ref, dv_ref):
        i = pl.program_id(1)

        @pl.when(i == 0)
        def _():
            dk_ref[...] = jnp.zeros(dk_ref.shape, F32)
            dv_ref[...] = jnp.zeros(dv_ref.shape, F32)

        jd = i
        row, col, trow, tcol, lane, klane = _attn_consts()
        q = q_ref[...]
        o_blk = o_ref[...]
        do_blk = do_ref[...]
        lse_blk = lse_ref[...]
        hm = [lane < HD, lane >= HD]
        kpm = [_pair_mask(BK, h) for h in range(2)]
        qh = [jnp.where(_pair_mask(BQ, h), q, 0) for h in range(2)]
        doh_f = [jnp.where(m, do_blk, 0.0) for m in hm]
        doh = [d.astype(MXU_DTYPE) for d in doh_f]
        delta = [jnp.sum(d * o_blk, axis=1, keepdims=True) for d in doh_f]
        lse_h = [jnp.sum(jnp.where(lane == h * HD, lse_blk, 0.0), axis=1, keepdims=True) for h in range(2)]

        def step(js, st, masks):
            dq = st
            chains = _chains(js)
            kj = [k_ref[_key_slice(j), :] for j in js]
            vj = [v_ref[_key_slice(j), :].astype(MXU_DTYPE) for j in js]
            s = {(h, t): _dot_nt(qh[h], kj[t]) for h, t in chains}
            dp = {(h, t): _dot_nt(doh[h], vj[t]) for h, t in chains}
            adds = [[None] * len(js) for _ in range(2)]

            def accumulate(slot, t, part):
                adds[slot][t] = part if adds[slot][t] is None else adds[slot][t] + part

            for h, t in chains:
                pr = jnp.exp(s[h, t] * scale - lse_h[h])
                if masks:
                    pr = jnp.where(masks[t], pr, 0.0)
                dsb = (pr * (dp[h, t] - delta[h]) * scale).astype(MXU_DTYPE)
                dq = dq + _dot(dsb, jnp.where(kpm[h], kj[t], 0))
                accumulate(0, t, _dot_tn(dsb, qh[h]))
                accumulate(1, t, _dot_tn(pr.astype(MXU_DTYPE), doh[h]))
            for t, j in enumerate(js):
                dk_ref[_key_slice(j), :] += adds[0][t]
                dv_ref[_key_slice(j), :] += adds[1][t]
            return dq

        diag_js, masks, left = _diag_step(jd, True, col <= row)
        st = step(diag_js, jnp.zeros((BQ, 2 * LANES), F32), masks)
        dq_ref[...] = _walk_blocks(lambda js, s: step(js, s, None), st, left, True, group=8)

    blk = pl.BlockSpec((BQ, LANES), lambda p, i: (i, p))
    full = pl.BlockSpec((s_len, LANES), lambda p, i: (0, p))
    out = jax.ShapeDtypeStruct((s_len, MLA_W), F32)
    out_cat = jax.ShapeDtypeStruct((s_len, 2 * MLA_W), F32)
    return pl.pallas_call(
        body, name="mla_bwd", grid=(HEADS // 2, nq),
        in_specs=[pl.BlockSpec((BQ, 2 * LANES), lambda p, i: (i, p)),
                  pl.BlockSpec((s_len, 2 * LANES), lambda p, i: (0, p)),
                  pl.BlockSpec((s_len, LANES), lambda p, i: (0, MLA_W // LANES + p)),
                  blk, blk, blk],
        out_specs=[pl.BlockSpec((BQ, 2 * LANES), lambda p, i: (i, p)),
                   pl.BlockSpec((s_len, 2 * LANES), lambda p, i: (0, p)), full],
        out_shape=[out_cat, out_cat, out],
        compiler_params=_cparams("arbitrary", "arbitrary"),
    )(q_cat, k_cat, kv, o, lse, do)


def _mesh_pos():
    return lax.axis_index("x"), lax.axis_index("y"), lax.axis_index("c")


def _dev_index(px, py, pc):
    return 4 * px + 2 * py + pc


def _all_gather(block, name):
    return _all_gather_parts([block], name)[0]


def _all_gather_parts(blocks, name):
    n = len(blocks)

    def body(*refs):
        x_refs, out_refs = refs[:n], refs[n:2 * n]
        send_sems, recv_sems, local_sems = refs[2 * n:]
        x, y, c = _mesh_pos()
        me, sibling = (x, y, c), (x, y, 1 - c)
        chips = [(1 - x, y), (x, 1 - y), (1 - x, 1 - y)]

        def copy(a, k, blockpos, to, src=None):
            slot = out_refs[a].at[_dev_index(*blockpos)]
            return pltpu.make_async_remote_copy(
                src_ref=slot if src is None else src, dst_ref=slot,
                send_sem=send_sems.at[7 * a + k], recv_sem=recv_sems.at[7 * a + k],
                device_id=to, device_id_type=pl.DeviceIdType.MESH)

        mine = [pltpu.make_async_copy(x_refs[a], out_refs[a].at[_dev_index(*me)], local_sems.at[a]) for a in range(n)]
        for cp in mine:
            cp.start()
        first = []
        for a in range(n):
            first.append(copy(a, 0, me, sibling, src=x_refs[a]))
            first += [copy(a, 1 + j, me, (*chip, c), src=x_refs[a]) for j, chip in enumerate(chips)]
        for cp in first:
            cp.start()
        passed = []
        for j, chip in enumerate(chips):
            for a in range(n):
                copy(a, 1 + j, (*chip, c), me).wait_recv()
                passed.append(copy(a, 4 + j, (*chip, c), sibling))
                passed[-1].start()
        for a in range(n):
            copy(a, 0, sibling, me).wait_recv()
            for j, chip in enumerate(chips):
                copy(a, 4 + j, (*chip, 1 - c), me).wait_recv()
        for cp in first + passed:
            cp.wait_send()
        for cp in mine:
            cp.wait()

    return pl.pallas_call(
        body, name=name,
        out_shape=[jax.ShapeDtypeStruct((N_DEV,) + b.shape, b.dtype) for b in blocks],
        in_specs=[pl.BlockSpec(memory_space=pl.ANY)] * n, out_specs=[pl.BlockSpec(memory_space=pl.ANY)] * n,
        scratch_shapes=[pltpu.SemaphoreType.DMA((7 * n,)), pltpu.SemaphoreType.DMA((7 * n,)),
                        pltpu.SemaphoreType.DMA((n,))],
    )(*blocks)


class _Exchange:
    def __init__(self, arrays):
        self.arrays = list(arrays)
        n = len(self.arrays)
        self.in_specs = [pl.BlockSpec(memory_space=pl.ANY)] * n
        self.out_specs = [pl.BlockSpec(memory_space=pl.ANY)] * n
        self.out_shape = [jax.ShapeDtypeStruct(a.shape, a.dtype) for a in self.arrays]
        self.scratch = [pltpu.SemaphoreType.DMA((7 * n,)), pltpu.SemaphoreType.DMA((7 * n,)),
                        pltpu.SemaphoreType.DMA((n,))]

    def _copies(self, x_refs, out_refs, sems, with_arrivals):
        send_sems, recv_sems, local_sems = sems
        x, y, c = _mesh_pos()
        me = _dev_index(x, y, c)
        flips = [(fx, fy, fc) for fx in (0, 1) for fy in (0, 1) for fc in (0, 1)][1:]
        peers = [(1 - x if fx else x, 1 - y if fy else y, 1 - c if fc else c) for fx, fy, fc in flips]
        mine, sends, arrivals = [], [], []
        for a in range(len(self.arrays)):
            mine.append(pltpu.make_async_copy(x_refs[a].at[me], out_refs[a].at[me], local_sems.at[a]))
            for k, peer in enumerate(peers):
                sends.append(pltpu.make_async_remote_copy(
                    src_ref=x_refs[a].at[_dev_index(*peer)], dst_ref=out_refs[a].at[me],
                    send_sem=send_sems.at[7 * a + k], recv_sem=recv_sems.at[7 * a + k],
                    device_id=peer, device_id_type=pl.DeviceIdType.MESH))
                if not with_arrivals:
                    continue
                arrivals.append(pltpu.make_async_remote_copy(
                    src_ref=x_refs[a].at[me], dst_ref=out_refs[a].at[_dev_index(*peer)],
                    send_sem=send_sems.at[7 * a + k], recv_sem=recv_sems.at[7 * a + k],
                    device_id=peer, device_id_type=pl.DeviceIdType.MESH))
        return mine, sends, arrivals

    def start(self, x_refs, out_refs, sems):
        mine, sends, _ = self._copies(x_refs, out_refs, sems, False)
        for cp in mine + sends:
            cp.start()

    def wait(self, x_refs, out_refs, sems):
        mine, sends, arrivals = self._copies(x_refs, out_refs, sems, True)
        for cp in arrivals:
            cp.wait_recv()
        for cp in sends:
            cp.wait_send()
        for cp in mine:
            cp.wait()


def _sum_blocks(parts, name):
    n, r, c = parts.shape
    row_tiles = [t for t in range(16, min(r, 2048) + 1, 16) if r % t == 0]
    if row_tiles:
        tr, tc = max(row_tiles), c
    else:
        tr, tc = r, 2 * LANES
    assert c % tc == 0

    def body(p_ref, o_ref):
        acc = p_ref[0].astype(F32)
        for s in range(1, n):
            acc = acc + p_ref[s].astype(F32)
        o_ref[...] = acc

    return pl.pallas_call(
        body, name=name, grid=(r // tr, c // tc),
        in_specs=[pl.BlockSpec((n, tr, tc), lambda i, j: (0, i, j))],
        out_specs=pl.BlockSpec((tr, tc), lambda i, j: (i, j)),
        out_shape=jax.ShapeDtypeStruct((r, c), F32),
        compiler_params=_cparams("parallel", "parallel"),
    )(parts)


def _sigmoid(x):
    return 1.0 / (1.0 + jnp.exp(-x))


def _silu(x):
    return x * _sigmoid(x)


def _silu_grad(x):
    s = _sigmoid(x)
    return s * (1.0 + x * (1.0 - s))


def _colsum(x):
    return jnp.sum(x, axis=0, keepdims=True)


def _rms(x):
    return lax.rsqrt(jnp.mean(x * x, axis=-1, keepdims=True) + EPS)


def _rms_bwd(xn, r, dxn):
    return r * (dxn - xn * jnp.mean(dxn * xn, axis=-1, keepdims=True))


def _adamw(w, g, m, v):
    m = ADAM_B1 * m + (1.0 - ADAM_B1) * g
    v = ADAM_B2 * v + (1.0 - ADAM_B2) * jnp.square(g)
    m_hat = m / (1.0 - ADAM_B1 ** ADAM_STEP)
    v_hat = v / (1.0 - ADAM_B2 ** ADAM_STEP)
    delta = -ADAM_LR * (m_hat / (jnp.sqrt(v_hat) + ADAM_EPS) + ADAM_WD * w)
    return delta, m, v


def _adamw_call(w, g, m, v, name):
    r, c = w.shape
    if r % 256 == 0:
        tr, tc = 256, c
    elif r * c <= 256 * 1024 or c % (2 * LANES):
        tr, tc = r, c
    else:
        tr, tc = r, 2 * LANES

    def body(w_ref, g_ref, m_ref, v_ref, d_out, m_out, v_out):
        d_out[...], m_out[...], v_out[...] = _adamw(w_ref[...], g_ref[...], m_ref[...], v_ref[...])

    spec = pl.BlockSpec((tr, tc), lambda i, j: (i, j))
    return pl.pallas_call(
        body, name=name, grid=(r // tr, c // tc), in_specs=[spec] * 4, out_specs=[spec] * 3,
        out_shape=[jax.ShapeDtypeStruct((r, c), F32)] * 3, compiler_params=_cparams("parallel", "parallel"),
    )(w, g, m, v)


def _uq_to_kernel_layout(w):
    lead = w.shape[:-1]
    t = w.reshape(lead + (HEADS, QK_DIM))
    return jnp.concatenate([t[..., :NOPE].reshape(lead + (HEADS * NOPE,)),
                            t[..., NOPE:NOPE + ROPE // 2].reshape(lead + (LANES,)),
                            t[..., NOPE + ROPE // 2:].reshape(lead + (LANES,))], axis=-1)


def _uq_from_kernel_layout(w):
    lead = w.shape[:-1]
    nope = w[..., :HEADS * NOPE].reshape(lead + (HEADS, NOPE))
    r1 = w[..., HEADS * NOPE:HEADS * NOPE + LANES].reshape(lead + (HEADS, ROPE // 2))
    r2 = w[..., HEADS * NOPE + LANES:].reshape(lead + (HEADS, ROPE // 2))
    return jnp.concatenate([nope, r1, r2], axis=-1).reshape(lead + (HEADS * QK_DIM,))


def _ukv_to_kernel_layout(w):
    lead = w.shape[:-1]
    t = w.reshape(lead + (HEADS, NOPE + HD))
    return jnp.concatenate([t[..., :NOPE].reshape(lead + (HEADS * NOPE,)),
                            t[..., NOPE:].reshape(lead + (HEADS * HD,))], axis=-1)


def _ukv_from_kernel_layout(w):
    lead = w.shape[:-1]
    kn = w[..., :HEADS * NOPE].reshape(lead + (HEADS, NOPE))
    vv = w[..., HEADS * NOPE:].reshape(lead + (HEADS, HD))
    return jnp.concatenate([kn, vv], axis=-1).reshape(lead + (HEADS * (NOPE + HD),))


def _w_in_t_to_kernel_layout(wt):
    sb = wt[0:2048]
    c_q = wt[2048:2432]
    c_kv = wt[2432:2688]
    k_rot = wt[2688:2720]
    mla_z = wt[2720:3232]
    gates = wt[3232:5280]
    zeros = jnp.zeros((LANES, wt.shape[1]), wt.dtype)
    k1 = jnp.tile(k_rot[:ROPE // 2], (HEADS, 1))
    k2 = jnp.tile(k_rot[ROPE // 2:], (HEADS, 1))
    return jnp.concatenate([gates, sb, mla_z, c_q, zeros, c_kv, k1, k2], axis=0)


def _w_in_t_from_kernel_layout(gt, g_rot):
    return jnp.concatenate([gt[O_SBQ:O_SBQ + 2048], gt[O_CQ:O_CQ + Q_RANK], gt[O_CKV:O_CKV + KV_RANK],
                            g_rot.astype(gt.dtype), gt[O_MLAZ:O_MLAZ + MLA_W], gt[O_GA:O_GA + 2 * D]], axis=0)


def kernel(x, c, positions, w_ada, b_ada, norm_gain, w_in, q_norm_gain, w_uq, kv_norm_gain, w_ukv, w_branch_a, w_branch_b, w_out, final_norm_gain, loss_target, m_w_ada, m_b_ada, m_norm_gain, m_w_in, m_q_norm_gain, m_w_uq, m_kv_norm_gain, m_w_ukv, m_w_branch_a, m_w_branch_b, m_w_out, m_final_norm_gain, v_w_ada, v_b_ada, v_norm_gain, v_w_in, v_q_norm_gain, v_w_uq, v_kv_norm_gain, v_w_ukv, v_w_branch_a, v_w_branch_b, v_w_out, v_final_norm_gain):
    s_len = x.shape[1]
    me = _dev_index(*_mesh_pos())
    x2d = x[0]
    tgt = loss_target[0]

    w_in_t = w_in[0].T.astype(BF16)
    big = [w_uq[0], w_ukv[0], w_branch_a[0], w_branch_b[0], w_out[0]]
    big_sizes = [int(w.size) for w in big]
    packed = jnp.concatenate([w.astype(BF16).reshape(-1, LANES) for w in big], axis=0)
    g_in_t, c_all = _all_gather_parts([w_in_t, c.reshape(8, LANES)], "gather_w_in")
    c_all = c_all.reshape(N_DEV, D)
    w_in_kt = _w_in_t_to_kernel_layout(g_in_t.reshape(N_DEV * w_in_t.shape[0], D))

    mod_cols = _mm(c_all, w_ada[0], name="ada_mod")
    mod_all = _all_gather(mod_cols, "gather_mod")
    mod = lax.dynamic_index_in_dim(mod_all, me, axis=1, keepdims=False).reshape(1, 3 * D)
    mod_shift, mod_scale, mod_gate = mod[:, :D], mod[:, D:2 * D], mod[:, 2 * D:]
    b_shift, b_scale, b_gate = b_ada[:, :D], b_ada[:, D:2 * D], b_ada[:, 2 * D:]
    g1 = norm_gain
    gq, gkv = q_norm_gain, kv_norm_gain
    gf = final_norm_gain.reshape(1, D)

    def f_h(x_, g1_, ms, bs, msc, bsc):
        xn = x_ * _rms(x_)
        return (xn * g1_ * (1.0 + (msc + bsc)) + (ms + bs),), ()

    (h,) = _rowwise(f_h, [x2d], [g1, mod_shift, b_shift, mod_scale, b_scale], [(D, BF16)], name="ada_norm")
    proj = _mm(h, w_in_kt, tb=True, name="proj_in", tiles=(min(s_len, 1024), IN_PAD // 2, D))

    (o_a, sb_tails), (gathered,) = _sb_forward(
        proj, s_len, _Exchange([jnp.broadcast_to(packed[None], (N_DEV,) + packed.shape)]))
    offs = [0]
    for n in big_sizes:
        offs.append(offs[-1] + n // LANES)

    def unpack(t, shape):
        return gathered[:, offs[t]:offs[t + 1], :].reshape((N_DEV,) + shape)

    def cols(t, shape):
        return unpack(t, shape).transpose(1, 0, 2).reshape(shape[0], N_DEV * shape[1])

    w_uq_k = _uq_to_kernel_layout(cols(0, big[0].shape))
    w_ukv_k = _ukv_to_kernel_layout(cols(1, big[1].shape))
    w_a_f = cols(2, big[2].shape)
    w_b_f = cols(3, big[3].shape)
    w_out_f = unpack(4, big[4].shape).reshape(D, D)

    def f_lat(cq, ckv, gq_, gkv_):
        return (cq * _rms(cq) * gq_, ckv * _rms(ckv) * gkv_), ()

    cq_n, ckv_n = _rowwise(f_lat, [(proj, O_CQ // Q_RANK, Q_RANK), (proj, O_CKV // KV_RANK, KV_RANK)], [gq, gkv],
                           [(Q_RANK, BF16), (KV_RANK, BF16)], name="latent_norm")
    q_mla = _mm(cq_n, w_uq_k, name="q_up")
    kv = _mm(ckv_n, w_ukv_k, name="kv_up")

    inv_freq = ROPE_BASE ** (-jnp.arange(0, ROPE, 2, dtype=F32) / ROPE)
    inv_freq_t = jnp.tile(inv_freq, HEADS).reshape(1, LANES)
    pos_col = positions.reshape(s_len, 1).astype(F32)

    pairs = HEADS // 2

    def f_rope(pos, qn, q1, q2, kn, k1, k2, freq):
        ang = pos * freq
        cs, sn = jnp.cos(ang), jnp.sin(ang)
        q1r, q2r = q1 * cs - q2 * sn, q1 * sn + q2 * cs
        k1r, k2r = k1 * cs - k2 * sn, k1 * sn + k2 * cs
        lane = lax.broadcasted_iota(jnp.int32, q1.shape, 1)
        first, second = lane < ROPE, (lane >= ROPE) & (lane < 2 * ROPE)
        k_rot = jnp.where(first, k1r, jnp.where(second, k2r, 0.0))
        q_parts, k_parts = [], []
        for p in range(pairs):
            q_rot = jnp.where(first, pltpu.roll(q1r, (LANES - ROPE * p) % LANES, 1),
                              jnp.where(second, pltpu.roll(q2r, (LANES + ROPE - ROPE * p) % LANES, 1), 0.0))
            q_parts += [qn[:, LANES * p:LANES * (p + 1)], q_rot]
            k_parts += [kn[:, LANES * p:LANES * (p + 1)], k_rot]
        return (jnp.concatenate(q_parts, axis=1), jnp.concatenate(k_parts, axis=1), cs, sn), ()

    q_cat, k_cat, cos_t, sin_t = _rowwise(
        f_rope, [pos_col, (q_mla, 0, MLA_W), (q_mla, 4, LANES), (q_mla, 5, LANES), (kv, 0, MLA_W),
                 (proj, O_KROT // LANES, LANES), (proj, O_KROT // LANES + 1, LANES)], [inv_freq_t],
        [(2 * MLA_W, BF16), (2 * MLA_W, BF16), (LANES, F32), (LANES, F32)], name="rope")

    o_b, lse = _mla_forward(q_cat, k_cat, kv, s_len)

    def f_gate(oa, za, ob, zb):
        return (oa * _silu(za), ob * _silu(zb)), ()

    ya_in, yb_in = _rowwise(f_gate, [o_a, (proj, O_SBZ // SB_W, SB_W), o_b, (proj, O_MLAZ // MLA_W, MLA_W)], [],
                            [(SB_W, BF16), (MLA_W, BF16)], name="branch_gate")
    y_a = _mm(ya_in, w_a_f, name="branch_a")

    def f_merge(yb, ga, gb, ya):
        return (yb, _sigmoid(ga) * ya + _sigmoid(gb) * yb), ()

    y_b, merged = _mm(yb_in, w_b_f, name="branch_b_merge", tiles=(min(s_len, 512), D, MLA_W),
                      epilogue=(f_merge, [(proj, O_GA // D, D), (proj, O_GB // D, D), y_a], [], [(D, F32), (D, BF16)], []))

    def f_loss(out_, x_, t_, mg, bg, gf_):
        gate = mg + bg
        x2 = x_ + gate * out_
        r2 = _rms(x2)
        xn2 = x2 * r2
        err = xn2 * gf_ - t_
        loss = jnp.full((1, LANES), 0.5 / D, F32) * jnp.sum(err * err)
        dy = err * (1.0 / D)
        dx2 = _rms_bwd(xn2, r2, dy * gf_)
        return (dx2, dx2 * gate), (loss, _colsum(dy * xn2), _colsum(dx2 * out_))

    dx2, d_out, loss_part, d_gf, d_gate = _mm(
        merged, w_out_f, name="out_proj_loss", tiles=(min(s_len, 512), D, D),
        epilogue=(f_loss, [x2d, tgt], [mod_gate, b_gate, gf], [(D, F32), (D, BF16)], [LANES, D, D]))

    dw_out = _mm(merged, d_out, ta=True, name="dw_out")

    def f_dmerge(dm, ga, gb, ya, yb):
        sa, sb = _sigmoid(ga), _sigmoid(gb)
        return (dm * sa, dm * sb, dm * ya * sa * (1.0 - sa), dm * yb * sb * (1.0 - sb)), ()

    d_ya, d_yb, d_ga, d_gb = _mm(
        d_out, w_out_f, tb=True, name="d_merge", tiles=(min(s_len, 256), D, D),
        epilogue=(f_dmerge, [(proj, O_GA // D, D), (proj, O_GB // D, D), y_a, y_b], [], [(D, BF16)] * 4, []))
    dw_a = _mm(ya_in, d_ya, ta=True, name="dw_branch_a")
    dw_b = _mm(yb_in, d_yb, ta=True, name="dw_branch_b")

    def f_dgate(d_in, o_, z_):
        return (d_in * _silu(z_), d_in * o_ * _silu_grad(z_)), ()

    d_oa, d_sbz = _mm(d_ya, w_a_f, tb=True, name="d_branch_a",
                      epilogue=(f_dgate, [o_a, (proj, O_SBZ // SB_W, SB_W)], [], [(SB_W, F32), (SB_W, BF16)], []))
    d_ob, d_mlaz = _mm(d_yb, w_b_f, tb=True, name="d_branch_b",
                       epilogue=(f_dgate, [o_b, (proj, O_MLAZ // MLA_W, MLA_W)], [], [(MLA_W, F32), (MLA_W, BF16)], []))

    dq_cat, dk_cat, dv_b = _mla_backward(q_cat, k_cat, kv, o_b, lse, d_ob, s_len)

    def f_drope(dq, dk, dv_, cs, sn):
        lane = lax.broadcasted_iota(jnp.int32, cs.shape, 1)
        first, second = lane < ROPE, (lane >= ROPE) & (lane < 2 * ROPE)
        dq1 = dq2 = dk1 = dk2 = None
        for p in range(pairs):
            q_rot = dq[:, LANES * (2 * p + 1):LANES * (2 * p + 2)]
            k_rot = dk[:, LANES * (2 * p + 1):LANES * (2 * p + 2)]
            parts = (pltpu.roll(jnp.where(first, q_rot, 0.0), (ROPE * p) % LANES, 1),
                     pltpu.roll(jnp.where(second, q_rot, 0.0), (LANES - ROPE + ROPE * p) % LANES, 1),
                     jnp.where(first, k_rot, 0.0), jnp.where(second, k_rot, 0.0))
            if p == 0:
                dq1, dq2, dk1, dk2 = parts
            else:
                dq1, dq2, dk1, dk2 = dq1 + parts[0], dq2 + parts[1], dk1 + parts[2], dk2 + parts[3]
        dqn_ = [dq[:, 2 * LANES * p:2 * LANES * p + LANES] for p in range(pairs)]
        dkn_ = [dk[:, 2 * LANES * p:2 * LANES * p + LANES] for p in range(pairs)]
        return (jnp.concatenate(dqn_ + [dq1 * cs + dq2 * sn, dq2 * cs - dq1 * sn], axis=1),
                jnp.concatenate(dkn_ + [dv_], axis=1),
                jnp.concatenate([dk1 * cs + dk2 * sn, dk2 * cs - dk1 * sn], axis=1)), ()

    dq_k, dkv_k, d_krot = _rowwise(f_drope, [dq_cat, dk_cat, dv_b, cos_t, sin_t], [],
                                   [(HEADS * QK_DIM, BF16), (2 * MLA_W, BF16), (2 * LANES, BF16)], name="d_rope")
    dw_uq_k = _mm(cq_n, dq_k, ta=True, name="dw_uq")
    dw_ukv_k = _mm(ckv_n, dkv_k, ta=True, name="dw_ukv")

    def f_dlat(d_normed, latent, gain):
        r = _rms(latent)
        normed = latent * r
        return (_rms_bwd(normed, r, d_normed * gain),), (_colsum(d_normed * normed),)

    d_cq, d_gq = _mm(dq_k, w_uq_k, tb=True, name="d_cq_norm",
                     epilogue=(f_dlat, [(proj, O_CQ // Q_RANK, Q_RANK)], [gq], [(Q_RANK, BF16)], [Q_RANK]))
    d_ckv, d_gkv = _mm(dkv_k, w_ukv_k, tb=True, name="d_ckv_norm",
                       epilogue=(f_dlat, [(proj, O_CKV // KV_RANK, KV_RANK)], [gkv], [(KV_RANK, BF16)], [KV_RANK]))

    def col_blocks(g):
        kdim, n8 = g.shape
        return g.astype(BF16).reshape(kdim, N_DEV, n8 // N_DEV).transpose(1, 0, 2).reshape(N_DEV, -1, LANES)

    g_blocks = jnp.concatenate([col_blocks(_uq_from_kernel_layout(dw_uq_k)), col_blocks(_ukv_from_kernel_layout(dw_ukv_k)),
                                col_blocks(dw_a), col_blocks(dw_b), dw_out.astype(BF16).reshape(N_DEV, -1, LANES)], axis=1)
    (d_sbq, d_sbk, d_sbv), (g_recv,) = _sb_backward(proj, sb_tails, d_oa, s_len, _Exchange([g_blocks]))

    d_proj = jnp.concatenate([d_ga, d_gb, d_sbq.astype(BF16), d_sbk.astype(BF16), d_sbv.astype(BF16), d_sbz, d_mlaz,
                              d_cq, jnp.zeros((s_len, LANES), BF16), d_ckv, d_krot], axis=1)
    dw_in_kt = _mm(d_proj, h, ta=True, out_dtype=BF16, name="dw_in", tiles=(512, D, s_len))

    def krot_body(t_ref, o_ref):
        half = ROPE // 2
        for part in range(2):
            acc = t_ref[part * LANES:part * LANES + half, :].astype(F32)
            for hh in range(1, HEADS):
                acc = acc + t_ref[part * LANES + hh * half:part * LANES + (hh + 1) * half, :].astype(F32)
            o_ref[part * half:(part + 1) * half, :] = acc

    dw_krot = pl.pallas_call(krot_body, name="dw_krot_sum", out_shape=jax.ShapeDtypeStruct((ROPE, D), F32))(
        dw_in_kt[O_KROT:O_KROT + 2 * LANES])

    g_in_blocks = _w_in_t_from_kernel_layout(dw_in_kt, dw_krot).reshape(N_DEV, -1, D)
    def f_dx(dh_, x_, dx2_, g1_, msc, bsc):
        r = _rms(x_)
        xn = x_ * r
        dn1 = dh_ * (1.0 + (msc + bsc))
        return ((dx2_ + _rms_bwd(xn, r, dn1 * g1_),),
                (_colsum(dh_), _colsum(dh_ * (xn * g1_)), _colsum(dn1 * xn)))

    (grad_x2d, d_shift, d_scale, d_g1), (g_in_recv,) = _mm(
        d_proj, w_in_kt, name="d_h", tiles=(min(s_len, 512), D, 512), exchange=_Exchange([g_in_blocks]),
        epilogue=(f_dx, [x2d, dx2], [g1, mod_scale, b_scale], [(D, F32)], [D, D, D]))

    g_in_sum_t = _sum_blocks(g_in_recv, "sum_grads_w_in")
    g_sum = _sum_blocks(g_recv, "sum_grads")
    g_big = [g_sum[offs[t]:offs[t + 1]].reshape(big[t].shape) for t in range(5)]

    small = jnp.concatenate([d_shift, d_scale, d_gate, d_g1, d_gq, d_gkv, d_gf, loss_part], axis=1)
    n_small = small.shape[1]
    pad = (-n_small) % (8 * LANES)
    small = jnp.pad(small, ((0, 0), (0, pad))).reshape(-1, LANES)
    small_all = _all_gather(small, "gather_small")
    small_sum = _sum_blocks(small_all, "sum_small").reshape(1, -1)
    g_b_ada = small_sum[:, :3 * D]
    g_g1 = small_sum[:, 3 * D:4 * D]
    g_gq = small_sum[:, 4 * D:4 * D + Q_RANK]
    g_gkv = small_sum[:, 4 * D + Q_RANK:4 * D + Q_RANK + KV_RANK]
    g_gf = small_sum[:, 4 * D + Q_RANK + KV_RANK:4 * D + Q_RANK + KV_RANK + D]

    dmod_all = small_all.reshape(N_DEV, -1)[:, :3 * D]
    dmod_cols = lax.dynamic_slice_in_dim(dmod_all, me * (3 * D // N_DEV), 3 * D // N_DEV, axis=1)
    g_w_ada = _mm(c_all, dmod_cols, ta=True, name="dw_ada")

    loss = small_sum[0, n_small - LANES]

    names = ["w_ada", "b_ada", "norm_gain", "w_in", "q_norm_gain", "w_uq", "kv_norm_gain", "w_ukv",
             "w_branch_a", "w_branch_b", "w_out", "final_norm_gain"]
    weights = dict(w_ada=w_ada, b_ada=b_ada, norm_gain=norm_gain, w_in=w_in, q_norm_gain=q_norm_gain, w_uq=w_uq,
                   kv_norm_gain=kv_norm_gain, w_ukv=w_ukv, w_branch_a=w_branch_a, w_branch_b=w_branch_b, w_out=w_out,
                   final_norm_gain=final_norm_gain)
    moms = dict(w_ada=m_w_ada, b_ada=m_b_ada, norm_gain=m_norm_gain, w_in=m_w_in, q_norm_gain=m_q_norm_gain,
                w_uq=m_w_uq, kv_norm_gain=m_kv_norm_gain, w_ukv=m_w_ukv, w_branch_a=m_w_branch_a,
                w_branch_b=m_w_branch_b, w_out=m_w_out, final_norm_gain=m_final_norm_gain)
    vels = dict(w_ada=v_w_ada, b_ada=v_b_ada, norm_gain=v_norm_gain, w_in=v_w_in, q_norm_gain=v_q_norm_gain,
                w_uq=v_w_uq, kv_norm_gain=v_kv_norm_gain, w_ukv=v_w_ukv, w_branch_a=v_w_branch_a,
                w_branch_b=v_w_branch_b, w_out=v_w_out, final_norm_gain=v_final_norm_gain)
    grads2d = dict(w_ada=g_w_ada, b_ada=g_b_ada, norm_gain=g_g1, w_in=g_in_sum_t, q_norm_gain=g_gq, w_uq=g_big[0],
                   kv_norm_gain=g_gkv, w_ukv=g_big[1], w_branch_a=g_big[2], w_branch_b=g_big[3], w_out=g_big[4],
                   final_norm_gain=g_gf)

    grads, deltas, new_m, new_v = [], [], [], []
    for n in names:
        w = weights[n]
        if n == "w_in":
            to2d = lambda t: t[0].T if t.ndim == 3 else t
            back = lambda t: t.T[None]
        else:
            shape2d = grads2d[n].shape
            to2d = lambda t, s=shape2d: t.reshape(s)
            back = lambda t, s=w.shape: t.reshape(s)
        g2d = to2d(grads2d[n])
        d_, m_, v_ = _adamw_call(to2d(w), g2d, to2d(moms[n]), to2d(vels[n]), "adamw_" + n)
        grads.append(back(g2d))
        deltas.append(back(d_))
        new_m.append(back(m_))
        new_v.append(back(v_))

    return (loss, grad_x2d.reshape(x.shape), *grads, *deltas, *new_m, *new_v)
```

```python
import functools
import math

import jax
import jax.numpy as jnp
from jax import lax
from jax.experimental import pallas as pl
from jax.experimental.pallas import tpu as pltpu

F32 = jnp.float32
BF16 = jnp.bfloat16
MXU_DTYPE = jnp.bfloat16

N_DEV = 8
D = 1024
HEADS = 8
HD = 64
SB_W = 512
MLA_W = 512
Q_RANK = 384
KV_RANK = 256
ROPE = 32
NOPE = 64
QK_DIM = NOPE + ROPE
EPS = 1e-6
ROPE_BASE = 10000.0

ADAM_LR = 0.001
ADAM_B1 = 0.9
ADAM_B2 = 0.999
ADAM_EPS = 1e-08
ADAM_WD = 0.01
ADAM_STEP = 10

LANES = 128
VMEM_LIMIT = 48 * 1024 * 1024

O_GA, O_GB = 0, 1024
O_SBQ, O_SBK, O_SBV, O_SBZ = 2048, 2560, 3072, 3584
O_MLAZ = 4096
O_CQ = 4608
O_CKV = 5120
O_KROT = 5376
IN_PAD = 5632

BQ = 256
BK = 256


def _cparams(*sem):
    return pltpu.CompilerParams(dimension_semantics=sem, vmem_limit_bytes=VMEM_LIMIT)


def _tile_of(n, cap=512):
    if n <= cap:
        return n
    for t in (1024, 768, 512, 384, 256, 128):
        if t <= cap and n % t == 0:
            return t
    raise ValueError(n)


def _rowwise(fn, rows, vecs, outs, reds=(), *, name, tile=512):
    norm = []
    for r in rows:
        if isinstance(r, tuple):
            arr, cb, w = r[:3]
            ro = r[3] if len(r) > 3 else 0
        else:
            arr, cb, w, ro = r, 0, r.shape[1], 0
        norm.append((arr, cb, w, ro))
    s_len = norm[0][0].shape[0]
    tile = min(tile, s_len)
    assert s_len % tile == 0
    n_row, n_vec, n_out, n_red = len(norm), len(vecs), len(outs), len(reds)

    def body(*refs):
        step = pl.program_id(0)
        row_refs = refs[:n_row]
        vec_refs = refs[n_row:n_row + n_vec]
        out_refs = refs[n_row + n_vec:n_row + n_vec + n_out]
        red_refs = refs[n_row + n_vec + n_out:]
        row_res, red_res = fn(*[r[...] for r in row_refs], *[v[...] for v in vec_refs])
        for o, val in zip(out_refs, row_res):
            o[...] = val.astype(o.dtype)
        if n_red:
            @pl.when(step == 0)
            def _():
                for r in red_refs:
                    r[...] = jnp.zeros(r.shape, r.dtype)
            for r, val in zip(red_refs, red_res):
                r[...] += val

    in_specs = []
    for arr, cb, w, ro in norm:
        in_specs.append(pl.BlockSpec((tile, w), functools.partial(lambda i, cb, rb: (i + rb, cb), cb=cb, rb=ro // tile)))
        assert ro % tile == 0
    for v in vecs:
        in_specs.append(pl.BlockSpec(v.shape, lambda i: (0, 0)))
    out_shape = [jax.ShapeDtypeStruct((s_len, w), dt) for w, dt in outs]
    out_specs = [pl.BlockSpec((tile, w), lambda i: (i, 0)) for w, _ in outs]
    out_shape += [jax.ShapeDtypeStruct((1, w), F32) for w in reds]
    out_specs += [pl.BlockSpec((1, w), lambda i: (0, 0)) for w in reds]
    res = pl.pallas_call(
        body, name=name, grid=(s_len // tile,), in_specs=in_specs, out_specs=out_specs, out_shape=out_shape,
        compiler_params=_cparams("arbitrary" if n_red else "parallel"),
    )(*[a for a, _, _, _ in norm], *vecs)
    return res


def _mm(a, b, *, ta=False, tb=False, out_dtype=F32, name, exchange=None, tiles=None, epilogue=None):
    m, k = (a.shape[1], a.shape[0]) if ta else a.shape
    n = b.shape[0] if tb else b.shape[1]
    assert (b.shape[1] if tb else b.shape[0]) == k
    tm, tn, tk = tiles or (_tile_of(m, 1024), _tile_of(n, 1024 if n <= 1024 else 512), _tile_of(k, 1024))
    assert m % tm == 0 and n % tn == 0 and k % tk == 0
    ni, nj, nk = m // tm, n // tn, k // tk
    dims = (((0 if ta else 1,), (1 if tb else 0,)), ((), ()))
    n_ex = len(exchange.arrays) if exchange else 0
    fn, rows, vecs, outs, reds = epilogue or (None, (), (), (), ())
    rows = [r if isinstance(r, tuple) else (r, 0, r.shape[1]) for r in rows]
    assert not epilogue or tn == n
    n_res = len(outs) + len(reds) if epilogue else 1

    def body(*refs):
        a_ref, b_ref = refs[:2]
        row_refs, refs = refs[2:2 + len(rows)], refs[2 + len(rows):]
        vec_refs, refs = refs[:len(vecs)], refs[len(vecs):]
        x_refs, refs = refs[:n_ex], refs[n_ex:]
        res_refs, refs = refs[:n_res], refs[n_res:]
        out_refs, acc_ref, sems = refs[:n_ex], refs[n_ex], refs[n_ex + 1:]
        i, j, kk = pl.program_id(0), pl.program_id(1), pl.program_id(2)
        first = (i == 0) & (j == 0) & (kk == 0)

        if exchange:
            @pl.when(first)
            def _():
                exchange.start(x_refs, out_refs, sems)

        if reds:
            @pl.when(first)
            def _():
                for r in res_refs[len(outs):]:
                    r[...] = jnp.zeros(r.shape, r.dtype)

        @pl.when(kk == 0)
        def _():
            acc_ref[...] = jnp.zeros(acc_ref.shape, F32)

        acc_ref[...] += lax.dot_general(a_ref[...].astype(MXU_DTYPE), b_ref[...].astype(MXU_DTYPE), dims,
                                        preferred_element_type=F32)

        @pl.when(kk == nk - 1)
        def _():
            if not epilogue:
                res_refs[0][...] = acc_ref[...].astype(res_refs[0].dtype)
                return
            row_res, red_res = fn(acc_ref[...], *[r[...] for r in row_refs], *[v[...] for v in vec_refs])
            for o, val in zip(res_refs, row_res):
                o[...] = val.astype(o.dtype)
            for r, val in zip(res_refs[len(outs):], red_res):
                r[...] += val

        if exchange:
            @pl.when((i == ni - 1) & (j == nj - 1) & (kk == nk - 1))
            def _():
                exchange.wait(x_refs, out_refs, sems)

    a_spec = pl.BlockSpec((tk, tm), lambda i, j, kk: (kk, i)) if ta else pl.BlockSpec((tm, tk), lambda i, j, kk: (i, kk))
    b_spec = pl.BlockSpec((tn, tk), lambda i, j, kk: (j, kk)) if tb else pl.BlockSpec((tk, tn), lambda i, j, kk: (kk, j))
    in_specs = [a_spec, b_spec]
    in_specs += [pl.BlockSpec((tm, w), functools.partial(lambda i, j, kk, cb: (i, cb), cb=cb)) for _, cb, w in rows]
    in_specs += [pl.BlockSpec(v.shape, lambda i, j, kk: (0, 0)) for v in vecs]
    if epilogue:
        res_specs = [pl.BlockSpec((tm, w), lambda i, j, kk: (i, 0)) for w, _ in outs]
        res_specs += [pl.BlockSpec((1, w), lambda i, j, kk: (0, 0)) for w in reds]
        res_shape = [jax.ShapeDtypeStruct((m, w), dt) for w, dt in outs] + [jax.ShapeDtypeStruct((1, w), F32) for w in reds]
    else:
        res_specs = [pl.BlockSpec((tm, tn), lambda i, j, kk: (i, j))]
        res_shape = [jax.ShapeDtypeStruct((m, n), out_dtype)]
    ordered = bool(exchange or reds)
    res = pl.pallas_call(
        body, name=name, grid=(ni, nj, nk),
        in_specs=in_specs + (exchange.in_specs if exchange else []),
        out_specs=res_specs + (exchange.out_specs if exchange else []),
        out_shape=res_shape + (exchange.out_shape if exchange else []),
        scratch_shapes=[pltpu.VMEM((tm, tn), F32)] + (exchange.scratch if exchange else []),
        compiler_params=_cparams(*(("arbitrary",) * 3 if ordered else ("parallel", "parallel", "arbitrary"))),
    )(a, b, *[r[0] for r in rows], *vecs, *(exchange.arrays if exchange else []))
    main = res[:n_res] if epilogue else res[0]
    return (main, res[n_res:]) if exchange else main


_NT = (((1,), (1,)), ((), ()))
_TN = (((0,), (0,)), ((), ()))


def _dot(a, b):
    return jnp.dot(a, b, preferred_element_type=F32)


def _dot_nt(a, b):
    return lax.dot_general(a, b, _NT, preferred_element_type=F32)


def _dot_tn(a, b):
    return lax.dot_general(a, b, _TN, preferred_element_type=F32)


def _running_sum(x, tri):
    return _dot(x.astype(MXU_DTYPE), tri)


def _neg_softplus(z):
    u = jnp.exp2(jnp.abs(z) * (-1.0 / math.log(2.0)))
    return -jnp.maximum(z, 0.0) - jnp.log(1.0 + u)


def _walk_blocks(step, st, n, descending, group=2):
    done = 0
    size = group
    while size >= 1:
        def trip(t, s, size=size, done=done):
            js = [done + size * t + g for g in range(size)]
            return step([n - 1 - j for j in js] if descending else js, s)

        trips = (n - done) // size
        st = lax.fori_loop(0, trips, trip, st)
        done = done + size * trips
        size //= 2
    return st


def _chains(js):
    return [(h, t) for t in range(len(js)) for h in range(2)]


def _rowsum(x):
    return jnp.sum(x, axis=1, keepdims=True)


def _attn_consts():
    row = lax.broadcasted_iota(jnp.int32, (BQ, BK), 0)
    col = lax.broadcasted_iota(jnp.int32, (BQ, BK), 1)
    trow = lax.broadcasted_iota(jnp.int32, (BK, BK), 0)
    tcol = lax.broadcasted_iota(jnp.int32, (BK, BK), 1)
    lane = lax.broadcasted_iota(jnp.int32, (BQ, LANES), 1)
    klane = lax.broadcasted_iota(jnp.int32, (BK, LANES), 1)
    return row, col, trow, tcol, lane, klane


assert BQ == BK


def _diag_step(jd, descending, diag_mask, group=4):
    below = list(range(group)) if descending else list(reversed(range(group)))
    js = [jnp.maximum(jd - o, 0) for o in below]
    masks = [diag_mask if o == 0 else jd - o >= 0 for o in below]
    return js, masks, jnp.maximum(jd - (group - 1), 0)


def _key_slice(j):
    return pl.ds(pl.multiple_of(j * BK, BK), BK)


def _sb_forward(proj, s_len, exchange):
    nq = s_len // BQ
    assert s_len // BK <= HD
    qc, kc, vc = O_SBQ // LANES, O_SBK // LANES, O_SBV // LANES
    n_ex = len(exchange.arrays)

    def body(q_ref, k_ref, v_ref, *rest):
        x_refs, (o_ref, tails_ref) = rest[:n_ex], rest[n_ex:n_ex + 2]
        out_refs, sems = rest[n_ex + 2:2 * n_ex + 2], rest[2 * n_ex + 2:]
        p = pl.program_id(0)
        i = pl.program_id(1)
        jd = i

        @pl.when((p == 0) & (i == 0))
        def _():
            exchange.start(x_refs, out_refs, sems)

        row, col, trow, tcol, lane, klane = _attn_consts()
        tri = (trow >= tcol).astype(MXU_DTYPE)
        q = q_ref[...] * 0.125
        qh = [jnp.where(lane < HD, q, 0.0).astype(MXU_DTYPE), jnp.where(lane >= HD, q, 0.0).astype(MXU_DTYPE)]

        km = [klane < HD, klane >= HD]

        def step(js, st, masks):
            carry, acc, tail = st
            chains = _chains(js)
            kj = [k_ref[_key_slice(j), :].astype(MXU_DTYPE) for j in js]
            vj = [v_ref[_key_slice(j), :].astype(MXU_DTYPE) for j in js]
            z = {(h, t): _dot_nt(qh[h], kj[t]) for h, t in chains}
            run = list(carry)
            suf, carry_in = {}, {}
            for h, t in chains:
                lom = _neg_softplus(z[h, t])
                if masks:
                    lom = jnp.where(masks[t], lom, 0.0)
                suf[h, t] = _running_sum(lom, tri)
                carry_in[h, t] = run[h]
                run[h] = run[h] + _rowsum(lom)
            for h, t in chains:
                a = jnp.exp(z[h, t] + suf[h, t] + carry_in[h, t])
                if masks:
                    a = jnp.where(masks[t], a, 0.0)
                acc = acc + _dot(a.astype(MXU_DTYPE), jnp.where(km[h], vj[t], 0))
                tail_lane = js[t] if not masks or masks[t].ndim else jnp.where(masks[t], js[t], -LANES)
                tail = jnp.where(lane == h * HD + tail_lane, carry_in[h, t], tail)
            return tuple(run), acc, tail

        zero = jnp.zeros((BQ, LANES), F32)
        diag_js, masks, left = _diag_step(jd, True, col < row, group=2)
        st = step(diag_js, ((jnp.zeros((BQ, 1), F32),) * 2, zero, zero), masks)
        st = _walk_blocks(lambda js, s: step(js, s, None), st, left, True, group=8)
        o_ref[...] = st[1]
        tails_ref[...] = st[2]

        @pl.when((p == HEADS // 2 - 1) & (i == nq - 1))
        def _():
            exchange.wait(x_refs, out_refs, sems)

    blk = pl.BlockSpec((BQ, LANES), lambda p, i: (i, p))
    out = jax.ShapeDtypeStruct((s_len, SB_W), F32)
    res = pl.pallas_call(
        body, name="sb_fwd", grid=(HEADS // 2, nq),
        in_specs=[pl.BlockSpec((BQ, LANES), lambda p, i: (i, qc + p)),
                  pl.BlockSpec((s_len, LANES), lambda p, i: (0, kc + p)),
                  pl.BlockSpec((s_len, LANES), lambda p, i: (0, vc + p))] + exchange.in_specs,
        out_specs=[blk, blk] + exchange.out_specs, out_shape=[out, out] + exchange.out_shape,
        scratch_shapes=exchange.scratch,
        compiler_params=_cparams("arbitrary", "arbitrary"),
    )(proj, proj, proj, *exchange.arrays)
    return res[:2], res[2:]


def _sb_backward(proj, tails, do, s_len, exchange):
    nq = s_len // BQ
    qc, kc, vc = O_SBQ // LANES, O_SBK // LANES, O_SBV // LANES
    n_ex = len(exchange.arrays)

    def body(q_ref, k_ref, v_ref, tails_ref, do_ref, *rest):
        x_refs, (dq_ref, dk_ref, dv_ref) = rest[:n_ex], rest[n_ex:n_ex + 3]
        out_refs, sems = rest[n_ex + 3:2 * n_ex + 3], rest[2 * n_ex + 3:]
        p = pl.program_id(0)
        i = pl.program_id(1)
        jd = i

        @pl.when((p == 0) & (i == 0))
        def _():
            exchange.start(x_refs, out_refs, sems)

        @pl.when(i == 0)
        def _():
            dk_ref[...] = jnp.zeros(dk_ref.shape, F32)
            dv_ref[...] = jnp.zeros(dv_ref.shape, F32)

        row, col, trow, tcol, lane, klane = _attn_consts()
        tri = (trow >= tcol).astype(MXU_DTYPE)
        tri_p = (trow <= tcol).astype(MXU_DTYPE)
        q = q_ref[...] * 0.125
        tails_blk = tails_ref[...]
        do_blk = do_ref[...]
        hm = [lane < HD, lane >= HD]
        km = [klane < HD, klane >= HD]
        qh = [jnp.where(m, q, 0.0).astype(MXU_DTYPE) for m in hm]
        doh = [jnp.where(m, do_blk, 0.0).astype(MXU_DTYPE) for m in hm]

        def step(js, st, masks):
            before, dq = st
            chains = _chains(js)
            kj = [k_ref[_key_slice(j), :].astype(MXU_DTYPE) for j in js]
            vj = [v_ref[_key_slice(j), :].astype(MXU_DTYPE) for j in js]
            z = {(h, t): _dot_nt(qh[h], kj[t]) for h, t in chains}
            da = {(h, t): _dot_nt(doh[h], vj[t]) for h, t in chains}
            suf, sig = {}, {}
            for h, t in chains:
                lom = _neg_softplus(z[h, t])
                if masks:
                    lom = jnp.where(masks[t], lom, 0.0)
                suf[h, t] = _running_sum(lom, tri)
                sig[h, t] = jnp.exp(z[h, t] + lom)
            run = list(before)
            dl, pre, before_in = {}, {}, {}
            dk_add, dv_add = [None] * len(js), [None] * len(js)
            for h, t in chains:
                tail = _rowsum(jnp.where(lane == h * HD + js[t], tails_blk, 0.0))
                a = jnp.exp(z[h, t] + suf[h, t] + tail)
                if masks:
                    a = jnp.where(masks[t], a, 0.0)
                dl[h, t] = da[h, t] * a
                pre[h, t] = _dot(dl[h, t].astype(MXU_DTYPE), tri_p)
                dv_h = _dot_tn(a.astype(MXU_DTYPE), doh[h])
                dv_add[t] = dv_h if dv_add[t] is None else dv_add[t] + dv_h
                before_in[h, t] = run[h]
                run[h] = run[h] + _rowsum(dl[h, t])
            for h, t in chains:
                upto = before_in[h, t] + pre[h, t]
                dz = dl[h, t] - sig[h, t] * upto
                if masks:
                    dz = jnp.where(masks[t], dz, 0.0)
                dzb = dz.astype(MXU_DTYPE)
                dq = dq + _dot(dzb, jnp.where(km[h], kj[t], 0))
                dk_h = _dot_tn(dzb, qh[h])
                dk_add[t] = dk_h if dk_add[t] is None else dk_add[t] + dk_h
            for t, j in enumerate(js):
                dk_ref[_key_slice(j), :] += dk_add[t]
                dv_ref[_key_slice(j), :] += dv_add[t]
            return tuple(run), dq

        st = ((jnp.zeros((BQ, 1), F32),) * 2, jnp.zeros((BQ, LANES), F32))
        diag_js, masks, left = _diag_step(jd, False, col < row, group=2)
        st = _walk_blocks(lambda js, s: step(js, s, None), st, left, False, group=4)
        st = step(diag_js, st, masks)
        dq_ref[...] = st[1] * 0.125

        @pl.when((p == HEADS // 2 - 1) & (i == nq - 1))
        def _():
            exchange.wait(x_refs, out_refs, sems)

    blk = pl.BlockSpec((BQ, LANES), lambda p, i: (i, p))
    full = pl.BlockSpec((s_len, LANES), lambda p, i: (0, p))
    out = jax.ShapeDtypeStruct((s_len, SB_W), F32)
    res = pl.pallas_call(
        body, name="sb_bwd", grid=(HEADS // 2, nq),
        in_specs=[pl.BlockSpec((BQ, LANES), lambda p, i: (i, qc + p)),
                  pl.BlockSpec((s_len, LANES), lambda p, i: (0, kc + p)),
                  pl.BlockSpec((s_len, LANES), lambda p, i: (0, vc + p)),
                  blk, blk] + exchange.in_specs,
        out_specs=[blk, full, full] + exchange.out_specs, out_shape=[out, out, out] + exchange.out_shape,
        scratch_shapes=exchange.scratch,
        compiler_params=_cparams("arbitrary", "arbitrary"),
    )(proj, proj, proj, tails, do, *exchange.arrays)
    return res[:3], res[3:]


def _pair_mask(rows, h):
    lane = lax.broadcasted_iota(jnp.int32, (rows, 2 * LANES), 1)
    rot = lane - LANES
    return (((lane < LANES) & (lane // HD == h))
            | ((lane >= LANES) & (rot < 2 * ROPE) & ((rot // (ROPE // 2)) % 2 == h)))


def _mla_forward(q_cat, k_cat, kv, s_len):
    nq = s_len // BQ
    scale = 1.0 / math.sqrt(QK_DIM)

    def body(q_ref, k_ref, v_ref, o_ref, lse_ref):
        i = pl.program_id(1)
        jd = i
        row, col, trow, tcol, lane, klane = _attn_consts()
        q = q_ref[...]
        hm = [lane < HD, lane >= HD]
        km = [klane < HD, klane >= HD]
        qh = [jnp.where(_pair_mask(BQ, h), q, 0) for h in range(2)]

        def step(js, st, masks):
            m_run, l_run, acc = st
            chains = _chains(js)
            kj = [k_ref[_key_slice(j), :] for j in js]
            vj = [v_ref[_key_slice(j), :].astype(MXU_DTYPE) for j in js]
            s = {}
            for h, t in chains:
                s[h, t] = _dot_nt(qh[h], kj[t]) * scale
                if masks:
                    s[h, t] = jnp.where(masks[t], s[h, t], -jnp.inf)
            m_new, alpha, l_new = [], [], []
            for h in range(2):
                top = m_run[h]
                for t in range(len(js)):
                    top = jnp.maximum(top, jnp.max(s[h, t], axis=1, keepdims=True))
                m_new.append(top)
                alpha.append(jnp.exp(m_run[h] - top))
                l_new.append(alpha[h] * l_run[h])
            add = None
            for h, t in chains:
                pr = jnp.exp(s[h, t] - m_new[h])
                l_new[h] = l_new[h] + _rowsum(pr)
                part = _dot(pr.astype(MXU_DTYPE), jnp.where(km[h], vj[t], 0))
                add = part if add is None else add + part
            acc = jnp.where(hm[0], alpha[0], alpha[1]) * acc + add
            return tuple(m_new), tuple(l_new), acc

        st = ((jnp.full((BQ, 1), -1e30, F32),) * 2, (jnp.zeros((BQ, 1), F32),) * 2, jnp.zeros((BQ, LANES), F32))
        diag_js, masks, left = _diag_step(jd, True, col <= row, group=2)
        st = step(diag_js, st, masks)
        m_run, l_run, acc = _walk_blocks(lambda js, s: step(js, s, None), st, left, True, group=8)
        o_ref[...] = acc / jnp.where(hm[0], l_run[0], l_run[1])
        lse_ref[...] = jnp.where(hm[0], m_run[0] + jnp.log(l_run[0]), m_run[1] + jnp.log(l_run[1]))

    blk = pl.BlockSpec((BQ, LANES), lambda p, i: (i, p))
    out = jax.ShapeDtypeStruct((s_len, MLA_W), F32)
    return pl.pallas_call(
        body, name="mla_fwd", grid=(HEADS // 2, nq),
        in_specs=[pl.BlockSpec((BQ, 2 * LANES), lambda p, i: (i, p)),
                  pl.BlockSpec((s_len, 2 * LANES), lambda p, i: (0, p)),
                  pl.BlockSpec((s_len, LANES), lambda p, i: (0, MLA_W // LANES + p))],
        out_specs=[blk, blk], out_shape=[out, out],
        compiler_params=_cparams("parallel", "parallel"),
    )(q_cat, k_cat, kv)


def _mla_backward(q_cat, k_cat, kv, o, lse, do, s_len):
    nq = s_len // BQ
    scale = 1.0 / math.sqrt(QK_DIM)

    def body(q_ref, k_ref, v_ref, o_ref, lse_ref, do_ref, dq_ref, dk_ref, dv_ref):
        i = pl.program_id(1)

        @pl.when(i == 0)
        def _():
            dk_ref[...] = jnp.zeros(dk_ref.shape, F32)
            dv_ref[...] = jnp.zeros(dv_ref.shape, F32)

        jd = i
        row, col, trow, tcol, lane, klane = _attn_consts()
        q = q_ref[...]
        o_blk = o_ref[...]
        do_blk = do_ref[...]
        lse_blk = lse_ref[...]
        hm = [lane < HD, lane >= HD]
        kpm = [_pair_mask(BK, h) for h in range(2)]
        qh = [jnp.where(_pair_mask(BQ, h), q, 0) for h in range(2)]
        doh_f = [jnp.where(m, do_blk, 0.0) for m in hm]
        doh = [d.astype(MXU_DTYPE) for d in doh_f]
        delta = [jnp.sum(d * o_blk, axis=1, keepdims=True) for d in doh_f]
        lse_h = [jnp.sum(jnp.where(lane == h * HD, lse_blk, 0.0), axis=1, keepdims=True) for h in range(2)]

        def step(js, st, masks):
            dq = st
            chains = _chains(js)
            kj = [k_ref[_key_slice(j), :] for j in js]
            vj = [v_ref[_key_slice(j), :].astype(MXU_DTYPE) for j in js]
            s = {(h, t): _dot_nt(qh[h], kj[t]) for h, t in chains}
            dp = {(h, t): _dot_nt(doh[h], vj[t]) for h, t in chains}
            adds = [[None] * len(js) for _ in range(2)]

            def accumulate(slot, t, part):
                adds[slot][t] = part if adds[slot][t] is None else adds[slot][t] + part

            for h, t in chains:
                pr = jnp.exp(s[h, t] * scale - lse_h[h])
                if masks:
                    pr = jnp.where(masks[t], pr, 0.0)
                dsb = (pr * (dp[h, t] - delta[h]) * scale).astype(MXU_DTYPE)
                dq = dq + _dot(dsb, jnp.where(kpm[h], kj[t], 0))
                accumulate(0, t, _dot_tn(dsb, qh[h]))
                accumulate(1, t, _dot_tn(pr.astype(MXU_DTYPE), doh[h]))
            for t, j in enumerate(js):
                dk_ref[_key_slice(j), :] += adds[0][t]
                dv_ref[_key_slice(j), :] += adds[1][t]
            return dq

        diag_js, masks, left = _diag_step(jd, True, col <= row, group=2)
        st = step(diag_js, jnp.zeros((BQ, 2 * LANES), F32), masks)
        dq_ref[...] = _walk_blocks(lambda js, s: step(js, s, None), st, left, True, group=8)

    blk = pl.BlockSpec((BQ, LANES), lambda p, i: (i, p))
    full = pl.BlockSpec((s_len, LANES), lambda p, i: (0, p))
    out = jax.ShapeDtypeStruct((s_len, MLA_W), F32)
    out_cat = jax.ShapeDtypeStruct((s_len, 2 * MLA_W), F32)
    return pl.pallas_call(
        body, name="mla_bwd", grid=(HEADS // 2, nq),
        in_specs=[pl.BlockSpec((BQ, 2 * LANES), lambda p, i: (i, p)),
                  pl.BlockSpec((s_len, 2 * LANES), lambda p, i: (0, p)),
                  pl.BlockSpec((s_len, LANES), lambda p, i: (0, MLA_W // LANES + p)),
                  blk, blk, blk],
        out_specs=[pl.BlockSpec((BQ, 2 * LANES), lambda p, i: (i, p)),
                   pl.BlockSpec((s_len, 2 * LANES), lambda p, i: (0, p)), full],
        out_shape=[out_cat, out_cat, out],
        compiler_params=_cparams("arbitrary", "arbitrary"),
    )(q_cat, k_cat, kv, o, lse, do)


def _mesh_pos():
    return lax.axis_index("x"), lax.axis_index("y"), lax.axis_index("c")


def _dev_index(px, py, pc):
    return 4 * px + 2 * py + pc


def _all_gather(block, name):
    return _all_gather_parts([block], name)[0]


def _all_gather_parts(blocks, name):
    n = len(blocks)

    def body(*refs):
        x_refs, out_refs = refs[:n], refs[n:2 * n]
        send_sems, recv_sems, local_sems = refs[2 * n:]
        x, y, c = _mesh_pos()
        me, sibling = (x, y, c), (x, y, 1 - c)
        chips = [(1 - x, y), (x, 1 - y), (1 - x, 1 - y)]

        def copy(a, k, blockpos, to, src=None):
            slot = out_refs[a].at[_dev_index(*blockpos)]
            return pltpu.make_async_remote_copy(
                src_ref=slot if src is None else src, dst_ref=slot,
                send_sem=send_sems.at[7 * a + k], recv_sem=recv_sems.at[7 * a + k],
                device_id=to, device_id_type=pl.DeviceIdType.MESH)

        mine = [pltpu.make_async_copy(x_refs[a], out_refs[a].at[_dev_index(*me)], local_sems.at[a]) for a in range(n)]
        for cp in mine:
            cp.start()
        first = []
        for a in range(n):
            first.append(copy(a, 0, me, sibling, src=x_refs[a]))
            first += [copy(a, 1 + j, me, (*chip, c), src=x_refs[a]) for j, chip in enumerate(chips)]
        for cp in first:
            cp.start()
        passed = []
        for j, chip in enumerate(chips):
            for a in range(n):
                copy(a, 1 + j, (*chip, c), me).wait_recv()
                passed.append(copy(a, 4 + j, (*chip, c), sibling))
                passed[-1].start()
        for a in range(n):
            copy(a, 0, sibling, me).wait_recv()
            for j, chip in enumerate(chips):
                copy(a, 4 + j, (*chip, 1 - c), me).wait_recv()
        for cp in first + passed:
            cp.wait_send()
        for cp in mine:
            cp.wait()

    return pl.pallas_call(
        body, name=name,
        out_shape=[jax.ShapeDtypeStruct((N_DEV,) + b.shape, b.dtype) for b in blocks],
        in_specs=[pl.BlockSpec(memory_space=pl.ANY)] * n, out_specs=[pl.BlockSpec(memory_space=pl.ANY)] * n,
        scratch_shapes=[pltpu.SemaphoreType.DMA((7 * n,)), pltpu.SemaphoreType.DMA((7 * n,)),
                        pltpu.SemaphoreType.DMA((n,))],
    )(*blocks)


class _Exchange:
    def __init__(self, arrays):
        self.arrays = list(arrays)
        n = len(self.arrays)
        self.in_specs = [pl.BlockSpec(memory_space=pl.ANY)] * n
        self.out_specs = [pl.BlockSpec(memory_space=pl.ANY)] * n
        self.out_shape = [jax.ShapeDtypeStruct(a.shape, a.dtype) for a in self.arrays]
        self.scratch = [pltpu.SemaphoreType.DMA((7 * n,)), pltpu.SemaphoreType.DMA((7 * n,)),
                        pltpu.SemaphoreType.DMA((n,))]

    def _copies(self, x_refs, out_refs, sems, with_arrivals):
        send_sems, recv_sems, local_sems = sems
        x, y, c = _mesh_pos()
        me = _dev_index(x, y, c)
        flips = [(fx, fy, fc) for fx in (0, 1) for fy in (0, 1) for fc in (0, 1)][1:]
        peers = [(1 - x if fx else x, 1 - y if fy else y, 1 - c if fc else c) for fx, fy, fc in flips]
        mine, sends, arrivals = [], [], []
        for a in range(len(self.arrays)):
            mine.append(pltpu.make_async_copy(x_refs[a].at[me], out_refs[a].at[me], local_sems.at[a]))
            for k, peer in enumerate(peers):
                sends.append(pltpu.make_async_remote_copy(
                    src_ref=x_refs[a].at[_dev_index(*peer)], dst_ref=out_refs[a].at[me],
                    send_sem=send_sems.at[7 * a + k], recv_sem=recv_sems.at[7 * a + k],
                    device_id=peer, device_id_type=pl.DeviceIdType.MESH))
                if not with_arrivals:
                    continue
                arrivals.append(pltpu.make_async_remote_copy(
                    src_ref=x_refs[a].at[me], dst_ref=out_refs[a].at[_dev_index(*peer)],
                    send_sem=send_sems.at[7 * a + k], recv_sem=recv_sems.at[7 * a + k],
                    device_id=peer, device_id_type=pl.DeviceIdType.MESH))
        return mine, sends, arrivals

    def start(self, x_refs, out_refs, sems):
        mine, sends, _ = self._copies(x_refs, out_refs, sems, False)
        for cp in mine + sends:
            cp.start()

    def wait(self, x_refs, out_refs, sems):
        mine, sends, arrivals = self._copies(x_refs, out_refs, sems, True)
        for cp in arrivals:
            cp.wait_recv()
        for cp in sends:
            cp.wait_send()
        for cp in mine:
            cp.wait()


def _sum_blocks(parts, name):
    n, r, c = parts.shape
    row_tiles = [t for t in range(16, min(r, 2048) + 1, 16) if r % t == 0]
    if row_tiles:
        tr, tc = max(row_tiles), c
    else:
        tr, tc = r, 2 * LANES
    assert c % tc == 0

    def body(p_ref, o_ref):
        acc = p_ref[0].astype(F32)
        for s in range(1, n):
            acc = acc + p_ref[s].astype(F32)
        o_ref[...] = acc

    return pl.pallas_call(
        body, name=name, grid=(r // tr, c // tc),
        in_specs=[pl.BlockSpec((n, tr, tc), lambda i, j: (0, i, j))],
        out_specs=pl.BlockSpec((tr, tc), lambda i, j: (i, j)),
        out_shape=jax.ShapeDtypeStruct((r, c), F32),
        compiler_params=_cparams("parallel", "parallel"),
    )(parts)


def _sigmoid(x):
    return 1.0 / (1.0 + jnp.exp(-x))


def _silu(x):
    return x * _sigmoid(x)


def _silu_grad(x):
    s = _sigmoid(x)
    return s * (1.0 + x * (1.0 - s))


def _colsum(x):
    return jnp.sum(x, axis=0, keepdims=True)


def _rms(x):
    return lax.rsqrt(jnp.mean(x * x, axis=-1, keepdims=True) + EPS)


def _rms_bwd(xn, r, dxn):
    return r * (dxn - xn * jnp.mean(dxn * xn, axis=-1, keepdims=True))


def _adamw(w, g, m, v):
    m = ADAM_B1 * m + (1.0 - ADAM_B1) * g
    v = ADAM_B2 * v + (1.0 - ADAM_B2) * jnp.square(g)
    m_hat = m / (1.0 - ADAM_B1 ** ADAM_STEP)
    v_hat = v / (1.0 - ADAM_B2 ** ADAM_STEP)
    delta = -ADAM_LR * (m_hat / (jnp.sqrt(v_hat) + ADAM_EPS) + ADAM_WD * w)
    return delta, m, v


def _adamw_call(w, g, m, v, name):
    r, c = w.shape
    if r % 256 == 0:
        tr, tc = 256, c
    elif r * c <= 256 * 1024 or c % (2 * LANES):
        tr, tc = r, c
    else:
        tr, tc = r, 2 * LANES

    def body(w_ref, g_ref, m_ref, v_ref, d_out, m_out, v_out):
        d_out[...], m_out[...], v_out[...] = _adamw(w_ref[...], g_ref[...], m_ref[...], v_ref[...])

    spec = pl.BlockSpec((tr, tc), lambda i, j: (i, j))
    return pl.pallas_call(
        body, name=name, grid=(r // tr, c // tc), in_specs=[spec] * 4, out_specs=[spec] * 3,
        out_shape=[jax.ShapeDtypeStruct((r, c), F32)] * 3, compiler_params=_cparams("parallel", "parallel"),
    )(w, g, m, v)


def _uq_to_kernel_layout(w):
    lead = w.shape[:-1]
    t = w.reshape(lead + (HEADS, QK_DIM))
    return jnp.concatenate([t[..., :NOPE].reshape(lead + (HEADS * NOPE,)),
                            t[..., NOPE:NOPE + ROPE // 2].reshape(lead + (LANES,)),
                            t[..., NOPE + ROPE // 2:].reshape(lead + (LANES,))], axis=-1)


def _uq_from_kernel_layout(w):
    lead = w.shape[:-1]
    nope = w[..., :HEADS * NOPE].reshape(lead + (HEADS, NOPE))
    r1 = w[..., HEADS * NOPE:HEADS * NOPE + LANES].reshape(lead + (HEADS, ROPE // 2))
    r2 = w[..., HEADS * NOPE + LANES:].reshape(lead + (HEADS, ROPE // 2))
    return jnp.concatenate([nope, r1, r2], axis=-1).reshape(lead + (HEADS * QK_DIM,))


def _ukv_to_kernel_layout(w):
    lead = w.shape[:-1]
    t = w.reshape(lead + (HEADS, NOPE + HD))
    return jnp.concatenate([t[..., :NOPE].reshape(lead + (HEADS * NOPE,)),
                            t[..., NOPE:].reshape(lead + (HEADS * HD,))], axis=-1)


def _ukv_from_kernel_layout(w):
    lead = w.shape[:-1]
    kn = w[..., :HEADS * NOPE].reshape(lead + (HEADS, NOPE))
    vv = w[..., HEADS * NOPE:].reshape(lead + (HEADS, HD))
    return jnp.concatenate([kn, vv], axis=-1).reshape(lead + (HEADS * (NOPE + HD),))


def _w_in_t_to_kernel_layout(wt):
    sb = wt[0:2048]
    c_q = wt[2048:2432]
    c_kv = wt[2432:2688]
    k_rot = wt[2688:2720]
    mla_z = wt[2720:3232]
    gates = wt[3232:5280]
    zeros = jnp.zeros((LANES, wt.shape[1]), wt.dtype)
    k1 = jnp.tile(k_rot[:ROPE // 2], (HEADS, 1))
    k2 = jnp.tile(k_rot[ROPE // 2:], (HEADS, 1))
    return jnp.concatenate([gates, sb, mla_z, c_q, zeros, c_kv, k1, k2], axis=0)


def _w_in_t_from_kernel_layout(gt, g_rot):
    return jnp.concatenate([gt[O_SBQ:O_SBQ + 2048], gt[O_CQ:O_CQ + Q_RANK], gt[O_CKV:O_CKV + KV_RANK],
                            g_rot.astype(gt.dtype), gt[O_MLAZ:O_MLAZ + MLA_W], gt[O_GA:O_GA + 2 * D]], axis=0)


def kernel(x, c, positions, w_ada, b_ada, norm_gain, w_in, q_norm_gain, w_uq, kv_norm_gain, w_ukv, w_branch_a, w_branch_b, w_out, final_norm_gain, loss_target, m_w_ada, m_b_ada, m_norm_gain, m_w_in, m_q_norm_gain, m_w_uq, m_kv_norm_gain, m_w_ukv, m_w_branch_a, m_w_branch_b, m_w_out, m_final_norm_gain, v_w_ada, v_b_ada, v_norm_gain, v_w_in, v_q_norm_gain, v_w_uq, v_kv_norm_gain, v_w_ukv, v_w_branch_a, v_w_branch_b, v_w_out, v_final_norm_gain):
    s_len = x.shape[1]
    me = _dev_index(*_mesh_pos())
    x2d = x[0]
    tgt = loss_target[0]

    w_in_t = w_in[0].T.astype(BF16)
    big = [w_uq[0], w_ukv[0], w_branch_a[0], w_branch_b[0], w_out[0]]
    big_sizes = [int(w.size) for w in big]
    packed = jnp.concatenate([w.astype(BF16).reshape(-1, LANES) for w in big], axis=0)
    g_in_t, c_all = _all_gather_parts([w_in_t, c.reshape(8, LANES)], "gather_w_in")
    c_all = c_all.reshape(N_DEV, D)
    w_in_kt = _w_in_t_to_kernel_layout(g_in_t.reshape(N_DEV * w_in_t.shape[0], D))

    mod_cols = _mm(c_all, w_ada[0], name="ada_mod")
    mod_all = _all_gather(mod_cols, "gather_mod")
    mod = lax.dynamic_index_in_dim(mod_all, me, axis=1, keepdims=False).reshape(1, 3 * D)
    mod_shift, mod_scale, mod_gate = mod[:, :D], mod[:, D:2 * D], mod[:, 2 * D:]
    b_shift, b_scale, b_gate = b_ada[:, :D], b_ada[:, D:2 * D], b_ada[:, 2 * D:]
    g1 = norm_gain
    gq, gkv = q_norm_gain, kv_norm_gain
    gf = final_norm_gain.reshape(1, D)

    def f_h(x_, g1_, ms, bs, msc, bsc):
        xn = x_ * _rms(x_)
        return (xn * g1_ * (1.0 + (msc + bsc)) + (ms + bs),), ()

    (h,) = _rowwise(f_h, [x2d], [g1, mod_shift, b_shift, mod_scale, b_scale], [(D, BF16)], name="ada_norm")
    proj = _mm(h, w_in_kt, tb=True, name="proj_in", tiles=(min(s_len, 1024), IN_PAD // 2, D))

    (o_a, sb_tails), (gathered,) = _sb_forward(
        proj, s_len, _Exchange([jnp.broadcast_to(packed[None], (N_DEV,) + packed.shape)]))
    offs = [0]
    for n in big_sizes:
        offs.append(offs[-1] + n // LANES)

    def unpack(t, shape):
        return gathered[:, offs[t]:offs[t + 1], :].reshape((N_DEV,) + shape)

    def cols(t, shape):
        return unpack(t, shape).transpose(1, 0, 2).reshape(shape[0], N_DEV * shape[1])

    w_uq_k = _uq_to_kernel_layout(cols(0, big[0].shape))
    w_ukv_k = _ukv_to_kernel_layout(cols(1, big[1].shape))
    w_a_f = cols(2, big[2].shape)
    w_b_f = cols(3, big[3].shape)
    w_out_f = unpack(4, big[4].shape).reshape(D, D)

    def f_lat(cq, ckv, gq_, gkv_):
        return (cq * _rms(cq) * gq_, ckv * _rms(ckv) * gkv_), ()

    cq_n, ckv_n = _rowwise(f_lat, [(proj, O_CQ // Q_RANK, Q_RANK), (proj, O_CKV // KV_RANK, KV_RANK)], [gq, gkv],
                           [(Q_RANK, BF16), (KV_RANK, BF16)], name="latent_norm")
    q_mla = _mm(cq_n, w_uq_k, name="q_up")
    kv = _mm(ckv_n, w_ukv_k, name="kv_up")

    inv_freq = ROPE_BASE ** (-jnp.arange(0, ROPE, 2, dtype=F32) / ROPE)
    inv_freq_t = jnp.tile(inv_freq, HEADS).reshape(1, LANES)
    pos_col = positions.reshape(s_len, 1).astype(F32)

    pairs = HEADS // 2

    def f_rope(pos, qn, q1, q2, kn, k1, k2, freq):
        ang = pos * freq
        cs, sn = jnp.cos(ang), jnp.sin(ang)
        q1r, q2r = q1 * cs - q2 * sn, q1 * sn + q2 * cs
        k1r, k2r = k1 * cs - k2 * sn, k1 * sn + k2 * cs
        lane = lax.broadcasted_iota(jnp.int32, q1.shape, 1)
        first, second = lane < ROPE, (lane >= ROPE) & (lane < 2 * ROPE)
        k_rot = jnp.where(first, k1r, jnp.where(second, k2r, 0.0))
        q_parts, k_parts = [], []
        for p in range(pairs):
            q_rot = jnp.where(first, pltpu.roll(q1r, (LANES - ROPE * p) % LANES, 1),
                              jnp.where(second, pltpu.roll(q2r, (LANES + ROPE - ROPE * p) % LANES, 1), 0.0))
            q_parts += [qn[:, LANES * p:LANES * (p + 1)], q_rot]
            k_parts += [kn[:, LANES * p:LANES * (p + 1)], k_rot]
        return (jnp.concatenate(q_parts, axis=1), jnp.concatenate(k_parts, axis=1), cs, sn), ()

    q_cat, k_cat, cos_t, sin_t = _rowwise(
        f_rope, [pos_col, (q_mla, 0, MLA_W), (q_mla, 4, LANES), (q_mla, 5, LANES), (kv, 0, MLA_W),
                 (proj, O_KROT // LANES, LANES), (proj, O_KROT // LANES + 1, LANES)], [inv_freq_t],
        [(2 * MLA_W, BF16), (2 * MLA_W, BF16), (LANES, F32), (LANES, F32)], name="rope")

    o_b, lse = _mla_forward(q_cat, k_cat, kv, s_len)

    def f_gate(oa, za, ob, zb):
        return (oa * _silu(za), ob * _silu(zb)), ()

    ya_in, yb_in = _rowwise(f_gate, [o_a, (proj, O_SBZ // SB_W, SB_W), o_b, (proj, O_MLAZ // MLA_W, MLA_W)], [],
                            [(SB_W, BF16), (MLA_W, BF16)], name="branch_gate")
    y_a = _mm(ya_in, w_a_f, name="branch_a")

    def f_merge(yb, ga, gb, ya):
        return (yb, _sigmoid(ga) * ya + _sigmoid(gb) * yb), ()

    y_b, merged = _mm(yb_in, w_b_f, name="branch_b_merge", tiles=(min(s_len, 512), D, MLA_W),
                      epilogue=(f_merge, [(proj, O_GA // D, D), (proj, O_GB // D, D), y_a], [], [(D, F32), (D, BF16)], []))

    def f_loss(out_, x_, t_, mg, bg, gf_):
        gate = mg + bg
        x2 = x_ + gate * out_
        r2 = _rms(x2)
        xn2 = x2 * r2
        err = xn2 * gf_ - t_
        loss = jnp.full((1, LANES), 0.5 / D, F32) * jnp.sum(err * err)
        dy = err * (1.0 / D)
        dx2 = _rms_bwd(xn2, r2, dy * gf_)
        return (dx2, dx2 * gate), (loss, _colsum(dy * xn2), _colsum(dx2 * out_))

    dx2, d_out, loss_part, d_gf, d_gate = _mm(
        merged, w_out_f, name="out_proj_loss", tiles=(min(s_len, 512), D, D),
        epilogue=(f_loss, [x2d, tgt], [mod_gate, b_gate, gf], [(D, F32), (D, BF16)], [LANES, D, D]))

    dw_out = _mm(merged, d_out, ta=True, name="dw_out")

    def f_dmerge(dm, ga, gb, ya, yb):
        sa, sb = _sigmoid(ga), _sigmoid(gb)
        return (dm * sa, dm * sb, dm * ya * sa * (1.0 - sa), dm * yb * sb * (1.0 - sb)), ()

    d_ya, d_yb, d_ga, d_gb = _mm(
        d_out, w_out_f, tb=True, name="d_merge", tiles=(min(s_len, 256), D, D),
        epilogue=(f_dmerge, [(proj, O_GA // D, D), (proj, O_GB // D, D), y_a, y_b], [], [(D, BF16)] * 4, []))
    dw_a = _mm(ya_in, d_ya, ta=True, name="dw_branch_a")
    dw_b = _mm(yb_in, d_yb, ta=True, name="dw_branch_b")

    def f_dgate(d_in, o_, z_):
        return (d_in * _silu(z_), d_in * o_ * _silu_grad(z_)), ()

    d_oa, d_sbz = _mm(d_ya, w_a_f, tb=True, name="d_branch_a",
                      epilogue=(f_dgate, [o_a, (proj, O_SBZ // SB_W, SB_W)], [], [(SB_W, F32), (SB_W, BF16)], []))
    d_ob, d_mlaz = _mm(d_yb, w_b_f, tb=True, name="d_branch_b",
                       epilogue=(f_dgate, [o_b, (proj, O_MLAZ // MLA_W, MLA_W)], [], [(MLA_W, F32), (MLA_W, BF16)], []))

    dq_cat, dk_cat, dv_b = _mla_backward(q_cat, k_cat, kv, o_b, lse, d_ob, s_len)

    def f_drope(dq, dk, dv_, cs, sn):
        lane = lax.broadcasted_iota(jnp.int32, cs.shape, 1)
        first, second = lane < ROPE, (lane >= ROPE) & (lane < 2 * ROPE)
        dq1 = dq2 = dk1 = dk2 = None
        for p in range(pairs):
            q_rot = dq[:, LANES * (2 * p + 1):LANES * (2 * p + 2)]
            k_rot = dk[:, LANES * (2 * p + 1):LANES * (2 * p + 2)]
            parts = (pltpu.roll(jnp.where(first, q_rot, 0.0), (ROPE * p) % LANES, 1),
                     pltpu.roll(jnp.where(second, q_rot, 0.0), (LANES - ROPE + ROPE * p) % LANES, 1),
                     jnp.where(first, k_rot, 0.0), jnp.where(second, k_rot, 0.0))
            if p == 0:
                dq1, dq2, dk1, dk2 = parts
            else:
                dq1, dq2, dk1, dk2 = dq1 + parts[0], dq2 + parts[1], dk1 + parts[2], dk2 + parts[3]
        dqn_ = [dq[:, 2 * LANES * p:2 * LANES * p + LANES] for p in range(pairs)]
        dkn_ = [dk[:, 2 * LANES * p:2 * LANES * p + LANES] for p in range(pairs)]
        return (jnp.concatenate(dqn_ + [dq1 * cs + dq2 * sn, dq2 * cs - dq1 * sn], axis=1),
                jnp.concatenate(dkn_ + [dv_], axis=1),
                jnp.concatenate([dk1 * cs + dk2 * sn, dk2 * cs - dk1 * sn], axis=1)), ()

    dq_k, dkv_k, d_krot = _rowwise(f_drope, [dq_cat, dk_cat, dv_b, cos_t, sin_t], [],
                                   [(HEADS * QK_DIM, BF16), (2 * MLA_W, BF16), (2 * LANES, BF16)], name="d_rope")
    dw_uq_k = _mm(cq_n, dq_k, ta=True, name="dw_uq")
    dw_ukv_k = _mm(ckv_n, dkv_k, ta=True, name="dw_ukv")

    def f_dlat(d_normed, latent, gain):
        r = _rms(latent)
        normed = latent * r
        return (_rms_bwd(normed, r, d_normed * gain),), (_colsum(d_normed * normed),)

    d_cq, d_gq = _mm(dq_k, w_uq_k, tb=True, name="d_cq_norm",
                     epilogue=(f_dlat, [(proj, O_CQ // Q_RANK, Q_RANK)], [gq], [(Q_RANK, BF16)], [Q_RANK]))
    d_ckv, d_gkv = _mm(dkv_k, w_ukv_k, tb=True, name="d_ckv_norm",
                       epilogue=(f_dlat, [(proj, O_CKV // KV_RANK, KV_RANK)], [gkv], [(KV_RANK, BF16)], [KV_RANK]))

    def col_blocks(g):
        kdim, n8 = g.shape
        return g.astype(BF16).reshape(kdim, N_DEV, n8 // N_DEV).transpose(1, 0, 2).reshape(N_DEV, -1, LANES)

    g_blocks = jnp.concatenate([col_blocks(_uq_from_kernel_layout(dw_uq_k)), col_blocks(_ukv_from_kernel_layout(dw_ukv_k)),
                                col_blocks(dw_a), col_blocks(dw_b), dw_out.astype(BF16).reshape(N_DEV, -1, LANES)], axis=1)
    (d_sbq, d_sbk, d_sbv), (g_recv,) = _sb_backward(proj, sb_tails, d_oa, s_len, _Exchange([g_blocks]))

    d_proj = jnp.concatenate([d_ga, d_gb, d_sbq.astype(BF16), d_sbk.astype(BF16), d_sbv.astype(BF16), d_sbz, d_mlaz,
                              d_cq, jnp.zeros((s_len, LANES), BF16), d_ckv, d_krot], axis=1)
    dw_in_kt = _mm(d_proj, h, ta=True, out_dtype=BF16, name="dw_in", tiles=(512, D, s_len))

    def krot_body(t_ref, o_ref):
        half = ROPE // 2
        for part in range(2):
            acc = t_ref[part * LANES:part * LANES + half, :].astype(F32)
            for hh in range(1, HEADS):
                acc = acc + t_ref[part * LANES + hh * half:part * LANES + (hh + 1) * half, :].astype(F32)
            o_ref[part * half:(part + 1) * half, :] = acc

    dw_krot = pl.pallas_call(krot_body, name="dw_krot_sum", out_shape=jax.ShapeDtypeStruct((ROPE, D), F32))(
        dw_in_kt[O_KROT:O_KROT + 2 * LANES])

    g_in_blocks = _w_in_t_from_kernel_layout(dw_in_kt, dw_krot).reshape(N_DEV, -1, D)
    def f_dx(dh_, x_, dx2_, g1_, msc, bsc):
        r = _rms(x_)
        xn = x_ * r
        dn1 = dh_ * (1.0 + (msc + bsc))
        return ((dx2_ + _rms_bwd(xn, r, dn1 * g1_),),
                (_colsum(dh_), _colsum(dh_ * (xn * g1_)), _colsum(dn1 * xn)))

    (grad_x2d, d_shift, d_scale, d_g1), (g_in_recv,) = _mm(
        d_proj, w_in_kt, name="d_h", tiles=(min(s_len, 512), D, 512), exchange=_Exchange([g_in_blocks]),
        epilogue=(f_dx, [x2d, dx2], [g1, mod_scale, b_scale], [(D, F32)], [D, D, D]))

    g_in_sum_t = _sum_blocks(g_in_recv, "sum_grads_w_in")
    g_sum = _sum_blocks(g_recv, "sum_grads")
    g_big = [g_sum[offs[t]:offs[t + 1]].reshape(big[t].shape) for t in range(5)]

    small = jnp.concatenate([d_shift, d_scale, d_gate, d_g1, d_gq, d_gkv, d_gf, loss_part], axis=1)
    n_small = small.shape[1]
    pad = (-n_small) % (8 * LANES)
    small = jnp.pad(small, ((0, 0), (0, pad))).reshape(-1, LANES)
    small_all = _all_gather(small, "gather_small")
    small_sum = _sum_blocks(small_all, "sum_small").reshape(1, -1)
    g_b_ada = small_sum[:, :3 * D]
    g_g1 = small_sum[:, 3 * D:4 * D]
    g_gq = small_sum[:, 4 * D:4 * D + Q_RANK]
    g_gkv = small_sum[:, 4 * D + Q_RANK:4 * D + Q_RANK + KV_RANK]
    g_gf = small_sum[:, 4 * D + Q_RANK + KV_RANK:4 * D + Q_RANK + KV_RANK + D]

    dmod_all = small_all.reshape(N_DEV, -1)[:, :3 * D]
    dmod_cols = lax.dynamic_slice_in_dim(dmod_all, me * (3 * D // N_DEV), 3 * D // N_DEV, axis=1)
    g_w_ada = _mm(c_all, dmod_cols, ta=True, name="dw_ada")

    loss = small_sum[0, n_small - LANES]

    names = ["w_ada", "b_ada", "norm_gain", "w_in", "q_norm_gain", "w_uq", "kv_norm_gain", "w_ukv",
             "w_branch_a", "w_branch_b", "w_out", "final_norm_gain"]
    weights = dict(w_ada=w_ada, b_ada=b_ada, norm_gain=norm_gain, w_in=w_in, q_norm_gain=q_norm_gain, w_uq=w_uq,
                   kv_norm_gain=kv_norm_gain, w_ukv=w_ukv, w_branch_a=w_branch_a, w_branch_b=w_branch_b, w_out=w_out,
                   final_norm_gain=final_norm_gain)
    moms = dict(w_ada=m_w_ada, b_ada=m_b_ada, norm_gain=m_norm_gain, w_in=m_w_in, q_norm_gain=m_q_norm_gain,
                w_uq=m_w_uq, kv_norm_gain=m_kv_norm_gain, w_ukv=m_w_ukv, w_branch_a=m_w_branch_a,
                w_branch_b=m_w_branch_b, w_out=m_w_out, final_norm_gain=m_final_norm_gain)
    vels = dict(w_ada=v_w_ada, b_ada=v_b_ada, norm_gain=v_norm_gain, w_in=v_w_in, q_norm_gain=v_q_norm_gain,
                w_uq=v_w_uq, kv_norm_gain=v_kv_norm_gain, w_ukv=v_w_ukv, w_branch_a=v_w_branch_a,
                w_branch_b=v_w_branch_b, w_out=v_w_out, final_norm_gain=v_final_norm_gain)
    grads2d = dict(w_ada=g_w_ada, b_ada=g_b_ada, norm_gain=g_g1, w_in=g_in_sum_t, q_norm_gain=g_gq, w_uq=g_big[0],
                   kv_norm_gain=g_gkv, w_ukv=g_big[1], w_branch_a=g_big[2], w_branch_b=g_big[3], w_out=g_big[4],
                   final_norm_gain=g_gf)

    grads, deltas, new_m, new_v = [], [], [], []
    for n in names:
        w = weights[n]
        if n == "w_in":
            to2d = lambda t: t[0].T if t.ndim == 3 else t
            back = lambda t: t.T[None]
        else:
            shape2d = grads2d[n].shape
            to2d = lambda t, s=shape2d: t.reshape(s)
            back = lambda t, s=w.shape: t.reshape(s)
        g2d = to2d(grads2d[n])
        d_, m_, v_ = _adamw_call(to2d(w), g2d, to2d(moms[n]), to2d(vels[n]), "adamw_" + n)
        grads.append(back(g2d))
        deltas.append(back(d_))
        new_m.append(back(m_))
        new_v.append(back(v_))

    return (loss, grad_x2d.reshape(x.shape), *grads, *deltas, *new_m, *new_v)
```

```python
import functools
import math

import jax
import jax.numpy as jnp
from jax import lax
from jax.experimental import pallas as pl
from jax.experimental.pallas import tpu as pltpu

F32 = jnp.float32
BF16 = jnp.bfloat16
MXU_DTYPE = jnp.bfloat16

N_DEV = 8
D = 1024
HEADS = 8
HD = 64
SB_W = 512
MLA_W = 512
Q_RANK = 384
KV_RANK = 256
ROPE = 32
NOPE = 64
QK_DIM = NOPE + ROPE
EPS = 1e-6
ROPE_BASE = 10000.0

ADAM_LR = 0.001
ADAM_B1 = 0.9
ADAM_B2 = 0.999
ADAM_EPS = 1e-08
ADAM_WD = 0.01
ADAM_STEP = 10

LANES = 128
VMEM_LIMIT = 48 * 1024 * 1024

O_GA, O_GB = 0, 1024
O_SBQ, O_SBK, O_SBV, O_SBZ = 2048, 2560, 3072, 3584
O_MLAZ = 4096
O_CQ = 4608
O_CKV = 5120
O_KROT = 5376
IN_PAD = 5632

BQ = 256
BK = 256


def _cparams(*sem):
    return pltpu.CompilerParams(dimension_semantics=sem, vmem_limit_bytes=VMEM_LIMIT)


def _tile_of(n, cap=512):
    if n <= cap:
        return n
    for t in (1024, 768, 512, 384, 256, 128):
        if t <= cap and n % t == 0:
            return t
    raise ValueError(n)


def _rowwise(fn, rows, vecs, outs, reds=(), *, name, tile=512):
    norm = []
    for r in rows:
        if isinstance(r, tuple):
            arr, cb, w = r[:3]
            ro = r[3] if len(r) > 3 else 0
        else:
            arr, cb, w, ro = r, 0, r.shape[1], 0
        norm.append((arr, cb, w, ro))
    s_len = norm[0][0].shape[0]
    tile = min(tile, s_len)
    assert s_len % tile == 0
    n_row, n_vec, n_out, n_red = len(norm), len(vecs), len(outs), len(reds)

    def body(*refs):
        step = pl.program_id(0)
        row_refs = refs[:n_row]
        vec_refs = refs[n_row:n_row + n_vec]
        out_refs = refs[n_row + n_vec:n_row + n_vec + n_out]
        red_refs = refs[n_row + n_vec + n_out:]
        row_res, red_res = fn(*[r[...] for r in row_refs], *[v[...] for v in vec_refs])
        for o, val in zip(out_refs, row_res):
            o[...] = val.astype(o.dtype)
        if n_red:
            @pl.when(step == 0)
            def _():
                for r in red_refs:
                    r[...] = jnp.zeros(r.shape, r.dtype)
            for r, val in zip(red_refs, red_res):
                r[...] += val

    in_specs = []
    for arr, cb, w, ro in norm:
        in_specs.append(pl.BlockSpec((tile, w), functools.partial(lambda i, cb, rb: (i + rb, cb), cb=cb, rb=ro // tile)))
        assert ro % tile == 0
    for v in vecs:
        in_specs.append(pl.BlockSpec(v.shape, lambda i: (0, 0)))
    out_shape = [jax.ShapeDtypeStruct((s_len, w), dt) for w, dt in outs]
    out_specs = [pl.BlockSpec((tile, w), lambda i: (i, 0)) for w, _ in outs]
    out_shape += [jax.ShapeDtypeStruct((1, w), F32) for w in reds]
    out_specs += [pl.BlockSpec((1, w), lambda i: (0, 0)) for w in reds]
    res = pl.pallas_call(
        body, name=name, grid=(s_len // tile,), in_specs=in_specs, out_specs=out_specs, out_shape=out_shape,
        compiler_params=_cparams("arbitrary" if n_red else "parallel"),
    )(*[a for a, _, _, _ in norm], *vecs)
    return res


def _mm(a, b, *, ta=False, tb=False, out_dtype=F32, name, exchange=None, tiles=None, epilogue=None):
    m, k = (a.shape[1], a.shape[0]) if ta else a.shape
    n = b.shape[0] if tb else b.shape[1]
    assert (b.shape[1] if tb else b.shape[0]) == k
    tm, tn, tk = tiles or (_tile_of(m, 1024), _tile_of(n, 1024 if n <= 1024 else 512), _tile_of(k, 1024))
    assert m % tm == 0 and n % tn == 0 and k % tk == 0
    ni, nj, nk = m // tm, n // tn, k // tk
    dims = (((0 if ta else 1,), (1 if tb else 0,)), ((), ()))
    n_ex = len(exchange.arrays) if exchange else 0
    fn, rows, vecs, outs, reds = epilogue or (None, (), (), (), ())
    rows = [r if isinstance(r, tuple) else (r, 0, r.shape[1]) for r in rows]
    assert not epilogue or tn == n
    n_res = len(outs) + len(reds) if epilogue else 1

    def body(*refs):
        a_ref, b_ref = refs[:2]
        row_refs, refs = refs[2:2 + len(rows)], refs[2 + len(rows):]
        vec_refs, refs = refs[:len(vecs)], refs[len(vecs):]
        x_refs, refs = refs[:n_ex], refs[n_ex:]
        res_refs, refs = refs[:n_res], refs[n_res:]
        out_refs, acc_ref, sems = refs[:n_ex], refs[n_ex], refs[n_ex + 1:]
        i, j, kk = pl.program_id(0), pl.program_id(1), pl.program_id(2)
        first = (i == 0) & (j == 0) & (kk == 0)

        if exchange:
            @pl.when(first)
            def _():
                exchange.start(x_refs, out_refs, sems)

        if reds:
            @pl.when(first)
            def _():
                for r in res_refs[len(outs):]:
                    r[...] = jnp.zeros(r.shape, r.dtype)

        @pl.when(kk == 0)
        def _():
            acc_ref[...] = jnp.zeros(acc_ref.shape, F32)

        acc_ref[...] += lax.dot_general(a_ref[...].astype(MXU_DTYPE), b_ref[...].astype(MXU_DTYPE), dims,
                                        preferred_element_type=F32)

        @pl.when(kk == nk - 1)
        def _():
            if not epilogue:
                res_refs[0][...] = acc_ref[...].astype(res_refs[0].dtype)
                return
            row_res, red_res = fn(acc_ref[...], *[r[...] for r in row_refs], *[v[...] for v in vec_refs])
            for o, val in zip(res_refs, row_res):
                o[...] = val.astype(o.dtype)
            for r, val in zip(res_refs[len(outs):], red_res):
                r[...] += val

        if exchange:
            @pl.when((i == ni - 1) & (j == nj - 1) & (kk == nk - 1))
            def _():
                exchange.wait(x_refs, out_refs, sems)

    a_spec = pl.BlockSpec((tk, tm), lambda i, j, kk: (kk, i)) if ta else pl.BlockSpec((tm, tk), lambda i, j, kk: (i, kk))
    b_spec = pl.BlockSpec((tn, tk), lambda i, j, kk: (j, kk)) if tb else pl.BlockSpec((tk, tn), lambda i, j, kk: (kk, j))
    in_specs = [a_spec, b_spec]
    in_specs += [pl.BlockSpec((tm, w), functools.partial(lambda i, j, kk, cb: (i, cb), cb=cb)) for _, cb, w in rows]
    in_specs += [pl.BlockSpec(v.shape, lambda i, j, kk: (0, 0)) for v in vecs]
    if epilogue:
        res_specs = [pl.BlockSpec((tm, w), lambda i, j, kk: (i, 0)) for w, _ in outs]
        res_specs += [pl.BlockSpec((1, w), lambda i, j, kk: (0, 0)) for w in reds]
        res_shape = [jax.ShapeDtypeStruct((m, w), dt) for w, dt in outs] + [jax.ShapeDtypeStruct((1, w), F32) for w in reds]
    else:
        res_specs = [pl.BlockSpec((tm, tn), lambda i, j, kk: (i, j))]
        res_shape = [jax.ShapeDtypeStruct((m, n), out_dtype)]
    ordered = bool(exchange or reds)
    res = pl.pallas_call(
        body, name=name, grid=(ni, nj, nk),
        in_specs=in_specs + (exchange.in_specs if exchange else []),
        out_specs=res_specs + (exchange.out_specs if exchange else []),
        out_shape=res_shape + (exchange.out_shape if exchange else []),
        scratch_shapes=[pltpu.VMEM((tm, tn), F32)] + (exchange.scratch if exchange else []),
        compiler_params=_cparams(*(("arbitrary",) * 3 if ordered else ("parallel", "parallel", "arbitrary"))),
    )(a, b, *[r[0] for r in rows], *vecs, *(exchange.arrays if exchange else []))
    main = res[:n_res] if epilogue else res[0]
    return (main, res[n_res:]) if exchange else main


_NT = (((1,), (1,)), ((), ()))
_TN = (((0,), (0,)), ((), ()))


def _dot(a, b):
    return jnp.dot(a, b, preferred_element_type=F32)


def _dot_nt(a, b):
    return lax.dot_general(a, b, _NT, preferred_element_type=F32)


def _dot_tn(a, b):
    return lax.dot_general(a, b, _TN, preferred_element_type=F32)


def _running_sum(x, tri):
    return _dot(x.astype(MXU_DTYPE), tri)


def _neg_softplus(z):
    u = jnp.exp2(jnp.abs(z) * (-1.0 / math.log(2.0)))
    return -jnp.maximum(z, 0.0) - jnp.log(1.0 + u)


def _walk_blocks(step, st, n, descending, group=2):
    done = 0
    size = group
    while size >= 1:
        def trip(t, s, size=size, done=done):
            js = [done + size * t + g for g in range(size)]
            return step([n - 1 - j for j in js] if descending else js, s)

        trips = (n - done) // size
        st = lax.fori_loop(0, trips, trip, st)
        done = done + size * trips
        size //= 2
    return st


def _chains(js):
    return [(h, t) for t in range(len(js)) for h in range(2)]


def _rowsum(x):
    return jnp.sum(x, axis=1, keepdims=True)


def _attn_consts():
    row = lax.broadcasted_iota(jnp.int32, (BQ, BK), 0)
    col = lax.broadcasted_iota(jnp.int32, (BQ, BK), 1)
    trow = lax.broadcasted_iota(jnp.int32, (BK, BK), 0)
    tcol = lax.broadcasted_iota(jnp.int32, (BK, BK), 1)
    lane = lax.broadcasted_iota(jnp.int32, (BQ, LANES), 1)
    klane = lax.broadcasted_iota(jnp.int32, (BK, LANES), 1)
    return row, col, trow, tcol, lane, klane


assert BQ == BK


def _diag_step(jd, descending, diag_mask, group=4):
    below = list(range(group)) if descending else list(reversed(range(group)))
    js = [jnp.maximum(jd - o, 0) for o in below]
    masks = [diag_mask if o == 0 else jd - o >= 0 for o in below]
    return js, masks, jnp.maximum(jd - (group - 1), 0)


def _key_slice(j):
    return pl.ds(pl.multiple_of(j * BK, BK), BK)


def _sb_forward(proj, s_len, exchange):
    nq = s_len // BQ
    assert s_len // BK <= HD
    qc, kc, vc = O_SBQ // LANES, O_SBK // LANES, O_SBV // LANES
    n_ex = len(exchange.arrays)

    def body(q_ref, k_ref, v_ref, *rest):
        x_refs, (o_ref, tails_ref) = rest[:n_ex], rest[n_ex:n_ex + 2]
        out_refs, sems = rest[n_ex + 2:2 * n_ex + 2], rest[2 * n_ex + 2:]
        p = pl.program_id(0)
        i = pl.program_id(1)
        jd = i

        @pl.when((p == 0) & (i == 0))
        def _():
            exchange.start(x_refs, out_refs, sems)

        row, col, trow, tcol, lane, klane = _attn_consts()
        tri = (trow >= tcol).astype(MXU_DTYPE)
        q = q_ref[...] * 0.125
        qh = [jnp.where(lane < HD, q, 0.0).astype(MXU_DTYPE), jnp.where(lane >= HD, q, 0.0).astype(MXU_DTYPE)]

        km = [klane < HD, klane >= HD]

        def step(js, st, masks):
            carry, acc, tail = st
            chains = _chains(js)
            kj = [k_ref[_key_slice(j), :].astype(MXU_DTYPE) for j in js]
            vj = [v_ref[_key_slice(j), :].astype(MXU_DTYPE) for j in js]
            z = {(h, t): _dot_nt(qh[h], kj[t]) for h, t in chains}
            run = list(carry)
            suf, carry_in = {}, {}
            for h, t in chains:
                lom = _neg_softplus(z[h, t])
                if masks:
                    lom = jnp.where(masks[t], lom, 0.0)
                suf[h, t] = _running_sum(lom, tri)
                carry_in[h, t] = run[h]
                run[h] = run[h] + _rowsum(lom)
            for h, t in chains:
                a = jnp.exp(z[h, t] + suf[h, t] + carry_in[h, t])
                if masks:
                    a = jnp.where(masks[t], a, 0.0)
                acc = acc + _dot(a.astype(MXU_DTYPE), jnp.where(km[h], vj[t], 0))
                tail_lane = js[t] if not masks or masks[t].ndim else jnp.where(masks[t], js[t], -LANES)
                tail = jnp.where(lane == h * HD + tail_lane, carry_in[h, t], tail)
            return tuple(run), acc, tail

        zero = jnp.zeros((BQ, LANES), F32)
        diag_js, masks, left = _diag_step(jd, True, col < row, group=2)
        st = step(diag_js, ((jnp.zeros((BQ, 1), F32),) * 2, zero, zero), masks)
        st = _walk_blocks(lambda js, s: step(js, s, None), st, left, True, group=8)
        o_ref[...] = st[1]
        tails_ref[...] = st[2]

        @pl.when((p == HEADS // 2 - 1) & (i == nq - 1))
        def _():
            exchange.wait(x_refs, out_refs, sems)

    blk = pl.BlockSpec((BQ, LANES), lambda p, i: (i, p))
    out = jax.ShapeDtypeStruct((s_len, SB_W), F32)
    res = pl.pallas_call(
        body, name="sb_fwd", grid=(HEADS // 2, nq),
        in_specs=[pl.BlockSpec((BQ, LANES), lambda p, i: (i, qc + p)),
                  pl.BlockSpec((s_len, LANES), lambda p, i: (0, kc + p)),
                  pl.BlockSpec((s_len, LANES), lambda p, i: (0, vc + p))] + exchange.in_specs,
        out_specs=[blk, blk] + exchange.out_specs, out_shape=[out, out] + exchange.out_shape,
        scratch_shapes=exchange.scratch,
        compiler_params=_cparams("arbitrary", "arbitrary"),
    )(proj, proj, proj, *exchange.arrays)
    return res[:2], res[2:]


def _sb_backward(proj, tails, do, s_len, exchange):
    nq = s_len // BQ
    qc, kc, vc = O_SBQ // LANES, O_SBK // LANES, O_SBV // LANES
    n_ex = len(exchange.arrays)

    def body(q_ref, k_ref, v_ref, tails_ref, do_ref, *rest):
        x_refs, (dq_ref, dk_ref, dv_ref) = rest[:n_ex], rest[n_ex:n_ex + 3]
        out_refs, sems = rest[n_ex + 3:2 * n_ex + 3], rest[2 * n_ex + 3:]
        p = pl.program_id(0)
        i = pl.program_id(1)
        jd = i

        @pl.when((p == 0) & (i == 0))
        def _():
            exchange.start(x_refs, out_refs, sems)

        @pl.when(i == 0)
        def _():
            dk_ref[...] = jnp.zeros(dk_ref.shape, F32)
            dv_ref[...] = jnp.zeros(dv_ref.shape, F32)

        row, col, trow, tcol, lane, klane = _attn_consts()
        tri = (trow >= tcol).astype(MXU_DTYPE)
        tri_p = (trow <= tcol).astype(MXU_DTYPE)
        q = q_ref[...] * 0.125
        tails_blk = tails_ref[...]
        do_blk = do_ref[...]
        hm = [lane < HD, lane >= HD]
        km = [klane < HD, klane >= HD]
        qh = [jnp.where(m, q, 0.0).astype(MXU_DTYPE) for m in hm]
        doh = [jnp.where(m, do_blk, 0.0).astype(MXU_DTYPE) for m in hm]

        def step(js, st, masks):
            before, dq = st
            chains = _chains(js)
            kj = [k_ref[_key_slice(j), :].astype(MXU_DTYPE) for j in js]
            vj = [v_ref[_key_slice(j), :].astype(MXU_DTYPE) for j in js]
            z = {(h, t): _dot_nt(qh[h], kj[t]) for h, t in chains}
            da = {(h, t): _dot_nt(doh[h], vj[t]) for h, t in chains}
            suf, sig = {}, {}
            for h, t in chains:
                lom = _neg_softplus(z[h, t])
                if masks:
                    lom = jnp.where(masks[t], lom, 0.0)
                suf[h, t] = _running_sum(lom, tri)
                sig[h, t] = jnp.exp(z[h, t] + lom)
            run = list(before)
            dl, pre, before_in = {}, {}, {}
            dk_add, dv_add = [None] * len(js), [None] * len(js)
            for h, t in chains:
                tail = _rowsum(jnp.where(lane == h * HD + js[t], tails_blk, 0.0))
                a = jnp.exp(z[h, t] + suf[h, t] + tail)
                if masks:
                    a = jnp.where(masks[t], a, 0.0)
                dl[h, t] = da[h, t] * a
                pre[h, t] = _dot(dl[h, t].astype(MXU_DTYPE), tri_p)
                dv_h = _dot_tn(a.astype(MXU_DTYPE), doh[h])
                dv_add[t] = dv_h if dv_add[t] is None else dv_add[t] + dv_h
                before_in[h, t] = run[h]
                run[h] = run[h] + _rowsum(dl[h, t])
            for h, t in chains:
                upto = before_in[h, t] + pre[h, t]
                dz = dl[h, t] - sig[h, t] * upto
                if masks:
                    dz = jnp.where(masks[t], dz, 0.0)
                dzb = dz.astype(MXU_DTYPE)
                dq = dq + _dot(dzb, jnp.where(km[h], kj[t], 0))
                dk_h = _dot_tn(dzb, qh[h])
                dk_add[t] = dk_h if dk_add[t] is None else dk_add[t] + dk_h
            for t, j in enumerate(js):
                dk_ref[_key_slice(j), :] += dk_add[t]
                dv_ref[_key_slice(j), :] += dv_add[t]
            return tuple(run), dq

        st = ((jnp.zeros((BQ, 1), F32),) * 2, jnp.zeros((BQ, LANES), F32))
        diag_js, masks, left = _diag_step(jd, False, col < row, group=2)
        st = _walk_blocks(lambda js, s: step(js, s, None), st, left, False, group=8)
        st = step(diag_js, st, masks)
        dq_ref[...] = st[1] * 0.125

        @pl.when((p == HEADS // 2 - 1) & (i == nq - 1))
        def _():
            exchange.wait(x_refs, out_refs, sems)

    blk = pl.BlockSpec((BQ, LANES), lambda p, i: (i, p))
    full = pl.BlockSpec((s_len, LANES), lambda p, i: (0, p))
    out = jax.ShapeDtypeStruct((s_len, SB_W), F32)
    res = pl.pallas_call(
        body, name="sb_bwd", grid=(HEADS // 2, nq),
        in_specs=[pl.BlockSpec((BQ, LANES), lambda p, i: (i, qc + p)),
                  pl.BlockSpec((s_len, LANES), lambda p, i: (0, kc + p)),
                  pl.BlockSpec((s_len, LANES), lambda p, i: (0, vc + p)),
                  blk, blk] + exchange.in_specs,
        out_specs=[blk, full, full] + exchange.out_specs, out_shape=[out, out, out] + exchange.out_shape,
        scratch_shapes=exchange.scratch,
        compiler_params=_cparams("arbitrary", "arbitrary"),
    )(proj, proj, proj, tails, do, *exchange.arrays)
    return res[:3], res[3:]


def _pair_mask(rows, h):
    lane = lax.broadcasted_iota(jnp.int32, (rows, 2 * LANES), 1)
    rot = lane - LANES
    return (((lane < LANES) & (lane // HD == h))
            | ((lane >= LANES) & (rot < 2 * ROPE) & ((rot // (ROPE // 2)) % 2 == h)))


def _mla_forward(q_cat, k_cat, kv, s_len):
    nq = s_len // BQ
    scale = 1.0 / math.sqrt(QK_DIM)

    def body(q_ref, k_ref, v_ref, o_ref, lse_ref):
        i = pl.program_id(1)
        jd = i
        row, col, trow, tcol, lane, klane = _attn_consts()
        q = q_ref[...]
        hm = [lane < HD, lane >= HD]
        km = [klane < HD, klane >= HD]
        qh = [jnp.where(_pair_mask(BQ, h), q, 0) for h in range(2)]

        def step(js, st, masks):
            m_run, l_run, acc = st
            chains = _chains(js)
            kj = [k_ref[_key_slice(j), :] for j in js]
            vj = [v_ref[_key_slice(j), :].astype(MXU_DTYPE) for j in js]
            s = {}
            for h, t in chains:
                s[h, t] = _dot_nt(qh[h], kj[t]) * scale
                if masks:
                    s[h, t] = jnp.where(masks[t], s[h, t], -jnp.inf)
            m_new, alpha, l_new = [], [], []
            for h in range(2):
                top = m_run[h]
                for t in range(len(js)):
                    top = jnp.maximum(top, jnp.max(s[h, t], axis=1, keepdims=True))
                m_new.append(top)
                alpha.append(jnp.exp(m_run[h] - top))
                l_new.append(alpha[h] * l_run[h])
            add = None
            for h, t in chains:
                pr = jnp.exp(s[h, t] - m_new[h])
                l_new[h] = l_new[h] + _rowsum(pr)
                part = _dot(pr.astype(MXU_DTYPE), jnp.where(km[h], vj[t], 0))
                add = part if add is None else add + part
            acc = jnp.where(hm[0], alpha[0], alpha[1]) * acc + add
            return tuple(m_new), tuple(l_new), acc

        st = ((jnp.full((BQ, 1), -1e30, F32),) * 2, (jnp.zeros((BQ, 1), F32),) * 2, jnp.zeros((BQ, LANES), F32))
        diag_js, masks, left = _diag_step(jd, True, col <= row)
        st = step(diag_js, st, masks)
        m_run, l_run, acc = _walk_blocks(lambda js, s: step(js, s, None), st, left, True, group=8)
        o_ref[...] = acc / jnp.where(hm[0], l_run[0], l_run[1])
        lse_ref[...] = jnp.where(hm[0], m_run[0] + jnp.log(l_run[0]), m_run[1] + jnp.log(l_run[1]))

    blk = pl.BlockSpec((BQ, LANES), lambda p, i: (i, p))
    out = jax.ShapeDtypeStruct((s_len, MLA_W), F32)
    return pl.pallas_call(
        body, name="mla_fwd", grid=(HEADS // 2, nq),
        in_specs=[pl.BlockSpec((BQ, 2 * LANES), lambda p, i: (i, p)),
                  pl.BlockSpec((s_len, 2 * LANES), lambda p, i: (0, p)),
                  pl.BlockSpec((s_len, LANES), lambda p, i: (0, MLA_W // LANES + p))],
        out_specs=[blk, blk], out_shape=[out, out],
        compiler_params=_cparams("parallel", "parallel"),
    )(q_cat, k_cat, kv)


def _mla_backward(q_cat, k_cat, kv, o, lse, do, s_len):
    nq = s_len // BQ
    scale = 1.0 / math.sqrt(QK_DIM)

    def body(q_ref, k_ref, v_ref, o_ref, lse_ref, do_ref, dq_ref, dk_ref, dv_ref):
        i = pl.program_id(1)

        @pl.when(i == 0)
        def _():
            dk_ref[...] = jnp.zeros(dk_ref.shape, F32)
            dv_ref[...] = jnp.zeros(dv_ref.shape, F32)

        jd = i
        row, col, trow, tcol, lane, klane = _attn_consts()
        q = q_ref[...]
        o_blk = o_ref[...]
        do_blk = do_ref[...]
        lse_blk = lse_ref[...]
        hm = [lane < HD, lane >= HD]
        kpm = [_pair_mask(BK, h) for h in range(2)]
        qh = [jnp.where(_pair_mask(BQ, h), q, 0) for h in range(2)]
        doh_f = [jnp.where(m, do_blk, 0.0) for m in hm]
        doh = [d.astype(MXU_DTYPE) for d in doh_f]
        delta = [jnp.sum(d * o_blk, axis=1, keepdims=True) for d in doh_f]
        lse_h = [jnp.sum(jnp.where(lane == h * HD, lse_blk, 0.0), axis=1, keepdims=True) for h in range(2)]

        def step(js, st, masks):
            dq = st
            chains = _chains(js)
            kj = [k_ref[_key_slice(j), :] for j in js]
            vj = [v_ref[_key_slice(j), :].astype(MXU_DTYPE) for j in js]
            s = {(h, t): _dot_nt(qh[h], kj[t]) for h, t in chains}
            dp = {(h, t): _dot_nt(doh[h], vj[t]) for h, t in chains}
            adds = [[None] * len(js) for _ in range(2)]

            def accumulate(slot, t, part):
                adds[slot][t] = part if adds[slot][t] is None else adds[slot][t] + part

            for h, t in chains:
                pr = jnp.exp(s[h, t] * scale - lse_h[h])
                if masks:
                    pr = jnp.where(masks[t], pr, 0.0)
                dsb = (pr * (dp[h, t] - delta[h]) * scale).astype(MXU_DTYPE)
                dq = dq + _dot(dsb, jnp.where(kpm[h], kj[t], 0))
                accumulate(0, t, _dot_tn(dsb, qh[h]))
                accumulate(1, t, _dot_tn(pr.astype(MXU_DTYPE), doh[h]))
            for t, j in enumerate(js):
                dk_ref[_key_slice(j), :] += adds[0][t]
                dv_ref[_key_slice(j), :] += adds[1][t]
            return dq

        diag_js, masks, left = _diag_step(jd, True, col <= row)
        st = step(diag_js, jnp.zeros((BQ, 2 * LANES), F32), masks)
        dq_ref[...] = _walk_blocks(lambda js, s: step(js, s, None), st, left, True, group=8)

    blk = pl.BlockSpec((BQ, LANES), lambda p, i: (i, p))
    full = pl.BlockSpec((s_len, LANES), lambda p, i: (0, p))
    out = jax.ShapeDtypeStruct((s_len, MLA_W), F32)
    out_cat = jax.ShapeDtypeStruct((s_len, 2 * MLA_W), F32)
    return pl.pallas_call(
        body, name="mla_bwd", grid=(HEADS // 2, nq),
        in_specs=[pl.BlockSpec((BQ, 2 * LANES), lambda p, i: (i, p)),
                  pl.BlockSpec((s_len, 2 * LANES), lambda p, i: (0, p)),
                  pl.BlockSpec((s_len, LANES), lambda p, i: (0, MLA_W // LANES + p)),
                  blk, blk, blk],
        out_specs=[pl.BlockSpec((BQ, 2 * LANES), lambda p, i: (i, p)),
                   pl.BlockSpec((s_len, 2 * LANES), lambda p, i: (0, p)), full],
        out_shape=[out_cat, out_cat, out],
        compiler_params=_cparams("arbitrary", "arbitrary"),
    )(q_cat, k_cat, kv, o, lse, do)


def _mesh_pos():
    return lax.axis_index("x"), lax.axis_index("y"), lax.axis_index("c")


def _dev_index(px, py, pc):
    return 4 * px + 2 * py + pc


def _all_gather(block, name):
    return _all_gather_parts([block], name)[0]


def _all_gather_parts(blocks, name):
    n = len(blocks)

    def body(*refs):
        x_refs, out_refs = refs[:n], refs[n:2 * n]
        send_sems, recv_sems, local_sems = refs[2 * n:]
        x, y, c = _mesh_pos()
        me, sibling = (x, y, c), (x, y, 1 - c)
        chips = [(1 - x, y), (x, 1 - y), (1 - x, 1 - y)]

        def copy(a, k, blockpos, to, src=None):
            slot = out_refs[a].at[_dev_index(*blockpos)]
            return pltpu.make_async_remote_copy(
                src_ref=slot if src is None else src, dst_ref=slot,
                send_sem=send_sems.at[7 * a + k], recv_sem=recv_sems.at[7 * a + k],
                device_id=to, device_id_type=pl.DeviceIdType.MESH)

        mine = [pltpu.make_async_copy(x_refs[a], out_refs[a].at[_dev_index(*me)], local_sems.at[a]) for a in range(n)]
        for cp in mine:
            cp.start()
        first = []
        for a in range(n):
            first.append(copy(a, 0, me, sibling, src=x_refs[a]))
            first += [copy(a, 1 + j, me, (*chip, c), src=x_refs[a]) for j, chip in enumerate(chips)]
        for cp in first:
            cp.start()
        passed = []
        for j, chip in enumerate(chips):
            for a in range(n):
                copy(a, 1 + j, (*chip, c), me).wait_recv()
                passed.append(copy(a, 4 + j, (*chip, c), sibling))
                passed[-1].start()
        for a in range(n):
            copy(a, 0, sibling, me).wait_recv()
            for j, chip in enumerate(chips):
                copy(a, 4 + j, (*chip, 1 - c), me).wait_recv()
        for cp in first + passed:
            cp.wait_send()
        for cp in mine:
            cp.wait()

    return pl.pallas_call(
        body, name=name,
        out_shape=[jax.ShapeDtypeStruct((N_DEV,) + b.shape, b.dtype) for b in blocks],
        in_specs=[pl.BlockSpec(memory_space=pl.ANY)] * n, out_specs=[pl.BlockSpec(memory_space=pl.ANY)] * n,
        scratch_shapes=[pltpu.SemaphoreType.DMA((7 * n,)), pltpu.SemaphoreType.DMA((7 * n,)),
                        pltpu.SemaphoreType.DMA((n,))],
    )(*blocks)


class _Exchange:
    def __init__(self, arrays):
        self.arrays = list(arrays)
        n = len(self.arrays)
        self.in_specs = [pl.BlockSpec(memory_space=pl.ANY)] * n
        self.out_specs = [pl.BlockSpec(memory_space=pl.ANY)] * n
        self.out_shape = [jax.ShapeDtypeStruct(a.shape, a.dtype) for a in self.arrays]
        self.scratch = [pltpu.SemaphoreType.DMA((7 * n,)), pltpu.SemaphoreType.DMA((7 * n,)),
                        pltpu.SemaphoreType.DMA((n,))]

    def _copies(self, x_refs, out_refs, sems, with_arrivals):
        send_sems, recv_sems, local_sems = sems
        x, y, c = _mesh_pos()
        me = _dev_index(x, y, c)
        flips = [(fx, fy, fc) for fx in (0, 1) for fy in (0, 1) for fc in (0, 1)][1:]
        peers = [(1 - x if fx else x, 1 - y if fy else y, 1 - c if fc else c) for fx, fy, fc in flips]
        mine, sends, arrivals = [], [], []
        for a in range(len(self.arrays)):
            mine.append(pltpu.make_async_copy(x_refs[a].at[me], out_refs[a].at[me], local_sems.at[a]))
            for k, peer in enumerate(peers):
                sends.append(pltpu.make_async_remote_copy(
                    src_ref=x_refs[a].at[_dev_index(*peer)], dst_ref=out_refs[a].at[me],
                    send_sem=send_sems.at[7 * a + k], recv_sem=recv_sems.at[7 * a + k],
                    device_id=peer, device_id_type=pl.DeviceIdType.MESH))
                if not with_arrivals:
                    continue
                arrivals.append(pltpu.make_async_remote_copy(
                    src_ref=x_refs[a].at[me], dst_ref=out_refs[a].at[_dev_index(*peer)],
                    send_sem=send_sems.at[7 * a + k], recv_sem=recv_sems.at[7 * a + k],
                    device_id=peer, device_id_type=pl.DeviceIdType.MESH))
        return mine, sends, arrivals

    def start(self, x_refs, out_refs, sems):
        mine, sends, _ = self._copies(x_refs, out_refs, sems, False)
        for cp in mine + sends:
            cp.start()

    def wait(self, x_refs, out_refs, sems):
        mine, sends, arrivals = self._copies(x_refs, out_refs, sems, True)
        for cp in arrivals:
            cp.wait_recv()
        for cp in sends:
            cp.wait_send()
        for cp in mine:
            cp.wait()


def _sum_blocks(parts, name):
    n, r, c = parts.shape
    row_tiles = [t for t in range(16, min(r, 2048) + 1, 16) if r % t == 0]
    if row_tiles:
        tr, tc = max(row_tiles), c
    else:
        tr, tc = r, 2 * LANES
    assert c % tc == 0

    def body(p_ref, o_ref):
        acc = p_ref[0].astype(F32)
        for s in range(1, n):
            acc = acc + p_ref[s].astype(F32)
        o_ref[...] = acc

    return pl.pallas_call(
        body, name=name, grid=(r // tr, c // tc),
        in_specs=[pl.BlockSpec((n, tr, tc), lambda i, j: (0, i, j))],
        out_specs=pl.BlockSpec((tr, tc), lambda i, j: (i, j)),
        out_shape=jax.ShapeDtypeStruct((r, c), F32),
        compiler_params=_cparams("parallel", "parallel"),
    )(parts)


def _sigmoid(x):
    return 1.0 / (1.0 + jnp.exp(-x))


def _silu(x):
    return x * _sigmoid(x)


def _silu_grad(x):
    s = _sigmoid(x)
    return s * (1.0 + x * (1.0 - s))


def _colsum(x):
    return jnp.sum(x, axis=0, keepdims=True)


def _rms(x):
    return lax.rsqrt(jnp.mean(x * x, axis=-1, keepdims=True) + EPS)


def _rms_bwd(xn, r, dxn):
    return r * (dxn - xn * jnp.mean(dxn * xn, axis=-1, keepdims=True))


def _adamw(w, g, m, v):
    m = ADAM_B1 * m + (1.0 - ADAM_B1) * g
    v = ADAM_B2 * v + (1.0 - ADAM_B2) * jnp.square(g)
    m_hat = m / (1.0 - ADAM_B1 ** ADAM_STEP)
    v_hat = v / (1.0 - ADAM_B2 ** ADAM_STEP)
    delta = -ADAM_LR * (m_hat / (jnp.sqrt(v_hat) + ADAM_EPS) + ADAM_WD * w)
    return delta, m, v


def _adamw_call(w, g, m, v, name):
    r, c = w.shape
    if r % 256 == 0:
        tr, tc = 256, c
    elif r * c <= 256 * 1024 or c % (2 * LANES):
        tr, tc = r, c
    else:
        tr, tc = r, 2 * LANES

    def body(w_ref, g_ref, m_ref, v_ref, d_out, m_out, v_out):
        d_out[...], m_out[...], v_out[...] = _adamw(w_ref[...], g_ref[...], m_ref[...], v_ref[...])

    spec = pl.BlockSpec((tr, tc), lambda i, j: (i, j))
    return pl.pallas_call(
        body, name=name, grid=(r // tr, c // tc), in_specs=[spec] * 4, out_specs=[spec] * 3,
        out_shape=[jax.ShapeDtypeStruct((r, c), F32)] * 3, compiler_params=_cparams("parallel", "parallel"),
    )(w, g, m, v)


def _uq_to_kernel_layout(w):
    lead = w.shape[:-1]
    t = w.reshape(lead + (HEADS, QK_DIM))
    return jnp.concatenate([t[..., :NOPE].reshape(lead + (HEADS * NOPE,)),
                            t[..., NOPE:NOPE + ROPE // 2].reshape(lead + (LANES,)),
                            t[..., NOPE + ROPE // 2:].reshape(lead + (LANES,))], axis=-1)


def _uq_from_kernel_layout(w):
    lead = w.shape[:-1]
    nope = w[..., :HEADS * NOPE].reshape(lead + (HEADS, NOPE))
    r1 = w[..., HEADS * NOPE:HEADS * NOPE + LANES].reshape(lead + (HEADS, ROPE // 2))
    r2 = w[..., HEADS * NOPE + LANES:].reshape(lead + (HEADS, ROPE // 2))
    return jnp.concatenate([nope, r1, r2], axis=-1).reshape(lead + (HEADS * QK_DIM,))


def _ukv_to_kernel_layout(w):
    lead = w.shape[:-1]
    t = w.reshape(lead + (HEADS, NOPE + HD))
    return jnp.concatenate([t[..., :NOPE].reshape(lead + (HEADS * NOPE,)),
                            t[..., NOPE:].reshape(lead + (HEADS * HD,))], axis=-1)


def _ukv_from_kernel_layout(w):
    lead = w.shape[:-1]
    kn = w[..., :HEADS * NOPE].reshape(lead + (HEADS, NOPE))
    vv = w[..., HEADS * NOPE:].reshape(lead + (HEADS, HD))
    return jnp.concatenate([kn, vv], axis=-1).reshape(lead + (HEADS * (NOPE + HD),))


def _w_in_t_to_kernel_layout(wt):
    sb = wt[0:2048]
    c_q = wt[2048:2432]
    c_kv = wt[2432:2688]
    k_rot = wt[2688:2720]
    mla_z = wt[2720:3232]
    gates = wt[3232:5280]
    zeros = jnp.zeros((LANES, wt.shape[1]), wt.dtype)
    k1 = jnp.tile(k_rot[:ROPE // 2], (HEADS, 1))
    k2 = jnp.tile(k_rot[ROPE // 2:], (HEADS, 1))
    return jnp.concatenate([gates, sb, mla_z, c_q, zeros, c_kv, k1, k2], axis=0)


def _w_in_t_from_kernel_layout(gt, g_rot):
    return jnp.concatenate([gt[O_SBQ:O_SBQ + 2048], gt[O_CQ:O_CQ + Q_RANK], gt[O_CKV:O_CKV + KV_RANK],
                            g_rot.astype(gt.dtype), gt[O_MLAZ:O_MLAZ + MLA_W], gt[O_GA:O_GA + 2 * D]], axis=0)


def kernel(x, c, positions, w_ada, b_ada, norm_gain, w_in, q_norm_gain, w_uq, kv_norm_gain, w_ukv, w_branch_a, w_branch_b, w_out, final_norm_gain, loss_target, m_w_ada, m_b_ada, m_norm_gain, m_w_in, m_q_norm_gain, m_w_uq, m_kv_norm_gain, m_w_ukv, m_w_branch_a, m_w_branch_b, m_w_out, m_final_norm_gain, v_w_ada, v_b_ada, v_norm_gain, v_w_in, v_q_norm_gain, v_w_uq, v_kv_norm_gain, v_w_ukv, v_w_branch_a, v_w_branch_b, v_w_out, v_final_norm_gain):
    s_len = x.shape[1]
    me = _dev_index(*_mesh_pos())
    x2d = x[0]
    tgt = loss_target[0]

    w_in_t = w_in[0].T.astype(BF16)
    big = [w_uq[0], w_ukv[0], w_branch_a[0], w_branch_b[0], w_out[0]]
    big_sizes = [int(w.size) for w in big]
    packed = jnp.concatenate([w.astype(BF16).reshape(-1, LANES) for w in big], axis=0)
    g_in_t, c_all = _all_gather_parts([w_in_t, c.reshape(8, LANES)], "gather_w_in")
    c_all = c_all.reshape(N_DEV, D)
    w_in_kt = _w_in_t_to_kernel_layout(g_in_t.reshape(N_DEV * w_in_t.shape[0], D))

    mod_cols = _mm(c_all, w_ada[0], name="ada_mod")
    mod_all = _all_gather(mod_cols, "gather_mod")
    mod = lax.dynamic_index_in_dim(mod_all, me, axis=1, keepdims=False).reshape(1, 3 * D)
    mod_shift, mod_scale, mod_gate = mod[:, :D], mod[:, D:2 * D], mod[:, 2 * D:]
    b_shift, b_scale, b_gate = b_ada[:, :D], b_ada[:, D:2 * D], b_ada[:, 2 * D:]
    g1 = norm_gain
    gq, gkv = q_norm_gain, kv_norm_gain
    gf = final_norm_gain.reshape(1, D)

    def f_h(x_, g1_, ms, bs, msc, bsc):
        xn = x_ * _rms(x_)
        return (xn * g1_ * (1.0 + (msc + bsc)) + (ms + bs),), ()

    (h,) = _rowwise(f_h, [x2d], [g1, mod_shift, b_shift, mod_scale, b_scale], [(D, BF16)], name="ada_norm")
    proj = _mm(h, w_in_kt, tb=True, name="proj_in", tiles=(min(s_len, 1024), IN_PAD // 2, D))

    (o_a, sb_tails), (gathered,) = _sb_forward(
        proj, s_len, _Exchange([jnp.broadcast_to(packed[None], (N_DEV,) + packed.shape)]))
    offs = [0]
    for n in big_sizes:
        offs.append(offs[-1] + n // LANES)

    def unpack(t, shape):
        return gathered[:, offs[t]:offs[t + 1], :].reshape((N_DEV,) + shape)

    def cols(t, shape):
        return unpack(t, shape).transpose(1, 0, 2).reshape(shape[0], N_DEV * shape[1])

    w_uq_k = _uq_to_kernel_layout(cols(0, big[0].shape))
    w_ukv_k = _ukv_to_kernel_layout(cols(1, big[1].shape))
    w_a_f = cols(2, big[2].shape)
    w_b_f = cols(3, big[3].shape)
    w_out_f = unpack(4, big[4].shape).reshape(D, D)

    def f_lat(cq, ckv, gq_, gkv_):
        return (cq * _rms(cq) * gq_, ckv * _rms(ckv) * gkv_), ()

    cq_n, ckv_n = _rowwise(f_lat, [(proj, O_CQ // Q_RANK, Q_RANK), (proj, O_CKV // KV_RANK, KV_RANK)], [gq, gkv],
                           [(Q_RANK, BF16), (KV_RANK, BF16)], name="latent_norm")
    q_mla = _mm(cq_n, w_uq_k, name="q_up")
    kv = _mm(ckv_n, w_ukv_k, name="kv_up")

    inv_freq = ROPE_BASE ** (-jnp.arange(0, ROPE, 2, dtype=F32) / ROPE)
    inv_freq_t = jnp.tile(inv_freq, HEADS).reshape(1, LANES)
    pos_col = positions.reshape(s_len, 1).astype(F32)

    pairs = HEADS // 2

    def f_rope(pos, qn, q1, q2, kn, k1, k2, freq):
        ang = pos * freq
        cs, sn = jnp.cos(ang), jnp.sin(ang)
        q1r, q2r = q1 * cs - q2 * sn, q1 * sn + q2 * cs
        k1r, k2r = k1 * cs - k2 * sn, k1 * sn + k2 * cs
        lane = lax.broadcasted_iota(jnp.int32, q1.shape, 1)
        first, second = lane < ROPE, (lane >= ROPE) & (lane < 2 * ROPE)
        k_rot = jnp.where(first, k1r, jnp.where(second, k2r, 0.0))
        q_parts, k_parts = [], []
        for p in range(pairs):
            q_rot = jnp.where(first, pltpu.roll(q1r, (LANES - ROPE * p) % LANES, 1),
                              jnp.where(second, pltpu.roll(q2r, (LANES + ROPE - ROPE * p) % LANES, 1), 0.0))
            q_parts += [qn[:, LANES * p:LANES * (p + 1)], q_rot]
            k_parts += [kn[:, LANES * p:LANES * (p + 1)], k_rot]
        return (jnp.concatenate(q_parts, axis=1), jnp.concatenate(k_parts, axis=1), cs, sn), ()

    q_cat, k_cat, cos_t, sin_t = _rowwise(
        f_rope, [pos_col, (q_mla, 0, MLA_W), (q_mla, 4, LANES), (q_mla, 5, LANES), (kv, 0, MLA_W),
                 (proj, O_KROT // LANES, LANES), (proj, O_KROT // LANES + 1, LANES)], [inv_freq_t],
        [(2 * MLA_W, BF16), (2 * MLA_W, BF16), (LANES, F32), (LANES, F32)], name="rope")

    o_b, lse = _mla_forward(q_cat, k_cat, kv, s_len)

    def f_gate(oa, za, ob, zb):
        return (oa * _silu(za), ob * _silu(zb)), ()

    ya_in, yb_in = _rowwise(f_gate, [o_a, (proj, O_SBZ // SB_W, SB_W), o_b, (proj, O_MLAZ // MLA_W, MLA_W)], [],
                            [(SB_W, BF16), (MLA_W, BF16)], name="branch_gate")
    y_a = _mm(ya_in, w_a_f, name="branch_a")

    def f_merge(yb, ga, gb, ya):
        return (yb, _sigmoid(ga) * ya + _sigmoid(gb) * yb), ()

    y_b, merged = _mm(yb_in, w_b_f, name="branch_b_merge", tiles=(min(s_len, 512), D, MLA_W),
                      epilogue=(f_merge, [(proj, O_GA // D, D), (proj, O_GB // D, D), y_a], [], [(D, F32), (D, BF16)], []))

    def f_loss(out_, x_, t_, mg, bg, gf_):
        gate = mg + bg
        x2 = x_ + gate * out_
        r2 = _rms(x2)
        xn2 = x2 * r2
        err = xn2 * gf_ - t_
        loss = jnp.full((1, LANES), 0.5 / D, F32) * jnp.sum(err * err)
        dy = err * (1.0 / D)
        dx2 = _rms_bwd(xn2, r2, dy * gf_)
        return (dx2, dx2 * gate), (loss, _colsum(dy * xn2), _colsum(dx2 * out_))

    dx2, d_out, loss_part, d_gf, d_gate = _mm(
        merged, w_out_f, name="out_proj_loss", tiles=(min(s_len, 512), D, D),
        epilogue=(f_loss, [x2d, tgt], [mod_gate, b_gate, gf], [(D, F32), (D, BF16)], [LANES, D, D]))

    dw_out = _mm(merged, d_out, ta=True, name="dw_out")

    def f_dmerge(dm, ga, gb, ya, yb):
        sa, sb = _sigmoid(ga), _sigmoid(gb)
        return (dm * sa, dm * sb, dm * ya * sa * (1.0 - sa), dm * yb * sb * (1.0 - sb)), ()

    d_ya, d_yb, d_ga, d_gb = _mm(
        d_out, w_out_f, tb=True, name="d_merge", tiles=(min(s_len, 256), D, D),
        epilogue=(f_dmerge, [(proj, O_GA // D, D), (proj, O_GB // D, D), y_a, y_b], [], [(D, BF16)] * 4, []))
    dw_a = _mm(ya_in, d_ya, ta=True, name="dw_branch_a")
    dw_b = _mm(yb_in, d_yb, ta=True, name="dw_branch_b")

    def f_dgate(d_in, o_, z_):
        return (d_in * _silu(z_), d_in * o_ * _silu_grad(z_)), ()

    d_oa, d_sbz = _mm(d_ya, w_a_f, tb=True, name="d_branch_a",
                      epilogue=(f_dgate, [o_a, (proj, O_SBZ // SB_W, SB_W)], [], [(SB_W, F32), (SB_W, BF16)], []))
    d_ob, d_mlaz = _mm(d_yb, w_b_f, tb=True, name="d_branch_b",
                       epilogue=(f_dgate, [o_b, (proj, O_MLAZ // MLA_W, MLA_W)], [], [(MLA_W, F32), (MLA_W, BF16)], []))

    dq_cat, dk_cat, dv_b = _mla_backward(q_cat, k_cat, kv, o_b, lse, d_ob, s_len)

    def f_drope(dq, dk, dv_, cs, sn):
        lane = lax.broadcasted_iota(jnp.int32, cs.shape, 1)
        first, second = lane < ROPE, (lane >= ROPE) & (lane < 2 * ROPE)
        dq1 = dq2 = dk1 = dk2 = None
        for p in range(pairs):
            q_rot = dq[:, LANES * (2 * p + 1):LANES * (2 * p + 2)]
            k_rot = dk[:, LANES * (2 * p + 1):LANES * (2 * p + 2)]
            parts = (pltpu.roll(jnp.where(first, q_rot, 0.0), (ROPE * p) % LANES, 1),
                     pltpu.roll(jnp.where(second, q_rot, 0.0), (LANES - ROPE + ROPE * p) % LANES, 1),
                     jnp.where(first, k_rot, 0.0), jnp.where(second, k_rot, 0.0))
            if p == 0:
                dq1, dq2, dk1, dk2 = parts
            else:
                dq1, dq2, dk1, dk2 = dq1 + parts[0], dq2 + parts[1], dk1 + parts[2], dk2 + parts[3]
        dqn_ = [dq[:, 2 * LANES * p:2 * LANES * p + LANES] for p in range(pairs)]
        dkn_ = [dk[:, 2 * LANES * p:2 * LANES * p + LANES] for p in range(pairs)]
        return (jnp.concatenate(dqn_ + [dq1 * cs + dq2 * sn, dq2 * cs - dq1 * sn], axis=1),
                jnp.concatenate(dkn_ + [dv_], axis=1),
                jnp.concatenate([dk1 * cs + dk2 * sn, dk2 * cs - dk1 * sn], axis=1)), ()

    dq_k, dkv_k, d_krot = _rowwise(f_drope, [dq_cat, dk_cat, dv_b, cos_t, sin_t], [],
                                   [(HEADS * QK_DIM, BF16), (2 * MLA_W, BF16), (2 * LANES, BF16)], name="d_rope")
    dw_uq_k = _mm(cq_n, dq_k, ta=True, name="dw_uq")
    dw_ukv_k = _mm(ckv_n, dkv_k, ta=True, name="dw_ukv")

    def f_dlat(d_normed, latent, gain):
        r = _rms(latent)
        normed = latent * r
        return (_rms_bwd(normed, r, d_normed * gain),), (_colsum(d_normed * normed),)

    d_cq, d_gq = _mm(dq_k, w_uq_k, tb=True, name="d_cq_norm",
                     epilogue=(f_dlat, [(proj, O_CQ // Q_RANK, Q_RANK)], [gq], [(Q_RANK, BF16)], [Q_RANK]))
    d_ckv, d_gkv = _mm(dkv_k, w_ukv_k, tb=True, name="d_ckv_norm",
                       epilogue=(f_dlat, [(proj, O_CKV // KV_RANK, KV_RANK)], [gkv], [(KV_RANK, BF16)], [KV_RANK]))

    def col_blocks(g):
        kdim, n8 = g.shape
        return g.astype(BF16).reshape(kdim, N_DEV, n8 // N_DEV).transpose(1, 0, 2).reshape(N_DEV, -1, LANES)

    g_blocks = jnp.concatenate([col_blocks(_uq_from_kernel_layout(dw_uq_k)), col_blocks(_ukv_from_kernel_layout(dw_ukv_k)),
                                col_blocks(dw_a), col_blocks(dw_b), dw_out.astype(BF16).reshape(N_DEV, -1, LANES)], axis=1)
    (d_sbq, d_sbk, d_sbv), (g_recv,) = _sb_backward(proj, sb_tails, d_oa, s_len, _Exchange([g_blocks]))

    d_proj = jnp.concatenate([d_ga, d_gb, d_sbq.astype(BF16), d_sbk.astype(BF16), d_sbv.astype(BF16), d_sbz, d_mlaz,
                              d_cq, jnp.zeros((s_len, LANES), BF16), d_ckv, d_krot], axis=1)
    dw_in_kt = _mm(d_proj, h, ta=True, out_dtype=BF16, name="dw_in", tiles=(512, D, s_len))

    def krot_body(t_ref, o_ref):
        half = ROPE // 2
        for part in range(2):
            acc = t_ref[part * LANES:part * LANES + half, :].astype(F32)
            for hh in range(1, HEADS):
                acc = acc + t_ref[part * LANES + hh * half:part * LANES + (hh + 1) * half, :].astype(F32)
            o_ref[part * half:(part + 1) * half, :] = acc

    dw_krot = pl.pallas_call(krot_body, name="dw_krot_sum", out_shape=jax.ShapeDtypeStruct((ROPE, D), F32))(
        dw_in_kt[O_KROT:O_KROT + 2 * LANES])

    g_in_blocks = _w_in_t_from_kernel_layout(dw_in_kt, dw_krot).reshape(N_DEV, -1, D)
    def f_dx(dh_, x_, dx2_, g1_, msc, bsc):
        r = _rms(x_)
        xn = x_ * r
        dn1 = dh_ * (1.0 + (msc + bsc))
        return ((dx2_ + _rms_bwd(xn, r, dn1 * g1_),),
                (_colsum(dh_), _colsum(dh_ * (xn * g1_)), _colsum(dn1 * xn)))

    (grad_x2d, d_shift, d_scale, d_g1), (g_in_recv,) = _mm(
        d_proj, w_in_kt, name="d_h", tiles=(min(s_len, 512), D, 512), exchange=_Exchange([g_in_blocks]),
        epilogue=(f_dx, [x2d, dx2], [g1, mod_scale, b_scale], [(D, F32)], [D, D, D]))

    g_in_sum_t = _sum_blocks(g_in_recv, "sum_grads_w_in")
    g_sum = _sum_blocks(g_recv, "sum_grads")
    g_big = [g_sum[offs[t]:offs[t + 1]].reshape(big[t].shape) for t in range(5)]

    small = jnp.concatenate([d_shift, d_scale, d_gate, d_g1, d_gq, d_gkv, d_gf, loss_part], axis=1)
    n_small = small.shape[1]
    pad = (-n_small) % (8 * LANES)
    small = jnp.pad(small, ((0, 0), (0, pad))).reshape(-1, LANES)
    small_all = _all_gather(small, "gather_small")
    small_sum = _sum_blocks(small_all, "sum_small").reshape(1, -1)
    g_b_ada = small_sum[:, :3 * D]
    g_g1 = small_sum[:, 3 * D:4 * D]
    g_gq = small_sum[:, 4 * D:4 * D + Q_RANK]
    g_gkv = small_sum[:, 4 * D + Q_RANK:4 * D + Q_RANK + KV_RANK]
    g_gf = small_sum[:, 4 * D + Q_RANK + KV_RANK:4 * D + Q_RANK + KV_RANK + D]

    dmod_all = small_all.reshape(N_DEV, -1)[:, :3 * D]
    dmod_cols = lax.dynamic_slice_in_dim(dmod_all, me * (3 * D // N_DEV), 3 * D // N_DEV, axis=1)
    g_w_ada = _mm(c_all, dmod_cols, ta=True, name="dw_ada")

    loss = small_sum[0, n_small - LANES]

    names = ["w_ada", "b_ada", "norm_gain", "w_in", "q_norm_gain", "w_uq", "kv_norm_gain", "w_ukv",
             "w_branch_a", "w_branch_b", "w_out", "final_norm_gain"]
    weights = dict(w_ada=w_ada, b_ada=b_ada, norm_gain=norm_gain, w_in=w_in, q_norm_gain=q_norm_gain, w_uq=w_uq,
                   kv_norm_gain=kv_norm_gain, w_ukv=w_ukv, w_branch_a=w_branch_a, w_branch_b=w_branch_b, w_out=w_out,
                   final_norm_gain=final_norm_gain)
    moms = dict(w_ada=m_w_ada, b_ada=m_b_ada, norm_gain=m_norm_gain, w_in=m_w_in, q_norm_gain=m_q_norm_gain,
                w_uq=m_w_uq, kv_norm_gain=m_kv_norm_gain, w_ukv=m_w_ukv, w_branch_a=m_w_branch_a,
                w_branch_b=m_w_branch_b, w_out=m_w_out, final_norm_gain=m_final_norm_gain)
    vels = dict(w_ada=v_w_ada, b_ada=v_b_ada, norm_gain=v_norm_gain, w_in=v_w_in, q_norm_gain=v_q_norm_gain,
                w_uq=v_w_uq, kv_norm_gain=v_kv_norm_gain, w_ukv=v_w_ukv, w_branch_a=v_w_branch_a,
                w_branch_b=v_w_branch_b, w_out=v_w_out, final_norm_gain=v_final_norm_gain)
    grads2d = dict(w_ada=g_w_ada, b_ada=g_b_ada, norm_gain=g_g1, w_in=g_in_sum_t, q_norm_gain=g_gq, w_uq=g_big[0],
                   kv_norm_gain=g_gkv, w_ukv=g_big[1], w_branch_a=g_big[2], w_branch_b=g_big[3], w_out=g_big[4],
                   final_norm_gain=g_gf)

    grads, deltas, new_m, new_v = [], [], [], []
    for n in names:
        w = weights[n]
        if n == "w_in":
            to2d = lambda t: t[0].T if t.ndim == 3 else t
            back = lambda t: t.T[None]
        else:
            shape2d = grads2d[n].shape
            to2d = lambda t, s=shape2d: t.reshape(s)
            back = lambda t, s=w.shape: t.reshape(s)
        g2d = to2d(grads2d[n])
        d_, m_, v_ = _adamw_call(to2d(w), g2d, to2d(moms[n]), to2d(vels[n]), "adamw_" + n)
        grads.append(back(g2d))
        deltas.append(back(d_))
        new_m.append(back(m_))
        new_v.append(back(v_))

    return (loss, grad_x2d.reshape(x.shape), *grads, *deltas, *new_m, *new_v)
```

```python
import functools
import math

import jax
import jax.numpy as jnp
from jax import lax
from jax.experimental import pallas as pl
from jax.experimental.pallas import tpu as pltpu

F32 = jnp.float32
BF16 = jnp.bfloat16
MXU_DTYPE = jnp.bfloat16

N_DEV = 8
D = 1024
HEADS = 8
HD = 64
SB_W = 512
MLA_W = 512
Q_RANK = 384
KV_RANK = 256
ROPE = 32
NOPE = 64
QK_DIM = NOPE + ROPE
EPS = 1e-6
ROPE_BASE = 10000.0

ADAM_LR = 0.001
ADAM_B1 = 0.9
ADAM_B2 = 0.999
ADAM_EPS = 1e-08
ADAM_WD = 0.01
ADAM_STEP = 10

LANES = 128
VMEM_LIMIT = 48 * 1024 * 1024

O_GA, O_GB = 0, 1024
O_SBQ, O_SBK, O_SBV, O_SBZ = 2048, 2560, 3072, 3584
O_MLAZ = 4096
O_CQ = 4608
O_CKV = 5120
O_KROT = 5376
IN_PAD = 5632

BQ = 256
BK = 256


def _cparams(*sem):
    return pltpu.CompilerParams(dimension_semantics=sem, vmem_limit_bytes=VMEM_LIMIT)


def _tile_of(n, cap=512):
    if n <= cap:
        return n
    for t in (1024, 768, 512, 384, 256, 128):
        if t <= cap and n % t == 0:
            return t
    raise ValueError(n)


def _rowwise(fn, rows, vecs, outs, reds=(), *, name, tile=512):
    norm = []
    for r in rows:
        if isinstance(r, tuple):
            arr, cb, w = r[:3]
            ro = r[3] if len(r) > 3 else 0
        else:
            arr, cb, w, ro = r, 0, r.shape[1], 0
        norm.append((arr, cb, w, ro))
    s_len = norm[0][0].shape[0]
    tile = min(tile, s_len)
    assert s_len % tile == 0
    n_row, n_vec, n_out, n_red = len(norm), len(vecs), len(outs), len(reds)

    def body(*refs):
        step = pl.program_id(0)
        row_refs = refs[:n_row]
        vec_refs = refs[n_row:n_row + n_vec]
        out_refs = refs[n_row + n_vec:n_row + n_vec + n_out]
        red_refs = refs[n_row + n_vec + n_out:]
        row_res, red_res = fn(*[r[...] for r in row_refs], *[v[...] for v in vec_refs])
        for o, val in zip(out_refs, row_res):
            o[...] = val.astype(o.dtype)
        if n_red:
            @pl.when(step == 0)
            def _():
                for r in red_refs:
                    r[...] = jnp.zeros(r.shape, r.dtype)
            for r, val in zip(red_refs, red_res):
                r[...] += val

    in_specs = []
    for arr, cb, w, ro in norm:
        in_specs.append(pl.BlockSpec((tile, w), functools.partial(lambda i, cb, rb: (i + rb, cb), cb=cb, rb=ro // tile)))
        assert ro % tile == 0
    for v in vecs:
        in_specs.append(pl.BlockSpec(v.shape, lambda i: (0, 0)))
    out_shape = [jax.ShapeDtypeStruct((s_len, w), dt) for w, dt in outs]
    out_specs = [pl.BlockSpec((tile, w), lambda i: (i, 0)) for w, _ in outs]
    out_shape += [jax.ShapeDtypeStruct((1, w), F32) for w in reds]
    out_specs += [pl.BlockSpec((1, w), lambda i: (0, 0)) for w in reds]
    res = pl.pallas_call(
        body, name=name, grid=(s_len // tile,), in_specs=in_specs, out_specs=out_specs, out_shape=out_shape,
        compiler_params=_cparams("arbitrary" if n_red else "parallel"),
    )(*[a for a, _, _, _ in norm], *vecs)
    return res


def _mm(a, b, *, ta=False, tb=False, out_dtype=F32, name, exchange=None, tiles=None, epilogue=None):
    m, k = (a.shape[1], a.shape[0]) if ta else a.shape
    n = b.shape[0] if tb else b.shape[1]
    assert (b.shape[1] if tb else b.shape[0]) == k
    tm, tn, tk = tiles or (_tile_of(m, 1024), _tile_of(n, 1024 if n <= 1024 else 512), _tile_of(k, 1024))
    assert m % tm == 0 and n % tn == 0 and k % tk == 0
    ni, nj, nk = m // tm, n // tn, k // tk
    dims = (((0 if ta else 1,), (1 if tb else 0,)), ((), ()))
    n_ex = len(exchange.arrays) if exchange else 0
    fn, rows, vecs, outs, reds = epilogue or (None, (), (), (), ())
    rows = [r if isinstance(r, tuple) else (r, 0, r.shape[1]) for r in rows]
    assert not epilogue or tn == n
    n_res = len(outs) + len(reds) if epilogue else 1

    def body(*refs):
        a_ref, b_ref = refs[:2]
        row_refs, refs = refs[2:2 + len(rows)], refs[2 + len(rows):]
        vec_refs, refs = refs[:len(vecs)], refs[len(vecs):]
        x_refs, refs = refs[:n_ex], refs[n_ex:]
        res_refs, refs = refs[:n_res], refs[n_res:]
        out_refs, acc_ref, sems = refs[:n_ex], refs[n_ex], refs[n_ex + 1:]
        i, j, kk = pl.program_id(0), pl.program_id(1), pl.program_id(2)
        first = (i == 0) & (j == 0) & (kk == 0)

        if exchange:
            @pl.when(first)
            def _():
                exchange.start(x_refs, out_refs, sems)

        if reds:
            @pl.when(first)
            def _():
                for r in res_refs[len(outs):]:
                    r[...] = jnp.zeros(r.shape, r.dtype)

        @pl.when(kk == 0)
        def _():
            acc_ref[...] = jnp.zeros(acc_ref.shape, F32)

        acc_ref[...] += lax.dot_general(a_ref[...].astype(MXU_DTYPE), b_ref[...].astype(MXU_DTYPE), dims,
                                        preferred_element_type=F32)

        @pl.when(kk == nk - 1)
        def _():
            if not epilogue:
                res_refs[0][...] = acc_ref[...].astype(res_refs[0].dtype)
                return
            row_res, red_res = fn(acc_ref[...], *[r[...] for r in row_refs], *[v[...] for v in vec_refs])
            for o, val in zip(res_refs, row_res):
                o[...] = val.astype(o.dtype)
            for r, val in zip(res_refs[len(outs):], red_res):
                r[...] += val

        if exchange:
            @pl.when((i == ni - 1) & (j == nj - 1) & (kk == nk - 1))
            def _():
                exchange.wait(x_refs, out_refs, sems)

    a_spec = pl.BlockSpec((tk, tm), lambda i, j, kk: (kk, i)) if ta else pl.BlockSpec((tm, tk), lambda i, j, kk: (i, kk))
    b_spec = pl.BlockSpec((tn, tk), lambda i, j, kk: (j, kk)) if tb else pl.BlockSpec((tk, tn), lambda i, j, kk: (kk, j))
    in_specs = [a_spec, b_spec]
    in_specs += [pl.BlockSpec((tm, w), functools.partial(lambda i, j, kk, cb: (i, cb), cb=cb)) for _, cb, w in rows]
    in_specs += [pl.BlockSpec(v.shape, lambda i, j, kk: (0, 0)) for v in vecs]
    if epilogue:
        res_specs = [pl.BlockSpec((tm, w), lambda i, j, kk: (i, 0)) for w, _ in outs]
        res_specs += [pl.BlockSpec((1, w), lambda i, j, kk: (0, 0)) for w in reds]
        res_shape = [jax.ShapeDtypeStruct((m, w), dt) for w, dt in outs] + [jax.ShapeDtypeStruct((1, w), F32) for w in reds]
    else:
        res_specs = [pl.BlockSpec((tm, tn), lambda i, j, kk: (i, j))]
        res_shape = [jax.ShapeDtypeStruct((m, n), out_dtype)]
    ordered = bool(exchange or reds)
    res = pl.pallas_call(
        body, name=name, grid=(ni, nj, nk),
        in_specs=in_specs + (exchange.in_specs if exchange else []),
        out_specs=res_specs + (exchange.out_specs if exchange else []),
        out_shape=res_shape + (exchange.out_shape if exchange else []),
        scratch_shapes=[pltpu.VMEM((tm, tn), F32)] + (exchange.scratch if exchange else []),
        compiler_params=_cparams(*(("arbitrary",) * 3 if ordered else ("parallel", "parallel", "arbitrary"))),
    )(a, b, *[r[0] for r in rows], *vecs, *(exchange.arrays if exchange else []))
    main = res[:n_res] if epilogue else res[0]
    return (main, res[n_res:]) if exchange else main


_NT = (((1,), (1,)), ((), ()))
_TN = (((0,), (0,)), ((), ()))


def _dot(a, b):
    return jnp.dot(a, b, preferred_element_type=F32)


def _dot_nt(a, b):
    return lax.dot_general(a, b, _NT, preferred_element_type=F32)


def _dot_tn(a, b):
    return lax.dot_general(a, b, _TN, preferred_element_type=F32)


def _running_sum(x, tri):
    return _dot(x.astype(MXU_DTYPE), tri)


def _neg_softplus(z):
    u = jnp.exp2(jnp.abs(z) * (-1.0 / math.log(2.0)))
    return -jnp.maximum(z, 0.0) - jnp.log(1.0 + u)


def _walk_blocks(step, st, n, descending, group=2):
    done = 0
    size = group
    while size >= 1:
        def trip(t, s, size=size, done=done):
            js = [done + size * t + g for g in range(size)]
            return step([n - 1 - j for j in js] if descending else js, s)

        trips = (n - done) // size
        st = lax.fori_loop(0, trips, trip, st)
        done = done + size * trips
        size //= 2
    return st


def _chains(js):
    return [(h, t) for t in range(len(js)) for h in range(2)]


def _rowsum(x):
    return jnp.sum(x, axis=1, keepdims=True)


def _attn_consts():
    row = lax.broadcasted_iota(jnp.int32, (BQ, BK), 0)
    col = lax.broadcasted_iota(jnp.int32, (BQ, BK), 1)
    trow = lax.broadcasted_iota(jnp.int32, (BK, BK), 0)
    tcol = lax.broadcasted_iota(jnp.int32, (BK, BK), 1)
    lane = lax.broadcasted_iota(jnp.int32, (BQ, LANES), 1)
    klane = lax.broadcasted_iota(jnp.int32, (BK, LANES), 1)
    return row, col, trow, tcol, lane, klane


assert BQ == BK


def _diag_step(jd, descending, diag_mask, group=4):
    below = list(range(group)) if descending else list(reversed(range(group)))
    js = [jnp.maximum(jd - o, 0) for o in below]
    masks = [diag_mask if o == 0 else jd - o >= 0 for o in below]
    return js, masks, jnp.maximum(jd - (group - 1), 0)


def _key_slice(j):
    return pl.ds(pl.multiple_of(j * BK, BK), BK)


def _sb_forward(proj, s_len, exchange):
    nq = s_len // BQ
    assert s_len // BK <= HD
    qc, kc, vc = O_SBQ // LANES, O_SBK // LANES, O_SBV // LANES
    n_ex = len(exchange.arrays)

    def body(q_ref, k_ref, v_ref, *rest):
        x_refs, (o_ref, tails_ref) = rest[:n_ex], rest[n_ex:n_ex + 2]
        out_refs, sems = rest[n_ex + 2:2 * n_ex + 2], rest[2 * n_ex + 2:]
        p = pl.program_id(0)
        i = pl.program_id(1)
        jd = i

        @pl.when((p == 0) & (i == 0))
        def _():
            exchange.start(x_refs, out_refs, sems)

        row, col, trow, tcol, lane, klane = _attn_consts()
        tri = (trow >= tcol).astype(MXU_DTYPE)
        q = q_ref[...] * 0.125
        qh = [jnp.where(lane < HD, q, 0.0).astype(MXU_DTYPE), jnp.where(lane >= HD, q, 0.0).astype(MXU_DTYPE)]

        km = [klane < HD, klane >= HD]

        def step(js, st, masks):
            carry, acc, tail = st
            chains = _chains(js)
            kj = [k_ref[_key_slice(j), :].astype(MXU_DTYPE) for j in js]
            vj = [v_ref[_key_slice(j), :].astype(MXU_DTYPE) for j in js]
            z = {(h, t): _dot_nt(qh[h], kj[t]) for h, t in chains}
            run = list(carry)
            suf, carry_in = {}, {}
            for h, t in chains:
                lom = _neg_softplus(z[h, t])
                if masks:
                    lom = jnp.where(masks[t], lom, 0.0)
                suf[h, t] = _running_sum(lom, tri)
                carry_in[h, t] = run[h]
                run[h] = run[h] + _rowsum(lom)
            for h, t in chains:
                a = jnp.exp(z[h, t] + suf[h, t] + carry_in[h, t])
                if masks:
                    a = jnp.where(masks[t], a, 0.0)
                acc = acc + _dot(a.astype(MXU_DTYPE), jnp.where(km[h], vj[t], 0))
                tail_lane = js[t] if not masks or masks[t].ndim else jnp.where(masks[t], js[t], -LANES)
                tail = jnp.where(lane == h * HD + tail_lane, carry_in[h, t], tail)
            return tuple(run), acc, tail

        zero = jnp.zeros((BQ, LANES), F32)
        diag_js, masks, left = _diag_step(jd, True, col < row, group=2)
        st = step(diag_js, ((jnp.zeros((BQ, 1), F32),) * 2, zero, zero), masks)
        st = _walk_blocks(lambda js, s: step(js, s, None), st, left, True, group=8)
        o_ref[...] = st[1]
        tails_ref[...] = st[2]

        @pl.when((p == HEADS // 2 - 1) & (i == nq - 1))
        def _():
            exchange.wait(x_refs, out_refs, sems)

    blk = pl.BlockSpec((BQ, LANES), lambda p, i: (i, p))
    out = jax.ShapeDtypeStruct((s_len, SB_W), F32)
    res = pl.pallas_call(
        body, name="sb_fwd", grid=(HEADS // 2, nq),
        in_specs=[pl.BlockSpec((BQ, LANES), lambda p, i: (i, qc + p)),
                  pl.BlockSpec((s_len, LANES), lambda p, i: (0, kc + p)),
                  pl.BlockSpec((s_len, LANES), lambda p, i: (0, vc + p))] + exchange.in_specs,
        out_specs=[blk, blk] + exchange.out_specs, out_shape=[out, out] + exchange.out_shape,
        scratch_shapes=exchange.scratch,
        compiler_params=_cparams("arbitrary", "arbitrary"),
    )(proj, proj, proj, *exchange.arrays)
    return res[:2], res[2:]


def _sb_backward(proj, tails, do, s_len, exchange):
    nq = s_len // BQ
    qc, kc, vc = O_SBQ // LANES, O_SBK // LANES, O_SBV // LANES
    n_ex = len(exchange.arrays)

    def body(q_ref, k_ref, v_ref, tails_ref, do_ref, *rest):
        x_refs, (dq_ref, dk_ref, dv_ref) = rest[:n_ex], rest[n_ex:n_ex + 3]
        out_refs, sems = rest[n_ex + 3:2 * n_ex + 3], rest[2 * n_ex + 3:]
        p = pl.program_id(0)
        i = pl.program_id(1)
        jd = i

        @pl.when((p == 0) & (i == 0))
        def _():
            exchange.start(x_refs, out_refs, sems)

        @pl.when(i == 0)
        def _():
            dk_ref[...] = jnp.zeros(dk_ref.shape, F32)
            dv_ref[...] = jnp.zeros(dv_ref.shape, F32)

        row, col, trow, tcol, lane, klane = _attn_consts()
        tri = (trow >= tcol).astype(MXU_DTYPE)
        tri_p = (trow <= tcol).astype(MXU_DTYPE)
        q = q_ref[...] * 0.125
        tails_blk = tails_ref[...]
        do_blk = do_ref[...]
        hm = [lane < HD, lane >= HD]
        km = [klane < HD, klane >= HD]
        qh = [jnp.where(m, q, 0.0).astype(MXU_DTYPE) for m in hm]
        doh = [jnp.where(m, do_blk, 0.0).astype(MXU_DTYPE) for m in hm]

        def step(js, st, masks):
            before, dq = st
            chains = _chains(js)
            kj = [k_ref[_key_slice(j), :].astype(MXU_DTYPE) for j in js]
            vj = [v_ref[_key_slice(j), :].astype(MXU_DTYPE) for j in js]
            z = {(h, t): _dot_nt(qh[h], kj[t]) for h, t in chains}
            da = {(h, t): _dot_nt(doh[h], vj[t]) for h, t in chains}
            suf, sig = {}, {}
            for h, t in chains:
                lom = _neg_softplus(z[h, t])
                if masks:
                    lom = jnp.where(masks[t], lom, 0.0)
                suf[h, t] = _running_sum(lom, tri)
                sig[h, t] = jnp.exp(z[h, t] + lom)
            run = list(before)
            dl, pre, before_in = {}, {}, {}
            dk_add, dv_add = [None] * len(js), [None] * len(js)
            for h, t in chains:
                tail = _rowsum(jnp.where(lane == h * HD + js[t], tails_blk, 0.0))
                a = jnp.exp(z[h, t] + suf[h, t] + tail)
                if masks:
                    a = jnp.where(masks[t], a, 0.0)
                dl[h, t] = da[h, t] * a
                pre[h, t] = _dot(dl[h, t].astype(MXU_DTYPE), tri_p)
                dv_h = _dot_tn(a.astype(MXU_DTYPE), doh[h])
                dv_add[t] = dv_h if dv_add[t] is None else dv_add[t] + dv_h
                before_in[h, t] = run[h]
                run[h] = run[h] + _rowsum(dl[h, t])
            for h, t in chains:
                upto = before_in[h, t] + pre[h, t]
                dz = dl[h, t] - sig[h, t] * upto
                if masks:
                    dz = jnp.where(masks[t], dz, 0.0)
                dzb = dz.astype(MXU_DTYPE)
                dq = dq + _dot(dzb, jnp.where(km[h], kj[t], 0))
                dk_h = _dot_tn(dzb, qh[h])
                dk_add[t] = dk_h if dk_add[t] is None else dk_add[t] + dk_h
            for t, j in enumerate(js):
                dk_ref[_key_slice(j), :] += dk_add[t]
                dv_ref[_key_slice(j), :] += dv_add[t]
            return tuple(run), dq

        st = ((jnp.zeros((BQ, 1), F32),) * 2, jnp.zeros((BQ, LANES), F32))
        diag_js, masks, left = _diag_step(jd, False, col < row, group=2)
        st = _walk_blocks(lambda js, s: step(js, s, None), st, left, False, group=8)
        st = step(diag_js, st, masks)
        dq_ref[...] = st[1] * 0.125

        @pl.when((p == HEADS // 2 - 1) & (i == nq - 1))
        def _():
            exchange.wait(x_refs, out_refs, sems)

    blk = pl.BlockSpec((BQ, LANES), lambda p, i: (i, p))
    full = pl.BlockSpec((s_len, LANES), lambda p, i: (0, p))
    out = jax.ShapeDtypeStruct((s_len, SB_W), F32)
    res = pl.pallas_call(
        body, name="sb_bwd", grid=(HEADS // 2, nq),
        in_specs=[pl.BlockSpec((BQ, LANES), lambda p, i: (i, qc + p)),
                  pl.BlockSpec((s_len, LANES), lambda p, i: (0, kc + p)),
                  pl.BlockSpec((s_len, LANES), lambda p, i: (0, vc + p)),
                  blk, blk] + exchange.in_specs,
        out_specs=[blk, full, full] + exchange.out_specs, out_shape=[out, out, out] + exchange.out_shape,
        scratch_shapes=exchange.scratch,
        compiler_params=_cparams("arbitrary", "arbitrary"),
    )(proj, proj, proj, tails, do, *exchange.arrays)
    return res[:3], res[3:]


def _pair_mask(rows, h):
    lane = lax.broadcasted_iota(jnp.int32, (rows, 2 * LANES), 1)
    rot = lane - LANES
    return (((lane < LANES) & (lane // HD == h))
            | ((lane >= LANES) & (rot < 2 * ROPE) & ((rot // (ROPE // 2)) % 2 == h)))


def _mla_forward(q_cat, k_cat, kv, s_len):
    nq = s_len // BQ
    scale = 1.0 / math.sqrt(QK_DIM)

    def body(q_ref, k_ref, v_ref, o_ref, lse_ref):
        i = pl.program_id(1)
        jd = i
        row, col, trow, tcol, lane, klane = _attn_consts()
        q = q_ref[...]
        hm = [lane < HD, lane >= HD]
        km = [klane < HD, klane >= HD]
        qh = [jnp.where(_pair_mask(BQ, h), q, 0) for h in range(2)]

        def step(js, st, masks):
            m_run, l_run, acc = st
            chains = _chains(js)
            kj = [k_ref[_key_slice(j), :] for j in js]
            vj = [v_ref[_key_slice(j), :].astype(MXU_DTYPE) for j in js]
            s = {}
            for h, t in chains:
                s[h, t] = _dot_nt(qh[h], kj[t]) * scale
                if masks:
                    s[h, t] = jnp.where(masks[t], s[h, t], -jnp.inf)
            m_new, alpha, l_new = [], [], []
            for h in range(2):
                top = m_run[h]
                for t in range(len(js)):
                    top = jnp.maximum(top, jnp.max(s[h, t], axis=1, keepdims=True))
                m_new.append(top)
                alpha.append(jnp.exp(m_run[h] - top))
                l_new.append(alpha[h] * l_run[h])
            add = None
            for h, t in chains:
                pr = jnp.exp(s[h, t] - m_new[h])
                l_new[h] = l_new[h] + _rowsum(pr)
                part = _dot(pr.astype(MXU_DTYPE), jnp.where(km[h], vj[t], 0))
                add = part if add is None else add + part
            acc = jnp.where(hm[0], alpha[0], alpha[1]) * acc + add
            return tuple(m_new), tuple(l_new), acc

        st = ((jnp.full((BQ, 1), -1e30, F32),) * 2, (jnp.zeros((BQ, 1), F32),) * 2, jnp.zeros((BQ, LANES), F32))
        diag_js, masks, left = _diag_step(jd, True, col <= row)
        st = step(diag_js, st, masks)
        m_run, l_run, acc = _walk_blocks(lambda js, s: step(js, s, None), st, left, True, group=8)
        o_ref[...] = acc / jnp.where(hm[0], l_run[0], l_run[1])
        lse_ref[...] = jnp.where(hm[0], m_run[0] + jnp.log(l_run[0]), m_run[1] + jnp.log(l_run[1]))

    blk = pl.BlockSpec((BQ, LANES), lambda p, i: (i, p))
    out = jax.ShapeDtypeStruct((s_len, MLA_W), F32)
    return pl.pallas_call(
        body, name="mla_fwd", grid=(HEADS // 2, nq),
        in_specs=[pl.BlockSpec((BQ, 2 * LANES), lambda p, i: (i, p)),
                  pl.BlockSpec((s_len, 2 * LANES), lambda p, i: (0, p)),
                  pl.BlockSpec((s_len, LANES), lambda p, i: (0, MLA_W // LANES + p))],
        out_specs=[blk, blk], out_shape=[out, out],
        compiler_params=_cparams("parallel", "parallel"),
    )(q_cat, k_cat, kv)


def _mla_backward(q_cat, k_cat, kv, o, lse, do, s_len):
    nq = s_len // BQ
    scale = 1.0 / math.sqrt(QK_DIM)

    def body(q_ref, k_ref, v_ref, o_ref, lse_ref, do_ref, dq_ref, dk_ref, dv_ref):
        i = pl.program_id(1)

        @pl.when(i == 0)
        def _():
            dk_ref[...] = jnp.zeros(dk_ref.shape, F32)
            dv_ref[...] = jnp.zeros(dv_ref.shape, F32)

        jd = i
        row, col, trow, tcol, lane, klane = _attn_consts()
        q = q_ref[...]
        o_blk = o_ref[...]
        do_blk = do_ref[...]
        lse_blk = lse_ref[...]
        hm = [lane < HD, lane >= HD]
        kpm = [_pair_mask(BK, h) for h in range(2)]
        qh = [jnp.where(_pair_mask(BQ, h), q, 0) for h in range(2)]
        doh_f = [jnp.where(m, do_blk, 0.0) for m in hm]
        doh = [d.astype(MXU_DTYPE) for d in doh_f]
        delta = [jnp.sum(d * o_blk, axis=1, keepdims=True) for d in doh_f]
        lse_h = [jnp.sum(jnp.where(lane == h * HD, lse_blk, 0.0), axis=1, keepdims=True) for h in range(2)]

        def step(js, st, masks):
            dq = st
            chains = _chains(js)
            kj = [k_ref[_key_slice(j), :] for j in js]
            vj = [v_ref[_key_slice(j), :].astype(MXU_DTYPE) for j in js]
            s = {(h, t): _dot_nt(qh[h], kj[t]) for h, t in chains}
            dp = {(h, t): _dot_nt(doh[h], vj[t]) for h, t in chains}
            adds = [[None] * len(js) for _ in range(2)]

            def accumulate(slot, t, part):
                adds[slot][t] = part if adds[slot][t] is None else adds[slot][t] + part

            for h, t in chains:
                pr = jnp.exp(s[h, t] * scale - lse_h[h])
                if masks:
                    pr = jnp.where(masks[t], pr, 0.0)
                dsb = (pr * (dp[h, t] - delta[h]) * scale).astype(MXU_DTYPE)
                dq = dq + _dot(dsb, jnp.where(kpm[h], kj[t], 0))
                accumulate(0, t, _dot_tn(dsb, qh[h]))
                accumulate(1, t, _dot_tn(pr.astype(MXU_DTYPE), doh[h]))
            for t, j in enumerate(js):
                dk_ref[_key_slice(j), :] += adds[0][t]
                dv_ref[_key_slice(j), :] += adds[1][t]
            return dq

        diag_js, masks, left = _diag_step(jd, True, col <= row)
        st = step(diag_js, jnp.zeros((BQ, 2 * LANES), F32), masks)
        dq_ref[...] = _walk_blocks(lambda js, s: step(js, s, None), st, left, True, group=8)

    blk = pl.BlockSpec((BQ, LANES), lambda p, i: (i, p))
    full = pl.BlockSpec((s_len, LANES), lambda p, i: (0, p))
    out = jax.ShapeDtypeStruct((s_len, MLA_W), F32)
    out_cat = jax.ShapeDtypeStruct((s_len, 2 * MLA_W), F32)
    return pl.pallas_call(
        body, name="mla_bwd", grid=(HEADS // 2, nq),
        in_specs=[pl.BlockSpec((BQ, 2 * LANES), lambda p, i: (i, p)),
                  pl.BlockSpec((s_len, 2 * LANES), lambda p, i: (0, p)),
                  pl.BlockSpec((s_len, LANES), lambda p, i: (0, MLA_W // LANES + p)),
                  blk, blk, blk],
        out_specs=[pl.BlockSpec((BQ, 2 * LANES), lambda p, i: (i, p)),
                   pl.BlockSpec((s_len, 2 * LANES), lambda p, i: (0, p)), full],
        out_shape=[out_cat, out_cat, out],
        compiler_params=_cparams("arbitrary", "arbitrary"),
    )(q_cat, k_cat, kv, o, lse, do)


def _mesh_pos():
    return lax.axis_index("x"), lax.axis_index("y"), lax.axis_index("c")


def _dev_index(px, py, pc):
    return 4 * px + 2 * py + pc


def _all_gather(block, name):
    return _all_gather_parts([block], name)[0]


def _all_gather_parts(blocks, name):
    n = len(blocks)

    def body(*refs):
        x_refs, out_refs = refs[:n], refs[n:2 * n]
        send_sems, recv_sems, local_sems = refs[2 * n:]
        x, y, c = _mesh_pos()
        me, sibling = (x, y, c), (x, y, 1 - c)
        chips = [(1 - x, y), (x, 1 - y), (1 - x, 1 - y)]

        def copy(a, k, blockpos, to, src=None):
            slot = out_refs[a].at[_dev_index(*blockpos)]
            return pltpu.make_async_remote_copy(
                src_ref=slot if src is None else src, dst_ref=slot,
                send_sem=send_sems.at[7 * a + k], recv_sem=recv_sems.at[7 * a + k],
                device_id=to, device_id_type=pl.DeviceIdType.MESH)

        mine = [pltpu.make_async_copy(x_refs[a], out_refs[a].at[_dev_index(*me)], local_sems.at[a]) for a in range(n)]
        for cp in mine:
            cp.start()
        first = []
        for a in range(n):
            first.append(copy(a, 0, me, sibling, src=x_refs[a]))
            first += [copy(a, 1 + j, me, (*chip, c), src=x_refs[a]) for j, chip in enumerate(chips)]
        for cp in first:
            cp.start()
        passed = []
        for j, chip in enumerate(chips):
            for a in range(n):
                copy(a, 1 + j, (*chip, c), me).wait_recv()
                passed.append(copy(a, 4 + j, (*chip, c), sibling))
                passed[-1].start()
        for a in range(n):
            copy(a, 0, sibling, me).wait_recv()
            for j, chip in enumerate(chips):
                copy(a, 4 + j, (*chip, 1 - c), me).wait_recv()
        for cp in first + passed:
            cp.wait_send()
        for cp in mine:
            cp.wait()

    return pl.pallas_call(
        body, name=name,
        out_shape=[jax.ShapeDtypeStruct((N_DEV,) + b.shape, b.dtype) for b in blocks],
        in_specs=[pl.BlockSpec(memory_space=pl.ANY)] * n, out_specs=[pl.BlockSpec(memory_space=pl.ANY)] * n,
        scratch_shapes=[pltpu.SemaphoreType.DMA((7 * n,)), pltpu.SemaphoreType.DMA((7 * n,)),
                        pltpu.SemaphoreType.DMA((n,))],
    )(*blocks)


class _Exchange:
    def __init__(self, arrays):
        self.arrays = list(arrays)
        n = len(self.arrays)
        self.in_specs = [pl.BlockSpec(memory_space=pl.ANY)] * n
        self.out_specs = [pl.BlockSpec(memory_space=pl.ANY)] * n
        self.out_shape = [jax.ShapeDtypeStruct(a.shape, a.dtype) for a in self.arrays]
        self.scratch = [pltpu.SemaphoreType.DMA((7 * n,)), pltpu.SemaphoreType.DMA((7 * n,)),
                        pltpu.SemaphoreType.DMA((n,))]

    def _copies(self, x_refs, out_refs, sems, with_arrivals):
        send_sems, recv_sems, local_sems = sems
        x, y, c = _mesh_pos()
        me = _dev_index(x, y, c)
        flips = [(fx, fy, fc) for fx in (0, 1) for fy in (0, 1) for fc in (0, 1)][1:]
        peers = [(1 - x if fx else x, 1 - y if fy else y, 1 - c if fc else c) for fx, fy, fc in flips]
        mine, sends, arrivals = [], [], []
        for a in range(len(self.arrays)):
            mine.append(pltpu.make_async_copy(x_refs[a].at[me], out_refs[a].at[me], local_sems.at[a]))
            for k, peer in enumerate(peers):
                sends.append(pltpu.make_async_remote_copy(
                    src_ref=x_refs[a].at[_dev_index(*peer)], dst_ref=out_refs[a].at[me],
                    send_sem=send_sems.at[7 * a + k], recv_sem=recv_sems.at[7 * a + k],
                    device_id=peer, device_id_type=pl.DeviceIdType.MESH))
                if not with_arrivals:
                    continue
                arrivals.append(pltpu.make_async_remote_copy(
                    src_ref=x_refs[a].at[me], dst_ref=out_refs[a].at[_dev_index(*peer)],
                    send_sem=send_sems.at[7 * a + k], recv_sem=recv_sems.at[7 * a + k],
                    device_id=peer, device_id_type=pl.DeviceIdType.MESH))
        return mine, sends, arrivals

    def start(self, x_refs, out_refs, sems):
        mine, sends, _ = self._copies(x_refs, out_refs, sems, False)
        for cp in mine + sends:
            cp.start()

    def wait(self, x_refs, out_refs, sems):
        mine, sends, arrivals = self._copies(x_refs, out_refs, sems, True)
        for cp in arrivals:
            cp.wait_recv()
        for cp in sends:
            cp.wait_send()
        for cp in mine:
            cp.wait()


def _sum_blocks(parts, name):
    n, r, c = parts.shape
    row_tiles = [t for t in range(16, min(r, 2048) + 1, 16) if r % t == 0]
    if row_tiles:
        tr, tc = max(row_tiles), c
    else:
        tr, tc = r, 2 * LANES
    assert c % tc == 0

    def body(p_ref, o_ref):
        acc = p_ref[0].astype(F32)
        for s in range(1, n):
            acc = acc + p_ref[s].astype(F32)
        o_ref[...] = acc

    return pl.pallas_call(
        body, name=name, grid=(r // tr, c // tc),
        in_specs=[pl.BlockSpec((n, tr, tc), lambda i, j: (0, i, j))],
        out_specs=pl.BlockSpec((tr, tc), lambda i, j: (i, j)),
        out_shape=jax.ShapeDtypeStruct((r, c), F32),
        compiler_params=_cparams("parallel", "parallel"),
    )(parts)


def _sigmoid(x):
    return 1.0 / (1.0 + jnp.exp(-x))


def _silu(x):
    return x * _sigmoid(x)


def _silu_grad(x):
    s = _sigmoid(x)
    return s * (1.0 + x * (1.0 - s))


def _colsum(x):
    return jnp.sum(x, axis=0, keepdims=True)


def _rms(x):
    return lax.rsqrt(jnp.mean(x * x, axis=-1, keepdims=True) + EPS)


def _rms_bwd(xn, r, dxn):
    return r * (dxn - xn * jnp.mean(dxn * xn, axis=-1, keepdims=True))


def _adamw(w, g, m, v):
    m = ADAM_B1 * m + (1.0 - ADAM_B1) * g
    v = ADAM_B2 * v + (1.0 - ADAM_B2) * jnp.square(g)
    m_hat = m / (1.0 - ADAM_B1 ** ADAM_STEP)
    v_hat = v / (1.0 - ADAM_B2 ** ADAM_STEP)
    delta = -ADAM_LR * (m_hat / (jnp.sqrt(v_hat) + ADAM_EPS) + ADAM_WD * w)
    return delta, m, v


def _adamw_call(w, g, m, v, name):
    r, c = w.shape
    if r % 256 == 0:
        tr, tc = 256, c
    elif r * c <= 256 * 1024 or c % (2 * LANES):
        tr, tc = r, c
    else:
        tr, tc = r, 2 * LANES

    def body(w_ref, g_ref, m_ref, v_ref, d_out, m_out, v_out):
        d_out[...], m_out[...], v_out[...] = _adamw(w_ref[...], g_ref[...], m_ref[...], v_ref[...])

    spec = pl.BlockSpec((tr, tc), lambda i, j: (i, j))
    return pl.pallas_call(
        body, name=name, grid=(r // tr, c // tc), in_specs=[spec] * 4, out_specs=[spec] * 3,
        out_shape=[jax.ShapeDtypeStruct((r, c), F32)] * 3, compiler_params=_cparams("parallel", "parallel"),
    )(w, g, m, v)


def _uq_to_kernel_layout(w):
    lead = w.shape[:-1]
    t = w.reshape(lead + (HEADS, QK_DIM))
    return jnp.concatenate([t[..., :NOPE].reshape(lead + (HEADS * NOPE,)),
                            t[..., NOPE:NOPE + ROPE // 2].reshape(lead + (LANES,)),
                            t[..., NOPE + ROPE // 2:].reshape(lead + (LANES,))], axis=-1)


def _uq_from_kernel_layout(w):
    lead = w.shape[:-1]
    nope = w[..., :HEADS * NOPE].reshape(lead + (HEADS, NOPE))
    r1 = w[..., HEADS * NOPE:HEADS * NOPE + LANES].reshape(lead + (HEADS, ROPE // 2))
    r2 = w[..., HEADS * NOPE + LANES:].reshape(lead + (HEADS, ROPE // 2))
    return jnp.concatenate([nope, r1, r2], axis=-1).reshape(lead + (HEADS * QK_DIM,))


def _ukv_to_kernel_layout(w):
    lead = w.shape[:-1]
    t = w.reshape(lead + (HEADS, NOPE + HD))
    return jnp.concatenate([t[..., :NOPE].reshape(lead + (HEADS * NOPE,)),
                            t[..., NOPE:].reshape(lead + (HEADS * HD,))], axis=-1)


def _ukv_from_kernel_layout(w):
    lead = w.shape[:-1]
    kn = w[..., :HEADS * NOPE].reshape(lead + (HEADS, NOPE))
    vv = w[..., HEADS * NOPE:].reshape(lead + (HEADS, HD))
    return jnp.concatenate([kn, vv], axis=-1).reshape(lead + (HEADS * (NOPE + HD),))


def _w_in_t_to_kernel_layout(wt):
    sb = wt[0:2048]
    c_q = wt[2048:2432]
    c_kv = wt[2432:2688]
    k_rot = wt[2688:2720]
    mla_z = wt[2720:3232]
    gates = wt[3232:5280]
    zeros = jnp.zeros((LANES, wt.shape[1]), wt.dtype)
    k1 = jnp.tile(k_rot[:ROPE // 2], (HEADS, 1))
    k2 = jnp.tile(k_rot[ROPE // 2:], (HEADS, 1))
    return jnp.concatenate([gates, sb, mla_z, c_q, zeros, c_kv, k1, k2], axis=0)


def _w_in_t_from_kernel_layout(gt, g_rot):
    return jnp.concatenate([gt[O_SBQ:O_SBQ + 2048], gt[O_CQ:O_CQ + Q_RANK], gt[O_CKV:O_CKV + KV_RANK],
                            g_rot.astype(gt.dtype), gt[O_MLAZ:O_MLAZ + MLA_W], gt[O_GA:O_GA + 2 * D]], axis=0)


def kernel(x, c, positions, w_ada, b_ada, norm_gain, w_in, q_norm_gain, w_uq, kv_norm_gain, w_ukv, w_branch_a, w_branch_b, w_out, final_norm_gain, loss_target, m_w_ada, m_b_ada, m_norm_gain, m_w_in, m_q_norm_gain, m_w_uq, m_kv_norm_gain, m_w_ukv, m_w_branch_a, m_w_branch_b, m_w_out, m_final_norm_gain, v_w_ada, v_b_ada, v_norm_gain, v_w_in, v_q_norm_gain, v_w_uq, v_kv_norm_gain, v_w_ukv, v_w_branch_a, v_w_branch_b, v_w_out, v_final_norm_gain):
    s_len = x.shape[1]
    me = _dev_index(*_mesh_pos())
    x2d = x[0]
    tgt = loss_target[0]

    w_in_t = w_in[0].T.astype(BF16)
    big = [w_uq[0], w_ukv[0], w_branch_a[0], w_branch_b[0], w_out[0]]
    big_sizes = [int(w.size) for w in big]
    packed = jnp.concatenate([w.astype(BF16).reshape(-1, LANES) for w in big], axis=0)
    g_in_t, c_all = _all_gather_parts([w_in_t, c.reshape(8, LANES)], "gather_w_in")
    c_all = c_all.reshape(N_DEV, D)
    w_in_kt = _w_in_t_to_kernel_layout(g_in_t.reshape(N_DEV * w_in_t.shape[0], D))

    mod_cols = _mm(c_all, w_ada[0], name="ada_mod")
    mod_all = _all_gather(mod_cols, "gather_mod")
    mod = lax.dynamic_index_in_dim(mod_all, me, axis=1, keepdims=False).reshape(1, 3 * D)
    mod_shift, mod_scale, mod_gate = mod[:, :D], mod[:, D:2 * D], mod[:, 2 * D:]
    b_shift, b_scale, b_gate = b_ada[:, :D], b_ada[:, D:2 * D], b_ada[:, 2 * D:]
    g1 = norm_gain
    gq, gkv = q_norm_gain, kv_norm_gain
    gf = final_norm_gain.reshape(1, D)

    def f_h(x_, g1_, ms, bs, msc, bsc):
        xn = x_ * _rms(x_)
        return (xn * g1_ * (1.0 + (msc + bsc)) + (ms + bs),), ()

    (h,) = _rowwise(f_h, [x2d], [g1, mod_shift, b_shift, mod_scale, b_scale], [(D, BF16)], name="ada_norm")
    proj = _mm(h, w_in_kt, tb=True, name="proj_in", tiles=(min(s_len, 1024), IN_PAD // 2, D))

    (o_a, sb_tails), (gathered,) = _sb_forward(
        proj, s_len, _Exchange([jnp.broadcast_to(packed[None], (N_DEV,) + packed.shape)]))
    offs = [0]
    for n in big_sizes:
        offs.append(offs[-1] + n // LANES)

    def unpack(t, shape):
        return gathered[:, offs[t]:offs[t + 1], :].reshape((N_DEV,) + shape)

    def cols(t, shape):
        return unpack(t, shape).transpose(1, 0, 2).reshape(shape[0], N_DEV * shape[1])

    w_uq_k = _uq_to_kernel_layout(cols(0, big[0].shape))
    w_ukv_k = _ukv_to_kernel_layout(cols(1, big[1].shape))
    w_a_f = cols(2, big[2].shape)
    w_b_f = cols(3, big[3].shape)
    w_out_f = unpack(4, big[4].shape).reshape(D, D)

    def f_lat(cq, ckv, gq_, gkv_):
        return (cq * _rms(cq) * gq_, ckv * _rms(ckv) * gkv_), ()

    cq_n, ckv_n = _rowwise(f_lat, [(proj, O_CQ // Q_RANK, Q_RANK), (proj, O_CKV // KV_RANK, KV_RANK)], [gq, gkv],
                           [(Q_RANK, BF16), (KV_RANK, BF16)], name="latent_norm")
    q_mla = _mm(cq_n, w_uq_k, name="q_up")
    kv = _mm(ckv_n, w_ukv_k, out_dtype=BF16, name="kv_up")

    inv_freq = ROPE_BASE ** (-jnp.arange(0, ROPE, 2, dtype=F32) / ROPE)
    inv_freq_t = jnp.tile(inv_freq, HEADS).reshape(1, LANES)
    pos_col = positions.reshape(s_len, 1).astype(F32)

    pairs = HEADS // 2

    def f_rope(pos, qn, q1, q2, kn, k1, k2, freq):
        ang = pos * freq
        cs, sn = jnp.cos(ang), jnp.sin(ang)
        q1r, q2r = q1 * cs - q2 * sn, q1 * sn + q2 * cs
        k1r, k2r = k1 * cs - k2 * sn, k1 * sn + k2 * cs
        lane = lax.broadcasted_iota(jnp.int32, q1.shape, 1)
        first, second = lane < ROPE, (lane >= ROPE) & (lane < 2 * ROPE)
        k_rot = jnp.where(first, k1r, jnp.where(second, k2r, 0.0))
        q_parts, k_parts = [], []
        for p in range(pairs):
            q_rot = jnp.where(first, pltpu.roll(q1r, (LANES - ROPE * p) % LANES, 1),
                              jnp.where(second, pltpu.roll(q2r, (LANES + ROPE - ROPE * p) % LANES, 1), 0.0))
            q_parts += [qn[:, LANES * p:LANES * (p + 1)], q_rot]
            k_parts += [kn[:, LANES * p:LANES * (p + 1)], k_rot]
        return (jnp.concatenate(q_parts, axis=1), jnp.concatenate(k_parts, axis=1), cs, sn), ()

    q_cat, k_cat, cos_t, sin_t = _rowwise(
        f_rope, [pos_col, (q_mla, 0, MLA_W), (q_mla, 4, LANES), (q_mla, 5, LANES), (kv, 0, MLA_W),
                 (proj, O_KROT // LANES, LANES), (proj, O_KROT // LANES + 1, LANES)], [inv_freq_t],
        [(2 * MLA_W, BF16), (2 * MLA_W, BF16), (LANES, F32), (LANES, F32)], name="rope")

    o_b, lse = _mla_forward(q_cat, k_cat, kv, s_len)

    def f_gate(oa, za, ob, zb):
        return (oa * _silu(za), ob * _silu(zb)), ()

    ya_in, yb_in = _rowwise(f_gate, [o_a, (proj, O_SBZ // SB_W, SB_W), o_b, (proj, O_MLAZ // MLA_W, MLA_W)], [],
                            [(SB_W, BF16), (MLA_W, BF16)], name="branch_gate")
    y_a = _mm(ya_in, w_a_f, out_dtype=BF16, name="branch_a")

    def f_merge(yb, ga, gb, ya):
        return (yb, _sigmoid(ga) * ya + _sigmoid(gb) * yb), ()

    y_b, merged = _mm(yb_in, w_b_f, name="branch_b_merge", tiles=(min(s_len, 512), D, MLA_W),
                      epilogue=(f_merge, [(proj, O_GA // D, D), (proj, O_GB // D, D), y_a], [], [(D, BF16), (D, BF16)], []))

    def f_loss(out_, x_, t_, mg, bg, gf_):
        gate = mg + bg
        x2 = x_ + gate * out_
        r2 = _rms(x2)
        xn2 = x2 * r2
        err = xn2 * gf_ - t_
        loss = jnp.full((1, LANES), 0.5 / D, F32) * jnp.sum(err * err)
        dy = err * (1.0 / D)
        dx2 = _rms_bwd(xn2, r2, dy * gf_)
        return (dx2, dx2 * gate), (loss, _colsum(dy * xn2), _colsum(dx2 * out_))

    dx2, d_out, loss_part, d_gf, d_gate = _mm(
        merged, w_out_f, name="out_proj_loss", tiles=(min(s_len, 512), D, D),
        epilogue=(f_loss, [x2d, tgt], [mod_gate, b_gate, gf], [(D, F32), (D, BF16)], [LANES, D, D]))

    dw_out = _mm(merged, d_out, ta=True, name="dw_out")

    def f_dmerge(dm, ga, gb, ya, yb):
        sa, sb = _sigmoid(ga), _sigmoid(gb)
        return (dm * sa, dm * sb, dm * ya * sa * (1.0 - sa), dm * yb * sb * (1.0 - sb)), ()

    d_ya, d_yb, d_ga, d_gb = _mm(
        d_out, w_out_f, tb=True, name="d_merge", tiles=(min(s_len, 256), D, D),
        epilogue=(f_dmerge, [(proj, O_GA // D, D), (proj, O_GB // D, D), y_a, y_b], [], [(D, BF16)] * 4, []))
    dw_a = _mm(ya_in, d_ya, ta=True, name="dw_branch_a")
    dw_b = _mm(yb_in, d_yb, ta=True, name="dw_branch_b")

    def f_dgate(d_in, o_, z_):
        return (d_in * _silu(z_), d_in * o_ * _silu_grad(z_)), ()

    d_oa, d_sbz = _mm(d_ya, w_a_f, tb=True, name="d_branch_a",
                      epilogue=(f_dgate, [o_a, (proj, O_SBZ // SB_W, SB_W)], [], [(SB_W, F32), (SB_W, BF16)], []))
    d_ob, d_mlaz = _mm(d_yb, w_b_f, tb=True, name="d_branch_b",
                       epilogue=(f_dgate, [o_b, (proj, O_MLAZ // MLA_W, MLA_W)], [], [(MLA_W, F32), (MLA_W, BF16)], []))

    dq_cat, dk_cat, dv_b = _mla_backward(q_cat, k_cat, kv, o_b, lse, d_ob, s_len)

    def f_drope(dq, dk, dv_, cs, sn):
        lane = lax.broadcasted_iota(jnp.int32, cs.shape, 1)
        first, second = lane < ROPE, (lane >= ROPE) & (lane < 2 * ROPE)
        dq1 = dq2 = dk1 = dk2 = None
        for p in range(pairs):
            q_rot = dq[:, LANES * (2 * p + 1):LANES * (2 * p + 2)]
            k_rot = dk[:, LANES * (2 * p + 1):LANES * (2 * p + 2)]
            parts = (pltpu.roll(jnp.where(first, q_rot, 0.0), (ROPE * p) % LANES, 1),
                     pltpu.roll(jnp.where(second, q_rot, 0.0), (LANES - ROPE + ROPE * p) % LANES, 1),
                     jnp.where(first, k_rot, 0.0), jnp.where(second, k_rot, 0.0))
            if p == 0:
                dq1, dq2, dk1, dk2 = parts
            else:
                dq1, dq2, dk1, dk2 = dq1 + parts[0], dq2 + parts[1], dk1 + parts[2], dk2 + parts[3]
        dqn_ = [dq[:, 2 * LANES * p:2 * LANES * p + LANES] for p in range(pairs)]
        dkn_ = [dk[:, 2 * LANES * p:2 * LANES * p + LANES] for p in range(pairs)]
        return (jnp.concatenate(dqn_ + [dq1 * cs + dq2 * sn, dq2 * cs - dq1 * sn], axis=1),
                jnp.concatenate(dkn_ + [dv_], axis=1),
                jnp.concatenate([dk1 * cs + dk2 * sn, dk2 * cs - dk1 * sn], axis=1)), ()

    dq_k, dkv_k, d_krot = _rowwise(f_drope, [dq_cat, dk_cat, dv_b, cos_t, sin_t], [],
                                   [(HEADS * QK_DIM, BF16), (2 * MLA_W, BF16), (2 * LANES, BF16)], name="d_rope")
    dw_uq_k = _mm(cq_n, dq_k, ta=True, name="dw_uq")
    dw_ukv_k = _mm(ckv_n, dkv_k, ta=True, name="dw_ukv")

    def f_dlat(d_normed, latent, gain):
        r = _rms(latent)
        normed = latent * r
        return (_rms_bwd(normed, r, d_normed * gain),), (_colsum(d_normed * normed),)

    d_cq, d_gq = _mm(dq_k, w_uq_k, tb=True, name="d_cq_norm",
                     epilogue=(f_dlat, [(proj, O_CQ // Q_RANK, Q_RANK)], [gq], [(Q_RANK, BF16)], [Q_RANK]))
    d_ckv, d_gkv = _mm(dkv_k, w_ukv_k, tb=True, name="d_ckv_norm",
                       epilogue=(f_dlat, [(proj, O_CKV // KV_RANK, KV_RANK)], [gkv], [(KV_RANK, BF16)], [KV_RANK]))

    def col_blocks(g):
        kdim, n8 = g.shape
        return g.astype(BF16).reshape(kdim, N_DEV, n8 // N_DEV).transpose(1, 0, 2).reshape(N_DEV, -1, LANES)

    g_blocks = jnp.concatenate([col_blocks(_uq_from_kernel_layout(dw_uq_k)), col_blocks(_ukv_from_kernel_layout(dw_ukv_k)),
                                col_blocks(dw_a), col_blocks(dw_b), dw_out.astype(BF16).reshape(N_DEV, -1, LANES)], axis=1)
    (d_sbq, d_sbk, d_sbv), (g_recv,) = _sb_backward(proj, sb_tails, d_oa, s_len, _Exchange([g_blocks]))

    d_proj = jnp.concatenate([d_ga, d_gb, d_sbq.astype(BF16), d_sbk.astype(BF16), d_sbv.astype(BF16), d_sbz, d_mlaz,
                              d_cq, jnp.zeros((s_len, LANES), BF16), d_ckv, d_krot], axis=1)
    dw_in_kt = _mm(d_proj, h, ta=True, out_dtype=BF16, name="dw_in", tiles=(512, D, s_len))

    def krot_body(t_ref, o_ref):
        half = ROPE // 2
        for part in range(2):
            acc = t_ref[part * LANES:part * LANES + half, :].astype(F32)
            for hh in range(1, HEADS):
                acc = acc + t_ref[part * LANES + hh * half:part * LANES + (hh + 1) * half, :].astype(F32)
            o_ref[part * half:(part + 1) * half, :] = acc

    dw_krot = pl.pallas_call(krot_body, name="dw_krot_sum", out_shape=jax.ShapeDtypeStruct((ROPE, D), F32))(
        dw_in_kt[O_KROT:O_KROT + 2 * LANES])

    g_in_blocks = _w_in_t_from_kernel_layout(dw_in_kt, dw_krot).reshape(N_DEV, -1, D)
    def f_dx(dh_, x_, dx2_, g1_, msc, bsc):
        r = _rms(x_)
        xn = x_ * r
        dn1 = dh_ * (1.0 + (msc + bsc))
        return ((dx2_ + _rms_bwd(xn, r, dn1 * g1_),),
                (_colsum(dh_), _colsum(dh_ * (xn * g1_)), _colsum(dn1 * xn)))

    (grad_x2d, d_shift, d_scale, d_g1), (g_in_recv,) = _mm(
        d_proj, w_in_kt, name="d_h", tiles=(min(s_len, 512), D, 512), exchange=_Exchange([g_in_blocks]),
        epilogue=(f_dx, [x2d, dx2], [g1, mod_scale, b_scale], [(D, F32)], [D, D, D]))

    g_in_sum_t = _sum_blocks(g_in_recv, "sum_grads_w_in")
    g_sum = _sum_blocks(g_recv, "sum_grads")
    g_big = [g_sum[offs[t]:offs[t + 1]].reshape(big[t].shape) for t in range(5)]

    small = jnp.concatenate([d_shift, d_scale, d_gate, d_g1, d_gq, d_gkv, d_gf, loss_part], axis=1)
    n_small = small.shape[1]
    pad = (-n_small) % (8 * LANES)
    small = jnp.pad(small, ((0, 0), (0, pad))).reshape(-1, LANES)
    small_all = _all_gather(small, "gather_small")
    small_sum = _sum_blocks(small_all, "sum_small").reshape(1, -1)
    g_b_ada = small_sum[:, :3 * D]
    g_g1 = small_sum[:, 3 * D:4 * D]
    g_gq = small_sum[:, 4 * D:4 * D + Q_RANK]
    g_gkv = small_sum[:, 4 * D + Q_RANK:4 * D + Q_RANK + KV_RANK]
    g_gf = small_sum[:, 4 * D + Q_RANK + KV_RANK:4 * D + Q_RANK + KV_RANK + D]

    dmod_all = small_all.reshape(N_DEV, -1)[:, :3 * D]
    dmod_cols = lax.dynamic_slice_in_dim(dmod_all, me * (3 * D // N_DEV), 3 * D // N_DEV, axis=1)
    g_w_ada = _mm(c_all, dmod_cols, ta=True, name="dw_ada")

    loss = small_sum[0, n_small - LANES]

    names = ["w_ada", "b_ada", "norm_gain", "w_in", "q_norm_gain", "w_uq", "kv_norm_gain", "w_ukv",
             "w_branch_a", "w_branch_b", "w_out", "final_norm_gain"]
    weights = dict(w_ada=w_ada, b_ada=b_ada, norm_gain=norm_gain, w_in=w_in, q_norm_gain=q_norm_gain, w_uq=w_uq,
                   kv_norm_gain=kv_norm_gain, w_ukv=w_ukv, w_branch_a=w_branch_a, w_branch_b=w_branch_b, w_out=w_out,
                   final_norm_gain=final_norm_gain)
    moms = dict(w_ada=m_w_ada, b_ada=m_b_ada, norm_gain=m_norm_gain, w_in=m_w_in, q_norm_gain=m_q_norm_gain,
                w_uq=m_w_uq, kv_norm_gain=m_kv_norm_gain, w_ukv=m_w_ukv, w_branch_a=m_w_branch_a,
                w_branch_b=m_w_branch_b, w_out=m_w_out, final_norm_gain=m_final_norm_gain)
    vels = dict(w_ada=v_w_ada, b_ada=v_b_ada, norm_gain=v_norm_gain, w_in=v_w_in, q_norm_gain=v_q_norm_gain,
                w_uq=v_w_uq, kv_norm_gain=v_kv_norm_gain, w_ukv=v_w_ukv, w_branch_a=v_w_branch_a,
                w_branch_b=v_w_branch_b, w_out=v_w_out, final_norm_gain=v_final_norm_gain)
    grads2d = dict(w_ada=g_w_ada, b_ada=g_b_ada, norm_gain=g_g1, w_in=g_in_sum_t, q_norm_gain=g_gq, w_uq=g_big[0],
                   kv_norm_gain=g_gkv, w_ukv=g_big[1], w_branch_a=g_big[2], w_branch_b=g_big[3], w_out=g_big[4],
                   final_norm_gain=g_gf)

    grads, deltas, new_m, new_v = [], [], [], []
    for n in names:
        w = weights[n]
        if n == "w_in":
            to2d = lambda t: t[0].T if t.ndim == 3 else t
            back = lambda t: t.T[None]
        else:
            shape2d = grads2d[n].shape
            to2d = lambda t, s=shape2d: t.reshape(s)
            back = lambda t, s=w.shape: t.reshape(s)
        g2d = to2d(grads2d[n])
        d_, m_, v_ = _adamw_call(to2d(w), g2d, to2d(moms[n]), to2d(vels[n]), "adamw_" + n)
        grads.append(back(g2d))
        deltas.append(back(d_))
        new_m.append(back(m_))
        new_v.append(back(v_))

    return (loss, grad_x2d.reshape(x.shape), *grads, *deltas, *new_m, *new_v)
```

```python
import functools
import math

import jax
import jax.numpy as jnp
from jax import lax
from jax.experimental import pallas as pl
from jax.experimental.pallas import tpu as pltpu

F32 = jnp.float32
BF16 = jnp.bfloat16
MXU_DTYPE = jnp.bfloat16

N_DEV = 8
D = 1024
HEADS = 8
HD = 64
SB_W = 512
MLA_W = 512
Q_RANK = 384
KV_RANK = 256
ROPE = 32
NOPE = 64
QK_DIM = NOPE + ROPE
EPS = 1e-6
ROPE_BASE = 10000.0

ADAM_LR = 0.001
ADAM_B1 = 0.9
ADAM_B2 = 0.999
ADAM_EPS = 1e-08
ADAM_WD = 0.01
ADAM_STEP = 10

LANES = 128
VMEM_LIMIT = 48 * 1024 * 1024

O_GA, O_GB = 0, 1024
O_SBQ, O_SBK, O_SBV, O_SBZ = 2048, 2560, 3072, 3584
O_MLAZ = 4096
O_CQ = 4608
O_CKV = 5120
O_KROT = 5376
IN_PAD = 5632

BQ = 256
BK = 256


def _cparams(*sem):
    return pltpu.CompilerParams(dimension_semantics=sem, vmem_limit_bytes=VMEM_LIMIT)


def _tile_of(n, cap=512):
    if n <= cap:
        return n
    for t in (1024, 768, 512, 384, 256, 128):
        if t <= cap and n % t == 0:
            return t
    raise ValueError(n)


def _rowwise(fn, rows, vecs, outs, reds=(), *, name, tile=512):
    norm = []
    for r in rows:
        if isinstance(r, tuple):
            arr, cb, w = r[:3]
            ro = r[3] if len(r) > 3 else 0
        else:
            arr, cb, w, ro = r, 0, r.shape[1], 0
        norm.append((arr, cb, w, ro))
    s_len = norm[0][0].shape[0]
    tile = min(tile, s_len)
    assert s_len % tile == 0
    n_row, n_vec, n_out, n_red = len(norm), len(vecs), len(outs), len(reds)

    def body(*refs):
        step = pl.program_id(0)
        row_refs = refs[:n_row]
        vec_refs = refs[n_row:n_row + n_vec]
        out_refs = refs[n_row + n_vec:n_row + n_vec + n_out]
        red_refs = refs[n_row + n_vec + n_out:]
        row_res, red_res = fn(*[r[...] for r in row_refs], *[v[...] for v in vec_refs])
        for o, val in zip(out_refs, row_res):
            o[...] = val.astype(o.dtype)
        if n_red:
            @pl.when(step == 0)
            def _():
                for r in red_refs:
                    r[...] = jnp.zeros(r.shape, r.dtype)
            for r, val in zip(red_refs, red_res):
                r[...] += val

    in_specs = []
    for arr, cb, w, ro in norm:
        in_specs.append(pl.BlockSpec((tile, w), functools.partial(lambda i, cb, rb: (i + rb, cb), cb=cb, rb=ro // tile)))
        assert ro % tile == 0
    for v in vecs:
        in_specs.append(pl.BlockSpec(v.shape, lambda i: (0, 0)))
    out_shape = [jax.ShapeDtypeStruct((s_len, w), dt) for w, dt in outs]
    out_specs = [pl.BlockSpec((tile, w), lambda i: (i, 0)) for w, _ in outs]
    out_shape += [jax.ShapeDtypeStruct((1, w), F32) for w in reds]
    out_specs += [pl.BlockSpec((1, w), lambda i: (0, 0)) for w in reds]
    res = pl.pallas_call(
        body, name=name, grid=(s_len // tile,), in_specs=in_specs, out_specs=out_specs, out_shape=out_shape,
        compiler_params=_cparams("arbitrary" if n_red else "parallel"),
    )(*[a for a, _, _, _ in norm], *vecs)
    return res


def _mm(a, b, *, ta=False, tb=False, out_dtype=F32, name, exchange=None, tiles=None, epilogue=None):
    m, k = (a.shape[1], a.shape[0]) if ta else a.shape
    n = b.shape[0] if tb else b.shape[1]
    assert (b.shape[1] if tb else b.shape[0]) == k
    tm, tn, tk = tiles or (_tile_of(m, 1024), _tile_of(n, 1024 if n <= 1024 else 512), _tile_of(k, 1024))
    assert m % tm == 0 and n % tn == 0 and k % tk == 0
    ni, nj, nk = m // tm, n // tn, k // tk
    dims = (((0 if ta else 1,), (1 if tb else 0,)), ((), ()))
    n_ex = len(exchange.arrays) if exchange else 0
    fn, rows, vecs, outs, reds = epilogue or (None, (), (), (), ())
    rows = [r if isinstance(r, tuple) else (r, 0, r.shape[1]) for r in rows]
    assert not epilogue or tn == n
    n_res = len(outs) + len(reds) if epilogue else 1

    def body(*refs):
        a_ref, b_ref = refs[:2]
        row_refs, refs = refs[2:2 + len(rows)], refs[2 + len(rows):]
        vec_refs, refs = refs[:len(vecs)], refs[len(vecs):]
        x_refs, refs = refs[:n_ex], refs[n_ex:]
        res_refs, refs = refs[:n_res], refs[n_res:]
        out_refs, acc_ref, sems = refs[:n_ex], refs[n_ex], refs[n_ex + 1:]
        i, j, kk = pl.program_id(0), pl.program_id(1), pl.program_id(2)
        first = (i == 0) & (j == 0) & (kk == 0)

        if exchange:
            @pl.when(first)
            def _():
                exchange.start(x_refs, out_refs, sems)

        if reds:
            @pl.when(first)
            def _():
                for r in res_refs[len(outs):]:
                    r[...] = jnp.zeros(r.shape, r.dtype)

        @pl.when(kk == 0)
        def _():
            acc_ref[...] = jnp.zeros(acc_ref.shape, F32)

        acc_ref[...] += lax.dot_general(a_ref[...].astype(MXU_DTYPE), b_ref[...].astype(MXU_DTYPE), dims,
                                        preferred_element_type=F32)

        @pl.when(kk == nk - 1)
        def _():
            if not epilogue:
                res_refs[0][...] = acc_ref[...].astype(res_refs[0].dtype)
                return
            row_res, red_res = fn(acc_ref[...], *[r[...] for r in row_refs], *[v[...] for v in vec_refs])
            for o, val in zip(res_refs, row_res):
                o[...] = val.astype(o.dtype)
            for r, val in zip(res_refs[len(outs):], red_res):
                r[...] += val

        if exchange:
            @pl.when((i == ni - 1) & (j == nj - 1) & (kk == nk - 1))
            def _():
                exchange.wait(x_refs, out_refs, sems)

    a_spec = pl.BlockSpec((tk, tm), lambda i, j, kk: (kk, i)) if ta else pl.BlockSpec((tm, tk), lambda i, j, kk: (i, kk))
    b_spec = pl.BlockSpec((tn, tk), lambda i, j, kk: (j, kk)) if tb else pl.BlockSpec((tk, tn), lambda i, j, kk: (kk, j))
    in_specs = [a_spec, b_spec]
    in_specs += [pl.BlockSpec((tm, w), functools.partial(lambda i, j, kk, cb: (i, cb), cb=cb)) for _, cb, w in rows]
    in_specs += [pl.BlockSpec(v.shape, lambda i, j, kk: (0, 0)) for v in vecs]
    if epilogue:
        res_specs = [pl.BlockSpec((tm, w), lambda i, j, kk: (i, 0)) for w, _ in outs]
        res_specs += [pl.BlockSpec((1, w), lambda i, j, kk: (0, 0)) for w in reds]
        res_shape = [jax.ShapeDtypeStruct((m, w), dt) for w, dt in outs] + [jax.ShapeDtypeStruct((1, w), F32) for w in reds]
    else:
        res_specs = [pl.BlockSpec((tm, tn), lambda i, j, kk: (i, j))]
        res_shape = [jax.ShapeDtypeStruct((m, n), out_dtype)]
    ordered = bool(exchange or reds)
    res = pl.pallas_call(
        body, name=name, grid=(ni, nj, nk),
        in_specs=in_specs + (exchange.in_specs if exchange else []),
        out_specs=res_specs + (exchange.out_specs if exchange else []),
        out_shape=res_shape + (exchange.out_shape if exchange else []),
        scratch_shapes=[pltpu.VMEM((tm, tn), F32)] + (exchange.scratch if exchange else []),
        compiler_params=_cparams(*(("arbitrary",) * 3 if ordered else ("parallel", "parallel", "arbitrary"))),
    )(a, b, *[r[0] for r in rows], *vecs, *(exchange.arrays if exchange else []))
    main = res[:n_res] if epilogue else res[0]
    return (main, res[n_res:]) if exchange else main


_NT = (((1,), (1,)), ((), ()))
_TN = (((0,), (0,)), ((), ()))


def _dot(a, b):
    return jnp.dot(a, b, preferred_element_type=F32)


def _dot_nt(a, b):
    return lax.dot_general(a, b, _NT, preferred_element_type=F32)


def _dot_tn(a, b):
    return lax.dot_general(a, b, _TN, preferred_element_type=F32)


def _running_sum(x, tri):
    return _dot(x.astype(MXU_DTYPE), tri)


def _neg_softplus(z):
    u = jnp.exp2(jnp.abs(z) * (-1.0 / math.log(2.0)))
    return -jnp.maximum(z, 0.0) - jnp.log(1.0 + u)


def _walk_blocks(step, st, n, descending, group=2):
    done = 0
    size = group
    while size >= 1:
        def trip(t, s, size=size, done=done):
            js = [done + size * t + g for g in range(size)]
            return step([n - 1 - j for j in js] if descending else js, s)

        trips = (n - done) // size
        st = lax.fori_loop(0, trips, trip, st)
        done = done + size * trips
        size //= 2
    return st


def _chains(js):
    return [(h, t) for t in range(len(js)) for h in range(2)]


def _rowsum(x):
    return jnp.sum(x, axis=1, keepdims=True)


def _attn_consts():
    row = lax.broadcasted_iota(jnp.int32, (BQ, BK), 0)
    col = lax.broadcasted_iota(jnp.int32, (BQ, BK), 1)
    trow = lax.broadcasted_iota(jnp.int32, (BK, BK), 0)
    tcol = lax.broadcasted_iota(jnp.int32, (BK, BK), 1)
    lane = lax.broadcasted_iota(jnp.int32, (BQ, LANES), 1)
    klane = lax.broadcasted_iota(jnp.int32, (BK, LANES), 1)
    return row, col, trow, tcol, lane, klane


assert BQ == BK


def _diag_step(jd, descending, diag_mask, group=4):
    below = list(range(group)) if descending else list(reversed(range(group)))
    js = [jnp.maximum(jd - o, 0) for o in below]
    masks = [diag_mask if o == 0 else jd - o >= 0 for o in below]
    return js, masks, jnp.maximum(jd - (group - 1), 0)


def _key_slice(j):
    return pl.ds(pl.multiple_of(j * BK, BK), BK)


def _sb_forward(proj, s_len, exchange):
    nq = s_len // BQ
    assert s_len // BK <= HD
    qc, kc, vc = O_SBQ // LANES, O_SBK // LANES, O_SBV // LANES
    n_ex = len(exchange.arrays)

    def body(q_ref, k_ref, v_ref, *rest):
        x_refs, (o_ref, tails_ref) = rest[:n_ex], rest[n_ex:n_ex + 2]
        out_refs, sems = rest[n_ex + 2:2 * n_ex + 2], rest[2 * n_ex + 2:]
        p = pl.program_id(0)
        i = pl.program_id(1)
        jd = i

        @pl.when((p == 0) & (i == 0))
        def _():
            exchange.start(x_refs, out_refs, sems)

        row, col, trow, tcol, lane, klane = _attn_consts()
        tri = (trow >= tcol).astype(MXU_DTYPE)
        q = q_ref[...] * 0.125
        qh = [jnp.where(lane < HD, q, 0.0).astype(MXU_DTYPE), jnp.where(lane >= HD, q, 0.0).astype(MXU_DTYPE)]

        km = [klane < HD, klane >= HD]

        def step(js, st, masks):
            carry, acc, tail = st
            chains = _chains(js)
            kj = [k_ref[_key_slice(j), :].astype(MXU_DTYPE) for j in js]
            vj = [v_ref[_key_slice(j), :].astype(MXU_DTYPE) for j in js]
            z = {(h, t): _dot_nt(qh[h], kj[t]) for h, t in chains}
            run = list(carry)
            suf, carry_in = {}, {}
            for h, t in chains:
                lom = _neg_softplus(z[h, t])
                if masks:
                    lom = jnp.where(masks[t], lom, 0.0)
                suf[h, t] = _running_sum(lom, tri)
                carry_in[h, t] = run[h]
                run[h] = run[h] + _rowsum(lom)
            for h, t in chains:
                a = jnp.exp(z[h, t] + suf[h, t] + carry_in[h, t])
                if masks:
                    a = jnp.where(masks[t], a, 0.0)
                acc = acc + _dot(a.astype(MXU_DTYPE), jnp.where(km[h], vj[t], 0))
                tail_lane = js[t] if not masks or masks[t].ndim else jnp.where(masks[t], js[t], -LANES)
                tail = jnp.where(lane == h * HD + tail_lane, carry_in[h, t], tail)
            return tuple(run), acc, tail

        zero = jnp.zeros((BQ, LANES), F32)
        diag_js, masks, left = _diag_step(jd, True, col < row, group=2)
        st = step(diag_js, ((jnp.zeros((BQ, 1), F32),) * 2, zero, zero), masks)
        st = _walk_blocks(lambda js, s: step(js, s, None), st, left, True, group=8)
        o_ref[...] = st[1]
        tails_ref[...] = st[2]

        @pl.when((p == HEADS // 2 - 1) & (i == nq - 1))
        def _():
            exchange.wait(x_refs, out_refs, sems)

    blk = pl.BlockSpec((BQ, LANES), lambda p, i: (i, p))
    out = jax.ShapeDtypeStruct((s_len, SB_W), F32)
    res = pl.pallas_call(
        body, name="sb_fwd", grid=(HEADS // 2, nq),
        in_specs=[pl.BlockSpec((BQ, LANES), lambda p, i: (i, qc + p)),
                  pl.BlockSpec((s_len, LANES), lambda p, i: (0, kc + p)),
                  pl.BlockSpec((s_len, LANES), lambda p, i: (0, vc + p))] + exchange.in_specs,
        out_specs=[blk, blk] + exchange.out_specs, out_shape=[out, out] + exchange.out_shape,
        scratch_shapes=exchange.scratch,
        compiler_params=_cparams("arbitrary", "arbitrary"),
    )(proj, proj, proj, *exchange.arrays)
    return res[:2], res[2:]


def _sb_backward(proj, tails, do, s_len, exchange):
    nq = s_len // BQ
    qc, kc, vc = O_SBQ // LANES, O_SBK // LANES, O_SBV // LANES
    n_ex = len(exchange.arrays)

    def body(q_ref, k_ref, v_ref, tails_ref, do_ref, *rest):
        x_refs, (dq_ref, dk_out, dv_out) = rest[:n_ex], rest[n_ex:n_ex + 3]
        out_refs, (dk_ref, dv_ref), sems = rest[n_ex + 3:2 * n_ex + 3], rest[2 * n_ex + 3:2 * n_ex + 5], rest[2 * n_ex + 5:]
        p = pl.program_id(0)
        i = pl.program_id(1)
        jd = i

        @pl.when((p == 0) & (i == 0))
        def _():
            exchange.start(x_refs, out_refs, sems)

        @pl.when(i == 0)
        def _():
            dk_ref[...] = jnp.zeros(dk_ref.shape, F32)
            dv_ref[...] = jnp.zeros(dv_ref.shape, F32)

        row, col, trow, tcol, lane, klane = _attn_consts()
        tri = (trow >= tcol).astype(MXU_DTYPE)
        tri_p = (trow <= tcol).astype(MXU_DTYPE)
        q = q_ref[...] * 0.125
        tails_blk = tails_ref[...]
        do_blk = do_ref[...]
        hm = [lane < HD, lane >= HD]
        km = [klane < HD, klane >= HD]
        qh = [jnp.where(m, q, 0.0).astype(MXU_DTYPE) for m in hm]
        doh = [jnp.where(m, do_blk, 0.0).astype(MXU_DTYPE) for m in hm]

        def step(js, st, masks):
            before, dq = st
            chains = _chains(js)
            kj = [k_ref[_key_slice(j), :].astype(MXU_DTYPE) for j in js]
            vj = [v_ref[_key_slice(j), :].astype(MXU_DTYPE) for j in js]
            z = {(h, t): _dot_nt(qh[h], kj[t]) for h, t in chains}
            da = {(h, t): _dot_nt(doh[h], vj[t]) for h, t in chains}
            suf, sig = {}, {}
            for h, t in chains:
                lom = _neg_softplus(z[h, t])
                if masks:
                    lom = jnp.where(masks[t], lom, 0.0)
                suf[h, t] = _running_sum(lom, tri)
                sig[h, t] = jnp.exp(z[h, t] + lom)
            run = list(before)
            dl, pre, before_in = {}, {}, {}
            dk_add, dv_add = [None] * len(js), [None] * len(js)
            for h, t in chains:
                tail = _rowsum(jnp.where(lane == h * HD + js[t], tails_blk, 0.0))
                a = jnp.exp(z[h, t] + suf[h, t] + tail)
                if masks:
                    a = jnp.where(masks[t], a, 0.0)
                dl[h, t] = da[h, t] * a
                pre[h, t] = _dot(dl[h, t].astype(MXU_DTYPE), tri_p)
                dv_h = _dot_tn(a.astype(MXU_DTYPE), doh[h])
                dv_add[t] = dv_h if dv_add[t] is None else dv_add[t] + dv_h
                before_in[h, t] = run[h]
                run[h] = run[h] + _rowsum(dl[h, t])
            for h, t in chains:
                upto = before_in[h, t] + pre[h, t]
                dz = dl[h, t] - sig[h, t] * upto
                if masks:
                    dz = jnp.where(masks[t], dz, 0.0)
                dzb = dz.astype(MXU_DTYPE)
                dq = dq + _dot(dzb, jnp.where(km[h], kj[t], 0))
                dk_h = _dot_tn(dzb, qh[h])
                dk_add[t] = dk_h if dk_add[t] is None else dk_add[t] + dk_h
            for t, j in enumerate(js):
                dk_ref[_key_slice(j), :] += dk_add[t]
                dv_ref[_key_slice(j), :] += dv_add[t]
            return tuple(run), dq

        st = ((jnp.zeros((BQ, 1), F32),) * 2, jnp.zeros((BQ, LANES), F32))
        diag_js, masks, left = _diag_step(jd, False, col < row, group=2)
        st = _walk_blocks(lambda js, s: step(js, s, None), st, left, False, group=8)
        st = step(diag_js, st, masks)
        dq_ref[...] = (st[1] * 0.125).astype(dq_ref.dtype)

        @pl.when(i == nq - 1)
        def _():
            dk_out[...] = dk_ref[...].astype(dk_out.dtype)
            dv_out[...] = dv_ref[...].astype(dv_out.dtype)

        @pl.when((p == HEADS // 2 - 1) & (i == nq - 1))
        def _():
            exchange.wait(x_refs, out_refs, sems)

    blk = pl.BlockSpec((BQ, LANES), lambda p, i: (i, p))
    full = pl.BlockSpec((s_len, LANES), lambda p, i: (0, p))
    out = jax.ShapeDtypeStruct((s_len, SB_W), BF16)
    res = pl.pallas_call(
        body, name="sb_bwd", grid=(HEADS // 2, nq),
        in_specs=[pl.BlockSpec((BQ, LANES), lambda p, i: (i, qc + p)),
                  pl.BlockSpec((s_len, LANES), lambda p, i: (0, kc + p)),
                  pl.BlockSpec((s_len, LANES), lambda p, i: (0, vc + p)),
                  blk, blk] + exchange.in_specs,
        out_specs=[blk, full, full] + exchange.out_specs, out_shape=[out, out, out] + exchange.out_shape,
        scratch_shapes=[pltpu.VMEM((s_len, LANES), F32)] * 2 + exchange.scratch,
        compiler_params=_cparams("arbitrary", "arbitrary"),
    )(proj, proj, proj, tails, do, *exchange.arrays)
    return res[:3], res[3:]


def _pair_mask(rows, h):
    lane = lax.broadcasted_iota(jnp.int32, (rows, 2 * LANES), 1)
    rot = lane - LANES
    return (((lane < LANES) & (lane // HD == h))
            | ((lane >= LANES) & (rot < 2 * ROPE) & ((rot // (ROPE // 2)) % 2 == h)))


def _mla_forward(q_cat, k_cat, kv, s_len):
    nq = s_len // BQ
    scale = 1.0 / math.sqrt(QK_DIM)

    def body(q_ref, k_ref, v_ref, o_ref, lse_ref):
        i = pl.program_id(1)
        jd = i
        row, col, trow, tcol, lane, klane = _attn_consts()
        q = q_ref[...]
        hm = [lane < HD, lane >= HD]
        km = [klane < HD, klane >= HD]
        qh = [jnp.where(_pair_mask(BQ, h), q, 0) for h in range(2)]

        def step(js, st, masks):
            m_run, l_run, acc = st
            chains = _chains(js)
            kj = [k_ref[_key_slice(j), :] for j in js]
            vj = [v_ref[_key_slice(j), :].astype(MXU_DTYPE) for j in js]
            s = {}
            for h, t in chains:
                s[h, t] = _dot_nt(qh[h], kj[t]) * scale
                if masks:
                    s[h, t] = jnp.where(masks[t], s[h, t], -jnp.inf)
            m_new, alpha, l_new = [], [], []
            for h in range(2):
                top = m_run[h]
                for t in range(len(js)):
                    top = jnp.maximum(top, jnp.max(s[h, t], axis=1, keepdims=True))
                m_new.append(top)
                alpha.append(jnp.exp(m_run[h] - top))
                l_new.append(alpha[h] * l_run[h])
            add = None
            for h, t in chains:
                pr = jnp.exp(s[h, t] - m_new[h])
                l_new[h] = l_new[h] + _rowsum(pr)
                part = _dot(pr.astype(MXU_DTYPE), jnp.where(km[h], vj[t], 0))
                add = part if add is None else add + part
            acc = jnp.where(hm[0], alpha[0], alpha[1]) * acc + add
            return tuple(m_new), tuple(l_new), acc

        st = ((jnp.full((BQ, 1), -1e30, F32),) * 2, (jnp.zeros((BQ, 1), F32),) * 2, jnp.zeros((BQ, LANES), F32))
        diag_js, masks, left = _diag_step(jd, True, col <= row)
        st = step(diag_js, st, masks)
        m_run, l_run, acc = _walk_blocks(lambda js, s: step(js, s, None), st, left, True, group=8)
        o_ref[...] = acc / jnp.where(hm[0], l_run[0], l_run[1])
        lse_ref[...] = jnp.where(hm[0], m_run[0] + jnp.log(l_run[0]), m_run[1] + jnp.log(l_run[1]))

    blk = pl.BlockSpec((BQ, LANES), lambda p, i: (i, p))
    out = jax.ShapeDtypeStruct((s_len, MLA_W), F32)
    return pl.pallas_call(
        body, name="mla_fwd", grid=(HEADS // 2, nq),
        in_specs=[pl.BlockSpec((BQ, 2 * LANES), lambda p, i: (i, p)),
                  pl.BlockSpec((s_len, 2 * LANES), lambda p, i: (0, p)),
                  pl.BlockSpec((s_len, LANES), lambda p, i: (0, MLA_W // LANES + p))],
        out_specs=[blk, blk], out_shape=[out, out],
        compiler_params=_cparams("parallel", "parallel"),
    )(q_cat, k_cat, kv)


def _mla_backward(q_cat, k_cat, kv, o, lse, do, s_len):
    nq = s_len // BQ
    scale = 1.0 / math.sqrt(QK_DIM)

    def body(q_ref, k_ref, v_ref, o_ref, lse_ref, do_ref, dq_ref, dk_ref, dv_ref):
        i = pl.program_id(1)

        @pl.when(i == 0)
        def _():
            dk_ref[...] = jnp.zeros(dk_ref.shape, F32)
            dv_ref[...] = jnp.zeros(dv_ref.shape, F32)

        jd = i
        row, col, trow, tcol, lane, klane = _attn_consts()
        q = q_ref[...]
        o_blk = o_ref[...]
        do_blk = do_ref[...]
        lse_blk = lse_ref[...]
        hm = [lane < HD, lane >= HD]
        kpm = [_pair_mask(BK, h) for h in range(2)]
        qh = [jnp.where(_pair_mask(BQ, h), q, 0) for h in range(2)]
        doh_f = [jnp.where(m, do_blk, 0.0) for m in hm]
        doh = [d.astype(MXU_DTYPE) for d in doh_f]
        delta = [jnp.sum(d * o_blk, axis=1, keepdims=True) for d in doh_f]
        lse_h = [jnp.sum(jnp.where(lane == h * HD, lse_blk, 0.0), axis=1, keepdims=True) for h in range(2)]

        def step(js, st, masks):
            dq = st
            chains = _chains(js)
            kj = [k_ref[_key_slice(j), :] for j in js]
            vj = [v_ref[_key_slice(j), :].astype(MXU_DTYPE) for j in js]
            s = {(h, t): _dot_nt(qh[h], kj[t]) for h, t in chains}
            dp = {(h, t): _dot_nt(doh[h], vj[t]) for h, t in chains}
            adds = [[None] * len(js) for _ in range(2)]

            def accumulate(slot, t, part):
                adds[slot][t] = part if adds[slot][t] is None else adds[slot][t] + part

            for h, t in chains:
                pr = jnp.exp(s[h, t] * scale - lse_h[h])
                if masks:
                    pr = jnp.where(masks[t], pr, 0.0)
                dsb = (pr * (dp[h, t] - delta[h]) * scale).astype(MXU_DTYPE)
                dq = dq + _dot(dsb, jnp.where(kpm[h], kj[t], 0))
                accumulate(0, t, _dot_tn(dsb, qh[h]))
                accumulate(1, t, _dot_tn(pr.astype(MXU_DTYPE), doh[h]))
            for t, j in enumerate(js):
                dk_ref[_key_slice(j), :] += adds[0][t]
                dv_ref[_key_slice(j), :] += adds[1][t]
            return dq

        diag_js, masks, left = _diag_step(jd, True, col <= row)
        st = step(diag_js, jnp.zeros((BQ, 2 * LANES), F32), masks)
        dq_ref[...] = _walk_blocks(lambda js, s: step(js, s, None), st, left, True, group=8)

    blk = pl.BlockSpec((BQ, LANES), lambda p, i: (i, p))
    full = pl.BlockSpec((s_len, LANES), lambda p, i: (0, p))
    out = jax.ShapeDtypeStruct((s_len, MLA_W), F32)
    out_cat = jax.ShapeDtypeStruct((s_len, 2 * MLA_W), F32)
    return pl.pallas_call(
        body, name="mla_bwd", grid=(HEADS // 2, nq),
        in_specs=[pl.BlockSpec((BQ, 2 * LANES), lambda p, i: (i, p)),
                  pl.BlockSpec((s_len, 2 * LANES), lambda p, i: (0, p)),
                  pl.BlockSpec((s_len, LANES), lambda p, i: (0, MLA_W // LANES + p)),
                  blk, blk, blk],
        out_specs=[pl.BlockSpec((BQ, 2 * LANES), lambda p, i: (i, p)),
                   pl.BlockSpec((s_len, 2 * LANES), lambda p, i: (0, p)), full],
        out_shape=[out_cat, out_cat, out],
        compiler_params=_cparams("arbitrary", "arbitrary"),
    )(q_cat, k_cat, kv, o, lse, do)


def _mesh_pos():
    return lax.axis_index("x"), lax.axis_index("y"), lax.axis_index("c")


def _dev_index(px, py, pc):
    return 4 * px + 2 * py + pc


def _all_gather(block, name):
    return _all_gather_parts([block], name)[0]


def _all_gather_parts(blocks, name):
    n = len(blocks)

    def body(*refs):
        x_refs, out_refs = refs[:n], refs[n:2 * n]
        send_sems, recv_sems, local_sems = refs[2 * n:]
        x, y, c = _mesh_pos()
        me, sibling = (x, y, c), (x, y, 1 - c)
        chips = [(1 - x, y), (x, 1 - y), (1 - x, 1 - y)]

        def copy(a, k, blockpos, to, src=None):
            slot = out_refs[a].at[_dev_index(*blockpos)]
            return pltpu.make_async_remote_copy(
                src_ref=slot if src is None else src, dst_ref=slot,
                send_sem=send_sems.at[7 * a + k], recv_sem=recv_sems.at[7 * a + k],
                device_id=to, device_id_type=pl.DeviceIdType.MESH)

        mine = [pltpu.make_async_copy(x_refs[a], out_refs[a].at[_dev_index(*me)], local_sems.at[a]) for a in range(n)]
        for cp in mine:
            cp.start()
        first = []
        for a in range(n):
            first.append(copy(a, 0, me, sibling, src=x_refs[a]))
            first += [copy(a, 1 + j, me, (*chip, c), src=x_refs[a]) for j, chip in enumerate(chips)]
        for cp in first:
            cp.start()
        passed = []
        for j, chip in enumerate(chips):
            for a in range(n):
                copy(a, 1 + j, (*chip, c), me).wait_recv()
                passed.append(copy(a, 4 + j, (*chip, c), sibling))
                passed[-1].start()
        for a in range(n):
            copy(a, 0, sibling, me).wait_recv()
            for j, chip in enumerate(chips):
                copy(a, 4 + j, (*chip, 1 - c), me).wait_recv()
        for cp in first + passed:
            cp.wait_send()
        for cp in mine:
            cp.wait()

    return pl.pallas_call(
        body, name=name,
        out_shape=[jax.ShapeDtypeStruct((N_DEV,) + b.shape, b.dtype) for b in blocks],
        in_specs=[pl.BlockSpec(memory_space=pl.ANY)] * n, out_specs=[pl.BlockSpec(memory_space=pl.ANY)] * n,
        scratch_shapes=[pltpu.SemaphoreType.DMA((7 * n,)), pltpu.SemaphoreType.DMA((7 * n,)),
                        pltpu.SemaphoreType.DMA((n,))],
    )(*blocks)


class _Exchange:
    def __init__(self, arrays):
        self.arrays = list(arrays)
        n = len(self.arrays)
        self.in_specs = [pl.BlockSpec(memory_space=pl.ANY)] * n
        self.out_specs = [pl.BlockSpec(memory_space=pl.ANY)] * n
        self.out_shape = [jax.ShapeDtypeStruct(a.shape, a.dtype) for a in self.arrays]
        self.scratch = [pltpu.SemaphoreType.DMA((7 * n,)), pltpu.SemaphoreType.DMA((7 * n,)),
                        pltpu.SemaphoreType.DMA((n,))]

    def _copies(self, x_refs, out_refs, sems, with_arrivals):
        send_sems, recv_sems, local_sems = sems
        x, y, c = _mesh_pos()
        me = _dev_index(x, y, c)
        flips = [(fx, fy, fc) for fx in (0, 1) for fy in (0, 1) for fc in (0, 1)][1:]
        peers = [(1 - x if fx else x, 1 - y if fy else y, 1 - c if fc else c) for fx, fy, fc in flips]
        mine, sends, arrivals = [], [], []
        for a in range(len(self.arrays)):
            mine.append(pltpu.make_async_copy(x_refs[a].at[me], out_refs[a].at[me], local_sems.at[a]))
            for k, peer in enumerate(peers):
                sends.append(pltpu.make_async_remote_copy(
                    src_ref=x_refs[a].at[_dev_index(*peer)], dst_ref=out_refs[a].at[me],
                    send_sem=send_sems.at[7 * a + k], recv_sem=recv_sems.at[7 * a + k],
                    device_id=peer, device_id_type=pl.DeviceIdType.MESH))
                if not with_arrivals:
                    continue
                arrivals.append(pltpu.make_async_remote_copy(
                    src_ref=x_refs[a].at[me], dst_ref=out_refs[a].at[_dev_index(*peer)],
                    send_sem=send_sems.at[7 * a + k], recv_sem=recv_sems.at[7 * a + k],
                    device_id=peer, device_id_type=pl.DeviceIdType.MESH))
        return mine, sends, arrivals

    def start(self, x_refs, out_refs, sems):
        mine, sends, _ = self._copies(x_refs, out_refs, sems, False)
        for cp in mine + sends:
            cp.start()

    def wait(self, x_refs, out_refs, sems):
        mine, sends, arrivals = self._copies(x_refs, out_refs, sems, True)
        for cp in arrivals:
            cp.wait_recv()
        for cp in sends:
            cp.wait_send()
        for cp in mine:
            cp.wait()


def _sum_blocks(parts, name):
    n, r, c = parts.shape
    row_tiles = [t for t in range(16, min(r, 2048) + 1, 16) if r % t == 0]
    if row_tiles:
        tr, tc = max(row_tiles), c
    else:
        tr, tc = r, 2 * LANES
    assert c % tc == 0

    def body(p_ref, o_ref):
        acc = p_ref[0].astype(F32)
        for s in range(1, n):
            acc = acc + p_ref[s].astype(F32)
        o_ref[...] = acc

    return pl.pallas_call(
        body, name=name, grid=(r // tr, c // tc),
        in_specs=[pl.BlockSpec((n, tr, tc), lambda i, j: (0, i, j))],
        out_specs=pl.BlockSpec((tr, tc), lambda i, j: (i, j)),
        out_shape=jax.ShapeDtypeStruct((r, c), F32),
        compiler_params=_cparams("parallel", "parallel"),
    )(parts)


def _sigmoid(x):
    return 1.0 / (1.0 + jnp.exp(-x))


def _silu(x):
    return x * _sigmoid(x)


def _silu_grad(x):
    s = _sigmoid(x)
    return s * (1.0 + x * (1.0 - s))


def _colsum(x):
    return jnp.sum(x, axis=0, keepdims=True)


def _rms(x):
    return lax.rsqrt(jnp.mean(x * x, axis=-1, keepdims=True) + EPS)


def _rms_bwd(xn, r, dxn):
    return r * (dxn - xn * jnp.mean(dxn * xn, axis=-1, keepdims=True))


def _adamw(w, g, m, v):
    m = ADAM_B1 * m + (1.0 - ADAM_B1) * g
    v = ADAM_B2 * v + (1.0 - ADAM_B2) * jnp.square(g)
    m_hat = m / (1.0 - ADAM_B1 ** ADAM_STEP)
    v_hat = v / (1.0 - ADAM_B2 ** ADAM_STEP)
    delta = -ADAM_LR * (m_hat / (jnp.sqrt(v_hat) + ADAM_EPS) + ADAM_WD * w)
    return delta, m, v


def _adamw_call(w, g, m, v, name):
    r, c = w.shape
    if r % 256 == 0:
        tr, tc = 256, c
    elif r * c <= 256 * 1024 or c % (2 * LANES):
        tr, tc = r, c
    else:
        tr, tc = r, 2 * LANES

    def body(w_ref, g_ref, m_ref, v_ref, d_out, m_out, v_out):
        d_out[...], m_out[...], v_out[...] = _adamw(w_ref[...], g_ref[...], m_ref[...], v_ref[...])

    spec = pl.BlockSpec((tr, tc), lambda i, j: (i, j))
    return pl.pallas_call(
        body, name=name, grid=(r // tr, c // tc), in_specs=[spec] * 4, out_specs=[spec] * 3,
        out_shape=[jax.ShapeDtypeStruct((r, c), F32)] * 3, compiler_params=_cparams("parallel", "parallel"),
    )(w, g, m, v)


def _uq_to_kernel_layout(w):
    lead = w.shape[:-1]
    t = w.reshape(lead + (HEADS, QK_DIM))
    return jnp.concatenate([t[..., :NOPE].reshape(lead + (HEADS * NOPE,)),
                            t[..., NOPE:NOPE + ROPE // 2].reshape(lead + (LANES,)),
                            t[..., NOPE + ROPE // 2:].reshape(lead + (LANES,))], axis=-1)


def _uq_from_kernel_layout(w):
    lead = w.shape[:-1]
    nope = w[..., :HEADS * NOPE].reshape(lead + (HEADS, NOPE))
    r1 = w[..., HEADS * NOPE:HEADS * NOPE + LANES].reshape(lead + (HEADS, ROPE // 2))
    r2 = w[..., HEADS * NOPE + LANES:].reshape(lead + (HEADS, ROPE // 2))
    return jnp.concatenate([nope, r1, r2], axis=-1).reshape(lead + (HEADS * QK_DIM,))


def _ukv_to_kernel_layout(w):
    lead = w.shape[:-1]
    t = w.reshape(lead + (HEADS, NOPE + HD))
    return jnp.concatenate([t[..., :NOPE].reshape(lead + (HEADS * NOPE,)),
                            t[..., NOPE:].reshape(lead + (HEADS * HD,))], axis=-1)


def _ukv_from_kernel_layout(w):
    lead = w.shape[:-1]
    kn = w[..., :HEADS * NOPE].reshape(lead + (HEADS, NOPE))
    vv = w[..., HEADS * NOPE:].reshape(lead + (HEADS, HD))
    return jnp.concatenate([kn, vv], axis=-1).reshape(lead + (HEADS * (NOPE + HD),))


def _w_in_t_to_kernel_layout(wt):
    sb = wt[0:2048]
    c_q = wt[2048:2432]
    c_kv = wt[2432:2688]
    k_rot = wt[2688:2720]
    mla_z = wt[2720:3232]
    gates = wt[3232:5280]
    zeros = jnp.zeros((LANES, wt.shape[1]), wt.dtype)
    k1 = jnp.tile(k_rot[:ROPE // 2], (HEADS, 1))
    k2 = jnp.tile(k_rot[ROPE // 2:], (HEADS, 1))
    return jnp.concatenate([gates, sb, mla_z, c_q, zeros, c_kv, k1, k2], axis=0)


def _w_in_t_from_kernel_layout(gt, g_rot):
    return jnp.concatenate([gt[O_SBQ:O_SBQ + 2048], gt[O_CQ:O_CQ + Q_RANK], gt[O_CKV:O_CKV + KV_RANK],
                            g_rot.astype(gt.dtype), gt[O_MLAZ:O_MLAZ + MLA_W], gt[O_GA:O_GA + 2 * D]], axis=0)


def kernel(x, c, positions, w_ada, b_ada, norm_gain, w_in, q_norm_gain, w_uq, kv_norm_gain, w_ukv, w_branch_a, w_branch_b, w_out, final_norm_gain, loss_target, m_w_ada, m_b_ada, m_norm_gain, m_w_in, m_q_norm_gain, m_w_uq, m_kv_norm_gain, m_w_ukv, m_w_branch_a, m_w_branch_b, m_w_out, m_final_norm_gain, v_w_ada, v_b_ada, v_norm_gain, v_w_in, v_q_norm_gain, v_w_uq, v_kv_norm_gain, v_w_ukv, v_w_branch_a, v_w_branch_b, v_w_out, v_final_norm_gain):
    s_len = x.shape[1]
    me = _dev_index(*_mesh_pos())
    x2d = x[0]
    tgt = loss_target[0]

    w_in_t = w_in[0].T.astype(BF16)
    big = [w_uq[0], w_ukv[0], w_branch_a[0], w_branch_b[0], w_out[0]]
    big_sizes = [int(w.size) for w in big]
    packed = jnp.concatenate([w.astype(BF16).reshape(-1, LANES) for w in big], axis=0)
    g_in_t, c_all = _all_gather_parts([w_in_t, c.reshape(8, LANES)], "gather_w_in")
    c_all = c_all.reshape(N_DEV, D)
    w_in_kt = _w_in_t_to_kernel_layout(g_in_t.reshape(N_DEV * w_in_t.shape[0], D))

    mod_cols = _mm(c_all, w_ada[0], name="ada_mod")
    mod_all = _all_gather(mod_cols, "gather_mod")
    mod = lax.dynamic_index_in_dim(mod_all, me, axis=1, keepdims=False).reshape(1, 3 * D)
    mod_shift, mod_scale, mod_gate = mod[:, :D], mod[:, D:2 * D], mod[:, 2 * D:]
    b_shift, b_scale, b_gate = b_ada[:, :D], b_ada[:, D:2 * D], b_ada[:, 2 * D:]
    g1 = norm_gain
    gq, gkv = q_norm_gain, kv_norm_gain
    gf = final_norm_gain.reshape(1, D)

    def f_h(x_, g1_, ms, bs, msc, bsc):
        xn = x_ * _rms(x_)
        return (xn * g1_ * (1.0 + (msc + bsc)) + (ms + bs),), ()

    (h,) = _rowwise(f_h, [x2d], [g1, mod_shift, b_shift, mod_scale, b_scale], [(D, BF16)], name="ada_norm")
    proj = _mm(h, w_in_kt, tb=True, name="proj_in", tiles=(min(s_len, 1024), IN_PAD // 2, D))

    (o_a, sb_tails), (gathered,) = _sb_forward(
        proj, s_len, _Exchange([jnp.broadcast_to(packed[None], (N_DEV,) + packed.shape)]))
    offs = [0]
    for n in big_sizes:
        offs.append(offs[-1] + n // LANES)

    def unpack(t, shape):
        return gathered[:, offs[t]:offs[t + 1], :].reshape((N_DEV,) + shape)

    def cols(t, shape):
        return unpack(t, shape).transpose(1, 0, 2).reshape(shape[0], N_DEV * shape[1])

    w_uq_k = _uq_to_kernel_layout(cols(0, big[0].shape))
    w_ukv_k = _ukv_to_kernel_layout(cols(1, big[1].shape))
    w_a_f = cols(2, big[2].shape)
    w_b_f = cols(3, big[3].shape)
    w_out_f = unpack(4, big[4].shape).reshape(D, D)

    def f_lat(cq, ckv, gq_, gkv_):
        return (cq * _rms(cq) * gq_, ckv * _rms(ckv) * gkv_), ()

    cq_n, ckv_n = _rowwise(f_lat, [(proj, O_CQ // Q_RANK, Q_RANK), (proj, O_CKV // KV_RANK, KV_RANK)], [gq, gkv],
                           [(Q_RANK, BF16), (KV_RANK, BF16)], name="latent_norm")
    q_mla = _mm(cq_n, w_uq_k, name="q_up")
    kv = _mm(ckv_n, w_ukv_k, out_dtype=BF16, name="kv_up")

    inv_freq = ROPE_BASE ** (-jnp.arange(0, ROPE, 2, dtype=F32) / ROPE)
    inv_freq_t = jnp.tile(inv_freq, HEADS).reshape(1, LANES)
    pos_col = positions.reshape(s_len, 1).astype(F32)

    pairs = HEADS // 2

    def f_rope(pos, qn, q1, q2, kn, k1, k2, freq):
        ang = pos * freq
        cs, sn = jnp.cos(ang), jnp.sin(ang)
        q1r, q2r = q1 * cs - q2 * sn, q1 * sn + q2 * cs
        k1r, k2r = k1 * cs - k2 * sn, k1 * sn + k2 * cs
        lane = lax.broadcasted_iota(jnp.int32, q1.shape, 1)
        first, second = lane < ROPE, (lane >= ROPE) & (lane < 2 * ROPE)
        k_rot = jnp.where(first, k1r, jnp.where(second, k2r, 0.0))
        q_parts, k_parts = [], []
        for p in range(pairs):
            q_rot = jnp.where(first, pltpu.roll(q1r, (LANES - ROPE * p) % LANES, 1),
                              jnp.where(second, pltpu.roll(q2r, (LANES + ROPE - ROPE * p) % LANES, 1), 0.0))
            q_parts += [qn[:, LANES * p:LANES * (p + 1)], q_rot]
            k_parts += [kn[:, LANES * p:LANES * (p + 1)], k_rot]
        return (jnp.concatenate(q_parts, axis=1), jnp.concatenate(k_parts, axis=1), cs, sn), ()

    q_cat, k_cat, cos_t, sin_t = _rowwise(
        f_rope, [pos_col, (q_mla, 0, MLA_W), (q_mla, 4, LANES), (q_mla, 5, LANES), (kv, 0, MLA_W),
                 (proj, O_KROT // LANES, LANES), (proj, O_KROT // LANES + 1, LANES)], [inv_freq_t],
        [(2 * MLA_W, BF16), (2 * MLA_W, BF16), (LANES, F32), (LANES, F32)], name="rope")

    o_b, lse = _mla_forward(q_cat, k_cat, kv, s_len)

    def f_gate(oa, za, ob, zb):
        return (oa * _silu(za), ob * _silu(zb)), ()

    ya_in, yb_in = _rowwise(f_gate, [o_a, (proj, O_SBZ // SB_W, SB_W), o_b, (proj, O_MLAZ // MLA_W, MLA_W)], [],
                            [(SB_W, BF16), (MLA_W, BF16)], name="branch_gate")
    y_a = _mm(ya_in, w_a_f, out_dtype=BF16, name="branch_a")

    def f_merge(yb, ga, gb, ya):
        return (yb, _sigmoid(ga) * ya + _sigmoid(gb) * yb), ()

    y_b, merged = _mm(yb_in, w_b_f, name="branch_b_merge", tiles=(min(s_len, 512), D, MLA_W),
                      epilogue=(f_merge, [(proj, O_GA // D, D), (proj, O_GB // D, D), y_a], [], [(D, BF16), (D, BF16)], []))

    def f_loss(out_, x_, t_, mg, bg, gf_):
        gate = mg + bg
        x2 = x_ + gate * out_
        r2 = _rms(x2)
        xn2 = x2 * r2
        err = xn2 * gf_ - t_
        loss = jnp.full((1, LANES), 0.5 / D, F32) * jnp.sum(err * err)
        dy = err * (1.0 / D)
        dx2 = _rms_bwd(xn2, r2, dy * gf_)
        return (dx2, dx2 * gate), (loss, _colsum(dy * xn2), _colsum(dx2 * out_))

    dx2, d_out, loss_part, d_gf, d_gate = _mm(
        merged, w_out_f, name="out_proj_loss", tiles=(min(s_len, 512), D, D),
        epilogue=(f_loss, [x2d, tgt], [mod_gate, b_gate, gf], [(D, F32), (D, BF16)], [LANES, D, D]))

    dw_out = _mm(merged, d_out, ta=True, name="dw_out")

    def f_dmerge(dm, ga, gb, ya, yb):
        sa, sb = _sigmoid(ga), _sigmoid(gb)
        return (dm * sa, dm * sb, dm * ya * sa * (1.0 - sa), dm * yb * sb * (1.0 - sb)), ()

    d_ya, d_yb, d_ga, d_gb = _mm(
        d_out, w_out_f, tb=True, name="d_merge", tiles=(min(s_len, 256), D, D),
        epilogue=(f_dmerge, [(proj, O_GA // D, D), (proj, O_GB // D, D), y_a, y_b], [], [(D, BF16)] * 4, []))
    dw_a = _mm(ya_in, d_ya, ta=True, name="dw_branch_a")
    dw_b = _mm(yb_in, d_yb, ta=True, name="dw_branch_b")

    def f_dgate(d_in, o_, z_):
        return (d_in * _silu(z_), d_in * o_ * _silu_grad(z_)), ()

    d_oa, d_sbz = _mm(d_ya, w_a_f, tb=True, name="d_branch_a",
                      epilogue=(f_dgate, [o_a, (proj, O_SBZ // SB_W, SB_W)], [], [(SB_W, BF16), (SB_W, BF16)], []))
    d_ob, d_mlaz = _mm(d_yb, w_b_f, tb=True, name="d_branch_b",
                       epilogue=(f_dgate, [o_b, (proj, O_MLAZ // MLA_W, MLA_W)], [], [(MLA_W, F32), (MLA_W, BF16)], []))

    dq_cat, dk_cat, dv_b = _mla_backward(q_cat, k_cat, kv, o_b, lse, d_ob, s_len)

    def f_drope(dq, dk, dv_, cs, sn):
        lane = lax.broadcasted_iota(jnp.int32, cs.shape, 1)
        first, second = lane < ROPE, (lane >= ROPE) & (lane < 2 * ROPE)
        dq1 = dq2 = dk1 = dk2 = None
        for p in range(pairs):
            q_rot = dq[:, LANES * (2 * p + 1):LANES * (2 * p + 2)]
            k_rot = dk[:, LANES * (2 * p + 1):LANES * (2 * p + 2)]
            parts = (pltpu.roll(jnp.where(first, q_rot, 0.0), (ROPE * p) % LANES, 1),
                     pltpu.roll(jnp.where(second, q_rot, 0.0), (LANES - ROPE + ROPE * p) % LANES, 1),
                     jnp.where(first, k_rot, 0.0), jnp.where(second, k_rot, 0.0))
            if p == 0:
                dq1, dq2, dk1, dk2 = parts
            else:
                dq1, dq2, dk1, dk2 = dq1 + parts[0], dq2 + parts[1], dk1 + parts[2], dk2 + parts[3]
        dqn_ = [dq[:, 2 * LANES * p:2 * LANES * p + LANES] for p in range(pairs)]
        dkn_ = [dk[:, 2 * LANES * p:2 * LANES * p + LANES] for p in range(pairs)]
        return (jnp.concatenate(dqn_ + [dq1 * cs + dq2 * sn, dq2 * cs - dq1 * sn], axis=1),
                jnp.concatenate(dkn_ + [dv_], axis=1),
                jnp.concatenate([dk1 * cs + dk2 * sn, dk2 * cs - dk1 * sn], axis=1)), ()

    dq_k, dkv_k, d_krot = _rowwise(f_drope, [dq_cat, dk_cat, dv_b, cos_t, sin_t], [],
                                   [(HEADS * QK_DIM, BF16), (2 * MLA_W, BF16), (2 * LANES, BF16)], name="d_rope")
    dw_uq_k = _mm(cq_n, dq_k, ta=True, name="dw_uq")
    dw_ukv_k = _mm(ckv_n, dkv_k, ta=True, name="dw_ukv")

    def f_dlat(d_normed, latent, gain):
        r = _rms(latent)
        normed = latent * r
        return (_rms_bwd(normed, r, d_normed * gain),), (_colsum(d_normed * normed),)

    d_cq, d_gq = _mm(dq_k, w_uq_k, tb=True, name="d_cq_norm",
                     epilogue=(f_dlat, [(proj, O_CQ // Q_RANK, Q_RANK)], [gq], [(Q_RANK, BF16)], [Q_RANK]))
    d_ckv, d_gkv = _mm(dkv_k, w_ukv_k, tb=True, name="d_ckv_norm",
                       epilogue=(f_dlat, [(proj, O_CKV // KV_RANK, KV_RANK)], [gkv], [(KV_RANK, BF16)], [KV_RANK]))

    def col_blocks(g):
        kdim, n8 = g.shape
        return g.astype(BF16).reshape(kdim, N_DEV, n8 // N_DEV).transpose(1, 0, 2).reshape(N_DEV, -1, LANES)

    g_blocks = jnp.concatenate([col_blocks(_uq_from_kernel_layout(dw_uq_k)), col_blocks(_ukv_from_kernel_layout(dw_ukv_k)),
                                col_blocks(dw_a), col_blocks(dw_b), dw_out.astype(BF16).reshape(N_DEV, -1, LANES)], axis=1)
    (d_sbq, d_sbk, d_sbv), (g_recv,) = _sb_backward(proj, sb_tails, d_oa, s_len, _Exchange([g_blocks]))

    d_proj = jnp.concatenate([d_ga, d_gb, d_sbq, d_sbk, d_sbv, d_sbz, d_mlaz,
                              d_cq, jnp.zeros((s_len, LANES), BF16), d_ckv, d_krot], axis=1)
    dw_in_kt = _mm(d_proj, h, ta=True, out_dtype=BF16, name="dw_in", tiles=(512, D, s_len))

    def krot_body(t_ref, o_ref):
        half = ROPE // 2
        for part in range(2):
            acc = t_ref[part * LANES:part * LANES + half, :].astype(F32)
            for hh in range(1, HEADS):
                acc = acc + t_ref[part * LANES + hh * half:part * LANES + (hh + 1) * half, :].astype(F32)
            o_ref[part * half:(part + 1) * half, :] = acc

    dw_krot = pl.pallas_call(krot_body, name="dw_krot_sum", out_shape=jax.ShapeDtypeStruct((ROPE, D), F32))(
        dw_in_kt[O_KROT:O_KROT + 2 * LANES])

    g_in_blocks = _w_in_t_from_kernel_layout(dw_in_kt, dw_krot).reshape(N_DEV, -1, D)
    def f_dx(dh_, x_, dx2_, g1_, msc, bsc):
        r = _rms(x_)
        xn = x_ * r
        dn1 = dh_ * (1.0 + (msc + bsc))
        return ((dx2_ + _rms_bwd(xn, r, dn1 * g1_),),
                (_colsum(dh_), _colsum(dh_ * (xn * g1_)), _colsum(dn1 * xn)))

    (grad_x2d, d_shift, d_scale, d_g1), (g_in_recv,) = _mm(
        d_proj, w_in_kt, name="d_h", tiles=(min(s_len, 512), D, 512), exchange=_Exchange([g_in_blocks]),
        epilogue=(f_dx, [x2d, dx2], [g1, mod_scale, b_scale], [(D, F32)], [D, D, D]))

    g_in_sum_t = _sum_blocks(g_in_recv, "sum_grads_w_in")
    g_sum = _sum_blocks(g_recv, "sum_grads")
    g_big = [g_sum[offs[t]:offs[t + 1]].reshape(big[t].shape) for t in range(5)]

    small = jnp.concatenate([d_shift, d_scale, d_gate, d_g1, d_gq, d_gkv, d_gf, loss_part], axis=1)
    n_small = small.shape[1]
    pad = (-n_small) % (8 * LANES)
    small = jnp.pad(small, ((0, 0), (0, pad))).reshape(-1, LANES)
    small_all = _all_gather(small, "gather_small")
    small_sum = _sum_blocks(small_all, "sum_small").reshape(1, -1)
    g_b_ada = small_sum[:, :3 * D]
    g_g1 = small_sum[:, 3 * D:4 * D]
    g_gq = small_sum[:, 4 * D:4 * D + Q_RANK]
    g_gkv = small_sum[:, 4 * D + Q_RANK:4 * D + Q_RANK + KV_RANK]
    g_gf = small_sum[:, 4 * D + Q_RANK + KV_RANK:4 * D + Q_RANK + KV_RANK + D]

    dmod_all = small_all.reshape(N_DEV, -1)[:, :3 * D]
    dmod_cols = lax.dynamic_slice_in_dim(dmod_all, me * (3 * D // N_DEV), 3 * D // N_DEV, axis=1)
    g_w_ada = _mm(c_all, dmod_cols, ta=True, name="dw_ada")

    loss = small_sum[0, n_small - LANES]

    names = ["w_ada", "b_ada", "norm_gain", "w_in", "q_norm_gain", "w_uq", "kv_norm_gain", "w_ukv",
             "w_branch_a", "w_branch_b", "w_out", "final_norm_gain"]
    weights = dict(w_ada=w_ada, b_ada=b_ada, norm_gain=norm_gain, w_in=w_in, q_norm_gain=q_norm_gain, w_uq=w_uq,
                   kv_norm_gain=kv_norm_gain, w_ukv=w_ukv, w_branch_a=w_branch_a, w_branch_b=w_branch_b, w_out=w_out,
                   final_norm_gain=final_norm_gain)
    moms = dict(w_ada=m_w_ada, b_ada=m_b_ada, norm_gain=m_norm_gain, w_in=m_w_in, q_norm_gain=m_q_norm_gain,
                w_uq=m_w_uq, kv_norm_gain=m_kv_norm_gain, w_ukv=m_w_ukv, w_branch_a=m_w_branch_a,
                w_branch_b=m_w_branch_b, w_out=m_w_out, final_norm_gain=m_final_norm_gain)
    vels = dict(w_ada=v_w_ada, b_ada=v_b_ada, norm_gain=v_norm_gain, w_in=v_w_in, q_norm_gain=v_q_norm_gain,
                w_uq=v_w_uq, kv_norm_gain=v_kv_norm_gain, w_ukv=v_w_ukv, w_branch_a=v_w_branch_a,
                w_branch_b=v_w_branch_b, w_out=v_w_out, final_norm_gain=v_final_norm_gain)
    grads2d = dict(w_ada=g_w_ada, b_ada=g_b_ada, norm_gain=g_g1, w_in=g_in_sum_t, q_norm_gain=g_gq, w_uq=g_big[0],
                   kv_norm_gain=g_gkv, w_ukv=g_big[1], w_branch_a=g_big[2], w_branch_b=g_big[3], w_out=g_big[4],
                   final_norm_gain=g_gf)

    grads, deltas, new_m, new_v = [], [], [], []
    for n in names:
        w = weights[n]
        if n == "w_in":
            to2d = lambda t: t[0].T if t.ndim == 3 else t
            back = lambda t: t.T[None]
        else:
            shape2d = grads2d[n].shape
            to2d = lambda t, s=shape2d: t.reshape(s)
            back = lambda t, s=w.shape: t.reshape(s)
        g2d = to2d(grads2d[n])
        d_, m_, v_ = _adamw_call(to2d(w), g2d, to2d(moms[n]), to2d(vels[n]), "adamw_" + n)
        grads.append(back(g2d))
        deltas.append(back(d_))
        new_m.append(back(m_))
        new_v.append(back(v_))

    return (loss, grad_x2d.reshape(x.shape), *grads, *deltas, *new_m, *new_v)
```

```python
import functools
import math

import jax
import jax.numpy as jnp
from jax import lax
from jax.experimental import pallas as pl
from jax.experimental.pallas import tpu as pltpu

F32 = jnp.float32
BF16 = jnp.bfloat16
MXU_DTYPE = jnp.bfloat16

N_DEV = 8
D = 1024
HEADS = 8
HD = 64
SB_W = 512
MLA_W = 512
Q_RANK = 384
KV_RANK = 256
ROPE = 32
NOPE = 64
QK_DIM = NOPE + ROPE
EPS = 1e-6
ROPE_BASE = 10000.0

ADAM_LR = 0.001
ADAM_B1 = 0.9
ADAM_B2 = 0.999
ADAM_EPS = 1e-08
ADAM_WD = 0.01
ADAM_STEP = 10

LANES = 128
VMEM_LIMIT = 48 * 1024 * 1024

O_GA, O_GB = 0, 1024
O_SBQ, O_SBK, O_SBV, O_SBZ = 2048, 2560, 3072, 3584
O_MLAZ = 4096
O_CQ = 4608
O_CKV = 5120
O_KROT = 5376
IN_PAD = 5632

BQ = 256
BK = 256


def _cparams(*sem):
    return pltpu.CompilerParams(dimension_semantics=sem, vmem_limit_bytes=VMEM_LIMIT)


def _tile_of(n, cap=512):
    if n <= cap:
        return n
    for t in (1024, 768, 512, 384, 256, 128):
        if t <= cap and n % t == 0:
            return t
    raise ValueError(n)


def _rowwise(fn, rows, vecs, outs, reds=(), *, name, tile=512):
    norm = []
    for r in rows:
        if isinstance(r, tuple):
            arr, cb, w = r[:3]
            ro = r[3] if len(r) > 3 else 0
        else:
            arr, cb, w, ro = r, 0, r.shape[1], 0
        norm.append((arr, cb, w, ro))
    s_len = norm[0][0].shape[0]
    tile = min(tile, s_len)
    assert s_len % tile == 0
    n_row, n_vec, n_out, n_red = len(norm), len(vecs), len(outs), len(reds)

    def body(*refs):
        step = pl.program_id(0)
        row_refs = refs[:n_row]
        vec_refs = refs[n_row:n_row + n_vec]
        out_refs = refs[n_row + n_vec:n_row + n_vec + n_out]
        red_refs = refs[n_row + n_vec + n_out:]
        row_res, red_res = fn(*[r[...] for r in row_refs], *[v[...] for v in vec_refs])
        for o, val in zip(out_refs, row_res):
            o[...] = val.astype(o.dtype)
        if n_red:
            @pl.when(step == 0)
            def _():
                for r in red_refs:
                    r[...] = jnp.zeros(r.shape, r.dtype)
            for r, val in zip(red_refs, red_res):
                r[...] += val

    in_specs = []
    for arr, cb, w, ro in norm:
        in_specs.append(pl.BlockSpec((tile, w), functools.partial(lambda i, cb, rb: (i + rb, cb), cb=cb, rb=ro // tile)))
        assert ro % tile == 0
    for v in vecs:
        in_specs.append(pl.BlockSpec(v.shape, lambda i: (0, 0)))
    out_shape = [jax.ShapeDtypeStruct((s_len, w), dt) for w, dt in outs]
    out_specs = [pl.BlockSpec((tile, w), lambda i: (i, 0)) for w, _ in outs]
    out_shape += [jax.ShapeDtypeStruct((1, w), F32) for w in reds]
    out_specs += [pl.BlockSpec((1, w), lambda i: (0, 0)) for w in reds]
    res = pl.pallas_call(
        body, name=name, grid=(s_len // tile,), in_specs=in_specs, out_specs=out_specs, out_shape=out_shape,
        compiler_params=_cparams("arbitrary" if n_red else "parallel"),
    )(*[a for a, _, _, _ in norm], *vecs)
    return res


def _mm(a, b, *, ta=False, tb=False, out_dtype=F32, name, exchange=None, tiles=None, epilogue=None):
    m, k = (a.shape[1], a.shape[0]) if ta else a.shape
    n = b.shape[0] if tb else b.shape[1]
    assert (b.shape[1] if tb else b.shape[0]) == k
    tm, tn, tk = tiles or (_tile_of(m, 1024), _tile_of(n, 1024 if n <= 1024 else 512), _tile_of(k, 1024))
    assert m % tm == 0 and n % tn == 0 and k % tk == 0
    ni, nj, nk = m // tm, n // tn, k // tk
    dims = (((0 if ta else 1,), (1 if tb else 0,)), ((), ()))
    n_ex = len(exchange.arrays) if exchange else 0
    fn, rows, vecs, outs, reds = epilogue or (None, (), (), (), ())
    rows = [r if isinstance(r, tuple) else (r, 0, r.shape[1]) for r in rows]
    assert not epilogue or tn == n
    n_res = len(outs) + len(reds) if epilogue else 1

    def body(*refs):
        a_ref, b_ref = refs[:2]
        row_refs, refs = refs[2:2 + len(rows)], refs[2 + len(rows):]
        vec_refs, refs = refs[:len(vecs)], refs[len(vecs):]
        x_refs, refs = refs[:n_ex], refs[n_ex:]
        res_refs, refs = refs[:n_res], refs[n_res:]
        out_refs, acc_ref, sems = refs[:n_ex], refs[n_ex], refs[n_ex + 1:]
        i, j, kk = pl.program_id(0), pl.program_id(1), pl.program_id(2)
        first = (i == 0) & (j == 0) & (kk == 0)

        if exchange:
            @pl.when(first)
            def _():
                exchange.start(x_refs, out_refs, sems)

        if reds:
            @pl.when(first)
            def _():
                for r in res_refs[len(outs):]:
                    r[...] = jnp.zeros(r.shape, r.dtype)

        @pl.when(kk == 0)
        def _():
            acc_ref[...] = jnp.zeros(acc_ref.shape, F32)

        acc_ref[...] += lax.dot_general(a_ref[...].astype(MXU_DTYPE), b_ref[...].astype(MXU_DTYPE), dims,
                                        preferred_element_type=F32)

        @pl.when(kk == nk - 1)
        def _():
            if not epilogue:
                res_refs[0][...] = acc_ref[...].astype(res_refs[0].dtype)
                return
            row_res, red_res = fn(acc_ref[...], *[r[...] for r in row_refs], *[v[...] for v in vec_refs])
            for o, val in zip(res_refs, row_res):
                o[...] = val.astype(o.dtype)
            for r, val in zip(res_refs[len(outs):], red_res):
                r[...] += val

        if exchange:
            @pl.when((i == ni - 1) & (j == nj - 1) & (kk == nk - 1))
            def _():
                exchange.wait(x_refs, out_refs, sems)

    a_spec = pl.BlockSpec((tk, tm), lambda i, j, kk: (kk, i)) if ta else pl.BlockSpec((tm, tk), lambda i, j, kk: (i, kk))
    b_spec = pl.BlockSpec((tn, tk), lambda i, j, kk: (j, kk)) if tb else pl.BlockSpec((tk, tn), lambda i, j, kk: (kk, j))
    in_specs = [a_spec, b_spec]
    in_specs += [pl.BlockSpec((tm, w), functools.partial(lambda i, j, kk, cb: (i, cb), cb=cb)) for _, cb, w in rows]
    in_specs += [pl.BlockSpec(v.shape, lambda i, j, kk: (0, 0)) for v in vecs]
    if epilogue:
        res_specs = [pl.BlockSpec((tm, w), lambda i, j, kk: (i, 0)) for w, _ in outs]
        res_specs += [pl.BlockSpec((1, w), lambda i, j, kk: (0, 0)) for w in reds]
        res_shape = [jax.ShapeDtypeStruct((m, w), dt) for w, dt in outs] + [jax.ShapeDtypeStruct((1, w), F32) for w in reds]
    else:
        res_specs = [pl.BlockSpec((tm, tn), lambda i, j, kk: (i, j))]
        res_shape = [jax.ShapeDtypeStruct((m, n), out_dtype)]
    ordered = bool(exchange or reds)
    res = pl.pallas_call(
        body, name=name, grid=(ni, nj, nk),
        in_specs=in_specs + (exchange.in_specs if exchange else []),
        out_specs=res_specs + (exchange.out_specs if exchange else []),
        out_shape=res_shape + (exchange.out_shape if exchange else []),
        scratch_shapes=[pltpu.VMEM((tm, tn), F32)] + (exchange.scratch if exchange else []),
        compiler_params=_cparams(*(("arbitrary",) * 3 if ordered else ("parallel", "parallel", "arbitrary"))),
    )(a, b, *[r[0] for r in rows], *vecs, *(exchange.arrays if exchange else []))
    main = res[:n_res] if epilogue else res[0]
    return (main, res[n_res:]) if exchange else main


_NT = (((1,), (1,)), ((), ()))
_TN = (((0,), (0,)), ((), ()))


def _dot(a, b):
    return jnp.dot(a, b, preferred_element_type=F32)


def _dot_nt(a, b):
    return lax.dot_general(a, b, _NT, preferred_element_type=F32)


def _dot_tn(a, b):
    return lax.dot_general(a, b, _TN, preferred_element_type=F32)


def _running_sum(x, tri):
    return _dot(x.astype(MXU_DTYPE), tri)


def _neg_softplus(z):
    u = jnp.exp2(jnp.abs(z) * (-1.0 / math.log(2.0)))
    return -jnp.maximum(z, 0.0) - jnp.log(1.0 + u)


def _walk_blocks(step, st, n, descending, group=2):
    done = 0
    size = group
    while size >= 1:
        def trip(t, s, size=size, done=done):
            js = [done + size * t + g for g in range(size)]
            return step([n - 1 - j for j in js] if descending else js, s)

        trips = (n - done) // size
        st = lax.fori_loop(0, trips, trip, st)
        done = done + size * trips
        size //= 2
    return st


def _chains(js):
    return [(h, t) for t in range(len(js)) for h in range(2)]


def _rowsum(x):
    return jnp.sum(x, axis=1, keepdims=True)


def _attn_consts():
    row = lax.broadcasted_iota(jnp.int32, (BQ, BK), 0)
    col = lax.broadcasted_iota(jnp.int32, (BQ, BK), 1)
    trow = lax.broadcasted_iota(jnp.int32, (BK, BK), 0)
    tcol = lax.broadcasted_iota(jnp.int32, (BK, BK), 1)
    lane = lax.broadcasted_iota(jnp.int32, (BQ, LANES), 1)
    klane = lax.broadcasted_iota(jnp.int32, (BK, LANES), 1)
    return row, col, trow, tcol, lane, klane


assert BQ == BK


def _diag_step(jd, descending, diag_mask, group=4):
    below = list(range(group)) if descending else list(reversed(range(group)))
    js = [jnp.maximum(jd - o, 0) for o in below]
    masks = [diag_mask if o == 0 else jd - o >= 0 for o in below]
    return js, masks, jnp.maximum(jd - (group - 1), 0)


def _key_slice(j):
    return pl.ds(pl.multiple_of(j * BK, BK), BK)


def _sb_forward(proj, s_len, exchange):
    nq = s_len // BQ
    assert s_len // BK <= HD
    qc, kc, vc = O_SBQ // LANES, O_SBK // LANES, O_SBV // LANES
    n_ex = len(exchange.arrays)

    def body(q_ref, k_ref, v_ref, *rest):
        x_refs, (o_ref, tails_ref) = rest[:n_ex], rest[n_ex:n_ex + 2]
        out_refs, sems = rest[n_ex + 2:2 * n_ex + 2], rest[2 * n_ex + 2:]
        p = pl.program_id(0)
        i = pl.program_id(1)
        jd = i

        @pl.when((p == 0) & (i == 0))
        def _():
            exchange.start(x_refs, out_refs, sems)

        row, col, trow, tcol, lane, klane = _attn_consts()
        tri = (trow >= tcol).astype(MXU_DTYPE)
        q = q_ref[...] * 0.125
        qh = [jnp.where(lane < HD, q, 0.0).astype(MXU_DTYPE), jnp.where(lane >= HD, q, 0.0).astype(MXU_DTYPE)]

        km = [klane < HD, klane >= HD]

        def step(js, st, masks):
            carry, acc, tail = st
            chains = _chains(js)
            kj = [k_ref[_key_slice(j), :].astype(MXU_DTYPE) for j in js]
            vj = [v_ref[_key_slice(j), :].astype(MXU_DTYPE) for j in js]
            z = {(h, t): _dot_nt(qh[h], kj[t]) for h, t in chains}
            run = list(carry)
            suf, carry_in = {}, {}
            for h, t in chains:
                lom = _neg_softplus(z[h, t])
                if masks:
                    lom = jnp.where(masks[t], lom, 0.0)
                suf[h, t] = _running_sum(lom, tri)
                carry_in[h, t] = run[h]
                run[h] = run[h] + _rowsum(lom)
            for h, t in chains:
                a = jnp.exp(z[h, t] + suf[h, t] + carry_in[h, t])
                if masks:
                    a = jnp.where(masks[t], a, 0.0)
                acc = acc + _dot(a.astype(MXU_DTYPE), jnp.where(km[h], vj[t], 0))
                tail_lane = js[t] if not masks or masks[t].ndim else jnp.where(masks[t], js[t], -LANES)
                tail = jnp.where(lane == h * HD + tail_lane, carry_in[h, t], tail)
            return tuple(run), acc, tail

        zero = jnp.zeros((BQ, LANES), F32)
        diag_js, masks, left = _diag_step(jd, True, col < row, group=2)
        st = step(diag_js, ((jnp.zeros((BQ, 1), F32),) * 2, zero, zero), masks)
        st = _walk_blocks(lambda js, s: step(js, s, None), st, left, True, group=8)
        o_ref[...] = st[1]
        tails_ref[...] = st[2]

        @pl.when((p == HEADS // 2 - 1) & (i == nq - 1))
        def _():
            exchange.wait(x_refs, out_refs, sems)

    blk = pl.BlockSpec((BQ, LANES), lambda p, i: (i, p))
    out = jax.ShapeDtypeStruct((s_len, SB_W), F32)
    res = pl.pallas_call(
        body, name="sb_fwd", grid=(HEADS // 2, nq),
        in_specs=[pl.BlockSpec((BQ, LANES), lambda p, i: (i, qc + p)),
                  pl.BlockSpec((s_len, LANES), lambda p, i: (0, kc + p)),
                  pl.BlockSpec((s_len, LANES), lambda p, i: (0, vc + p))] + exchange.in_specs,
        out_specs=[blk, blk] + exchange.out_specs, out_shape=[out, out] + exchange.out_shape,
        scratch_shapes=exchange.scratch,
        compiler_params=_cparams("arbitrary", "arbitrary"),
    )(proj, proj, proj, *exchange.arrays)
    return res[:2], res[2:]


def _sb_backward(proj, tails, do, s_len, exchange):
    nq = s_len // BQ
    qc, kc, vc = O_SBQ // LANES, O_SBK // LANES, O_SBV // LANES
    n_ex = len(exchange.arrays)

    def body(q_ref, k_ref, v_ref, tails_ref, do_ref, *rest):
        x_refs, (dq_ref, dk_out, dv_out) = rest[:n_ex], rest[n_ex:n_ex + 3]
        out_refs, (dk_ref, dv_ref), sems = rest[n_ex + 3:2 * n_ex + 3], rest[2 * n_ex + 3:2 * n_ex + 5], rest[2 * n_ex + 5:]
        p = pl.program_id(0)
        i = pl.program_id(1)
        jd = i

        @pl.when((p == 0) & (i == 0))
        def _():
            exchange.start(x_refs, out_refs, sems)

        @pl.when(i == 0)
        def _():
            dk_ref[...] = jnp.zeros(dk_ref.shape, F32)
            dv_ref[...] = jnp.zeros(dv_ref.shape, F32)

        row, col, trow, tcol, lane, klane = _attn_consts()
        tri = (trow >= tcol).astype(MXU_DTYPE)
        tri_p = (trow <= tcol).astype(MXU_DTYPE)
        q = q_ref[...] * 0.125
        tails_blk = tails_ref[...]
        do_blk = do_ref[...]
        hm = [lane < HD, lane >= HD]
        km = [klane < HD, klane >= HD]
        qh = [jnp.where(m, q, 0.0).astype(MXU_DTYPE) for m in hm]
        doh = [jnp.where(m, do_blk, 0.0).astype(MXU_DTYPE) for m in hm]

        def step(js, st, masks):
            before, dq = st
            chains = _chains(js)
            kj = [k_ref[_key_slice(j), :].astype(MXU_DTYPE) for j in js]
            vj = [v_ref[_key_slice(j), :].astype(MXU_DTYPE) for j in js]
            z = {(h, t): _dot_nt(qh[h], kj[t]) for h, t in chains}
            da = {(h, t): _dot_nt(doh[h], vj[t]) for h, t in chains}
            suf, sig = {}, {}
            for h, t in chains:
                lom = _neg_softplus(z[h, t])
                if masks:
                    lom = jnp.where(masks[t], lom, 0.0)
                suf[h, t] = _running_sum(lom, tri)
                sig[h, t] = jnp.exp(z[h, t] + lom)
            run = list(before)
            dl, pre, before_in = {}, {}, {}
            dk_add, dv_add = [None] * len(js), [None] * len(js)
            for h, t in chains:
                tail = _rowsum(jnp.where(lane == h * HD + js[t], tails_blk, 0.0))
                a = jnp.exp(z[h, t] + suf[h, t] + tail)
                if masks:
                    a = jnp.where(masks[t], a, 0.0)
                dl[h, t] = da[h, t] * a
                pre[h, t] = _dot(dl[h, t].astype(MXU_DTYPE), tri_p)
                dv_h = _dot_tn(a.astype(MXU_DTYPE), doh[h])
                dv_add[t] = dv_h if dv_add[t] is None else dv_add[t] + dv_h
                before_in[h, t] = run[h]
                run[h] = run[h] + _rowsum(dl[h, t])
            for h, t in chains:
                upto = before_in[h, t] + pre[h, t]
                dz = dl[h, t] - sig[h, t] * upto
                if masks:
                    dz = jnp.where(masks[t], dz, 0.0)
                dzb = dz.astype(MXU_DTYPE)
                dq = dq + _dot(dzb, jnp.where(km[h], kj[t], 0))
                dk_h = _dot_tn(dzb, qh[h])
                dk_add[t] = dk_h if dk_add[t] is None else dk_add[t] + dk_h
            for t, j in enumerate(js):
                dk_ref[_key_slice(j), :] += dk_add[t]
                dv_ref[_key_slice(j), :] += dv_add[t]
            return tuple(run), dq

        st = ((jnp.zeros((BQ, 1), F32),) * 2, jnp.zeros((BQ, LANES), F32))
        diag_js, masks, left = _diag_step(jd, False, col < row, group=2)
        st = _walk_blocks(lambda js, s: step(js, s, None), st, left, False, group=8)
        st = step(diag_js, st, masks)
        dq_ref[...] = (st[1] * 0.125).astype(dq_ref.dtype)

        @pl.when(i == nq - 1)
        def _():
            dk_out[...] = dk_ref[...].astype(dk_out.dtype)
            dv_out[...] = dv_ref[...].astype(dv_out.dtype)

        @pl.when((p == HEADS // 2 - 1) & (i == nq - 1))
        def _():
            exchange.wait(x_refs, out_refs, sems)

    blk = pl.BlockSpec((BQ, LANES), lambda p, i: (i, p))
    full = pl.BlockSpec((s_len, LANES), lambda p, i: (0, p))
    out = jax.ShapeDtypeStruct((s_len, SB_W), BF16)
    res = pl.pallas_call(
        body, name="sb_bwd", grid=(HEADS // 2, nq),
        in_specs=[pl.BlockSpec((BQ, LANES), lambda p, i: (i, qc + p)),
                  pl.BlockSpec((s_len, LANES), lambda p, i: (0, kc + p)),
                  pl.BlockSpec((s_len, LANES), lambda p, i: (0, vc + p)),
                  blk, blk] + exchange.in_specs,
        out_specs=[blk, full, full] + exchange.out_specs, out_shape=[out, out, out] + exchange.out_shape,
        scratch_shapes=[pltpu.VMEM((s_len, LANES), F32)] * 2 + exchange.scratch,
        compiler_params=_cparams("arbitrary", "arbitrary"),
    )(proj, proj, proj, tails, do, *exchange.arrays)
    return res[:3], res[3:]


def _pair_mask(rows, h):
    lane = lax.broadcasted_iota(jnp.int32, (rows, 2 * LANES), 1)
    rot = lane - LANES
    return (((lane < LANES) & (lane // HD == h))
            | ((lane >= LANES) & (rot < 2 * ROPE) & ((rot // (ROPE // 2)) % 2 == h)))


def _mla_forward(q_cat, k_cat, kv, s_len):
    nq = s_len // BQ
    scale = 1.0 / math.sqrt(QK_DIM)

    def body(q_ref, k_ref, v_ref, o_ref, lse_ref):
        i = pl.program_id(1)
        jd = i
        row, col, trow, tcol, lane, klane = _attn_consts()
        q = q_ref[...]
        hm = [lane < HD, lane >= HD]
        km = [klane < HD, klane >= HD]
        qh = [jnp.where(_pair_mask(BQ, h), q, 0) for h in range(2)]

        def step(js, st, masks):
            m_run, l_run, acc = st
            chains = _chains(js)
            kj = [k_ref[_key_slice(j), :] for j in js]
            vj = [v_ref[_key_slice(j), :].astype(MXU_DTYPE) for j in js]
            s = {}
            for h, t in chains:
                s[h, t] = _dot_nt(qh[h], kj[t]) * scale
                if masks:
                    s[h, t] = jnp.where(masks[t], s[h, t], -jnp.inf)
            m_new, alpha, l_new = [], [], []
            for h in range(2):
                top = m_run[h]
                for t in range(len(js)):
                    top = jnp.maximum(top, jnp.max(s[h, t], axis=1, keepdims=True))
                m_new.append(top)
                alpha.append(jnp.exp(m_run[h] - top))
                l_new.append(alpha[h] * l_run[h])
            add = None
            for h, t in chains:
                pr = jnp.exp(s[h, t] - m_new[h])
                l_new[h] = l_new[h] + _rowsum(pr)
                part = _dot(pr.astype(MXU_DTYPE), jnp.where(km[h], vj[t], 0))
                add = part if add is None else add + part
            acc = jnp.where(hm[0], alpha[0], alpha[1]) * acc + add
            return tuple(m_new), tuple(l_new), acc

        st = ((jnp.full((BQ, 1), -1e30, F32),) * 2, (jnp.zeros((BQ, 1), F32),) * 2, jnp.zeros((BQ, LANES), F32))
        diag_js, masks, left = _diag_step(jd, True, col <= row)
        st = step(diag_js, st, masks)
        m_run, l_run, acc = _walk_blocks(lambda js, s: step(js, s, None), st, left, True, group=8)
        o_ref[...] = acc / jnp.where(hm[0], l_run[0], l_run[1])
        lse_ref[...] = jnp.where(hm[0], m_run[0] + jnp.log(l_run[0]), m_run[1] + jnp.log(l_run[1]))

    blk = pl.BlockSpec((BQ, LANES), lambda p, i: (i, p))
    out = jax.ShapeDtypeStruct((s_len, MLA_W), F32)
    return pl.pallas_call(
        body, name="mla_fwd", grid=(HEADS // 2, nq),
        in_specs=[pl.BlockSpec((BQ, 2 * LANES), lambda p, i: (i, p)),
                  pl.BlockSpec((s_len, 2 * LANES), lambda p, i: (0, p)),
                  pl.BlockSpec((s_len, LANES), lambda p, i: (0, MLA_W // LANES + p))],
        out_specs=[blk, blk], out_shape=[out, out],
        compiler_params=_cparams("parallel", "parallel"),
    )(q_cat, k_cat, kv)


def _mla_backward(q_cat, k_cat, kv, o, lse, do, s_len):
    nq = s_len // BQ
    scale = 1.0 / math.sqrt(QK_DIM)

    def body(q_ref, k_ref, v_ref, o_ref, lse_ref, do_ref, dq_ref, dk_ref, dv_out, dv_ref):
        i = pl.program_id(1)

        @pl.when(i == 0)
        def _():
            dk_ref[...] = jnp.zeros(dk_ref.shape, F32)
            dv_ref[...] = jnp.zeros(dv_ref.shape, F32)

        jd = i
        row, col, trow, tcol, lane, klane = _attn_consts()
        q = q_ref[...]
        o_blk = o_ref[...]
        do_blk = do_ref[...]
        lse_blk = lse_ref[...]
        hm = [lane < HD, lane >= HD]
        kpm = [_pair_mask(BK, h) for h in range(2)]
        qh = [jnp.where(_pair_mask(BQ, h), q, 0) for h in range(2)]
        doh_f = [jnp.where(m, do_blk, 0.0) for m in hm]
        doh = [d.astype(MXU_DTYPE) for d in doh_f]
        delta = [jnp.sum(d * o_blk, axis=1, keepdims=True) for d in doh_f]
        lse_h = [jnp.sum(jnp.where(lane == h * HD, lse_blk, 0.0), axis=1, keepdims=True) for h in range(2)]

        def step(js, st, masks):
            dq = st
            chains = _chains(js)
            kj = [k_ref[_key_slice(j), :] for j in js]
            vj = [v_ref[_key_slice(j), :].astype(MXU_DTYPE) for j in js]
            s = {(h, t): _dot_nt(qh[h], kj[t]) for h, t in chains}
            dp = {(h, t): _dot_nt(doh[h], vj[t]) for h, t in chains}
            adds = [[None] * len(js) for _ in range(2)]

            def accumulate(slot, t, part):
                adds[slot][t] = part if adds[slot][t] is None else adds[slot][t] + part

            for h, t in chains:
                pr = jnp.exp(s[h, t] * scale - lse_h[h])
                if masks:
                    pr = jnp.where(masks[t], pr, 0.0)
                dsb = (pr * (dp[h, t] - delta[h]) * scale).astype(MXU_DTYPE)
                dq = dq + _dot(dsb, jnp.where(kpm[h], kj[t], 0))
                accumulate(0, t, _dot_tn(dsb, qh[h]))
                accumulate(1, t, _dot_tn(pr.astype(MXU_DTYPE), doh[h]))
            for t, j in enumerate(js):
                dk_ref[_key_slice(j), :] += adds[0][t]
                dv_ref[_key_slice(j), :] += adds[1][t]
            return dq

        diag_js, masks, left = _diag_step(jd, True, col <= row)
        st = step(diag_js, jnp.zeros((BQ, 2 * LANES), F32), masks)
        dq_ref[...] = _walk_blocks(lambda js, s: step(js, s, None), st, left, True, group=8)

        @pl.when(i == nq - 1)
        def _():
            dv_out[...] = dv_ref[...].astype(dv_out.dtype)

    blk = pl.BlockSpec((BQ, LANES), lambda p, i: (i, p))
    full = pl.BlockSpec((s_len, LANES), lambda p, i: (0, p))
    out = jax.ShapeDtypeStruct((s_len, MLA_W), BF16)
    out_cat = jax.ShapeDtypeStruct((s_len, 2 * MLA_W), F32)
    return pl.pallas_call(
        body, name="mla_bwd", grid=(HEADS // 2, nq),
        in_specs=[pl.BlockSpec((BQ, 2 * LANES), lambda p, i: (i, p)),
                  pl.BlockSpec((s_len, 2 * LANES), lambda p, i: (0, p)),
                  pl.BlockSpec((s_len, LANES), lambda p, i: (0, MLA_W // LANES + p)),
                  blk, blk, blk],
        out_specs=[pl.BlockSpec((BQ, 2 * LANES), lambda p, i: (i, p)),
                   pl.BlockSpec((s_len, 2 * LANES), lambda p, i: (0, p)), full],
        out_shape=[out_cat, out_cat, out],
        scratch_shapes=[pltpu.VMEM((s_len, LANES), F32)],
        compiler_params=_cparams("arbitrary", "arbitrary"),
    )(q_cat, k_cat, kv, o, lse, do)


def _mesh_pos():
    return lax.axis_index("x"), lax.axis_index("y"), lax.axis_index("c")


def _dev_index(px, py, pc):
    return 4 * px + 2 * py + pc


def _all_gather(block, name):
    return _all_gather_parts([block], name)[0]


def _all_gather_parts(blocks, name):
    n = len(blocks)

    def body(*refs):
        x_refs, out_refs = refs[:n], refs[n:2 * n]
        send_sems, recv_sems, local_sems = refs[2 * n:]
        x, y, c = _mesh_pos()
        me, sibling = (x, y, c), (x, y, 1 - c)
        chips = [(1 - x, y), (x, 1 - y), (1 - x, 1 - y)]

        def copy(a, k, blockpos, to, src=None):
            slot = out_refs[a].at[_dev_index(*blockpos)]
            return pltpu.make_async_remote_copy(
                src_ref=slot if src is None else src, dst_ref=slot,
                send_sem=send_sems.at[7 * a + k], recv_sem=recv_sems.at[7 * a + k],
                device_id=to, device_id_type=pl.DeviceIdType.MESH)

        mine = [pltpu.make_async_copy(x_refs[a], out_refs[a].at[_dev_index(*me)], local_sems.at[a]) for a in range(n)]
        for cp in mine:
            cp.start()
        first = []
        for a in range(n):
            first.append(copy(a, 0, me, sibling, src=x_refs[a]))
            first += [copy(a, 1 + j, me, (*chip, c), src=x_refs[a]) for j, chip in enumerate(chips)]
        for cp in first:
            cp.start()
        passed = []
        for j, chip in enumerate(chips):
            for a in range(n):
                copy(a, 1 + j, (*chip, c), me).wait_recv()
                passed.append(copy(a, 4 + j, (*chip, c), sibling))
                passed[-1].start()
        for a in range(n):
            copy(a, 0, sibling, me).wait_recv()
            for j, chip in enumerate(chips):
                copy(a, 4 + j, (*chip, 1 - c), me).wait_recv()
        for cp in first + passed:
            cp.wait_send()
        for cp in mine:
            cp.wait()

    return pl.pallas_call(
        body, name=name,
        out_shape=[jax.ShapeDtypeStruct((N_DEV,) + b.shape, b.dtype) for b in blocks],
        in_specs=[pl.BlockSpec(memory_space=pl.ANY)] * n, out_specs=[pl.BlockSpec(memory_space=pl.ANY)] * n,
        scratch_shapes=[pltpu.SemaphoreType.DMA((7 * n,)), pltpu.SemaphoreType.DMA((7 * n,)),
                        pltpu.SemaphoreType.DMA((n,))],
    )(*blocks)


class _Exchange:
    def __init__(self, arrays):
        self.arrays = list(arrays)
        n = len(self.arrays)
        self.in_specs = [pl.BlockSpec(memory_space=pl.ANY)] * n
        self.out_specs = [pl.BlockSpec(memory_space=pl.ANY)] * n
        self.out_shape = [jax.ShapeDtypeStruct(a.shape, a.dtype) for a in self.arrays]
        self.scratch = [pltpu.SemaphoreType.DMA((7 * n,)), pltpu.SemaphoreType.DMA((7 * n,)),
                        pltpu.SemaphoreType.DMA((n,))]

    def _copies(self, x_refs, out_refs, sems, with_arrivals):
        send_sems, recv_sems, local_sems = sems
        x, y, c = _mesh_pos()
        me = _dev_index(x, y, c)
        flips = [(fx, fy, fc) for fx in (0, 1) for fy in (0, 1) for fc in (0, 1)][1:]
        peers = [(1 - x if fx else x, 1 - y if fy else y, 1 - c if fc else c) for fx, fy, fc in flips]
        mine, sends, arrivals = [], [], []
        for a in range(len(self.arrays)):
            mine.append(pltpu.make_async_copy(x_refs[a].at[me], out_refs[a].at[me], local_sems.at[a]))
            for k, peer in enumerate(peers):
                sends.append(pltpu.make_async_remote_copy(
                    src_ref=x_refs[a].at[_dev_index(*peer)], dst_ref=out_refs[a].at[me],
                    send_sem=send_sems.at[7 * a + k], recv_sem=recv_sems.at[7 * a + k],
                    device_id=peer, device_id_type=pl.DeviceIdType.MESH))
                if not with_arrivals:
                    continue
                arrivals.append(pltpu.make_async_remote_copy(
                    src_ref=x_refs[a].at[me], dst_ref=out_refs[a].at[_dev_index(*peer)],
                    send_sem=send_sems.at[7 * a + k], recv_sem=recv_sems.at[7 * a + k],
                    device_id=peer, device_id_type=pl.DeviceIdType.MESH))
        return mine, sends, arrivals

    def start(self, x_refs, out_refs, sems):
        mine, sends, _ = self._copies(x_refs, out_refs, sems, False)
        for cp in mine + sends:
            cp.start()

    def wait(self, x_refs, out_refs, sems):
        mine, sends, arrivals = self._copies(x_refs, out_refs, sems, True)
        for cp in arrivals:
            cp.wait_recv()
        for cp in sends:
            cp.wait_send()
        for cp in mine:
            cp.wait()


def _sum_blocks(parts, name):
    n, r, c = parts.shape
    row_tiles = [t for t in range(16, min(r, 2048) + 1, 16) if r % t == 0]
    if row_tiles:
        tr, tc = max(row_tiles), c
    else:
        tr, tc = r, 2 * LANES
    assert c % tc == 0

    def body(p_ref, o_ref):
        acc = p_ref[0].astype(F32)
        for s in range(1, n):
            acc = acc + p_ref[s].astype(F32)
        o_ref[...] = acc

    return pl.pallas_call(
        body, name=name, grid=(r // tr, c // tc),
        in_specs=[pl.BlockSpec((n, tr, tc), lambda i, j: (0, i, j))],
        out_specs=pl.BlockSpec((tr, tc), lambda i, j: (i, j)),
        out_shape=jax.ShapeDtypeStruct((r, c), F32),
        compiler_params=_cparams("parallel", "parallel"),
    )(parts)


def _sigmoid(x):
    return 1.0 / (1.0 + jnp.exp(-x))


def _silu(x):
    return x * _sigmoid(x)


def _silu_grad(x):
    s = _sigmoid(x)
    return s * (1.0 + x * (1.0 - s))


def _colsum(x):
    return jnp.sum(x, axis=0, keepdims=True)


def _rms(x):
    return lax.rsqrt(jnp.mean(x * x, axis=-1, keepdims=True) + EPS)


def _rms_bwd(xn, r, dxn):
    return r * (dxn - xn * jnp.mean(dxn * xn, axis=-1, keepdims=True))


def _adamw(w, g, m, v):
    m = ADAM_B1 * m + (1.0 - ADAM_B1) * g
    v = ADAM_B2 * v + (1.0 - ADAM_B2) * jnp.square(g)
    m_hat = m / (1.0 - ADAM_B1 ** ADAM_STEP)
    v_hat = v / (1.0 - ADAM_B2 ** ADAM_STEP)
    delta = -ADAM_LR * (m_hat / (jnp.sqrt(v_hat) + ADAM_EPS) + ADAM_WD * w)
    return delta, m, v


def _adamw_call(w, g, m, v, name):
    r, c = w.shape
    if r % 256 == 0:
        tr, tc = 256, c
    elif r * c <= 256 * 1024 or c % (2 * LANES):
        tr, tc = r, c
    else:
        tr, tc = r, 2 * LANES

    def body(w_ref, g_ref, m_ref, v_ref, d_out, m_out, v_out):
        d_out[...], m_out[...], v_out[...] = _adamw(w_ref[...], g_ref[...], m_ref[...], v_ref[...])

    spec = pl.BlockSpec((tr, tc), lambda i, j: (i, j))
    return pl.pallas_call(
        body, name=name, grid=(r // tr, c // tc), in_specs=[spec] * 4, out_specs=[spec] * 3,
        out_shape=[jax.ShapeDtypeStruct((r, c), F32)] * 3, compiler_params=_cparams("parallel", "parallel"),
    )(w, g, m, v)


def _uq_to_kernel_layout(w):
    lead = w.shape[:-1]
    t = w.reshape(lead + (HEADS, QK_DIM))
    return jnp.concatenate([t[..., :NOPE].reshape(lead + (HEADS * NOPE,)),
                            t[..., NOPE:NOPE + ROPE // 2].reshape(lead + (LANES,)),
                            t[..., NOPE + ROPE // 2:].reshape(lead + (LANES,))], axis=-1)


def _uq_from_kernel_layout(w):
    lead = w.shape[:-1]
    nope = w[..., :HEADS * NOPE].reshape(lead + (HEADS, NOPE))
    r1 = w[..., HEADS * NOPE:HEADS * NOPE + LANES].reshape(lead + (HEADS, ROPE // 2))
    r2 = w[..., HEADS * NOPE + LANES:].reshape(lead + (HEADS, ROPE // 2))
    return jnp.concatenate([nope, r1, r2], axis=-1).reshape(lead + (HEADS * QK_DIM,))


def _ukv_to_kernel_layout(w):
    lead = w.shape[:-1]
    t = w.reshape(lead + (HEADS, NOPE + HD))
    return jnp.concatenate([t[..., :NOPE].reshape(lead + (HEADS * NOPE,)),
                            t[..., NOPE:].reshape(lead + (HEADS * HD,))], axis=-1)


def _ukv_from_kernel_layout(w):
    lead = w.shape[:-1]
    kn = w[..., :HEADS * NOPE].reshape(lead + (HEADS, NOPE))
    vv = w[..., HEADS * NOPE:].reshape(lead + (HEADS, HD))
    return jnp.concatenate([kn, vv], axis=-1).reshape(lead + (HEADS * (NOPE + HD),))


def _w_in_t_to_kernel_layout(wt):
    sb = wt[0:2048]
    c_q = wt[2048:2432]
    c_kv = wt[2432:2688]
    k_rot = wt[2688:2720]
    mla_z = wt[2720:3232]
    gates = wt[3232:5280]
    zeros = jnp.zeros((LANES, wt.shape[1]), wt.dtype)
    k1 = jnp.tile(k_rot[:ROPE // 2], (HEADS, 1))
    k2 = jnp.tile(k_rot[ROPE // 2:], (HEADS, 1))
    return jnp.concatenate([gates, sb, mla_z, c_q, zeros, c_kv, k1, k2], axis=0)


def _w_in_t_from_kernel_layout(gt, g_rot):
    return jnp.concatenate([gt[O_SBQ:O_SBQ + 2048], gt[O_CQ:O_CQ + Q_RANK], gt[O_CKV:O_CKV + KV_RANK],
                            g_rot.astype(gt.dtype), gt[O_MLAZ:O_MLAZ + MLA_W], gt[O_GA:O_GA + 2 * D]], axis=0)


def kernel(x, c, positions, w_ada, b_ada, norm_gain, w_in, q_norm_gain, w_uq, kv_norm_gain, w_ukv, w_branch_a, w_branch_b, w_out, final_norm_gain, loss_target, m_w_ada, m_b_ada, m_norm_gain, m_w_in, m_q_norm_gain, m_w_uq, m_kv_norm_gain, m_w_ukv, m_w_branch_a, m_w_branch_b, m_w_out, m_final_norm_gain, v_w_ada, v_b_ada, v_norm_gain, v_w_in, v_q_norm_gain, v_w_uq, v_kv_norm_gain, v_w_ukv, v_w_branch_a, v_w_branch_b, v_w_out, v_final_norm_gain):
    s_len = x.shape[1]
    me = _dev_index(*_mesh_pos())
    x2d = x[0]
    tgt = loss_target[0]

    w_in_t = w_in[0].T.astype(BF16)
    big = [w_uq[0], w_ukv[0], w_branch_a[0], w_branch_b[0], w_out[0]]
    big_sizes = [int(w.size) for w in big]
    packed = jnp.concatenate([w.astype(BF16).reshape(-1, LANES) for w in big], axis=0)
    g_in_t, c_all = _all_gather_parts([w_in_t, c.reshape(8, LANES)], "gather_w_in")
    c_all = c_all.reshape(N_DEV, D)
    w_in_kt = _w_in_t_to_kernel_layout(g_in_t.reshape(N_DEV * w_in_t.shape[0], D))

    mod_cols = _mm(c_all, w_ada[0], name="ada_mod")
    mod_all = _all_gather(mod_cols, "gather_mod")
    mod = lax.dynamic_index_in_dim(mod_all, me, axis=1, keepdims=False).reshape(1, 3 * D)
    mod_shift, mod_scale, mod_gate = mod[:, :D], mod[:, D:2 * D], mod[:, 2 * D:]
    b_shift, b_scale, b_gate = b_ada[:, :D], b_ada[:, D:2 * D], b_ada[:, 2 * D:]
    g1 = norm_gain
    gq, gkv = q_norm_gain, kv_norm_gain
    gf = final_norm_gain.reshape(1, D)

    def f_h(x_, g1_, ms, bs, msc, bsc):
        xn = x_ * _rms(x_)
        return (xn * g1_ * (1.0 + (msc + bsc)) + (ms + bs),), ()

    (h,) = _rowwise(f_h, [x2d], [g1, mod_shift, b_shift, mod_scale, b_scale], [(D, BF16)], name="ada_norm")
    proj = _mm(h, w_in_kt, tb=True, name="proj_in", tiles=(min(s_len, 1024), IN_PAD // 2, D))

    (o_a, sb_tails), (gathered,) = _sb_forward(
        proj, s_len, _Exchange([jnp.broadcast_to(packed[None], (N_DEV,) + packed.shape)]))
    offs = [0]
    for n in big_sizes:
        offs.append(offs[-1] + n // LANES)

    def unpack(t, shape):
        return gathered[:, offs[t]:offs[t + 1], :].reshape((N_DEV,) + shape)

    def cols(t, shape):
        return unpack(t, shape).transpose(1, 0, 2).reshape(shape[0], N_DEV * shape[1])

    w_uq_k = _uq_to_kernel_layout(cols(0, big[0].shape))
    w_ukv_k = _ukv_to_kernel_layout(cols(1, big[1].shape))
    w_a_f = cols(2, big[2].shape)
    w_b_f = cols(3, big[3].shape)
    w_out_f = unpack(4, big[4].shape).reshape(D, D)

    def f_lat(cq, ckv, gq_, gkv_):
        return (cq * _rms(cq) * gq_, ckv * _rms(ckv) * gkv_), ()

    cq_n, ckv_n = _rowwise(f_lat, [(proj, O_CQ // Q_RANK, Q_RANK), (proj, O_CKV // KV_RANK, KV_RANK)], [gq, gkv],
                           [(Q_RANK, BF16), (KV_RANK, BF16)], name="latent_norm")
    q_mla = _mm(cq_n, w_uq_k, name="q_up")
    kv = _mm(ckv_n, w_ukv_k, out_dtype=BF16, name="kv_up")

    inv_freq = ROPE_BASE ** (-jnp.arange(0, ROPE, 2, dtype=F32) / ROPE)
    inv_freq_t = jnp.tile(inv_freq, HEADS).reshape(1, LANES)
    pos_col = positions.reshape(s_len, 1).astype(F32)

    pairs = HEADS // 2

    def f_rope(pos, qn, q1, q2, kn, k1, k2, freq):
        ang = pos * freq
        cs, sn = jnp.cos(ang), jnp.sin(ang)
        q1r, q2r = q1 * cs - q2 * sn, q1 * sn + q2 * cs
        k1r, k2r = k1 * cs - k2 * sn, k1 * sn + k2 * cs
        lane = lax.broadcasted_iota(jnp.int32, q1.shape, 1)
        first, second = lane < ROPE, (lane >= ROPE) & (lane < 2 * ROPE)
        k_rot = jnp.where(first, k1r, jnp.where(second, k2r, 0.0))
        q_parts, k_parts = [], []
        for p in range(pairs):
            q_rot = jnp.where(first, pltpu.roll(q1r, (LANES - ROPE * p) % LANES, 1),
                              jnp.where(second, pltpu.roll(q2r, (LANES + ROPE - ROPE * p) % LANES, 1), 0.0))
            q_parts += [qn[:, LANES * p:LANES * (p + 1)], q_rot]
            k_parts += [kn[:, LANES * p:LANES * (p + 1)], k_rot]
        return (jnp.concatenate(q_parts, axis=1), jnp.concatenate(k_parts, axis=1), cs, sn), ()

    q_cat, k_cat, cos_t, sin_t = _rowwise(
        f_rope, [pos_col, (q_mla, 0, MLA_W), (q_mla, 4, LANES), (q_mla, 5, LANES), (kv, 0, MLA_W),
                 (proj, O_KROT // LANES, LANES), (proj, O_KROT // LANES + 1, LANES)], [inv_freq_t],
        [(2 * MLA_W, BF16), (2 * MLA_W, BF16), (LANES, F32), (LANES, F32)], name="rope")

    o_b, lse = _mla_forward(q_cat, k_cat, kv, s_len)

    def f_gate(oa, za, ob, zb):
        return (oa * _silu(za), ob * _silu(zb)), ()

    ya_in, yb_in = _rowwise(f_gate, [o_a, (proj, O_SBZ // SB_W, SB_W), o_b, (proj, O_MLAZ // MLA_W, MLA_W)], [],
                            [(SB_W, BF16), (MLA_W, BF16)], name="branch_gate")
    y_a = _mm(ya_in, w_a_f, out_dtype=BF16, name="branch_a")

    def f_merge(yb, ga, gb, ya):
        return (yb, _sigmoid(ga) * ya + _sigmoid(gb) * yb), ()

    y_b, merged = _mm(yb_in, w_b_f, name="branch_b_merge", tiles=(min(s_len, 512), D, MLA_W),
                      epilogue=(f_merge, [(proj, O_GA // D, D), (proj, O_GB // D, D), y_a], [], [(D, BF16), (D, BF16)], []))

    def f_loss(out_, x_, t_, mg, bg, gf_):
        gate = mg + bg
        x2 = x_ + gate * out_
        r2 = _rms(x2)
        xn2 = x2 * r2
        err = xn2 * gf_ - t_
        loss = jnp.full((1, LANES), 0.5 / D, F32) * jnp.sum(err * err)
        dy = err * (1.0 / D)
        dx2 = _rms_bwd(xn2, r2, dy * gf_)
        return (dx2, dx2 * gate), (loss, _colsum(dy * xn2), _colsum(dx2 * out_))

    dx2, d_out, loss_part, d_gf, d_gate = _mm(
        merged, w_out_f, name="out_proj_loss", tiles=(min(s_len, 512), D, D),
        epilogue=(f_loss, [x2d, tgt], [mod_gate, b_gate, gf], [(D, F32), (D, BF16)], [LANES, D, D]))

    dw_out = _mm(merged, d_out, ta=True, name="dw_out")

    def f_dmerge(dm, ga, gb, ya, yb):
        sa, sb = _sigmoid(ga), _sigmoid(gb)
        return (dm * sa, dm * sb, dm * ya * sa * (1.0 - sa), dm * yb * sb * (1.0 - sb)), ()

    d_ya, d_yb, d_ga, d_gb = _mm(
        d_out, w_out_f, tb=True, name="d_merge", tiles=(min(s_len, 256), D, D),
        epilogue=(f_dmerge, [(proj, O_GA // D, D), (proj, O_GB // D, D), y_a, y_b], [], [(D, BF16)] * 4, []))
    dw_a = _mm(ya_in, d_ya, ta=True, name="dw_branch_a")
    dw_b = _mm(yb_in, d_yb, ta=True, name="dw_branch_b")

    def f_dgate(d_in, o_, z_):
        return (d_in * _silu(z_), d_in * o_ * _silu_grad(z_)), ()

    d_oa, d_sbz = _mm(d_ya, w_a_f, tb=True, name="d_branch_a",
                      epilogue=(f_dgate, [o_a, (proj, O_SBZ // SB_W, SB_W)], [], [(SB_W, BF16), (SB_W, BF16)], []))
    d_ob, d_mlaz = _mm(d_yb, w_b_f, tb=True, name="d_branch_b",
                       epilogue=(f_dgate, [o_b, (proj, O_MLAZ // MLA_W, MLA_W)], [], [(MLA_W, F32), (MLA_W, BF16)], []))

    dq_cat, dk_cat, dv_b = _mla_backward(q_cat, k_cat, kv, o_b, lse, d_ob, s_len)

    def f_drope(dq, dk, dv_, cs, sn):
        lane = lax.broadcasted_iota(jnp.int32, cs.shape, 1)
        first, second = lane < ROPE, (lane >= ROPE) & (lane < 2 * ROPE)
        dq1 = dq2 = dk1 = dk2 = None
        for p in range(pairs):
            q_rot = dq[:, LANES * (2 * p + 1):LANES * (2 * p + 2)]
            k_rot = dk[:, LANES * (2 * p + 1):LANES * (2 * p + 2)]
            parts = (pltpu.roll(jnp.where(first, q_rot, 0.0), (ROPE * p) % LANES, 1),
                     pltpu.roll(jnp.where(second, q_rot, 0.0), (LANES - ROPE + ROPE * p) % LANES, 1),
                     jnp.where(first, k_rot, 0.0), jnp.where(second, k_rot, 0.0))
            if p == 0:
                dq1, dq2, dk1, dk2 = parts
            else:
                dq1, dq2, dk1, dk2 = dq1 + parts[0], dq2 + parts[1], dk1 + parts[2], dk2 + parts[3]
        dqn_ = [dq[:, 2 * LANES * p:2 * LANES * p + LANES] for p in range(pairs)]
        dkn_ = [dk[:, 2 * LANES * p:2 * LANES * p + LANES] for p in range(pairs)]
        return (jnp.concatenate(dqn_ + [dq1 * cs + dq2 * sn, dq2 * cs - dq1 * sn], axis=1),
                jnp.concatenate(dkn_ + [dv_], axis=1),
                jnp.concatenate([dk1 * cs + dk2 * sn, dk2 * cs - dk1 * sn], axis=1)), ()

    dq_k, dkv_k, d_krot = _rowwise(f_drope, [dq_cat, dk_cat, dv_b, cos_t, sin_t], [],
                                   [(HEADS * QK_DIM, BF16), (2 * MLA_W, BF16), (2 * LANES, BF16)], name="d_rope")
    dw_uq_k = _mm(cq_n, dq_k, ta=True, name="dw_uq")
    dw_ukv_k = _mm(ckv_n, dkv_k, ta=True, name="dw_ukv")

    def f_dlat(d_normed, latent, gain):
        r = _rms(latent)
        normed = latent * r
        return (_rms_bwd(normed, r, d_normed * gain),), (_colsum(d_normed * normed),)

    d_cq, d_gq = _mm(dq_k, w_uq_k, tb=True, name="d_cq_norm",
                     epilogue=(f_dlat, [(proj, O_CQ // Q_RANK, Q_RANK)], [gq], [(Q_RANK, BF16)], [Q_RANK]))
    d_ckv, d_gkv = _mm(dkv_k, w_ukv_k, tb=True, name="d_ckv_norm",
                       epilogue=(f_dlat, [(proj, O_CKV // KV_RANK, KV_RANK)], [gkv], [(KV_RANK, BF16)], [KV_RANK]))

    def col_blocks(g):
        kdim, n8 = g.shape
        return g.astype(BF16).reshape(kdim, N_DEV, n8 // N_DEV).transpose(1, 0, 2).reshape(N_DEV, -1, LANES)

    g_blocks = jnp.concatenate([col_blocks(_uq_from_kernel_layout(dw_uq_k)), col_blocks(_ukv_from_kernel_layout(dw_ukv_k)),
                                col_blocks(dw_a), col_blocks(dw_b), dw_out.astype(BF16).reshape(N_DEV, -1, LANES)], axis=1)
    (d_sbq, d_sbk, d_sbv), (g_recv,) = _sb_backward(proj, sb_tails, d_oa, s_len, _Exchange([g_blocks]))

    d_proj = jnp.concatenate([d_ga, d_gb, d_sbq, d_sbk, d_sbv, d_sbz, d_mlaz,
                              d_cq, jnp.zeros((s_len, LANES), BF16), d_ckv, d_krot], axis=1)
    dw_in_kt = _mm(d_proj, h, ta=True, out_dtype=BF16, name="dw_in", tiles=(512, D, s_len))

    def krot_body(t_ref, o_ref):
        half = ROPE // 2
        for part in range(2):
            acc = t_ref[part * LANES:part * LANES + half, :].astype(F32)
            for hh in range(1, HEADS):
                acc = acc + t_ref[part * LANES + hh * half:part * LANES + (hh + 1) * half, :].astype(F32)
            o_ref[part * half:(part + 1) * half, :] = acc

    dw_krot = pl.pallas_call(krot_body, name="dw_krot_sum", out_shape=jax.ShapeDtypeStruct((ROPE, D), F32))(
        dw_in_kt[O_KROT:O_KROT + 2 * LANES])

    g_in_blocks = _w_in_t_from_kernel_layout(dw_in_kt, dw_krot).reshape(N_DEV, -1, D)
    def f_dx(dh_, x_, dx2_, g1_, msc, bsc):
        r = _rms(x_)
        xn = x_ * r
        dn1 = dh_ * (1.0 + (msc + bsc))
        return ((dx2_ + _rms_bwd(xn, r, dn1 * g1_),),
                (_colsum(dh_), _colsum(dh_ * (xn * g1_)), _colsum(dn1 * xn)))

    (grad_x2d, d_shift, d_scale, d_g1), (g_in_recv,) = _mm(
        d_proj, w_in_kt, name="d_h", tiles=(min(s_len, 512), D, 512), exchange=_Exchange([g_in_blocks]),
        epilogue=(f_dx, [x2d, dx2], [g1, mod_scale, b_scale], [(D, F32)], [D, D, D]))

    g_in_sum_t = _sum_blocks(g_in_recv, "sum_grads_w_in")
    g_sum = _sum_blocks(g_recv, "sum_grads")
    g_big = [g_sum[offs[t]:offs[t + 1]].reshape(big[t].shape) for t in range(5)]

    small = jnp.concatenate([d_shift, d_scale, d_gate, d_g1, d_gq, d_gkv, d_gf, loss_part], axis=1)
    n_small = small.shape[1]
    pad = (-n_small) % (8 * LANES)
    small = jnp.pad(small, ((0, 0), (0, pad))).reshape(-1, LANES)
    small_all = _all_gather(small, "gather_small")
    small_sum = _sum_blocks(small_all, "sum_small").reshape(1, -1)
    g_b_ada = small_sum[:, :3 * D]
    g_g1 = small_sum[:, 3 * D:4 * D]
    g_gq = small_sum[:, 4 * D:4 * D + Q_RANK]
    g_gkv = small_sum[:, 4 * D + Q_RANK:4 * D + Q_RANK + KV_RANK]
    g_gf = small_sum[:, 4 * D + Q_RANK + KV_RANK:4 * D + Q_RANK + KV_RANK + D]

    dmod_all = small_all.reshape(N_DEV, -1)[:, :3 * D]
    dmod_cols = lax.dynamic_slice_in_dim(dmod_all, me * (3 * D // N_DEV), 3 * D // N_DEV, axis=1)
    g_w_ada = _mm(c_all, dmod_cols, ta=True, name="dw_ada")

    loss = small_sum[0, n_small - LANES]

    names = ["w_ada", "b_ada", "norm_gain", "w_in", "q_norm_gain", "w_uq", "kv_norm_gain", "w_ukv",
             "w_branch_a", "w_branch_b", "w_out", "final_norm_gain"]
    weights = dict(w_ada=w_ada, b_ada=b_ada, norm_gain=norm_gain, w_in=w_in, q_norm_gain=q_norm_gain, w_uq=w_uq,
                   kv_norm_gain=kv_norm_gain, w_ukv=w_ukv, w_branch_a=w_branch_a, w_branch_b=w_branch_b, w_out=w_out,
                   final_norm_gain=final_norm_gain)
    moms = dict(w_ada=m_w_ada, b_ada=m_b_ada, norm_gain=m_norm_gain, w_in=m_w_in, q_norm_gain=m_q_norm_gain,
                w_uq=m_w_uq, kv_norm_gain=m_kv_norm_gain, w_ukv=m_w_ukv, w_branch_a=m_w_branch_a,
                w_branch_b=m_w_branch_b, w_out=m_w_out, final_norm_gain=m_final_norm_gain)
    vels = dict(w_ada=v_w_ada, b_ada=v_b_ada, norm_gain=v_norm_gain, w_in=v_w_in, q_norm_gain=v_q_norm_gain,
                w_uq=v_w_uq, kv_norm_gain=v_kv_norm_gain, w_ukv=v_w_ukv, w_branch_a=v_w_branch_a,
                w_branch_b=v_w_branch_b, w_out=v_w_out, final_norm_gain=v_final_norm_gain)
    grads2d = dict(w_ada=g_w_ada, b_ada=g_b_ada, norm_gain=g_g1, w_in=g_in_sum_t, q_norm_gain=g_gq, w_uq=g_big[0],
                   kv_norm_gain=g_gkv, w_ukv=g_big[1], w_branch_a=g_big[2], w_branch_b=g_big[3], w_out=g_big[4],
                   final_norm_gain=g_gf)

    grads, deltas, new_m, new_v = [], [], [], []
    for n in names:
        w = weights[n]
        if n == "w_in":
            to2d = lambda t: t[0].T if t.ndim == 3 else t
            back = lambda t: t.T[None]
        else:
            shape2d = grads2d[n].shape
            to2d = lambda t, s=shape2d: t.reshape(s)
            back = lambda t, s=w.shape: t.reshape(s)
        g2d = to2d(grads2d[n])
        d_, m_, v_ = _adamw_call(to2d(w), g2d, to2d(moms[n]), to2d(vels[n]), "adamw_" + n)
        grads.append(back(g2d))
        deltas.append(back(d_))
        new_m.append(back(m_))
        new_v.append(back(v_))

    return (loss, grad_x2d.reshape(x.shape), *grads, *deltas, *new_m, *new_v)
```

```python
import functools
import math

import jax
import jax.numpy as jnp
from jax import lax
from jax.experimental import pallas as pl
from jax.experimental.pallas import tpu as pltpu

F32 = jnp.float32
BF16 = jnp.bfloat16
MXU_DTYPE = jnp.bfloat16

N_DEV = 8
D = 1024
HEADS = 8
HD = 64
SB_W = 512
MLA_W = 512
Q_RANK = 384
KV_RANK = 256
ROPE = 32
NOPE = 64
QK_DIM = NOPE + ROPE
EPS = 1e-6
ROPE_BASE = 10000.0

ADAM_LR = 0.001
ADAM_B1 = 0.9
ADAM_B2 = 0.999
ADAM_EPS = 1e-08
ADAM_WD = 0.01
ADAM_STEP = 10

LANES = 128
VMEM_LIMIT = 48 * 1024 * 1024

O_GA, O_GB = 0, 1024
O_SBQ, O_SBK, O_SBV, O_SBZ = 2048, 2560, 3072, 3584
O_MLAZ = 4096
O_CQ = 4608
O_CKV = 5120
O_KROT = 5376
IN_PAD = 5632

BQ = 256
BK = 256


def _cparams(*sem):
    return pltpu.CompilerParams(dimension_semantics=sem, vmem_limit_bytes=VMEM_LIMIT)


def _tile_of(n, cap=512):
    if n <= cap:
        return n
    for t in (1024, 768, 512, 384, 256, 128):
        if t <= cap and n % t == 0:
            return t
    raise ValueError(n)


def _rowwise(fn, rows, vecs, outs, reds=(), *, name, tile=512):
    norm = []
    for r in rows:
        if isinstance(r, tuple):
            arr, cb, w = r[:3]
            ro = r[3] if len(r) > 3 else 0
        else:
            arr, cb, w, ro = r, 0, r.shape[1], 0
        norm.append((arr, cb, w, ro))
    s_len = norm[0][0].shape[0]
    tile = min(tile, s_len)
    assert s_len % tile == 0
    n_row, n_vec, n_out, n_red = len(norm), len(vecs), len(outs), len(reds)

    def body(*refs):
        step = pl.program_id(0)
        row_refs = refs[:n_row]
        vec_refs = refs[n_row:n_row + n_vec]
        out_refs = refs[n_row + n_vec:n_row + n_vec + n_out]
        red_refs = refs[n_row + n_vec + n_out:]
        row_res, red_res = fn(*[r[...] for r in row_refs], *[v[...] for v in vec_refs])
        for o, val in zip(out_refs, row_res):
            o[...] = val.astype(o.dtype)
        if n_red:
            @pl.when(step == 0)
            def _():
                for r in red_refs:
                    r[...] = jnp.zeros(r.shape, r.dtype)
            for r, val in zip(red_refs, red_res):
                r[...] += val

    in_specs = []
    for arr, cb, w, ro in norm:
        in_specs.append(pl.BlockSpec((tile, w), functools.partial(lambda i, cb, rb: (i + rb, cb), cb=cb, rb=ro // tile)))
        assert ro % tile == 0
    for v in vecs:
        in_specs.append(pl.BlockSpec(v.shape, lambda i: (0, 0)))
    out_shape = [jax.ShapeDtypeStruct((s_len, w), dt) for w, dt in outs]
    out_specs = [pl.BlockSpec((tile, w), lambda i: (i, 0)) for w, _ in outs]
    out_shape += [jax.ShapeDtypeStruct((1, w), F32) for w in reds]
    out_specs += [pl.BlockSpec((1, w), lambda i: (0, 0)) for w in reds]
    res = pl.pallas_call(
        body, name=name, grid=(s_len // tile,), in_specs=in_specs, out_specs=out_specs, out_shape=out_shape,
        compiler_params=_cparams("arbitrary" if n_red else "parallel"),
    )(*[a for a, _, _, _ in norm], *vecs)
    return res


def _mm(a, b, *, ta=False, tb=False, out_dtype=F32, name, exchange=None, tiles=None, epilogue=None):
    m, k = (a.shape[1], a.shape[0]) if ta else a.shape
    n = b.shape[0] if tb else b.shape[1]
    assert (b.shape[1] if tb else b.shape[0]) == k
    tm, tn, tk = tiles or (_tile_of(m, 1024), _tile_of(n, 1024 if n <= 1024 else 512), _tile_of(k, 1024))
    assert m % tm == 0 and n % tn == 0 and k % tk == 0
    ni, nj, nk = m // tm, n // tn, k // tk
    dims = (((0 if ta else 1,), (1 if tb else 0,)), ((), ()))
    n_ex = len(exchange.arrays) if exchange else 0
    fn, rows, vecs, outs, reds = epilogue or (None, (), (), (), ())
    rows = [r if isinstance(r, tuple) else (r, 0, r.shape[1]) for r in rows]
    assert not epilogue or tn == n
    n_res = len(outs) + len(reds) if epilogue else 1

    def body(*refs):
        a_ref, b_ref = refs[:2]
        row_refs, refs = refs[2:2 + len(rows)], refs[2 + len(rows):]
        vec_refs, refs = refs[:len(vecs)], refs[len(vecs):]
        x_refs, refs = refs[:n_ex], refs[n_ex:]
        res_refs, refs = refs[:n_res], refs[n_res:]
        out_refs, acc_ref, sems = refs[:n_ex], refs[n_ex], refs[n_ex + 1:]
        i, j, kk = pl.program_id(0), pl.program_id(1), pl.program_id(2)
        first = (i == 0) & (j == 0) & (kk == 0)

        if exchange:
            @pl.when(first)
            def _():
                exchange.start(x_refs, out_refs, sems)

        if reds:
            @pl.when(first)
            def _():
                for r in res_refs[len(outs):]:
                    r[...] = jnp.zeros(r.shape, r.dtype)

        @pl.when(kk == 0)
        def _():
            acc_ref[...] = jnp.zeros(acc_ref.shape, F32)

        acc_ref[...] += lax.dot_general(a_ref[...].astype(MXU_DTYPE), b_ref[...].astype(MXU_DTYPE), dims,
                                        preferred_element_type=F32)

        @pl.when(kk == nk - 1)
        def _():
            if not epilogue:
                res_refs[0][...] = acc_ref[...].astype(res_refs[0].dtype)
                return
            row_res, red_res = fn(acc_ref[...], *[r[...] for r in row_refs], *[v[...] for v in vec_refs])
            for o, val in zip(res_refs, row_res):
                o[...] = val.astype(o.dtype)
            for r, val in zip(res_refs[len(outs):], red_res):
                r[...] += val

        if exchange:
            @pl.when((i == ni - 1) & (j == nj - 1) & (kk == nk - 1))
            def _():
                exchange.wait(x_refs, out_refs, sems)

    a_spec = pl.BlockSpec((tk, tm), lambda i, j, kk: (kk, i)) if ta else pl.BlockSpec((tm, tk), lambda i, j, kk: (i, kk))
    b_spec = pl.BlockSpec((tn, tk), lambda i, j, kk: (j, kk)) if tb else pl.BlockSpec((tk, tn), lambda i, j, kk: (kk, j))
    in_specs = [a_spec, b_spec]
    in_specs += [pl.BlockSpec((tm, w), functools.partial(lambda i, j, kk, cb: (i, cb), cb=cb)) for _, cb, w in rows]
    in_specs += [pl.BlockSpec(v.shape, lambda i, j, kk: (0, 0)) for v in vecs]
    if epilogue:
        res_specs = [pl.BlockSpec((tm, w), lambda i, j, kk: (i, 0)) for w, _ in outs]
        res_specs += [pl.BlockSpec((1, w), lambda i, j, kk: (0, 0)) for w in reds]
        res_shape = [jax.ShapeDtypeStruct((m, w), dt) for w, dt in outs] + [jax.ShapeDtypeStruct((1, w), F32) for w in reds]
    else:
        res_specs = [pl.BlockSpec((tm, tn), lambda i, j, kk: (i, j))]
        res_shape = [jax.ShapeDtypeStruct((m, n), out_dtype)]
    ordered = bool(exchange or reds)
    res = pl.pallas_call(
        body, name=name, grid=(ni, nj, nk),
        in_specs=in_specs + (exchange.in_specs if exchange else []),
        out_specs=res_specs + (exchange.out_specs if exchange else []),
        out_shape=res_shape + (exchange.out_shape if exchange else []),
        scratch_shapes=[pltpu.VMEM((tm, tn), F32)] + (exchange.scratch if exchange else []),
        compiler_params=_cparams(*(("arbitrary",) * 3 if ordered else ("parallel", "parallel", "arbitrary"))),
    )(a, b, *[r[0] for r in rows], *vecs, *(exchange.arrays if exchange else []))
    main = res[:n_res] if epilogue else res[0]
    return (main, res[n_res:]) if exchange else main


_NT = (((1,), (1,)), ((), ()))
_TN = (((0,), (0,)), ((), ()))


def _dot(a, b):
    return jnp.dot(a, b, preferred_element_type=F32)


def _dot_nt(a, b):
    return lax.dot_general(a, b, _NT, preferred_element_type=F32)


def _dot_tn(a, b):
    return lax.dot_general(a, b, _TN, preferred_element_type=F32)


def _running_sum(x, tri):
    return _dot(x.astype(MXU_DTYPE), tri)


def _neg_softplus(z):
    u = jnp.exp2(jnp.abs(z) * (-1.0 / math.log(2.0)))
    return -jnp.maximum(z, 0.0) - jnp.log(1.0 + u)


def _walk_blocks(step, st, n, descending, group=2):
    done = 0
    size = group
    while size >= 1:
        def trip(t, s, size=size, done=done):
            js = [done + size * t + g for g in range(size)]
            return step([n - 1 - j for j in js] if descending else js, s)

        trips = (n - done) // size
        st = lax.fori_loop(0, trips, trip, st)
        done = done + size * trips
        size //= 2
    return st


def _chains(js):
    return [(h, t) for t in range(len(js)) for h in range(2)]


def _rowsum(x):
    return jnp.sum(x, axis=1, keepdims=True)


def _attn_consts():
    row = lax.broadcasted_iota(jnp.int32, (BQ, BK), 0)
    col = lax.broadcasted_iota(jnp.int32, (BQ, BK), 1)
    trow = lax.broadcasted_iota(jnp.int32, (BK, BK), 0)
    tcol = lax.broadcasted_iota(jnp.int32, (BK, BK), 1)
    lane = lax.broadcasted_iota(jnp.int32, (BQ, LANES), 1)
    klane = lax.broadcasted_iota(jnp.int32, (BK, LANES), 1)
    return row, col, trow, tcol, lane, klane


assert BQ == BK


def _diag_step(jd, descending, diag_mask, group=4):
    below = list(range(group)) if descending else list(reversed(range(group)))
    js = [jnp.maximum(jd - o, 0) for o in below]
    masks = [diag_mask if o == 0 else jd - o >= 0 for o in below]
    return js, masks, jnp.maximum(jd - (group - 1), 0)


def _key_slice(j):
    return pl.ds(pl.multiple_of(j * BK, BK), BK)


def _sb_forward(proj, s_len, exchange):
    nq = s_len // BQ
    assert s_len // BK <= HD
    qc, kc, vc = 0, SB_W // LANES, 2 * SB_W // LANES
    n_ex = len(exchange.arrays)

    def body(q_ref, k_ref, v_ref, *rest):
        x_refs, (o_ref, tails_ref) = rest[:n_ex], rest[n_ex:n_ex + 2]
        out_refs, sems = rest[n_ex + 2:2 * n_ex + 2], rest[2 * n_ex + 2:]
        p = pl.program_id(0)
        i = pl.program_id(1)
        jd = i

        @pl.when((p == 0) & (i == 0))
        def _():
            exchange.start(x_refs, out_refs, sems)

        row, col, trow, tcol, lane, klane = _attn_consts()
        tri = (trow >= tcol).astype(MXU_DTYPE)
        q = q_ref[...] * 0.125
        qh = [jnp.where(lane < HD, q, 0.0).astype(MXU_DTYPE), jnp.where(lane >= HD, q, 0.0).astype(MXU_DTYPE)]

        km = [klane < HD, klane >= HD]

        def step(js, st, masks):
            carry, acc, tail = st
            chains = _chains(js)
            kj = [k_ref[_key_slice(j), :].astype(MXU_DTYPE) for j in js]
            vj = [v_ref[_key_slice(j), :].astype(MXU_DTYPE) for j in js]
            z = {(h, t): _dot_nt(qh[h], kj[t]) for h, t in chains}
            run = list(carry)
            suf, carry_in = {}, {}
            for h, t in chains:
                lom = _neg_softplus(z[h, t])
                if masks:
                    lom = jnp.where(masks[t], lom, 0.0)
                suf[h, t] = _running_sum(lom, tri)
                carry_in[h, t] = run[h]
                run[h] = run[h] + _rowsum(lom)
            for h, t in chains:
                a = jnp.exp(z[h, t] + suf[h, t] + carry_in[h, t])
                if masks:
                    a = jnp.where(masks[t], a, 0.0)
                acc = acc + _dot(a.astype(MXU_DTYPE), jnp.where(km[h], vj[t], 0))
                tail_lane = js[t] if not masks or masks[t].ndim else jnp.where(masks[t], js[t], -LANES)
                tail = jnp.where(lane == h * HD + tail_lane, carry_in[h, t], tail)
            return tuple(run), acc, tail

        zero = jnp.zeros((BQ, LANES), F32)
        diag_js, masks, left = _diag_step(jd, True, col < row, group=2)
        st = step(diag_js, ((jnp.zeros((BQ, 1), F32),) * 2, zero, zero), masks)
        st = _walk_blocks(lambda js, s: step(js, s, None), st, left, True, group=8)
        o_ref[...] = st[1]
        tails_ref[...] = st[2]

        @pl.when((p == HEADS // 2 - 1) & (i == nq - 1))
        def _():
            exchange.wait(x_refs, out_refs, sems)

    blk = pl.BlockSpec((BQ, LANES), lambda p, i: (i, p))
    out = jax.ShapeDtypeStruct((s_len, SB_W), F32)
    res = pl.pallas_call(
        body, name="sb_fwd", grid=(HEADS // 2, nq),
        in_specs=[pl.BlockSpec((BQ, LANES), lambda p, i: (i, qc + p)),
                  pl.BlockSpec((s_len, LANES), lambda p, i: (0, kc + p)),
                  pl.BlockSpec((s_len, LANES), lambda p, i: (0, vc + p))] + exchange.in_specs,
        out_specs=[blk, blk] + exchange.out_specs, out_shape=[out, out] + exchange.out_shape,
        scratch_shapes=exchange.scratch,
        compiler_params=_cparams("arbitrary", "arbitrary"),
    )(proj, proj, proj, *exchange.arrays)
    return res[:2], res[2:]


def _sb_backward(proj, tails, do, s_len, exchange):
    nq = s_len // BQ
    qc, kc, vc = 0, SB_W // LANES, 2 * SB_W // LANES
    n_ex = len(exchange.arrays)

    def body(q_ref, k_ref, v_ref, tails_ref, do_ref, *rest):
        x_refs, (dq_ref, dk_out, dv_out) = rest[:n_ex], rest[n_ex:n_ex + 3]
        out_refs, (dk_ref, dv_ref), sems = rest[n_ex + 3:2 * n_ex + 3], rest[2 * n_ex + 3:2 * n_ex + 5], rest[2 * n_ex + 5:]
        p = pl.program_id(0)
        i = pl.program_id(1)
        jd = i

        @pl.when((p == 0) & (i == 0))
        def _():
            exchange.start(x_refs, out_refs, sems)

        @pl.when(i == 0)
        def _():
            dk_ref[...] = jnp.zeros(dk_ref.shape, F32)
            dv_ref[...] = jnp.zeros(dv_ref.shape, F32)

        row, col, trow, tcol, lane, klane = _attn_consts()
        tri = (trow >= tcol).astype(MXU_DTYPE)
        tri_p = (trow <= tcol).astype(MXU_DTYPE)
        q = q_ref[...] * 0.125
        tails_blk = tails_ref[...]
        do_blk = do_ref[...]
        hm = [lane < HD, lane >= HD]
        km = [klane < HD, klane >= HD]
        qh = [jnp.where(m, q, 0.0).astype(MXU_DTYPE) for m in hm]
        doh = [jnp.where(m, do_blk, 0.0).astype(MXU_DTYPE) for m in hm]

        def step(js, st, masks):
            before, dq = st
            chains = _chains(js)
            kj = [k_ref[_key_slice(j), :].astype(MXU_DTYPE) for j in js]
            vj = [v_ref[_key_slice(j), :].astype(MXU_DTYPE) for j in js]
            z = {(h, t): _dot_nt(qh[h], kj[t]) for h, t in chains}
            da = {(h, t): _dot_nt(doh[h], vj[t]) for h, t in chains}
            suf, sig = {}, {}
            for h, t in chains:
                lom = _neg_softplus(z[h, t])
                if masks:
                    lom = jnp.where(masks[t], lom, 0.0)
                suf[h, t] = _running_sum(lom, tri)
                sig[h, t] = jnp.exp(z[h, t] + lom)
            run = list(before)
            dl, pre, before_in = {}, {}, {}
            dk_add, dv_add = [None] * len(js), [None] * len(js)
            for h, t in chains:
                tail = _rowsum(jnp.where(lane == h * HD + js[t], tails_blk, 0.0))
                a = jnp.exp(z[h, t] + suf[h, t] + tail)
                if masks:
                    a = jnp.where(masks[t], a, 0.0)
                dl[h, t] = da[h, t] * a
                pre[h, t] = _dot(dl[h, t].astype(MXU_DTYPE), tri_p)
                dv_h = _dot_tn(a.astype(MXU_DTYPE), doh[h])
                dv_add[t] = dv_h if dv_add[t] is None else dv_add[t] + dv_h
                before_in[h, t] = run[h]
                run[h] = run[h] + _rowsum(dl[h, t])
            for h, t in chains:
                upto = before_in[h, t] + pre[h, t]
                dz = dl[h, t] - sig[h, t] * upto
                if masks:
                    dz = jnp.where(masks[t], dz, 0.0)
                dzb = dz.astype(MXU_DTYPE)
                dq = dq + _dot(dzb, jnp.where(km[h], kj[t], 0))
                dk_h = _dot_tn(dzb, qh[h])
                dk_add[t] = dk_h if dk_add[t] is None else dk_add[t] + dk_h
            for t, j in enumerate(js):
                dk_ref[_key_slice(j), :] += dk_add[t]
                dv_ref[_key_slice(j), :] += dv_add[t]
            return tuple(run), dq

        st = ((jnp.zeros((BQ, 1), F32),) * 2, jnp.zeros((BQ, LANES), F32))
        diag_js, masks, left = _diag_step(jd, False, col < row, group=2)
        st = _walk_blocks(lambda js, s: step(js, s, None), st, left, False, group=8)
        st = step(diag_js, st, masks)
        dq_ref[...] = (st[1] * 0.125).astype(dq_ref.dtype)

        @pl.when(i == nq - 1)
        def _():
            dk_out[...] = dk_ref[...].astype(dk_out.dtype)
            dv_out[...] = dv_ref[...].astype(dv_out.dtype)

        @pl.when((p == HEADS // 2 - 1) & (i == nq - 1))
        def _():
            exchange.wait(x_refs, out_refs, sems)

    blk = pl.BlockSpec((BQ, LANES), lambda p, i: (i, p))
    full = pl.BlockSpec((s_len, LANES), lambda p, i: (0, p))
    out = jax.ShapeDtypeStruct((s_len, SB_W), BF16)
    res = pl.pallas_call(
        body, name="sb_bwd", grid=(HEADS // 2, nq),
        in_specs=[pl.BlockSpec((BQ, LANES), lambda p, i: (i, qc + p)),
                  pl.BlockSpec((s_len, LANES), lambda p, i: (0, kc + p)),
                  pl.BlockSpec((s_len, LANES), lambda p, i: (0, vc + p)),
                  blk, blk] + exchange.in_specs,
        out_specs=[blk, full, full] + exchange.out_specs, out_shape=[out, out, out] + exchange.out_shape,
        scratch_shapes=[pltpu.VMEM((s_len, LANES), F32)] * 2 + exchange.scratch,
        compiler_params=_cparams("arbitrary", "arbitrary"),
    )(proj, proj, proj, tails, do, *exchange.arrays)
    return res[:3], res[3:]


def _pair_mask(rows, h):
    lane = lax.broadcasted_iota(jnp.int32, (rows, 2 * LANES), 1)
    rot = lane - LANES
    return (((lane < LANES) & (lane // HD == h))
            | ((lane >= LANES) & (rot < 2 * ROPE) & ((rot // (ROPE // 2)) % 2 == h)))


def _mla_forward(q_cat, k_cat, kv, s_len):
    nq = s_len // BQ
    scale = 1.0 / math.sqrt(QK_DIM)

    def body(q_ref, k_ref, v_ref, o_ref, lse_ref):
        i = pl.program_id(1)
        jd = i
        row, col, trow, tcol, lane, klane = _attn_consts()
        q = q_ref[...]
        hm = [lane < HD, lane >= HD]
        km = [klane < HD, klane >= HD]
        qh = [jnp.where(_pair_mask(BQ, h), q, 0) for h in range(2)]

        def step(js, st, masks):
            m_run, l_run, acc = st
            chains = _chains(js)
            kj = [k_ref[_key_slice(j), :] for j in js]
            vj = [v_ref[_key_slice(j), :].astype(MXU_DTYPE) for j in js]
            s = {}
            for h, t in chains:
                s[h, t] = _dot_nt(qh[h], kj[t]) * scale
                if masks:
                    s[h, t] = jnp.where(masks[t], s[h, t], -jnp.inf)
            m_new, alpha, l_new = [], [], []
            for h in range(2):
                top = m_run[h]
                for t in range(len(js)):
                    top = jnp.maximum(top, jnp.max(s[h, t], axis=1, keepdims=True))
                m_new.append(top)
                alpha.append(jnp.exp(m_run[h] - top))
                l_new.append(alpha[h] * l_run[h])
            add = None
            for h, t in chains:
                pr = jnp.exp(s[h, t] - m_new[h])
                l_new[h] = l_new[h] + _rowsum(pr)
                part = _dot(pr.astype(MXU_DTYPE), jnp.where(km[h], vj[t], 0))
                add = part if add is None else add + part
            acc = jnp.where(hm[0], alpha[0], alpha[1]) * acc + add
            return tuple(m_new), tuple(l_new), acc

        st = ((jnp.full((BQ, 1), -1e30, F32),) * 2, (jnp.zeros((BQ, 1), F32),) * 2, jnp.zeros((BQ, LANES), F32))
        diag_js, masks, left = _diag_step(jd, True, col <= row)
        st = step(diag_js, st, masks)
        m_run, l_run, acc = _walk_blocks(lambda js, s: step(js, s, None), st, left, True, group=8)
        o_ref[...] = acc / jnp.where(hm[0], l_run[0], l_run[1])
        lse_ref[...] = jnp.where(hm[0], m_run[0] + jnp.log(l_run[0]), m_run[1] + jnp.log(l_run[1]))

    blk = pl.BlockSpec((BQ, LANES), lambda p, i: (i, p))
    out = jax.ShapeDtypeStruct((s_len, MLA_W), F32)
    return pl.pallas_call(
        body, name="mla_fwd", grid=(HEADS // 2, nq),
        in_specs=[pl.BlockSpec((BQ, 2 * LANES), lambda p, i: (i, p)),
                  pl.BlockSpec((s_len, 2 * LANES), lambda p, i: (0, p)),
                  pl.BlockSpec((s_len, LANES), lambda p, i: (0, MLA_W // LANES + p))],
        out_specs=[blk, blk], out_shape=[out, out],
        compiler_params=_cparams("parallel", "parallel"),
    )(q_cat, k_cat, kv)


def _mla_backward(q_cat, k_cat, kv, o, lse, do, s_len):
    nq = s_len // BQ
    scale = 1.0 / math.sqrt(QK_DIM)

    def body(q_ref, k_ref, v_ref, o_ref, lse_ref, do_ref, dq_ref, dk_ref, dv_ref):
        i = pl.program_id(1)

        @pl.when(i == 0)
        def _():
            dk_ref[...] = jnp.zeros(dk_ref.shape, F32)
            dv_ref[...] = jnp.zeros(dv_ref.shape, F32)

        jd = i
        row, col, trow, tcol, lane, klane = _attn_consts()
        q = q_ref[...]
        o_blk = o_ref[...]
        do_blk = do_ref[...]
        lse_blk = lse_ref[...]
        hm = [lane < HD, lane >= HD]
        kpm = [_pair_mask(BK, h) for h in range(2)]
        qh = [jnp.where(_pair_mask(BQ, h), q, 0) for h in range(2)]
        doh_f = [jnp.where(m, do_blk, 0.0) for m in hm]
        doh = [d.astype(MXU_DTYPE) for d in doh_f]
        delta = [jnp.sum(d * o_blk, axis=1, keepdims=True) for d in doh_f]
        lse_h = [jnp.sum(jnp.where(lane == h * HD, lse_blk, 0.0), axis=1, keepdims=True) for h in range(2)]

        def step(js, st, masks):
            dq = st
            chains = _chains(js)
            kj = [k_ref[_key_slice(j), :] for j in js]
            vj = [v_ref[_key_slice(j), :].astype(MXU_DTYPE) for j in js]
            s = {(h, t): _dot_nt(qh[h], kj[t]) for h, t in chains}
            dp = {(h, t): _dot_nt(doh[h], vj[t]) for h, t in chains}
            adds = [[None] * len(js) for _ in range(2)]

            def accumulate(slot, t, part):
                adds[slot][t] = part if adds[slot][t] is None else adds[slot][t] + part

            for h, t in chains:
                pr = jnp.exp(s[h, t] * scale - lse_h[h])
                if masks:
                    pr = jnp.where(masks[t], pr, 0.0)
                dsb = (pr * (dp[h, t] - delta[h]) * scale).astype(MXU_DTYPE)
                dq = dq + _dot(dsb, jnp.where(kpm[h], kj[t], 0))
                accumulate(0, t, _dot_tn(dsb, qh[h]))
                accumulate(1, t, _dot_tn(pr.astype(MXU_DTYPE), doh[h]))
            for t, j in enumerate(js):
                dk_ref[_key_slice(j), :] += adds[0][t]
                dv_ref[_key_slice(j), :] += adds[1][t]
            return dq

        diag_js, masks, left = _diag_step(jd, True, col <= row)
        st = step(diag_js, jnp.zeros((BQ, 2 * LANES), F32), masks)
        dq_ref[...] = _walk_blocks(lambda js, s: step(js, s, None), st, left, True, group=8)

    blk = pl.BlockSpec((BQ, LANES), lambda p, i: (i, p))
    full = pl.BlockSpec((s_len, LANES), lambda p, i: (0, p))
    out = jax.ShapeDtypeStruct((s_len, MLA_W), F32)
    out_cat = jax.ShapeDtypeStruct((s_len, 2 * MLA_W), F32)
    return pl.pallas_call(
        body, name="mla_bwd", grid=(HEADS // 2, nq),
        in_specs=[pl.BlockSpec((BQ, 2 * LANES), lambda p, i: (i, p)),
                  pl.BlockSpec((s_len, 2 * LANES), lambda p, i: (0, p)),
                  pl.BlockSpec((s_len, LANES), lambda p, i: (0, MLA_W // LANES + p)),
                  blk, blk, blk],
        out_specs=[pl.BlockSpec((BQ, 2 * LANES), lambda p, i: (i, p)),
                   pl.BlockSpec((s_len, 2 * LANES), lambda p, i: (0, p)), full],
        out_shape=[out_cat, out_cat, out],
        compiler_params=_cparams("arbitrary", "arbitrary"),
    )(q_cat, k_cat, kv, o, lse, do)


def _mesh_pos():
    return lax.axis_index("x"), lax.axis_index("y"), lax.axis_index("c")


def _dev_index(px, py, pc):
    return 4 * px + 2 * py + pc


def _all_gather(block, name):
    return _all_gather_parts([block], name)[0]


def _all_gather_parts(blocks, name):
    n = len(blocks)

    def body(*refs):
        x_refs, out_refs = refs[:n], refs[n:2 * n]
        send_sems, recv_sems, local_sems = refs[2 * n:]
        x, y, c = _mesh_pos()
        me, sibling = (x, y, c), (x, y, 1 - c)
        chips = [(1 - x, y), (x, 1 - y), (1 - x, 1 - y)]

        def copy(a, k, blockpos, to, src=None):
            slot = out_refs[a].at[_dev_index(*blockpos)]
            return pltpu.make_async_remote_copy(
                src_ref=slot if src is None else src, dst_ref=slot,
                send_sem=send_sems.at[7 * a + k], recv_sem=recv_sems.at[7 * a + k],
                device_id=to, device_id_type=pl.DeviceIdType.MESH)

        mine = [pltpu.make_async_copy(x_refs[a], out_refs[a].at[_dev_index(*me)], local_sems.at[a]) for a in range(n)]
        for cp in mine:
            cp.start()
        first = []
        for a in range(n):
            first.append(copy(a, 0, me, sibling, src=x_refs[a]))
            first += [copy(a, 1 + j, me, (*chip, c), src=x_refs[a]) for j, chip in enumerate(chips)]
        for cp in first:
            cp.start()
        passed = []
        for j, chip in enumerate(chips):
            for a in range(n):
                copy(a, 1 + j, (*chip, c), me).wait_recv()
                passed.append(copy(a, 4 + j, (*chip, c), sibling))
                passed[-1].start()
        for a in range(n):
            copy(a, 0, sibling, me).wait_recv()
            for j, chip in enumerate(chips):
                copy(a, 4 + j, (*chip, 1 - c), me).wait_recv()
        for cp in first + passed:
            cp.wait_send()
        for cp in mine:
            cp.wait()

    return pl.pallas_call(
        body, name=name,
        out_shape=[jax.ShapeDtypeStruct((N_DEV,) + b.shape, b.dtype) for b in blocks],
        in_specs=[pl.BlockSpec(memory_space=pl.ANY)] * n, out_specs=[pl.BlockSpec(memory_space=pl.ANY)] * n,
        scratch_shapes=[pltpu.SemaphoreType.DMA((7 * n,)), pltpu.SemaphoreType.DMA((7 * n,)),
                        pltpu.SemaphoreType.DMA((n,))],
    )(*blocks)


class _Exchange:
    def __init__(self, arrays):
        self.arrays = list(arrays)
        n = len(self.arrays)
        self.in_specs = [pl.BlockSpec(memory_space=pl.ANY)] * n
        self.out_specs = [pl.BlockSpec(memory_space=pl.ANY)] * n
        self.out_shape = [jax.ShapeDtypeStruct(a.shape, a.dtype) for a in self.arrays]
        self.scratch = [pltpu.SemaphoreType.DMA((7 * n,)), pltpu.SemaphoreType.DMA((7 * n,)),
                        pltpu.SemaphoreType.DMA((n,))]

    def _copies(self, x_refs, out_refs, sems, with_arrivals):
        send_sems, recv_sems, local_sems = sems
        x, y, c = _mesh_pos()
        me = _dev_index(x, y, c)
        flips = [(fx, fy, fc) for fx in (0, 1) for fy in (0, 1) for fc in (0, 1)][1:]
        peers = [(1 - x if fx else x, 1 - y if fy else y, 1 - c if fc else c) for fx, fy, fc in flips]
        mine, sends, arrivals = [], [], []
        for a in range(len(self.arrays)):
            mine.append(pltpu.make_async_copy(x_refs[a].at[me], out_refs[a].at[me], local_sems.at[a]))
            for k, peer in enumerate(peers):
                sends.append(pltpu.make_async_remote_copy(
                    src_ref=x_refs[a].at[_dev_index(*peer)], dst_ref=out_refs[a].at[me],
                    send_sem=send_sems.at[7 * a + k], recv_sem=recv_sems.at[7 * a + k],
                    device_id=peer, device_id_type=pl.DeviceIdType.MESH))
                if not with_arrivals:
                    continue
                arrivals.append(pltpu.make_async_remote_copy(
                    src_ref=x_refs[a].at[me], dst_ref=out_refs[a].at[_dev_index(*peer)],
                    send_sem=send_sems.at[7 * a + k], recv_sem=recv_sems.at[7 * a + k],
                    device_id=peer, device_id_type=pl.DeviceIdType.MESH))
        return mine, sends, arrivals

    def start(self, x_refs, out_refs, sems):
        mine, sends, _ = self._copies(x_refs, out_refs, sems, False)
        for cp in mine + sends:
            cp.start()

    def wait(self, x_refs, out_refs, sems):
        mine, sends, arrivals = self._copies(x_refs, out_refs, sems, True)
        for cp in arrivals:
            cp.wait_recv()
        for cp in sends:
            cp.wait_send()
        for cp in mine:
            cp.wait()


def _sum_blocks(parts, name):
    n, r, c = parts.shape
    row_tiles = [t for t in range(16, min(r, 2048) + 1, 16) if r % t == 0]
    if row_tiles:
        tr, tc = max(row_tiles), c
    else:
        tr, tc = r, 2 * LANES
    assert c % tc == 0

    def body(p_ref, o_ref):
        acc = p_ref[0].astype(F32)
        for s in range(1, n):
            acc = acc + p_ref[s].astype(F32)
        o_ref[...] = acc

    return pl.pallas_call(
        body, name=name, grid=(r // tr, c // tc),
        in_specs=[pl.BlockSpec((n, tr, tc), lambda i, j: (0, i, j))],
        out_specs=pl.BlockSpec((tr, tc), lambda i, j: (i, j)),
        out_shape=jax.ShapeDtypeStruct((r, c), F32),
        compiler_params=_cparams("parallel", "parallel"),
    )(parts)


def _sigmoid(x):
    return 1.0 / (1.0 + jnp.exp(-x))


def _silu(x):
    return x * _sigmoid(x)


def _silu_grad(x):
    s = _sigmoid(x)
    return s * (1.0 + x * (1.0 - s))


def _colsum(x):
    return jnp.sum(x, axis=0, keepdims=True)


def _rms(x):
    return lax.rsqrt(jnp.mean(x * x, axis=-1, keepdims=True) + EPS)


def _rms_bwd(xn, r, dxn):
    return r * (dxn - xn * jnp.mean(dxn * xn, axis=-1, keepdims=True))


def _adamw(w, g, m, v):
    m = ADAM_B1 * m + (1.0 - ADAM_B1) * g
    v = ADAM_B2 * v + (1.0 - ADAM_B2) * jnp.square(g)
    m_hat = m / (1.0 - ADAM_B1 ** ADAM_STEP)
    v_hat = v / (1.0 - ADAM_B2 ** ADAM_STEP)
    delta = -ADAM_LR * (m_hat / (jnp.sqrt(v_hat) + ADAM_EPS) + ADAM_WD * w)
    return delta, m, v


def _adamw_call(w, g, m, v, name):
    r, c = w.shape
    if r % 256 == 0:
        tr, tc = 256, c
    elif r * c <= 256 * 1024 or c % (2 * LANES):
        tr, tc = r, c
    else:
        tr, tc = r, 2 * LANES

    def body(w_ref, g_ref, m_ref, v_ref, d_out, m_out, v_out):
        d_out[...], m_out[...], v_out[...] = _adamw(w_ref[...], g_ref[...], m_ref[...], v_ref[...])

    spec = pl.BlockSpec((tr, tc), lambda i, j: (i, j))
    return pl.pallas_call(
        body, name=name, grid=(r // tr, c // tc), in_specs=[spec] * 4, out_specs=[spec] * 3,
        out_shape=[jax.ShapeDtypeStruct((r, c), F32)] * 3, compiler_params=_cparams("parallel", "parallel"),
    )(w, g, m, v)


def _uq_to_kernel_layout(w):
    lead = w.shape[:-1]
    t = w.reshape(lead + (HEADS, QK_DIM))
    return jnp.concatenate([t[..., :NOPE].reshape(lead + (HEADS * NOPE,)),
                            t[..., NOPE:NOPE + ROPE // 2].reshape(lead + (LANES,)),
                            t[..., NOPE + ROPE // 2:].reshape(lead + (LANES,))], axis=-1)


def _uq_from_kernel_layout(w):
    lead = w.shape[:-1]
    nope = w[..., :HEADS * NOPE].reshape(lead + (HEADS, NOPE))
    r1 = w[..., HEADS * NOPE:HEADS * NOPE + LANES].reshape(lead + (HEADS, ROPE // 2))
    r2 = w[..., HEADS * NOPE + LANES:].reshape(lead + (HEADS, ROPE // 2))
    return jnp.concatenate([nope, r1, r2], axis=-1).reshape(lead + (HEADS * QK_DIM,))


def _ukv_to_kernel_layout(w):
    lead = w.shape[:-1]
    t = w.reshape(lead + (HEADS, NOPE + HD))
    return jnp.concatenate([t[..., :NOPE].reshape(lead + (HEADS * NOPE,)),
                            t[..., NOPE:].reshape(lead + (HEADS * HD,))], axis=-1)


def _ukv_from_kernel_layout(w):
    lead = w.shape[:-1]
    kn = w[..., :HEADS * NOPE].reshape(lead + (HEADS, NOPE))
    vv = w[..., HEADS * NOPE:].reshape(lead + (HEADS, HD))
    return jnp.concatenate([kn, vv], axis=-1).reshape(lead + (HEADS * (NOPE + HD),))


def _w_in_t_to_kernel_layout(wt):
    sb = wt[0:2048]
    c_q = wt[2048:2432]
    c_kv = wt[2432:2688]
    k_rot = wt[2688:2720]
    mla_z = wt[2720:3232]
    gates = wt[3232:5280]
    zeros = jnp.zeros((LANES, wt.shape[1]), wt.dtype)
    k1 = jnp.tile(k_rot[:ROPE // 2], (HEADS, 1))
    k2 = jnp.tile(k_rot[ROPE // 2:], (HEADS, 1))
    return jnp.concatenate([gates, sb, mla_z, c_q, zeros, c_kv, k1, k2], axis=0)


def _w_in_t_from_kernel_layout(gt, g_rot):
    return jnp.concatenate([gt[O_SBQ:O_SBQ + 2048], gt[O_CQ:O_CQ + Q_RANK], gt[O_CKV:O_CKV + KV_RANK],
                            g_rot.astype(gt.dtype), gt[O_MLAZ:O_MLAZ + MLA_W], gt[O_GA:O_GA + 2 * D]], axis=0)


def kernel(x, c, positions, w_ada, b_ada, norm_gain, w_in, q_norm_gain, w_uq, kv_norm_gain, w_ukv, w_branch_a, w_branch_b, w_out, final_norm_gain, loss_target, m_w_ada, m_b_ada, m_norm_gain, m_w_in, m_q_norm_gain, m_w_uq, m_kv_norm_gain, m_w_ukv, m_w_branch_a, m_w_branch_b, m_w_out, m_final_norm_gain, v_w_ada, v_b_ada, v_norm_gain, v_w_in, v_q_norm_gain, v_w_uq, v_kv_norm_gain, v_w_ukv, v_w_branch_a, v_w_branch_b, v_w_out, v_final_norm_gain):
    s_len = x.shape[1]
    me = _dev_index(*_mesh_pos())
    x2d = x[0]
    tgt = loss_target[0]

    w_in_t = w_in[0].T.astype(BF16)
    big = [w_uq[0], w_ukv[0], w_branch_a[0], w_branch_b[0], w_out[0]]
    big_sizes = [int(w.size) for w in big]
    packed = jnp.concatenate([w.astype(BF16).reshape(-1, LANES) for w in big], axis=0)
    g_in_t, c_all = _all_gather_parts([w_in_t, c.reshape(8, LANES)], "gather_w_in")
    c_all = c_all.reshape(N_DEV, D)
    w_in_kt = _w_in_t_to_kernel_layout(g_in_t.reshape(N_DEV * w_in_t.shape[0], D))

    mod_cols = _mm(c_all, w_ada[0], name="ada_mod")
    mod_all = _all_gather(mod_cols, "gather_mod")
    mod = lax.dynamic_index_in_dim(mod_all, me, axis=1, keepdims=False).reshape(1, 3 * D)
    mod_shift, mod_scale, mod_gate = mod[:, :D], mod[:, D:2 * D], mod[:, 2 * D:]
    b_shift, b_scale, b_gate = b_ada[:, :D], b_ada[:, D:2 * D], b_ada[:, 2 * D:]
    g1 = norm_gain
    gq, gkv = q_norm_gain, kv_norm_gain
    gf = final_norm_gain.reshape(1, D)

    def f_h(x_, g1_, ms, bs, msc, bsc):
        xn = x_ * _rms(x_)
        return (xn * g1_ * (1.0 + (msc + bsc)) + (ms + bs),), ()

    (h,) = _rowwise(f_h, [x2d], [g1, mod_shift, b_shift, mod_scale, b_scale], [(D, BF16)], name="ada_norm")
    qkv_w = O_SBZ - O_SBQ
    sb_qkv = _mm(h, w_in_kt[O_SBQ:O_SBZ], tb=True, out_dtype=BF16, name="proj_sb",
                 tiles=(min(s_len, 1024), qkv_w, D))
    w_rest_t = jnp.concatenate([w_in_kt[:O_SBQ], w_in_kt[O_SBZ:]], axis=0)
    proj = _mm(h, w_rest_t, tb=True, name="proj_in", tiles=(min(s_len, 1024), w_rest_t.shape[0] // 2, D))
    proj_at = dict(ga=(O_GA, D), gb=(O_GB, D), sbz=(O_SBZ - qkv_w, SB_W), mlaz=(O_MLAZ - qkv_w, MLA_W),
                   cq=(O_CQ - qkv_w, Q_RANK), ckv=(O_CKV - qkv_w, KV_RANK), krot=(O_KROT - qkv_w, LANES))

    def win(key, shift=0):
        offset, width = proj_at[key]
        assert offset % width == 0
        return proj, offset // width + shift, width

    (o_a, sb_tails), (gathered,) = _sb_forward(
        sb_qkv, s_len, _Exchange([jnp.broadcast_to(packed[None], (N_DEV,) + packed.shape)]))
    offs = [0]
    for n in big_sizes:
        offs.append(offs[-1] + n // LANES)

    def unpack(t, shape):
        return gathered[:, offs[t]:offs[t + 1], :].reshape((N_DEV,) + shape)

    def cols(t, shape):
        return unpack(t, shape).transpose(1, 0, 2).reshape(shape[0], N_DEV * shape[1])

    w_uq_k = _uq_to_kernel_layout(cols(0, big[0].shape))
    w_ukv_k = _ukv_to_kernel_layout(cols(1, big[1].shape))
    w_a_f = cols(2, big[2].shape)
    w_b_f = cols(3, big[3].shape)
    w_out_f = unpack(4, big[4].shape).reshape(D, D)

    def f_lat(cq, ckv, gq_, gkv_):
        return (cq * _rms(cq) * gq_, ckv * _rms(ckv) * gkv_), ()

    cq_n, ckv_n = _rowwise(f_lat, [win("cq"), win("ckv")], [gq, gkv],
                           [(Q_RANK, BF16), (KV_RANK, BF16)], name="latent_norm")
    q_mla = _mm(cq_n, w_uq_k, name="q_up")
    kv = _mm(ckv_n, w_ukv_k, out_dtype=BF16, name="kv_up")

    inv_freq = ROPE_BASE ** (-jnp.arange(0, ROPE, 2, dtype=F32) / ROPE)
    inv_freq_t = jnp.tile(inv_freq, HEADS).reshape(1, LANES)
    pos_col = positions.reshape(s_len, 1).astype(F32)

    pairs = HEADS // 2

    def f_rope(pos, qn, q1, q2, kn, k1, k2, freq):
        ang = pos * freq
        cs, sn = jnp.cos(ang), jnp.sin(ang)
        q1r, q2r = q1 * cs - q2 * sn, q1 * sn + q2 * cs
        k1r, k2r = k1 * cs - k2 * sn, k1 * sn + k2 * cs
        lane = lax.broadcasted_iota(jnp.int32, q1.shape, 1)
        first, second = lane < ROPE, (lane >= ROPE) & (lane < 2 * ROPE)
        k_rot = jnp.where(first, k1r, jnp.where(second, k2r, 0.0))
        q_parts, k_parts = [], []
        for p in range(pairs):
            q_rot = jnp.where(first, pltpu.roll(q1r, (LANES - ROPE * p) % LANES, 1),
                              jnp.where(second, pltpu.roll(q2r, (LANES + ROPE - ROPE * p) % LANES, 1), 0.0))
            q_parts += [qn[:, LANES * p:LANES * (p + 1)], q_rot]
            k_parts += [kn[:, LANES * p:LANES * (p + 1)], k_rot]
        return (jnp.concatenate(q_parts, axis=1), jnp.concatenate(k_parts, axis=1), cs, sn), ()

    q_cat, k_cat, cos_t, sin_t = _rowwise(
        f_rope, [pos_col, (q_mla, 0, MLA_W), (q_mla, 4, LANES), (q_mla, 5, LANES), (kv, 0, MLA_W),
                 win("krot"), win("krot", 1)], [inv_freq_t],
        [(2 * MLA_W, BF16), (2 * MLA_W, BF16), (LANES, F32), (LANES, F32)], name="rope")

    o_b, lse = _mla_forward(q_cat, k_cat, kv, s_len)

    def f_gate(oa, za, ob, zb):
        return (oa * _silu(za), ob * _silu(zb)), ()

    ya_in, yb_in = _rowwise(f_gate, [o_a, win("sbz"), o_b, win("mlaz")], [],
                            [(SB_W, BF16), (MLA_W, BF16)], name="branch_gate")
    y_a = _mm(ya_in, w_a_f, out_dtype=BF16, name="branch_a")

    def f_merge(yb, ga, gb, ya):
        return (yb, _sigmoid(ga) * ya + _sigmoid(gb) * yb), ()

    y_b, merged = _mm(yb_in, w_b_f, name="branch_b_merge", tiles=(min(s_len, 512), D, MLA_W),
                      epilogue=(f_merge, [win("ga"), win("gb"), y_a], [], [(D, BF16), (D, BF16)], []))

    def f_loss(out_, x_, t_, mg, bg, gf_):
        gate = mg + bg
        x2 = x_ + gate * out_
        r2 = _rms(x2)
        xn2 = x2 * r2
        err = xn2 * gf_ - t_
        loss = jnp.full((1, LANES), 0.5 / D, F32) * jnp.sum(err * err)
        dy = err * (1.0 / D)
        dx2 = _rms_bwd(xn2, r2, dy * gf_)
        return (dx2, dx2 * gate), (loss, _colsum(dy * xn2), _colsum(dx2 * out_))

    dx2, d_out, loss_part, d_gf, d_gate = _mm(
        merged, w_out_f, name="out_proj_loss", tiles=(min(s_len, 512), D, D),
        epilogue=(f_loss, [x2d, tgt], [mod_gate, b_gate, gf], [(D, F32), (D, BF16)], [LANES, D, D]))

    dw_out = _mm(merged, d_out, ta=True, name="dw_out")

    def f_dmerge(dm, ga, gb, ya, yb):
        sa, sb = _sigmoid(ga), _sigmoid(gb)
        return (dm * sa, dm * sb, dm * ya * sa * (1.0 - sa), dm * yb * sb * (1.0 - sb)), ()

    d_ya, d_yb, d_ga, d_gb = _mm(
        d_out, w_out_f, tb=True, name="d_merge", tiles=(min(s_len, 256), D, D),
        epilogue=(f_dmerge, [win("ga"), win("gb"), y_a, y_b], [], [(D, BF16)] * 4, []))
    dw_a = _mm(ya_in, d_ya, ta=True, name="dw_branch_a")
    dw_b = _mm(yb_in, d_yb, ta=True, name="dw_branch_b")

    def f_dgate(d_in, o_, z_):
        return (d_in * _silu(z_), d_in * o_ * _silu_grad(z_)), ()

    d_oa, d_sbz = _mm(d_ya, w_a_f, tb=True, name="d_branch_a",
                      epilogue=(f_dgate, [o_a, win("sbz")], [], [(SB_W, BF16), (SB_W, BF16)], []))
    d_ob, d_mlaz = _mm(d_yb, w_b_f, tb=True, name="d_branch_b",
                       epilogue=(f_dgate, [o_b, win("mlaz")], [], [(MLA_W, F32), (MLA_W, BF16)], []))

    dq_cat, dk_cat, dv_b = _mla_backward(q_cat, k_cat, kv, o_b, lse, d_ob, s_len)

    def f_drope(dq, dk, dv_, cs, sn):
        lane = lax.broadcasted_iota(jnp.int32, cs.shape, 1)
        first, second = lane < ROPE, (lane >= ROPE) & (lane < 2 * ROPE)
        dq1 = dq2 = dk1 = dk2 = None
        for p in range(pairs):
            q_rot = dq[:, LANES * (2 * p + 1):LANES * (2 * p + 2)]
            k_rot = dk[:, LANES * (2 * p + 1):LANES * (2 * p + 2)]
            parts = (pltpu.roll(jnp.where(first, q_rot, 0.0), (ROPE * p) % LANES, 1),
                     pltpu.roll(jnp.where(second, q_rot, 0.0), (LANES - ROPE + ROPE * p) % LANES, 1),
                     jnp.where(first, k_rot, 0.0), jnp.where(second, k_rot, 0.0))
            if p == 0:
                dq1, dq2, dk1, dk2 = parts
            else:
                dq1, dq2, dk1, dk2 = dq1 + parts[0], dq2 + parts[1], dk1 + parts[2], dk2 + parts[3]
        dqn_ = [dq[:, 2 * LANES * p:2 * LANES * p + LANES] for p in range(pairs)]
        dkn_ = [dk[:, 2 * LANES * p:2 * LANES * p + LANES] for p in range(pairs)]
        return (jnp.concatenate(dqn_ + [dq1 * cs + dq2 * sn, dq2 * cs - dq1 * sn], axis=1),
                jnp.concatenate(dkn_ + [dv_], axis=1),
                jnp.concatenate([dk1 * cs + dk2 * sn, dk2 * cs - dk1 * sn], axis=1)), ()

    dq_k, dkv_k, d_krot = _rowwise(f_drope, [dq_cat, dk_cat, dv_b, cos_t, sin_t], [],
                                   [(HEADS * QK_DIM, BF16), (2 * MLA_W, BF16), (2 * LANES, BF16)], name="d_rope")
    dw_uq_k = _mm(cq_n, dq_k, ta=True, name="dw_uq")
    dw_ukv_k = _mm(ckv_n, dkv_k, ta=True, name="dw_ukv")

    def f_dlat(d_normed, latent, gain):
        r = _rms(latent)
        normed = latent * r
        return (_rms_bwd(normed, r, d_normed * gain),), (_colsum(d_normed * normed),)

    d_cq, d_gq = _mm(dq_k, w_uq_k, tb=True, name="d_cq_norm",
                     epilogue=(f_dlat, [win("cq")], [gq], [(Q_RANK, BF16)], [Q_RANK]))
    d_ckv, d_gkv = _mm(dkv_k, w_ukv_k, tb=True, name="d_ckv_norm",
                       epilogue=(f_dlat, [win("ckv")], [gkv], [(KV_RANK, BF16)], [KV_RANK]))

    def col_blocks(g):
        kdim, n8 = g.shape
        return g.astype(BF16).reshape(kdim, N_DEV, n8 // N_DEV).transpose(1, 0, 2).reshape(N_DEV, -1, LANES)

    g_blocks = jnp.concatenate([col_blocks(_uq_from_kernel_layout(dw_uq_k)), col_blocks(_ukv_from_kernel_layout(dw_ukv_k)),
                                col_blocks(dw_a), col_blocks(dw_b), dw_out.astype(BF16).reshape(N_DEV, -1, LANES)], axis=1)
    (d_sbq, d_sbk, d_sbv), (g_recv,) = _sb_backward(sb_qkv, sb_tails, d_oa, s_len, _Exchange([g_blocks]))

    d_proj = jnp.concatenate([d_ga, d_gb, d_sbq, d_sbk, d_sbv, d_sbz, d_mlaz,
                              d_cq, jnp.zeros((s_len, LANES), BF16), d_ckv, d_krot], axis=1)
    dw_in_kt = _mm(d_proj, h, ta=True, out_dtype=BF16, name="dw_in", tiles=(512, D, s_len))

    def krot_body(t_ref, o_ref):
        half = ROPE // 2
        for part in range(2):
            acc = t_ref[part * LANES:part * LANES + half, :].astype(F32)
            for hh in range(1, HEADS):
                acc = acc + t_ref[part * LANES + hh * half:part * LANES + (hh + 1) * half, :].astype(F32)
            o_ref[part * half:(part + 1) * half, :] = acc

    dw_krot = pl.pallas_call(krot_body, name="dw_krot_sum", out_shape=jax.ShapeDtypeStruct((ROPE, D), F32))(
        dw_in_kt[O_KROT:O_KROT + 2 * LANES])

    g_in_blocks = _w_in_t_from_kernel_layout(dw_in_kt, dw_krot).reshape(N_DEV, -1, D)
    def f_dx(dh_, x_, dx2_, g1_, msc, bsc):
        r = _rms(x_)
        xn = x_ * r
        dn1 = dh_ * (1.0 + (msc + bsc))
        return ((dx2_ + _rms_bwd(xn, r, dn1 * g1_),),
                (_colsum(dh_), _colsum(dh_ * (xn * g1_)), _colsum(dn1 * xn)))

    (grad_x2d, d_shift, d_scale, d_g1), (g_in_recv,) = _mm(
        d_proj, w_in_kt, name="d_h", tiles=(min(s_len, 512), D, 512), exchange=_Exchange([g_in_blocks]),
        epilogue=(f_dx, [x2d, dx2], [g1, mod_scale, b_scale], [(D, F32)], [D, D, D]))

    g_in_sum_t = _sum_blocks(g_in_recv, "sum_grads_w_in")
    g_sum = _sum_blocks(g_recv, "sum_grads")
    g_big = [g_sum[offs[t]:offs[t + 1]].reshape(big[t].shape) for t in range(5)]

    small = jnp.concatenate([d_shift, d_scale, d_gate, d_g1, d_gq, d_gkv, d_gf, loss_part], axis=1)
    n_small = small.shape[1]
    pad = (-n_small) % (8 * LANES)
    small = jnp.pad(small, ((0, 0), (0, pad))).reshape(-1, LANES)
    small_all = _all_gather(small, "gather_small")
    small_sum = _sum_blocks(small_all, "sum_small").reshape(1, -1)
    g_b_ada = small_sum[:, :3 * D]
    g_g1 = small_sum[:, 3 * D:4 * D]
    g_gq = small_sum[:, 4 * D:4 * D + Q_RANK]
    g_gkv = small_sum[:, 4 * D + Q_RANK:4 * D + Q_RANK + KV_RANK]
    g_gf = small_sum[:, 4 * D + Q_RANK + KV_RANK:4 * D + Q_RANK + KV_RANK + D]

    dmod_all = small_all.reshape(N_DEV, -1)[:, :3 * D]
    dmod_cols = lax.dynamic_slice_in_dim(dmod_all, me * (3 * D // N_DEV), 3 * D // N_DEV, axis=1)
    g_w_ada = _mm(c_all, dmod_cols, ta=True, name="dw_ada")

    loss = small_sum[0, n_small - LANES]

    names = ["w_ada", "b_ada", "norm_gain", "w_in", "q_norm_gain", "w_uq", "kv_norm_gain", "w_ukv",
             "w_branch_a", "w_branch_b", "w_out", "final_norm_gain"]
    weights = dict(w_ada=w_ada, b_ada=b_ada, norm_gain=norm_gain, w_in=w_in, q_norm_gain=q_norm_gain, w_uq=w_uq,
                   kv_norm_gain=kv_norm_gain, w_ukv=w_ukv, w_branch_a=w_branch_a, w_branch_b=w_branch_b, w_out=w_out,
                   final_norm_gain=final_norm_gain)
    moms = dict(w_ada=m_w_ada, b_ada=m_b_ada, norm_gain=m_norm_gain, w_in=m_w_in, q_norm_gain=m_q_norm_gain,
                w_uq=m_w_uq, kv_norm_gain=m_kv_norm_gain, w_ukv=m_w_ukv, w_branch_a=m_w_branch_a,
                w_branch_b=m_w_branch_b, w_out=m_w_out, final_norm_gain=m_final_norm_gain)
    vels = dict(w_ada=v_w_ada, b_ada=v_b_ada, norm_gain=v_norm_gain, w_in=v_w_in, q_norm_gain=v_q_norm_gain,
                w_uq=v_w_uq, kv_norm_gain=v_kv_norm_gain, w_ukv=v_w_ukv, w_branch_a=v_w_branch_a,
                w_branch_b=v_w_branch_b, w_out=v_w_out, final_norm_gain=v_final_norm_gain)
    grads2d = dict(w_ada=g_w_ada, b_ada=g_b_ada, norm_gain=g_g1, w_in=g_in_sum_t, q_norm_gain=g_gq, w_uq=g_big[0],
                   kv_norm_gain=g_gkv, w_ukv=g_big[1], w_branch_a=g_big[2], w_branch_b=g_big[3], w_out=g_big[4],
                   final_norm_gain=g_gf)

    grads, deltas, new_m, new_v = [], [], [], []
    for n in names:
        w = weights[n]
        if n == "w_in":
            to2d = lambda t: t[0].T if t.ndim == 3 else t
            back = lambda t: t.T[None]
        else:
            shape2d = grads2d[n].shape
            to2d = lambda t, s=shape2d: t.reshape(s)
            back = lambda t, s=w.shape: t.reshape(s)
        g2d = to2d(grads2d[n])
        d_, m_, v_ = _adamw_call(to2d(w), g2d, to2d(moms[n]), to2d(vels[n]), "adamw_" + n)
        grads.append(back(g2d))
        deltas.append(back(d_))
        new_m.append(back(m_))
        new_v.append(back(v_))

    return (loss, grad_x2d.reshape(x.shape), *grads, *deltas, *new_m, *new_v)
```

```python
import functools
import math

import jax
import jax.numpy as jnp
from jax import lax
from jax.experimental import pallas as pl
from jax.experimental.pallas import tpu as pltpu

F32 = jnp.float32
BF16 = jnp.bfloat16
MXU_DTYPE = jnp.bfloat16

N_DEV = 8
D = 1024
HEADS = 8
HD = 64
SB_W = 512
MLA_W = 512
Q_RANK = 384
KV_RANK = 256
ROPE = 32
NOPE = 64
QK_DIM = NOPE + ROPE
EPS = 1e-6
ROPE_BASE = 10000.0

ADAM_LR = 0.001
ADAM_B1 = 0.9
ADAM_B2 = 0.999
ADAM_EPS = 1e-08
ADAM_WD = 0.01
ADAM_STEP = 10

LANES = 128
VMEM_LIMIT = 48 * 1024 * 1024

O_GA, O_GB = 0, 1024
O_SBQ, O_SBK, O_SBV, O_SBZ = 2048, 2560, 3072, 3584
O_MLAZ = 4096
O_CQ = 4608
O_CKV = 5120
O_KROT = 5376
IN_PAD = 5632

BQ = 256
BK = 256


def _cparams(*sem):
    return pltpu.CompilerParams(dimension_semantics=sem, vmem_limit_bytes=VMEM_LIMIT)


def _tile_of(n, cap=512):
    if n <= cap:
        return n
    for t in (1024, 768, 512, 384, 256, 128):
        if t <= cap and n % t == 0:
            return t
    raise ValueError(n)


def _rowwise(fn, rows, vecs, outs, reds=(), *, name, tile=512):
    norm = []
    for r in rows:
        if isinstance(r, tuple):
            arr, cb, w = r[:3]
            ro = r[3] if len(r) > 3 else 0
        else:
            arr, cb, w, ro = r, 0, r.shape[1], 0
        norm.append((arr, cb, w, ro))
    s_len = norm[0][0].shape[0]
    tile = min(tile, s_len)
    assert s_len % tile == 0
    n_row, n_vec, n_out, n_red = len(norm), len(vecs), len(outs), len(reds)

    def body(*refs):
        step = pl.program_id(0)
        row_refs = refs[:n_row]
        vec_refs = refs[n_row:n_row + n_vec]
        out_refs = refs[n_row + n_vec:n_row + n_vec + n_out]
        red_refs = refs[n_row + n_vec + n_out:]
        row_res, red_res = fn(*[r[...] for r in row_refs], *[v[...] for v in vec_refs])
        for o, val in zip(out_refs, row_res):
            o[...] = val.astype(o.dtype)
        if n_red:
            @pl.when(step == 0)
            def _():
                for r in red_refs:
                    r[...] = jnp.zeros(r.shape, r.dtype)
            for r, val in zip(red_refs, red_res):
                r[...] += val

    in_specs = []
    for arr, cb, w, ro in norm:
        in_specs.append(pl.BlockSpec((tile, w), functools.partial(lambda i, cb, rb: (i + rb, cb), cb=cb, rb=ro // tile)))
        assert ro % tile == 0
    for v in vecs:
        in_specs.append(pl.BlockSpec(v.shape, lambda i: (0, 0)))
    out_shape = [jax.ShapeDtypeStruct((s_len, w), dt) for w, dt in outs]
    out_specs = [pl.BlockSpec((tile, w), lambda i: (i, 0)) for w, _ in outs]
    out_shape += [jax.ShapeDtypeStruct((1, w), F32) for w in reds]
    out_specs += [pl.BlockSpec((1, w), lambda i: (0, 0)) for w in reds]
    res = pl.pallas_call(
        body, name=name, grid=(s_len // tile,), in_specs=in_specs, out_specs=out_specs, out_shape=out_shape,
        compiler_params=_cparams("arbitrary" if n_red else "parallel"),
    )(*[a for a, _, _, _ in norm], *vecs)
    return res


def _mm(a, b, *, ta=False, tb=False, out_dtype=F32, name, exchange=None, tiles=None, epilogue=None):
    m, k = (a.shape[1], a.shape[0]) if ta else a.shape
    n = b.shape[0] if tb else b.shape[1]
    assert (b.shape[1] if tb else b.shape[0]) == k
    tm, tn, tk = tiles or (_tile_of(m, 1024), _tile_of(n, 1024 if n <= 1024 else 512), _tile_of(k, 1024))
    assert m % tm == 0 and n % tn == 0 and k % tk == 0
    ni, nj, nk = m // tm, n // tn, k // tk
    dims = (((0 if ta else 1,), (1 if tb else 0,)), ((), ()))
    n_ex = len(exchange.arrays) if exchange else 0
    fn, rows, vecs, outs, reds = epilogue or (None, (), (), (), ())
    rows = [r if isinstance(r, tuple) else (r, 0, r.shape[1]) for r in rows]
    assert not epilogue or tn == n
    n_res = len(outs) + len(reds) if epilogue else 1

    def body(*refs):
        a_ref, b_ref = refs[:2]
        row_refs, refs = refs[2:2 + len(rows)], refs[2 + len(rows):]
        vec_refs, refs = refs[:len(vecs)], refs[len(vecs):]
        x_refs, refs = refs[:n_ex], refs[n_ex:]
        res_refs, refs = refs[:n_res], refs[n_res:]
        out_refs, acc_ref, sems = refs[:n_ex], refs[n_ex], refs[n_ex + 1:]
        i, j, kk = pl.program_id(0), pl.program_id(1), pl.program_id(2)
        first = (i == 0) & (j == 0) & (kk == 0)

        if exchange:
            @pl.when(first)
            def _():
                exchange.start(x_refs, out_refs, sems)

        if reds:
            @pl.when(first)
            def _():
                for r in res_refs[len(outs):]:
                    r[...] = jnp.zeros(r.shape, r.dtype)

        @pl.when(kk == 0)
        def _():
            acc_ref[...] = jnp.zeros(acc_ref.shape, F32)

        acc_ref[...] += lax.dot_general(a_ref[...].astype(MXU_DTYPE), b_ref[...].astype(MXU_DTYPE), dims,
                                        preferred_element_type=F32)

        @pl.when(kk == nk - 1)
        def _():
            if not epilogue:
                res_refs[0][...] = acc_ref[...].astype(res_refs[0].dtype)
                return
            row_res, red_res = fn(acc_ref[...], *[r[...] for r in row_refs], *[v[...] for v in vec_refs])
            for o, val in zip(res_refs, row_res):
                o[...] = val.astype(o.dtype)
            for r, val in zip(res_refs[len(outs):], red_res):
                r[...] += val

        if exchange:
            @pl.when((i == ni - 1) & (j == nj - 1) & (kk == nk - 1))
            def _():
                exchange.wait(x_refs, out_refs, sems)

    a_spec = pl.BlockSpec((tk, tm), lambda i, j, kk: (kk, i)) if ta else pl.BlockSpec((tm, tk), lambda i, j, kk: (i, kk))
    b_spec = pl.BlockSpec((tn, tk), lambda i, j, kk: (j, kk)) if tb else pl.BlockSpec((tk, tn), lambda i, j, kk: (kk, j))
    in_specs = [a_spec, b_spec]
    in_specs += [pl.BlockSpec((tm, w), functools.partial(lambda i, j, kk, cb: (i, cb), cb=cb)) for _, cb, w in rows]
    in_specs += [pl.BlockSpec(v.shape, lambda i, j, kk: (0, 0)) for v in vecs]
    if epilogue:
        res_specs = [pl.BlockSpec((tm, w), lambda i, j, kk: (i, 0)) for w, _ in outs]
        res_specs += [pl.BlockSpec((1, w), lambda i, j, kk: (0, 0)) for w in reds]
        res_shape = [jax.ShapeDtypeStruct((m, w), dt) for w, dt in outs] + [jax.ShapeDtypeStruct((1, w), F32) for w in reds]
    else:
        res_specs = [pl.BlockSpec((tm, tn), lambda i, j, kk: (i, j))]
        res_shape = [jax.ShapeDtypeStruct((m, n), out_dtype)]
    ordered = bool(exchange or reds)
    res = pl.pallas_call(
        body, name=name, grid=(ni, nj, nk),
        in_specs=in_specs + (exchange.in_specs if exchange else []),
        out_specs=res_specs + (exchange.out_specs if exchange else []),
        out_shape=res_shape + (exchange.out_shape if exchange else []),
        scratch_shapes=[pltpu.VMEM((tm, tn), F32)] + (exchange.scratch if exchange else []),
        compiler_params=_cparams(*(("arbitrary",) * 3 if ordered else ("parallel", "parallel", "arbitrary"))),
    )(a, b, *[r[0] for r in rows], *vecs, *(exchange.arrays if exchange else []))
    main = res[:n_res] if epilogue else res[0]
    return (main, res[n_res:]) if exchange else main


_NT = (((1,), (1,)), ((), ()))
_TN = (((0,), (0,)), ((), ()))


def _dot(a, b):
    return jnp.dot(a, b, preferred_element_type=F32)


def _dot_nt(a, b):
    return lax.dot_general(a, b, _NT, preferred_element_type=F32)


def _dot_tn(a, b):
    return lax.dot_general(a, b, _TN, preferred_element_type=F32)


def _running_sum(x, tri):
    return _dot(x.astype(MXU_DTYPE), tri)


def _neg_softplus(z):
    u = jnp.exp2(jnp.abs(z) * (-1.0 / math.log(2.0)))
    return -jnp.maximum(z, 0.0) - jnp.log(1.0 + u)


def _walk_blocks(step, st, n, descending, group=2):
    done = 0
    size = group
    while size >= 1:
        def trip(t, s, size=size, done=done):
            js = [done + size * t + g for g in range(size)]
            return step([n - 1 - j for j in js] if descending else js, s)

        trips = (n - done) // size
        st = lax.fori_loop(0, trips, trip, st)
        done = done + size * trips
        size //= 2
    return st


def _chains(js):
    return [(h, t) for t in range(len(js)) for h in range(2)]


def _rowsum(x):
    return jnp.sum(x, axis=1, keepdims=True)


def _attn_consts():
    row = lax.broadcasted_iota(jnp.int32, (BQ, BK), 0)
    col = lax.broadcasted_iota(jnp.int32, (BQ, BK), 1)
    trow = lax.broadcasted_iota(jnp.int32, (BK, BK), 0)
    tcol = lax.broadcasted_iota(jnp.int32, (BK, BK), 1)
    lane = lax.broadcasted_iota(jnp.int32, (BQ, LANES), 1)
    klane = lax.broadcasted_iota(jnp.int32, (BK, LANES), 1)
    return row, col, trow, tcol, lane, klane


assert BQ == BK


def _diag_step(jd, descending, diag_mask, group=4):
    below = list(range(group)) if descending else list(reversed(range(group)))
    js = [jnp.maximum(jd - o, 0) for o in below]
    masks = [diag_mask if o == 0 else jd - o >= 0 for o in below]
    return js, masks, jnp.maximum(jd - (group - 1), 0)


def _key_slice(j):
    return pl.ds(pl.multiple_of(j * BK, BK), BK)


def _sb_forward(proj, s_len, exchange):
    nq = s_len // BQ
    assert s_len // BK <= HD
    qc, kc, vc = O_SBQ // LANES, O_SBK // LANES, O_SBV // LANES
    n_ex = len(exchange.arrays)

    def body(q_ref, k_ref, v_ref, *rest):
        x_refs, (o_ref, tails_ref) = rest[:n_ex], rest[n_ex:n_ex + 2]
        out_refs, sems = rest[n_ex + 2:2 * n_ex + 2], rest[2 * n_ex + 2:]
        p = pl.program_id(0)
        i = pl.program_id(1)
        jd = i

        @pl.when((p == 0) & (i == 0))
        def _():
            exchange.start(x_refs, out_refs, sems)

        row, col, trow, tcol, lane, klane = _attn_consts()
        tri = (trow >= tcol).astype(MXU_DTYPE)
        q = q_ref[...] * 0.125
        qh = [jnp.where(lane < HD, q, 0.0).astype(MXU_DTYPE), jnp.where(lane >= HD, q, 0.0).astype(MXU_DTYPE)]

        km = [klane < HD, klane >= HD]

        def step(js, st, masks):
            carry, acc, tail = st
            chains = _chains(js)
            kj = [k_ref[_key_slice(j), :].astype(MXU_DTYPE) for j in js]
            vj = [v_ref[_key_slice(j), :].astype(MXU_DTYPE) for j in js]
            z = {(h, t): _dot_nt(qh[h], kj[t]) for h, t in chains}
            run = list(carry)
            suf, carry_in = {}, {}
            for h, t in chains:
                lom = _neg_softplus(z[h, t])
                if masks:
                    lom = jnp.where(masks[t], lom, 0.0)
                suf[h, t] = _running_sum(lom, tri)
                carry_in[h, t] = run[h]
                run[h] = run[h] + _rowsum(lom)
            for h, t in chains:
                a = jnp.exp(z[h, t] + suf[h, t] + carry_in[h, t])
                if masks:
                    a = jnp.where(masks[t], a, 0.0)
                acc = acc + _dot(a.astype(MXU_DTYPE), jnp.where(km[h], vj[t], 0))
                tail_lane = js[t] if not masks or masks[t].ndim else jnp.where(masks[t], js[t], -LANES)
                tail = jnp.where(lane == h * HD + tail_lane, carry_in[h, t], tail)
            return tuple(run), acc, tail

        zero = jnp.zeros((BQ, LANES), F32)
        diag_js, masks, left = _diag_step(jd, True, col < row, group=2)
        st = step(diag_js, ((jnp.zeros((BQ, 1), F32),) * 2, zero, zero), masks)
        st = _walk_blocks(lambda js, s: step(js, s, None), st, left, True, group=8)
        o_ref[...] = st[1]
        tails_ref[...] = st[2]

        @pl.when((p == HEADS // 2 - 1) & (i == nq - 1))
        def _():
            exchange.wait(x_refs, out_refs, sems)

    blk = pl.BlockSpec((BQ, LANES), lambda p, i: (i, p))
    out = jax.ShapeDtypeStruct((s_len, SB_W), F32)
    res = pl.pallas_call(
        body, name="sb_fwd", grid=(HEADS // 2, nq),
        in_specs=[pl.BlockSpec((BQ, LANES), lambda p, i: (i, qc + p)),
                  pl.BlockSpec((s_len, LANES), lambda p, i: (0, kc + p)),
                  pl.BlockSpec((s_len, LANES), lambda p, i: (0, vc + p))] + exchange.in_specs,
        out_specs=[blk, blk] + exchange.out_specs, out_shape=[out, out] + exchange.out_shape,
        scratch_shapes=exchange.scratch,
        compiler_params=_cparams("arbitrary", "arbitrary"),
    )(proj, proj, proj, *exchange.arrays)
    return res[:2], res[2:]


def _sb_backward(proj, tails, do, s_len, exchange):
    nq = s_len // BQ
    qc, kc, vc = O_SBQ // LANES, O_SBK // LANES, O_SBV // LANES
    n_ex = len(exchange.arrays)

    def body(q_ref, k_ref, v_ref, tails_ref, do_ref, *rest):
        x_refs, (dq_ref, dk_out, dv_out) = rest[:n_ex], rest[n_ex:n_ex + 3]
        out_refs, (dk_ref, dv_ref), sems = rest[n_ex + 3:2 * n_ex + 3], rest[2 * n_ex + 3:2 * n_ex + 5], rest[2 * n_ex + 5:]
        p = pl.program_id(0)
        i = pl.program_id(1)
        jd = i

        @pl.when((p == 0) & (i == 0))
        def _():
            exchange.start(x_refs, out_refs, sems)

        @pl.when(i == 0)
        def _():
            dk_ref[...] = jnp.zeros(dk_ref.shape, F32)
            dv_ref[...] = jnp.zeros(dv_ref.shape, F32)

        row, col, trow, tcol, lane, klane = _attn_consts()
        tri = (trow >= tcol).astype(MXU_DTYPE)
        tri_p = (trow <= tcol).astype(MXU_DTYPE)
        q = q_ref[...] * 0.125
        tails_blk = tails_ref[...]
        do_blk = do_ref[...]
        hm = [lane < HD, lane >= HD]
        km = [klane < HD, klane >= HD]
        qh = [jnp.where(m, q, 0.0).astype(MXU_DTYPE) for m in hm]
        doh = [jnp.where(m, do_blk, 0.0).astype(MXU_DTYPE) for m in hm]

        def step(js, st, masks):
            before, dq = st
            chains = _chains(js)
            kj = [k_ref[_key_slice(j), :].astype(MXU_DTYPE) for j in js]
            vj = [v_ref[_key_slice(j), :].astype(MXU_DTYPE) for j in js]
            z = {(h, t): _dot_nt(qh[h], kj[t]) for h, t in chains}
            da = {(h, t): _dot_nt(doh[h], vj[t]) for h, t in chains}
            suf, sig = {}, {}
            for h, t in chains:
                lom = _neg_softplus(z[h, t])
                if masks:
                    lom = jnp.where(masks[t], lom, 0.0)
                suf[h, t] = _running_sum(lom, tri)
                sig[h, t] = jnp.exp(z[h, t] + lom)
            run = list(before)
            dl, pre, before_in = {}, {}, {}
            dk_add, dv_add = [None] * len(js), [None] * len(js)
            for h, t in chains:
                tail = _rowsum(jnp.where(lane == h * HD + js[t], tails_blk, 0.0))
                a = jnp.exp(z[h, t] + suf[h, t] + tail)
                if masks:
                    a = jnp.where(masks[t], a, 0.0)
                dl[h, t] = da[h, t] * a
                pre[h, t] = _dot(dl[h, t].astype(MXU_DTYPE), tri_p)
                dv_h = _dot_tn(a.astype(MXU_DTYPE), doh[h])
                dv_add[t] = dv_h if dv_add[t] is None else dv_add[t] + dv_h
                before_in[h, t] = run[h]
                run[h] = run[h] + _rowsum(dl[h, t])
            for h, t in chains:
                upto = before_in[h, t] + pre[h, t]
                dz = dl[h, t] - sig[h, t] * upto
                if masks:
                    dz = jnp.where(masks[t], dz, 0.0)
                dzb = dz.astype(MXU_DTYPE)
                dq = dq + _dot(dzb, jnp.where(km[h], kj[t], 0))
                dk_h = _dot_tn(dzb, qh[h])
                dk_add[t] = dk_h if dk_add[t] is None else dk_add[t] + dk_h
            for t, j in enumerate(js):
                dk_ref[_key_slice(j), :] += dk_add[t]
                dv_ref[_key_slice(j), :] += dv_add[t]
            return tuple(run), dq

        st = ((jnp.zeros((BQ, 1), F32),) * 2, jnp.zeros((BQ, LANES), F32))
        diag_js, masks, left = _diag_step(jd, False, col < row, group=2)
        st = _walk_blocks(lambda js, s: step(js, s, None), st, left, False, group=8)
        st = step(diag_js, st, masks)
        dq_ref[...] = (st[1] * 0.125).astype(dq_ref.dtype)

        @pl.when(i == nq - 1)
        def _():
            dk_out[...] = dk_ref[...].astype(dk_out.dtype)
            dv_out[...] = dv_ref[...].astype(dv_out.dtype)

        @pl.when((p == HEADS // 2 - 1) & (i == nq - 1))
        def _():
            exchange.wait(x_refs, out_refs, sems)

    blk = pl.BlockSpec((BQ, LANES), lambda p, i: (i, p))
    full = pl.BlockSpec((s_len, LANES), lambda p, i: (0, p))
    out = jax.ShapeDtypeStruct((s_len, SB_W), BF16)
    res = pl.pallas_call(
        body, name="sb_bwd", grid=(HEADS // 2, nq),
        in_specs=[pl.BlockSpec((BQ, LANES), lambda p, i: (i, qc + p)),
                  pl.BlockSpec((s_len, LANES), lambda p, i: (0, kc + p)),
                  pl.BlockSpec((s_len, LANES), lambda p, i: (0, vc + p)),
                  blk, blk] + exchange.in_specs,
        out_specs=[blk, full, full] + exchange.out_specs, out_shape=[out, out, out] + exchange.out_shape,
        scratch_shapes=[pltpu.VMEM((s_len, LANES), F32)] * 2 + exchange.scratch,
        compiler_params=_cparams("arbitrary", "arbitrary"),
    )(proj, proj, proj, tails, do, *exchange.arrays)
    return res[:3], res[3:]


def _pair_mask(rows, h):
    lane = lax.broadcasted_iota(jnp.int32, (rows, 2 * LANES), 1)
    rot = lane - LANES
    return (((lane < LANES) & (lane // HD == h))
            | ((lane >= LANES) & (rot < 2 * ROPE) & ((rot // (ROPE // 2)) % 2 == h)))


def _mla_forward(q_cat, k_cat, kv, s_len):
    nq = s_len // BQ
    scale = 1.0 / math.sqrt(QK_DIM)

    def body(q_ref, k_ref, v_ref, o_ref, lse_ref):
        i = pl.program_id(1)
        jd = i
        row, col, trow, tcol, lane, klane = _attn_consts()
        q = q_ref[...]
        hm = [lane < HD, lane >= HD]
        km = [klane < HD, klane >= HD]
        qh = [jnp.where(_pair_mask(BQ, h), q, 0) for h in range(2)]

        def step(js, st, masks):
            m_run, l_run, acc = st
            chains = _chains(js)
            kj = [k_ref[_key_slice(j), :] for j in js]
            vj = [v_ref[_key_slice(j), :].astype(MXU_DTYPE) for j in js]
            s = {}
            for h, t in chains:
                s[h, t] = _dot_nt(qh[h], kj[t]) * scale
                if masks:
                    s[h, t] = jnp.where(masks[t], s[h, t], -jnp.inf)
            m_new, alpha, l_new = [], [], []
            for h in range(2):
                top = m_run[h]
                for t in range(len(js)):
                    top = jnp.maximum(top, jnp.max(s[h, t], axis=1, keepdims=True))
                m_new.append(top)
                alpha.append(jnp.exp(m_run[h] - top))
                l_new.append(alpha[h] * l_run[h])
            add = None
            for h, t in chains:
                pr = jnp.exp(s[h, t] - m_new[h])
                l_new[h] = l_new[h] + _rowsum(pr)
                part = _dot(pr.astype(MXU_DTYPE), jnp.where(km[h], vj[t], 0))
                add = part if add is None else add + part
            acc = jnp.where(hm[0], alpha[0], alpha[1]) * acc + add
            return tuple(m_new), tuple(l_new), acc

        st = ((jnp.full((BQ, 1), -1e30, F32),) * 2, (jnp.zeros((BQ, 1), F32),) * 2, jnp.zeros((BQ, LANES), F32))
        diag_js, masks, left = _diag_step(jd, True, col <= row)
        st = step(diag_js, st, masks)
        m_run, l_run, acc = _walk_blocks(lambda js, s: step(js, s, None), st, left, True, group=8)
        o_ref[...] = acc / jnp.where(hm[0], l_run[0], l_run[1])
        lse_ref[...] = jnp.where(hm[0], m_run[0] + jnp.log(l_run[0]), m_run[1] + jnp.log(l_run[1]))

    blk = pl.BlockSpec((BQ, LANES), lambda p, i: (i, p))
    out = jax.ShapeDtypeStruct((s_len, MLA_W), F32)
    return pl.pallas_call(
        body, name="mla_fwd", grid=(HEADS // 2, nq),
        in_specs=[pl.BlockSpec((BQ, 2 * LANES), lambda p, i: (i, p)),
                  pl.BlockSpec((s_len, 2 * LANES), lambda p, i: (0, p)),
                  pl.BlockSpec((s_len, LANES), lambda p, i: (0, MLA_W // LANES + p))],
        out_specs=[blk, blk], out_shape=[out, out],
        compiler_params=_cparams("parallel", "parallel"),
    )(q_cat, k_cat, kv)


def _mla_backward(q_cat, k_cat, kv, o, lse, do, s_len):
    nq = s_len // BQ
    scale = 1.0 / math.sqrt(QK_DIM)

    def body(q_ref, k_ref, v_ref, o_ref, lse_ref, do_ref, dq_ref, dk_ref, dv_ref):
        i = pl.program_id(1)

        @pl.when(i == 0)
        def _():
            dk_ref[...] = jnp.zeros(dk_ref.shape, F32)
            dv_ref[...] = jnp.zeros(dv_ref.shape, F32)

        jd = i
        row, col, trow, tcol, lane, klane = _attn_consts()
        q = q_ref[...]
        o_blk = o_ref[...]
        do_blk = do_ref[...]
        lse_blk = lse_ref[...]
        hm = [lane < HD, lane >= HD]
        kpm = [_pair_mask(BK, h) for h in range(2)]
        qh = [jnp.where(_pair_mask(BQ, h), q, 0) for h in range(2)]
        doh_f = [jnp.where(m, do_blk, 0.0) for m in hm]
        doh = [d.astype(MXU_DTYPE) for d in doh_f]
        delta = [jnp.sum(d * o_blk, axis=1, keepdims=True) for d in doh_f]
        lse_h = [jnp.sum(jnp.where(lane == h * HD, lse_blk, 0.0), axis=1, keepdims=True) for h in range(2)]

        def step(js, st, masks):
            dq = st
            chains = _chains(js)
            kj = [k_ref[_key_slice(j), :] for j in js]
            vj = [v_ref[_key_slice(j), :].astype(MXU_DTYPE) for j in js]
            s = {(h, t): _dot_nt(qh[h], kj[t]) for h, t in chains}
            dp = {(h, t): _dot_nt(doh[h], vj[t]) for h, t in chains}
            adds = [[None] * len(js) for _ in range(2)]

            def accumulate(slot, t, part):
                adds[slot][t] = part if adds[slot][t] is None else adds[slot][t] + part

            for h, t in chains:
                pr = jnp.exp(s[h, t] * scale - lse_h[h])
                if masks:
                    pr = jnp.where(masks[t], pr, 0.0)
                dsb = (pr * (dp[h, t] - delta[h]) * scale).astype(MXU_DTYPE)
                dq = dq + _dot(dsb, jnp.where(kpm[h], kj[t], 0))
                accumulate(0, t, _dot_tn(dsb, qh[h]))
                accumulate(1, t, _dot_tn(pr.astype(MXU_DTYPE), doh[h]))
            for t, j in enumerate(js):
                dk_ref[_key_slice(j), :] += adds[0][t]
                dv_ref[_key_slice(j), :] += adds[1][t]
            return dq

        diag_js, masks, left = _diag_step(jd, True, col <= row)
        st = step(diag_js, jnp.zeros((BQ, 2 * LANES), F32), masks)
        dq_ref[...] = _walk_blocks(lambda js, s: step(js, s, None), st, left, True, group=8)

    blk = pl.BlockSpec((BQ, LANES), lambda p, i: (i, p))
    full = pl.BlockSpec((s_len, LANES), lambda p, i: (0, p))
    out = jax.ShapeDtypeStruct((s_len, MLA_W), F32)
    out_cat = jax.ShapeDtypeStruct((s_len, 2 * MLA_W), F32)
    return pl.pallas_call(
        body, name="mla_bwd", grid=(HEADS // 2, nq),
        in_specs=[pl.BlockSpec((BQ, 2 * LANES), lambda p, i: (i, p)),
                  pl.BlockSpec((s_len, 2 * LANES), lambda p, i: (0, p)),
                  pl.BlockSpec((s_len, LANES), lambda p, i: (0, MLA_W // LANES + p)),
                  blk, blk, blk],
        out_specs=[pl.BlockSpec((BQ, 2 * LANES), lambda p, i: (i, p)),
                   pl.BlockSpec((s_len, 2 * LANES), lambda p, i: (0, p)), full],
        out_shape=[out_cat, out_cat, out],
        compiler_params=_cparams("arbitrary", "arbitrary"),
    )(q_cat, k_cat, kv, o, lse, do)


def _mesh_pos():
    return lax.axis_index("x"), lax.axis_index("y"), lax.axis_index("c")


def _dev_index(px, py, pc):
    return 4 * px + 2 * py + pc


def _all_gather(block, name):
    return _all_gather_parts([block], name)[0]


def _all_gather_parts(blocks, name):
    n = len(blocks)

    def body(*refs):
        x_refs, out_refs = refs[:n], refs[n:2 * n]
        send_sems, recv_sems, local_sems = refs[2 * n:]
        x, y, c = _mesh_pos()
        me, sibling = (x, y, c), (x, y, 1 - c)
        chips = [(1 - x, y), (x, 1 - y), (1 - x, 1 - y)]

        def copy(a, k, blockpos, to, src=None):
            slot = out_refs[a].at[_dev_index(*blockpos)]
            return pltpu.make_async_remote_copy(
                src_ref=slot if src is None else src, dst_ref=slot,
                send_sem=send_sems.at[7 * a + k], recv_sem=recv_sems.at[7 * a + k],
                device_id=to, device_id_type=pl.DeviceIdType.MESH)

        mine = [pltpu.make_async_copy(x_refs[a], out_refs[a].at[_dev_index(*me)], local_sems.at[a]) for a in range(n)]
        for cp in mine:
            cp.start()
        first = []
        for a in range(n):
            first.append(copy(a, 0, me, sibling, src=x_refs[a]))
            first += [copy(a, 1 + j, me, (*chip, c), src=x_refs[a]) for j, chip in enumerate(chips)]
        for cp in first:
            cp.start()
        passed = []
        for j, chip in enumerate(chips):
            for a in range(n):
                copy(a, 1 + j, (*chip, c), me).wait_recv()
                passed.append(copy(a, 4 + j, (*chip, c), sibling))
                passed[-1].start()
        for a in range(n):
            copy(a, 0, sibling, me).wait_recv()
            for j, chip in enumerate(chips):
                copy(a, 4 + j, (*chip, 1 - c), me).wait_recv()
        for cp in first + passed:
            cp.wait_send()
        for cp in mine:
            cp.wait()

    return pl.pallas_call(
        body, name=name,
        out_shape=[jax.ShapeDtypeStruct((N_DEV,) + b.shape, b.dtype) for b in blocks],
        in_specs=[pl.BlockSpec(memory_space=pl.ANY)] * n, out_specs=[pl.BlockSpec(memory_space=pl.ANY)] * n,
        scratch_shapes=[pltpu.SemaphoreType.DMA((7 * n,)), pltpu.SemaphoreType.DMA((7 * n,)),
                        pltpu.SemaphoreType.DMA((n,))],
    )(*blocks)


class _Exchange:
    SCOPES = {
        "devices": ([(fx, fy, fc) for fx in (0, 1) for fy in (0, 1) for fc in (0, 1)][1:], lambda x, y, c: 4 * x + 2 * y + c),
        "chips": ([(1, 0, 0), (0, 1, 0), (1, 1, 0)], lambda x, y, c: 2 * x + y),
        "cores": ([(0, 0, 1)], lambda x, y, c: c),
    }

    def __init__(self, arrays, scope="devices"):
        self.arrays = list(arrays)
        self.flips, self.index = self.SCOPES[scope]
        assert all(a.shape[0] == len(self.flips) + 1 for a in self.arrays)
        n, self.n_peers = len(self.arrays), len(self.flips)
        self.in_specs = [pl.BlockSpec(memory_space=pl.ANY)] * n
        self.out_specs = [pl.BlockSpec(memory_space=pl.ANY)] * n
        self.out_shape = [jax.ShapeDtypeStruct(a.shape, a.dtype) for a in self.arrays]
        self.scratch = [pltpu.SemaphoreType.DMA((self.n_peers * n,)), pltpu.SemaphoreType.DMA((self.n_peers * n,)),
                        pltpu.SemaphoreType.DMA((n,))]

    def _copies(self, x_refs, out_refs, sems, with_arrivals):
        send_sems, recv_sems, local_sems = sems
        x, y, c = _mesh_pos()
        me = self.index(x, y, c)
        peers = [(1 - x if fx else x, 1 - y if fy else y, 1 - c if fc else c) for fx, fy, fc in self.flips]
        mine, sends, arrivals = [], [], []
        for a in range(len(self.arrays)):
            mine.append(pltpu.make_async_copy(x_refs[a].at[me], out_refs[a].at[me], local_sems.at[a]))
            for k, peer in enumerate(peers):
                sem = self.n_peers * a + k
                sends.append(pltpu.make_async_remote_copy(
                    src_ref=x_refs[a].at[self.index(*peer)], dst_ref=out_refs[a].at[me],
                    send_sem=send_sems.at[sem], recv_sem=recv_sems.at[sem],
                    device_id=peer, device_id_type=pl.DeviceIdType.MESH))
                if not with_arrivals:
                    continue
                arrivals.append(pltpu.make_async_remote_copy(
                    src_ref=x_refs[a].at[me], dst_ref=out_refs[a].at[self.index(*peer)],
                    send_sem=send_sems.at[sem], recv_sem=recv_sems.at[sem],
                    device_id=peer, device_id_type=pl.DeviceIdType.MESH))
        return mine, sends, arrivals

    def start(self, x_refs, out_refs, sems):
        mine, sends, _ = self._copies(x_refs, out_refs, sems, False)
        for cp in mine + sends:
            cp.start()

    def wait(self, x_refs, out_refs, sems):
        mine, sends, arrivals = self._copies(x_refs, out_refs, sems, True)
        for cp in arrivals:
            cp.wait_recv()
        for cp in sends:
            cp.wait_send()
        for cp in mine:
            cp.wait()


def _exchange_call(exchange, name):
    n = len(exchange.arrays)

    def body(*refs):
        exchange.start(refs[:n], refs[n:2 * n], refs[2 * n:])
        exchange.wait(refs[:n], refs[n:2 * n], refs[2 * n:])

    return pl.pallas_call(body, name=name, out_shape=exchange.out_shape, in_specs=exchange.in_specs,
                          out_specs=exchange.out_specs, scratch_shapes=exchange.scratch)(*exchange.arrays)


def _sum_blocks(parts, name, out_dtype=F32):
    n, r, c = parts.shape
    row_tiles = [t for t in range(16, min(r, 2048) + 1, 16) if r % t == 0]
    if row_tiles:
        tr, tc = max(row_tiles), c
    else:
        tr, tc = r, 2 * LANES
    assert c % tc == 0

    def body(p_ref, o_ref):
        acc = p_ref[0].astype(F32)
        for s in range(1, n):
            acc = acc + p_ref[s].astype(F32)
        o_ref[...] = acc.astype(o_ref.dtype)

    return pl.pallas_call(
        body, name=name, grid=(r // tr, c // tc),
        in_specs=[pl.BlockSpec((n, tr, tc), lambda i, j: (0, i, j))],
        out_specs=pl.BlockSpec((tr, tc), lambda i, j: (i, j)),
        out_shape=jax.ShapeDtypeStruct((r, c), out_dtype),
        compiler_params=_cparams("parallel", "parallel"),
    )(parts)


def _sigmoid(x):
    return 1.0 / (1.0 + jnp.exp(-x))


def _silu(x):
    return x * _sigmoid(x)


def _silu_grad(x):
    s = _sigmoid(x)
    return s * (1.0 + x * (1.0 - s))


def _colsum(x):
    return jnp.sum(x, axis=0, keepdims=True)


def _rms(x):
    return lax.rsqrt(jnp.mean(x * x, axis=-1, keepdims=True) + EPS)


def _rms_bwd(xn, r, dxn):
    return r * (dxn - xn * jnp.mean(dxn * xn, axis=-1, keepdims=True))


def _adamw(w, g, m, v):
    m = ADAM_B1 * m + (1.0 - ADAM_B1) * g
    v = ADAM_B2 * v + (1.0 - ADAM_B2) * jnp.square(g)
    m_hat = m / (1.0 - ADAM_B1 ** ADAM_STEP)
    v_hat = v / (1.0 - ADAM_B2 ** ADAM_STEP)
    delta = -ADAM_LR * (m_hat / (jnp.sqrt(v_hat) + ADAM_EPS) + ADAM_WD * w)
    return delta, m, v


def _adamw_call(w, g, m, v, name):
    r, c = w.shape
    if r % 256 == 0:
        tr, tc = 256, c
    elif r * c <= 256 * 1024 or c % (2 * LANES):
        tr, tc = r, c
    else:
        tr, tc = r, 2 * LANES

    def body(w_ref, g_ref, m_ref, v_ref, d_out, m_out, v_out):
        d_out[...], m_out[...], v_out[...] = _adamw(w_ref[...], g_ref[...], m_ref[...], v_ref[...])

    spec = pl.BlockSpec((tr, tc), lambda i, j: (i, j))
    return pl.pallas_call(
        body, name=name, grid=(r // tr, c // tc), in_specs=[spec] * 4, out_specs=[spec] * 3,
        out_shape=[jax.ShapeDtypeStruct((r, c), F32)] * 3, compiler_params=_cparams("parallel", "parallel"),
    )(w, g, m, v)


def _uq_to_kernel_layout(w):
    lead = w.shape[:-1]
    t = w.reshape(lead + (HEADS, QK_DIM))
    return jnp.concatenate([t[..., :NOPE].reshape(lead + (HEADS * NOPE,)),
                            t[..., NOPE:NOPE + ROPE // 2].reshape(lead + (LANES,)),
                            t[..., NOPE + ROPE // 2:].reshape(lead + (LANES,))], axis=-1)


def _uq_from_kernel_layout(w):
    lead = w.shape[:-1]
    nope = w[..., :HEADS * NOPE].reshape(lead + (HEADS, NOPE))
    r1 = w[..., HEADS * NOPE:HEADS * NOPE + LANES].reshape(lead + (HEADS, ROPE // 2))
    r2 = w[..., HEADS * NOPE + LANES:].reshape(lead + (HEADS, ROPE // 2))
    return jnp.concatenate([nope, r1, r2], axis=-1).reshape(lead + (HEADS * QK_DIM,))


def _ukv_to_kernel_layout(w):
    lead = w.shape[:-1]
    t = w.reshape(lead + (HEADS, NOPE + HD))
    return jnp.concatenate([t[..., :NOPE].reshape(lead + (HEADS * NOPE,)),
                            t[..., NOPE:].reshape(lead + (HEADS * HD,))], axis=-1)


def _ukv_from_kernel_layout(w):
    lead = w.shape[:-1]
    kn = w[..., :HEADS * NOPE].reshape(lead + (HEADS, NOPE))
    vv = w[..., HEADS * NOPE:].reshape(lead + (HEADS, HD))
    return jnp.concatenate([kn, vv], axis=-1).reshape(lead + (HEADS * (NOPE + HD),))


def _w_in_t_to_kernel_layout(wt):
    sb = wt[0:2048]
    c_q = wt[2048:2432]
    c_kv = wt[2432:2688]
    k_rot = wt[2688:2720]
    mla_z = wt[2720:3232]
    gates = wt[3232:5280]
    zeros = jnp.zeros((LANES, wt.shape[1]), wt.dtype)
    k1 = jnp.tile(k_rot[:ROPE // 2], (HEADS, 1))
    k2 = jnp.tile(k_rot[ROPE // 2:], (HEADS, 1))
    return jnp.concatenate([gates, sb, mla_z, c_q, zeros, c_kv, k1, k2], axis=0)


def _w_in_t_from_kernel_layout(gt, g_rot):
    return jnp.concatenate([gt[O_SBQ:O_SBQ + 2048], gt[O_CQ:O_CQ + Q_RANK], gt[O_CKV:O_CKV + KV_RANK],
                            g_rot.astype(gt.dtype), gt[O_MLAZ:O_MLAZ + MLA_W], gt[O_GA:O_GA + 2 * D]], axis=0)


def kernel(x, c, positions, w_ada, b_ada, norm_gain, w_in, q_norm_gain, w_uq, kv_norm_gain, w_ukv, w_branch_a, w_branch_b, w_out, final_norm_gain, loss_target, m_w_ada, m_b_ada, m_norm_gain, m_w_in, m_q_norm_gain, m_w_uq, m_kv_norm_gain, m_w_ukv, m_w_branch_a, m_w_branch_b, m_w_out, m_final_norm_gain, v_w_ada, v_b_ada, v_norm_gain, v_w_in, v_q_norm_gain, v_w_uq, v_kv_norm_gain, v_w_ukv, v_w_branch_a, v_w_branch_b, v_w_out, v_final_norm_gain):
    s_len = x.shape[1]
    me = _dev_index(*_mesh_pos())
    x2d = x[0]
    tgt = loss_target[0]

    w_in_t = w_in[0].T.astype(BF16)
    big = [w_uq[0], w_ukv[0], w_branch_a[0], w_branch_b[0], w_out[0]]
    big_sizes = [int(w.size) for w in big]
    packed = jnp.concatenate([w.astype(BF16).reshape(-1, LANES) for w in big], axis=0)
    g_in_t, c_all = _all_gather_parts([w_in_t, c.reshape(8, LANES)], "gather_w_in")
    c_all = c_all.reshape(N_DEV, D)
    w_in_kt = _w_in_t_to_kernel_layout(g_in_t.reshape(N_DEV * w_in_t.shape[0], D))

    mod_cols = _mm(c_all, w_ada[0], name="ada_mod")
    mod_all = _all_gather(mod_cols, "gather_mod")
    mod = lax.dynamic_index_in_dim(mod_all, me, axis=1, keepdims=False).reshape(1, 3 * D)
    mod_shift, mod_scale, mod_gate = mod[:, :D], mod[:, D:2 * D], mod[:, 2 * D:]
    b_shift, b_scale, b_gate = b_ada[:, :D], b_ada[:, D:2 * D], b_ada[:, 2 * D:]
    g1 = norm_gain
    gq, gkv = q_norm_gain, kv_norm_gain
    gf = final_norm_gain.reshape(1, D)

    def f_h(x_, g1_, ms, bs, msc, bsc):
        xn = x_ * _rms(x_)
        return (xn * g1_ * (1.0 + (msc + bsc)) + (ms + bs),), ()

    (h,) = _rowwise(f_h, [x2d], [g1, mod_shift, b_shift, mod_scale, b_scale], [(D, BF16)], name="ada_norm")
    proj = _mm(h, w_in_kt, tb=True, name="proj_in", tiles=(min(s_len, 1024), IN_PAD // 2, D))
    proj_at = dict(ga=(O_GA, D), gb=(O_GB, D), sbz=(O_SBZ, SB_W), mlaz=(O_MLAZ, MLA_W),
                   cq=(O_CQ, Q_RANK), ckv=(O_CKV, KV_RANK), krot=(O_KROT, LANES))

    def win(key, shift=0):
        offset, width = proj_at[key]
        assert offset % width == 0
        return proj, offset // width + shift, width

    (o_a, sb_tails), (gathered,) = _sb_forward(
        proj, s_len, _Exchange([jnp.broadcast_to(packed[None], (N_DEV,) + packed.shape)]))
    offs = [0]
    for n in big_sizes:
        offs.append(offs[-1] + n // LANES)

    def unpack(t, shape):
        return gathered[:, offs[t]:offs[t + 1], :].reshape((N_DEV,) + shape)

    def cols(t, shape):
        return unpack(t, shape).transpose(1, 0, 2).reshape(shape[0], N_DEV * shape[1])

    w_uq_k = _uq_to_kernel_layout(cols(0, big[0].shape))
    w_ukv_k = _ukv_to_kernel_layout(cols(1, big[1].shape))
    w_a_f = cols(2, big[2].shape)
    w_b_f = cols(3, big[3].shape)
    w_out_f = unpack(4, big[4].shape).reshape(D, D)

    def f_lat(cq, ckv, gq_, gkv_):
        return (cq * _rms(cq) * gq_, ckv * _rms(ckv) * gkv_), ()

    cq_n, ckv_n = _rowwise(f_lat, [win("cq"), win("ckv")], [gq, gkv],
                           [(Q_RANK, BF16), (KV_RANK, BF16)], name="latent_norm")
    q_mla = _mm(cq_n, w_uq_k, name="q_up")
    kv = _mm(ckv_n, w_ukv_k, out_dtype=BF16, name="kv_up")

    inv_freq = ROPE_BASE ** (-jnp.arange(0, ROPE, 2, dtype=F32) / ROPE)
    inv_freq_t = jnp.tile(inv_freq, HEADS).reshape(1, LANES)
    pos_col = positions.reshape(s_len, 1).astype(F32)

    pairs = HEADS // 2

    def f_rope(pos, qn, q1, q2, kn, k1, k2, freq):
        ang = pos * freq
        cs, sn = jnp.cos(ang), jnp.sin(ang)
        q1r, q2r = q1 * cs - q2 * sn, q1 * sn + q2 * cs
        k1r, k2r = k1 * cs - k2 * sn, k1 * sn + k2 * cs
        lane = lax.broadcasted_iota(jnp.int32, q1.shape, 1)
        first, second = lane < ROPE, (lane >= ROPE) & (lane < 2 * ROPE)
        k_rot = jnp.where(first, k1r, jnp.where(second, k2r, 0.0))
        q_parts, k_parts = [], []
        for p in range(pairs):
            q_rot = jnp.where(first, pltpu.roll(q1r, (LANES - ROPE * p) % LANES, 1),
                              jnp.where(second, pltpu.roll(q2r, (LANES + ROPE - ROPE * p) % LANES, 1), 0.0))
            q_parts += [qn[:, LANES * p:LANES * (p + 1)], q_rot]
            k_parts += [kn[:, LANES * p:LANES * (p + 1)], k_rot]
        return (jnp.concatenate(q_parts, axis=1), jnp.concatenate(k_parts, axis=1), cs, sn), ()

    q_cat, k_cat, cos_t, sin_t = _rowwise(
        f_rope, [pos_col, (q_mla, 0, MLA_W), (q_mla, 4, LANES), (q_mla, 5, LANES), (kv, 0, MLA_W),
                 win("krot"), win("krot", 1)], [inv_freq_t],
        [(2 * MLA_W, BF16), (2 * MLA_W, BF16), (LANES, F32), (LANES, F32)], name="rope")

    o_b, lse = _mla_forward(q_cat, k_cat, kv, s_len)

    def f_gate(oa, za, ob, zb):
        return (oa * _silu(za), ob * _silu(zb)), ()

    ya_in, yb_in = _rowwise(f_gate, [o_a, win("sbz"), o_b, win("mlaz")], [],
                            [(SB_W, BF16), (MLA_W, BF16)], name="branch_gate")
    y_a = _mm(ya_in, w_a_f, out_dtype=BF16, name="branch_a")

    def f_merge(yb, ga, gb, ya):
        return (yb, _sigmoid(ga) * ya + _sigmoid(gb) * yb), ()

    y_b, merged = _mm(yb_in, w_b_f, name="branch_b_merge", tiles=(min(s_len, 512), D, MLA_W),
                      epilogue=(f_merge, [win("ga"), win("gb"), y_a], [], [(D, BF16), (D, BF16)], []))

    def f_loss(out_, x_, t_, mg, bg, gf_):
        gate = mg + bg
        x2 = x_ + gate * out_
        r2 = _rms(x2)
        xn2 = x2 * r2
        err = xn2 * gf_ - t_
        loss = jnp.full((1, LANES), 0.5 / D, F32) * jnp.sum(err * err)
        dy = err * (1.0 / D)
        dx2 = _rms_bwd(xn2, r2, dy * gf_)
        return (dx2, dx2 * gate), (loss, _colsum(dy * xn2), _colsum(dx2 * out_))

    dx2, d_out, loss_part, d_gf, d_gate = _mm(
        merged, w_out_f, name="out_proj_loss", tiles=(min(s_len, 512), D, D),
        epilogue=(f_loss, [x2d, tgt], [mod_gate, b_gate, gf], [(D, F32), (D, BF16)], [LANES, D, D]))

    dw_out = _mm(merged, d_out, ta=True, name="dw_out")

    def f_dmerge(dm, ga, gb, ya, yb):
        sa, sb = _sigmoid(ga), _sigmoid(gb)
        return (dm * sa, dm * sb, dm * ya * sa * (1.0 - sa), dm * yb * sb * (1.0 - sb)), ()

    d_ya, d_yb, d_ga, d_gb = _mm(
        d_out, w_out_f, tb=True, name="d_merge", tiles=(min(s_len, 256), D, D),
        epilogue=(f_dmerge, [win("ga"), win("gb"), y_a, y_b], [], [(D, BF16)] * 4, []))
    dw_a = _mm(ya_in, d_ya, ta=True, name="dw_branch_a")
    dw_b = _mm(yb_in, d_yb, ta=True, name="dw_branch_b")

    def f_dgate(d_in, o_, z_):
        return (d_in * _silu(z_), d_in * o_ * _silu_grad(z_)), ()

    d_oa, d_sbz = _mm(d_ya, w_a_f, tb=True, name="d_branch_a",
                      epilogue=(f_dgate, [o_a, win("sbz")], [], [(SB_W, BF16), (SB_W, BF16)], []))
    d_ob, d_mlaz = _mm(d_yb, w_b_f, tb=True, name="d_branch_b",
                       epilogue=(f_dgate, [o_b, win("mlaz")], [], [(MLA_W, F32), (MLA_W, BF16)], []))

    dq_cat, dk_cat, dv_b = _mla_backward(q_cat, k_cat, kv, o_b, lse, d_ob, s_len)

    def f_drope(dq, dk, dv_, cs, sn):
        lane = lax.broadcasted_iota(jnp.int32, cs.shape, 1)
        first, second = lane < ROPE, (lane >= ROPE) & (lane < 2 * ROPE)
        dq1 = dq2 = dk1 = dk2 = None
        for p in range(pairs):
            q_rot = dq[:, LANES * (2 * p + 1):LANES * (2 * p + 2)]
            k_rot = dk[:, LANES * (2 * p + 1):LANES * (2 * p + 2)]
            parts = (pltpu.roll(jnp.where(first, q_rot, 0.0), (ROPE * p) % LANES, 1),
                     pltpu.roll(jnp.where(second, q_rot, 0.0), (LANES - ROPE + ROPE * p) % LANES, 1),
                     jnp.where(first, k_rot, 0.0), jnp.where(second, k_rot, 0.0))
            if p == 0:
                dq1, dq2, dk1, dk2 = parts
            else:
                dq1, dq2, dk1, dk2 = dq1 + parts[0], dq2 + parts[1], dk1 + parts[2], dk2 + parts[3]
        dqn_ = [dq[:, 2 * LANES * p:2 * LANES * p + LANES] for p in range(pairs)]
        dkn_ = [dk[:, 2 * LANES * p:2 * LANES * p + LANES] for p in range(pairs)]
        return (jnp.concatenate(dqn_ + [dq1 * cs + dq2 * sn, dq2 * cs - dq1 * sn], axis=1),
                jnp.concatenate(dkn_ + [dv_], axis=1),
                jnp.concatenate([dk1 * cs + dk2 * sn, dk2 * cs - dk1 * sn], axis=1)), ()

    dq_k, dkv_k, d_krot = _rowwise(f_drope, [dq_cat, dk_cat, dv_b, cos_t, sin_t], [],
                                   [(HEADS * QK_DIM, BF16), (2 * MLA_W, BF16), (2 * LANES, BF16)], name="d_rope")
    dw_uq_k = _mm(cq_n, dq_k, ta=True, name="dw_uq")
    dw_ukv_k = _mm(ckv_n, dkv_k, ta=True, name="dw_ukv")

    def f_dlat(d_normed, latent, gain):
        r = _rms(latent)
        normed = latent * r
        return (_rms_bwd(normed, r, d_normed * gain),), (_colsum(d_normed * normed),)

    d_cq, d_gq = _mm(dq_k, w_uq_k, tb=True, name="d_cq_norm",
                     epilogue=(f_dlat, [win("cq")], [gq], [(Q_RANK, BF16)], [Q_RANK]))
    d_ckv, d_gkv = _mm(dkv_k, w_ukv_k, tb=True, name="d_ckv_norm",
                       epilogue=(f_dlat, [win("ckv")], [gkv], [(KV_RANK, BF16)], [KV_RANK]))

    def col_blocks(g):
        kdim, n8 = g.shape
        return g.astype(BF16).reshape(kdim, N_DEV, n8 // N_DEV).transpose(1, 0, 2).reshape(N_DEV, -1, LANES)

    g_blocks = jnp.concatenate([col_blocks(_uq_from_kernel_layout(dw_uq_k)), col_blocks(_ukv_from_kernel_layout(dw_ukv_k)),
                                col_blocks(dw_a), col_blocks(dw_b), dw_out.astype(BF16).reshape(N_DEV, -1, LANES)], axis=1)
    (d_sbq, d_sbk, d_sbv), (g_recv,) = _sb_backward(proj, sb_tails, d_oa, s_len, _Exchange([g_blocks]))

    d_proj = jnp.concatenate([d_ga, d_gb, d_sbq, d_sbk, d_sbv, d_sbz, d_mlaz,
                              d_cq, jnp.zeros((s_len, LANES), BF16), d_ckv, d_krot], axis=1)
    dw_in_kt = _mm(d_proj, h, ta=True, out_dtype=BF16, name="dw_in", tiles=(512, D, s_len))

    def krot_body(t_ref, o_ref):
        half = ROPE // 2
        for part in range(2):
            acc = t_ref[part * LANES:part * LANES + half, :].astype(F32)
            for hh in range(1, HEADS):
                acc = acc + t_ref[part * LANES + hh * half:part * LANES + (hh + 1) * half, :].astype(F32)
            o_ref[part * half:(part + 1) * half, :] = acc

    dw_krot = pl.pallas_call(krot_body, name="dw_krot_sum", out_shape=jax.ShapeDtypeStruct((ROPE, D), F32))(
        dw_in_kt[O_KROT:O_KROT + 2 * LANES])

    w_cols = w_in.shape[2]
    g_by_core = _w_in_t_from_kernel_layout(dw_in_kt, dw_krot).reshape(N_DEV // 2, 2, w_cols, D).transpose(1, 0, 2, 3)
    (g_pair,) = _exchange_call(_Exchange([g_by_core.reshape(2, -1, D)], scope="cores"), "pair_grads_w_in")
    g_in_blocks = _sum_blocks(g_pair, "sum_pair_grads_w_in", out_dtype=BF16).reshape(N_DEV // 2, w_cols, D)
    def f_dx(dh_, x_, dx2_, g1_, msc, bsc):
        r = _rms(x_)
        xn = x_ * r
        dn1 = dh_ * (1.0 + (msc + bsc))
        return ((dx2_ + _rms_bwd(xn, r, dn1 * g1_),),
                (_colsum(dh_), _colsum(dh_ * (xn * g1_)), _colsum(dn1 * xn)))

    (grad_x2d, d_shift, d_scale, d_g1), (g_in_recv,) = _mm(
        d_proj, w_in_kt, name="d_h", tiles=(min(s_len, 512), D, 512), exchange=_Exchange([g_in_blocks], scope="chips"),
        epilogue=(f_dx, [x2d, dx2], [g1, mod_scale, b_scale], [(D, F32)], [D, D, D]))

    g_in_sum_t = _sum_blocks(g_in_recv, "sum_grads_w_in")
    g_sum = _sum_blocks(g_recv, "sum_grads")
    g_big = [g_sum[offs[t]:offs[t + 1]].reshape(big[t].shape) for t in range(5)]

    small = jnp.concatenate([d_shift, d_scale, d_gate, d_g1, d_gq, d_gkv, d_gf, loss_part], axis=1)
    n_small = small.shape[1]
    pad = (-n_small) % (8 * LANES)
    small = jnp.pad(small, ((0, 0), (0, pad))).reshape(-1, LANES)
    small_all = _all_gather(small, "gather_small")
    small_sum = _sum_blocks(small_all, "sum_small").reshape(1, -1)
    g_b_ada = small_sum[:, :3 * D]
    g_g1 = small_sum[:, 3 * D:4 * D]
    g_gq = small_sum[:, 4 * D:4 * D + Q_RANK]
    g_gkv = small_sum[:, 4 * D + Q_RANK:4 * D + Q_RANK + KV_RANK]
    g_gf = small_sum[:, 4 * D + Q_RANK + KV_RANK:4 * D + Q_RANK + KV_RANK + D]

    dmod_all = small_all.reshape(N_DEV, -1)[:, :3 * D]
    dmod_cols = lax.dynamic_slice_in_dim(dmod_all, me * (3 * D // N_DEV), 3 * D // N_DEV, axis=1)
    g_w_ada = _mm(c_all, dmod_cols, ta=True, name="dw_ada")

    loss = small_sum[0, n_small - LANES]

    names = ["w_ada", "b_ada", "norm_gain", "w_in", "q_norm_gain", "w_uq", "kv_norm_gain", "w_ukv",
             "w_branch_a", "w_branch_b", "w_out", "final_norm_gain"]
    weights = dict(w_ada=w_ada, b_ada=b_ada, norm_gain=norm_gain, w_in=w_in, q_norm_gain=q_norm_gain, w_uq=w_uq,
                   kv_norm_gain=kv_norm_gain, w_ukv=w_ukv, w_branch_a=w_branch_a, w_branch_b=w_branch_b, w_out=w_out,
                   final_norm_gain=final_norm_gain)
    moms = dict(w_ada=m_w_ada, b_ada=m_b_ada, norm_gain=m_norm_gain, w_in=m_w_in, q_norm_gain=m_q_norm_gain,
                w_uq=m_w_uq, kv_norm_gain=m_kv_norm_gain, w_ukv=m_w_ukv, w_branch_a=m_w_branch_a,
                w_branch_b=m_w_branch_b, w_out=m_w_out, final_norm_gain=m_final_norm_gain)
    vels = dict(w_ada=v_w_ada, b_ada=v_b_ada, norm_gain=v_norm_gain, w_in=v_w_in, q_norm_gain=v_q_norm_gain,
                w_uq=v_w_uq, kv_norm_gain=v_kv_norm_gain, w_ukv=v_w_ukv, w_branch_a=v_w_branch_a,
                w_branch_b=v_w_branch_b, w_out=v_w_out, final_norm_gain=v_final_norm_gain)
    grads2d = dict(w_ada=g_w_ada, b_ada=g_b_ada, norm_gain=g_g1, w_in=g_in_sum_t, q_norm_gain=g_gq, w_uq=g_big[0],
                   kv_norm_gain=g_gkv, w_ukv=g_big[1], w_branch_a=g_big[2], w_branch_b=g_big[3], w_out=g_big[4],
                   final_norm_gain=g_gf)

    grads, deltas, new_m, new_v = [], [], [], []
    for n in names:
        w = weights[n]
        if n == "w_in":
            to2d = lambda t: t[0].T if t.ndim == 3 else t
            back = lambda t: t.T[None]
        else:
            shape2d = grads2d[n].shape
            to2d = lambda t, s=shape2d: t.reshape(s)
            back = lambda t, s=w.shape: t.reshape(s)
        g2d = to2d(grads2d[n])
        d_, m_, v_ = _adamw_call(to2d(w), g2d, to2d(moms[n]), to2d(vels[n]), "adamw_" + n)
        grads.append(back(g2d))
        deltas.append(back(d_))
        new_m.append(back(m_))
        new_v.append(back(v_))

    return (loss, grad_x2d.reshape(x.shape), *grads, *deltas, *new_m, *new_v)
```

```python
import functools
import math

import jax
import jax.numpy as jnp
from jax import lax
from jax.experimental import pallas as pl
from jax.experimental.pallas import tpu as pltpu

F32 = jnp.float32
BF16 = jnp.bfloat16
MXU_DTYPE = jnp.bfloat16

N_DEV = 8
D = 1024
HEADS = 8
HD = 64
SB_W = 512
MLA_W = 512
Q_RANK = 384
KV_RANK = 256
ROPE = 32
NOPE = 64
QK_DIM = NOPE + ROPE
EPS = 1e-6
ROPE_BASE = 10000.0

ADAM_LR = 0.001
ADAM_B1 = 0.9
ADAM_B2 = 0.999
ADAM_EPS = 1e-08
ADAM_WD = 0.01
ADAM_STEP = 10

LANES = 128
VMEM_LIMIT = 48 * 1024 * 1024

O_GA, O_GB = 0, 1024
O_SBQ, O_SBK, O_SBV, O_SBZ = 2048, 2560, 3072, 3584
O_MLAZ = 4096
O_CQ = 4608
O_CKV = 5120
O_KROT = 5376
IN_PAD = 5632

BQ = 256
BK = 256


def _cparams(*sem):
    return pltpu.CompilerParams(dimension_semantics=sem, vmem_limit_bytes=VMEM_LIMIT)


def _tile_of(n, cap=512):
    if n <= cap:
        return n
    for t in (1024, 768, 512, 384, 256, 128):
        if t <= cap and n % t == 0:
            return t
    raise ValueError(n)


def _rowwise(fn, rows, vecs, outs, reds=(), *, name, tile=512):
    norm = []
    for r in rows:
        if isinstance(r, tuple):
            arr, cb, w = r[:3]
            ro = r[3] if len(r) > 3 else 0
        else:
            arr, cb, w, ro = r, 0, r.shape[1], 0
        norm.append((arr, cb, w, ro))
    s_len = norm[0][0].shape[0]
    tile = min(tile, s_len)
    assert s_len % tile == 0
    n_row, n_vec, n_out, n_red = len(norm), len(vecs), len(outs), len(reds)

    def body(*refs):
        step = pl.program_id(0)
        row_refs = refs[:n_row]
        vec_refs = refs[n_row:n_row + n_vec]
        out_refs = refs[n_row + n_vec:n_row + n_vec + n_out]
        red_refs = refs[n_row + n_vec + n_out:]
        row_res, red_res = fn(*[r[...] for r in row_refs], *[v[...] for v in vec_refs])
        for o, val in zip(out_refs, row_res):
            o[...] = val.astype(o.dtype)
        if n_red:
            @pl.when(step == 0)
            def _():
                for r in red_refs:
                    r[...] = jnp.zeros(r.shape, r.dtype)
            for r, val in zip(red_refs, red_res):
                r[...] += val

    in_specs = []
    for arr, cb, w, ro in norm:
        in_specs.append(pl.BlockSpec((tile, w), functools.partial(lambda i, cb, rb: (i + rb, cb), cb=cb, rb=ro // tile)))
        assert ro % tile == 0
    for v in vecs:
        in_specs.append(pl.BlockSpec(v.shape, lambda i: (0, 0)))
    out_shape = [jax.ShapeDtypeStruct((s_len, w), dt) for w, dt in outs]
    out_specs = [pl.BlockSpec((tile, w), lambda i: (i, 0)) for w, _ in outs]
    out_shape += [jax.ShapeDtypeStruct((1, w), F32) for w in reds]
    out_specs += [pl.BlockSpec((1, w), lambda i: (0, 0)) for w in reds]
    res = pl.pallas_call(
        body, name=name, grid=(s_len // tile,), in_specs=in_specs, out_specs=out_specs, out_shape=out_shape,
        compiler_params=_cparams("arbitrary" if n_red else "parallel"),
    )(*[a for a, _, _, _ in norm], *vecs)
    return res


def _mm(a, b, *, ta=False, tb=False, out_dtype=F32, name, exchange=None, tiles=None, epilogue=None):
    m, k = (a.shape[1], a.shape[0]) if ta else a.shape
    n = b.shape[0] if tb else b.shape[1]
    assert (b.shape[1] if tb else b.shape[0]) == k
    tm, tn, tk = tiles or (_tile_of(m, 1024), _tile_of(n, 1024 if n <= 1024 else 512), _tile_of(k, 1024))
    assert m % tm == 0 and n % tn == 0 and k % tk == 0
    ni, nj, nk = m // tm, n // tn, k // tk
    dims = (((0 if ta else 1,), (1 if tb else 0,)), ((), ()))
    n_ex = len(exchange.arrays) if exchange else 0
    fn, rows, vecs, outs, reds = epilogue or (None, (), (), (), ())
    rows = [r if isinstance(r, tuple) else (r, 0, r.shape[1]) for r in rows]
    assert not epilogue or tn == n
    n_res = len(outs) + len(reds) if epilogue else 1

    def body(*refs):
        a_ref, b_ref = refs[:2]
        row_refs, refs = refs[2:2 + len(rows)], refs[2 + len(rows):]
        vec_refs, refs = refs[:len(vecs)], refs[len(vecs):]
        x_refs, refs = refs[:n_ex], refs[n_ex:]
        res_refs, refs = refs[:n_res], refs[n_res:]
        out_refs, acc_ref, sems = refs[:n_ex], refs[n_ex], refs[n_ex + 1:]
        i, j, kk = pl.program_id(0), pl.program_id(1), pl.program_id(2)
        first = (i == 0) & (j == 0) & (kk == 0)

        if exchange:
            @pl.when(first)
            def _():
                exchange.start(x_refs, out_refs, sems)

        if reds:
            @pl.when(first)
            def _():
                for r in res_refs[len(outs):]:
                    r[...] = jnp.zeros(r.shape, r.dtype)

        @pl.when(kk == 0)
        def _():
            acc_ref[...] = jnp.zeros(acc_ref.shape, F32)

        acc_ref[...] += lax.dot_general(a_ref[...].astype(MXU_DTYPE), b_ref[...].astype(MXU_DTYPE), dims,
                                        preferred_element_type=F32)

        @pl.when(kk == nk - 1)
        def _():
            if not epilogue:
                res_refs[0][...] = acc_ref[...].astype(res_refs[0].dtype)
                return
            row_res, red_res = fn(acc_ref[...], *[r[...] for r in row_refs], *[v[...] for v in vec_refs])
            for o, val in zip(res_refs, row_res):
                o[...] = val.astype(o.dtype)
            for r, val in zip(res_refs[len(outs):], red_res):
                r[...] += val

        if exchange:
            @pl.when((i == ni - 1) & (j == nj - 1) & (kk == nk - 1))
            def _():
                exchange.wait(x_refs, out_refs, sems)

    a_spec = pl.BlockSpec((tk, tm), lambda i, j, kk: (kk, i)) if ta else pl.BlockSpec((tm, tk), lambda i, j, kk: (i, kk))
    b_spec = pl.BlockSpec((tn, tk), lambda i, j, kk: (j, kk)) if tb else pl.BlockSpec((tk, tn), lambda i, j, kk: (kk, j))
    in_specs = [a_spec, b_spec]
    in_specs += [pl.BlockSpec((tm, w), functools.partial(lambda i, j, kk, cb: (i, cb), cb=cb)) for _, cb, w in rows]
    in_specs += [pl.BlockSpec(v.shape, lambda i, j, kk: (0, 0)) for v in vecs]
    if epilogue:
        res_specs = [pl.BlockSpec((tm, w), lambda i, j, kk: (i, 0)) for w, _ in outs]
        res_specs += [pl.BlockSpec((1, w), lambda i, j, kk: (0, 0)) for w in reds]
        res_shape = [jax.ShapeDtypeStruct((m, w), dt) for w, dt in outs] + [jax.ShapeDtypeStruct((1, w), F32) for w in reds]
    else:
        res_specs = [pl.BlockSpec((tm, tn), lambda i, j, kk: (i, j))]
        res_shape = [jax.ShapeDtypeStruct((m, n), out_dtype)]
    ordered = bool(exchange or reds)
    res = pl.pallas_call(
        body, name=name, grid=(ni, nj, nk),
        in_specs=in_specs + (exchange.in_specs if exchange else []),
        out_specs=res_specs + (exchange.out_specs if exchange else []),
        out_shape=res_shape + (exchange.out_shape if exchange else []),
        scratch_shapes=[pltpu.VMEM((tm, tn), F32)] + (exchange.scratch if exchange else []),
        compiler_params=_cparams(*(("arbitrary",) * 3 if ordered else ("parallel", "parallel", "arbitrary"))),
    )(a, b, *[r[0] for r in rows], *vecs, *(exchange.arrays if exchange else []))
    main = res[:n_res] if epilogue else res[0]
    return (main, res[n_res:]) if exchange else main


_NT = (((1,), (1,)), ((), ()))
_TN = (((0,), (0,)), ((), ()))


def _dot(a, b):
    return jnp.dot(a, b, preferred_element_type=F32)


def _dot_nt(a, b):
    return lax.dot_general(a, b, _NT, preferred_element_type=F32)


def _dot_tn(a, b):
    return lax.dot_general(a, b, _TN, preferred_element_type=F32)


def _running_sum(x, tri):
    return _dot(x.astype(MXU_DTYPE), tri)


def _neg_softplus(z):
    u = jnp.exp2(jnp.abs(z) * (-1.0 / math.log(2.0)))
    return -jnp.maximum(z, 0.0) - jnp.log(1.0 + u)


def _walk_blocks(step, st, n, descending, group=2):
    done = 0
    size = group
    while size >= 1:
        def trip(t, s, size=size, done=done):
            js = [done + size * t + g for g in range(size)]
            return step([n - 1 - j for j in js] if descending else js, s)

        trips = (n - done) // size
        st = lax.fori_loop(0, trips, trip, st)
        done = done + size * trips
        size //= 2
    return st


def _chains(js):
    return [(h, t) for t in range(len(js)) for h in range(2)]


def _rowsum(x):
    return jnp.sum(x, axis=1, keepdims=True)


def _attn_consts():
    row = lax.broadcasted_iota(jnp.int32, (BQ, BK), 0)
    col = lax.broadcasted_iota(jnp.int32, (BQ, BK), 1)
    trow = lax.broadcasted_iota(jnp.int32, (BK, BK), 0)
    tcol = lax.broadcasted_iota(jnp.int32, (BK, BK), 1)
    lane = lax.broadcasted_iota(jnp.int32, (BQ, LANES), 1)
    klane = lax.broadcasted_iota(jnp.int32, (BK, LANES), 1)
    return row, col, trow, tcol, lane, klane


assert BQ == BK


def _diag_step(jd, descending, diag_mask, group=4):
    below = list(range(group)) if descending else list(reversed(range(group)))
    js = [jnp.maximum(jd - o, 0) for o in below]
    masks = [diag_mask if o == 0 else jd - o >= 0 for o in below]
    return js, masks, jnp.maximum(jd - (group - 1), 0)


def _key_slice(j):
    return pl.ds(pl.multiple_of(j * BK, BK), BK)


def _sb_forward(proj, s_len, exchange):
    nq = s_len // BQ
    assert s_len // BK <= HD
    qc, kc, vc = O_SBQ // LANES, O_SBK // LANES, O_SBV // LANES
    n_ex = len(exchange.arrays)

    def body(q_ref, k_ref, v_ref, *rest):
        x_refs, (o_ref, tails_ref) = rest[:n_ex], rest[n_ex:n_ex + 2]
        out_refs, sems = rest[n_ex + 2:2 * n_ex + 2], rest[2 * n_ex + 2:]
        p = pl.program_id(0)
        i = pl.program_id(1)
        jd = i

        @pl.when((p == 0) & (i == 0))
        def _():
            exchange.start(x_refs, out_refs, sems)

        row, col, trow, tcol, lane, klane = _attn_consts()
        tri = (trow >= tcol).astype(MXU_DTYPE)
        q = q_ref[...] * 0.125
        qh = [jnp.where(lane < HD, q, 0.0).astype(MXU_DTYPE), jnp.where(lane >= HD, q, 0.0).astype(MXU_DTYPE)]

        km = [klane < HD, klane >= HD]

        def step(js, st, masks):
            carry, acc, tail = st
            chains = _chains(js)
            kj = [k_ref[_key_slice(j), :].astype(MXU_DTYPE) for j in js]
            vj = [v_ref[_key_slice(j), :].astype(MXU_DTYPE) for j in js]
            z = {(h, t): _dot_nt(qh[h], kj[t]) for h, t in chains}
            run = list(carry)
            suf, carry_in = {}, {}
            for h, t in chains:
                lom = _neg_softplus(z[h, t])
                if masks:
                    lom = jnp.where(masks[t], lom, 0.0)
                suf[h, t] = _running_sum(lom, tri)
                carry_in[h, t] = run[h]
                run[h] = run[h] + _rowsum(lom)
            for h, t in chains:
                a = jnp.exp(z[h, t] + suf[h, t] + carry_in[h, t])
                if masks:
                    a = jnp.where(masks[t], a, 0.0)
                acc = acc + _dot(a.astype(MXU_DTYPE), jnp.where(km[h], vj[t], 0))
                tail_lane = js[t] if not masks or masks[t].ndim else jnp.where(masks[t], js[t], -LANES)
                tail = jnp.where(lane == h * HD + tail_lane, carry_in[h, t], tail)
            return tuple(run), acc, tail

        zero = jnp.zeros((BQ, LANES), F32)
        diag_js, masks, left = _diag_step(jd, True, col < row, group=2)
        st = step(diag_js, ((jnp.zeros((BQ, 1), F32),) * 2, zero, zero), masks)
        st = _walk_blocks(lambda js, s: step(js, s, None), st, left, True, group=8)
        o_ref[...] = st[1]
        tails_ref[...] = st[2]

        @pl.when((p == HEADS // 2 - 1) & (i == nq - 1))
        def _():
            exchange.wait(x_refs, out_refs, sems)

    blk = pl.BlockSpec((BQ, LANES), lambda p, i: (i, p))
    out = jax.ShapeDtypeStruct((s_len, SB_W), F32)
    res = pl.pallas_call(
        body, name="sb_fwd", grid=(HEADS // 2, nq),
        in_specs=[pl.BlockSpec((BQ, LANES), lambda p, i: (i, qc + p)),
                  pl.BlockSpec((s_len, LANES), lambda p, i: (0, kc + p)),
                  pl.BlockSpec((s_len, LANES), lambda p, i: (0, vc + p))] + exchange.in_specs,
        out_specs=[blk, blk] + exchange.out_specs, out_shape=[out, out] + exchange.out_shape,
        scratch_shapes=exchange.scratch,
        compiler_params=_cparams("arbitrary", "arbitrary"),
    )(proj, proj, proj, *exchange.arrays)
    return res[:2], res[2:]


def _sb_backward(proj, tails, do, s_len, exchange):
    nq = s_len // BQ
    qc, kc, vc = O_SBQ // LANES, O_SBK // LANES, O_SBV // LANES
    n_ex = len(exchange.arrays)

    def body(q_ref, k_ref, v_ref, tails_ref, do_ref, *rest):
        x_refs, (dq_ref, dk_out, dv_out) = rest[:n_ex], rest[n_ex:n_ex + 3]
        out_refs, (dk_ref, dv_ref), sems = rest[n_ex + 3:2 * n_ex + 3], rest[2 * n_ex + 3:2 * n_ex + 5], rest[2 * n_ex + 5:]
        p = pl.program_id(0)
        i = pl.program_id(1)
        jd = i

        @pl.when((p == 0) & (i == 0))
        def _():
            exchange.start(x_refs, out_refs, sems)

        @pl.when(i == 0)
        def _():
            dk_ref[...] = jnp.zeros(dk_ref.shape, F32)
            dv_ref[...] = jnp.zeros(dv_ref.shape, F32)

        row, col, trow, tcol, lane, klane = _attn_consts()
        tri = (trow >= tcol).astype(MXU_DTYPE)
        tri_p = (trow <= tcol).astype(MXU_DTYPE)
        q = q_ref[...] * 0.125
        tails_blk = tails_ref[...]
        do_blk = do_ref[...]
        hm = [lane < HD, lane >= HD]
        km = [klane < HD, klane >= HD]
        qh = [jnp.where(m, q, 0.0).astype(MXU_DTYPE) for m in hm]
        doh = [jnp.where(m, do_blk, 0.0).astype(MXU_DTYPE) for m in hm]

        def step(js, st, masks):
            before, dq = st
            chains = _chains(js)
            kj = [k_ref[_key_slice(j), :].astype(MXU_DTYPE) for j in js]
            vj = [v_ref[_key_slice(j), :].astype(MXU_DTYPE) for j in js]
            z = {(h, t): _dot_nt(qh[h], kj[t]) for h, t in chains}
            da = {(h, t): _dot_nt(doh[h], vj[t]) for h, t in chains}
            suf, sig = {}, {}
            for h, t in chains:
                lom = _neg_softplus(z[h, t])
                if masks:
                    lom = jnp.where(masks[t], lom, 0.0)
                suf[h, t] = _running_sum(lom, tri)
                sig[h, t] = jnp.exp(z[h, t] + lom)
            run = list(before)
            dl, pre, before_in = {}, {}, {}
            dk_add, dv_add = [None] * len(js), [None] * len(js)
            for h, t in chains:
                tail = _rowsum(jnp.where(lane == h * HD + js[t], tails_blk, 0.0))
                a = jnp.exp(z[h, t] + suf[h, t] + tail)
                if masks:
                    a = jnp.where(masks[t], a, 0.0)
                dl[h, t] = da[h, t] * a
                pre[h, t] = _dot(dl[h, t].astype(MXU_DTYPE), tri_p)
                dv_h = _dot_tn(a.astype(MXU_DTYPE), doh[h])
                dv_add[t] = dv_h if dv_add[t] is None else dv_add[t] + dv_h
                before_in[h, t] = run[h]
                run[h] = run[h] + _rowsum(dl[h, t])
            for h, t in chains:
                upto = before_in[h, t] + pre[h, t]
                dz = dl[h, t] - sig[h, t] * upto
                if masks:
                    dz = jnp.where(masks[t], dz, 0.0)
                dzb = dz.astype(MXU_DTYPE)
                dq = dq + _dot(dzb, jnp.where(km[h], kj[t], 0))
                dk_h = _dot_tn(dzb, qh[h])
                dk_add[t] = dk_h if dk_add[t] is None else dk_add[t] + dk_h
            for t, j in enumerate(js):
                dk_ref[_key_slice(j), :] += dk_add[t]
                dv_ref[_key_slice(j), :] += dv_add[t]
            return tuple(run), dq

        st = ((jnp.zeros((BQ, 1), F32),) * 2, jnp.zeros((BQ, LANES), F32))
        diag_js, masks, left = _diag_step(jd, False, col < row, group=2)
        st = _walk_blocks(lambda js, s: step(js, s, None), st, left, False, group=8)
        st = step(diag_js, st, masks)
        dq_ref[...] = (st[1] * 0.125).astype(dq_ref.dtype)

        @pl.when(i == nq - 1)
        def _():
            dk_out[...] = dk_ref[...].astype(dk_out.dtype)
            dv_out[...] = dv_ref[...].astype(dv_out.dtype)

        @pl.when((p == HEADS // 2 - 1) & (i == nq - 1))
        def _():
            exchange.wait(x_refs, out_refs, sems)

    blk = pl.BlockSpec((BQ, LANES), lambda p, i: (i, p))
    full = pl.BlockSpec((s_len, LANES), lambda p, i: (0, p))
    out = jax.ShapeDtypeStruct((s_len, SB_W), BF16)
    res = pl.pallas_call(
        body, name="sb_bwd", grid=(HEADS // 2, nq),
        in_specs=[pl.BlockSpec((BQ, LANES), lambda p, i: (i, qc + p)),
                  pl.BlockSpec((s_len, LANES), lambda p, i: (0, kc + p)),
                  pl.BlockSpec((s_len, LANES), lambda p, i: (0, vc + p)),
                  blk, blk] + exchange.in_specs,
        out_specs=[blk, full, full] + exchange.out_specs, out_shape=[out, out, out] + exchange.out_shape,
        scratch_shapes=[pltpu.VMEM((s_len, LANES), F32)] * 2 + exchange.scratch,
        compiler_params=_cparams("arbitrary", "arbitrary"),
    )(proj, proj, proj, tails, do, *exchange.arrays)
    return res[:3], res[3:]


def _pair_mask(rows, h):
    lane = lax.broadcasted_iota(jnp.int32, (rows, 2 * LANES), 1)
    rot = lane - LANES
    return (((lane < LANES) & (lane // HD == h))
            | ((lane >= LANES) & (rot < 2 * ROPE) & ((rot // (ROPE // 2)) % 2 == h)))


def _mla_forward(q_cat, k_cat, kv, s_len):
    nq = s_len // BQ
    scale = 1.0 / math.sqrt(QK_DIM)

    def body(q_ref, k_ref, v_ref, o_ref, lse_ref):
        i = pl.program_id(1)
        jd = i
        row, col, trow, tcol, lane, klane = _attn_consts()
        q = q_ref[...]
        hm = [lane < HD, lane >= HD]
        km = [klane < HD, klane >= HD]
        qh = [jnp.where(_pair_mask(BQ, h), q, 0) for h in range(2)]

        def step(js, st, masks):
            m_run, l_run, acc = st
            chains = _chains(js)
            kj = [k_ref[_key_slice(j), :] for j in js]
            vj = [v_ref[_key_slice(j), :].astype(MXU_DTYPE) for j in js]
            s = {}
            for h, t in chains:
                s[h, t] = _dot_nt(qh[h], kj[t]) * scale
                if masks:
                    s[h, t] = jnp.where(masks[t], s[h, t], -jnp.inf)
            m_new, alpha, l_new = [], [], []
            for h in range(2):
                top = m_run[h]
                for t in range(len(js)):
                    top = jnp.maximum(top, jnp.max(s[h, t], axis=1, keepdims=True))
                m_new.append(top)
                alpha.append(jnp.exp(m_run[h] - top))
                l_new.append(alpha[h] * l_run[h])
            add = None
            for h, t in chains:
                pr = jnp.exp(s[h, t] - m_new[h])
                l_new[h] = l_new[h] + _rowsum(pr)
                part = _dot(pr.astype(MXU_DTYPE), jnp.where(km[h], vj[t], 0))
                add = part if add is None else add + part
            acc = jnp.where(hm[0], alpha[0], alpha[1]) * acc + add
            return tuple(m_new), tuple(l_new), acc

        st = ((jnp.full((BQ, 1), -1e30, F32),) * 2, (jnp.zeros((BQ, 1), F32),) * 2, jnp.zeros((BQ, LANES), F32))
        diag_js, masks, left = _diag_step(jd, True, col <= row)
        st = step(diag_js, st, masks)
        m_run, l_run, acc = _walk_blocks(lambda js, s: step(js, s, None), st, left, True, group=8)
        o_ref[...] = acc / jnp.where(hm[0], l_run[0], l_run[1])
        lse_ref[...] = jnp.where(hm[0], m_run[0] + jnp.log(l_run[0]), m_run[1] + jnp.log(l_run[1]))

    blk = pl.BlockSpec((BQ, LANES), lambda p, i: (i, p))
    out = jax.ShapeDtypeStruct((s_len, MLA_W), F32)
    return pl.pallas_call(
        body, name="mla_fwd", grid=(HEADS // 2, nq),
        in_specs=[pl.BlockSpec((BQ, 2 * LANES), lambda p, i: (i, p)),
                  pl.BlockSpec((s_len, 2 * LANES), lambda p, i: (0, p)),
                  pl.BlockSpec((s_len, LANES), lambda p, i: (0, MLA_W // LANES + p))],
        out_specs=[blk, blk], out_shape=[out, out],
        compiler_params=_cparams("parallel", "parallel"),
    )(q_cat, k_cat, kv)


def _mla_backward(q_cat, k_cat, kv, o, lse, do, s_len):
    nq = s_len // BQ
    scale = 1.0 / math.sqrt(QK_DIM)

    def body(q_ref, k_ref, v_ref, o_ref, lse_ref, do_ref, dq_ref, dk_ref, dv_ref):
        i = pl.program_id(1)

        @pl.when(i == 0)
        def _():
            dk_ref[...] = jnp.zeros(dk_ref.shape, F32)
            dv_ref[...] = jnp.zeros(dv_ref.shape, F32)

        jd = i
        row, col, trow, tcol, lane, klane = _attn_consts()
        q = q_ref[...]
        o_blk = o_ref[...]
        do_blk = do_ref[...]
        lse_blk = lse_ref[...]
        hm = [lane < HD, lane >= HD]
        kpm = [_pair_mask(BK, h) for h in range(2)]
        qh = [jnp.where(_pair_mask(BQ, h), q, 0) for h in range(2)]
        doh_f = [jnp.where(m, do_blk, 0.0) for m in hm]
        doh = [d.astype(MXU_DTYPE) for d in doh_f]
        delta = [jnp.sum(d * o_blk, axis=1, keepdims=True) for d in doh_f]
        lse_h = [jnp.sum(jnp.where(lane == h * HD, lse_blk, 0.0), axis=1, keepdims=True) for h in range(2)]

        def step(js, st, masks):
            dq = st
            chains = _chains(js)
            kj = [k_ref[_key_slice(j), :] for j in js]
            vj = [v_ref[_key_slice(j), :].astype(MXU_DTYPE) for j in js]
            s = {(h, t): _dot_nt(qh[h], kj[t]) for h, t in chains}
            dp = {(h, t): _dot_nt(doh[h], vj[t]) for h, t in chains}
            adds = [[None] * len(js) for _ in range(2)]

            def accumulate(slot, t, part):
                adds[slot][t] = part if adds[slot][t] is None else adds[slot][t] + part

            for h, t in chains:
                pr = jnp.exp(s[h, t] * scale - lse_h[h])
                if masks:
                    pr = jnp.where(masks[t], pr, 0.0)
                dsb = (pr * (dp[h, t] - delta[h]) * scale).astype(MXU_DTYPE)
                dq = dq + _dot(dsb, jnp.where(kpm[h], kj[t], 0))
                accumulate(0, t, _dot_tn(dsb, qh[h]))
                accumulate(1, t, _dot_tn(pr.astype(MXU_DTYPE), doh[h]))
            for t, j in enumerate(js):
                dk_ref[_key_slice(j), :] += adds[0][t]
                dv_ref[_key_slice(j), :] += adds[1][t]
            return dq

        diag_js, masks, left = _diag_step(jd, True, col <= row)
        st = step(diag_js, jnp.zeros((BQ, 2 * LANES), F32), masks)
        dq_ref[...] = _walk_blocks(lambda js, s: step(js, s, None), st, left, True, group=8)

    blk = pl.BlockSpec((BQ, LANES), lambda p, i: (i, p))
    full = pl.BlockSpec((s_len, LANES), lambda p, i: (0, p))
    out = jax.ShapeDtypeStruct((s_len, MLA_W), F32)
    out_cat = jax.ShapeDtypeStruct((s_len, 2 * MLA_W), F32)
    return pl.pallas_call(
        body, name="mla_bwd", grid=(HEADS // 2, nq),
        in_specs=[pl.BlockSpec((BQ, 2 * LANES), lambda p, i: (i, p)),
                  pl.BlockSpec((s_len, 2 * LANES), lambda p, i: (0, p)),
                  pl.BlockSpec((s_len, LANES), lambda p, i: (0, MLA_W // LANES + p)),
                  blk, blk, blk],
        out_specs=[pl.BlockSpec((BQ, 2 * LANES), lambda p, i: (i, p)),
                   pl.BlockSpec((s_len, 2 * LANES), lambda p, i: (0, p)), full],
        out_shape=[out_cat, out_cat, out],
        compiler_params=_cparams("arbitrary", "arbitrary"),
    )(q_cat, k_cat, kv, o, lse, do)


def _mesh_pos():
    return lax.axis_index("x"), lax.axis_index("y"), lax.axis_index("c")


def _dev_index(px, py, pc):
    return 4 * px + 2 * py + pc


def _all_gather(block, name):
    return _all_gather_parts([block], name)[0]


def _all_gather_parts(blocks, name):
    n = len(blocks)

    def body(*refs):
        x_refs, out_refs = refs[:n], refs[n:2 * n]
        send_sems, recv_sems, local_sems = refs[2 * n:]
        x, y, c = _mesh_pos()
        me, sibling = (x, y, c), (x, y, 1 - c)
        chips = [(1 - x, y), (x, 1 - y), (1 - x, 1 - y)]

        def copy(a, k, blockpos, to, src=None):
            slot = out_refs[a].at[_dev_index(*blockpos)]
            return pltpu.make_async_remote_copy(
                src_ref=slot if src is None else src, dst_ref=slot,
                send_sem=send_sems.at[7 * a + k], recv_sem=recv_sems.at[7 * a + k],
                device_id=to, device_id_type=pl.DeviceIdType.MESH)

        mine = [pltpu.make_async_copy(x_refs[a], out_refs[a].at[_dev_index(*me)], local_sems.at[a]) for a in range(n)]
        for cp in mine:
            cp.start()
        first = []
        for a in range(n):
            first.append(copy(a, 0, me, sibling, src=x_refs[a]))
            first += [copy(a, 1 + j, me, (*chip, c), src=x_refs[a]) for j, chip in enumerate(chips)]
        for cp in first:
            cp.start()
        passed = []
        for j, chip in enumerate(chips):
            for a in range(n):
                copy(a, 1 + j, (*chip, c), me).wait_recv()
                passed.append(copy(a, 4 + j, (*chip, c), sibling))
                passed[-1].start()
        for a in range(n):
            copy(a, 0, sibling, me).wait_recv()
            for j, chip in enumerate(chips):
                copy(a, 4 + j, (*chip, 1 - c), me).wait_recv()
        for cp in first + passed:
            cp.wait_send()
        for cp in mine:
            cp.wait()

    return pl.pallas_call(
        body, name=name,
        out_shape=[jax.ShapeDtypeStruct((N_DEV,) + b.shape, b.dtype) for b in blocks],
        in_specs=[pl.BlockSpec(memory_space=pl.ANY)] * n, out_specs=[pl.BlockSpec(memory_space=pl.ANY)] * n,
        scratch_shapes=[pltpu.SemaphoreType.DMA((7 * n,)), pltpu.SemaphoreType.DMA((7 * n,)),
                        pltpu.SemaphoreType.DMA((n,))],
    )(*blocks)


class _Exchange:
    def __init__(self, arrays):
        self.arrays = list(arrays)
        n = len(self.arrays)
        self.in_specs = [pl.BlockSpec(memory_space=pl.ANY)] * n
        self.out_specs = [pl.BlockSpec(memory_space=pl.ANY)] * n
        self.out_shape = [jax.ShapeDtypeStruct(a.shape, a.dtype) for a in self.arrays]
        self.scratch = [pltpu.SemaphoreType.DMA((7 * n,)), pltpu.SemaphoreType.DMA((7 * n,)),
                        pltpu.SemaphoreType.DMA((n,))]

    def _copies(self, x_refs, out_refs, sems, with_arrivals):
        send_sems, recv_sems, local_sems = sems
        x, y, c = _mesh_pos()
        me = _dev_index(x, y, c)
        flips = [(fx, fy, fc) for fx in (0, 1) for fy in (0, 1) for fc in (0, 1)][1:]
        peers = [(1 - x if fx else x, 1 - y if fy else y, 1 - c if fc else c) for fx, fy, fc in flips]
        mine, sends, arrivals = [], [], []
        for a in range(len(self.arrays)):
            mine.append(pltpu.make_async_copy(x_refs[a].at[me], out_refs[a].at[me], local_sems.at[a]))
            for k, peer in enumerate(peers):
                sends.append(pltpu.make_async_remote_copy(
                    src_ref=x_refs[a].at[_dev_index(*peer)], dst_ref=out_refs[a].at[me],
                    send_sem=send_sems.at[7 * a + k], recv_sem=recv_sems.at[7 * a + k],
                    device_id=peer, device_id_type=pl.DeviceIdType.MESH))
                if not with_arrivals:
                    continue
                arrivals.append(pltpu.make_async_remote_copy(
                    src_ref=x_refs[a].at[me], dst_ref=out_refs[a].at[_dev_index(*peer)],
                    send_sem=send_sems.at[7 * a + k], recv_sem=recv_sems.at[7 * a + k],
                    device_id=peer, device_id_type=pl.DeviceIdType.MESH))
        return mine, sends, arrivals

    def start(self, x_refs, out_refs, sems):
        mine, sends, _ = self._copies(x_refs, out_refs, sems, False)
        for cp in mine + sends:
            cp.start()

    def wait(self, x_refs, out_refs, sems):
        mine, sends, arrivals = self._copies(x_refs, out_refs, sems, True)
        for cp in arrivals:
            cp.wait_recv()
        for cp in sends:
            cp.wait_send()
        for cp in mine:
            cp.wait()


def _sum_blocks(parts, name):
    n, r, c = parts.shape
    row_tiles = [t for t in range(16, min(r, 2048) + 1, 16) if r % t == 0]
    if row_tiles:
        tr, tc = max(row_tiles), c
    else:
        tr, tc = r, 2 * LANES
    assert c % tc == 0

    def body(p_ref, o_ref):
        acc = p_ref[0].astype(F32)
        for s in range(1, n):
            acc = acc + p_ref[s].astype(F32)
        o_ref[...] = acc

    return pl.pallas_call(
        body, name=name, grid=(r // tr, c // tc),
        in_specs=[pl.BlockSpec((n, tr, tc), lambda i, j: (0, i, j))],
        out_specs=pl.BlockSpec((tr, tc), lambda i, j: (i, j)),
        out_shape=jax.ShapeDtypeStruct((r, c), F32),
        compiler_params=_cparams("parallel", "parallel"),
    )(parts)


def _sigmoid(x):
    return 1.0 / (1.0 + jnp.exp(-x))


def _silu(x):
    return x * _sigmoid(x)


def _silu_grad(x):
    s = _sigmoid(x)
    return s * (1.0 + x * (1.0 - s))


def _colsum(x):
    return jnp.sum(x, axis=0, keepdims=True)


def _rms(x):
    return lax.rsqrt(jnp.mean(x * x, axis=-1, keepdims=True) + EPS)


def _rms_bwd(xn, r, dxn):
    return r * (dxn - xn * jnp.mean(dxn * xn, axis=-1, keepdims=True))


def _adamw(w, g, m, v):
    m = ADAM_B1 * m + (1.0 - ADAM_B1) * g
    v = ADAM_B2 * v + (1.0 - ADAM_B2) * jnp.square(g)
    m_hat = m / (1.0 - ADAM_B1 ** ADAM_STEP)
    v_hat = v / (1.0 - ADAM_B2 ** ADAM_STEP)
    delta = -ADAM_LR * (m_hat / (jnp.sqrt(v_hat) + ADAM_EPS) + ADAM_WD * w)
    return delta, m, v


def _adamw_call(w, g, m, v, name):
    r, c = w.shape
    if r % 256 == 0:
        tr, tc = 256, c
    elif r * c <= 256 * 1024 or c % (2 * LANES):
        tr, tc = r, c
    else:
        tr, tc = r, 2 * LANES

    def body(w_ref, g_ref, m_ref, v_ref, d_out, m_out, v_out):
        d_out[...], m_out[...], v_out[...] = _adamw(w_ref[...], g_ref[...], m_ref[...], v_ref[...])

    spec = pl.BlockSpec((tr, tc), lambda i, j: (i, j))
    return pl.pallas_call(
        body, name=name, grid=(r // tr, c // tc), in_specs=[spec] * 4, out_specs=[spec] * 3,
        out_shape=[jax.ShapeDtypeStruct((r, c), F32)] * 3, compiler_params=_cparams("parallel", "parallel"),
    )(w, g, m, v)


def _uq_to_kernel_layout(w):
    lead = w.shape[:-1]
    t = w.reshape(lead + (HEADS, QK_DIM))
    return jnp.concatenate([t[..., :NOPE].reshape(lead + (HEADS * NOPE,)),
                            t[..., NOPE:NOPE + ROPE // 2].reshape(lead + (LANES,)),
                            t[..., NOPE + ROPE // 2:].reshape(lead + (LANES,))], axis=-1)


def _uq_from_kernel_layout(w):
    lead = w.shape[:-1]
    nope = w[..., :HEADS * NOPE].reshape(lead + (HEADS, NOPE))
    r1 = w[..., HEADS * NOPE:HEADS * NOPE + LANES].reshape(lead + (HEADS, ROPE // 2))
    r2 = w[..., HEADS * NOPE + LANES:].reshape(lead + (HEADS, ROPE // 2))
    return jnp.concatenate([nope, r1, r2], axis=-1).reshape(lead + (HEADS * QK_DIM,))


def _ukv_to_kernel_layout(w):
    lead = w.shape[:-1]
    t = w.reshape(lead + (HEADS, NOPE + HD))
    return jnp.concatenate([t[..., :NOPE].reshape(lead + (HEADS * NOPE,)),
                            t[..., NOPE:].reshape(lead + (HEADS * HD,))], axis=-1)


def _ukv_from_kernel_layout(w):
    lead = w.shape[:-1]
    kn = w[..., :HEADS * NOPE].reshape(lead + (HEADS, NOPE))
    vv = w[..., HEADS * NOPE:].reshape(lead + (HEADS, HD))
    return jnp.concatenate([kn, vv], axis=-1).reshape(lead + (HEADS * (NOPE + HD),))


def _w_in_t_to_kernel_layout(wt):
    sb = wt[0:2048]
    c_q = wt[2048:2432]
    c_kv = wt[2432:2688]
    k_rot = wt[2688:2720]
    mla_z = wt[2720:3232]
    gates = wt[3232:5280]
    zeros = jnp.zeros((LANES, wt.shape[1]), wt.dtype)
    k1 = jnp.tile(k_rot[:ROPE // 2], (HEADS, 1))
    k2 = jnp.tile(k_rot[ROPE // 2:], (HEADS, 1))
    return jnp.concatenate([gates, sb, mla_z, c_q, zeros, c_kv, k1, k2], axis=0)


def _w_in_t_from_kernel_layout(gt, g_rot):
    return jnp.concatenate([gt[O_SBQ:O_SBQ + 2048], gt[O_CQ:O_CQ + Q_RANK], gt[O_CKV:O_CKV + KV_RANK],
                            g_rot.astype(gt.dtype), gt[O_MLAZ:O_MLAZ + MLA_W], gt[O_GA:O_GA + 2 * D]], axis=0)


def kernel(x, c, positions, w_ada, b_ada, norm_gain, w_in, q_norm_gain, w_uq, kv_norm_gain, w_ukv, w_branch_a, w_branch_b, w_out, final_norm_gain, loss_target, m_w_ada, m_b_ada, m_norm_gain, m_w_in, m_q_norm_gain, m_w_uq, m_kv_norm_gain, m_w_ukv, m_w_branch_a, m_w_branch_b, m_w_out, m_final_norm_gain, v_w_ada, v_b_ada, v_norm_gain, v_w_in, v_q_norm_gain, v_w_uq, v_kv_norm_gain, v_w_ukv, v_w_branch_a, v_w_branch_b, v_w_out, v_final_norm_gain):
    s_len = x.shape[1]
    me = _dev_index(*_mesh_pos())
    x2d = x[0]
    tgt = loss_target[0]

    w_in_t = w_in[0].T.astype(BF16)
    big = [w_uq[0], w_ukv[0], w_branch_a[0], w_branch_b[0], w_out[0]]
    big_sizes = [int(w.size) for w in big]
    packed = jnp.concatenate([w.astype(BF16).reshape(-1, LANES) for w in big], axis=0)
    g_in_t, c_all = _all_gather_parts([w_in_t, c.reshape(8, LANES)], "gather_w_in")
    c_all = c_all.reshape(N_DEV, D)
    w_in_kt = _w_in_t_to_kernel_layout(g_in_t.reshape(N_DEV * w_in_t.shape[0], D))

    mod_cols = _mm(c_all, w_ada[0], name="ada_mod")
    mod_all = _all_gather(mod_cols, "gather_mod")
    mod = lax.dynamic_index_in_dim(mod_all, me, axis=1, keepdims=False).reshape(1, 3 * D)
    mod_shift, mod_scale, mod_gate = mod[:, :D], mod[:, D:2 * D], mod[:, 2 * D:]
    b_shift, b_scale, b_gate = b_ada[:, :D], b_ada[:, D:2 * D], b_ada[:, 2 * D:]
    g1 = norm_gain
    gq, gkv = q_norm_gain, kv_norm_gain
    gf = final_norm_gain.reshape(1, D)

    def f_h(x_, g1_, ms, bs, msc, bsc):
        xn = x_ * _rms(x_)
        return (xn * g1_ * (1.0 + (msc + bsc)) + (ms + bs),), ()

    (h,) = _rowwise(f_h, [x2d], [g1, mod_shift, b_shift, mod_scale, b_scale], [(D, BF16)], name="ada_norm")
    proj = _mm(h, w_in_kt, tb=True, name="proj_in", tiles=(min(s_len, 1024), IN_PAD // 2, D))
    proj_at = dict(ga=(O_GA, D), gb=(O_GB, D), sbz=(O_SBZ, SB_W), mlaz=(O_MLAZ, MLA_W),
                   cq=(O_CQ, Q_RANK), ckv=(O_CKV, KV_RANK), krot=(O_KROT, LANES))

    def win(key, shift=0):
        offset, width = proj_at[key]
        assert offset % width == 0
        return proj, offset // width + shift, width

    (o_a, sb_tails), (gathered,) = _sb_forward(
        proj, s_len, _Exchange([jnp.broadcast_to(packed[None], (N_DEV,) + packed.shape)]))
    offs = [0]
    for n in big_sizes:
        offs.append(offs[-1] + n // LANES)

    def unpack(t, shape):
        return gathered[:, offs[t]:offs[t + 1], :].reshape((N_DEV,) + shape)

    def cols(t, shape):
        return unpack(t, shape).transpose(1, 0, 2).reshape(shape[0], N_DEV * shape[1])

    w_uq_k = _uq_to_kernel_layout(cols(0, big[0].shape))
    w_ukv_k = _ukv_to_kernel_layout(cols(1, big[1].shape))
    w_a_f = cols(2, big[2].shape)
    w_b_f = cols(3, big[3].shape)
    w_out_f = unpack(4, big[4].shape).reshape(D, D)

    def f_lat(cq, ckv, gq_, gkv_):
        return (cq * _rms(cq) * gq_, ckv * _rms(ckv) * gkv_), ()

    cq_n, ckv_n = _rowwise(f_lat, [win("cq"), win("ckv")], [gq, gkv],
                           [(Q_RANK, BF16), (KV_RANK, BF16)], name="latent_norm")
    q_mla = _mm(cq_n, w_uq_k, name="q_up")
    kv = _mm(ckv_n, w_ukv_k, out_dtype=BF16, name="kv_up")

    inv_freq = ROPE_BASE ** (-jnp.arange(0, ROPE, 2, dtype=F32) / ROPE)
    inv_freq_t = jnp.tile(inv_freq, HEADS).reshape(1, LANES)
    pos_col = positions.reshape(s_len, 1).astype(F32)

    pairs = HEADS // 2

    def f_rope(pos, qn, q1, q2, kn, k1, k2, freq):
        ang = pos * freq
        cs, sn = jnp.cos(ang), jnp.sin(ang)
        q1r, q2r = q1 * cs - q2 * sn, q1 * sn + q2 * cs
        k1r, k2r = k1 * cs - k2 * sn, k1 * sn + k2 * cs
        lane = lax.broadcasted_iota(jnp.int32, q1.shape, 1)
        first, second = lane < ROPE, (lane >= ROPE) & (lane < 2 * ROPE)
        k_rot = jnp.where(first, k1r, jnp.where(second, k2r, 0.0))
        q_parts, k_parts = [], []
        for p in range(pairs):
            q_rot = jnp.where(first, pltpu.roll(q1r, (LANES - ROPE * p) % LANES, 1),
                              jnp.where(second, pltpu.roll(q2r, (LANES + ROPE - ROPE * p) % LANES, 1), 0.0))
            q_parts += [qn[:, LANES * p:LANES * (p + 1)], q_rot]
            k_parts += [kn[:, LANES * p:LANES * (p + 1)], k_rot]
        return (jnp.concatenate(q_parts, axis=1), jnp.concatenate(k_parts, axis=1), cs, sn), ()

    q_cat, k_cat, cos_t, sin_t = _rowwise(
        f_rope, [pos_col, (q_mla, 0, MLA_W), (q_mla, 4, LANES), (q_mla, 5, LANES), (kv, 0, MLA_W),
                 win("krot"), win("krot", 1)], [inv_freq_t],
        [(2 * MLA_W, BF16), (2 * MLA_W, BF16), (LANES, F32), (LANES, F32)], name="rope")

    o_b, lse = _mla_forward(q_cat, k_cat, kv, s_len)

    def f_gate(oa, za, ob, zb):
        return (oa * _silu(za), ob * _silu(zb)), ()

    ya_in, yb_in = _rowwise(f_gate, [o_a, win("sbz"), o_b, win("mlaz")], [],
                            [(SB_W, BF16), (MLA_W, BF16)], name="branch_gate")
    y_a = _mm(ya_in, w_a_f, out_dtype=BF16, name="branch_a")

    def f_merge(yb, ga, gb, ya):
        return (yb, _sigmoid(ga) * ya + _sigmoid(gb) * yb), ()

    y_b, merged = _mm(yb_in, w_b_f, name="branch_b_merge", tiles=(min(s_len, 512), D, MLA_W),
                      epilogue=(f_merge, [win("ga"), win("gb"), y_a], [], [(D, BF16), (D, BF16)], []))

    def f_loss(out_, x_, t_, mg, bg, gf_):
        gate = mg + bg
        x2 = x_ + gate * out_
        r2 = _rms(x2)
        xn2 = x2 * r2
        err = xn2 * gf_ - t_
        loss = jnp.full((1, LANES), 0.5 / D, F32) * jnp.sum(err * err)
        dy = err * (1.0 / D)
        dx2 = _rms_bwd(xn2, r2, dy * gf_)
        return (dx2, dx2 * gate), (loss, _colsum(dy * xn2), _colsum(dx2 * out_))

    dx2, d_out, loss_part, d_gf, d_gate = _mm(
        merged, w_out_f, name="out_proj_loss", tiles=(min(s_len, 512), D, D),
        epilogue=(f_loss, [x2d, tgt], [mod_gate, b_gate, gf], [(D, F32), (D, BF16)], [LANES, D, D]))

    dw_out = _mm(merged, d_out, ta=True, name="dw_out")

    def f_dmerge(dm, ga, gb, ya, yb):
        sa, sb = _sigmoid(ga), _sigmoid(gb)
        return (dm * sa, dm * sb, dm * ya * sa * (1.0 - sa), dm * yb * sb * (1.0 - sb)), ()

    d_ya, d_yb, d_ga, d_gb = _mm(
        d_out, w_out_f, tb=True, name="d_merge", tiles=(min(s_len, 256), D, D),
        epilogue=(f_dmerge, [win("ga"), win("gb"), y_a, y_b], [], [(D, BF16)] * 4, []))
    dw_a = _mm(ya_in, d_ya, ta=True, name="dw_branch_a")
    dw_b = _mm(yb_in, d_yb, ta=True, name="dw_branch_b")

    def f_dgate(d_in, o_, z_):
        return (d_in * _silu(z_), d_in * o_ * _silu_grad(z_)), ()

    d_oa, d_sbz = _mm(d_ya, w_a_f, tb=True, name="d_branch_a",
                      epilogue=(f_dgate, [o_a, win("sbz")], [], [(SB_W, BF16), (SB_W, BF16)], []))
    d_ob, d_mlaz = _mm(d_yb, w_b_f, tb=True, name="d_branch_b",
                       epilogue=(f_dgate, [o_b, win("mlaz")], [], [(MLA_W, F32), (MLA_W, BF16)], []))

    dq_cat, dk_cat, dv_b = _mla_backward(q_cat, k_cat, kv, o_b, lse, d_ob, s_len)

    def f_drope(dq, dk, dv_, cs, sn):
        lane = lax.broadcasted_iota(jnp.int32, cs.shape, 1)
        first, second = lane < ROPE, (lane >= ROPE) & (lane < 2 * ROPE)
        dq1 = dq2 = dk1 = dk2 = None
        for p in range(pairs):
            q_rot = dq[:, LANES * (2 * p + 1):LANES * (2 * p + 2)]
            k_rot = dk[:, LANES * (2 * p + 1):LANES * (2 * p + 2)]
            parts = (pltpu.roll(jnp.where(first, q_rot, 0.0), (ROPE * p) % LANES, 1),
                     pltpu.roll(jnp.where(second, q_rot, 0.0), (LANES - ROPE + ROPE * p) % LANES, 1),
                     jnp.where(first, k_rot, 0.0), jnp.where(second, k_rot, 0.0))
            if p == 0:
                dq1, dq2, dk1, dk2 = parts
            else:
                dq1, dq2, dk1, dk2 = dq1 + parts[0], dq2 + parts[1], dk1 + parts[2], dk2 + parts[3]
        dqn_ = [dq[:, 2 * LANES * p:2 * LANES * p + LANES] for p in range(pairs)]
        dkn_ = [dk[:, 2 * LANES * p:2 * LANES * p + LANES] for p in range(pairs)]
        return (jnp.concatenate(dqn_ + [dq1 * cs + dq2 * sn, dq2 * cs - dq1 * sn], axis=1),
                jnp.concatenate(dkn_ + [dv_], axis=1),
                jnp.concatenate([dk1 * cs + dk2 * sn, dk2 * cs - dk1 * sn], axis=1)), ()

    dq_k, dkv_k, d_krot = _rowwise(f_drope, [dq_cat, dk_cat, dv_b, cos_t, sin_t], [],
                                   [(HEADS * QK_DIM, BF16), (2 * MLA_W, BF16), (2 * LANES, BF16)], name="d_rope")
    dw_uq_k = _mm(cq_n, dq_k, ta=True, name="dw_uq")
    dw_ukv_k = _mm(ckv_n, dkv_k, ta=True, name="dw_ukv")

    def f_dlat(d_normed, latent, gain):
        r = _rms(latent)
        normed = latent * r
        return (_rms_bwd(normed, r, d_normed * gain),), (_colsum(d_normed * normed),)

    d_cq, d_gq = _mm(dq_k, w_uq_k, tb=True, name="d_cq_norm",
                     epilogue=(f_dlat, [win("cq")], [gq], [(Q_RANK, BF16)], [Q_RANK]))
    d_ckv, d_gkv = _mm(dkv_k, w_ukv_k, tb=True, name="d_ckv_norm",
                       epilogue=(f_dlat, [win("ckv")], [gkv], [(KV_RANK, BF16)], [KV_RANK]))

    def col_blocks(g):
        kdim, n8 = g.shape
        return g.astype(BF16).reshape(kdim, N_DEV, n8 // N_DEV).transpose(1, 0, 2).reshape(N_DEV, -1, LANES)

    g_blocks = jnp.concatenate([col_blocks(_uq_from_kernel_layout(dw_uq_k)), col_blocks(_ukv_from_kernel_layout(dw_ukv_k)),
                                col_blocks(dw_a), col_blocks(dw_b), dw_out.astype(BF16).reshape(N_DEV, -1, LANES)], axis=1)
    (d_sbq, d_sbk, d_sbv), (g_recv,) = _sb_backward(proj, sb_tails, d_oa, s_len, _Exchange([g_blocks]))

    d_proj = jnp.concatenate([d_ga, d_gb, d_sbq, d_sbk, d_sbv, d_sbz, d_mlaz,
                              d_cq, jnp.zeros((s_len, LANES), BF16), d_ckv, d_krot], axis=1)
    dw_in_kt = _mm(d_proj, h, ta=True, out_dtype=BF16, name="dw_in", tiles=(512, D, s_len))

    def krot_body(t_ref, o_ref):
        half = ROPE // 2
        for part in range(2):
            acc = t_ref[part * LANES:part * LANES + half, :].astype(F32)
            for hh in range(1, HEADS):
                acc = acc + t_ref[part * LANES + hh * half:part * LANES + (hh + 1) * half, :].astype(F32)
            o_ref[part * half:(part + 1) * half, :] = acc

    dw_krot = pl.pallas_call(krot_body, name="dw_krot_sum", out_shape=jax.ShapeDtypeStruct((ROPE, D), F32))(
        dw_in_kt[O_KROT:O_KROT + 2 * LANES])

    g_in_blocks = _w_in_t_from_kernel_layout(dw_in_kt, dw_krot).reshape(N_DEV, -1, D)
    def f_dx(dh_, x_, dx2_, g1_, msc, bsc):
        r = _rms(x_)
        xn = x_ * r
        dn1 = dh_ * (1.0 + (msc + bsc))
        return ((dx2_ + _rms_bwd(xn, r, dn1 * g1_),),
                (_colsum(dh_), _colsum(dh_ * (xn * g1_)), _colsum(dn1 * xn)))

    (grad_x2d, d_shift, d_scale, d_g1), (g_in_recv,) = _mm(
        d_proj, w_in_kt, name="d_h", tiles=(min(s_len, 512), D, IN_PAD // 2), exchange=_Exchange([g_in_blocks]),
        epilogue=(f_dx, [x2d, dx2], [g1, mod_scale, b_scale], [(D, F32)], [D, D, D]))

    g_in_sum_t = _sum_blocks(g_in_recv, "sum_grads_w_in")
    g_sum = _sum_blocks(g_recv, "sum_grads")
    g_big = [g_sum[offs[t]:offs[t + 1]].reshape(big[t].shape) for t in range(5)]

    small = jnp.concatenate([d_shift, d_scale, d_gate, d_g1, d_gq, d_gkv, d_gf, loss_part], axis=1)
    n_small = small.shape[1]
    pad = (-n_small) % (8 * LANES)
    small = jnp.pad(small, ((0, 0), (0, pad))).reshape(-1, LANES)
    small_all = _all_gather(small, "gather_small")
    small_sum = _sum_blocks(small_all, "sum_small").reshape(1, -1)
    g_b_ada = small_sum[:, :3 * D]
    g_g1 = small_sum[:, 3 * D:4 * D]
    g_gq = small_sum[:, 4 * D:4 * D + Q_RANK]
    g_gkv = small_sum[:, 4 * D + Q_RANK:4 * D + Q_RANK + KV_RANK]
    g_gf = small_sum[:, 4 * D + Q_RANK + KV_RANK:4 * D + Q_RANK + KV_RANK + D]

    dmod_all = small_all.reshape(N_DEV, -1)[:, :3 * D]
    dmod_cols = lax.dynamic_slice_in_dim(dmod_all, me * (3 * D // N_DEV), 3 * D // N_DEV, axis=1)
    g_w_ada = _mm(c_all, dmod_cols, ta=True, name="dw_ada")

    loss = small_sum[0, n_small - LANES]

    names = ["w_ada", "b_ada", "norm_gain", "w_in", "q_norm_gain", "w_uq", "kv_norm_gain", "w_ukv",
             "w_branch_a", "w_branch_b", "w_out", "final_norm_gain"]
    weights = dict(w_ada=w_ada, b_ada=b_ada, norm_gain=norm_gain, w_in=w_in, q_norm_gain=q_norm_gain, w_uq=w_uq,
                   kv_norm_gain=kv_norm_gain, w_ukv=w_ukv, w_branch_a=w_branch_a, w_branch_b=w_branch_b, w_out=w_out,
                   final_norm_gain=final_norm_gain)
    moms = dict(w_ada=m_w_ada, b_ada=m_b_ada, norm_gain=m_norm_gain, w_in=m_w_in, q_norm_gain=m_q_norm_gain,
                w_uq=m_w_uq, kv_norm_gain=m_kv_norm_gain, w_ukv=m_w_ukv, w_branch_a=m_w_branch_a,
                w_branch_b=m_w_branch_b, w_out=m_w_out, final_norm_gain=m_final_norm_gain)
    vels = dict(w_ada=v_w_ada, b_ada=v_b_ada, norm_gain=v_norm_gain, w_in=v_w_in, q_norm_gain=v_q_norm_gain,
                w_uq=v_w_uq, kv_norm_gain=v_kv_norm_gain, w_ukv=v_w_ukv, w_branch_a=v_w_branch_a,
                w_branch_b=v_w_branch_b, w_out=v_w_out, final_norm_gain=v_final_norm_gain)
    grads2d = dict(w_ada=g_w_ada, b_ada=g_b_ada, norm_gain=g_g1, w_in=g_in_sum_t, q_norm_gain=g_gq, w_uq=g_big[0],
                   kv_norm_gain=g_gkv, w_ukv=g_big[1], w_branch_a=g_big[2], w_branch_b=g_big[3], w_out=g_big[4],
                   final_norm_gain=g_gf)

    grads, deltas, new_m, new_v = [], [], [], []
    for n in names:
        w = weights[n]
        if n == "w_in":
            to2d = lambda t: t[0].T if t.ndim == 3 else t
            back = lambda t: t.T[None]
        else:
            shape2d = grads2d[n].shape
            to2d = lambda t, s=shape2d: t.reshape(s)
            back = lambda t, s=w.shape: t.reshape(s)
        g2d = to2d(grads2d[n])
        d_, m_, v_ = _adamw_call(to2d(w), g2d, to2d(moms[n]), to2d(vels[n]), "adamw_" + n)
        grads.append(back(g2d))
        deltas.append(back(d_))
        new_m.append(back(m_))
        new_v.append(back(v_))

    return (loss, grad_x2d.reshape(x.shape), *grads, *deltas, *new_m, *new_v)
```

```python
import functools
import math

import jax
import jax.numpy as jnp
from jax import lax
from jax.experimental import pallas as pl
from jax.experimental.pallas import tpu as pltpu

F32 = jnp.float32
BF16 = jnp.bfloat16
MXU_DTYPE = jnp.bfloat16

N_DEV = 8
D = 1024
HEADS = 8
HD = 64
SB_W = 512
MLA_W = 512
Q_RANK = 384
KV_RANK = 256
ROPE = 32
NOPE = 64
QK_DIM = NOPE + ROPE
EPS = 1e-6
ROPE_BASE = 10000.0

ADAM_LR = 0.001
ADAM_B1 = 0.9
ADAM_B2 = 0.999
ADAM_EPS = 1e-08
ADAM_WD = 0.01
ADAM_STEP = 10

LANES = 128
VMEM_LIMIT = 48 * 1024 * 1024

O_GA, O_GB = 0, 1024
O_SBQ, O_SBK, O_SBV, O_SBZ = 2048, 2560, 3072, 3584
O_MLAZ = 4096
O_CQ = 4608
O_CKV = 5120
O_KROT = 5376
IN_PAD = 5632

BQ = 256
BK = 256


def _cparams(*sem):
    return pltpu.CompilerParams(dimension_semantics=sem, vmem_limit_bytes=VMEM_LIMIT)


def _tile_of(n, cap=512):
    if n <= cap:
        return n
    for t in (1024, 768, 512, 384, 256, 128):
        if t <= cap and n % t == 0:
            return t
    raise ValueError(n)


def _rowwise(fn, rows, vecs, outs, reds=(), *, name, tile=512):
    norm = []
    for r in rows:
        if isinstance(r, tuple):
            arr, cb, w = r[:3]
            ro = r[3] if len(r) > 3 else 0
        else:
            arr, cb, w, ro = r, 0, r.shape[1], 0
        norm.append((arr, cb, w, ro))
    s_len = norm[0][0].shape[0]
    tile = min(tile, s_len)
    assert s_len % tile == 0
    n_row, n_vec, n_out, n_red = len(norm), len(vecs), len(outs), len(reds)

    def body(*refs):
        step = pl.program_id(0)
        row_refs = refs[:n_row]
        vec_refs = refs[n_row:n_row + n_vec]
        out_refs = refs[n_row + n_vec:n_row + n_vec + n_out]
        red_refs = refs[n_row + n_vec + n_out:]
        row_res, red_res = fn(*[r[...] for r in row_refs], *[v[...] for v in vec_refs])
        for o, val in zip(out_refs, row_res):
            o[...] = val.astype(o.dtype)
        if n_red:
            @pl.when(step == 0)
            def _():
                for r in red_refs:
                    r[...] = jnp.zeros(r.shape, r.dtype)
            for r, val in zip(red_refs, red_res):
                r[...] += val

    in_specs = []
    for arr, cb, w, ro in norm:
        in_specs.append(pl.BlockSpec((tile, w), functools.partial(lambda i, cb, rb: (i + rb, cb), cb=cb, rb=ro // tile)))
        assert ro % tile == 0
    for v in vecs:
        in_specs.append(pl.BlockSpec(v.shape, lambda i: (0, 0)))
    out_shape = [jax.ShapeDtypeStruct((s_len, w), dt) for w, dt in outs]
    out_specs = [pl.BlockSpec((tile, w), lambda i: (i, 0)) for w, _ in outs]
    out_shape += [jax.ShapeDtypeStruct((1, w), F32) for w in reds]
    out_specs += [pl.BlockSpec((1, w), lambda i: (0, 0)) for w in reds]
    res = pl.pallas_call(
        body, name=name, grid=(s_len // tile,), in_specs=in_specs, out_specs=out_specs, out_shape=out_shape,
        compiler_params=_cparams("arbitrary" if n_red else "parallel"),
    )(*[a for a, _, _, _ in norm], *vecs)
    return res


def _mm(a, b, *, ta=False, tb=False, out_dtype=F32, name, exchange=None, tiles=None, epilogue=None):
    m, k = (a.shape[1], a.shape[0]) if ta else a.shape
    n = b.shape[0] if tb else b.shape[1]
    assert (b.shape[1] if tb else b.shape[0]) == k
    tm, tn, tk = tiles or (_tile_of(m, 1024), _tile_of(n, 1024 if n <= 1024 else 512), _tile_of(k, 1024))
    assert m % tm == 0 and n % tn == 0 and k % tk == 0
    ni, nj, nk = m // tm, n // tn, k // tk
    dims = (((0 if ta else 1,), (1 if tb else 0,)), ((), ()))
    n_ex = len(exchange.arrays) if exchange else 0
    fn, rows, vecs, outs, reds = epilogue or (None, (), (), (), ())
    rows = [r if isinstance(r, tuple) else (r, 0, r.shape[1]) for r in rows]
    assert not epilogue or tn == n
    n_res = len(outs) + len(reds) if epilogue else 1

    def body(*refs):
        a_ref, b_ref = refs[:2]
        row_refs, refs = refs[2:2 + len(rows)], refs[2 + len(rows):]
        vec_refs, refs = refs[:len(vecs)], refs[len(vecs):]
        x_refs, refs = refs[:n_ex], refs[n_ex:]
        res_refs, refs = refs[:n_res], refs[n_res:]
        out_refs, acc_ref, sems = refs[:n_ex], refs[n_ex], refs[n_ex + 1:]
        i, j, kk = pl.program_id(0), pl.program_id(1), pl.program_id(2)
        first = (i == 0) & (j == 0) & (kk == 0)

        if exchange:
            @pl.when(first)
            def _():
                exchange.start(x_refs, out_refs, sems)

        if reds:
            @pl.when(first)
            def _():
                for r in res_refs[len(outs):]:
                    r[...] = jnp.zeros(r.shape, r.dtype)

        @pl.when(kk == 0)
        def _():
            acc_ref[...] = jnp.zeros(acc_ref.shape, F32)

        acc_ref[...] += lax.dot_general(a_ref[...].astype(MXU_DTYPE), b_ref[...].astype(MXU_DTYPE), dims,
                                        preferred_element_type=F32)

        @pl.when(kk == nk - 1)
        def _():
            if not epilogue:
                res_refs[0][...] = acc_ref[...].astype(res_refs[0].dtype)
                return
            row_res, red_res = fn(acc_ref[...], *[r[...] for r in row_refs], *[v[...] for v in vec_refs])
            for o, val in zip(res_refs, row_res):
                o[...] = val.astype(o.dtype)
            for r, val in zip(res_refs[len(outs):], red_res):
                r[...] += val

        if exchange:
            @pl.when((i == ni - 1) & (j == nj - 1) & (kk == nk - 1))
            def _():
                exchange.wait(x_refs, out_refs, sems)

    a_spec = pl.BlockSpec((tk, tm), lambda i, j, kk: (kk, i)) if ta else pl.BlockSpec((tm, tk), lambda i, j, kk: (i, kk))
    b_spec = pl.BlockSpec((tn, tk), lambda i, j, kk: (j, kk)) if tb else pl.BlockSpec((tk, tn), lambda i, j, kk: (kk, j))
    in_specs = [a_spec, b_spec]
    in_specs += [pl.BlockSpec((tm, w), functools.partial(lambda i, j, kk, cb: (i, cb), cb=cb)) for _, cb, w in rows]
    in_specs += [pl.BlockSpec(v.shape, lambda i, j, kk: (0, 0)) for v in vecs]
    if epilogue:
        res_specs = [pl.BlockSpec((tm, w), lambda i, j, kk: (i, 0)) for w, _ in outs]
        res_specs += [pl.BlockSpec((1, w), lambda i, j, kk: (0, 0)) for w in reds]
        res_shape = [jax.ShapeDtypeStruct((m, w), dt) for w, dt in outs] + [jax.ShapeDtypeStruct((1, w), F32) for w in reds]
    else:
        res_specs = [pl.BlockSpec((tm, tn), lambda i, j, kk: (i, j))]
        res_shape = [jax.ShapeDtypeStruct((m, n), out_dtype)]
    ordered = bool(exchange or reds)
    res = pl.pallas_call(
        body, name=name, grid=(ni, nj, nk),
        in_specs=in_specs + (exchange.in_specs if exchange else []),
        out_specs=res_specs + (exchange.out_specs if exchange else []),
        out_shape=res_shape + (exchange.out_shape if exchange else []),
        scratch_shapes=[pltpu.VMEM((tm, tn), F32)] + (exchange.scratch if exchange else []),
        compiler_params=_cparams(*(("arbitrary",) * 3 if ordered else ("parallel", "parallel", "arbitrary"))),
    )(a, b, *[r[0] for r in rows], *vecs, *(exchange.arrays if exchange else []))
    main = res[:n_res] if epilogue else res[0]
    return (main, res[n_res:]) if exchange else main


_NT = (((1,), (1,)), ((), ()))
_TN = (((0,), (0,)), ((), ()))


def _dot(a, b):
    return jnp.dot(a, b, preferred_element_type=F32)


def _dot_nt(a, b):
    return lax.dot_general(a, b, _NT, preferred_element_type=F32)


def _dot_tn(a, b):
    return lax.dot_general(a, b, _TN, preferred_element_type=F32)


def _running_sum(x, tri):
    return _dot(x.astype(MXU_DTYPE), tri)


def _neg_softplus(z):
    u = jnp.exp2(jnp.abs(z) * (-1.0 / math.log(2.0)))
    return -jnp.maximum(z, 0.0) - jnp.log(1.0 + u)


def _walk_blocks(step, st, n, descending, group=2):
    done = 0
    size = group
    while size >= 1:
        def trip(t, s, size=size, done=done):
            js = [done + size * t + g for g in range(size)]
            return step([n - 1 - j for j in js] if descending else js, s)

        trips = (n - done) // size
        st = lax.fori_loop(0, trips, trip, st)
        done = done + size * trips
        size //= 2
    return st


def _chains(js):
    return [(h, t) for t in range(len(js)) for h in range(2)]


def _rowsum(x):
    return jnp.sum(x, axis=1, keepdims=True)


def _attn_consts():
    row = lax.broadcasted_iota(jnp.int32, (BQ, BK), 0)
    col = lax.broadcasted_iota(jnp.int32, (BQ, BK), 1)
    trow = lax.broadcasted_iota(jnp.int32, (BK, BK), 0)
    tcol = lax.broadcasted_iota(jnp.int32, (BK, BK), 1)
    lane = lax.broadcasted_iota(jnp.int32, (BQ, LANES), 1)
    klane = lax.broadcasted_iota(jnp.int32, (BK, LANES), 1)
    return row, col, trow, tcol, lane, klane


assert BQ == BK


def _diag_step(jd, descending, diag_mask, group=4):
    below = list(range(group)) if descending else list(reversed(range(group)))
    js = [jnp.maximum(jd - o, 0) for o in below]
    masks = [diag_mask if o == 0 else jd - o >= 0 for o in below]
    return js, masks, jnp.maximum(jd - (group - 1), 0)


def _key_slice(j):
    return pl.ds(pl.multiple_of(j * BK, BK), BK)


def _sb_forward(proj, s_len, exchange):
    nq = s_len // BQ
    assert s_len // BK <= HD
    qc, kc, vc = O_SBQ // LANES, O_SBK // LANES, O_SBV // LANES
    n_ex = len(exchange.arrays)

    def body(q_ref, k_ref, v_ref, *rest):
        x_refs, (o_ref, tails_ref) = rest[:n_ex], rest[n_ex:n_ex + 2]
        out_refs, sems = rest[n_ex + 2:2 * n_ex + 2], rest[2 * n_ex + 2:]
        p = pl.program_id(0)
        i = pl.program_id(1)
        jd = i

        @pl.when((p == 0) & (i == 0))
        def _():
            exchange.start(x_refs, out_refs, sems)

        row, col, trow, tcol, lane, klane = _attn_consts()
        tri = (trow >= tcol).astype(MXU_DTYPE)
        q = q_ref[...] * 0.125
        qh = [jnp.where(lane < HD, q, 0.0).astype(MXU_DTYPE), jnp.where(lane >= HD, q, 0.0).astype(MXU_DTYPE)]

        km = [klane < HD, klane >= HD]

        def step(js, st, masks):
            carry, acc, tail = st
            chains = _chains(js)
            kj = [k_ref[_key_slice(j), :].astype(MXU_DTYPE) for j in js]
            vj = [v_ref[_key_slice(j), :].astype(MXU_DTYPE) for j in js]
            z = {(h, t): _dot_nt(qh[h], kj[t]) for h, t in chains}
            run = list(carry)
            suf, carry_in = {}, {}
            for h, t in chains:
                lom = _neg_softplus(z[h, t])
                if masks:
                    lom = jnp.where(masks[t], lom, 0.0)
                suf[h, t] = _running_sum(lom, tri)
                carry_in[h, t] = run[h]
                run[h] = run[h] + _rowsum(lom)
            for h, t in chains:
                a = jnp.exp(z[h, t] + suf[h, t] + carry_in[h, t])
                if masks:
                    a = jnp.where(masks[t], a, 0.0)
                acc = acc + _dot(a.astype(MXU_DTYPE), jnp.where(km[h], vj[t], 0))
                tail_lane = js[t] if not masks or masks[t].ndim else jnp.where(masks[t], js[t], -LANES)
                tail = jnp.where(lane == h * HD + tail_lane, carry_in[h, t], tail)
            return tuple(run), acc, tail

        zero = jnp.zeros((BQ, LANES), F32)
        diag_js, masks, left = _diag_step(jd, True, col < row, group=2)
        st = step(diag_js, ((jnp.zeros((BQ, 1), F32),) * 2, zero, zero), masks)
        st = _walk_blocks(lambda js, s: step(js, s, None), st, left, True, group=8)
        o_ref[...] = st[1]
        tails_ref[...] = st[2]

        @pl.when((p == HEADS // 2 - 1) & (i == nq - 1))
        def _():
            exchange.wait(x_refs, out_refs, sems)

    blk = pl.BlockSpec((BQ, LANES), lambda p, i: (i, p))
    out = jax.ShapeDtypeStruct((s_len, SB_W), F32)
    res = pl.pallas_call(
        body, name="sb_fwd", grid=(HEADS // 2, nq),
        in_specs=[pl.BlockSpec((BQ, LANES), lambda p, i: (i, qc + p)),
                  pl.BlockSpec((s_len, LANES), lambda p, i: (0, kc + p)),
                  pl.BlockSpec((s_len, LANES), lambda p, i: (0, vc + p))] + exchange.in_specs,
        out_specs=[blk, blk] + exchange.out_specs, out_shape=[out, out] + exchange.out_shape,
        scratch_shapes=exchange.scratch,
        compiler_params=_cparams("arbitrary", "arbitrary"),
    )(proj, proj, proj, *exchange.arrays)
    return res[:2], res[2:]


def _sb_backward(proj, tails, do, s_len, exchange):
    nq = s_len // BQ
    qc, kc, vc = O_SBQ // LANES, O_SBK // LANES, O_SBV // LANES
    n_ex = len(exchange.arrays)

    def body(q_ref, k_ref, v_ref, tails_ref, do_ref, *rest):
        x_refs, (dq_ref, dk_out, dv_out) = rest[:n_ex], rest[n_ex:n_ex + 3]
        out_refs, (dk_ref, dv_ref), sems = rest[n_ex + 3:2 * n_ex + 3], rest[2 * n_ex + 3:2 * n_ex + 5], rest[2 * n_ex + 5:]
        p = pl.program_id(0)
        i = pl.program_id(1)
        jd = i

        @pl.when((p == 0) & (i == 0))
        def _():
            exchange.start(x_refs, out_refs, sems)

        @pl.when(i == 0)
        def _():
            dk_ref[...] = jnp.zeros(dk_ref.shape, F32)
            dv_ref[...] = jnp.zeros(dv_ref.shape, F32)

        row, col, trow, tcol, lane, klane = _attn_consts()
        tri = (trow >= tcol).astype(MXU_DTYPE)
        tri_p = (trow <= tcol).astype(MXU_DTYPE)
        q = q_ref[...] * 0.125
        tails_blk = tails_ref[...]
        do_blk = do_ref[...]
        hm = [lane < HD, lane >= HD]
        km = [klane < HD, klane >= HD]
        qh = [jnp.where(m, q, 0.0).astype(MXU_DTYPE) for m in hm]
        doh = [jnp.where(m, do_blk, 0.0).astype(MXU_DTYPE) for m in hm]

        def step(js, st, masks):
            before, dq = st
            chains = _chains(js)
            kj = [k_ref[_key_slice(j), :].astype(MXU_DTYPE) for j in js]
            vj = [v_ref[_key_slice(j), :].astype(MXU_DTYPE) for j in js]
            z = {(h, t): _dot_nt(qh[h], kj[t]) for h, t in chains}
            da = {(h, t): _dot_nt(doh[h], vj[t]) for h, t in chains}
            suf, sig = {}, {}
            for h, t in chains:
                lom = _neg_softplus(z[h, t])
                if masks:
                    lom = jnp.where(masks[t], lom, 0.0)
                suf[h, t] = _running_sum(lom, tri)
                sig[h, t] = jnp.exp(z[h, t] + lom)
            run = list(before)
            dl, pre, before_in = {}, {}, {}
            dk_add, dv_add = [None] * len(js), [None] * len(js)
            for h, t in chains:
                tail = _rowsum(jnp.where(lane == h * HD + js[t], tails_blk, 0.0))
                a = jnp.exp(z[h, t] + suf[h, t] + tail)
                if masks:
                    a = jnp.where(masks[t], a, 0.0)
                dl[h, t] = da[h, t] * a
                pre[h, t] = _dot(dl[h, t].astype(MXU_DTYPE), tri_p)
                dv_h = _dot_tn(a.astype(MXU_DTYPE), doh[h])
                dv_add[t] = dv_h if dv_add[t] is None else dv_add[t] + dv_h
                before_in[h, t] = run[h]
                run[h] = run[h] + _rowsum(dl[h, t])
            for h, t in chains:
                upto = before_in[h, t] + pre[h, t]
                dz = dl[h, t] - sig[h, t] * upto
                if masks:
                    dz = jnp.where(masks[t], dz, 0.0)
                dzb = dz.astype(MXU_DTYPE)
                dq = dq + _dot(dzb, jnp.where(km[h], kj[t], 0))
                dk_h = _dot_tn(dzb, qh[h])
                dk_add[t] = dk_h if dk_add[t] is None else dk_add[t] + dk_h
            for t, j in enumerate(js):
                dk_ref[_key_slice(j), :] += dk_add[t]
                dv_ref[_key_slice(j), :] += dv_add[t]
            return tuple(run), dq

        st = ((jnp.zeros((BQ, 1), F32),) * 2, jnp.zeros((BQ, LANES), F32))
        diag_js, masks, left = _diag_step(jd, False, col < row, group=2)
        st = _walk_blocks(lambda js, s: step(js, s, None), st, left, False, group=8)
        st = step(diag_js, st, masks)
        dq_ref[...] = (st[1] * 0.125).astype(dq_ref.dtype)

        @pl.when(i == nq - 1)
        def _():
            dk_out[...] = dk_ref[...].astype(dk_out.dtype)
            dv_out[...] = dv_ref[...].astype(dv_out.dtype)

        @pl.when((p == HEADS // 2 - 1) & (i == nq - 1))
        def _():
            exchange.wait(x_refs, out_refs, sems)

    blk = pl.BlockSpec((BQ, LANES), lambda p, i: (i, p))
    full = pl.BlockSpec((s_len, LANES), lambda p, i: (0, p))
    out = jax.ShapeDtypeStruct((s_len, SB_W), BF16)
    res = pl.pallas_call(
        body, name="sb_bwd", grid=(HEADS // 2, nq),
        in_specs=[pl.BlockSpec((BQ, LANES), lambda p, i: (i, qc + p)),
                  pl.BlockSpec((s_len, LANES), lambda p, i: (0, kc + p)),
                  pl.BlockSpec((s_len, LANES), lambda p, i: (0, vc + p)),
                  blk, blk] + exchange.in_specs,
        out_specs=[blk, full, full] + exchange.out_specs, out_shape=[out, out, out] + exchange.out_shape,
        scratch_shapes=[pltpu.VMEM((s_len, LANES), F32)] * 2 + exchange.scratch,
        compiler_params=_cparams("arbitrary", "arbitrary"),
    )(proj, proj, proj, tails, do, *exchange.arrays)
    return res[:3], res[3:]


def _pair_mask(rows, h):
    lane = lax.broadcasted_iota(jnp.int32, (rows, 2 * LANES), 1)
    rot = lane - LANES
    return (((lane < LANES) & (lane // HD == h))
            | ((lane >= LANES) & (rot < 2 * ROPE) & ((rot // (ROPE // 2)) % 2 == h)))


def _mla_forward(q_cat, k_cat, kv, s_len):
    nq = s_len // BQ
    scale = 1.0 / math.sqrt(QK_DIM)

    def body(q_ref, k_ref, v_ref, o_ref, lse_ref):
        i = pl.program_id(1)
        jd = i
        row, col, trow, tcol, lane, klane = _attn_consts()
        q = q_ref[...]
        hm = [lane < HD, lane >= HD]
        km = [klane < HD, klane >= HD]
        qh = [jnp.where(_pair_mask(BQ, h), q, 0) for h in range(2)]

        def step(js, st, masks):
            m_run, l_run, acc = st
            chains = _chains(js)
            kj = [k_ref[_key_slice(j), :] for j in js]
            vj = [v_ref[_key_slice(j), :].astype(MXU_DTYPE) for j in js]
            s = {}
            for h, t in chains:
                s[h, t] = _dot_nt(qh[h], kj[t]) * scale
                if masks:
                    s[h, t] = jnp.where(masks[t], s[h, t], -jnp.inf)
            m_new, alpha, l_new = [], [], []
            for h in range(2):
                top = m_run[h]
                for t in range(len(js)):
                    top = jnp.maximum(top, jnp.max(s[h, t], axis=1, keepdims=True))
                m_new.append(top)
                alpha.append(jnp.exp(m_run[h] - top))
                l_new.append(alpha[h] * l_run[h])
            add = None
            for h, t in chains:
                pr = jnp.exp(s[h, t] - m_new[h])
                l_new[h] = l_new[h] + _rowsum(pr)
                part = _dot(pr.astype(MXU_DTYPE), jnp.where(km[h], vj[t], 0))
                add = part if add is None else add + part
            acc = jnp.where(hm[0], alpha[0], alpha[1]) * acc + add
            return tuple(m_new), tuple(l_new), acc

        st = ((jnp.full((BQ, 1), -1e30, F32),) * 2, (jnp.zeros((BQ, 1), F32),) * 2, jnp.zeros((BQ, LANES), F32))
        diag_js, masks, left = _diag_step(jd, True, col <= row)
        st = step(diag_js, st, masks)
        m_run, l_run, acc = _walk_blocks(lambda js, s: step(js, s, None), st, left, True, group=8)
        o_ref[...] = acc / jnp.where(hm[0], l_run[0], l_run[1])
        lse_ref[...] = jnp.where(hm[0], m_run[0] + jnp.log(l_run[0]), m_run[1] + jnp.log(l_run[1]))

    blk = pl.BlockSpec((BQ, LANES), lambda p, i: (i, p))
    out = jax.ShapeDtypeStruct((s_len, MLA_W), F32)
    return pl.pallas_call(
        body, name="mla_fwd", grid=(HEADS // 2, nq),
        in_specs=[pl.BlockSpec((BQ, 2 * LANES), lambda p, i: (i, p)),
                  pl.BlockSpec((s_len, 2 * LANES), lambda p, i: (0, p)),
                  pl.BlockSpec((s_len, LANES), lambda p, i: (0, MLA_W // LANES + p))],
        out_specs=[blk, blk], out_shape=[out, out],
        compiler_params=_cparams("parallel", "parallel"),
    )(q_cat, k_cat, kv)


def _mla_backward(q_cat, k_cat, kv, o, lse, do, s_len):
    nq = s_len // BQ
    scale = 1.0 / math.sqrt(QK_DIM)

    def body(q_ref, k_ref, v_ref, o_ref, lse_ref, do_ref, dq_ref, dk_ref, dv_ref):
        i = pl.program_id(1)

        @pl.when(i == 0)
        def _():
            dk_ref[...] = jnp.zeros(dk_ref.shape, F32)
            dv_ref[...] = jnp.zeros(dv_ref.shape, F32)

        jd = i
        row, col, trow, tcol, lane, klane = _attn_consts()
        q = q_ref[...]
        o_blk = o_ref[...]
        do_blk = do_ref[...]
        lse_blk = lse_ref[...]
        hm = [lane < HD, lane >= HD]
        kpm = [_pair_mask(BK, h) for h in range(2)]
        qh = [jnp.where(_pair_mask(BQ, h), q, 0) for h in range(2)]
        doh_f = [jnp.where(m, do_blk, 0.0) for m in hm]
        doh = [d.astype(MXU_DTYPE) for d in doh_f]
        delta = [jnp.sum(d * o_blk, axis=1, keepdims=True) for d in doh_f]
        lse_h = [jnp.sum(jnp.where(lane == h * HD, lse_blk, 0.0), axis=1, keepdims=True) for h in range(2)]

        def step(js, st, masks):
            dq = st
            chains = _chains(js)
            kj = [k_ref[_key_slice(j), :] for j in js]
            vj = [v_ref[_key_slice(j), :].astype(MXU_DTYPE) for j in js]
            s = {(h, t): _dot_nt(qh[h], kj[t]) for h, t in chains}
            dp = {(h, t): _dot_nt(doh[h], vj[t]) for h, t in chains}
            adds = [[None] * len(js) for _ in range(2)]

            def accumulate(slot, t, part):
                adds[slot][t] = part if adds[slot][t] is None else adds[slot][t] + part

            for h, t in chains:
                pr = jnp.exp(s[h, t] * scale - lse_h[h])
                if masks:
                    pr = jnp.where(masks[t], pr, 0.0)
                dsb = (pr * (dp[h, t] - delta[h]) * scale).astype(MXU_DTYPE)
                dq = dq + _dot(dsb, jnp.where(kpm[h], kj[t], 0))
                accumulate(0, t, _dot_tn(dsb, qh[h]))
                accumulate(1, t, _dot_tn(pr.astype(MXU_DTYPE), doh[h]))
            for t, j in enumerate(js):
                dk_ref[_key_slice(j), :] += adds[0][t]
                dv_ref[_key_slice(j), :] += adds[1][t]
            return dq

        diag_js, masks, left = _diag_step(jd, True, col <= row)
        st = step(diag_js, jnp.zeros((BQ, 2 * LANES), F32), masks)
        dq_ref[...] = _walk_blocks(lambda js, s: step(js, s, None), st, left, True, group=8)

    blk = pl.BlockSpec((BQ, LANES), lambda p, i: (i, p))
    full = pl.BlockSpec((s_len, LANES), lambda p, i: (0, p))
    out = jax.ShapeDtypeStruct((s_len, MLA_W), F32)
    out_cat = jax.ShapeDtypeStruct((s_len, 2 * MLA_W), F32)
    return pl.pallas_call(
        body, name="mla_bwd", grid=(HEADS // 2, nq),
        in_specs=[pl.BlockSpec((BQ, 2 * LANES), lambda p, i: (i, p)),
                  pl.BlockSpec((s_len, 2 * LANES), lambda p, i: (0, p)),
                  pl.BlockSpec((s_len, LANES), lambda p, i: (0, MLA_W // LANES + p)),
                  blk, blk, blk],
        out_specs=[pl.BlockSpec((BQ, 2 * LANES), lambda p, i: (i, p)),
                   pl.BlockSpec((s_len, 2 * LANES), lambda p, i: (0, p)), full],
        out_shape=[out_cat, out_cat, out],
        compiler_params=_cparams("arbitrary", "arbitrary"),
    )(q_cat, k_cat, kv, o, lse, do)


def _mesh_pos():
    return lax.axis_index("x"), lax.axis_index("y"), lax.axis_index("c")


def _dev_index(px, py, pc):
    return 4 * px + 2 * py + pc


def _all_gather(block, name):
    return _all_gather_parts([block], name)[0]


def _all_gather_parts(blocks, name):
    n = len(blocks)

    def body(*refs):
        x_refs, out_refs = refs[:n], refs[n:2 * n]
        send_sems, recv_sems, local_sems = refs[2 * n:]
        x, y, c = _mesh_pos()
        me, sibling = (x, y, c), (x, y, 1 - c)
        chips = [(1 - x, y), (x, 1 - y), (1 - x, 1 - y)]

        def copy(a, k, blockpos, to, src=None):
            slot = out_refs[a].at[_dev_index(*blockpos)]
            return pltpu.make_async_remote_copy(
                src_ref=slot if src is None else src, dst_ref=slot,
                send_sem=send_sems.at[7 * a + k], recv_sem=recv_sems.at[7 * a + k],
                device_id=to, device_id_type=pl.DeviceIdType.MESH)

        mine = [pltpu.make_async_copy(x_refs[a], out_refs[a].at[_dev_index(*me)], local_sems.at[a]) for a in range(n)]
        for cp in mine:
            cp.start()
        first = []
        for a in range(n):
            first.append(copy(a, 0, me, sibling, src=x_refs[a]))
            first += [copy(a, 1 + j, me, (*chip, c), src=x_refs[a]) for j, chip in enumerate(chips)]
        for cp in first:
            cp.start()
        passed = []
        for j, chip in enumerate(chips):
            for a in range(n):
                copy(a, 1 + j, (*chip, c), me).wait_recv()
                passed.append(copy(a, 4 + j, (*chip, c), sibling))
                passed[-1].start()
        for a in range(n):
            copy(a, 0, sibling, me).wait_recv()
            for j, chip in enumerate(chips):
                copy(a, 4 + j, (*chip, 1 - c), me).wait_recv()
        for cp in first + passed:
            cp.wait_send()
        for cp in mine:
            cp.wait()

    return pl.pallas_call(
        body, name=name,
        out_shape=[jax.ShapeDtypeStruct((N_DEV,) + b.shape, b.dtype) for b in blocks],
        in_specs=[pl.BlockSpec(memory_space=pl.ANY)] * n, out_specs=[pl.BlockSpec(memory_space=pl.ANY)] * n,
        scratch_shapes=[pltpu.SemaphoreType.DMA((7 * n,)), pltpu.SemaphoreType.DMA((7 * n,)),
                        pltpu.SemaphoreType.DMA((n,))],
    )(*blocks)


class _Exchange:
    def __init__(self, arrays):
        self.arrays = list(arrays)
        n = len(self.arrays)
        self.in_specs = [pl.BlockSpec(memory_space=pl.ANY)] * n
        self.out_specs = [pl.BlockSpec(memory_space=pl.ANY)] * n
        self.out_shape = [jax.ShapeDtypeStruct(a.shape, a.dtype) for a in self.arrays]
        self.scratch = [pltpu.SemaphoreType.DMA((7 * n,)), pltpu.SemaphoreType.DMA((7 * n,)),
                        pltpu.SemaphoreType.DMA((n,))]

    def _copies(self, x_refs, out_refs, sems, with_arrivals):
        send_sems, recv_sems, local_sems = sems
        x, y, c = _mesh_pos()
        me = _dev_index(x, y, c)
        flips = [(fx, fy, fc) for fx in (0, 1) for fy in (0, 1) for fc in (0, 1)][1:]
        peers = [(1 - x if fx else x, 1 - y if fy else y, 1 - c if fc else c) for fx, fy, fc in flips]
        mine, sends, arrivals = [], [], []
        for a in range(len(self.arrays)):
            mine.append(pltpu.make_async_copy(x_refs[a].at[me], out_refs[a].at[me], local_sems.at[a]))
            for k, peer in enumerate(peers):
                sends.append(pltpu.make_async_remote_copy(
                    src_ref=x_refs[a].at[_dev_index(*peer)], dst_ref=out_refs[a].at[me],
                    send_sem=send_sems.at[7 * a + k], recv_sem=recv_sems.at[7 * a + k],
                    device_id=peer, device_id_type=pl.DeviceIdType.MESH))
                if not with_arrivals:
                    continue
                arrivals.append(pltpu.make_async_remote_copy(
                    src_ref=x_refs[a].at[me], dst_ref=out_refs[a].at[_dev_index(*peer)],
                    send_sem=send_sems.at[7 * a + k], recv_sem=recv_sems.at[7 * a + k],
                    device_id=peer, device_id_type=pl.DeviceIdType.MESH))
        return mine, sends, arrivals

    def start(self, x_refs, out_refs, sems):
        mine, sends, _ = self._copies(x_refs, out_refs, sems, False)
        for cp in mine + sends:
            cp.start()

    def wait(self, x_refs, out_refs, sems):
        mine, sends, arrivals = self._copies(x_refs, out_refs, sems, True)
        for cp in arrivals:
            cp.wait_recv()
        for cp in sends:
            cp.wait_send()
        for cp in mine:
            cp.wait()


def _sum_blocks(parts, name):
    n, r, c = parts.shape
    row_tiles = [t for t in range(16, min(r, 2048) + 1, 16) if r % t == 0]
    if row_tiles:
        tr, tc = max(row_tiles), c
    else:
        tr, tc = r, 2 * LANES
    assert c % tc == 0

    def body(p_ref, o_ref):
        acc = p_ref[0].astype(F32)
        for s in range(1, n):
            acc = acc + p_ref[s].astype(F32)
        o_ref[...] = acc

    return pl.pallas_call(
        body, name=name, grid=(r // tr, c // tc),
        in_specs=[pl.BlockSpec((n, tr, tc), lambda i, j: (0, i, j))],
        out_specs=pl.BlockSpec((tr, tc), lambda i, j: (i, j)),
        out_shape=jax.ShapeDtypeStruct((r, c), F32),
        compiler_params=_cparams("parallel", "parallel"),
    )(parts)


def _sigmoid(x):
    return 1.0 / (1.0 + jnp.exp(-x))


def _silu(x):
    return x * _sigmoid(x)


def _silu_grad(x):
    s = _sigmoid(x)
    return s * (1.0 + x * (1.0 - s))


def _colsum(x):
    return jnp.sum(x, axis=0, keepdims=True)


def _rms(x):
    return lax.rsqrt(jnp.mean(x * x, axis=-1, keepdims=True) + EPS)


def _rms_bwd(xn, r, dxn):
    return r * (dxn - xn * jnp.mean(dxn * xn, axis=-1, keepdims=True))


def _adamw(w, g, m, v):
    m = ADAM_B1 * m + (1.0 - ADAM_B1) * g
    v = ADAM_B2 * v + (1.0 - ADAM_B2) * jnp.square(g)
    m_hat = m / (1.0 - ADAM_B1 ** ADAM_STEP)
    v_hat = v / (1.0 - ADAM_B2 ** ADAM_STEP)
    delta = -ADAM_LR * (m_hat / (jnp.sqrt(v_hat) + ADAM_EPS) + ADAM_WD * w)
    return delta, m, v


def _adamw_call(w, g, m, v, name):
    r, c = w.shape
    if r % 256 == 0:
        tr, tc = 256, c
    elif r * c <= 256 * 1024 or c % (2 * LANES):
        tr, tc = r, c
    else:
        tr, tc = r, 2 * LANES

    def body(w_ref, g_ref, m_ref, v_ref, d_out, m_out, v_out):
        d_out[...], m_out[...], v_out[...] = _adamw(w_ref[...], g_ref[...], m_ref[...], v_ref[...])

    spec = pl.BlockSpec((tr, tc), lambda i, j: (i, j))
    return pl.pallas_call(
        body, name=name, grid=(r // tr, c // tc), in_specs=[spec] * 4, out_specs=[spec] * 3,
        out_shape=[jax.ShapeDtypeStruct((r, c), F32)] * 3, compiler_params=_cparams("parallel", "parallel"),
    )(w, g, m, v)


def _adamw_from_parts(w, parts, m, v, name):
    r, c = w.shape
    tc = 2 * LANES
    assert c % tc == 0 and parts.shape[1:] == (r, c)

    def body(w_ref, p_ref, m_ref, v_ref, g_out, d_out, m_out, v_out):
        g = p_ref[0].astype(F32)
        for s in range(1, parts.shape[0]):
            g = g + p_ref[s].astype(F32)
        g_out[...] = g
        d_out[...], m_out[...], v_out[...] = _adamw(w_ref[...], g, m_ref[...], v_ref[...])

    spec = pl.BlockSpec((r, tc), lambda j: (0, j))
    return pl.pallas_call(
        body, name=name, grid=(c // tc,),
        in_specs=[spec, pl.BlockSpec((parts.shape[0], r, tc), lambda j: (0, 0, j)), spec, spec],
        out_specs=[spec] * 4, out_shape=[jax.ShapeDtypeStruct((r, c), F32)] * 4, compiler_params=_cparams("parallel"),
    )(w, parts, m, v)


def _uq_to_kernel_layout(w):
    lead = w.shape[:-1]
    t = w.reshape(lead + (HEADS, QK_DIM))
    return jnp.concatenate([t[..., :NOPE].reshape(lead + (HEADS * NOPE,)),
                            t[..., NOPE:NOPE + ROPE // 2].reshape(lead + (LANES,)),
                            t[..., NOPE + ROPE // 2:].reshape(lead + (LANES,))], axis=-1)


def _uq_from_kernel_layout(w):
    lead = w.shape[:-1]
    nope = w[..., :HEADS * NOPE].reshape(lead + (HEADS, NOPE))
    r1 = w[..., HEADS * NOPE:HEADS * NOPE + LANES].reshape(lead + (HEADS, ROPE // 2))
    r2 = w[..., HEADS * NOPE + LANES:].reshape(lead + (HEADS, ROPE // 2))
    return jnp.concatenate([nope, r1, r2], axis=-1).reshape(lead + (HEADS * QK_DIM,))


def _ukv_to_kernel_layout(w):
    lead = w.shape[:-1]
    t = w.reshape(lead + (HEADS, NOPE + HD))
    return jnp.concatenate([t[..., :NOPE].reshape(lead + (HEADS * NOPE,)),
                            t[..., NOPE:].reshape(lead + (HEADS * HD,))], axis=-1)


def _ukv_from_kernel_layout(w):
    lead = w.shape[:-1]
    kn = w[..., :HEADS * NOPE].reshape(lead + (HEADS, NOPE))
    vv = w[..., HEADS * NOPE:].reshape(lead + (HEADS, HD))
    return jnp.concatenate([kn, vv], axis=-1).reshape(lead + (HEADS * (NOPE + HD),))


def _w_in_t_to_kernel_layout(wt):
    sb = wt[0:2048]
    c_q = wt[2048:2432]
    c_kv = wt[2432:2688]
    k_rot = wt[2688:2720]
    mla_z = wt[2720:3232]
    gates = wt[3232:5280]
    zeros = jnp.zeros((LANES, wt.shape[1]), wt.dtype)
    k1 = jnp.tile(k_rot[:ROPE // 2], (HEADS, 1))
    k2 = jnp.tile(k_rot[ROPE // 2:], (HEADS, 1))
    return jnp.concatenate([gates, sb, mla_z, c_q, zeros, c_kv, k1, k2], axis=0)


def _w_in_t_from_kernel_layout(gt, g_rot):
    return jnp.concatenate([gt[O_SBQ:O_SBQ + 2048], gt[O_CQ:O_CQ + Q_RANK], gt[O_CKV:O_CKV + KV_RANK],
                            g_rot.astype(gt.dtype), gt[O_MLAZ:O_MLAZ + MLA_W], gt[O_GA:O_GA + 2 * D]], axis=0)


def kernel(x, c, positions, w_ada, b_ada, norm_gain, w_in, q_norm_gain, w_uq, kv_norm_gain, w_ukv, w_branch_a, w_branch_b, w_out, final_norm_gain, loss_target, m_w_ada, m_b_ada, m_norm_gain, m_w_in, m_q_norm_gain, m_w_uq, m_kv_norm_gain, m_w_ukv, m_w_branch_a, m_w_branch_b, m_w_out, m_final_norm_gain, v_w_ada, v_b_ada, v_norm_gain, v_w_in, v_q_norm_gain, v_w_uq, v_kv_norm_gain, v_w_ukv, v_w_branch_a, v_w_branch_b, v_w_out, v_final_norm_gain):
    s_len = x.shape[1]
    me = _dev_index(*_mesh_pos())
    x2d = x[0]
    tgt = loss_target[0]

    w_in_t = w_in[0].T.astype(BF16)
    big = [w_uq[0], w_ukv[0], w_branch_a[0], w_branch_b[0], w_out[0]]
    big_sizes = [int(w.size) for w in big]
    packed = jnp.concatenate([w.astype(BF16).reshape(-1, LANES) for w in big], axis=0)
    g_in_t, c_all = _all_gather_parts([w_in_t, c.reshape(8, LANES)], "gather_w_in")
    c_all = c_all.reshape(N_DEV, D)
    w_in_kt = _w_in_t_to_kernel_layout(g_in_t.reshape(N_DEV * w_in_t.shape[0], D))

    mod_cols = _mm(c_all, w_ada[0], name="ada_mod")
    mod_all = _all_gather(mod_cols, "gather_mod")
    mod = lax.dynamic_index_in_dim(mod_all, me, axis=1, keepdims=False).reshape(1, 3 * D)
    mod_shift, mod_scale, mod_gate = mod[:, :D], mod[:, D:2 * D], mod[:, 2 * D:]
    b_shift, b_scale, b_gate = b_ada[:, :D], b_ada[:, D:2 * D], b_ada[:, 2 * D:]
    g1 = norm_gain
    gq, gkv = q_norm_gain, kv_norm_gain
    gf = final_norm_gain.reshape(1, D)

    def f_h(x_, g1_, ms, bs, msc, bsc):
        xn = x_ * _rms(x_)
        return (xn * g1_ * (1.0 + (msc + bsc)) + (ms + bs),), ()

    (h,) = _rowwise(f_h, [x2d], [g1, mod_shift, b_shift, mod_scale, b_scale], [(D, BF16)], name="ada_norm")
    proj = _mm(h, w_in_kt, tb=True, name="proj_in", tiles=(min(s_len, 1024), IN_PAD // 2, D))
    proj_at = dict(ga=(O_GA, D), gb=(O_GB, D), sbz=(O_SBZ, SB_W), mlaz=(O_MLAZ, MLA_W),
                   cq=(O_CQ, Q_RANK), ckv=(O_CKV, KV_RANK), krot=(O_KROT, LANES))

    def win(key, shift=0):
        offset, width = proj_at[key]
        assert offset % width == 0
        return proj, offset // width + shift, width

    (o_a, sb_tails), (gathered,) = _sb_forward(
        proj, s_len, _Exchange([jnp.broadcast_to(packed[None], (N_DEV,) + packed.shape)]))
    offs = [0]
    for n in big_sizes:
        offs.append(offs[-1] + n // LANES)

    def unpack(t, shape):
        return gathered[:, offs[t]:offs[t + 1], :].reshape((N_DEV,) + shape)

    def cols(t, shape):
        return unpack(t, shape).transpose(1, 0, 2).reshape(shape[0], N_DEV * shape[1])

    w_uq_k = _uq_to_kernel_layout(cols(0, big[0].shape))
    w_ukv_k = _ukv_to_kernel_layout(cols(1, big[1].shape))
    w_a_f = cols(2, big[2].shape)
    w_b_f = cols(3, big[3].shape)
    w_out_f = unpack(4, big[4].shape).reshape(D, D)

    def f_lat(cq, ckv, gq_, gkv_):
        return (cq * _rms(cq) * gq_, ckv * _rms(ckv) * gkv_), ()

    cq_n, ckv_n = _rowwise(f_lat, [win("cq"), win("ckv")], [gq, gkv],
                           [(Q_RANK, BF16), (KV_RANK, BF16)], name="latent_norm")
    q_mla = _mm(cq_n, w_uq_k, name="q_up")
    kv = _mm(ckv_n, w_ukv_k, out_dtype=BF16, name="kv_up")

    inv_freq = ROPE_BASE ** (-jnp.arange(0, ROPE, 2, dtype=F32) / ROPE)
    inv_freq_t = jnp.tile(inv_freq, HEADS).reshape(1, LANES)
    pos_col = positions.reshape(s_len, 1).astype(F32)

    pairs = HEADS // 2

    def f_rope(pos, qn, q1, q2, kn, k1, k2, freq):
        ang = pos * freq
        cs, sn = jnp.cos(ang), jnp.sin(ang)
        q1r, q2r = q1 * cs - q2 * sn, q1 * sn + q2 * cs
        k1r, k2r = k1 * cs - k2 * sn, k1 * sn + k2 * cs
        lane = lax.broadcasted_iota(jnp.int32, q1.shape, 1)
        first, second = lane < ROPE, (lane >= ROPE) & (lane < 2 * ROPE)
        k_rot = jnp.where(first, k1r, jnp.where(second, k2r, 0.0))
        q_parts, k_parts = [], []
        for p in range(pairs):
            q_rot = jnp.where(first, pltpu.roll(q1r, (LANES - ROPE * p) % LANES, 1),
                              jnp.where(second, pltpu.roll(q2r, (LANES + ROPE - ROPE * p) % LANES, 1), 0.0))
            q_parts += [qn[:, LANES * p:LANES * (p + 1)], q_rot]
            k_parts += [kn[:, LANES * p:LANES * (p + 1)], k_rot]
        return (jnp.concatenate(q_parts, axis=1), jnp.concatenate(k_parts, axis=1), cs, sn), ()

    q_cat, k_cat, cos_t, sin_t = _rowwise(
        f_rope, [pos_col, (q_mla, 0, MLA_W), (q_mla, 4, LANES), (q_mla, 5, LANES), (kv, 0, MLA_W),
                 win("krot"), win("krot", 1)], [inv_freq_t],
        [(2 * MLA_W, BF16), (2 * MLA_W, BF16), (LANES, F32), (LANES, F32)], name="rope")

    o_b, lse = _mla_forward(q_cat, k_cat, kv, s_len)

    def f_gate(oa, za, ob, zb):
        return (oa * _silu(za), ob * _silu(zb)), ()

    ya_in, yb_in = _rowwise(f_gate, [o_a, win("sbz"), o_b, win("mlaz")], [],
                            [(SB_W, BF16), (MLA_W, BF16)], name="branch_gate")
    y_a = _mm(ya_in, w_a_f, out_dtype=BF16, name="branch_a")

    def f_merge(yb, ga, gb, ya):
        return (yb, _sigmoid(ga) * ya + _sigmoid(gb) * yb), ()

    y_b, merged = _mm(yb_in, w_b_f, name="branch_b_merge", tiles=(min(s_len, 512), D, MLA_W),
                      epilogue=(f_merge, [win("ga"), win("gb"), y_a], [], [(D, BF16), (D, BF16)], []))

    def f_loss(out_, x_, t_, mg, bg, gf_):
        gate = mg + bg
        x2 = x_ + gate * out_
        r2 = _rms(x2)
        xn2 = x2 * r2
        err = xn2 * gf_ - t_
        loss = jnp.full((1, LANES), 0.5 / D, F32) * jnp.sum(err * err)
        dy = err * (1.0 / D)
        dx2 = _rms_bwd(xn2, r2, dy * gf_)
        return (dx2, dx2 * gate), (loss, _colsum(dy * xn2), _colsum(dx2 * out_))

    dx2, d_out, loss_part, d_gf, d_gate = _mm(
        merged, w_out_f, name="out_proj_loss", tiles=(min(s_len, 512), D, D),
        epilogue=(f_loss, [x2d, tgt], [mod_gate, b_gate, gf], [(D, F32), (D, BF16)], [LANES, D, D]))

    dw_out = _mm(merged, d_out, ta=True, name="dw_out")

    def f_dmerge(dm, ga, gb, ya, yb):
        sa, sb = _sigmoid(ga), _sigmoid(gb)
        return (dm * sa, dm * sb, dm * ya * sa * (1.0 - sa), dm * yb * sb * (1.0 - sb)), ()

    d_ya, d_yb, d_ga, d_gb = _mm(
        d_out, w_out_f, tb=True, name="d_merge", tiles=(min(s_len, 256), D, D),
        epilogue=(f_dmerge, [win("ga"), win("gb"), y_a, y_b], [], [(D, BF16)] * 4, []))
    dw_a = _mm(ya_in, d_ya, ta=True, name="dw_branch_a")
    dw_b = _mm(yb_in, d_yb, ta=True, name="dw_branch_b")

    def f_dgate(d_in, o_, z_):
        return (d_in * _silu(z_), d_in * o_ * _silu_grad(z_)), ()

    d_oa, d_sbz = _mm(d_ya, w_a_f, tb=True, name="d_branch_a",
                      epilogue=(f_dgate, [o_a, win("sbz")], [], [(SB_W, BF16), (SB_W, BF16)], []))
    d_ob, d_mlaz = _mm(d_yb, w_b_f, tb=True, name="d_branch_b",
                       epilogue=(f_dgate, [o_b, win("mlaz")], [], [(MLA_W, F32), (MLA_W, BF16)], []))

    dq_cat, dk_cat, dv_b = _mla_backward(q_cat, k_cat, kv, o_b, lse, d_ob, s_len)

    def f_drope(dq, dk, dv_, cs, sn):
        lane = lax.broadcasted_iota(jnp.int32, cs.shape, 1)
        first, second = lane < ROPE, (lane >= ROPE) & (lane < 2 * ROPE)
        dq1 = dq2 = dk1 = dk2 = None
        for p in range(pairs):
            q_rot = dq[:, LANES * (2 * p + 1):LANES * (2 * p + 2)]
            k_rot = dk[:, LANES * (2 * p + 1):LANES * (2 * p + 2)]
            parts = (pltpu.roll(jnp.where(first, q_rot, 0.0), (ROPE * p) % LANES, 1),
                     pltpu.roll(jnp.where(second, q_rot, 0.0), (LANES - ROPE + ROPE * p) % LANES, 1),
                     jnp.where(first, k_rot, 0.0), jnp.where(second, k_rot, 0.0))
            if p == 0:
                dq1, dq2, dk1, dk2 = parts
            else:
                dq1, dq2, dk1, dk2 = dq1 + parts[0], dq2 + parts[1], dk1 + parts[2], dk2 + parts[3]
        dqn_ = [dq[:, 2 * LANES * p:2 * LANES * p + LANES] for p in range(pairs)]
        dkn_ = [dk[:, 2 * LANES * p:2 * LANES * p + LANES] for p in range(pairs)]
        return (jnp.concatenate(dqn_ + [dq1 * cs + dq2 * sn, dq2 * cs - dq1 * sn], axis=1),
                jnp.concatenate(dkn_ + [dv_], axis=1),
                jnp.concatenate([dk1 * cs + dk2 * sn, dk2 * cs - dk1 * sn], axis=1)), ()

    dq_k, dkv_k, d_krot = _rowwise(f_drope, [dq_cat, dk_cat, dv_b, cos_t, sin_t], [],
                                   [(HEADS * QK_DIM, BF16), (2 * MLA_W, BF16), (2 * LANES, BF16)], name="d_rope")
    dw_uq_k = _mm(cq_n, dq_k, ta=True, name="dw_uq")
    dw_ukv_k = _mm(ckv_n, dkv_k, ta=True, name="dw_ukv")

    def f_dlat(d_normed, latent, gain):
        r = _rms(latent)
        normed = latent * r
        return (_rms_bwd(normed, r, d_normed * gain),), (_colsum(d_normed * normed),)

    d_cq, d_gq = _mm(dq_k, w_uq_k, tb=True, name="d_cq_norm",
                     epilogue=(f_dlat, [win("cq")], [gq], [(Q_RANK, BF16)], [Q_RANK]))
    d_ckv, d_gkv = _mm(dkv_k, w_ukv_k, tb=True, name="d_ckv_norm",
                       epilogue=(f_dlat, [win("ckv")], [gkv], [(KV_RANK, BF16)], [KV_RANK]))

    def col_blocks(g):
        kdim, n8 = g.shape
        return g.astype(BF16).reshape(kdim, N_DEV, n8 // N_DEV).transpose(1, 0, 2).reshape(N_DEV, -1, LANES)

    g_blocks = jnp.concatenate([col_blocks(_uq_from_kernel_layout(dw_uq_k)), col_blocks(_ukv_from_kernel_layout(dw_ukv_k)),
                                col_blocks(dw_a), col_blocks(dw_b), dw_out.astype(BF16).reshape(N_DEV, -1, LANES)], axis=1)
    (d_sbq, d_sbk, d_sbv), (g_recv,) = _sb_backward(proj, sb_tails, d_oa, s_len, _Exchange([g_blocks]))

    d_proj = jnp.concatenate([d_ga, d_gb, d_sbq, d_sbk, d_sbv, d_sbz, d_mlaz,
                              d_cq, jnp.zeros((s_len, LANES), BF16), d_ckv, d_krot], axis=1)
    dw_in_kt = _mm(d_proj, h, ta=True, out_dtype=BF16, name="dw_in", tiles=(512, D, s_len))

    def krot_body(t_ref, o_ref):
        half = ROPE // 2
        for part in range(2):
            acc = t_ref[part * LANES:part * LANES + half, :].astype(F32)
            for hh in range(1, HEADS):
                acc = acc + t_ref[part * LANES + hh * half:part * LANES + (hh + 1) * half, :].astype(F32)
            o_ref[part * half:(part + 1) * half, :] = acc

    dw_krot = pl.pallas_call(krot_body, name="dw_krot_sum", out_shape=jax.ShapeDtypeStruct((ROPE, D), F32))(
        dw_in_kt[O_KROT:O_KROT + 2 * LANES])

    g_in_blocks = _w_in_t_from_kernel_layout(dw_in_kt, dw_krot).reshape(N_DEV, -1, D)
    def f_dx(dh_, x_, dx2_, g1_, msc, bsc):
        r = _rms(x_)
        xn = x_ * r
        dn1 = dh_ * (1.0 + (msc + bsc))
        return ((dx2_ + _rms_bwd(xn, r, dn1 * g1_),),
                (_colsum(dh_), _colsum(dh_ * (xn * g1_)), _colsum(dn1 * xn)))

    (grad_x2d, d_shift, d_scale, d_g1), (g_in_recv,) = _mm(
        d_proj, w_in_kt, name="d_h", tiles=(min(s_len, 512), D, 512), exchange=_Exchange([g_in_blocks]),
        epilogue=(f_dx, [x2d, dx2], [g1, mod_scale, b_scale], [(D, F32)], [D, D, D]))

    g_sum = _sum_blocks(g_recv, "sum_grads")
    g_big = [g_sum[offs[t]:offs[t + 1]].reshape(big[t].shape) for t in range(5)]

    small = jnp.concatenate([d_shift, d_scale, d_gate, d_g1, d_gq, d_gkv, d_gf, loss_part], axis=1)
    n_small = small.shape[1]
    pad = (-n_small) % (8 * LANES)
    small = jnp.pad(small, ((0, 0), (0, pad))).reshape(-1, LANES)
    small_all = _all_gather(small, "gather_small")
    small_sum = _sum_blocks(small_all, "sum_small").reshape(1, -1)
    g_b_ada = small_sum[:, :3 * D]
    g_g1 = small_sum[:, 3 * D:4 * D]
    g_gq = small_sum[:, 4 * D:4 * D + Q_RANK]
    g_gkv = small_sum[:, 4 * D + Q_RANK:4 * D + Q_RANK + KV_RANK]
    g_gf = small_sum[:, 4 * D + Q_RANK + KV_RANK:4 * D + Q_RANK + KV_RANK + D]

    dmod_all = small_all.reshape(N_DEV, -1)[:, :3 * D]
    dmod_cols = lax.dynamic_slice_in_dim(dmod_all, me * (3 * D // N_DEV), 3 * D // N_DEV, axis=1)
    g_w_ada = _mm(c_all, dmod_cols, ta=True, name="dw_ada")

    loss = small_sum[0, n_small - LANES]

    names = ["w_ada", "b_ada", "norm_gain", "w_in", "q_norm_gain", "w_uq", "kv_norm_gain", "w_ukv",
             "w_branch_a", "w_branch_b", "w_out", "final_norm_gain"]
    weights = dict(w_ada=w_ada, b_ada=b_ada, norm_gain=norm_gain, w_in=w_in, q_norm_gain=q_norm_gain, w_uq=w_uq,
                   kv_norm_gain=kv_norm_gain, w_ukv=w_ukv, w_branch_a=w_branch_a, w_branch_b=w_branch_b, w_out=w_out,
                   final_norm_gain=final_norm_gain)
    moms = dict(w_ada=m_w_ada, b_ada=m_b_ada, norm_gain=m_norm_gain, w_in=m_w_in, q_norm_gain=m_q_norm_gain,
                w_uq=m_w_uq, kv_norm_gain=m_kv_norm_gain, w_ukv=m_w_ukv, w_branch_a=m_w_branch_a,
                w_branch_b=m_w_branch_b, w_out=m_w_out, final_norm_gain=m_final_norm_gain)
    vels = dict(w_ada=v_w_ada, b_ada=v_b_ada, norm_gain=v_norm_gain, w_in=v_w_in, q_norm_gain=v_q_norm_gain,
                w_uq=v_w_uq, kv_norm_gain=v_kv_norm_gain, w_ukv=v_w_ukv, w_branch_a=v_w_branch_a,
                w_branch_b=v_w_branch_b, w_out=v_w_out, final_norm_gain=v_final_norm_gain)
    grads2d = dict(w_ada=g_w_ada, b_ada=g_b_ada, norm_gain=g_g1, w_in=None, q_norm_gain=g_gq, w_uq=g_big[0],
                   kv_norm_gain=g_gkv, w_ukv=g_big[1], w_branch_a=g_big[2], w_branch_b=g_big[3], w_out=g_big[4],
                   final_norm_gain=g_gf)

    grads, deltas, new_m, new_v = [], [], [], []
    for n in names:
        w = weights[n]
        if n == "w_in":
            back = lambda t: t.T[None]
            g2d, d_, m_, v_ = _adamw_from_parts(w[0].T, g_in_recv, moms[n][0].T, vels[n][0].T, "adamw_" + n)
        else:
            shape2d = grads2d[n].shape
            to2d = lambda t, s=shape2d: t.reshape(s)
            back = lambda t, s=w.shape: t.reshape(s)
            g2d = to2d(grads2d[n])
            d_, m_, v_ = _adamw_call(to2d(w), g2d, to2d(moms[n]), to2d(vels[n]), "adamw_" + n)
        grads.append(back(g2d))
        deltas.append(back(d_))
        new_m.append(back(m_))
        new_v.append(back(v_))

    return (loss, grad_x2d.reshape(x.shape), *grads, *deltas, *new_m, *new_v)
```
